```python
import math
import jax, jax.numpy as jnp
from jax import lax
import numpy as np

D_MODEL = 1024
BATCH = 8
SEQ = 8192
DEPTH = 4

N_EVEN = (DEPTH + 1) // 2
N_ODD = DEPTH // 2

MEM_LEN = 256
XA_HEADS = 4
XA_HEAD_DIM = D_MODEL // XA_HEADS

POOL_W = D_MODEL // 2
POOL_WINDOWS = (2, 4, 8, 16)
N_POOL_GROUPS = len(POOL_WINDOWS)
POOL_GROUP = POOL_W // N_POOL_GROUPS
CONV_W = D_MODEL // 2
CONV_K = 31

MLA_HEADS = 16
QK_NOPE = 64
QK_ROPE = 32
V_HEAD = 64
Q_LORA = 384
KV_LORA = 256
ROPE_THETA = 10000.0
Q_BLOCK = 128
MLA_SCALE = 1.0 / math.sqrt(QK_NOPE + QK_ROPE)

D_FF = 2816
FFN_CONV_K = 3

EPS = 1e-6
NEG = -1e30

kernel_name = "hybrid_pool_conv_mla_memxattn_convffn"


def rmsnorm(x, g):
    xf = x.astype(jnp.float32)
    y = xf * lax.rsqrt(jnp.mean(xf * xf, axis=-1, keepdims=True) + EPS)
    return (y * g.astype(jnp.float32)).astype(x.dtype)


def layernorm(x, g, b):
    xf = x.astype(jnp.float32)
    mu = jnp.mean(xf, axis=-1, keepdims=True)
    xc = xf - mu
    y = xc * lax.rsqrt(jnp.mean(xc * xc, axis=-1, keepdims=True) + EPS)
    return (y * g.astype(jnp.float32) + b.astype(jnp.float32)).astype(x.dtype)


def causal_dwconv(u, w):
    k = w.shape[0]
    return lax.conv_general_dilated(
        u, w[:, None, :], window_strides=(1,), padding=[(k - 1, 0)],
        dimension_numbers=("NWC", "WIO", "NWC"), feature_group_count=u.shape[-1])


def window_mean_minus_self(u, w):
    t = u.shape[1]
    uf = u.astype(jnp.float32)
    cs = jnp.cumsum(uf, axis=1)
    cs_lag = jnp.pad(cs, ((0, 0), (w, 0), (0, 0)))[:, :t]
    cnt = jnp.minimum(jnp.arange(t) + 1, w).astype(jnp.float32)
    return ((cs - cs_lag) / cnt[None, :, None] - uf).astype(u.dtype)


def rope_tables(positions):
    inv = 1.0 / (ROPE_THETA ** (jnp.arange(0, QK_ROPE, 2, dtype=jnp.float32) / QK_ROPE))
    ang = positions.astype(jnp.float32)[..., None] * inv
    return jnp.cos(ang), jnp.sin(ang)


def apply_rope(x, cos, sin):
    half = x.shape[-1] // 2
    c = cos.astype(x.dtype)
    s = sin.astype(x.dtype)
    x1, x2 = x[..., :half], x[..., half:]
    return jnp.concatenate([x1 * c - x2 * s, x1 * s + x2 * c], axis=-1)


def pool_conv_mixer(h, w_in, pool_w, pool_scale, dw_w, dw_b, ln_g, ln_b, w_out):
    b, t, _ = h.shape
    z = h @ w_in
    u, glu_a, glu_b = jnp.split(z, [POOL_W, POOL_W + CONV_W], axis=-1)
    ug = u.reshape(b, t, N_POOL_GROUPS, POOL_GROUP)
    pooled = jnp.stack([window_mean_minus_self(ug[:, :, i], w)
                        for i, w in enumerate(POOL_WINDOWS)], axis=2)
    ya = jnp.einsum("btgc,gcd->btgd", pooled, pool_w).reshape(b, t, POOL_W) * pool_scale
    gl = glu_a * jax.nn.sigmoid(glu_b)
    cv = causal_dwconv(gl, dw_w) + dw_b
    yb = jax.nn.silu(layernorm(cv, ln_g, ln_b))
    return jnp.concatenate([ya, yb], axis=-1) @ w_out


def mla_attention(h, cos, sin, w_dq_dkv, q_norm_g, w_uq, kv_norm_g, w_ukv, w_o):
    b, t, _ = h.shape
    c = h @ w_dq_dkv
    cq, ckv, k_pe = jnp.split(c, [Q_LORA, Q_LORA + KV_LORA], axis=-1)
    q = (rmsnorm(cq, q_norm_g) @ w_uq).reshape(b, t, MLA_HEADS, QK_NOPE + QK_ROPE)
    q_nope = q[..., :QK_NOPE]
    q_pe = apply_rope(q[..., QK_NOPE:], cos[:, :, None, :], sin[:, :, None, :])
    kv = (rmsnorm(ckv, kv_norm_g) @ w_ukv).reshape(b, t, MLA_HEADS, QK_NOPE + V_HEAD)
    k_nope, v = kv[..., :QK_NOPE], kv[..., QK_NOPE:]
    k_pe = apply_rope(k_pe, cos, sin)
    nb = t // Q_BLOCK
    kpos = jnp.arange(t)

    def block(args):
        qn, qp, i = args
        s = (jnp.einsum("bqhd,bkhd->bhqk", qn, k_nope)
             + jnp.einsum("bqhr,bkr->bhqk", qp, k_pe)).astype(jnp.float32) * MLA_SCALE
        qpos = i * Q_BLOCK + jnp.arange(Q_BLOCK)
        s = jnp.where(kpos[None, :] <= qpos[:, None], s, NEG)
        p = jax.nn.softmax(s, axis=-1).astype(v.dtype)
        return jnp.einsum("bhqk,bkhd->bqhd", p, v)

    qn_b = q_nope.reshape(b, nb, Q_BLOCK, MLA_HEADS, QK_NOPE).transpose(1, 0, 2, 3, 4)
    qp_b = q_pe.reshape(b, nb, Q_BLOCK, MLA_HEADS, QK_ROPE).transpose(1, 0, 2, 3, 4)
    o = lax.map(block, (qn_b, qp_b, jnp.arange(nb)))
    o = o.transpose(1, 0, 2, 3, 4).reshape(b, t, MLA_HEADS * V_HEAD)
    return o @ w_o


def memory_cross_attention(h, m, wq, wkv, wo):
    b, t, _ = h.shape
    q = (h @ wq).reshape(b, t, XA_HEADS, XA_HEAD_DIM)
    k, v = jnp.split(m @ wkv, 2, axis=-1)
    k = k.reshape(b, MEM_LEN, XA_HEADS, XA_HEAD_DIM)
    v = v.reshape(b, MEM_LEN, XA_HEADS, XA_HEAD_DIM)
    s = jnp.einsum("bthd,bmhd->bhtm", q, k).astype(jnp.float32) * (XA_HEAD_DIM ** -0.5)
    p = jax.nn.softmax(s, axis=-1).astype(v.dtype)
    o = jnp.einsum("bhtm,bmhd->bthd", p, v).reshape(b, t, D_MODEL)
    return o @ wo


def conv_ffn(h, w_up, conv_w, conv_b, w_down):
    a, g = jnp.split(h @ w_up, 2, axis=-1)
    g = causal_dwconv(g, conv_w) + conv_b
    return (jax.nn.silu(g) * a) @ w_down


def _fwd_setup_inputs(seed: int = 0) -> dict:
    key = jax.random.key(seed)
    ks = iter(jax.random.split(key, 40))
    f32 = jnp.float32

    def dense(shape, fan_in, scale=1.0):
        return jax.random.normal(next(ks), shape, f32) * (scale * fan_in ** -0.5)

    def gain(shape):
        return 1.0 + 0.02 * jax.random.normal(next(ks), shape, f32)

    def bias(shape):
        return 0.01 * jax.random.normal(next(ks), shape, f32)

    out_scale = 0.5
    x = jax.random.normal(next(ks), (BATCH, SEQ, D_MODEL), f32)
    mem = jax.random.normal(next(ks), (BATCH, MEM_LEN, D_MODEL), f32)
    offsets = jax.random.randint(next(ks), (BATCH, 1), 0, 4096, dtype=jnp.int32)
    positions = offsets + jnp.arange(SEQ, dtype=jnp.int32)[None, :]
    return {
        "x": x,
        "mem": mem,
        "positions": positions,
        "norm_mix_g": gain((DEPTH, D_MODEL)),
        "norm_xa_g": gain((DEPTH, D_MODEL)),
        "norm_mem_g": gain((DEPTH, D_MODEL)),
        "xa_wq": dense((DEPTH, D_MODEL, D_MODEL), D_MODEL),
        "xa_wkv": dense((DEPTH, D_MODEL, 2 * D_MODEL), D_MODEL),
        "xa_wo": dense((DEPTH, D_MODEL, D_MODEL), D_MODEL, out_scale),
        "norm_ffn_g": gain((DEPTH, D_MODEL)),
        "ffn_w_up": dense((DEPTH, D_MODEL, 2 * D_FF), D_MODEL),
        "ffn_conv_w": dense((DEPTH, FFN_CONV_K, D_FF), FFN_CONV_K),
        "ffn_conv_b": bias((DEPTH, D_FF)),
        "ffn_w_down": dense((DEPTH, D_FF, D_MODEL), D_FF, out_scale),
        "pc_w_in": dense((N_EVEN, D_MODEL, POOL_W + 2 * CONV_W), D_MODEL),
        "pool_w": dense((N_EVEN, N_POOL_GROUPS, POOL_GROUP, POOL_GROUP), POOL_GROUP),
        "pool_scale": gain((N_EVEN, POOL_W)),
        "conv_dw_w": dense((N_EVEN, CONV_K, CONV_W), CONV_K),
        "conv_dw_b": bias((N_EVEN, CONV_W)),
        "conv_ln_g": gain((N_EVEN, CONV_W)),
        "conv_ln_b": bias((N_EVEN, CONV_W)),
        "pc_w_out": dense((N_EVEN, POOL_W + CONV_W, D_MODEL), POOL_W + CONV_W, out_scale),
        "mla_w_dq_dkv": dense((N_ODD, D_MODEL, Q_LORA + KV_LORA + QK_ROPE), D_MODEL),
        "mla_q_norm_g": gain((N_ODD, Q_LORA)),
        "mla_w_uq": dense((N_ODD, Q_LORA, MLA_HEADS * (QK_NOPE + QK_ROPE)), Q_LORA),
        "mla_kv_norm_g": gain((N_ODD, KV_LORA)),
        "mla_w_ukv": dense((N_ODD, KV_LORA, MLA_HEADS * (QK_NOPE + V_HEAD)), KV_LORA),
        "mla_w_o": dense((N_ODD, MLA_HEADS * V_HEAD, D_MODEL), MLA_HEADS * V_HEAD, out_scale),
        "final_norm_g": gain((D_MODEL,)),
    }


def _fwd_reference(x, mem, positions, norm_mix_g, norm_xa_g, norm_mem_g, xa_wq, xa_wkv, xa_wo,
              norm_ffn_g, ffn_w_up, ffn_conv_w, ffn_conv_b, ffn_w_down,
              pc_w_in, pool_w, pool_scale, conv_dw_w, conv_dw_b, conv_ln_g, conv_ln_b, pc_w_out,
              mla_w_dq_dkv, mla_q_norm_g, mla_w_uq, mla_kv_norm_g, mla_w_ukv, mla_w_o,
              final_norm_g):
    cos, sin = rope_tables(positions)
    for l in range(DEPTH):
        h = rmsnorm(x, norm_mix_g[l])
        if l % 2 == 0:
            e = l // 2
            x = x + pool_conv_mixer(h, pc_w_in[e], pool_w[e], pool_scale[e], conv_dw_w[e],
                                    conv_dw_b[e], conv_ln_g[e], conv_ln_b[e], pc_w_out[e])
        else:
            o = l // 2
            x = x + mla_attention(h, cos, sin, mla_w_dq_dkv[o], mla_q_norm_g[o], mla_w_uq[o],
                                  mla_kv_norm_g[o], mla_w_ukv[o], mla_w_o[o])
        x = x + memory_cross_attention(rmsnorm(x, norm_xa_g[l]), rmsnorm(mem, norm_mem_g[l]),
                                       xa_wq[l], xa_wkv[l], xa_wo[l])
        x = x + conv_ffn(rmsnorm(x, norm_ffn_g[l]), ffn_w_up[l], ffn_conv_w[l], ffn_conv_b[l],
                         ffn_w_down[l])
    return rmsnorm(x, final_norm_g)


import jax as _jax
import jax.numpy as _jnp

TWIN_FORMAT = 'train_step'
FWD_PARAMS = ['x', 'mem', 'positions', 'norm_mix_g', 'norm_xa_g', 'norm_mem_g', 'xa_wq', 'xa_wkv', 'xa_wo', 'norm_ffn_g', 'ffn_w_up', 'ffn_conv_w', 'ffn_conv_b', 'ffn_w_down', 'pc_w_in', 'pool_w', 'pool_scale', 'conv_dw_w', 'conv_dw_b', 'conv_ln_g', 'conv_ln_b', 'pc_w_out', 'mla_w_dq_dkv', 'mla_q_norm_g', 'mla_w_uq', 'mla_kv_norm_g', 'mla_w_ukv', 'mla_w_o', 'final_norm_g']
TWIN_WEIGHTS = ['norm_mix_g', 'norm_xa_g', 'norm_mem_g', 'xa_wq', 'xa_wkv', 'xa_wo', 'norm_ffn_g', 'ffn_w_up', 'ffn_conv_w', 'ffn_conv_b', 'ffn_w_down', 'pc_w_in', 'pool_w', 'pool_scale', 'conv_dw_w', 'conv_dw_b', 'conv_ln_g', 'conv_ln_b', 'pc_w_out', 'mla_w_dq_dkv', 'mla_q_norm_g', 'mla_w_uq', 'mla_kv_norm_g', 'mla_w_ukv', 'mla_w_o', 'final_norm_g']
TWIN_DIFF_INPUT = 'x'
TWIN_INPUTS = ['x', 'mem', 'positions', 'norm_mix_g', 'norm_xa_g', 'norm_mem_g', 'xa_wq', 'xa_wkv', 'xa_wo', 'norm_ffn_g', 'ffn_w_up', 'ffn_conv_w', 'ffn_conv_b', 'ffn_w_down', 'pc_w_in', 'pool_w', 'pool_scale', 'conv_dw_w', 'conv_dw_b', 'conv_ln_g', 'conv_ln_b', 'pc_w_out', 'mla_w_dq_dkv', 'mla_q_norm_g', 'mla_w_uq', 'mla_kv_norm_g', 'mla_w_ukv', 'mla_w_o', 'final_norm_g', 'loss_target', 'm_norm_mix_g', 'm_norm_xa_g', 'm_norm_mem_g', 'm_xa_wq', 'm_xa_wkv', 'm_xa_wo', 'm_norm_ffn_g', 'm_ffn_w_up', 'm_ffn_conv_w', 'm_ffn_conv_b', 'm_ffn_w_down', 'm_pc_w_in', 'm_pool_w', 'm_pool_scale', 'm_conv_dw_w', 'm_conv_dw_b', 'm_conv_ln_g', 'm_conv_ln_b', 'm_pc_w_out', 'm_mla_w_dq_dkv', 'm_mla_q_norm_g', 'm_mla_w_uq', 'm_mla_kv_norm_g', 'm_mla_w_ukv', 'm_mla_w_o', 'm_final_norm_g', 'v_norm_mix_g', 'v_norm_xa_g', 'v_norm_mem_g', 'v_xa_wq', 'v_xa_wkv', 'v_xa_wo', 'v_norm_ffn_g', 'v_ffn_w_up', 'v_ffn_conv_w', 'v_ffn_conv_b', 'v_ffn_w_down', 'v_pc_w_in', 'v_pool_w', 'v_pool_scale', 'v_conv_dw_w', 'v_conv_dw_b', 'v_conv_ln_g', 'v_conv_ln_b', 'v_pc_w_out', 'v_mla_w_dq_dkv', 'v_mla_q_norm_g', 'v_mla_w_uq', 'v_mla_kv_norm_g', 'v_mla_w_ukv', 'v_mla_w_o', 'v_final_norm_g']
TWIN_OUTPUTS = ['loss', 'grad_x', 'grad_norm_mix_g', 'grad_norm_xa_g', 'grad_norm_mem_g', 'grad_xa_wq', 'grad_xa_wkv', 'grad_xa_wo', 'grad_norm_ffn_g', 'grad_ffn_w_up', 'grad_ffn_conv_w', 'grad_ffn_conv_b', 'grad_ffn_w_down', 'grad_pc_w_in', 'grad_pool_w', 'grad_pool_scale', 'grad_conv_dw_w', 'grad_conv_dw_b', 'grad_conv_ln_g', 'grad_conv_ln_b', 'grad_pc_w_out', 'grad_mla_w_dq_dkv', 'grad_mla_q_norm_g', 'grad_mla_w_uq', 'grad_mla_kv_norm_g', 'grad_mla_w_ukv', 'grad_mla_w_o', 'grad_final_norm_g', 'delta_norm_mix_g', 'delta_norm_xa_g', 'delta_norm_mem_g', 'delta_xa_wq', 'delta_xa_wkv', 'delta_xa_wo', 'delta_norm_ffn_g', 'delta_ffn_w_up', 'delta_ffn_conv_w', 'delta_ffn_conv_b', 'delta_ffn_w_down', 'delta_pc_w_in', 'delta_pool_w', 'delta_pool_scale', 'delta_conv_dw_w', 'delta_conv_dw_b', 'delta_conv_ln_g', 'delta_conv_ln_b', 'delta_pc_w_out', 'delta_mla_w_dq_dkv', 'delta_mla_q_norm_g', 'delta_mla_w_uq', 'delta_mla_kv_norm_g', 'delta_mla_w_ukv', 'delta_mla_w_o', 'delta_final_norm_g', 'new_m_norm_mix_g', 'new_m_norm_xa_g', 'new_m_norm_mem_g', 'new_m_xa_wq', 'new_m_xa_wkv', 'new_m_xa_wo', 'new_m_norm_ffn_g', 'new_m_ffn_w_up', 'new_m_ffn_conv_w', 'new_m_ffn_conv_b', 'new_m_ffn_w_down', 'new_m_pc_w_in', 'new_m_pool_w', 'new_m_pool_scale', 'new_m_conv_dw_w', 'new_m_conv_dw_b', 'new_m_conv_ln_g', 'new_m_conv_ln_b', 'new_m_pc_w_out', 'new_m_mla_w_dq_dkv', 'new_m_mla_q_norm_g', 'new_m_mla_w_uq', 'new_m_mla_kv_norm_g', 'new_m_mla_w_ukv', 'new_m_mla_w_o', 'new_m_final_norm_g', 'new_v_norm_mix_g', 'new_v_norm_xa_g', 'new_v_norm_mem_g', 'new_v_xa_wq', 'new_v_xa_wkv', 'new_v_xa_wo', 'new_v_norm_ffn_g', 'new_v_ffn_w_up', 'new_v_ffn_conv_w', 'new_v_ffn_conv_b', 'new_v_ffn_w_down', 'new_v_pc_w_in', 'new_v_pool_w', 'new_v_pool_scale', 'new_v_conv_dw_w', 'new_v_conv_dw_b', 'new_v_conv_ln_g', 'new_v_conv_ln_b', 'new_v_pc_w_out', 'new_v_mla_w_dq_dkv', 'new_v_mla_q_norm_g', 'new_v_mla_w_uq', 'new_v_mla_kv_norm_g', 'new_v_mla_w_ukv', 'new_v_mla_w_o', 'new_v_final_norm_g']
TWIN_LEAF_KINDS = {'loss': 'loss', 'grad_x': 'grad_x', 'grad_norm_mix_g': 'grad_w', 'grad_norm_xa_g': 'grad_w', 'grad_norm_mem_g': 'grad_w', 'grad_xa_wq': 'grad_w', 'grad_xa_wkv': 'grad_w', 'grad_xa_wo': 'grad_w', 'grad_norm_ffn_g': 'grad_w', 'grad_ffn_w_up': 'grad_w', 'grad_ffn_conv_w': 'grad_w', 'grad_ffn_conv_b': 'grad_w', 'grad_ffn_w_down': 'grad_w', 'grad_pc_w_in': 'grad_w', 'grad_pool_w': 'grad_w', 'grad_pool_scale': 'grad_w', 'grad_conv_dw_w': 'grad_w', 'grad_conv_dw_b': 'grad_w', 'grad_conv_ln_g': 'grad_w', 'grad_conv_ln_b': 'grad_w', 'grad_pc_w_out': 'grad_w', 'grad_mla_w_dq_dkv': 'grad_w', 'grad_mla_q_norm_g': 'grad_w', 'grad_mla_w_uq': 'grad_w', 'grad_mla_kv_norm_g': 'grad_w', 'grad_mla_w_ukv': 'grad_w', 'grad_mla_w_o': 'grad_w', 'grad_final_norm_g': 'grad_w', 'delta_norm_mix_g': 'delta_w', 'delta_norm_xa_g': 'delta_w', 'delta_norm_mem_g': 'delta_w', 'delta_xa_wq': 'delta_w', 'delta_xa_wkv': 'delta_w', 'delta_xa_wo': 'delta_w', 'delta_norm_ffn_g': 'delta_w', 'delta_ffn_w_up': 'delta_w', 'delta_ffn_conv_w': 'delta_w', 'delta_ffn_conv_b': 'delta_w', 'delta_ffn_w_down': 'delta_w', 'delta_pc_w_in': 'delta_w', 'delta_pool_w': 'delta_w', 'delta_pool_scale': 'delta_w', 'delta_conv_dw_w': 'delta_w', 'delta_conv_dw_b': 'delta_w', 'delta_conv_ln_g': 'delta_w', 'delta_conv_ln_b': 'delta_w', 'delta_pc_w_out': 'delta_w', 'delta_mla_w_dq_dkv': 'delta_w', 'delta_mla_q_norm_g': 'delta_w', 'delta_mla_w_uq': 'delta_w', 'delta_mla_kv_norm_g': 'delta_w', 'delta_mla_w_ukv': 'delta_w', 'delta_mla_w_o': 'delta_w', 'delta_final_norm_g': 'delta_w', 'new_m_norm_mix_g': 'new_m', 'new_m_norm_xa_g': 'new_m', 'new_m_norm_mem_g': 'new_m', 'new_m_xa_wq': 'new_m', 'new_m_xa_wkv': 'new_m', 'new_m_xa_wo': 'new_m', 'new_m_norm_ffn_g': 'new_m', 'new_m_ffn_w_up': 'new_m', 'new_m_ffn_conv_w': 'new_m', 'new_m_ffn_conv_b': 'new_m', 'new_m_ffn_w_down': 'new_m', 'new_m_pc_w_in': 'new_m', 'new_m_pool_w': 'new_m', 'new_m_pool_scale': 'new_m', 'new_m_conv_dw_w': 'new_m', 'new_m_conv_dw_b': 'new_m', 'new_m_conv_ln_g': 'new_m', 'new_m_conv_ln_b': 'new_m', 'new_m_pc_w_out': 'new_m', 'new_m_mla_w_dq_dkv': 'new_m', 'new_m_mla_q_norm_g': 'new_m', 'new_m_mla_w_uq': 'new_m', 'new_m_mla_kv_norm_g': 'new_m', 'new_m_mla_w_ukv': 'new_m', 'new_m_mla_w_o': 'new_m', 'new_m_final_norm_g': 'new_m', 'new_v_norm_mix_g': 'new_v', 'new_v_norm_xa_g': 'new_v', 'new_v_norm_mem_g': 'new_v', 'new_v_xa_wq': 'new_v', 'new_v_xa_wkv': 'new_v', 'new_v_xa_wo': 'new_v', 'new_v_norm_ffn_g': 'new_v', 'new_v_ffn_w_up': 'new_v', 'new_v_ffn_conv_w': 'new_v', 'new_v_ffn_conv_b': 'new_v', 'new_v_ffn_w_down': 'new_v', 'new_v_pc_w_in': 'new_v', 'new_v_pool_w': 'new_v', 'new_v_pool_scale': 'new_v', 'new_v_conv_dw_w': 'new_v', 'new_v_conv_dw_b': 'new_v', 'new_v_conv_ln_g': 'new_v', 'new_v_conv_ln_b': 'new_v', 'new_v_pc_w_out': 'new_v', 'new_v_mla_w_dq_dkv': 'new_v', 'new_v_mla_q_norm_g': 'new_v', 'new_v_mla_w_uq': 'new_v', 'new_v_mla_kv_norm_g': 'new_v', 'new_v_mla_w_ukv': 'new_v', 'new_v_mla_w_o': 'new_v', 'new_v_final_norm_g': 'new_v'}


def _forward(args):
    return _fwd_reference(*[args[k] for k in FWD_PARAMS])


def _output_shape():
    out = _jax.eval_shape(lambda: _forward(_fwd_setup_inputs(0)))
    return out.shape, out.dtype

N_MICROBATCH = 1
ADAM_LR = 0.001
ADAM_B1 = 0.9
ADAM_B2 = 0.999
ADAM_EPS = 1e-08
ADAM_WD = 0.01
ADAM_STEP = 10
PER_EXAMPLE_BATCH_AXIS = {'x': 0, 'mem': 0, 'positions': 0, 'loss_target': 0}
SHARED_INPUTS = []
_WEIGHT_DTYPES = {'norm_mix_g': _jnp.float32, 'norm_xa_g': _jnp.float32, 'norm_mem_g': _jnp.float32, 'xa_wq': _jnp.float32, 'xa_wkv': _jnp.float32, 'xa_wo': _jnp.float32, 'norm_ffn_g': _jnp.float32, 'ffn_w_up': _jnp.float32, 'ffn_conv_w': _jnp.float32, 'ffn_conv_b': _jnp.float32, 'ffn_w_down': _jnp.float32, 'pc_w_in': _jnp.float32, 'pool_w': _jnp.float32, 'pool_scale': _jnp.float32, 'conv_dw_w': _jnp.float32, 'conv_dw_b': _jnp.float32, 'conv_ln_g': _jnp.float32, 'conv_ln_b': _jnp.float32, 'pc_w_out': _jnp.float32, 'mla_w_dq_dkv': _jnp.float32, 'mla_q_norm_g': _jnp.float32, 'mla_w_uq': _jnp.float32, 'mla_kv_norm_g': _jnp.float32, 'mla_w_ukv': _jnp.float32, 'mla_w_o': _jnp.float32, 'final_norm_g': _jnp.float32}
MOMENT_SCALE = {'norm_mix_g': 7.199030e-02, 'norm_xa_g': 1.301351e-02, 'norm_mem_g': 2.003485e-02, 'xa_wq': 1.243562e-02, 'xa_wkv': 1.272807e-02, 'xa_wo': 2.575790e-02, 'norm_ffn_g': 9.932029e-02, 'ffn_w_up': 4.178580e-02, 'ffn_conv_w': 4.195424e-02, 'ffn_conv_b': 4.139530e-02, 'ffn_w_down': 1.361983e-01, 'pc_w_in': 7.797703e-02, 'pool_w': 1.077345e-01, 'pool_scale': 1.153455e-01, 'conv_dw_w': 7.720885e-02, 'conv_dw_b': 1.925799e-01, 'conv_ln_g': 9.764741e-02, 'conv_ln_b': 8.389693e-02, 'pc_w_out': 1.848595e-01, 'mla_w_dq_dkv': 4.829470e-02, 'mla_q_norm_g': 3.284038e-02, 'mla_w_uq': 1.661019e-02, 'mla_kv_norm_g': 6.599293e-02, 'mla_w_ukv': 2.273173e-02, 'mla_w_o': 5.493446e-02, 'final_norm_g': 6.392790e+01}


def _to_microbatches(a, axis):
    t = _jnp.moveaxis(a, axis, 0)
    t = t.reshape((N_MICROBATCH, t.shape[0] // N_MICROBATCH) + t.shape[1:])
    return _jnp.moveaxis(t, 1, axis + 1)


def setup_inputs(seed: int = 0) -> dict:
    inp = _fwd_setup_inputs(seed)
    key = _jax.random.fold_in(_jax.random.key(seed), 7919)
    shape, _ = _output_shape()
    out = dict(inp)
    out["loss_target"] = _jax.random.normal(_jax.random.fold_in(key, 0), shape, _jnp.float32)
    for i, name in enumerate(TWIN_WEIGHTS):
        w = inp[name].astype(_jnp.float32)
        if MOMENT_SCALE is None:
            s = _jnp.sqrt(_jnp.mean(_jnp.square(w)) + 1e-30)
        else:
            s = MOMENT_SCALE[name]
        km, kv = _jax.random.split(_jax.random.fold_in(key, i + 1))
        out[name] = w
        out["m_" + name] = s * _jax.random.normal(km, w.shape, _jnp.float32)
        out["v_" + name] = (s * s) * _jax.random.uniform(kv, w.shape, _jnp.float32, 0.5, 1.5)
    if N_MICROBATCH > 1:
        for name, axis in PER_EXAMPLE_BATCH_AXIS.items():
            out[name] = _to_microbatches(out[name], axis)
    return {'x': out['x'], 'mem': out['mem'], 'positions': out['positions'], 'norm_mix_g': out['norm_mix_g'], 'norm_xa_g': out['norm_xa_g'], 'norm_mem_g': out['norm_mem_g'], 'xa_wq': out['xa_wq'], 'xa_wkv': out['xa_wkv'], 'xa_wo': out['xa_wo'], 'norm_ffn_g': out['norm_ffn_g'], 'ffn_w_up': out['ffn_w_up'], 'ffn_conv_w': out['ffn_conv_w'], 'ffn_conv_b': out['ffn_conv_b'], 'ffn_w_down': out['ffn_w_down'], 'pc_w_in': out['pc_w_in'], 'pool_w': out['pool_w'], 'pool_scale': out['pool_scale'], 'conv_dw_w': out['conv_dw_w'], 'conv_dw_b': out['conv_dw_b'], 'conv_ln_g': out['conv_ln_g'], 'conv_ln_b': out['conv_ln_b'], 'pc_w_out': out['pc_w_out'], 'mla_w_dq_dkv': out['mla_w_dq_dkv'], 'mla_q_norm_g': out['mla_q_norm_g'], 'mla_w_uq': out['mla_w_uq'], 'mla_kv_norm_g': out['mla_kv_norm_g'], 'mla_w_ukv': out['mla_w_ukv'], 'mla_w_o': out['mla_w_o'], 'final_norm_g': out['final_norm_g'], 'loss_target': out['loss_target'], 'm_norm_mix_g': out['m_norm_mix_g'], 'm_norm_xa_g': out['m_norm_xa_g'], 'm_norm_mem_g': out['m_norm_mem_g'], 'm_xa_wq': out['m_xa_wq'], 'm_xa_wkv': out['m_xa_wkv'], 'm_xa_wo': out['m_xa_wo'], 'm_norm_ffn_g': out['m_norm_ffn_g'], 'm_ffn_w_up': out['m_ffn_w_up'], 'm_ffn_conv_w': out['m_ffn_conv_w'], 'm_ffn_conv_b': out['m_ffn_conv_b'], 'm_ffn_w_down': out['m_ffn_w_down'], 'm_pc_w_in': out['m_pc_w_in'], 'm_pool_w': out['m_pool_w'], 'm_pool_scale': out['m_pool_scale'], 'm_conv_dw_w': out['m_conv_dw_w'], 'm_conv_dw_b': out['m_conv_dw_b'], 'm_conv_ln_g': out['m_conv_ln_g'], 'm_conv_ln_b': out['m_conv_ln_b'], 'm_pc_w_out': out['m_pc_w_out'], 'm_mla_w_dq_dkv': out['m_mla_w_dq_dkv'], 'm_mla_q_norm_g': out['m_mla_q_norm_g'], 'm_mla_w_uq': out['m_mla_w_uq'], 'm_mla_kv_norm_g': out['m_mla_kv_norm_g'], 'm_mla_w_ukv': out['m_mla_w_ukv'], 'm_mla_w_o': out['m_mla_w_o'], 'm_final_norm_g': out['m_final_norm_g'], 'v_norm_mix_g': out['v_norm_mix_g'], 'v_norm_xa_g': out['v_norm_xa_g'], 'v_norm_mem_g': out['v_norm_mem_g'], 'v_xa_wq': out['v_xa_wq'], 'v_xa_wkv': out['v_xa_wkv'], 'v_xa_wo': out['v_xa_wo'], 'v_norm_ffn_g': out['v_norm_ffn_g'], 'v_ffn_w_up': out['v_ffn_w_up'], 'v_ffn_conv_w': out['v_ffn_conv_w'], 'v_ffn_conv_b': out['v_ffn_conv_b'], 'v_ffn_w_down': out['v_ffn_w_down'], 'v_pc_w_in': out['v_pc_w_in'], 'v_pool_w': out['v_pool_w'], 'v_pool_scale': out['v_pool_scale'], 'v_conv_dw_w': out['v_conv_dw_w'], 'v_conv_dw_b': out['v_conv_dw_b'], 'v_conv_ln_g': out['v_conv_ln_g'], 'v_conv_ln_b': out['v_conv_ln_b'], 'v_pc_w_out': out['v_pc_w_out'], 'v_mla_w_dq_dkv': out['v_mla_w_dq_dkv'], 'v_mla_q_norm_g': out['v_mla_q_norm_g'], 'v_mla_w_uq': out['v_mla_w_uq'], 'v_mla_kv_norm_g': out['v_mla_kv_norm_g'], 'v_mla_w_ukv': out['v_mla_w_ukv'], 'v_mla_w_o': out['v_mla_w_o'], 'v_final_norm_g': out['v_final_norm_g']}


def _loss(weights, diff, rest, loss_target):
    with _jax.named_scope("forward"):
        args = {**rest, TWIN_DIFF_INPUT: diff, **{k: w.astype(_WEIGHT_DTYPES[k]) for k, w in weights.items()}}
        y = _forward(args)
    with _jax.named_scope("loss_head"):
        err = _jnp.square(y.astype(_jnp.float32) - loss_target)
        return 0.5 * _jnp.sum(_jnp.mean(err, axis=-1)) if err.ndim else 0.5 * err


def _adamw(w, g, m, v):
    m = ADAM_B1 * m + (1.0 - ADAM_B1) * g
    v = ADAM_B2 * v + (1.0 - ADAM_B2) * _jnp.square(g)
    m_hat = m / (1.0 - ADAM_B1 ** ADAM_STEP)
    v_hat = v / (1.0 - ADAM_B2 ** ADAM_STEP)
    delta = -ADAM_LR * (m_hat / (_jnp.sqrt(v_hat) + ADAM_EPS) + ADAM_WD * w)
    return delta, m, v


def reference(x, mem, positions, norm_mix_g, norm_xa_g, norm_mem_g, xa_wq, xa_wkv, xa_wo, norm_ffn_g, ffn_w_up, ffn_conv_w, ffn_conv_b, ffn_w_down, pc_w_in, pool_w, pool_scale, conv_dw_w, conv_dw_b, conv_ln_g, conv_ln_b, pc_w_out, mla_w_dq_dkv, mla_q_norm_g, mla_w_uq, mla_kv_norm_g, mla_w_ukv, mla_w_o, final_norm_g, loss_target, m_norm_mix_g, m_norm_xa_g, m_norm_mem_g, m_xa_wq, m_xa_wkv, m_xa_wo, m_norm_ffn_g, m_ffn_w_up, m_ffn_conv_w, m_ffn_conv_b, m_ffn_w_down, m_pc_w_in, m_pool_w, m_pool_scale, m_conv_dw_w, m_conv_dw_b, m_conv_ln_g, m_conv_ln_b, m_pc_w_out, m_mla_w_dq_dkv, m_mla_q_norm_g, m_mla_w_uq, m_mla_kv_norm_g, m_mla_w_ukv, m_mla_w_o, m_final_norm_g, v_norm_mix_g, v_norm_xa_g, v_norm_mem_g, v_xa_wq, v_xa_wkv, v_xa_wo, v_norm_ffn_g, v_ffn_w_up, v_ffn_conv_w, v_ffn_conv_b, v_ffn_w_down, v_pc_w_in, v_pool_w, v_pool_scale, v_conv_dw_w, v_conv_dw_b, v_conv_ln_g, v_conv_ln_b, v_pc_w_out, v_mla_w_dq_dkv, v_mla_q_norm_g, v_mla_w_uq, v_mla_kv_norm_g, v_mla_w_ukv, v_mla_w_o, v_final_norm_g):
    given = dict(x=x, mem=mem, positions=positions, norm_mix_g=norm_mix_g, norm_xa_g=norm_xa_g, norm_mem_g=norm_mem_g, xa_wq=xa_wq, xa_wkv=xa_wkv, xa_wo=xa_wo, norm_ffn_g=norm_ffn_g, ffn_w_up=ffn_w_up, ffn_conv_w=ffn_conv_w, ffn_conv_b=ffn_conv_b, ffn_w_down=ffn_w_down, pc_w_in=pc_w_in, pool_w=pool_w, pool_scale=pool_scale, conv_dw_w=conv_dw_w, conv_dw_b=conv_dw_b, conv_ln_g=conv_ln_g, conv_ln_b=conv_ln_b, pc_w_out=pc_w_out, mla_w_dq_dkv=mla_w_dq_dkv, mla_q_norm_g=mla_q_norm_g, mla_w_uq=mla_w_uq, mla_kv_norm_g=mla_kv_norm_g, mla_w_ukv=mla_w_ukv, mla_w_o=mla_w_o, final_norm_g=final_norm_g, loss_target=loss_target, m_norm_mix_g=m_norm_mix_g, m_norm_xa_g=m_norm_xa_g, m_norm_mem_g=m_norm_mem_g, m_xa_wq=m_xa_wq, m_xa_wkv=m_xa_wkv, m_xa_wo=m_xa_wo, m_norm_ffn_g=m_norm_ffn_g, m_ffn_w_up=m_ffn_w_up, m_ffn_conv_w=m_ffn_conv_w, m_ffn_conv_b=m_ffn_conv_b, m_ffn_w_down=m_ffn_w_down, m_pc_w_in=m_pc_w_in, m_pool_w=m_pool_w, m_pool_scale=m_pool_scale, m_conv_dw_w=m_conv_dw_w, m_conv_dw_b=m_conv_dw_b, m_conv_ln_g=m_conv_ln_g, m_conv_ln_b=m_conv_ln_b, m_pc_w_out=m_pc_w_out, m_mla_w_dq_dkv=m_mla_w_dq_dkv, m_mla_q_norm_g=m_mla_q_norm_g, m_mla_w_uq=m_mla_w_uq, m_mla_kv_norm_g=m_mla_kv_norm_g, m_mla_w_ukv=m_mla_w_ukv, m_mla_w_o=m_mla_w_o, m_final_norm_g=m_final_norm_g, v_norm_mix_g=v_norm_mix_g, v_norm_xa_g=v_norm_xa_g, v_norm_mem_g=v_norm_mem_g, v_xa_wq=v_xa_wq, v_xa_wkv=v_xa_wkv, v_xa_wo=v_xa_wo, v_norm_ffn_g=v_norm_ffn_g, v_ffn_w_up=v_ffn_w_up, v_ffn_conv_w=v_ffn_conv_w, v_ffn_conv_b=v_ffn_conv_b, v_ffn_w_down=v_ffn_w_down, v_pc_w_in=v_pc_w_in, v_pool_w=v_pool_w, v_pool_scale=v_pool_scale, v_conv_dw_w=v_conv_dw_w, v_conv_dw_b=v_conv_dw_b, v_conv_ln_g=v_conv_ln_g, v_conv_ln_b=v_conv_ln_b, v_pc_w_out=v_pc_w_out, v_mla_w_dq_dkv=v_mla_w_dq_dkv, v_mla_q_norm_g=v_mla_q_norm_g, v_mla_w_uq=v_mla_w_uq, v_mla_kv_norm_g=v_mla_kv_norm_g, v_mla_w_ukv=v_mla_w_ukv, v_mla_w_o=v_mla_w_o, v_final_norm_g=v_final_norm_g)
    weights = {n: given[n] for n in TWIN_WEIGHTS}
    shared = {n: given[n] for n in SHARED_INPUTS}
    per_example = {n: given[n] for n in ['x', 'mem', 'positions']}
    grad_fn = _jax.value_and_grad(_loss, argnums=(0, 1))

    def one_microbatch(ex, loss_target):
        ex = dict(ex)
        diff = ex.pop(TWIN_DIFF_INPUT)
        return grad_fn(weights, diff, {**shared, **ex}, loss_target)

    if N_MICROBATCH == 1:
        loss, (grad_w, grad_x) = one_microbatch(per_example, given["loss_target"])
    else:
        def body(carry, xs):
            loss_sum, grad_sum = carry
            l_k, (gw_k, gx_k) = one_microbatch(xs[0], xs[1])
            with _jax.named_scope("update"):
                return (loss_sum + l_k, _jax.tree.map(_jnp.add, grad_sum, gw_k)), gx_k

        init = (_jnp.zeros((), _jnp.float32), _jax.tree.map(_jnp.zeros_like, weights))
        (loss, grad_w), grad_x = _jax.lax.scan(body, init, (per_example, given["loss_target"]))
    with _jax.named_scope("update"):
        delta_w, new_m, new_v = {}, {}, {}
        for n in TWIN_WEIGHTS:
            delta_w[n], new_m[n], new_v[n] = _adamw(weights[n], grad_w[n], given["m_" + n], given["v_" + n])
    return (loss, grad_x, *[grad_w[n] for n in TWIN_WEIGHTS], *[delta_w[n] for n in TWIN_WEIGHTS],
            *[new_m[n] for n in TWIN_WEIGHTS], *[new_v[n] for n in TWIN_WEIGHTS])
```

```python
import functools
import math

import jax
import jax.numpy as jnp
from jax import lax
from jax.experimental import pallas as pl
from jax.experimental.pallas import tpu as pltpu

F32 = jnp.float32
CD = jnp.bfloat16
EPS = 1e-6
NEG = -1e30
N_DEV = 8
LANES = 128
HALO = 32

D_MODEL = 1024
DEPTH = 4
XA_HEADS = 4
XA_DH = 256
MEM_LEN = 256
POOL_WINDOWS = (2, 4, 8, 16)
CONV_K = 31
FFN_K = 3
D_FF = 2816
MLA_HEADS = 16
QK_NOPE = 64
QK_ROPE = 32
V_HEAD = 64
Q_LORA = 384
KV_LORA = 256
ROPE_THETA = 10000.0
MLA_SCALE = 1.0 / math.sqrt(QK_NOPE + QK_ROPE)
XA_SCALE = XA_DH ** -0.5

ADAM_LR = 0.001
ADAM_B1 = 0.9
ADAM_B2 = 0.999
ADAM_EPS = 1e-08
ADAM_WD = 0.01
ADAM_STEP = 10

NT = (((1,), (1,)), ((), ()))
TN = (((0,), (0,)), ((), ()))
MESH = pl.DeviceIdType.MESH


def _tile(n, target):
    if n <= target:
        return n
    best = None
    for t in range(LANES, target + 1, LANES):
        if n % t == 0:
            best = t
    assert best is not None, (n, target)
    return best


def _params(*sem):
    return pltpu.CompilerParams(dimension_semantics=sem)


def _sigmoid(v):
    return 1.0 / (1.0 + jnp.exp(-v))


def _rms_bwd(x, gain, dh):
    r = lax.rsqrt(jnp.mean(x * x, axis=-1, keepdims=True) + EPS)
    xhat = x * r
    dxhat = dh * gain
    dx = r * (dxhat - xhat * jnp.mean(dxhat * xhat, axis=-1, keepdims=True))
    return dx, dh * xhat


def _nmm(x, g, w, *, name, out_dtype, tm=512, tn_target=1024):
    M, K = x.shape
    N = w.shape[1]
    tm = min(tm, M)
    tn = _tile(N, tn_target)

    def body(x_ref, g_ref, w_ref, z_ref, h_ref):
        @pl.when(pl.program_id(1) == 0)
        def _():
            xf = x_ref[...]
            r = lax.rsqrt(jnp.mean(xf * xf, axis=-1, keepdims=True) + EPS)
            h_ref[...] = (xf * r * g_ref[...]).astype(h_ref.dtype)

        z_ref[...] = jnp.dot(h_ref[...], w_ref[...], preferred_element_type=F32).astype(z_ref.dtype)

    return pl.pallas_call(
        body, name=name, grid=(M // tm, N // tn),
        in_specs=[pl.BlockSpec((tm, K), lambda i, j: (i, 0)),
                  pl.BlockSpec((1, K), lambda i, j: (0, 0)),
                  pl.BlockSpec((K, tn), lambda i, j: (0, j))],
        out_specs=[pl.BlockSpec((tm, tn), lambda i, j: (i, j)),
                   pl.BlockSpec((tm, K), lambda i, j: (i, 0))],
        out_shape=[jax.ShapeDtypeStruct((M, N), out_dtype), jax.ShapeDtypeStruct((M, K), CD)],
        compiler_params=_params("parallel", "arbitrary"),
    )(x, g, w)


def _mm_res(a, w, res, *, name, tm=512, tn_target=1024):
    M, K = a.shape
    N = w.shape[1]
    tm = min(tm, M)
    tn = _tile(N, tn_target)

    def body(a_ref, w_ref, r_ref, o_ref):
        o_ref[...] = r_ref[...] + jnp.dot(a_ref[...].astype(CD), w_ref[...], preferred_element_type=F32)

    return pl.pallas_call(
        body, name=name, grid=(M // tm, N // tn),
        in_specs=[pl.BlockSpec((tm, K), lambda i, j: (i, 0)),
                  pl.BlockSpec((K, tn), lambda i, j: (0, j)),
                  pl.BlockSpec((tm, tn), lambda i, j: (i, j))],
        out_specs=pl.BlockSpec((tm, tn), lambda i, j: (i, j)),
        out_shape=jax.ShapeDtypeStruct((M, N), F32),
        compiler_params=_params("parallel", "arbitrary"),
    )(a, w, res)


def _mm_nt(a, w, *, name, out_dtype, tm=512, tn_target=1024):
    M, K = a.shape
    N = w.shape[0]
    tm = min(tm, M)
    tn = _tile(N, tn_target)

    def body(a_ref, w_ref, o_ref):
        o_ref[...] = lax.dot_general(a_ref[...].astype(CD), w_ref[...], NT,
                                     preferred_element_type=F32).astype(o_ref.dtype)

    return pl.pallas_call(
        body, name=name, grid=(M // tm, N // tn),
        in_specs=[pl.BlockSpec((tm, K), lambda i, j: (i, 0)),
                  pl.BlockSpec((tn, K), lambda i, j: (j, 0))],
        out_specs=pl.BlockSpec((tm, tn), lambda i, j: (i, j)),
        out_shape=jax.ShapeDtypeStruct((M, N), out_dtype),
        compiler_params=_params("parallel", "arbitrary"),
    )(a, w)


def _mm_nt_normbwd(gy, w, x, gain, dres, *, name, tm=512, tk_target=1408):
    M, K = gy.shape
    D = w.shape[0]
    tm = min(tm, M)
    tk = _tile(K, tk_target)
    nk = K // tk

    def body(g_ref, w_ref, x_ref, gain_ref, dres_ref, dx_ref, dg_ref, acc):
        i, k = pl.program_id(0), pl.program_id(1)

        @pl.when(k == 0)
        def _():
            acc[...] = jnp.zeros_like(acc)

        acc[...] += lax.dot_general(g_ref[...].astype(CD), w_ref[...], NT, preferred_element_type=F32)

        @pl.when(k == nk - 1)
        def _():
            dx, dg_rows = _rms_bwd(x_ref[...], gain_ref[...], acc[...])
            dx_ref[...] = dres_ref[...] + dx

            @pl.when(i == 0)
            def _():
                dg_ref[...] = jnp.zeros_like(dg_ref)

            dg_ref[...] += jnp.sum(dg_rows, axis=0, keepdims=True)

    return pl.pallas_call(
        body, name=name, grid=(M // tm, nk),
        in_specs=[pl.BlockSpec((tm, tk), lambda i, k: (i, k)),
                  pl.BlockSpec((D, tk), lambda i, k: (0, k)),
                  pl.BlockSpec((tm, D), lambda i, k: (i, 0)),
                  pl.BlockSpec((1, D), lambda i, k: (0, 0)),
                  pl.BlockSpec((tm, D), lambda i, k: (i, 0))],
        out_specs=[pl.BlockSpec((tm, D), lambda i, k: (i, 0)),
                   pl.BlockSpec((1, D), lambda i, k: (0, 0))],
        out_shape=[jax.ShapeDtypeStruct((M, D), F32), jax.ShapeDtypeStruct((1, D), F32)],
        scratch_shapes=[pltpu.VMEM((tm, D), F32)],
        compiler_params=_params("arbitrary", "arbitrary"),
    )(gy, w, x, gain, dres)


def _mm_tn(a, g, *, name, tt=512, tk_target=1024, tn_target=1024):
    T, K = a.shape
    N = g.shape[1]
    tt = min(tt, T)
    tk = _tile(K, tk_target)
    tn = _tile(N, tn_target)

    def body(a_ref, g_ref, o_ref):
        @pl.when(pl.program_id(2) == 0)
        def _():
            o_ref[...] = jnp.zeros_like(o_ref)

        o_ref[...] += lax.dot_general(a_ref[...].astype(CD), g_ref[...].astype(CD), TN,
                                      preferred_element_type=F32)

    return pl.pallas_call(
        body, name=name, grid=(K // tk, N // tn, T // tt),
        in_specs=[pl.BlockSpec((tt, tk), lambda i, j, t: (t, i)),
                  pl.BlockSpec((tt, tn), lambda i, j, t: (t, j))],
        out_specs=pl.BlockSpec((tk, tn), lambda i, j, t: (i, j)),
        out_shape=jax.ShapeDtypeStruct((K, N), F32),
        compiler_params=_params("parallel", "parallel", "arbitrary"),
    )(a, g)


POOL_W = 512
CONV_W = 512
POOL_GROUP = 128


def _fill_ext(ext, prev_ref, cur_ref, next_ref, i, n, tt):
    zeros = jnp.zeros((HALO,) + ext.shape[1:], F32)
    ext[pl.ds(0, HALO), :] = jnp.where(i > 0, prev_ref[...].astype(F32), zeros)
    ext[pl.ds(HALO, tt), :] = cur_ref[...].astype(F32)
    if next_ref is not None:
        ext[pl.ds(HALO + tt, HALO), :] = jnp.where(i < n - 1, next_ref[...].astype(F32), zeros)


def _mixer_fwd(z, pool_w, pool_scale, dw_w, dw_b, ln_g, ln_b, *, name, tt=256):
    T = z.shape[0]
    tt = min(tt, T)
    n = T // tt
    hb = tt // HALO

    def body(zp_ref, z_ref, pw_ref, ps_ref, w_ref, b_ref, g_ref, bb_ref, o_ref, ext, gl):
        i = pl.program_id(0)
        _fill_ext(ext, zp_ref, z_ref, None, i, n, tt)
        t_glob = i * tt + lax.broadcasted_iota(jnp.int32, (tt, 1), 0)
        for gi, win in enumerate(POOL_WINDOWS):
            cols = pl.ds(gi * POOL_GROUP, POOL_GROUP)
            u = ext[pl.ds(HALO, tt), cols]
            s = u
            for j in range(1, win):
                s = s + ext[pl.ds(HALO - j, tt), cols]
            cnt = jnp.minimum(t_glob + 1, win).astype(F32)
            pooled = s / cnt - u
            ya = jnp.dot(pooled.astype(CD), pw_ref[gi].astype(CD), preferred_element_type=F32)
            o_ref[:, cols] = (ya * ps_ref[:, cols]).astype(o_ref.dtype)
        a = ext[:, pl.ds(POOL_W, CONV_W)]
        b = ext[:, pl.ds(POOL_W + CONV_W, CONV_W)]
        gl[...] = a * _sigmoid(b)
        cv = jnp.zeros((tt, CONV_W), F32) + b_ref[...]
        for j in range(CONV_K):
            cv = cv + w_ref[pl.ds(j, 1), :] * gl[pl.ds(HALO - (CONV_K - 1) + j, tt), :]
        mu = jnp.mean(cv, axis=-1, keepdims=True)
        xc = cv - mu
        yn = xc * lax.rsqrt(jnp.mean(xc * xc, axis=-1, keepdims=True) + EPS) * g_ref[...] + bb_ref[...]
        o_ref[:, pl.ds(POOL_W, CONV_W)] = (yn * _sigmoid(yn)).astype(o_ref.dtype)

    C = z.shape[1]
    full = lambda shape: pl.BlockSpec(shape, lambda i: (0,) * len(shape))
    return pl.pallas_call(
        body, name=name, grid=(n,),
        in_specs=[pl.BlockSpec((HALO, C), lambda i: (jnp.maximum(i * hb - 1, 0), 0)),
                  pl.BlockSpec((tt, C), lambda i: (i, 0)),
                  full((4, POOL_GROUP, POOL_GROUP)), full((1, POOL_W)), full((CONV_K + 1, CONV_W)),
                  full((1, CONV_W)), full((1, CONV_W)), full((1, CONV_W))],
        out_specs=pl.BlockSpec((tt, POOL_W + CONV_W), lambda i: (i, 0)),
        out_shape=jax.ShapeDtypeStruct((T, POOL_W + CONV_W), CD),
        scratch_shapes=[pltpu.VMEM((tt + HALO, C), F32), pltpu.VMEM((tt + HALO, CONV_W), F32)],
        compiler_params=_params("parallel"),
    )(z, z, pool_w, pool_scale, dw_w, dw_b, ln_g, ln_b)


def _mixer_bwd(z, dy, pool_w, pool_scale, dw_w, dw_b, ln_g, ln_b, *, name, tt=256):
    T, C = z.shape
    tt = min(tt, T)
    n = T // tt
    hb = tt // HALO
    R = tt + HALO

    def body(zp_ref, z_ref, zn_ref, dy_ref, dyn_ref, pw_ref, ps_ref, w_ref, b_ref, g_ref, bb_ref,
             dz_ref, dpw_ref, dps_ref, dw_ref, db_ref, dg_ref, dbb_ref, ext, gl, dye, dcv, dpe):
        i = pl.program_id(0)

        @pl.when(i == 0)
        def _():
            for r in (dpw_ref, dps_ref, dw_ref, db_ref, dg_ref, dbb_ref):
                r[...] = jnp.zeros_like(r)

        _fill_ext(ext, zp_ref, z_ref, zn_ref, i, n, tt)
        dye[pl.ds(0, tt), :] = dy_ref[...]
        dye[pl.ds(tt, HALO), :] = jnp.where(i < n - 1, dyn_ref[...], jnp.zeros((HALO, 2 * POOL_W), F32))
        t_glob = i * tt + lax.broadcasted_iota(jnp.int32, (R, 1), 0)

        for gi, win in enumerate(POOL_WINDOWS):
            cols = pl.ds(gi * POOL_GROUP, POOL_GROUP)
            u = ext[pl.ds(HALO, tt), cols]
            s = u
            for j in range(1, win):
                s = s + ext[pl.ds(HALO - j, tt), cols]
            cnt = jnp.minimum(t_glob + 1, win).astype(F32)
            pooled = (s / cnt[:tt] - u).astype(CD)
            pw = pw_ref[gi].astype(CD)
            dya = dye[:, cols]
            mm = jnp.dot(pooled, pw, preferred_element_type=F32)
            dps_ref[:, cols] += jnp.sum(dya[:tt] * mm, axis=0, keepdims=True)
            dm = (dya * ps_ref[:, cols]).astype(CD)
            dpw_ref[gi] += lax.dot_general(pooled, dm[:tt], TN, preferred_element_type=F32)
            dpool = lax.dot_general(dm, pw, NT, preferred_element_type=F32)
            dpe[...] = dpool / cnt
            du = -dpool[:tt]
            for j in range(win):
                du = du + dpe[pl.ds(j, tt), :]
            dz_ref[:, cols] = du.astype(dz_ref.dtype)

        a = ext[:, pl.ds(POOL_W, CONV_W)]
        b = ext[:, pl.ds(POOL_W + CONV_W, CONV_W)]
        sb = _sigmoid(b)
        gl[...] = a * sb
        cv = jnp.zeros((R, CONV_W), F32) + b_ref[...]
        for j in range(CONV_K):
            cv = cv + w_ref[pl.ds(j, 1), :] * gl[pl.ds(HALO - (CONV_K - 1) + j, R), :]
        mu = jnp.mean(cv, axis=-1, keepdims=True)
        xc = cv - mu
        rstd = lax.rsqrt(jnp.mean(xc * xc, axis=-1, keepdims=True) + EPS)
        xhat = xc * rstd
        yn = xhat * g_ref[...] + bb_ref[...]
        sy = _sigmoid(yn)
        dyn = dye[:, pl.ds(POOL_W, CONV_W)] * (sy * (1.0 + yn * (1.0 - sy)))
        dg_ref[...] += jnp.sum(dyn[:tt] * xhat[:tt], axis=0, keepdims=True)
        dbb_ref[...] += jnp.sum(dyn[:tt], axis=0, keepdims=True)
        dxh = dyn * g_ref[...]
        dcv_v = rstd * (dxh - jnp.mean(dxh, axis=-1, keepdims=True)
                        - xhat * jnp.mean(dxh * xhat, axis=-1, keepdims=True))
        dcv[...] = dcv_v
        db_ref[...] += jnp.sum(dcv_v[:tt], axis=0, keepdims=True)
        dgl = jnp.zeros((tt, CONV_W), F32)
        for j in range(CONV_K):
            dgl = dgl + w_ref[pl.ds(j, 1), :] * dcv[pl.ds(CONV_K - 1 - j, tt), :]
            dw_ref[pl.ds(j, 1), :] += jnp.sum(dcv_v[:tt] * gl[pl.ds(HALO - (CONV_K - 1) + j, tt), :],
                                              axis=0, keepdims=True)
        a_t = a[HALO:HALO + tt]
        sb_t = sb[HALO:HALO + tt]
        dz_ref[:, pl.ds(POOL_W, CONV_W)] = (dgl * sb_t).astype(dz_ref.dtype)
        dz_ref[:, pl.ds(POOL_W + CONV_W, CONV_W)] = (dgl * a_t * sb_t * (1.0 - sb_t)).astype(dz_ref.dtype)

    full = lambda shape: pl.BlockSpec(shape, lambda i: (0,) * len(shape))
    nb = T // HALO
    outs = pl.pallas_call(
        body, name=name, grid=(n,),
        in_specs=[pl.BlockSpec((HALO, C), lambda i: (jnp.maximum(i * hb - 1, 0), 0)),
                  pl.BlockSpec((tt, C), lambda i: (i, 0)),
                  pl.BlockSpec((HALO, C), lambda i: (jnp.minimum((i + 1) * hb, nb - 1), 0)),
                  pl.BlockSpec((tt, 2 * POOL_W), lambda i: (i, 0)),
                  pl.BlockSpec((HALO, 2 * POOL_W), lambda i: (jnp.minimum((i + 1) * hb, nb - 1), 0)),
                  full((4, POOL_GROUP, POOL_GROUP)), full((1, POOL_W)), full((CONV_K + 1, CONV_W)),
                  full((1, CONV_W)), full((1, CONV_W)), full((1, CONV_W))],
        out_specs=[pl.BlockSpec((tt, C), lambda i: (i, 0)),
                   full((4, POOL_GROUP, POOL_GROUP)), full((1, POOL_W)), full((CONV_K + 1, CONV_W)),
                   full((1, CONV_W)), full((1, CONV_W)), full((1, CONV_W))],
        out_shape=[jax.ShapeDtypeStruct((T, C), CD),
                   jax.ShapeDtypeStruct((4, POOL_GROUP, POOL_GROUP), F32),
                   jax.ShapeDtypeStruct((1, POOL_W), F32),
                   jax.ShapeDtypeStruct((CONV_K + 1, CONV_W), F32),
                   jax.ShapeDtypeStruct((1, CONV_W), F32),
                   jax.ShapeDtypeStruct((1, CONV_W), F32),
                   jax.ShapeDtypeStruct((1, CONV_W), F32)],
        scratch_shapes=[pltpu.VMEM((tt + 2 * HALO, C), F32), pltpu.VMEM((tt + 2 * HALO, CONV_W), F32),
                        pltpu.VMEM((R, 2 * POOL_W), F32), pltpu.VMEM((R, CONV_W), F32),
                        pltpu.VMEM((R, POOL_GROUP), F32)],
        compiler_params=_params("arbitrary"),
    )(z, z, z, dy, dy, pool_w, pool_scale, dw_w, dw_b, ln_g, ln_b)
    return outs


def _ffn_mid_fwd(up, cw, cb, *, name, tt=256):
    T = up.shape[0]
    tt = min(tt, T)
    n = T // tt
    hb = tt // HALO

    def body(a_ref, gp_ref, g_ref, w_ref, b_ref, o_ref, ext):
        i = pl.program_id(0)
        _fill_ext(ext, gp_ref, g_ref, None, i, n, tt)
        gc = jnp.zeros((tt, D_FF), F32) + b_ref[...]
        for j in range(FFN_K):
            gc = gc + w_ref[pl.ds(j, 1), :] * ext[pl.ds(HALO - (FFN_K - 1) + j, tt), :]
        o_ref[...] = (gc * _sigmoid(gc) * a_ref[...].astype(F32)).astype(o_ref.dtype)

    return pl.pallas_call(
        body, name=name, grid=(n,),
        in_specs=[pl.BlockSpec((tt, D_FF), lambda i: (i, 0)),
                  pl.BlockSpec((HALO, D_FF), lambda i: (jnp.maximum(i * hb - 1, 0), 1)),
                  pl.BlockSpec((tt, D_FF), lambda i: (i, 1)),
                  pl.BlockSpec((8, D_FF), lambda i: (0, 0)),
                  pl.BlockSpec((1, D_FF), lambda i: (0, 0))],
        out_specs=pl.BlockSpec((tt, D_FF), lambda i: (i, 0)),
        out_shape=jax.ShapeDtypeStruct((T, D_FF), CD),
        scratch_shapes=[pltpu.VMEM((tt + HALO, D_FF), F32)],
        compiler_params=_params("parallel"),
    )(up, up, up, cw, cb)


def _ffn_mid_bwd(up, dact, cw, cb, *, name, tt=256):
    T = up.shape[0]
    tt = min(tt, T)
    n = T // tt
    hb = tt // HALO
    nb = T // HALO
    R = tt + HALO

    def body(a_ref, an_ref, gp_ref, g_ref, gn_ref, d_ref, dn_ref, w_ref, b_ref,
             dup_ref, dw_ref, db_ref, ext, dgc):
        i = pl.program_id(0)

        @pl.when(i == 0)
        def _():
            dw_ref[...] = jnp.zeros_like(dw_ref)
            db_ref[...] = jnp.zeros_like(db_ref)

        _fill_ext(ext, gp_ref, g_ref, gn_ref, i, n, tt)
        gc = jnp.zeros((R, D_FF), F32) + b_ref[...]
        for j in range(FFN_K):
            gc = gc + w_ref[pl.ds(j, 1), :] * ext[pl.ds(HALO - (FFN_K - 1) + j, R), :]
        sg = _sigmoid(gc)
        zeros = jnp.zeros((HALO, D_FF), F32)
        d_t = d_ref[...].astype(F32)
        a_t = a_ref[...].astype(F32)
        dsilu = sg * (1.0 + gc * (1.0 - sg))
        dgc_t = d_t * a_t * dsilu[:tt]
        dgc[pl.ds(0, tt), :] = dgc_t
        dgc[pl.ds(tt, HALO), :] = jnp.where(
            i < n - 1, dn_ref[...].astype(F32) * an_ref[...].astype(F32) * dsilu[tt:], zeros)
        dup_ref[:, pl.ds(0, D_FF)] = (d_t * gc[:tt] * sg[:tt]).astype(dup_ref.dtype)
        dg = jnp.zeros((tt, D_FF), F32)
        for j in range(FFN_K):
            dg = dg + w_ref[pl.ds(j, 1), :] * dgc[pl.ds(FFN_K - 1 - j, tt), :]
            dw_ref[pl.ds(j, 1), :] += jnp.sum(dgc_t * ext[pl.ds(HALO - (FFN_K - 1) + j, tt), :],
                                              axis=0, keepdims=True)
        dup_ref[:, pl.ds(D_FF, D_FF)] = dg.astype(dup_ref.dtype)
        db_ref[...] += jnp.sum(dgc_t, axis=0, keepdims=True)

    nxt = lambda i: jnp.minimum((i + 1) * hb, nb - 1)
    return pl.pallas_call(
        body, name=name, grid=(n,),
        in_specs=[pl.BlockSpec((tt, D_FF), lambda i: (i, 0)),
                  pl.BlockSpec((HALO, D_FF), lambda i: (nxt(i), 0)),
                  pl.BlockSpec((HALO, D_FF), lambda i: (jnp.maximum(i * hb - 1, 0), 1)),
                  pl.BlockSpec((tt, D_FF), lambda i: (i, 1)),
                  pl.BlockSpec((HALO, D_FF), lambda i: (nxt(i), 1)),
                  pl.BlockSpec((tt, D_FF), lambda i: (i, 0)),
                  pl.BlockSpec((HALO, D_FF), lambda i: (nxt(i), 0)),
                  pl.BlockSpec((8, D_FF), lambda i: (0, 0)),
                  pl.BlockSpec((1, D_FF), lambda i: (0, 0))],
        out_specs=[pl.BlockSpec((tt, 2 * D_FF), lambda i: (i, 0)),
                   pl.BlockSpec((8, D_FF), lambda i: (0, 0)),
                   pl.BlockSpec((1, D_FF), lambda i: (0, 0))],
        out_shape=[jax.ShapeDtypeStruct((T, 2 * D_FF), CD),
                   jax.ShapeDtypeStruct((8, D_FF), F32),
                   jax.ShapeDtypeStruct((1, D_FF), F32)],
        scratch_shapes=[pltpu.VMEM((tt + 2 * HALO, D_FF), F32), pltpu.VMEM((R, D_FF), F32)],
        compiler_params=_params("arbitrary"),
    )(up, up, up, up, up, dact, dact, cw, cb)


def _xattn_probs(q, k):
    s = lax.dot_general(q, k, NT, preferred_element_type=F32) * XA_SCALE
    p = jnp.exp(s - jnp.max(s, axis=-1, keepdims=True))
    return p / jnp.sum(p, axis=-1, keepdims=True)


def _xattn_fwd(q, kv, *, name, tq=512):
    T = q.shape[0]
    tq = min(tq, T)

    def body(q_ref, kv_ref, o_ref):
        for h in range(XA_HEADS):
            cols = pl.ds(h * XA_DH, XA_DH)
            p = _xattn_probs(q_ref[:, cols], kv_ref[:, cols])
            v = kv_ref[:, pl.ds(D_MODEL + h * XA_DH, XA_DH)]
            o_ref[:, cols] = jnp.dot(p.astype(CD), v, preferred_element_type=F32).astype(o_ref.dtype)

    return pl.pallas_call(
        body, name=name, grid=(T // tq,),
        in_specs=[pl.BlockSpec((tq, D_MODEL), lambda i: (i, 0)),
                  pl.BlockSpec((MEM_LEN, 2 * D_MODEL), lambda i: (0, 0))],
        out_specs=pl.BlockSpec((tq, D_MODEL), lambda i: (i, 0)),
        out_shape=jax.ShapeDtypeStruct((T, D_MODEL), CD),
        compiler_params=_params("parallel"),
    )(q, kv)


def _xattn_bwd(q, kv, do, *, name, tq=512):
    T = q.shape[0]
    tq = min(tq, T)

    def body(q_ref, kv_ref, do_ref, dq_ref, dkv_ref):
        @pl.when(pl.program_id(0) == 0)
        def _():
            dkv_ref[...] = jnp.zeros_like(dkv_ref)

        for h in range(XA_HEADS):
            cols = pl.ds(h * XA_DH, XA_DH)
            vcols = pl.ds(D_MODEL + h * XA_DH, XA_DH)
            qh, kh, vh, doh = q_ref[:, cols], kv_ref[:, cols], kv_ref[:, vcols], do_ref[:, cols]
            p = _xattn_probs(qh, kh)
            dkv_ref[:, vcols] += lax.dot_general(p.astype(CD), doh, TN, preferred_element_type=F32)
            dp = lax.dot_general(doh, vh, NT, preferred_element_type=F32)
            ds = (p * (dp - jnp.sum(dp * p, axis=-1, keepdims=True)) * XA_SCALE).astype(CD)
            dq_ref[:, cols] = jnp.dot(ds, kh, preferred_element_type=F32).astype(dq_ref.dtype)
            dkv_ref[:, cols] += lax.dot_general(ds, qh, TN, preferred_element_type=F32)

    return pl.pallas_call(
        body, name=name, grid=(T // tq,),
        in_specs=[pl.BlockSpec((tq, D_MODEL), lambda i: (i, 0)),
                  pl.BlockSpec((MEM_LEN, 2 * D_MODEL), lambda i: (0, 0)),
                  pl.BlockSpec((tq, D_MODEL), lambda i: (i, 0))],
        out_specs=[pl.BlockSpec((tq, D_MODEL), lambda i: (i, 0)),
                   pl.BlockSpec((MEM_LEN, 2 * D_MODEL), lambda i: (0, 0))],
        out_shape=[jax.ShapeDtypeStruct((T, D_MODEL), CD),
                   jax.ShapeDtypeStruct((MEM_LEN, 2 * D_MODEL), F32)],
        compiler_params=_params("arbitrary"),
    )(q, kv, do)


C_W = Q_LORA + KV_LORA + LANES


def _rot(x):
    lane = lax.broadcasted_iota(jnp.int32, x.shape, x.ndim - 1)
    up = pltpu.roll(x, LANES - QK_ROPE // 2, x.ndim - 1)
    dn = pltpu.roll(x, QK_ROPE // 2, x.ndim - 1)
    lo, mid, hi = QK_NOPE, QK_NOPE + QK_ROPE // 2, QK_NOPE + QK_ROPE
    return jnp.where((lane >= lo) & (lane < mid), -up, jnp.where((lane >= mid) & (lane < hi), dn, 0.0))


def _mla_mid_fwd(c, qg, kvg, cs, sn, *, name, tt=512):
    T = c.shape[0]
    tt = min(tt, T)

    def body(c_ref, qg_ref, kg_ref, cs_ref, sn_ref, qn_ref, kn_ref, kpe_ref):
        cq = c_ref[:, pl.ds(0, Q_LORA)]
        qn_ref[...] = (cq * lax.rsqrt(jnp.mean(cq * cq, axis=-1, keepdims=True) + EPS)
                       * qg_ref[...]).astype(qn_ref.dtype)
        ck = c_ref[:, pl.ds(Q_LORA, KV_LORA)]
        kn_ref[...] = (ck * lax.rsqrt(jnp.mean(ck * ck, axis=-1, keepdims=True) + EPS)
                       * kg_ref[...]).astype(kn_ref.dtype)
        kp = c_ref[:, pl.ds(Q_LORA + KV_LORA, LANES)]
        kpe_ref[...] = kp * cs_ref[...] + _rot(kp) * sn_ref[...]

    row = lambda w: pl.BlockSpec((tt, w), lambda i: (i, 0))
    one = lambda w: pl.BlockSpec((1, w), lambda i: (0, 0))
    return pl.pallas_call(
        body, name=name, grid=(T // tt,),
        in_specs=[row(C_W), one(Q_LORA), one(KV_LORA), row(LANES), row(LANES)],
        out_specs=[row(Q_LORA), row(KV_LORA), row(LANES)],
        out_shape=[jax.ShapeDtypeStruct((T, Q_LORA), CD), jax.ShapeDtypeStruct((T, KV_LORA), CD),
                   jax.ShapeDtypeStruct((T, LANES), F32)],
        compiler_params=_params("parallel"),
    )(c, qg, kvg, cs, sn)


def _mla_mid_bwd(c, dqn, dkvn, dksum, qg, kvg, cs, sn, *, name, tt=512):
    T = c.shape[0]
    tt = min(tt, T)

    def body(c_ref, dq_ref, dk_ref, ds_ref, qg_ref, kg_ref, cs_ref, sn_ref, dc_ref, dqg_ref, dkg_ref):
        @pl.when(pl.program_id(0) == 0)
        def _():
            dqg_ref[...] = jnp.zeros_like(dqg_ref)
            dkg_ref[...] = jnp.zeros_like(dkg_ref)

        dx, dg = _rms_bwd(c_ref[:, pl.ds(0, Q_LORA)], qg_ref[...], dq_ref[...])
        dc_ref[:, pl.ds(0, Q_LORA)] = dx.astype(dc_ref.dtype)
        dqg_ref[...] += jnp.sum(dg, axis=0, keepdims=True)
        dx, dg = _rms_bwd(c_ref[:, pl.ds(Q_LORA, KV_LORA)], kg_ref[...], dk_ref[...])
        dc_ref[:, pl.ds(Q_LORA, KV_LORA)] = dx.astype(dc_ref.dtype)
        dkg_ref[...] += jnp.sum(dg, axis=0, keepdims=True)
        d = ds_ref[...]
        lane = lax.broadcasted_iota(jnp.int32, d.shape, 1)
        dkp = d * cs_ref[...] - _rot(d * sn_ref[...])
        dc_ref[:, pl.ds(Q_LORA + KV_LORA, LANES)] = jnp.where(
            (lane >= QK_NOPE) & (lane < QK_NOPE + QK_ROPE), dkp, 0.0).astype(dc_ref.dtype)

    row = lambda w: pl.BlockSpec((tt, w), lambda i: (i, 0))
    one = lambda w: pl.BlockSpec((1, w), lambda i: (0, 0))
    return pl.pallas_call(
        body, name=name, grid=(T // tt,),
        in_specs=[row(C_W), row(Q_LORA), row(KV_LORA), row(LANES), one(Q_LORA), one(KV_LORA),
                  row(LANES), row(LANES)],
        out_specs=[row(C_W), one(Q_LORA), one(KV_LORA)],
        out_shape=[jax.ShapeDtypeStruct((T, C_W), CD), jax.ShapeDtypeStruct((1, Q_LORA), F32),
                   jax.ShapeDtypeStruct((1, KV_LORA), F32)],
        compiler_params=_params("arbitrary"),
    )(c, dqn, dkvn, dksum, qg, kvg, cs, sn)


def _mla_qkv_fwd(qn, kvn, kpe, cs, sn, wq, wk, wv, *, name, tt=256):
    T = qn.shape[0]
    tt = min(tt, T)
    H = MLA_HEADS

    def body(qn_ref, kn_ref, kpe_ref, cs_ref, sn_ref, wq_ref, wk_ref, wv_ref, q_ref, k_ref, v_ref):
        qn_v, kn_v, kpe_v, cs_v, sn_v = qn_ref[...], kn_ref[...], kpe_ref[...], cs_ref[...], sn_ref[...]
        for h in range(H):
            q = jnp.dot(qn_v, wq_ref[h], preferred_element_type=F32)
            q_ref[h] = (q * cs_v + _rot(q) * sn_v).astype(q_ref.dtype)
            k_ref[h] = (jnp.dot(kn_v, wk_ref[h], preferred_element_type=F32) + kpe_v).astype(k_ref.dtype)
            v_ref[h] = jnp.dot(kn_v, wv_ref[h], preferred_element_type=F32).astype(v_ref.dtype)

    row = lambda w: pl.BlockSpec((tt, w), lambda i: (i, 0))
    wsp = lambda k: pl.BlockSpec((H, k, LANES), lambda i: (0, 0, 0))
    hsp = pl.BlockSpec((H, tt, LANES), lambda i: (0, i, 0))
    sh = jax.ShapeDtypeStruct((H, T, LANES), CD)
    return pl.pallas_call(
        body, name=name, grid=(T // tt,),
        in_specs=[row(Q_LORA), row(KV_LORA), row(LANES), row(LANES), row(LANES),
                  wsp(Q_LORA), wsp(KV_LORA), wsp(KV_LORA)],
        out_specs=[hsp, hsp, hsp], out_shape=[sh, sh, sh],
        compiler_params=_params("parallel"),
    )(qn, kvn, kpe, cs, sn, wq, wk, wv)


def _mla_qkv_bwd(dq, dk, dv, qn, kvn, wq, wk, wv, *, name, tt=256):
    T = qn.shape[0]
    tt = min(tt, T)
    H = MLA_HEADS

    def body(dq_ref, dk_ref, dv_ref, qn_ref, kn_ref, wq_ref, wk_ref, wv_ref,
             dqn_ref, dkn_ref, dks_ref, dwq_ref, dwk_ref, dwv_ref):
        @pl.when(pl.program_id(0) == 0)
        def _():
            for r in (dwq_ref, dwk_ref, dwv_ref):
                r[...] = jnp.zeros_like(r)

        qn_v, kn_v = qn_ref[...], kn_ref[...]
        dqn = jnp.zeros((tt, Q_LORA), F32)
        dkn = jnp.zeros((tt, KV_LORA), F32)
        dks = jnp.zeros((tt, LANES), F32)
        for h in range(H):
            dqh, dkh, dvh = dq_ref[h], dk_ref[h], dv_ref[h]
            dqn = dqn + lax.dot_general(dqh, wq_ref[h], NT, preferred_element_type=F32)
            dkn = dkn + lax.dot_general(dkh, wk_ref[h], NT, preferred_element_type=F32)
            dkn = dkn + lax.dot_general(dvh, wv_ref[h], NT, preferred_element_type=F32)
            dks = dks + dkh.astype(F32)
            dwq_ref[h] += lax.dot_general(qn_v, dqh, TN, preferred_element_type=F32)
            dwk_ref[h] += lax.dot_general(kn_v, dkh, TN, preferred_element_type=F32)
            dwv_ref[h] += lax.dot_general(kn_v, dvh, TN, preferred_element_type=F32)
        dqn_ref[...] = dqn
        dkn_ref[...] = dkn
        dks_ref[...] = dks

    row = lambda w: pl.BlockSpec((tt, w), lambda i: (i, 0))
    wsp = lambda k: pl.BlockSpec((H, k, LANES), lambda i: (0, 0, 0))
    hsp = pl.BlockSpec((H, tt, LANES), lambda i: (0, i, 0))
    return pl.pallas_call(
        body, name=name, grid=(T // tt,),
        in_specs=[hsp, hsp, hsp, row(Q_LORA), row(KV_LORA), wsp(Q_LORA), wsp(KV_LORA), wsp(KV_LORA)],
        out_specs=[row(Q_LORA), row(KV_LORA), row(LANES), wsp(Q_LORA), wsp(KV_LORA), wsp(KV_LORA)],
        out_shape=[jax.ShapeDtypeStruct((T, Q_LORA), F32), jax.ShapeDtypeStruct((T, KV_LORA), F32),
                   jax.ShapeDtypeStruct((T, LANES), F32),
                   jax.ShapeDtypeStruct((H, Q_LORA, LANES), F32),
                   jax.ShapeDtypeStruct((H, KV_LORA, LANES), F32),
                   jax.ShapeDtypeStruct((H, KV_LORA, LANES), F32)],
        compiler_params=_params("arbitrary"),
    )(dq, dk, dv, qn, kvn, wq, wk, wv)


def _causal_scores(q, k, i, j, tq, tk):
    s = lax.dot_general(q, k, NT, preferred_element_type=F32) * MLA_SCALE
    row = i * tq + lax.broadcasted_iota(jnp.int32, (tq, tk), 0)
    col = j * tk + lax.broadcasted_iota(jnp.int32, (tq, tk), 1)
    return jnp.where(col <= row, s, NEG)


def _flash_fwd(q, k, v, *, name, tq=512):
    H, T, _ = q.shape
    tq = min(tq, T)
    tk = tq
    nq = T // tq

    def body(q_ref, k_ref, v_ref, o_ref, lse_ref, m_sc, l_sc, acc):
        i, j = pl.program_id(1), pl.program_id(2)

        @pl.when(j == 0)
        def _():
            m_sc[...] = jnp.full_like(m_sc, NEG)
            l_sc[...] = jnp.zeros_like(l_sc)
            acc[...] = jnp.zeros_like(acc)

        @pl.when(j <= i)
        def _():
            lane = lax.broadcasted_iota(jnp.int32, (tq, LANES), 1)
            alphas, pvs = [], []
            for h in range(2):
                s = _causal_scores(q_ref[h], k_ref[h], i, j, tq, tk)
                m_prev = m_sc[h]
                m_new = jnp.maximum(m_prev, jnp.max(s, axis=-1, keepdims=True))
                alpha = jnp.exp(m_prev - m_new)
                p = jnp.exp(s - m_new[:, :1])
                l_sc[h] = alpha * l_sc[h] + jnp.sum(p, axis=-1, keepdims=True)
                m_sc[h] = m_new
                alphas.append(alpha)
                pvs.append(jnp.dot(p.astype(CD), v_ref[h], preferred_element_type=F32))
            acc[...] = acc[...] * jnp.where(lane < V_HEAD, alphas[0], alphas[1]) + pvs[0] + pvs[1]

        @pl.when(j == i)
        def _():
            lane = lax.broadcasted_iota(jnp.int32, (tq, LANES), 1)
            o_ref[...] = (acc[...] / jnp.where(lane < V_HEAD, l_sc[0], l_sc[1])).astype(o_ref.dtype)
            for h in range(2):
                lse_ref[h] = m_sc[h] + jnp.log(l_sc[h])

    qsp = pl.BlockSpec((2, tq, LANES), lambda p, i, j: (p, i, 0))
    ksp = pl.BlockSpec((2, tk, LANES), lambda p, i, j: (p, jnp.minimum(j, i), 0))
    return pl.pallas_call(
        body, name=name, grid=(H // 2, nq, nq),
        in_specs=[qsp, ksp, ksp],
        out_specs=[pl.BlockSpec((tq, LANES), lambda p, i, j: (i, p)), qsp],
        out_shape=[jax.ShapeDtypeStruct((T, H * V_HEAD), CD), jax.ShapeDtypeStruct((H, T, LANES), F32)],
        scratch_shapes=[pltpu.VMEM((2, tq, LANES), F32), pltpu.VMEM((2, tq, LANES), F32),
                        pltpu.VMEM((tq, LANES), F32)],
        compiler_params=_params("parallel", "parallel", "arbitrary"),
    )(q, k, v)


def _flash_bwd_dq(q, k, v, o, do, lse, cs, sn, *, name, tq=512):
    H, T, _ = q.shape
    tq = min(tq, T)
    tk = tq
    nq = T // tq

    def body(q_ref, k_ref, v_ref, o_ref, do_ref, lse_ref, cs_ref, sn_ref, dq_ref, dl_ref, acc):
        i, j = pl.program_id(1), pl.program_id(2)

        @pl.when(j == 0)
        def _():
            acc[...] = jnp.zeros_like(acc)
            lane = lax.broadcasted_iota(jnp.int32, (tq, LANES), 1)
            prod = do_ref[...].astype(F32) * o_ref[...].astype(F32)
            d0 = jnp.sum(jnp.where(lane < V_HEAD, prod, 0.0), axis=-1, keepdims=True)
            d1 = jnp.sum(jnp.where(lane < V_HEAD, 0.0, prod), axis=-1, keepdims=True)
            dl_ref[0] = jnp.broadcast_to(d0, (tq, LANES))
            dl_ref[1] = jnp.broadcast_to(d1, (tq, LANES))

        @pl.when(j <= i)
        def _():
            do_v = do_ref[...]
            for h in range(2):
                s = _causal_scores(q_ref[h], k_ref[h], i, j, tq, tk)
                p = jnp.exp(s - lse_ref[h][:, :1])
                dp = lax.dot_general(do_v, v_ref[h], NT, preferred_element_type=F32)
                ds = (p * (dp - dl_ref[h][:, :1]) * MLA_SCALE).astype(CD)
                acc[h] += jnp.dot(ds, k_ref[h], preferred_element_type=F32)

        @pl.when(j == i)
        def _():
            for h in range(2):
                d = acc[h]
                dq_ref[h] = (d * cs_ref[...] - _rot(d * sn_ref[...])).astype(dq_ref.dtype)

    qsp = pl.BlockSpec((2, tq, LANES), lambda p, i, j: (p, i, 0))
    ksp = pl.BlockSpec((2, tk, LANES), lambda p, i, j: (p, jnp.minimum(j, i), 0))
    osp = pl.BlockSpec((tq, LANES), lambda p, i, j: (i, p))
    tsp = pl.BlockSpec((tq, LANES), lambda p, i, j: (i, 0))
    return pl.pallas_call(
        body, name=name, grid=(H // 2, nq, nq),
        in_specs=[qsp, ksp, ksp, osp, osp, qsp, tsp, tsp],
        out_specs=[qsp, qsp],
        out_shape=[jax.ShapeDtypeStruct((H, T, LANES), CD), jax.ShapeDtypeStruct((H, T, LANES), F32)],
        scratch_shapes=[pltpu.VMEM((2, tq, LANES), F32)],
        compiler_params=_params("parallel", "parallel", "arbitrary"),
    )(q, k, v, o, do, lse, cs, sn)


def _flash_bwd_dkv(q, k, v, do, lse, delta, *, name, tq=512):
    H, T, _ = q.shape
    tq = min(tq, T)
    tk = tq
    nq = T // tq

    def body(q_ref, k_ref, v_ref, do_ref, lse_ref, dl_ref, dk_ref, dv_ref, dk_acc, dv_acc):
        j, i = pl.program_id(1), pl.program_id(2)

        @pl.when(i == 0)
        def _():
            dk_acc[...] = jnp.zeros_like(dk_acc)
            dv_acc[...] = jnp.zeros_like(dv_acc)

        @pl.when(i >= j)
        def _():
            do_v = do_ref[...]
            for h in range(2):
                s = _causal_scores(q_ref[h], k_ref[h], i, j, tq, tk)
                p = jnp.exp(s - lse_ref[h][:, :1])
                dv_acc[h] += lax.dot_general(p.astype(CD), do_v, TN, preferred_element_type=F32)
                dp = lax.dot_general(do_v, v_ref[h], NT, preferred_element_type=F32)
                ds = (p * (dp - dl_ref[h][:, :1]) * MLA_SCALE).astype(CD)
                dk_acc[h] += lax.dot_general(ds, q_ref[h], TN, preferred_element_type=F32)

        @pl.when(i == nq - 1)
        def _():
            lane = lax.broadcasted_iota(jnp.int32, (tk, LANES), 1)
            dk_ref[...] = dk_acc[...].astype(dk_ref.dtype)
            dv_ref[0] = jnp.where(lane < V_HEAD, dv_acc[0], 0.0).astype(dv_ref.dtype)
            dv_ref[1] = jnp.where(lane < V_HEAD, 0.0, dv_acc[1]).astype(dv_ref.dtype)

    qsp = pl.BlockSpec((2, tq, LANES), lambda p, j, i: (p, jnp.maximum(i, j), 0))
    ksp = pl.BlockSpec((2, tk, LANES), lambda p, j, i: (p, j, 0))
    osp = pl.BlockSpec((tq, LANES), lambda p, j, i: (jnp.maximum(i, j), p))
    sh = jax.ShapeDtypeStruct((H, T, LANES), CD)
    return pl.pallas_call(
        body, name=name, grid=(H // 2, nq, nq),
        in_specs=[qsp, ksp, ksp, osp, qsp, qsp],
        out_specs=[ksp, ksp], out_shape=[sh, sh],
        scratch_shapes=[pltpu.VMEM((2, tk, LANES), F32), pltpu.VMEM((2, tk, LANES), F32)],
        compiler_params=_params("parallel", "parallel", "arbitrary"),
    )(q, k, v, do, lse, delta)


def _loss_head(x, g, target, *, name, tt=512):
    T, D = x.shape
    tt = min(tt, T)

    def body(x_ref, g_ref, t_ref, dx_ref, dg_ref, loss_ref):
        @pl.when(pl.program_id(0) == 0)
        def _():
            dg_ref[...] = jnp.zeros_like(dg_ref)
            loss_ref[...] = jnp.zeros_like(loss_ref)

        xv, gv = x_ref[...], g_ref[...]
        r = lax.rsqrt(jnp.mean(xv * xv, axis=-1, keepdims=True) + EPS)
        err = xv * r * gv - t_ref[...]
        tok = jnp.mean(err * err, axis=-1, keepdims=True)
        loss_ref[...] += 0.5 * jnp.sum(tok, axis=0, keepdims=True)
        dx, dg_rows = _rms_bwd(xv, gv, err * (1.0 / D))
        dx_ref[...] = dx
        dg_ref[...] += jnp.sum(dg_rows, axis=0, keepdims=True)

    return pl.pallas_call(
        body, name=name, grid=(T // tt,),
        in_specs=[pl.BlockSpec((tt, D), lambda i: (i, 0)), pl.BlockSpec((1, D), lambda i: (0, 0)),
                  pl.BlockSpec((tt, D), lambda i: (i, 0))],
        out_specs=[pl.BlockSpec((tt, D), lambda i: (i, 0)), pl.BlockSpec((1, D), lambda i: (0, 0)),
                   pl.BlockSpec((1, LANES), lambda i: (0, 0))],
        out_shape=[jax.ShapeDtypeStruct((T, D), F32), jax.ShapeDtypeStruct((1, D), F32),
                   jax.ShapeDtypeStruct((1, LANES), F32)],
        compiler_params=_params("arbitrary"),
    )(x, g, target)


def _rope_tables(positions):
    inv = 1.0 / (ROPE_THETA ** (jnp.arange(0, QK_ROPE, 2, dtype=F32) / QK_ROPE))
    ang = positions.astype(F32)[:, None] * inv
    c, s = jnp.cos(ang), jnp.sin(ang)
    T = positions.shape[0]
    cs = jnp.concatenate([jnp.ones((T, QK_NOPE), F32), c, c, jnp.zeros((T, LANES - QK_NOPE - QK_ROPE), F32)], 1)
    sn = jnp.concatenate([jnp.zeros((T, QK_NOPE), F32), s, s, jnp.zeros((T, LANES - QK_NOPE - QK_ROPE), F32)], 1)
    return cs, sn


def _pad_rows(w, rows):
    return jnp.concatenate([w, jnp.zeros((rows - w.shape[0],) + w.shape[1:], w.dtype)], 0)


def _mla_weights(w_dq_dkv, w_uq, w_ukv):
    K = w_dq_dkv.shape[0]
    z = lambda n: jnp.zeros((K, n), w_dq_dkv.dtype)
    wc = jnp.concatenate([w_dq_dkv[:, :Q_LORA + KV_LORA], z(QK_NOPE), w_dq_dkv[:, Q_LORA + KV_LORA:],
                          z(LANES - QK_NOPE - QK_ROPE)], 1)
    wq = w_uq.reshape(Q_LORA, MLA_HEADS, QK_NOPE + QK_ROPE).transpose(1, 0, 2)
    wq = jnp.concatenate([wq, jnp.zeros((MLA_HEADS, Q_LORA, LANES - QK_NOPE - QK_ROPE), wq.dtype)], 2)
    wkv = w_ukv.reshape(KV_LORA, MLA_HEADS, QK_NOPE + V_HEAD).transpose(1, 0, 2)
    zero = jnp.zeros_like(wkv[:, :, :QK_NOPE])
    wk = jnp.concatenate([wkv[:, :, :QK_NOPE], zero], 2)
    wv_lo = jnp.concatenate([wkv[:, :, QK_NOPE:], zero], 2)
    wv_hi = jnp.concatenate([zero, wkv[:, :, QK_NOPE:]], 2)
    odd = (jnp.arange(MLA_HEADS) % 2 == 1)[:, None, None]
    wv = jnp.where(odd, wv_hi, wv_lo)
    return wc, wq, wk, wv


def _mla_weight_grads(dwc, dwq, dwk, dwv):
    d_dq = jnp.concatenate([dwc[:, :Q_LORA + KV_LORA],
                            dwc[:, Q_LORA + KV_LORA + QK_NOPE:Q_LORA + KV_LORA + QK_NOPE + QK_ROPE]], 1)
    d_uq = dwq[:, :, :QK_NOPE + QK_ROPE].transpose(1, 0, 2).reshape(Q_LORA, MLA_HEADS * (QK_NOPE + QK_ROPE))
    odd = (jnp.arange(MLA_HEADS) % 2 == 1)[:, None, None]
    dv = jnp.where(odd, dwv[:, :, V_HEAD:], dwv[:, :, :V_HEAD])
    d_ukv = jnp.concatenate([dwk[:, :, :QK_NOPE], dv], 2).transpose(1, 0, 2).reshape(
        KV_LORA, MLA_HEADS * (QK_NOPE + V_HEAD))
    return d_dq, d_uq, d_ukv


def _local_step(x, mem, positions, target, W):
    G = {}
    row = lambda v: v.reshape(1, -1)
    cs, sn = _rope_tables(positions)
    saved = []
    for l in range(DEPTH):
        L = f"l{l}"
        s = {"x0": x}
        if l % 2 == 0:
            e = l // 2
            s["z"], s["h"] = _nmm(x, row(W["norm_mix_g"][l]), W["pc_w_in"][e], name=f"{L}_mix_in", out_dtype=F32)
            s["dw_w"] = _pad_rows(W["conv_dw_w"][e], CONV_K + 1)
            s["mix_p"] = (W["pool_w"][e], row(W["pool_scale"][e]), s["dw_w"], row(W["conv_dw_b"][e]),
                          row(W["conv_ln_g"][e]), row(W["conv_ln_b"][e]))
            s["ycat"] = _mixer_fwd(s["z"], *s["mix_p"], name=f"{L}_mix_mid")
            x = _mm_res(s["ycat"], W["pc_w_out"][e], x, name=f"{L}_mix_out")
        else:
            o = l // 2
            wc, wq, wk, wv = _mla_weights(W["mla_w_dq_dkv"][o], W["mla_w_uq"][o], W["mla_w_ukv"][o])
            s["mla_w"] = (wc, wq, wk, wv)
            s["c"], s["h"] = _nmm(x, row(W["norm_mix_g"][l]), wc, name=f"{L}_mla_down", out_dtype=F32)
            s["qg"], s["kvg"] = row(W["mla_q_norm_g"][o]), row(W["mla_kv_norm_g"][o])
            s["qn"], s["kvn"], kpe = _mla_mid_fwd(s["c"], s["qg"], s["kvg"], cs, sn, name=f"{L}_mla_mid")
            s["q"], s["k"], s["v"] = _mla_qkv_fwd(s["qn"], s["kvn"], kpe, cs, sn, wq, wk, wv, name=f"{L}_mla_qkv")
            s["o"], s["lse"] = _flash_fwd(s["q"], s["k"], s["v"], name=f"{L}_mla_attn")
            x = _mm_res(s["o"], W["mla_w_o"][o], x, name=f"{L}_mla_out")
        s["x1"] = x
        s["xq"], s["hx"] = _nmm(x, row(W["norm_xa_g"][l]), W["xa_wq"][l], name=f"{L}_xa_q", out_dtype=CD)
        s["xkv"], s["hm"] = _nmm(mem, row(W["norm_mem_g"][l]), W["xa_wkv"][l], name=f"{L}_xa_kv", out_dtype=CD)
        s["xo"] = _xattn_fwd(s["xq"], s["xkv"], name=f"{L}_xa_attn")
        x = _mm_res(s["xo"], W["xa_wo"][l], x, name=f"{L}_xa_out")
        s["x2"] = x
        s["up"], s["hf"] = _nmm(x, row(W["norm_ffn_g"][l]), W["ffn_w_up"][l], name=f"{L}_ffn_up", out_dtype=CD)
        s["cw"], s["cb"] = _pad_rows(W["ffn_conv_w"][l], 8), row(W["ffn_conv_b"][l])
        s["act"] = _ffn_mid_fwd(s["up"], s["cw"], s["cb"], name=f"{L}_ffn_mid")
        x = _mm_res(s["act"], W["ffn_w_down"][l], x, name=f"{L}_ffn_down")
        saved.append(s)
    dx, G["final_norm_g"], loss = _loss_head(x, row(W["final_norm_g"]), target, name="loss_head")
    G["final_norm_g"] = G["final_norm_g"].reshape(-1)

    per_layer = {}

    def put(name, l, val):
        per_layer.setdefault(name, {})[l] = val

    for l in reversed(range(DEPTH)):
        L = f"l{l}"
        s = saved[l]
        put("ffn_w_down", l, _mm_tn(s["act"], dx, name=f"{L}_ffn_down_dw"))
        dact = _mm_nt(dx, W["ffn_w_down"][l], name=f"{L}_ffn_down_dx", out_dtype=CD)
        dup, dcw, dcb = _ffn_mid_bwd(s["up"], dact, s["cw"], s["cb"], name=f"{L}_ffn_mid_bwd")
        put("ffn_conv_w", l, dcw[:FFN_K])
        put("ffn_conv_b", l, dcb[0])
        put("ffn_w_up", l, _mm_tn(s["hf"], dup, name=f"{L}_ffn_up_dw", tn_target=1408))
        dx, dg = _mm_nt_normbwd(dup, W["ffn_w_up"][l], s["x2"], row(W["norm_ffn_g"][l]), dx, name=f"{L}_ffn_up_dx")
        put("norm_ffn_g", l, dg[0])
        put("xa_wo", l, _mm_tn(s["xo"], dx, name=f"{L}_xa_out_dw"))
        do = _mm_nt(dx, W["xa_wo"][l], name=f"{L}_xa_out_dx", out_dtype=CD)
        dq, dkv = _xattn_bwd(s["xq"], s["xkv"], do, name=f"{L}_xa_attn_bwd")
        put("xa_wq", l, _mm_tn(s["hx"], dq, name=f"{L}_xa_q_dw"))
        dx, dg = _mm_nt_normbwd(dq, W["xa_wq"][l], s["x1"], row(W["norm_xa_g"][l]), dx, name=f"{L}_xa_q_dx")
        put("norm_xa_g", l, dg[0])
        put("xa_wkv", l, _mm_tn(s["hm"], dkv, name=f"{L}_xa_kv_dw", tt=MEM_LEN))
        _, dg = _mm_nt_normbwd(dkv, W["xa_wkv"][l], mem, row(W["norm_mem_g"][l]), jnp.zeros_like(mem),
                               name=f"{L}_xa_kv_dx", tm=MEM_LEN)
        put("norm_mem_g", l, dg[0])
        if l % 2 == 0:
            e = l // 2
            put("pc_w_out", e, _mm_tn(s["ycat"], dx, name=f"{L}_mix_out_dw"))
            dy = _mm_nt(dx, W["pc_w_out"][e], name=f"{L}_mix_out_dx", out_dtype=F32)
            dz, dpw, dps, ddw, ddb, dlg, dlb = _mixer_bwd(s["z"], dy, *s["mix_p"], name=f"{L}_mix_mid_bwd")
            put("pool_w", e, dpw)
            put("pool_scale", e, dps[0])
            put("conv_dw_w", e, ddw[:CONV_K])
            put("conv_dw_b", e, ddb[0])
            put("conv_ln_g", e, dlg[0])
            put("conv_ln_b", e, dlb[0])
            put("pc_w_in", e, _mm_tn(s["h"], dz, name=f"{L}_mix_in_dw"))
            dx, dg = _mm_nt_normbwd(dz, W["pc_w_in"][e], s["x0"], row(W["norm_mix_g"][l]), dx, name=f"{L}_mix_in_dx")
        else:
            o = l // 2
            wc, wq, wk, wv = s["mla_w"]
            put("mla_w_o", o, _mm_tn(s["o"], dx, name=f"{L}_mla_out_dw"))
            do = _mm_nt(dx, W["mla_w_o"][o], name=f"{L}_mla_out_dx", out_dtype=CD)
            dq, delta = _flash_bwd_dq(s["q"], s["k"], s["v"], s["o"], do, s["lse"], cs, sn, name=f"{L}_mla_attn_dq")
            dk, dv = _flash_bwd_dkv(s["q"], s["k"], s["v"], do, s["lse"], delta, name=f"{L}_mla_attn_dkv")
            dqn, dkvn, dks, dwq, dwk, dwv = _mla_qkv_bwd(dq, dk, dv, s["qn"], s["kvn"], wq, wk, wv,
                                                         name=f"{L}_mla_qkv_bwd")
            dc, dqg, dkg = _mla_mid_bwd(s["c"], dqn, dkvn, dks, s["qg"], s["kvg"], cs, sn, name=f"{L}_mla_mid_bwd")
            put("mla_q_norm_g", o, dqg[0])
            put("mla_kv_norm_g", o, dkg[0])
            dwc = _mm_tn(s["h"], dc, name=f"{L}_mla_down_dw")
            d_dq, d_uq, d_ukv = _mla_weight_grads(dwc, dwq, dwk, dwv)
            put("mla_w_dq_dkv", o, d_dq)
            put("mla_w_uq", o, d_uq)
            put("mla_w_ukv", o, d_ukv)
            dx, dg = _mm_nt_normbwd(dc, wc, s["x0"], row(W["norm_mix_g"][l]), dx, name=f"{L}_mla_down_dx",
                                    tk_target=768)
        put("norm_mix_g", l, dg[0])
    for name, d in per_layer.items():
        G[name] = jnp.stack([d[i] for i in sorted(d)], 0)
    return loss, dx, G


_ANY = pl.BlockSpec(memory_space=pl.ANY)


def _all_gather(x, *, name):
    R, C = x.shape

    def body(x_ref, out_ref, send_sems, recv_sems, local_sem):
        mx, my, mc = lax.axis_index("x"), lax.axis_index("y"), lax.axis_index("c")
        me, sibling = (mx, my, mc), (mx, my, 1 - mc)
        chips = [(1 - mx, my), (mx, 1 - my), (1 - mx, 1 - my)]

        def slot(px, py, pc):
            return out_ref.at[4 * px + 2 * py + pc]

        def copy(k, block, to, src=None):
            return pltpu.make_async_remote_copy(
                src_ref=slot(*block) if src is None else src, dst_ref=slot(*block),
                send_sem=send_sems.at[k], recv_sem=recv_sems.at[k], device_id=to, device_id_type=MESH)

        mine = pltpu.make_async_copy(x_ref, slot(*me), local_sem)
        mine.start()
        first = [copy(0, me, sibling, src=x_ref)]
        first += [copy(1 + j, me, (*chip, mc), src=x_ref) for j, chip in enumerate(chips)]
        for cp in first:
            cp.start()
        passed = [copy(4 + j, (*chip, mc), sibling) for j, chip in enumerate(chips)]
        for j, chip in enumerate(chips):
            copy(1 + j, (*chip, mc), me).wait_recv()
            passed[j].start()
        copy(0, sibling, me).wait_recv()
        for j, chip in enumerate(chips):
            copy(4 + j, (*chip, 1 - mc), me).wait_recv()
        for cp in first + passed:
            cp.wait_send()
        mine.wait()

    return pl.pallas_call(
        body, name=name, in_specs=[_ANY], out_specs=_ANY,
        out_shape=jax.ShapeDtypeStruct((N_DEV, R, C), x.dtype),
        scratch_shapes=[pltpu.SemaphoreType.DMA((7,)), pltpu.SemaphoreType.DMA((7,)), pltpu.SemaphoreType.DMA],
    )(x)


def _all_to_all(p, *, name):
    _, R, C = p.shape

    def body(p_ref, out_ref, send_sems, recv_sems, local_sem):
        mx, my, mc = lax.axis_index("x"), lax.axis_index("y"), lax.axis_index("c")
        me = 4 * mx + 2 * my + mc
        mine = pltpu.make_async_copy(p_ref.at[me], out_ref.at[me], local_sem)
        mine.start()
        copies = []
        for k in range(1, N_DEV):
            px, py, pc = mx ^ ((k >> 2) & 1), my ^ ((k >> 1) & 1), mc ^ (k & 1)
            copies.append(pltpu.make_async_remote_copy(
                src_ref=p_ref.at[4 * px + 2 * py + pc], dst_ref=out_ref.at[me],
                send_sem=send_sems.at[k - 1], recv_sem=recv_sems.at[k - 1],
                device_id=(px, py, pc), device_id_type=MESH))
        for cp in copies:
            cp.start()
        for cp in copies:
            cp.wait()
        mine.wait()

    return pl.pallas_call(
        body, name=name, in_specs=[_ANY], out_specs=_ANY,
        out_shape=jax.ShapeDtypeStruct(p.shape, p.dtype),
        scratch_shapes=[pltpu.SemaphoreType.DMA((7,)), pltpu.SemaphoreType.DMA((7,)), pltpu.SemaphoreType.DMA],
    )(p)


def _row_tile(R):
    for t in (2048, 1024, 512, 256, 128, 64, 32, 16, 8):
        if R % t == 0:
            return t
    raise ValueError(R)


def _sum_slots(gs, *, name):
    S, R, C = gs.shape
    tr = _row_tile(R)

    def body(g_ref, o_ref):
        g = g_ref[0].astype(F32)
        for s in range(1, S):
            g = g + g_ref[s].astype(F32)
        o_ref[...] = g

    return pl.pallas_call(
        body, name=name, grid=(R // tr,),
        in_specs=[pl.BlockSpec((S, tr, C), lambda i: (0, i, 0))],
        out_specs=pl.BlockSpec((tr, C), lambda i: (i, 0)),
        out_shape=jax.ShapeDtypeStruct((R, C), F32),
        compiler_params=_params("parallel"),
    )(gs)


def _adamw(gs, w, m, v, *, name):
    S, R, C = gs.shape
    tr = _row_tile(R)

    def body(g_ref, w_ref, m_ref, v_ref, g_out, d_out, m_out, v_out):
        g = g_ref[0].astype(F32)
        for s in range(1, S):
            g = g + g_ref[s].astype(F32)
        m_new = ADAM_B1 * m_ref[...] + (1.0 - ADAM_B1) * g
        v_new = ADAM_B2 * v_ref[...] + (1.0 - ADAM_B2) * (g * g)
        m_hat = m_new / (1.0 - ADAM_B1 ** ADAM_STEP)
        v_hat = v_new / (1.0 - ADAM_B2 ** ADAM_STEP)
        g_out[...] = g
        d_out[...] = -ADAM_LR * (m_hat / (jnp.sqrt(v_hat) + ADAM_EPS) + ADAM_WD * w_ref[...])
        m_out[...] = m_new
        v_out[...] = v_new

    blk = pl.BlockSpec((tr, C), lambda i: (i, 0))
    sh = jax.ShapeDtypeStruct((R, C), F32)
    return pl.pallas_call(
        body, name=name, grid=(R // tr,),
        in_specs=[pl.BlockSpec((S, tr, C), lambda i: (0, i, 0)), blk, blk, blk],
        out_specs=[blk, blk, blk, blk], out_shape=[sh, sh, sh, sh],
        compiler_params=_params("parallel"),
    )(gs, w, m, v)


PIECE = 16 * LANES


def _pack(arrs, dtype, lead, row_mult):
    lead_shape = arrs[0].shape[:lead]
    parts, meta, off = [], [], 0
    for a in arrs:
        size = math.prod(a.shape[lead:])
        padded = -(-size // PIECE) * PIECE
        flat = a.astype(dtype).reshape(lead_shape + (size,))
        if padded != size:
            flat = jnp.concatenate([flat, jnp.zeros(lead_shape + (padded - size,), dtype)], -1)
        parts.append(flat)
        meta.append((off, size, a.shape[lead:]))
        off += padded
    total = -(-off // (row_mult * LANES)) * (row_mult * LANES)
    if total != off:
        parts.append(jnp.zeros(lead_shape + (total - off,), dtype))
    return jnp.concatenate(parts, -1).reshape(lead_shape + (total // LANES, LANES)), meta


def _unpack(packed, meta, lead):
    lead_shape = packed.shape[:lead]
    flat = packed.reshape(lead_shape + (-1,))
    return [flat[..., off:off + size].reshape(lead_shape + shape) for off, size, shape in meta]


ARG_NAMES = ['x', 'mem', 'positions', 'norm_mix_g', 'norm_xa_g', 'norm_mem_g', 'xa_wq', 'xa_wkv', 'xa_wo', 'norm_ffn_g', 'ffn_w_up', 'ffn_conv_w', 'ffn_conv_b', 'ffn_w_down', 'pc_w_in', 'pool_w', 'pool_scale', 'conv_dw_w', 'conv_dw_b', 'conv_ln_g', 'conv_ln_b', 'pc_w_out', 'mla_w_dq_dkv', 'mla_q_norm_g', 'mla_w_uq', 'mla_kv_norm_g', 'mla_w_ukv', 'mla_w_o', 'final_norm_g', 'loss_target']
WEIGHTS = ARG_NAMES[3:29]
BIG = {'xa_wq': 1, 'xa_wkv': 2, 'xa_wo': 1, 'ffn_w_up': 2, 'ffn_w_down': 1, 'pc_w_in': 2, 'pc_w_out': 1,
       'mla_w_dq_dkv': 1, 'mla_w_uq': 2, 'mla_w_ukv': 2, 'mla_w_o': 1}
SMALL_SHARDED = {'ffn_conv_w': 2, 'conv_dw_w': 2, 'mla_q_norm_g': 1, 'mla_kv_norm_g': 1}
REPLICATED = [n for n in WEIGHTS if n not in BIG and n not in SMALL_SHARDED]


def _from_slots(g, axis):
    t = jnp.moveaxis(g, 0, axis)
    return t.reshape(t.shape[:axis] + (t.shape[axis] * t.shape[axis + 1],) + t.shape[axis + 2:])


def _to_slots(full, axis):
    n = full.shape[axis] // N_DEV
    t = full.reshape(full.shape[:axis] + (N_DEV, n) + full.shape[axis + 1:])
    return jnp.moveaxis(t, axis, 0)


def kernel(x, mem, positions, norm_mix_g, norm_xa_g, norm_mem_g, xa_wq, xa_wkv, xa_wo, norm_ffn_g, ffn_w_up, ffn_conv_w, ffn_conv_b, ffn_w_down, pc_w_in, pool_w, pool_scale, conv_dw_w, conv_dw_b, conv_ln_g, conv_ln_b, pc_w_out, mla_w_dq_dkv, mla_q_norm_g, mla_w_uq, mla_kv_norm_g, mla_w_ukv, mla_w_o, final_norm_g, loss_target, m_norm_mix_g, m_norm_xa_g, m_norm_mem_g, m_xa_wq, m_xa_wkv, m_xa_wo, m_norm_ffn_g, m_ffn_w_up, m_ffn_conv_w, m_ffn_conv_b, m_ffn_w_down, m_pc_w_in, m_pool_w, m_pool_scale, m_conv_dw_w, m_conv_dw_b, m_conv_ln_g, m_conv_ln_b, m_pc_w_out, m_mla_w_dq_dkv, m_mla_q_norm_g, m_mla_w_uq, m_mla_kv_norm_g, m_mla_w_ukv, m_mla_w_o, m_final_norm_g, v_norm_mix_g, v_norm_xa_g, v_norm_mem_g, v_xa_wq, v_xa_wkv, v_xa_wo, v_norm_ffn_g, v_ffn_w_up, v_ffn_conv_w, v_ffn_conv_b, v_ffn_w_down, v_pc_w_in, v_pool_w, v_pool_scale, v_conv_dw_w, v_conv_dw_b, v_conv_ln_g, v_conv_ln_b, v_pc_w_out, v_mla_w_dq_dkv, v_mla_q_norm_g, v_mla_w_uq, v_mla_kv_norm_g, v_mla_w_ukv, v_mla_w_o, v_final_norm_g):
    args = (x, mem, positions, norm_mix_g, norm_xa_g, norm_mem_g, xa_wq, xa_wkv, xa_wo, norm_ffn_g, ffn_w_up, ffn_conv_w, ffn_conv_b, ffn_w_down, pc_w_in, pool_w, pool_scale, conv_dw_w, conv_dw_b, conv_ln_g, conv_ln_b, pc_w_out, mla_w_dq_dkv, mla_q_norm_g, mla_w_uq, mla_kv_norm_g, mla_w_ukv, mla_w_o, final_norm_g, loss_target)
    a = dict(zip(ARG_NAMES, args))
    mom = dict(zip(WEIGHTS, (m_norm_mix_g, m_norm_xa_g, m_norm_mem_g, m_xa_wq, m_xa_wkv, m_xa_wo, m_norm_ffn_g, m_ffn_w_up, m_ffn_conv_w, m_ffn_conv_b, m_ffn_w_down, m_pc_w_in, m_pool_w, m_pool_scale, m_conv_dw_w, m_conv_dw_b, m_conv_ln_g, m_conv_ln_b, m_pc_w_out, m_mla_w_dq_dkv, m_mla_q_norm_g, m_mla_w_uq, m_mla_kv_norm_g, m_mla_w_ukv, m_mla_w_o, m_final_norm_g)))
    var = dict(zip(WEIGHTS, (v_norm_mix_g, v_norm_xa_g, v_norm_mem_g, v_xa_wq, v_xa_wkv, v_xa_wo, v_norm_ffn_g, v_ffn_w_up, v_ffn_conv_w, v_ffn_conv_b, v_ffn_w_down, v_pc_w_in, v_pool_w, v_pool_scale, v_conv_dw_w, v_conv_dw_b, v_conv_ln_g, v_conv_ln_b, v_pc_w_out, v_mla_w_dq_dkv, v_mla_q_norm_g, v_mla_w_uq, v_mla_kv_norm_g, v_mla_w_ukv, v_mla_w_o, v_final_norm_g)))
    me = 4 * lax.axis_index("x") + 2 * lax.axis_index("y") + lax.axis_index("c")

    big_pack, big_meta = _pack([a[n] for n in BIG], CD, 0, 2048)
    big_all = _unpack(_all_gather(big_pack, name="gather_weights"), big_meta, 1)
    sm_pack, sm_meta = _pack([a[n] for n in SMALL_SHARDED], F32, 0, 8)
    sm_all = _unpack(_all_gather(sm_pack, name="gather_small"), sm_meta, 1)
    W = {n: a[n] for n in REPLICATED}
    for (n, ax), g in zip(BIG.items(), big_all):
        W[n] = _from_slots(g, ax)
    for (n, ax), g in zip(SMALL_SHARDED.items(), sm_all):
        W[n] = _from_slots(g, ax)

    loss, dx, G = _local_step(x[0], mem[0], positions[0], loss_target[0], W)

    gpack, gmeta = _pack([_to_slots(G[n], ax) for n, ax in BIG.items()], CD, 1, 2048)
    recv = _all_to_all(gpack, name="scatter_grads")
    wp, _ = _pack([a[n] for n in BIG], F32, 0, 2048)
    mp, _ = _pack([mom[n] for n in BIG], F32, 0, 2048)
    vp, _ = _pack([var[n] for n in BIG], F32, 0, 2048)
    out = {}
    res = [_unpack(r, gmeta, 0) for r in _adamw(recv, wp, mp, vp, name="adamw_big")]
    for i, n in enumerate(BIG):
        out[n] = tuple(r[i] for r in res)

    small_names = REPLICATED + list(SMALL_SHARDED)
    spack, smeta = _pack([G[n] for n in small_names] + [loss], F32, 0, 256)
    stot = _unpack(_sum_slots(_all_gather(spack, name="gather_small_grads"), name="sum_small_grads"), smeta, 0)
    loss_total = stot[-1][0, 0]
    gsm = dict(zip(small_names, stot[:-1]))
    for n, ax in SMALL_SHARDED.items():
        width = a[n].shape[ax]
        gsm[n] = lax.dynamic_slice_in_dim(gsm[n], me * width, width, ax)
    g1, meta1 = _pack([gsm[n] for n in small_names], F32, 0, 256)
    w1, _ = _pack([a[n] for n in small_names], F32, 0, 256)
    m1, _ = _pack([mom[n] for n in small_names], F32, 0, 256)
    v1, _ = _pack([var[n] for n in small_names], F32, 0, 256)
    res = [_unpack(r, meta1, 0) for r in _adamw(g1[None], w1, m1, v1, name="adamw_small")]
    for i, n in enumerate(small_names):
        out[n] = tuple(r[i] for r in res)

    return (loss_total, dx[None],
            *[out[n][0] for n in WEIGHTS], *[out[n][1] for n in WEIGHTS],
            *[out[n][2] for n in WEIGHTS], *[out[n][3] for n in WEIGHTS])
```

```python
import functools
import math

import jax
import jax.numpy as jnp
from jax import lax
from jax.experimental import pallas as pl
from jax.experimental.pallas import tpu as pltpu

F32 = jnp.float32
CD = jnp.bfloat16
EPS = 1e-6
NEG = -1e30
N_DEV = 8
LANES = 128
HALO = 32

D_MODEL = 1024
DEPTH = 4
XA_HEADS = 4
XA_DH = 256
MEM_LEN = 256
POOL_WINDOWS = (2, 4, 8, 16)
CONV_K = 31
FFN_K = 3
D_FF = 2816
MLA_HEADS = 16
QK_NOPE = 64
QK_ROPE = 32
V_HEAD = 64
Q_LORA = 384
KV_LORA = 256
ROPE_THETA = 10000.0
MLA_SCALE = 1.0 / math.sqrt(QK_NOPE + QK_ROPE)
XA_SCALE = XA_DH ** -0.5

ADAM_LR = 0.001
ADAM_B1 = 0.9
ADAM_B2 = 0.999
ADAM_EPS = 1e-08
ADAM_WD = 0.01
ADAM_STEP = 10

NT = (((1,), (1,)), ((), ()))
TN = (((0,), (0,)), ((), ()))
MESH = pl.DeviceIdType.MESH


def _tile(n, target):
    if n <= target:
        return n
    best = None
    for t in range(LANES, target + 1, LANES):
        if n % t == 0:
            best = t
    assert best is not None, (n, target)
    return best


def _params(*sem):
    return pltpu.CompilerParams(dimension_semantics=sem)


def _sigmoid(v):
    return 1.0 / (1.0 + jnp.exp(-v))


def _rms_bwd(x, gain, dh):
    r = lax.rsqrt(jnp.mean(x * x, axis=-1, keepdims=True) + EPS)
    xhat = x * r
    dxhat = dh * gain
    dx = r * (dxhat - xhat * jnp.mean(dxhat * xhat, axis=-1, keepdims=True))
    return dx, dh * xhat


def _nmm(x, g, w, *, name, out_dtype, tm=512, tn_target=1024):
    M, K = x.shape
    N = w.shape[1]
    tm = min(tm, M)
    tn = _tile(N, tn_target)

    def body(x_ref, g_ref, w_ref, z_ref, h_ref):
        @pl.when(pl.program_id(1) == 0)
        def _():
            xf = x_ref[...]
            r = lax.rsqrt(jnp.mean(xf * xf, axis=-1, keepdims=True) + EPS)
            h_ref[...] = (xf * r * g_ref[...]).astype(h_ref.dtype)

        z_ref[...] = jnp.dot(h_ref[...], w_ref[...], preferred_element_type=F32).astype(z_ref.dtype)

    return pl.pallas_call(
        body, name=name, grid=(M // tm, N // tn),
        in_specs=[pl.BlockSpec((tm, K), lambda i, j: (i, 0)),
                  pl.BlockSpec((1, K), lambda i, j: (0, 0)),
                  pl.BlockSpec((K, tn), lambda i, j: (0, j))],
        out_specs=[pl.BlockSpec((tm, tn), lambda i, j: (i, j)),
                   pl.BlockSpec((tm, K), lambda i, j: (i, 0))],
        out_shape=[jax.ShapeDtypeStruct((M, N), out_dtype), jax.ShapeDtypeStruct((M, K), CD)],
        compiler_params=_params("parallel", "arbitrary"),
    )(x, g, w)


def _mm_res(a, w, res, *, name, tm=512, tn_target=1024):
    M, K = a.shape
    N = w.shape[1]
    tm = min(tm, M)
    tn = _tile(N, tn_target)

    def body(a_ref, w_ref, r_ref, o_ref):
        o_ref[...] = r_ref[...] + jnp.dot(a_ref[...].astype(CD), w_ref[...], preferred_element_type=F32)

    return pl.pallas_call(
        body, name=name, grid=(M // tm, N // tn),
        in_specs=[pl.BlockSpec((tm, K), lambda i, j: (i, 0)),
                  pl.BlockSpec((K, tn), lambda i, j: (0, j)),
                  pl.BlockSpec((tm, tn), lambda i, j: (i, j))],
        out_specs=pl.BlockSpec((tm, tn), lambda i, j: (i, j)),
        out_shape=jax.ShapeDtypeStruct((M, N), F32),
        compiler_params=_params("parallel", "arbitrary"),
    )(a, w, res)


def _mm_nt(a, w, *, name, out_dtype, tm=512, tn_target=1024):
    M, K = a.shape
    N = w.shape[0]
    tm = min(tm, M)
    tn = _tile(N, tn_target)

    def body(a_ref, w_ref, o_ref):
        o_ref[...] = lax.dot_general(a_ref[...].astype(CD), w_ref[...], NT,
                                     preferred_element_type=F32).astype(o_ref.dtype)

    return pl.pallas_call(
        body, name=name, grid=(M // tm, N // tn),
        in_specs=[pl.BlockSpec((tm, K), lambda i, j: (i, 0)),
                  pl.BlockSpec((tn, K), lambda i, j: (j, 0))],
        out_specs=pl.BlockSpec((tm, tn), lambda i, j: (i, j)),
        out_shape=jax.ShapeDtypeStruct((M, N), out_dtype),
        compiler_params=_params("parallel", "arbitrary"),
    )(a, w)


def _mm_nt_normbwd(gy, w, x, gain, dres, *, name, tm=512, tk_target=1408):
    M, K = gy.shape
    D = w.shape[0]
    tm = min(tm, M)
    tk = _tile(K, tk_target)
    nk = K // tk

    def body(g_ref, w_ref, x_ref, gain_ref, dres_ref, dx_ref, dg_ref, acc):
        i, k = pl.program_id(0), pl.program_id(1)

        @pl.when(k == 0)
        def _():
            acc[...] = jnp.zeros_like(acc)

        acc[...] += lax.dot_general(g_ref[...].astype(CD), w_ref[...], NT, preferred_element_type=F32)

        @pl.when(k == nk - 1)
        def _():
            dx, dg_rows = _rms_bwd(x_ref[...], gain_ref[...], acc[...])
            dx_ref[...] = dres_ref[...] + dx

            @pl.when(i == 0)
            def _():
                dg_ref[...] = jnp.zeros_like(dg_ref)

            dg_ref[...] += jnp.sum(dg_rows, axis=0, keepdims=True)

    return pl.pallas_call(
        body, name=name, grid=(M // tm, nk),
        in_specs=[pl.BlockSpec((tm, tk), lambda i, k: (i, k)),
                  pl.BlockSpec((D, tk), lambda i, k: (0, k)),
                  pl.BlockSpec((tm, D), lambda i, k: (i, 0)),
                  pl.BlockSpec((1, D), lambda i, k: (0, 0)),
                  pl.BlockSpec((tm, D), lambda i, k: (i, 0))],
        out_specs=[pl.BlockSpec((tm, D), lambda i, k: (i, 0)),
                   pl.BlockSpec((1, D), lambda i, k: (0, 0))],
        out_shape=[jax.ShapeDtypeStruct((M, D), F32), jax.ShapeDtypeStruct((1, D), F32)],
        scratch_shapes=[pltpu.VMEM((tm, D), F32)],
        compiler_params=_params("arbitrary", "arbitrary"),
    )(gy, w, x, gain, dres)


def _mm_tn(a, g, *, name, tt=512, tk_target=1024, tn_target=1024):
    T, K = a.shape
    N = g.shape[1]
    tt = min(tt, T)
    tk = _tile(K, tk_target)
    tn = _tile(N, tn_target)

    def body(a_ref, g_ref, o_ref):
        @pl.when(pl.program_id(2) == 0)
        def _():
            o_ref[...] = jnp.zeros_like(o_ref)

        o_ref[...] += lax.dot_general(a_ref[...].astype(CD), g_ref[...].astype(CD), TN,
                                      preferred_element_type=F32)

    return pl.pallas_call(
        body, name=name, grid=(K // tk, N // tn, T // tt),
        in_specs=[pl.BlockSpec((tt, tk), lambda i, j, t: (t, i)),
                  pl.BlockSpec((tt, tn), lambda i, j, t: (t, j))],
        out_specs=pl.BlockSpec((tk, tn), lambda i, j, t: (i, j)),
        out_shape=jax.ShapeDtypeStruct((K, N), F32),
        compiler_params=_params("parallel", "parallel", "arbitrary"),
    )(a, g)


POOL_W = 512
CONV_W = 512
POOL_GROUP = 128


def _fill_ext(ext, prev_ref, cur_ref, next_ref, i, n, tt):
    zeros = jnp.zeros((HALO,) + ext.shape[1:], F32)
    ext[pl.ds(0, HALO), :] = jnp.where(i > 0, prev_ref[...].astype(F32), zeros)
    ext[pl.ds(HALO, tt), :] = cur_ref[...].astype(F32)
    if next_ref is not None:
        ext[pl.ds(HALO + tt, HALO), :] = jnp.where(i < n - 1, next_ref[...].astype(F32), zeros)


def _mixer_fwd(z, pool_w, pool_scale, dw_w, dw_b, ln_g, ln_b, *, name, tt=256):
    T = z.shape[0]
    tt = min(tt, T)
    n = T // tt
    hb = tt // HALO

    def body(zp_ref, z_ref, pw_ref, ps_ref, w_ref, b_ref, g_ref, bb_ref, o_ref, ext, gl):
        i = pl.program_id(0)
        _fill_ext(ext, zp_ref, z_ref, None, i, n, tt)
        t_glob = i * tt + lax.broadcasted_iota(jnp.int32, (tt, 1), 0)
        for gi, win in enumerate(POOL_WINDOWS):
            cols = pl.ds(gi * POOL_GROUP, POOL_GROUP)
            u = ext[pl.ds(HALO, tt), cols]
            s = u
            for j in range(1, win):
                s = s + ext[pl.ds(HALO - j, tt), cols]
            cnt = jnp.minimum(t_glob + 1, win).astype(F32)
            pooled = s / cnt - u
            ya = jnp.dot(pooled.astype(CD), pw_ref[gi].astype(CD), preferred_element_type=F32)
            o_ref[:, cols] = (ya * ps_ref[:, cols]).astype(o_ref.dtype)
        a = ext[:, pl.ds(POOL_W, CONV_W)]
        b = ext[:, pl.ds(POOL_W + CONV_W, CONV_W)]
        gl[...] = a * _sigmoid(b)
        cv = jnp.zeros((tt, CONV_W), F32) + b_ref[...]
        for j in range(CONV_K):
            cv = cv + w_ref[pl.ds(j, 1), :] * gl[pl.ds(HALO - (CONV_K - 1) + j, tt), :]
        mu = jnp.mean(cv, axis=-1, keepdims=True)
        xc = cv - mu
        yn = xc * lax.rsqrt(jnp.mean(xc * xc, axis=-1, keepdims=True) + EPS) * g_ref[...] + bb_ref[...]
        o_ref[:, pl.ds(POOL_W, CONV_W)] = (yn * _sigmoid(yn)).astype(o_ref.dtype)

    C = z.shape[1]
    full = lambda shape: pl.BlockSpec(shape, lambda i: (0,) * len(shape))
    return pl.pallas_call(
        body, name=name, grid=(n,),
        in_specs=[pl.BlockSpec((HALO, C), lambda i: (jnp.maximum(i * hb - 1, 0), 0)),
                  pl.BlockSpec((tt, C), lambda i: (i, 0)),
                  full((4, POOL_GROUP, POOL_GROUP)), full((1, POOL_W)), full((CONV_K + 1, CONV_W)),
                  full((1, CONV_W)), full((1, CONV_W)), full((1, CONV_W))],
        out_specs=pl.BlockSpec((tt, POOL_W + CONV_W), lambda i: (i, 0)),
        out_shape=jax.ShapeDtypeStruct((T, POOL_W + CONV_W), CD),
        scratch_shapes=[pltpu.VMEM((tt + HALO, C), F32), pltpu.VMEM((tt + HALO, CONV_W), F32)],
        compiler_params=_params("parallel"),
    )(z, z, pool_w, pool_scale, dw_w, dw_b, ln_g, ln_b)


def _mixer_bwd(z, dy, pool_w, pool_scale, dw_w, dw_b, ln_g, ln_b, *, name, tt=256):
    T, C = z.shape
    tt = min(tt, T)
    n = T // tt
    hb = tt // HALO
    R = tt + HALO

    def body(zp_ref, z_ref, zn_ref, dy_ref, dyn_ref, pw_ref, ps_ref, w_ref, b_ref, g_ref, bb_ref,
             dz_ref, dpw_ref, dps_ref, dw_ref, db_ref, dg_ref, dbb_ref, ext, gl, dye, dcv, dpe):
        i = pl.program_id(0)

        @pl.when(i == 0)
        def _():
            for r in (dpw_ref, dps_ref, dw_ref, db_ref, dg_ref, dbb_ref):
                r[...] = jnp.zeros_like(r)

        _fill_ext(ext, zp_ref, z_ref, zn_ref, i, n, tt)
        dye[pl.ds(0, tt), :] = dy_ref[...]
        dye[pl.ds(tt, HALO), :] = jnp.where(i < n - 1, dyn_ref[...], jnp.zeros((HALO, 2 * POOL_W), F32))
        t_glob = i * tt + lax.broadcasted_iota(jnp.int32, (R, 1), 0)

        for gi, win in enumerate(POOL_WINDOWS):
            cols = pl.ds(gi * POOL_GROUP, POOL_GROUP)
            u = ext[pl.ds(HALO, tt), cols]
            s = u
            for j in range(1, win):
                s = s + ext[pl.ds(HALO - j, tt), cols]
            cnt = jnp.minimum(t_glob + 1, win).astype(F32)
            pooled = (s / cnt[:tt] - u).astype(CD)
            pw = pw_ref[gi].astype(CD)
            dya = dye[:, cols]
            mm = jnp.dot(pooled, pw, preferred_element_type=F32)
            dps_ref[:, cols] += jnp.sum(dya[:tt] * mm, axis=0, keepdims=True)
            dm = (dya * ps_ref[:, cols]).astype(CD)
            dpw_ref[gi] += lax.dot_general(pooled, dm[:tt], TN, preferred_element_type=F32)
            dpool = lax.dot_general(dm, pw, NT, preferred_element_type=F32)
            dpe[...] = dpool / cnt
            du = -dpool[:tt]
            for j in range(win):
                du = du + dpe[pl.ds(j, tt), :]
            dz_ref[:, cols] = du.astype(dz_ref.dtype)

        a = ext[:, pl.ds(POOL_W, CONV_W)]
        b = ext[:, pl.ds(POOL_W + CONV_W, CONV_W)]
        sb = _sigmoid(b)
        gl[...] = a * sb
        cv = jnp.zeros((R, CONV_W), F32) + b_ref[...]
        for j in range(CONV_K):
            cv = cv + w_ref[pl.ds(j, 1), :] * gl[pl.ds(HALO - (CONV_K - 1) + j, R), :]
        mu = jnp.mean(cv, axis=-1, keepdims=True)
        xc = cv - mu
        rstd = lax.rsqrt(jnp.mean(xc * xc, axis=-1, keepdims=True) + EPS)
        xhat = xc * rstd
        yn = xhat * g_ref[...] + bb_ref[...]
        sy = _sigmoid(yn)
        dyn = dye[:, pl.ds(POOL_W, CONV_W)] * (sy * (1.0 + yn * (1.0 - sy)))
        dg_ref[...] += jnp.sum(dyn[:tt] * xhat[:tt], axis=0, keepdims=True)
        dbb_ref[...] += jnp.sum(dyn[:tt], axis=0, keepdims=True)
        dxh = dyn * g_ref[...]
        dcv_v = rstd * (dxh - jnp.mean(dxh, axis=-1, keepdims=True)
                        - xhat * jnp.mean(dxh * xhat, axis=-1, keepdims=True))
        dcv[...] = dcv_v
        db_ref[...] += jnp.sum(dcv_v[:tt], axis=0, keepdims=True)
        dgl = jnp.zeros((tt, CONV_W), F32)
        for j in range(CONV_K):
            dgl = dgl + w_ref[pl.ds(j, 1), :] * dcv[pl.ds(CONV_K - 1 - j, tt), :]
            dw_ref[pl.ds(j, 1), :] += jnp.sum(dcv_v[:tt] * gl[pl.ds(HALO - (CONV_K - 1) + j, tt), :],
                                              axis=0, keepdims=True)
        a_t = a[HALO:HALO + tt]
        sb_t = sb[HALO:HALO + tt]
        dz_ref[:, pl.ds(POOL_W, CONV_W)] = (dgl * sb_t).astype(dz_ref.dtype)
        dz_ref[:, pl.ds(POOL_W + CONV_W, CONV_W)] = (dgl * a_t * sb_t * (1.0 - sb_t)).astype(dz_ref.dtype)

    full = lambda shape: pl.BlockSpec(shape, lambda i: (0,) * len(shape))
    nb = T // HALO
    outs = pl.pallas_call(
        body, name=name, grid=(n,),
        in_specs=[pl.BlockSpec((HALO, C), lambda i: (jnp.maximum(i * hb - 1, 0), 0)),
                  pl.BlockSpec((tt, C), lambda i: (i, 0)),
                  pl.BlockSpec((HALO, C), lambda i: (jnp.minimum((i + 1) * hb, nb - 1), 0)),
                  pl.BlockSpec((tt, 2 * POOL_W), lambda i: (i, 0)),
                  pl.BlockSpec((HALO, 2 * POOL_W), lambda i: (jnp.minimum((i + 1) * hb, nb - 1), 0)),
                  full((4, POOL_GROUP, POOL_GROUP)), full((1, POOL_W)), full((CONV_K + 1, CONV_W)),
                  full((1, CONV_W)), full((1, CONV_W)), full((1, CONV_W))],
        out_specs=[pl.BlockSpec((tt, C), lambda i: (i, 0)),
                   full((4, POOL_GROUP, POOL_GROUP)), full((1, POOL_W)), full((CONV_K + 1, CONV_W)),
                   full((1, CONV_W)), full((1, CONV_W)), full((1, CONV_W))],
        out_shape=[jax.ShapeDtypeStruct((T, C), CD),
                   jax.ShapeDtypeStruct((4, POOL_GROUP, POOL_GROUP), F32),
                   jax.ShapeDtypeStruct((1, POOL_W), F32),
                   jax.ShapeDtypeStruct((CONV_K + 1, CONV_W), F32),
                   jax.ShapeDtypeStruct((1, CONV_W), F32),
                   jax.ShapeDtypeStruct((1, CONV_W), F32),
                   jax.ShapeDtypeStruct((1, CONV_W), F32)],
        scratch_shapes=[pltpu.VMEM((tt + 2 * HALO, C), F32), pltpu.VMEM((tt + 2 * HALO, CONV_W), F32),
                        pltpu.VMEM((R, 2 * POOL_W), F32), pltpu.VMEM((R, CONV_W), F32),
                        pltpu.VMEM((R, POOL_GROUP), F32)],
        compiler_params=_params("arbitrary"),
    )(z, z, z, dy, dy, pool_w, pool_scale, dw_w, dw_b, ln_g, ln_b)
    return outs


def _ffn_mid_fwd(up, cw, cb, *, name, tt=256):
    T = up.shape[0]
    tt = min(tt, T)
    n = T // tt
    hb = tt // HALO

    def body(a_ref, gp_ref, g_ref, w_ref, b_ref, o_ref, ext):
        i = pl.program_id(0)
        _fill_ext(ext, gp_ref, g_ref, None, i, n, tt)
        gc = jnp.zeros((tt, D_FF), F32) + b_ref[...]
        for j in range(FFN_K):
            gc = gc + w_ref[pl.ds(j, 1), :] * ext[pl.ds(HALO - (FFN_K - 1) + j, tt), :]
        o_ref[...] = (gc * _sigmoid(gc) * a_ref[...].astype(F32)).astype(o_ref.dtype)

    return pl.pallas_call(
        body, name=name, grid=(n,),
        in_specs=[pl.BlockSpec((tt, D_FF), lambda i: (i, 0)),
                  pl.BlockSpec((HALO, D_FF), lambda i: (jnp.maximum(i * hb - 1, 0), 1)),
                  pl.BlockSpec((tt, D_FF), lambda i: (i, 1)),
                  pl.BlockSpec((8, D_FF), lambda i: (0, 0)),
                  pl.BlockSpec((1, D_FF), lambda i: (0, 0))],
        out_specs=pl.BlockSpec((tt, D_FF), lambda i: (i, 0)),
        out_shape=jax.ShapeDtypeStruct((T, D_FF), CD),
        scratch_shapes=[pltpu.VMEM((tt + HALO, D_FF), F32)],
        compiler_params=_params("parallel"),
    )(up, up, up, cw, cb)


def _ffn_mid_bwd(up, dact, cw, cb, *, name, tt=256):
    T = up.shape[0]
    tt = min(tt, T)
    n = T // tt
    hb = tt // HALO
    nb = T // HALO
    R = tt + HALO

    def body(a_ref, an_ref, gp_ref, g_ref, gn_ref, d_ref, dn_ref, w_ref, b_ref,
             dup_ref, dw_ref, db_ref, ext, dgc):
        i = pl.program_id(0)

        @pl.when(i == 0)
        def _():
            dw_ref[...] = jnp.zeros_like(dw_ref)
            db_ref[...] = jnp.zeros_like(db_ref)

        _fill_ext(ext, gp_ref, g_ref, gn_ref, i, n, tt)
        gc = jnp.zeros((R, D_FF), F32) + b_ref[...]
        for j in range(FFN_K):
            gc = gc + w_ref[pl.ds(j, 1), :] * ext[pl.ds(HALO - (FFN_K - 1) + j, R), :]
        sg = _sigmoid(gc)
        zeros = jnp.zeros((HALO, D_FF), F32)
        d_t = d_ref[...].astype(F32)
        a_t = a_ref[...].astype(F32)
        dsilu = sg * (1.0 + gc * (1.0 - sg))
        dgc_t = d_t * a_t * dsilu[:tt]
        dgc[pl.ds(0, tt), :] = dgc_t
        dgc[pl.ds(tt, HALO), :] = jnp.where(
            i < n - 1, dn_ref[...].astype(F32) * an_ref[...].astype(F32) * dsilu[tt:], zeros)
        dup_ref[:, pl.ds(0, D_FF)] = (d_t * gc[:tt] * sg[:tt]).astype(dup_ref.dtype)
        dg = jnp.zeros((tt, D_FF), F32)
        for j in range(FFN_K):
            dg = dg + w_ref[pl.ds(j, 1), :] * dgc[pl.ds(FFN_K - 1 - j, tt), :]
            dw_ref[pl.ds(j, 1), :] += jnp.sum(dgc_t * ext[pl.ds(HALO - (FFN_K - 1) + j, tt), :],
                                              axis=0, keepdims=True)
        dup_ref[:, pl.ds(D_FF, D_FF)] = dg.astype(dup_ref.dtype)
        db_ref[...] += jnp.sum(dgc_t, axis=0, keepdims=True)

    nxt = lambda i: jnp.minimum((i + 1) * hb, nb - 1)
    return pl.pallas_call(
        body, name=name, grid=(n,),
        in_specs=[pl.BlockSpec((tt, D_FF), lambda i: (i, 0)),
                  pl.BlockSpec((HALO, D_FF), lambda i: (nxt(i), 0)),
                  pl.BlockSpec((HALO, D_FF), lambda i: (jnp.maximum(i * hb - 1, 0), 1)),
                  pl.BlockSpec((tt, D_FF), lambda i: (i, 1)),
                  pl.BlockSpec((HALO, D_FF), lambda i: (nxt(i), 1)),
                  pl.BlockSpec((tt, D_FF), lambda i: (i, 0)),
                  pl.BlockSpec((HALO, D_FF), lambda i: (nxt(i), 0)),
                  pl.BlockSpec((8, D_FF), lambda i: (0, 0)),
                  pl.BlockSpec((1, D_FF), lambda i: (0, 0))],
        out_specs=[pl.BlockSpec((tt, 2 * D_FF), lambda i: (i, 0)),
                   pl.BlockSpec((8, D_FF), lambda i: (0, 0)),
                   pl.BlockSpec((1, D_FF), lambda i: (0, 0))],
        out_shape=[jax.ShapeDtypeStruct((T, 2 * D_FF), CD),
                   jax.ShapeDtypeStruct((8, D_FF), F32),
                   jax.ShapeDtypeStruct((1, D_FF), F32)],
        scratch_shapes=[pltpu.VMEM((tt + 2 * HALO, D_FF), F32), pltpu.VMEM((R, D_FF), F32)],
        compiler_params=_params("arbitrary"),
    )(up, up, up, up, up, dact, dact, cw, cb)


def _xattn_probs(q, k):
    s = lax.dot_general(q, k, NT, preferred_element_type=F32) * XA_SCALE
    p = jnp.exp(s - jnp.max(s, axis=-1, keepdims=True))
    return p / jnp.sum(p, axis=-1, keepdims=True)


def _xattn_fwd(q, kv, *, name, tq=512):
    T = q.shape[0]
    tq = min(tq, T)

    def body(q_ref, kv_ref, o_ref):
        for h in range(XA_HEADS):
            cols = pl.ds(h * XA_DH, XA_DH)
            p = _xattn_probs(q_ref[:, cols], kv_ref[:, cols])
            v = kv_ref[:, pl.ds(D_MODEL + h * XA_DH, XA_DH)]
            o_ref[:, cols] = jnp.dot(p.astype(CD), v, preferred_element_type=F32).astype(o_ref.dtype)

    return pl.pallas_call(
        body, name=name, grid=(T // tq,),
        in_specs=[pl.BlockSpec((tq, D_MODEL), lambda i: (i, 0)),
                  pl.BlockSpec((MEM_LEN, 2 * D_MODEL), lambda i: (0, 0))],
        out_specs=pl.BlockSpec((tq, D_MODEL), lambda i: (i, 0)),
        out_shape=jax.ShapeDtypeStruct((T, D_MODEL), CD),
        compiler_params=_params("parallel"),
    )(q, kv)


def _xattn_bwd(q, kv, do, *, name, tq=512):
    T = q.shape[0]
    tq = min(tq, T)

    def body(q_ref, kv_ref, do_ref, dq_ref, dkv_ref):
        @pl.when(pl.program_id(0) == 0)
        def _():
            dkv_ref[...] = jnp.zeros_like(dkv_ref)

        for h in range(XA_HEADS):
            cols = pl.ds(h * XA_DH, XA_DH)
            vcols = pl.ds(D_MODEL + h * XA_DH, XA_DH)
            qh, kh, vh, doh = q_ref[:, cols], kv_ref[:, cols], kv_ref[:, vcols], do_ref[:, cols]
            p = _xattn_probs(qh, kh)
            dkv_ref[:, vcols] += lax.dot_general(p.astype(CD), doh, TN, preferred_element_type=F32)
            dp = lax.dot_general(doh, vh, NT, preferred_element_type=F32)
            ds = (p * (dp - jnp.sum(dp * p, axis=-1, keepdims=True)) * XA_SCALE).astype(CD)
            dq_ref[:, cols] = jnp.dot(ds, kh, preferred_element_type=F32).astype(dq_ref.dtype)
            dkv_ref[:, cols] += lax.dot_general(ds, qh, TN, preferred_element_type=F32)

    return pl.pallas_call(
        body, name=name, grid=(T // tq,),
        in_specs=[pl.BlockSpec((tq, D_MODEL), lambda i: (i, 0)),
                  pl.BlockSpec((MEM_LEN, 2 * D_MODEL), lambda i: (0, 0)),
                  pl.BlockSpec((tq, D_MODEL), lambda i: (i, 0))],
        out_specs=[pl.BlockSpec((tq, D_MODEL), lambda i: (i, 0)),
                   pl.BlockSpec((MEM_LEN, 2 * D_MODEL), lambda i: (0, 0))],
        out_shape=[jax.ShapeDtypeStruct((T, D_MODEL), CD),
                   jax.ShapeDtypeStruct((MEM_LEN, 2 * D_MODEL), F32)],
        compiler_params=_params("arbitrary"),
    )(q, kv, do)


C_W = Q_LORA + KV_LORA + LANES


def _rot(x):
    lane = lax.broadcasted_iota(jnp.int32, x.shape, x.ndim - 1)
    up = pltpu.roll(x, LANES - QK_ROPE // 2, x.ndim - 1)
    dn = pltpu.roll(x, QK_ROPE // 2, x.ndim - 1)
    lo, mid, hi = QK_NOPE, QK_NOPE + QK_ROPE // 2, QK_NOPE + QK_ROPE
    return jnp.where((lane >= lo) & (lane < mid), -up, jnp.where((lane >= mid) & (lane < hi), dn, 0.0))


def _mla_mid_fwd(c, qg, kvg, cs, sn, *, name, tt=512):
    T = c.shape[0]
    tt = min(tt, T)

    def body(c_ref, qg_ref, kg_ref, cs_ref, sn_ref, qn_ref, kn_ref, kpe_ref):
        cq = c_ref[:, pl.ds(0, Q_LORA)]
        qn_ref[...] = (cq * lax.rsqrt(jnp.mean(cq * cq, axis=-1, keepdims=True) + EPS)
                       * qg_ref[...]).astype(qn_ref.dtype)
        ck = c_ref[:, pl.ds(Q_LORA, KV_LORA)]
        kn_ref[...] = (ck * lax.rsqrt(jnp.mean(ck * ck, axis=-1, keepdims=True) + EPS)
                       * kg_ref[...]).astype(kn_ref.dtype)
        kp = c_ref[:, pl.ds(Q_LORA + KV_LORA, LANES)]
        kpe_ref[...] = kp * cs_ref[...] + _rot(kp) * sn_ref[...]

    row = lambda w: pl.BlockSpec((tt, w), lambda i: (i, 0))
    one = lambda w: pl.BlockSpec((1, w), lambda i: (0, 0))
    return pl.pallas_call(
        body, name=name, grid=(T // tt,),
        in_specs=[row(C_W), one(Q_LORA), one(KV_LORA), row(LANES), row(LANES)],
        out_specs=[row(Q_LORA), row(KV_LORA), row(LANES)],
        out_shape=[jax.ShapeDtypeStruct((T, Q_LORA), CD), jax.ShapeDtypeStruct((T, KV_LORA), CD),
                   jax.ShapeDtypeStruct((T, LANES), F32)],
        compiler_params=_params("parallel"),
    )(c, qg, kvg, cs, sn)


def _mla_mid_bwd(c, dqn, dkvn, dksum, qg, kvg, cs, sn, *, name, tt=512):
    T = c.shape[0]
    tt = min(tt, T)

    def body(c_ref, dq_ref, dk_ref, ds_ref, qg_ref, kg_ref, cs_ref, sn_ref, dc_ref, dqg_ref, dkg_ref):
        @pl.when(pl.program_id(0) == 0)
        def _():
            dqg_ref[...] = jnp.zeros_like(dqg_ref)
            dkg_ref[...] = jnp.zeros_like(dkg_ref)

        dx, dg = _rms_bwd(c_ref[:, pl.ds(0, Q_LORA)], qg_ref[...], dq_ref[...])
        dc_ref[:, pl.ds(0, Q_LORA)] = dx.astype(dc_ref.dtype)
        dqg_ref[...] += jnp.sum(dg, axis=0, keepdims=True)
        dx, dg = _rms_bwd(c_ref[:, pl.ds(Q_LORA, KV_LORA)], kg_ref[...], dk_ref[...])
        dc_ref[:, pl.ds(Q_LORA, KV_LORA)] = dx.astype(dc_ref.dtype)
        dkg_ref[...] += jnp.sum(dg, axis=0, keepdims=True)
        d = ds_ref[...]
        lane = lax.broadcasted_iota(jnp.int32, d.shape, 1)
        dkp = d * cs_ref[...] - _rot(d * sn_ref[...])
        dc_ref[:, pl.ds(Q_LORA + KV_LORA, LANES)] = jnp.where(
            (lane >= QK_NOPE) & (lane < QK_NOPE + QK_ROPE), dkp, 0.0).astype(dc_ref.dtype)

    row = lambda w: pl.BlockSpec((tt, w), lambda i: (i, 0))
    one = lambda w: pl.BlockSpec((1, w), lambda i: (0, 0))
    return pl.pallas_call(
        body, name=name, grid=(T // tt,),
        in_specs=[row(C_W), row(Q_LORA), row(KV_LORA), row(LANES), one(Q_LORA), one(KV_LORA),
                  row(LANES), row(LANES)],
        out_specs=[row(C_W), one(Q_LORA), one(KV_LORA)],
        out_shape=[jax.ShapeDtypeStruct((T, C_W), CD), jax.ShapeDtypeStruct((1, Q_LORA), F32),
                   jax.ShapeDtypeStruct((1, KV_LORA), F32)],
        compiler_params=_params("arbitrary"),
    )(c, dqn, dkvn, dksum, qg, kvg, cs, sn)


def _mla_qkv_fwd(qn, kvn, kpe, cs, sn, wq, wk, wv, *, name, tt=256):
    T = qn.shape[0]
    tt = min(tt, T)
    H = MLA_HEADS

    def body(qn_ref, kn_ref, kpe_ref, cs_ref, sn_ref, wq_ref, wk_ref, wv_ref, q_ref, k_ref, v_ref):
        qn_v, kn_v, kpe_v, cs_v, sn_v = qn_ref[...], kn_ref[...], kpe_ref[...], cs_ref[...], sn_ref[...]
        for h in range(H):
            q = jnp.dot(qn_v, wq_ref[h], preferred_element_type=F32)
            q_ref[h] = (q * cs_v + _rot(q) * sn_v).astype(q_ref.dtype)
            k_ref[h] = (jnp.dot(kn_v, wk_ref[h], preferred_element_type=F32) + kpe_v).astype(k_ref.dtype)
            v_ref[h] = jnp.dot(kn_v, wv_ref[h], preferred_element_type=F32).astype(v_ref.dtype)

    row = lambda w: pl.BlockSpec((tt, w), lambda i: (i, 0))
    wsp = lambda k: pl.BlockSpec((H, k, LANES), lambda i: (0, 0, 0))
    hsp = pl.BlockSpec((H, tt, LANES), lambda i: (0, i, 0))
    sh = jax.ShapeDtypeStruct((H, T, LANES), CD)
    return pl.pallas_call(
        body, name=name, grid=(T // tt,),
        in_specs=[row(Q_LORA), row(KV_LORA), row(LANES), row(LANES), row(LANES),
                  wsp(Q_LORA), wsp(KV_LORA), wsp(KV_LORA)],
        out_specs=[hsp, hsp, hsp], out_shape=[sh, sh, sh],
        compiler_params=_params("parallel"),
    )(qn, kvn, kpe, cs, sn, wq, wk, wv)


def _mla_qkv_bwd(dq, dk, dv, qn, kvn, cs, sn, wq, wk, wv, *, name, tt=256):
    T = qn.shape[0]
    tt = min(tt, T)
    H = MLA_HEADS

    def body(dq_ref, dk_ref, dv_ref, qn_ref, kn_ref, cs_ref, sn_ref, wq_ref, wk_ref, wv_ref,
             dqn_ref, dkn_ref, dks_ref, dwq_ref, dwk_ref, dwv_ref):
        @pl.when(pl.program_id(0) == 0)
        def _():
            for r in (dwq_ref, dwk_ref, dwv_ref):
                r[...] = jnp.zeros_like(r)

        qn_v, kn_v, cs_v, sn_v = qn_ref[...], kn_ref[...], cs_ref[...], sn_ref[...]
        dqn = jnp.zeros((tt, Q_LORA), F32)
        dkn = jnp.zeros((tt, KV_LORA), F32)
        dks = jnp.zeros((tt, LANES), F32)
        for h in range(H):
            d = dq_ref[h]
            dqh = (d * cs_v - _rot(d * sn_v)).astype(CD)
            dkh, dvh = dk_ref[h], dv_ref[h]
            dqn = dqn + lax.dot_general(dqh, wq_ref[h], NT, preferred_element_type=F32)
            dkn = dkn + lax.dot_general(dkh, wk_ref[h], NT, preferred_element_type=F32)
            dkn = dkn + lax.dot_general(dvh, wv_ref[h], NT, preferred_element_type=F32)
            dks = dks + dkh.astype(F32)
            dwq_ref[h] += lax.dot_general(qn_v, dqh, TN, preferred_element_type=F32)
            dwk_ref[h] += lax.dot_general(kn_v, dkh, TN, preferred_element_type=F32)
            dwv_ref[h] += lax.dot_general(kn_v, dvh, TN, preferred_element_type=F32)
        dqn_ref[...] = dqn
        dkn_ref[...] = dkn
        dks_ref[...] = dks

    row = lambda w: pl.BlockSpec((tt, w), lambda i: (i, 0))
    wsp = lambda k: pl.BlockSpec((H, k, LANES), lambda i: (0, 0, 0))
    hsp = pl.BlockSpec((H, tt, LANES), lambda i: (0, i, 0))
    return pl.pallas_call(
        body, name=name, grid=(T // tt,),
        in_specs=[hsp, hsp, hsp, row(Q_LORA), row(KV_LORA), row(LANES), row(LANES),
                  wsp(Q_LORA), wsp(KV_LORA), wsp(KV_LORA)],
        out_specs=[row(Q_LORA), row(KV_LORA), row(LANES), wsp(Q_LORA), wsp(KV_LORA), wsp(KV_LORA)],
        out_shape=[jax.ShapeDtypeStruct((T, Q_LORA), F32), jax.ShapeDtypeStruct((T, KV_LORA), F32),
                   jax.ShapeDtypeStruct((T, LANES), F32),
                   jax.ShapeDtypeStruct((H, Q_LORA, LANES), F32),
                   jax.ShapeDtypeStruct((H, KV_LORA, LANES), F32),
                   jax.ShapeDtypeStruct((H, KV_LORA, LANES), F32)],
        compiler_params=_params("arbitrary"),
    )(dq, dk, dv, qn, kvn, cs, sn, wq, wk, wv)


FLASH_BLOCK = 1024
EXP2_SCALE = MLA_SCALE * math.log2(math.e)


def _causal_steps(nq, by_key):
    pairs = [(i, j) for j in range(nq) for i in range(j, nq)] if by_key else \
            [(i, j) for i in range(nq) for j in range(i + 1)]
    return (jnp.asarray([p[0] for p in pairs], jnp.int32), jnp.asarray([p[1] for p in pairs], jnp.int32))


def _raw_scores(q, k, masked):
    s = lax.dot_general(q, k, NT, preferred_element_type=F32)
    if masked:
        row = lax.broadcasted_iota(jnp.int32, s.shape, 0)
        col = lax.broadcasted_iota(jnp.int32, s.shape, 1)
        s = jnp.where(col <= row, s, NEG)
    return s


def _flash_fwd(q, k, v, *, name):
    H, T, _ = q.shape
    tq = min(FLASH_BLOCK, T)
    nq = T // tq
    i_tab, j_tab = _causal_steps(nq, by_key=False)

    def body(i_tab, j_tab, q_ref, k_ref, v_ref, o_ref, lse_ref, m_sc, l_sc, acc):
        t = pl.program_id(1)
        i, j = i_tab[t], j_tab[t]

        @pl.when(j == 0)
        def _():
            m_sc[...] = jnp.full_like(m_sc, NEG)
            l_sc[...] = jnp.zeros_like(l_sc)
            acc[...] = jnp.zeros_like(acc)

        def step(masked):
            lane = lax.broadcasted_iota(jnp.int32, (tq, LANES), 1)
            alphas, pvs = [], []
            for h in range(2):
                s = _raw_scores(q_ref[h], k_ref[h], masked)
                m_prev = m_sc[h]
                m_new = jnp.maximum(m_prev, jnp.max(s, axis=-1, keepdims=True))
                alpha = jnp.exp2((m_prev - m_new) * EXP2_SCALE)
                p = jnp.exp2((s - m_new[:, :1]) * EXP2_SCALE)
                l_sc[h] = alpha * l_sc[h] + jnp.sum(p, axis=-1, keepdims=True)
                m_sc[h] = m_new
                alphas.append(alpha)
                pvs.append(jnp.dot(p.astype(CD), v_ref[h], preferred_element_type=F32))
            acc[...] = acc[...] * jnp.where(lane < V_HEAD, alphas[0], alphas[1]) + pvs[0] + pvs[1]

        @pl.when(j < i)
        def _():
            step(False)

        @pl.when(j == i)
        def _():
            step(True)
            lane = lax.broadcasted_iota(jnp.int32, (tq, LANES), 1)
            o_ref[...] = (acc[...] / jnp.where(lane < V_HEAD, l_sc[0], l_sc[1])).astype(o_ref.dtype)
            for h in range(2):
                lse_ref[h] = m_sc[h] * EXP2_SCALE + jnp.log2(l_sc[h])

    qsp = pl.BlockSpec((2, tq, LANES), lambda p, t, it, jt: (p, it[t], 0))
    ksp = pl.BlockSpec((2, tq, LANES), lambda p, t, it, jt: (p, jt[t], 0))
    return pl.pallas_call(
        body, name=name,
        grid_spec=pltpu.PrefetchScalarGridSpec(
            num_scalar_prefetch=2, grid=(H // 2, int(i_tab.shape[0])),
            in_specs=[qsp, ksp, ksp],
            out_specs=[pl.BlockSpec((tq, LANES), lambda p, t, it, jt: (it[t], p)), qsp],
            scratch_shapes=[pltpu.VMEM((2, tq, LANES), F32), pltpu.VMEM((2, tq, LANES), F32),
                            pltpu.VMEM((tq, LANES), F32)]),
        out_shape=[jax.ShapeDtypeStruct((T, H * V_HEAD), CD), jax.ShapeDtypeStruct((H, T, LANES), F32)],
        compiler_params=_params("parallel", "arbitrary"),
    )(i_tab, j_tab, q, k, v)


def _flash_delta(o, do, *, name, tt=512):
    T = o.shape[0]
    tt = min(tt, T)
    H = MLA_HEADS

    def body(o_ref, do_ref, dl_ref):
        lane = lax.broadcasted_iota(jnp.int32, (tt, LANES), 1)
        for p in range(H // 2):
            cols = pl.ds(p * LANES, LANES)
            prod = do_ref[:, cols].astype(F32) * o_ref[:, cols].astype(F32)
            d0 = jnp.sum(jnp.where(lane < V_HEAD, prod, 0.0), axis=-1, keepdims=True)
            d1 = jnp.sum(jnp.where(lane < V_HEAD, 0.0, prod), axis=-1, keepdims=True)
            dl_ref[2 * p] = jnp.broadcast_to(d0, (tt, LANES))
            dl_ref[2 * p + 1] = jnp.broadcast_to(d1, (tt, LANES))

    row = pl.BlockSpec((tt, H * V_HEAD), lambda i: (i, 0))
    return pl.pallas_call(
        body, name=name, grid=(T // tt,), in_specs=[row, row],
        out_specs=pl.BlockSpec((H, tt, LANES), lambda i: (0, i, 0)),
        out_shape=jax.ShapeDtypeStruct((H, T, LANES), F32),
        compiler_params=_params("parallel"),
    )(o, do)


def _flash_bwd(q, k, v, do, lse, delta, *, name):
    H, T, _ = q.shape
    tq = min(FLASH_BLOCK, T)
    nq = T // tq
    i_tab, j_tab = _causal_steps(nq, by_key=True)

    def body(i_tab, j_tab, q_ref, k_ref, v_ref, do_ref, lse_ref, dl_ref, dq_ref, dk_ref, dv_ref, dk_acc, dv_acc):
        t = pl.program_id(1)
        i, j = i_tab[t], j_tab[t]
        rows = pl.ds(pl.multiple_of(i * tq, tq), tq)

        @pl.when(t == 0)
        def _():
            dq_ref[...] = jnp.zeros_like(dq_ref)

        def step(masked):
            do_v = do_ref[...]
            for h in range(2):
                s = _raw_scores(q_ref[h], k_ref[h], masked)
                p = jnp.exp2(s * EXP2_SCALE - lse_ref[h][:, :1])
                dv_acc[h] += lax.dot_general(p.astype(CD), do_v, TN, preferred_element_type=F32)
                dp = lax.dot_general(do_v, v_ref[h], NT, preferred_element_type=F32)
                ds = (p * (dp - dl_ref[h][:, :1]) * MLA_SCALE).astype(CD)
                dk_acc[h] += lax.dot_general(ds, q_ref[h], TN, preferred_element_type=F32)
                dq_ref[h, rows, :] += jnp.dot(ds, k_ref[h], preferred_element_type=F32)

        @pl.when(i == j)
        def _():
            dk_acc[...] = jnp.zeros_like(dk_acc)
            dv_acc[...] = jnp.zeros_like(dv_acc)
            step(True)

        @pl.when(i > j)
        def _():
            step(False)

        @pl.when(i == nq - 1)
        def _():
            lane = lax.broadcasted_iota(jnp.int32, (tq, LANES), 1)
            dk_ref[...] = dk_acc[...].astype(dk_ref.dtype)
            dv_ref[0] = jnp.where(lane < V_HEAD, dv_acc[0], 0.0).astype(dv_ref.dtype)
            dv_ref[1] = jnp.where(lane < V_HEAD, 0.0, dv_acc[1]).astype(dv_ref.dtype)

    qsp = pl.BlockSpec((2, tq, LANES), lambda p, t, it, jt: (p, it[t], 0))
    ksp = pl.BlockSpec((2, tq, LANES), lambda p, t, it, jt: (p, jt[t], 0))
    osp = pl.BlockSpec((tq, LANES), lambda p, t, it, jt: (it[t], p))
    sh = jax.ShapeDtypeStruct((H, T, LANES), CD)
    return pl.pallas_call(
        body, name=name,
        grid_spec=pltpu.PrefetchScalarGridSpec(
            num_scalar_prefetch=2, grid=(H // 2, int(i_tab.shape[0])),
            in_specs=[qsp, ksp, ksp, osp, qsp, qsp],
            out_specs=[pl.BlockSpec((2, T, LANES), lambda p, t, it, jt: (p, 0, 0)), ksp, ksp],
            scratch_shapes=[pltpu.VMEM((2, tq, LANES), F32), pltpu.VMEM((2, tq, LANES), F32)]),
        out_shape=[jax.ShapeDtypeStruct((H, T, LANES), F32), sh, sh],
        compiler_params=_params("parallel", "arbitrary"),
    )(i_tab, j_tab, q, k, v, do, lse, delta)


def _loss_head(x, g, target, *, name, tt=512):
    T, D = x.shape
    tt = min(tt, T)

    def body(x_ref, g_ref, t_ref, dx_ref, dg_ref, loss_ref):
        @pl.when(pl.program_id(0) == 0)
        def _():
            dg_ref[...] = jnp.zeros_like(dg_ref)
            loss_ref[...] = jnp.zeros_like(loss_ref)

        xv, gv = x_ref[...], g_ref[...]
        r = lax.rsqrt(jnp.mean(xv * xv, axis=-1, keepdims=True) + EPS)
        err = xv * r * gv - t_ref[...]
        tok = jnp.mean(err * err, axis=-1, keepdims=True)
        loss_ref[...] += 0.5 * jnp.sum(tok, axis=0, keepdims=True)
        dx, dg_rows = _rms_bwd(xv, gv, err * (1.0 / D))
        dx_ref[...] = dx
        dg_ref[...] += jnp.sum(dg_rows, axis=0, keepdims=True)

    return pl.pallas_call(
        body, name=name, grid=(T // tt,),
        in_specs=[pl.BlockSpec((tt, D), lambda i: (i, 0)), pl.BlockSpec((1, D), lambda i: (0, 0)),
                  pl.BlockSpec((tt, D), lambda i: (i, 0))],
        out_specs=[pl.BlockSpec((tt, D), lambda i: (i, 0)), pl.BlockSpec((1, D), lambda i: (0, 0)),
                   pl.BlockSpec((1, LANES), lambda i: (0, 0))],
        out_shape=[jax.ShapeDtypeStruct((T, D), F32), jax.ShapeDtypeStruct((1, D), F32),
                   jax.ShapeDtypeStruct((1, LANES), F32)],
        compiler_params=_params("arbitrary"),
    )(x, g, target)


def _rope_tables(positions):
    inv = 1.0 / (ROPE_THETA ** (jnp.arange(0, QK_ROPE, 2, dtype=F32) / QK_ROPE))
    ang = positions.astype(F32)[:, None] * inv
    c, s = jnp.cos(ang), jnp.sin(ang)
    T = positions.shape[0]
    cs = jnp.concatenate([jnp.ones((T, QK_NOPE), F32), c, c, jnp.zeros((T, LANES - QK_NOPE - QK_ROPE), F32)], 1)
    sn = jnp.concatenate([jnp.zeros((T, QK_NOPE), F32), s, s, jnp.zeros((T, LANES - QK_NOPE - QK_ROPE), F32)], 1)
    return cs, sn


def _pad_rows(w, rows):
    return jnp.concatenate([w, jnp.zeros((rows - w.shape[0],) + w.shape[1:], w.dtype)], 0)


def _mla_weights(w_dq_dkv, w_uq, w_ukv):
    K = w_dq_dkv.shape[0]
    z = lambda n: jnp.zeros((K, n), w_dq_dkv.dtype)
    wc = jnp.concatenate([w_dq_dkv[:, :Q_LORA + KV_LORA], z(QK_NOPE), w_dq_dkv[:, Q_LORA + KV_LORA:],
                          z(LANES - QK_NOPE - QK_ROPE)], 1)
    wq = w_uq.reshape(Q_LORA, MLA_HEADS, QK_NOPE + QK_ROPE).transpose(1, 0, 2)
    wq = jnp.concatenate([wq, jnp.zeros((MLA_HEADS, Q_LORA, LANES - QK_NOPE - QK_ROPE), wq.dtype)], 2)
    wkv = w_ukv.reshape(KV_LORA, MLA_HEADS, QK_NOPE + V_HEAD).transpose(1, 0, 2)
    zero = jnp.zeros_like(wkv[:, :, :QK_NOPE])
    wk = jnp.concatenate([wkv[:, :, :QK_NOPE], zero], 2)
    wv_lo = jnp.concatenate([wkv[:, :, QK_NOPE:], zero], 2)
    wv_hi = jnp.concatenate([zero, wkv[:, :, QK_NOPE:]], 2)
    odd = (jnp.arange(MLA_HEADS) % 2 == 1)[:, None, None]
    wv = jnp.where(odd, wv_hi, wv_lo)
    return wc, wq, wk, wv


def _mla_weight_grads(dwc, dwq, dwk, dwv):
    d_dq = jnp.concatenate([dwc[:, :Q_LORA + KV_LORA],
                            dwc[:, Q_LORA + KV_LORA + QK_NOPE:Q_LORA + KV_LORA + QK_NOPE + QK_ROPE]], 1)
    d_uq = dwq[:, :, :QK_NOPE + QK_ROPE].transpose(1, 0, 2).reshape(Q_LORA, MLA_HEADS * (QK_NOPE + QK_ROPE))
    odd = (jnp.arange(MLA_HEADS) % 2 == 1)[:, None, None]
    dv = jnp.where(odd, dwv[:, :, V_HEAD:], dwv[:, :, :V_HEAD])
    d_ukv = jnp.concatenate([dwk[:, :, :QK_NOPE], dv], 2).transpose(1, 0, 2).reshape(
        KV_LORA, MLA_HEADS * (QK_NOPE + V_HEAD))
    return d_dq, d_uq, d_ukv


def _local_step(x, mem, positions, target, W):
    G = {}
    row = lambda v: v.reshape(1, -1)
    cs, sn = _rope_tables(positions)
    saved = []
    for l in range(DEPTH):
        L = f"l{l}"
        s = {"x0": x}
        if l % 2 == 0:
            e = l // 2
            s["z"], s["h"] = _nmm(x, row(W["norm_mix_g"][l]), W["pc_w_in"][e], name=f"{L}_mix_in", out_dtype=F32)
            s["dw_w"] = _pad_rows(W["conv_dw_w"][e], CONV_K + 1)
            s["mix_p"] = (W["pool_w"][e], row(W["pool_scale"][e]), s["dw_w"], row(W["conv_dw_b"][e]),
                          row(W["conv_ln_g"][e]), row(W["conv_ln_b"][e]))
            s["ycat"] = _mixer_fwd(s["z"], *s["mix_p"], name=f"{L}_mix_mid")
            x = _mm_res(s["ycat"], W["pc_w_out"][e], x, name=f"{L}_mix_out")
        else:
            o = l // 2
            wc, wq, wk, wv = _mla_weights(W["mla_w_dq_dkv"][o], W["mla_w_uq"][o], W["mla_w_ukv"][o])
            s["mla_w"] = (wc, wq, wk, wv)
            s["c"], s["h"] = _nmm(x, row(W["norm_mix_g"][l]), wc, name=f"{L}_mla_down", out_dtype=F32)
            s["qg"], s["kvg"] = row(W["mla_q_norm_g"][o]), row(W["mla_kv_norm_g"][o])
            s["qn"], s["kvn"], kpe = _mla_mid_fwd(s["c"], s["qg"], s["kvg"], cs, sn, name=f"{L}_mla_mid")
            s["q"], s["k"], s["v"] = _mla_qkv_fwd(s["qn"], s["kvn"], kpe, cs, sn, wq, wk, wv, name=f"{L}_mla_qkv")
            s["o"], s["lse"] = _flash_fwd(s["q"], s["k"], s["v"], name=f"{L}_mla_attn")
            x = _mm_res(s["o"], W["mla_w_o"][o], x, name=f"{L}_mla_out")
        s["x1"] = x
        s["xq"], s["hx"] = _nmm(x, row(W["norm_xa_g"][l]), W["xa_wq"][l], name=f"{L}_xa_q", out_dtype=CD)
        s["xkv"], s["hm"] = _nmm(mem, row(W["norm_mem_g"][l]), W["xa_wkv"][l], name=f"{L}_xa_kv", out_dtype=CD)
        s["xo"] = _xattn_fwd(s["xq"], s["xkv"], name=f"{L}_xa_attn")
        x = _mm_res(s["xo"], W["xa_wo"][l], x, name=f"{L}_xa_out")
        s["x2"] = x
        s["up"], s["hf"] = _nmm(x, row(W["norm_ffn_g"][l]), W["ffn_w_up"][l], name=f"{L}_ffn_up", out_dtype=CD,
                                tn_target=1408)
        s["cw"], s["cb"] = _pad_rows(W["ffn_conv_w"][l], 8), row(W["ffn_conv_b"][l])
        s["act"] = _ffn_mid_fwd(s["up"], s["cw"], s["cb"], name=f"{L}_ffn_mid")
        x = _mm_res(s["act"], W["ffn_w_down"][l], x, name=f"{L}_ffn_down")
        saved.append(s)
    dx, G["final_norm_g"], loss = _loss_head(x, row(W["final_norm_g"]), target, name="loss_head")
    G["final_norm_g"] = G["final_norm_g"].reshape(-1)

    per_layer = {}

    def put(name, l, val):
        per_layer.setdefault(name, {})[l] = val

    for l in reversed(range(DEPTH)):
        L = f"l{l}"
        s = saved[l]
        put("ffn_w_down", l, _mm_tn(s["act"], dx, name=f"{L}_ffn_down_dw", tk_target=1408))
        dact = _mm_nt(dx, W["ffn_w_down"][l], name=f"{L}_ffn_down_dx", out_dtype=CD, tn_target=1408)
        dup, dcw, dcb = _ffn_mid_bwd(s["up"], dact, s["cw"], s["cb"], name=f"{L}_ffn_mid_bwd")
        put("ffn_conv_w", l, dcw[:FFN_K])
        put("ffn_conv_b", l, dcb[0])
        put("ffn_w_up", l, _mm_tn(s["hf"], dup, name=f"{L}_ffn_up_dw", tn_target=1408))
        dx, dg = _mm_nt_normbwd(dup, W["ffn_w_up"][l], s["x2"], row(W["norm_ffn_g"][l]), dx, name=f"{L}_ffn_up_dx")
        put("norm_ffn_g", l, dg[0])
        put("xa_wo", l, _mm_tn(s["xo"], dx, name=f"{L}_xa_out_dw"))
        do = _mm_nt(dx, W["xa_wo"][l], name=f"{L}_xa_out_dx", out_dtype=CD)
        dq, dkv = _xattn_bwd(s["xq"], s["xkv"], do, name=f"{L}_xa_attn_bwd")
        put("xa_wq", l, _mm_tn(s["hx"], dq, name=f"{L}_xa_q_dw"))
        dx, dg = _mm_nt_normbwd(dq, W["xa_wq"][l], s["x1"], row(W["norm_xa_g"][l]), dx, name=f"{L}_xa_q_dx")
        put("norm_xa_g", l, dg[0])
        put("xa_wkv", l, _mm_tn(s["hm"], dkv, name=f"{L}_xa_kv_dw", tt=MEM_LEN))
        _, dg = _mm_nt_normbwd(dkv, W["xa_wkv"][l], mem, row(W["norm_mem_g"][l]), jnp.zeros_like(mem),
                               name=f"{L}_xa_kv_dx", tm=MEM_LEN)
        put("norm_mem_g", l, dg[0])
        if l % 2 == 0:
            e = l // 2
            put("pc_w_out", e, _mm_tn(s["ycat"], dx, name=f"{L}_mix_out_dw"))
            dy = _mm_nt(dx, W["pc_w_out"][e], name=f"{L}_mix_out_dx", out_dtype=F32)
            dz, dpw, dps, ddw, ddb, dlg, dlb = _mixer_bwd(s["z"], dy, *s["mix_p"], name=f"{L}_mix_mid_bwd")
            put("pool_w", e, dpw)
            put("pool_scale", e, dps[0])
            put("conv_dw_w", e, ddw[:CONV_K])
            put("conv_dw_b", e, ddb[0])
            put("conv_ln_g", e, dlg[0])
            put("conv_ln_b", e, dlb[0])
            put("pc_w_in", e, _mm_tn(s["h"], dz, name=f"{L}_mix_in_dw"))
            dx, dg = _mm_nt_normbwd(dz, W["pc_w_in"][e], s["x0"], row(W["norm_mix_g"][l]), dx, name=f"{L}_mix_in_dx")
        else:
            o = l // 2
            wc, wq, wk, wv = s["mla_w"]
            put("mla_w_o", o, _mm_tn(s["o"], dx, name=f"{L}_mla_out_dw"))
            do = _mm_nt(dx, W["mla_w_o"][o], name=f"{L}_mla_out_dx", out_dtype=CD)
            delta = _flash_delta(s["o"], do, name=f"{L}_mla_attn_delta")
            dq, dk, dv = _flash_bwd(s["q"], s["k"], s["v"], do, s["lse"], delta, name=f"{L}_mla_attn_bwd")
            dqn, dkvn, dks, dwq, dwk, dwv = _mla_qkv_bwd(dq, dk, dv, s["qn"], s["kvn"], cs, sn, wq, wk, wv,
                                                         name=f"{L}_mla_qkv_bwd")
            dc, dqg, dkg = _mla_mid_bwd(s["c"], dqn, dkvn, dks, s["qg"], s["kvg"], cs, sn, name=f"{L}_mla_mid_bwd")
            put("mla_q_norm_g", o, dqg[0])
            put("mla_kv_norm_g", o, dkg[0])
            dwc = _mm_tn(s["h"], dc, name=f"{L}_mla_down_dw")
            d_dq, d_uq, d_ukv = _mla_weight_grads(dwc, dwq, dwk, dwv)
            put("mla_w_dq_dkv", o, d_dq)
            put("mla_w_uq", o, d_uq)
            put("mla_w_ukv", o, d_ukv)
            dx, dg = _mm_nt_normbwd(dc, wc, s["x0"], row(W["norm_mix_g"][l]), dx, name=f"{L}_mla_down_dx",
                                    tk_target=768)
        put("norm_mix_g", l, dg[0])
    for name, d in per_layer.items():
        G[name] = jnp.stack([d[i] for i in sorted(d)], 0)
    return loss, dx, G


_ANY = pl.BlockSpec(memory_space=pl.ANY)


def _all_gather(xs, *, name):
    n = len(xs)

    def body(*refs):
        x_refs, out_refs = refs[:n], refs[n:2 * n]
        send_sems, recv_sems, local_sems = refs[2 * n:]
        mx, my, mc = lax.axis_index("x"), lax.axis_index("y"), lax.axis_index("c")
        me, sibling = (mx, my, mc), (mx, my, 1 - mc)
        chips = [(1 - mx, my), (mx, 1 - my), (1 - mx, 1 - my)]

        def copy(a, k, block, to, own=False):
            px, py, pc = block
            dst = out_refs[a].at[4 * px + 2 * py + pc]
            return pltpu.make_async_remote_copy(
                src_ref=x_refs[a] if own else dst, dst_ref=dst,
                send_sem=send_sems.at[7 * a + k], recv_sem=recv_sems.at[7 * a + k],
                device_id=to, device_id_type=MESH)

        mine = [pltpu.make_async_copy(x_refs[a], out_refs[a].at[4 * mx + 2 * my + mc], local_sems.at[a])
                for a in range(n)]
        for cp in mine:
            cp.start()
        first = []
        for j, chip in enumerate(chips):
            first += [copy(a, 1 + j, me, (*chip, mc), own=True) for a in range(n)]
        first += [copy(a, 0, me, sibling, own=True) for a in range(n)]
        for cp in first:
            cp.start()
        passed = []
        for j, chip in enumerate(chips):
            for a in range(n):
                copy(a, 1 + j, (*chip, mc), me).wait_recv()
                passed.append(copy(a, 4 + j, (*chip, mc), sibling))
                passed[-1].start()
        for a in range(n):
            copy(a, 0, sibling, me).wait_recv()
        for j, chip in enumerate(chips):
            for a in range(n):
                copy(a, 4 + j, (*chip, 1 - mc), me).wait_recv()
        for cp in first + passed:
            cp.wait_send()
        for cp in mine:
            cp.wait()

    return pl.pallas_call(
        body, name=name, in_specs=[_ANY] * n, out_specs=[_ANY] * n,
        out_shape=[jax.ShapeDtypeStruct((N_DEV,) + x.shape, x.dtype) for x in xs],
        scratch_shapes=[pltpu.SemaphoreType.DMA((7 * n,)), pltpu.SemaphoreType.DMA((7 * n,)),
                        pltpu.SemaphoreType.DMA((n,))],
    )(*xs)


def _all_to_all(ps, *, name):
    n = len(ps)

    def body(*refs):
        p_refs, out_refs = refs[:n], refs[n:2 * n]
        send_sems, recv_sems, local_sems = refs[2 * n:]
        mx, my, mc = lax.axis_index("x"), lax.axis_index("y"), lax.axis_index("c")
        me = 4 * mx + 2 * my + mc
        mine = [pltpu.make_async_copy(p_refs[a].at[me], out_refs[a].at[me], local_sems.at[a]) for a in range(n)]
        for cp in mine:
            cp.start()
        copies = []
        for k in range(1, N_DEV):
            px, py, pc = mx ^ ((k >> 2) & 1), my ^ ((k >> 1) & 1), mc ^ (k & 1)
            for a in range(n):
                copies.append(pltpu.make_async_remote_copy(
                    src_ref=p_refs[a].at[4 * px + 2 * py + pc], dst_ref=out_refs[a].at[me],
                    send_sem=send_sems.at[7 * a + k - 1], recv_sem=recv_sems.at[7 * a + k - 1],
                    device_id=(px, py, pc), device_id_type=MESH))
        for cp in copies:
            cp.start()
        for cp in copies:
            cp.wait()
        for cp in mine:
            cp.wait()

    return pl.pallas_call(
        body, name=name, in_specs=[_ANY] * n, out_specs=[_ANY] * n,
        out_shape=[jax.ShapeDtypeStruct(p.shape, p.dtype) for p in ps],
        scratch_shapes=[pltpu.SemaphoreType.DMA((7 * n,)), pltpu.SemaphoreType.DMA((7 * n,)),
                        pltpu.SemaphoreType.DMA((n,))],
    )(*ps)


ROW_TILE_ELEMS = 256 * 1024


def _row_tile(R, C):
    for t in (2048, 1024, 512, 256, 128, 64, 32, 16):
        if R % t == 0 and t * C <= ROW_TILE_ELEMS:
            return t
    raise ValueError((R, C))


def _sum_slots(gs, *, name):
    S, R, C = gs.shape
    tr = _row_tile(R, C)

    def body(g_ref, o_ref):
        g = g_ref[0].astype(F32)
        for s in range(1, S):
            g = g + g_ref[s].astype(F32)
        o_ref[...] = g

    return pl.pallas_call(
        body, name=name, grid=(R // tr,),
        in_specs=[pl.BlockSpec((S, tr, C), lambda i: (0, i, 0))],
        out_specs=pl.BlockSpec((tr, C), lambda i: (i, 0)),
        out_shape=jax.ShapeDtypeStruct((R, C), F32),
        compiler_params=_params("parallel"),
    )(gs)


def _adamw(gs, w, m, v, *, name):
    S, R, C = gs.shape
    tr = _row_tile(R, C)

    def body(g_ref, w_ref, m_ref, v_ref, g_out, d_out, m_out, v_out):
        g = g_ref[0].astype(F32)
        for s in range(1, S):
            g = g + g_ref[s].astype(F32)
        m_new = ADAM_B1 * m_ref[...] + (1.0 - ADAM_B1) * g
        v_new = ADAM_B2 * v_ref[...] + (1.0 - ADAM_B2) * (g * g)
        m_hat = m_new / (1.0 - ADAM_B1 ** ADAM_STEP)
        v_hat = v_new / (1.0 - ADAM_B2 ** ADAM_STEP)
        g_out[...] = g
        d_out[...] = -ADAM_LR * (m_hat / (jnp.sqrt(v_hat) + ADAM_EPS) + ADAM_WD * w_ref[...])
        m_out[...] = m_new
        v_out[...] = v_new

    blk = pl.BlockSpec((tr, C), lambda i: (i, 0))
    sh = jax.ShapeDtypeStruct((R, C), F32)
    return pl.pallas_call(
        body, name=name, grid=(R // tr,),
        in_specs=[pl.BlockSpec((S, tr, C), lambda i: (0, i, 0)), blk, blk, blk],
        out_specs=[blk, blk, blk, blk], out_shape=[sh, sh, sh, sh],
        compiler_params=_params("parallel"),
    )(gs, w, m, v)


PIECE = 16 * LANES


def _pack(arrs, dtype, lead, row_mult):
    lead_shape = arrs[0].shape[:lead]
    parts, meta, off = [], [], 0
    for a in arrs:
        size = math.prod(a.shape[lead:])
        padded = -(-size // PIECE) * PIECE
        flat = a.astype(dtype).reshape(lead_shape + (size,))
        if padded != size:
            flat = jnp.concatenate([flat, jnp.zeros(lead_shape + (padded - size,), dtype)], -1)
        parts.append(flat)
        meta.append((off, size, a.shape[lead:]))
        off += padded
    total = -(-off // (row_mult * LANES)) * (row_mult * LANES)
    if total != off:
        parts.append(jnp.zeros(lead_shape + (total - off,), dtype))
    return jnp.concatenate(parts, -1).reshape(lead_shape + (total // LANES, LANES)), meta


def _unpack(packed, meta, lead):
    lead_shape = packed.shape[:lead]
    flat = packed.reshape(lead_shape + (-1,))
    return [flat[..., off:off + size].reshape(lead_shape + shape) for off, size, shape in meta]


ARG_NAMES = ['x', 'mem', 'positions', 'norm_mix_g', 'norm_xa_g', 'norm_mem_g', 'xa_wq', 'xa_wkv', 'xa_wo', 'norm_ffn_g', 'ffn_w_up', 'ffn_conv_w', 'ffn_conv_b', 'ffn_w_down', 'pc_w_in', 'pool_w', 'pool_scale', 'conv_dw_w', 'conv_dw_b', 'conv_ln_g', 'conv_ln_b', 'pc_w_out', 'mla_w_dq_dkv', 'mla_q_norm_g', 'mla_w_uq', 'mla_kv_norm_g', 'mla_w_ukv', 'mla_w_o', 'final_norm_g', 'loss_target']
WEIGHTS = ARG_NAMES[3:29]
BIG = {'xa_wq': 1, 'xa_wkv': 2, 'xa_wo': 1, 'ffn_w_up': 2, 'ffn_w_down': 1, 'pc_w_in': 2, 'pc_w_out': 1,
       'mla_w_dq_dkv': 1, 'mla_w_uq': 2, 'mla_w_ukv': 2, 'mla_w_o': 1}
SMALL_SHARDED = {'ffn_conv_w': 2, 'conv_dw_w': 2, 'mla_q_norm_g': 1, 'mla_kv_norm_g': 1}
REPLICATED = [n for n in WEIGHTS if n not in BIG and n not in SMALL_SHARDED]


def _from_slots(g, axis):
    t = jnp.moveaxis(g, 0, axis)
    return t.reshape(t.shape[:axis] + (t.shape[axis] * t.shape[axis + 1],) + t.shape[axis + 2:])


def _to_slots(full, axis):
    n = full.shape[axis] // N_DEV
    t = full.reshape(full.shape[:axis] + (N_DEV, n) + full.shape[axis + 1:])
    return jnp.moveaxis(t, axis, 0)


def kernel(x, mem, positions, norm_mix_g, norm_xa_g, norm_mem_g, xa_wq, xa_wkv, xa_wo, norm_ffn_g, ffn_w_up, ffn_conv_w, ffn_conv_b, ffn_w_down, pc_w_in, pool_w, pool_scale, conv_dw_w, conv_dw_b, conv_ln_g, conv_ln_b, pc_w_out, mla_w_dq_dkv, mla_q_norm_g, mla_w_uq, mla_kv_norm_g, mla_w_ukv, mla_w_o, final_norm_g, loss_target, m_norm_mix_g, m_norm_xa_g, m_norm_mem_g, m_xa_wq, m_xa_wkv, m_xa_wo, m_norm_ffn_g, m_ffn_w_up, m_ffn_conv_w, m_ffn_conv_b, m_ffn_w_down, m_pc_w_in, m_pool_w, m_pool_scale, m_conv_dw_w, m_conv_dw_b, m_conv_ln_g, m_conv_ln_b, m_pc_w_out, m_mla_w_dq_dkv, m_mla_q_norm_g, m_mla_w_uq, m_mla_kv_norm_g, m_mla_w_ukv, m_mla_w_o, m_final_norm_g, v_norm_mix_g, v_norm_xa_g, v_norm_mem_g, v_xa_wq, v_xa_wkv, v_xa_wo, v_norm_ffn_g, v_ffn_w_up, v_ffn_conv_w, v_ffn_conv_b, v_ffn_w_down, v_pc_w_in, v_pool_w, v_pool_scale, v_conv_dw_w, v_conv_dw_b, v_conv_ln_g, v_conv_ln_b, v_pc_w_out, v_mla_w_dq_dkv, v_mla_q_norm_g, v_mla_w_uq, v_mla_kv_norm_g, v_mla_w_ukv, v_mla_w_o, v_final_norm_g):
    args = (x, mem, positions, norm_mix_g, norm_xa_g, norm_mem_g, xa_wq, xa_wkv, xa_wo, norm_ffn_g, ffn_w_up, ffn_conv_w, ffn_conv_b, ffn_w_down, pc_w_in, pool_w, pool_scale, conv_dw_w, conv_dw_b, conv_ln_g, conv_ln_b, pc_w_out, mla_w_dq_dkv, mla_q_norm_g, mla_w_uq, mla_kv_norm_g, mla_w_ukv, mla_w_o, final_norm_g, loss_target)
    a = dict(zip(ARG_NAMES, args))
    mom = dict(zip(WEIGHTS, (m_norm_mix_g, m_norm_xa_g, m_norm_mem_g, m_xa_wq, m_xa_wkv, m_xa_wo, m_norm_ffn_g, m_ffn_w_up, m_ffn_conv_w, m_ffn_conv_b, m_ffn_w_down, m_pc_w_in, m_pool_w, m_pool_scale, m_conv_dw_w, m_conv_dw_b, m_conv_ln_g, m_conv_ln_b, m_pc_w_out, m_mla_w_dq_dkv, m_mla_q_norm_g, m_mla_w_uq, m_mla_kv_norm_g, m_mla_w_ukv, m_mla_w_o, m_final_norm_g)))
    var = dict(zip(WEIGHTS, (v_norm_mix_g, v_norm_xa_g, v_norm_mem_g, v_xa_wq, v_xa_wkv, v_xa_wo, v_norm_ffn_g, v_ffn_w_up, v_ffn_conv_w, v_ffn_conv_b, v_ffn_w_down, v_pc_w_in, v_pool_w, v_pool_scale, v_conv_dw_w, v_conv_dw_b, v_conv_ln_g, v_conv_ln_b, v_pc_w_out, v_mla_w_dq_dkv, v_mla_q_norm_g, v_mla_w_uq, v_mla_kv_norm_g, v_mla_w_ukv, v_mla_w_o, v_final_norm_g)))
    me = 4 * lax.axis_index("x") + 2 * lax.axis_index("y") + lax.axis_index("c")

    big_all = _all_gather([a[n].astype(CD) for n in BIG], name="gather_weights")
    sm_pack, sm_meta = _pack([a[n] for n in SMALL_SHARDED], F32, 0, 8)
    sm_all = _unpack(_all_gather([sm_pack], name="gather_small")[0], sm_meta, 1)
    W = {n: a[n] for n in REPLICATED}
    for (n, ax), g in zip(BIG.items(), big_all):
        W[n] = _from_slots(g, ax)
    for (n, ax), g in zip(SMALL_SHARDED.items(), sm_all):
        W[n] = _from_slots(g, ax)

    loss, dx, G = _local_step(x[0], mem[0], positions[0], loss_target[0], W)

    recv = _all_to_all([_to_slots(G[n], ax).astype(CD) for n, ax in BIG.items()], name="scatter_grads")
    out = {}
    for n, r in zip(BIG, recv):
        shape = a[n].shape
        rows = lambda t: t.reshape(-1, shape[-1])
        res = _adamw(r.reshape(N_DEV, -1, shape[-1]), rows(a[n]), rows(mom[n]), rows(var[n]), name=f"adamw_{n}")
        out[n] = tuple(t.reshape(shape) for t in res)

    small_names = REPLICATED + list(SMALL_SHARDED)
    spack, smeta = _pack([G[n] for n in small_names] + [loss], F32, 0, 256)
    stot = _unpack(_sum_slots(_all_gather([spack], name="gather_small_grads")[0], name="sum_small_grads"), smeta, 0)
    loss_total = stot[-1][0, 0]
    gsm = dict(zip(small_names, stot[:-1]))
    for n, ax in SMALL_SHARDED.items():
        width = a[n].shape[ax]
        gsm[n] = lax.dynamic_slice_in_dim(gsm[n], me * width, width, ax)
    g1, meta1 = _pack([gsm[n] for n in small_names], F32, 0, 256)
    w1, _ = _pack([a[n] for n in small_names], F32, 0, 256)
    m1, _ = _pack([mom[n] for n in small_names], F32, 0, 256)
    v1, _ = _pack([var[n] for n in small_names], F32, 0, 256)
    res = [_unpack(r, meta1, 0) for r in _adamw(g1[None], w1, m1, v1, name="adamw_small")]
    for i, n in enumerate(small_names):
        out[n] = tuple(r[i] for r in res)

    return (loss_total, dx[None],
            *[out[n][0] for n in WEIGHTS], *[out[n][1] for n in WEIGHTS],
            *[out[n][2] for n in WEIGHTS], *[out[n][3] for n in WEIGHTS])
```

```python
import functools
import math

import jax
import jax.numpy as jnp
from jax import lax
from jax.experimental import pallas as pl
from jax.experimental.pallas import tpu as pltpu

F32 = jnp.float32
CD = jnp.bfloat16
EPS = 1e-6
NEG = -1e30
N_DEV = 8
LANES = 128
HALO = 32

D_MODEL = 1024
DEPTH = 4
XA_HEADS = 4
XA_DH = 256
MEM_LEN = 256
POOL_WINDOWS = (2, 4, 8, 16)
CONV_K = 31
FFN_K = 3
D_FF = 2816
MLA_HEADS = 16
QK_NOPE = 64
QK_ROPE = 32
V_HEAD = 64
Q_LORA = 384
KV_LORA = 256
ROPE_THETA = 10000.0
MLA_SCALE = 1.0 / math.sqrt(QK_NOPE + QK_ROPE)
XA_SCALE = XA_DH ** -0.5

ADAM_LR = 0.001
ADAM_B1 = 0.9
ADAM_B2 = 0.999
ADAM_EPS = 1e-08
ADAM_WD = 0.01
ADAM_STEP = 10

NT = (((1,), (1,)), ((), ()))
TN = (((0,), (0,)), ((), ()))
MESH = pl.DeviceIdType.MESH


def _tile(n, target):
    if n <= target:
        return n
    best = None
    for t in range(LANES, target + 1, LANES):
        if n % t == 0:
            best = t
    assert best is not None, (n, target)
    return best


def _params(*sem):
    return pltpu.CompilerParams(dimension_semantics=sem)


def _sigmoid(v):
    return 0.5 * jnp.tanh(0.5 * v) + 0.5


def _rms_bwd(x, gain, dh):
    r = lax.rsqrt(jnp.mean(x * x, axis=-1, keepdims=True) + EPS)
    xhat = x * r
    dxhat = dh * gain
    dx = r * (dxhat - xhat * jnp.mean(dxhat * xhat, axis=-1, keepdims=True))
    return dx, dh * xhat


def _weight(w):
    if not isinstance(w, tuple):
        return w, w.shape, pl.BlockSpec
    arr, layer = w

    def spec(block, imap):
        return pl.BlockSpec((None,) + tuple(block), lambda *a: (layer,) + tuple(imap(*a)))

    return arr, arr.shape[1:], spec


def _nmm(x, g, w, *, name, out_dtype, tm=512, tn_target=1024):
    M, K = x.shape
    w, (_, N), wspec = _weight(w)
    tm = min(tm, M)
    tn = _tile(N, tn_target)

    def body(x_ref, g_ref, w_ref, z_ref, h_ref):
        @pl.when(pl.program_id(1) == 0)
        def _():
            xf = x_ref[...]
            r = lax.rsqrt(jnp.mean(xf * xf, axis=-1, keepdims=True) + EPS)
            h_ref[...] = (xf * r * g_ref[...]).astype(h_ref.dtype)

        z_ref[...] = jnp.dot(h_ref[...], w_ref[...], preferred_element_type=F32).astype(z_ref.dtype)

    return pl.pallas_call(
        body, name=name, grid=(M // tm, N // tn),
        in_specs=[pl.BlockSpec((tm, K), lambda i, j: (i, 0)),
                  pl.BlockSpec((1, K), lambda i, j: (0, 0)),
                  wspec((K, tn), lambda i, j: (0, j))],
        out_specs=[pl.BlockSpec((tm, tn), lambda i, j: (i, j)),
                   pl.BlockSpec((tm, K), lambda i, j: (i, 0))],
        out_shape=[jax.ShapeDtypeStruct((M, N), out_dtype), jax.ShapeDtypeStruct((M, K), CD)],
        compiler_params=_params("parallel", "arbitrary"),
    )(x, g, w)


def _mm_res(a, w, res, *, name, tm=512, tn_target=1024):
    M, K = a.shape
    w, (_, N), wspec = _weight(w)
    tm = min(tm, M)
    tn = _tile(N, tn_target)

    def body(a_ref, w_ref, r_ref, o_ref):
        o_ref[...] = r_ref[...] + jnp.dot(a_ref[...].astype(CD), w_ref[...], preferred_element_type=F32)

    return pl.pallas_call(
        body, name=name, grid=(M // tm, N // tn),
        in_specs=[pl.BlockSpec((tm, K), lambda i, j: (i, 0)),
                  wspec((K, tn), lambda i, j: (0, j)),
                  pl.BlockSpec((tm, tn), lambda i, j: (i, j))],
        out_specs=pl.BlockSpec((tm, tn), lambda i, j: (i, j)),
        out_shape=jax.ShapeDtypeStruct((M, N), F32),
        compiler_params=_params("parallel", "arbitrary"),
    )(a, w, res)


def _mm_nt(a, w, *, name, out_dtype, tm=512, tn_target=1024):
    M, K = a.shape
    w, (N, _), wspec = _weight(w)
    tm = min(tm, M)
    tn = _tile(N, tn_target)

    def body(a_ref, w_ref, o_ref):
        o_ref[...] = lax.dot_general(a_ref[...].astype(CD), w_ref[...], NT,
                                     preferred_element_type=F32).astype(o_ref.dtype)

    return pl.pallas_call(
        body, name=name, grid=(M // tm, N // tn),
        in_specs=[pl.BlockSpec((tm, K), lambda i, j: (i, 0)),
                  wspec((tn, K), lambda i, j: (j, 0))],
        out_specs=pl.BlockSpec((tm, tn), lambda i, j: (i, j)),
        out_shape=jax.ShapeDtypeStruct((M, N), out_dtype),
        compiler_params=_params("parallel", "arbitrary"),
    )(a, w)


def _mm_nt_normbwd(gy, w, x, gain, dres, *, name, tm=512, tk_target=1408):
    M, K = gy.shape
    w, (D, _), wspec = _weight(w)
    tm = min(tm, M)
    tk = _tile(K, tk_target)
    nk = K // tk

    def body(g_ref, w_ref, x_ref, gain_ref, dres_ref, dx_ref, dg_ref, acc):
        i, k = pl.program_id(0), pl.program_id(1)

        @pl.when(k == 0)
        def _():
            acc[...] = jnp.zeros_like(acc)

        acc[...] += lax.dot_general(g_ref[...].astype(CD), w_ref[...], NT, preferred_element_type=F32)

        @pl.when(k == nk - 1)
        def _():
            dx, dg_rows = _rms_bwd(x_ref[...], gain_ref[...], acc[...])
            dx_ref[...] = dres_ref[...] + dx

            @pl.when(i == 0)
            def _():
                dg_ref[...] = jnp.zeros_like(dg_ref)

            dg_ref[...] += jnp.sum(dg_rows, axis=0, keepdims=True)

    return pl.pallas_call(
        body, name=name, grid=(M // tm, nk),
        in_specs=[pl.BlockSpec((tm, tk), lambda i, k: (i, k)),
                  wspec((D, tk), lambda i, k: (0, k)),
                  pl.BlockSpec((tm, D), lambda i, k: (i, 0)),
                  pl.BlockSpec((1, D), lambda i, k: (0, 0)),
                  pl.BlockSpec((tm, D), lambda i, k: (i, 0))],
        out_specs=[pl.BlockSpec((tm, D), lambda i, k: (i, 0)),
                   pl.BlockSpec((1, D), lambda i, k: (0, 0))],
        out_shape=[jax.ShapeDtypeStruct((M, D), F32), jax.ShapeDtypeStruct((1, D), F32)],
        scratch_shapes=[pltpu.VMEM((tm, D), F32)],
        compiler_params=_params("arbitrary", "arbitrary"),
    )(gy, w, x, gain, dres)


def _mm_tn(a, g, *, name, tt=512, tk_target=1024, tn_target=1024):
    T, K = a.shape
    N = g.shape[1]
    tt = min(tt, T)
    tk = _tile(K, tk_target)
    tn = _tile(N, tn_target)

    def body(a_ref, g_ref, o_ref):
        @pl.when(pl.program_id(2) == 0)
        def _():
            o_ref[...] = jnp.zeros_like(o_ref)

        o_ref[...] += lax.dot_general(a_ref[...].astype(CD), g_ref[...].astype(CD), TN,
                                      preferred_element_type=F32)

    return pl.pallas_call(
        body, name=name, grid=(K // tk, N // tn, T // tt),
        in_specs=[pl.BlockSpec((tt, tk), lambda i, j, t: (t, i)),
                  pl.BlockSpec((tt, tn), lambda i, j, t: (t, j))],
        out_specs=pl.BlockSpec((tk, tn), lambda i, j, t: (i, j)),
        out_shape=jax.ShapeDtypeStruct((K, N), F32),
        compiler_params=_params("parallel", "parallel", "arbitrary"),
    )(a, g)


POOL_W = 512
CONV_W = 512
POOL_GROUP = 128


def _fill_ext(ext, prev_ref, cur_ref, next_ref, i, n, tt):
    zeros = jnp.zeros((HALO,) + ext.shape[1:], F32)
    ext[pl.ds(0, HALO), :] = jnp.where(i > 0, prev_ref[...].astype(F32), zeros)
    ext[pl.ds(HALO, tt), :] = cur_ref[...].astype(F32)
    if next_ref is not None:
        ext[pl.ds(HALO + tt, HALO), :] = jnp.where(i < n - 1, next_ref[...].astype(F32), zeros)


def _mixer_fwd(z, pool_w, pool_scale, dw_w, dw_b, ln_g, ln_b, *, name, tt=256):
    T = z.shape[0]
    tt = min(tt, T)
    n = T // tt
    hb = tt // HALO

    def body(zp_ref, z_ref, pw_ref, ps_ref, w_ref, b_ref, g_ref, bb_ref, o_ref, ext, gl):
        i = pl.program_id(0)
        _fill_ext(ext, zp_ref, z_ref, None, i, n, tt)
        t_glob = i * tt + lax.broadcasted_iota(jnp.int32, (tt, 1), 0)
        for gi, win in enumerate(POOL_WINDOWS):
            cols = pl.ds(gi * POOL_GROUP, POOL_GROUP)
            u = ext[pl.ds(HALO, tt), cols]
            s = u
            for j in range(1, win):
                s = s + ext[pl.ds(HALO - j, tt), cols]
            cnt = jnp.minimum(t_glob + 1, win).astype(F32)
            pooled = s / cnt - u
            ya = jnp.dot(pooled.astype(CD), pw_ref[gi].astype(CD), preferred_element_type=F32)
            o_ref[:, cols] = (ya * ps_ref[:, cols]).astype(o_ref.dtype)
        a = ext[:, pl.ds(POOL_W, CONV_W)]
        b = ext[:, pl.ds(POOL_W + CONV_W, CONV_W)]
        gl[...] = a * _sigmoid(b)
        cv = jnp.zeros((tt, CONV_W), F32) + b_ref[...]
        for j in range(CONV_K):
            cv = cv + w_ref[pl.ds(j, 1), :] * gl[pl.ds(HALO - (CONV_K - 1) + j, tt), :]
        mu = jnp.mean(cv, axis=-1, keepdims=True)
        xc = cv - mu
        yn = xc * lax.rsqrt(jnp.mean(xc * xc, axis=-1, keepdims=True) + EPS) * g_ref[...] + bb_ref[...]
        o_ref[:, pl.ds(POOL_W, CONV_W)] = (yn * _sigmoid(yn)).astype(o_ref.dtype)

    C = z.shape[1]
    full = lambda shape: pl.BlockSpec(shape, lambda i: (0,) * len(shape))
    return pl.pallas_call(
        body, name=name, grid=(n,),
        in_specs=[pl.BlockSpec((HALO, C), lambda i: (jnp.maximum(i * hb - 1, 0), 0)),
                  pl.BlockSpec((tt, C), lambda i: (i, 0)),
                  full((4, POOL_GROUP, POOL_GROUP)), full((1, POOL_W)), full((CONV_K + 1, CONV_W)),
                  full((1, CONV_W)), full((1, CONV_W)), full((1, CONV_W))],
        out_specs=pl.BlockSpec((tt, POOL_W + CONV_W), lambda i: (i, 0)),
        out_shape=jax.ShapeDtypeStruct((T, POOL_W + CONV_W), CD),
        scratch_shapes=[pltpu.VMEM((tt + HALO, C), F32), pltpu.VMEM((tt + HALO, CONV_W), F32)],
        compiler_params=_params("parallel"),
    )(z, z, pool_w, pool_scale, dw_w, dw_b, ln_g, ln_b)


def _mixer_bwd(z, dy, pool_w, pool_scale, dw_w, dw_b, ln_g, ln_b, *, name, tt=256):
    T, C = z.shape
    tt = min(tt, T)
    n = T // tt
    hb = tt // HALO
    R = tt + HALO

    def body(zp_ref, z_ref, zn_ref, dy_ref, dyn_ref, pw_ref, ps_ref, w_ref, b_ref, g_ref, bb_ref,
             dz_ref, dpw_ref, dps_ref, dw_ref, db_ref, dg_ref, dbb_ref, ext, gl, dye, dcv, dpe):
        i = pl.program_id(0)

        @pl.when(i == 0)
        def _():
            for r in (dpw_ref, dps_ref, dw_ref, db_ref, dg_ref, dbb_ref):
                r[...] = jnp.zeros_like(r)

        _fill_ext(ext, zp_ref, z_ref, zn_ref, i, n, tt)
        dye[pl.ds(0, tt), :] = dy_ref[...]
        dye[pl.ds(tt, HALO), :] = jnp.where(i < n - 1, dyn_ref[...], jnp.zeros((HALO, 2 * POOL_W), F32))
        t_glob = i * tt + lax.broadcasted_iota(jnp.int32, (R, 1), 0)

        for gi, win in enumerate(POOL_WINDOWS):
            cols = pl.ds(gi * POOL_GROUP, POOL_GROUP)
            u = ext[pl.ds(HALO, tt), cols]
            s = u
            for j in range(1, win):
                s = s + ext[pl.ds(HALO - j, tt), cols]
            cnt = jnp.minimum(t_glob + 1, win).astype(F32)
            pooled = (s / cnt[:tt] - u).astype(CD)
            pw = pw_ref[gi].astype(CD)
            dya = dye[:, cols]
            mm = jnp.dot(pooled, pw, preferred_element_type=F32)
            dps_ref[:, cols] += jnp.sum(dya[:tt] * mm, axis=0, keepdims=True)
            dm = (dya * ps_ref[:, cols]).astype(CD)
            dpw_ref[gi] += lax.dot_general(pooled, dm[:tt], TN, preferred_element_type=F32)
            dpool = lax.dot_general(dm, pw, NT, preferred_element_type=F32)
            dpe[...] = dpool / cnt
            du = -dpool[:tt]
            for j in range(win):
                du = du + dpe[pl.ds(j, tt), :]
            dz_ref[:, cols] = du.astype(dz_ref.dtype)

        a = ext[:, pl.ds(POOL_W, CONV_W)]
        b = ext[:, pl.ds(POOL_W + CONV_W, CONV_W)]
        sb = _sigmoid(b)
        gl[...] = a * sb
        cv = jnp.zeros((R, CONV_W), F32) + b_ref[...]
        for j in range(CONV_K):
            cv = cv + w_ref[pl.ds(j, 1), :] * gl[pl.ds(HALO - (CONV_K - 1) + j, R), :]
        mu = jnp.mean(cv, axis=-1, keepdims=True)
        xc = cv - mu
        rstd = lax.rsqrt(jnp.mean(xc * xc, axis=-1, keepdims=True) + EPS)
        xhat = xc * rstd
        yn = xhat * g_ref[...] + bb_ref[...]
        sy = _sigmoid(yn)
        dyn = dye[:, pl.ds(POOL_W, CONV_W)] * (sy * (1.0 + yn * (1.0 - sy)))
        dg_ref[...] += jnp.sum(dyn[:tt] * xhat[:tt], axis=0, keepdims=True)
        dbb_ref[...] += jnp.sum(dyn[:tt], axis=0, keepdims=True)
        dxh = dyn * g_ref[...]
        dcv_v = rstd * (dxh - jnp.mean(dxh, axis=-1, keepdims=True)
                        - xhat * jnp.mean(dxh * xhat, axis=-1, keepdims=True))
        dcv[...] = dcv_v
        db_ref[...] += jnp.sum(dcv_v[:tt], axis=0, keepdims=True)
        dgl = jnp.zeros((tt, CONV_W), F32)
        for j in range(CONV_K):
            dgl = dgl + w_ref[pl.ds(j, 1), :] * dcv[pl.ds(CONV_K - 1 - j, tt), :]
            dw_ref[pl.ds(j, 1), :] += jnp.sum(dcv_v[:tt] * gl[pl.ds(HALO - (CONV_K - 1) + j, tt), :],
                                              axis=0, keepdims=True)
        a_t = a[HALO:HALO + tt]
        sb_t = sb[HALO:HALO + tt]
        dz_ref[:, pl.ds(POOL_W, CONV_W)] = (dgl * sb_t).astype(dz_ref.dtype)
        dz_ref[:, pl.ds(POOL_W + CONV_W, CONV_W)] = (dgl * a_t * sb_t * (1.0 - sb_t)).astype(dz_ref.dtype)

    full = lambda shape: pl.BlockSpec(shape, lambda i: (0,) * len(shape))
    nb = T // HALO
    outs = pl.pallas_call(
        body, name=name, grid=(n,),
        in_specs=[pl.BlockSpec((HALO, C), lambda i: (jnp.maximum(i * hb - 1, 0), 0)),
                  pl.BlockSpec((tt, C), lambda i: (i, 0)),
                  pl.BlockSpec((HALO, C), lambda i: (jnp.minimum((i + 1) * hb, nb - 1), 0)),
                  pl.BlockSpec((tt, 2 * POOL_W), lambda i: (i, 0)),
                  pl.BlockSpec((HALO, 2 * POOL_W), lambda i: (jnp.minimum((i + 1) * hb, nb - 1), 0)),
                  full((4, POOL_GROUP, POOL_GROUP)), full((1, POOL_W)), full((CONV_K + 1, CONV_W)),
                  full((1, CONV_W)), full((1, CONV_W)), full((1, CONV_W))],
        out_specs=[pl.BlockSpec((tt, C), lambda i: (i, 0)),
                   full((4, POOL_GROUP, POOL_GROUP)), full((1, POOL_W)), full((CONV_K + 1, CONV_W)),
                   full((1, CONV_W)), full((1, CONV_W)), full((1, CONV_W))],
        out_shape=[jax.ShapeDtypeStruct((T, C), CD),
                   jax.ShapeDtypeStruct((4, POOL_GROUP, POOL_GROUP), F32),
                   jax.ShapeDtypeStruct((1, POOL_W), F32),
                   jax.ShapeDtypeStruct((CONV_K + 1, CONV_W), F32),
                   jax.ShapeDtypeStruct((1, CONV_W), F32),
                   jax.ShapeDtypeStruct((1, CONV_W), F32),
                   jax.ShapeDtypeStruct((1, CONV_W), F32)],
        scratch_shapes=[pltpu.VMEM((tt + 2 * HALO, C), F32), pltpu.VMEM((tt + 2 * HALO, CONV_W), F32),
                        pltpu.VMEM((R, 2 * POOL_W), F32), pltpu.VMEM((R, CONV_W), F32),
                        pltpu.VMEM((R, POOL_GROUP), F32)],
        compiler_params=_params("arbitrary"),
    )(z, z, z, dy, dy, pool_w, pool_scale, dw_w, dw_b, ln_g, ln_b)
    return outs


CHUNK_HALO = 16
FFN_ROWS = 64
FFN_LANES = 128


def _rows(cur, prev, nxt, r, rb, before, after, cols, n_r, first, last):
    lo, hi = r * rb - before, r * rb + rb + after
    tt = n_r * rb
    parts = []
    if lo < 0:
        p = prev[pl.ds(HALO + lo, -lo), cols]
        parts.append(jnp.where(first, jnp.zeros_like(p), p))
        lo = 0
    parts.append(cur[pl.ds(lo, min(hi, tt) - lo), cols])
    if hi > tt:
        p = nxt[pl.ds(0, hi - tt), cols]
        parts.append(jnp.where(last, jnp.zeros_like(p), p))
    return parts[0] if len(parts) == 1 else jnp.concatenate(parts, axis=0)


def _ffn_mid_fwd(up, cw, cb, *, name, tt=512):
    T = up.shape[0]
    tt = min(tt, T)
    n = T // tt
    hb = tt // HALO
    RB, CW, HB = min(FFN_ROWS, tt), FFN_LANES, CHUNK_HALO
    n_r = tt // RB

    def body(a_ref, gp_ref, g_ref, w_ref, b_ref, o_ref):
        first = pl.program_id(0) == 0

        def col_chunk(c, carry):
            cols = pl.ds(pl.multiple_of(c * CW, CW), CW)
            w = w_ref[:, cols]
            b = b_ref[:, cols]
            for r in range(n_r):
                v = _rows(g_ref, gp_ref, None, r, RB, HB, 0, cols, n_r, first, None).astype(F32)
                gc = b + w[0:1] * v[HB - 2:HB - 2 + RB] + w[1:2] * v[HB - 1:HB - 1 + RB] + w[2:3] * v[HB:HB + RB]
                a = a_ref[pl.ds(r * RB, RB), cols].astype(F32)
                o_ref[pl.ds(r * RB, RB), cols] = (gc * _sigmoid(gc) * a).astype(o_ref.dtype)
            return carry

        lax.fori_loop(0, D_FF // CW, col_chunk, 0)

    return pl.pallas_call(
        body, name=name, grid=(n,),
        in_specs=[pl.BlockSpec((tt, D_FF), lambda i: (i, 0)),
                  pl.BlockSpec((HALO, D_FF), lambda i: (jnp.maximum(i * hb - 1, 0), 1)),
                  pl.BlockSpec((tt, D_FF), lambda i: (i, 1)),
                  pl.BlockSpec((8, D_FF), lambda i: (0, 0)),
                  pl.BlockSpec((1, D_FF), lambda i: (0, 0))],
        out_specs=pl.BlockSpec((tt, D_FF), lambda i: (i, 0)),
        out_shape=jax.ShapeDtypeStruct((T, D_FF), CD),
        compiler_params=_params("parallel"),
    )(up, up, up, cw, cb)


def _ffn_mid_bwd(up, dact, cw, cb, *, name, tt=512):
    T = up.shape[0]
    tt = min(tt, T)
    n = T // tt
    hb = tt // HALO
    nb = T // HALO
    RB, CW, HB = min(FFN_ROWS, tt), FFN_LANES, CHUNK_HALO
    n_r = tt // RB
    RE = RB + 8

    def body(a_ref, an_ref, gp_ref, g_ref, gn_ref, d_ref, dn_ref, w_ref, b_ref, dup_ref, dw_ref, db_ref, acc):
        i = pl.program_id(0)
        first, last = i == 0, i == n - 1

        @pl.when(first)
        def _():
            dw_ref[...] = jnp.zeros_like(dw_ref)
            db_ref[...] = jnp.zeros_like(db_ref)

        def col_chunk(c, carry):
            cols = pl.ds(pl.multiple_of(c * CW, CW), CW)
            w = w_ref[:, cols]
            b = b_ref[:, cols]
            part = [jnp.zeros((8, CW), F32) for _ in range(FFN_K + 1)]
            for r in range(n_r):
                v = _rows(g_ref, gp_ref, gn_ref, r, RB, HB, HB, cols, n_r, first, last).astype(F32)
                gs = [v[HB - 2 + j:HB - 2 + j + RE] for j in range(FFN_K)]
                gc = b + w[0:1] * gs[0] + w[1:2] * gs[1] + w[2:3] * gs[2]
                sg = _sigmoid(gc)
                d = _rows(d_ref, None, dn_ref, r, RB, 0, HB, cols, n_r, None, last).astype(F32)[:RE]
                a = _rows(a_ref, None, an_ref, r, RB, 0, HB, cols, n_r, None, last).astype(F32)[:RE]
                silu = gc * sg
                dgc = d * a * (sg + silu - silu * sg)
                dup_ref[pl.ds(r * RB, RB), cols] = (d[:RB] * silu[:RB]).astype(dup_ref.dtype)
                dg = w[2:3] * dgc[0:RB] + w[1:2] * dgc[1:RB + 1] + w[0:1] * dgc[2:RB + 2]
                dup_ref[pl.ds(r * RB, RB), pl.ds(pl.multiple_of(D_FF + c * CW, CW), CW)] = dg.astype(dup_ref.dtype)
                dgc_t = dgc[:RB]
                for j in range(FFN_K):
                    part[j] = part[j] + jnp.sum((dgc_t * gs[j][:RB]).reshape(RB // 8, 8, CW), axis=0)
                part[FFN_K] = part[FFN_K] + jnp.sum(dgc_t.reshape(RB // 8, 8, CW), axis=0)
            for j in range(FFN_K + 1):
                acc[pl.ds(8 * j, 8), cols] = part[j]
            return carry

        lax.fori_loop(0, D_FF // CW, col_chunk, 0)
        for j in range(FFN_K):
            dw_ref[pl.ds(j, 1), :] += jnp.sum(acc[pl.ds(8 * j, 8), :], axis=0, keepdims=True)
        db_ref[...] += jnp.sum(acc[pl.ds(8 * FFN_K, 8), :], axis=0, keepdims=True)

    nxt = lambda i: jnp.minimum((i + 1) * hb, nb - 1)
    return pl.pallas_call(
        body, name=name, grid=(n,),
        in_specs=[pl.BlockSpec((tt, D_FF), lambda i: (i, 0)),
                  pl.BlockSpec((HALO, D_FF), lambda i: (nxt(i), 0)),
                  pl.BlockSpec((HALO, D_FF), lambda i: (jnp.maximum(i * hb - 1, 0), 1)),
                  pl.BlockSpec((tt, D_FF), lambda i: (i, 1)),
                  pl.BlockSpec((HALO, D_FF), lambda i: (nxt(i), 1)),
                  pl.BlockSpec((tt, D_FF), lambda i: (i, 0)),
                  pl.BlockSpec((HALO, D_FF), lambda i: (nxt(i), 0)),
                  pl.BlockSpec((8, D_FF), lambda i: (0, 0)),
                  pl.BlockSpec((1, D_FF), lambda i: (0, 0))],
        out_specs=[pl.BlockSpec((tt, 2 * D_FF), lambda i: (i, 0)),
                   pl.BlockSpec((8, D_FF), lambda i: (0, 0)),
                   pl.BlockSpec((1, D_FF), lambda i: (0, 0))],
        out_shape=[jax.ShapeDtypeStruct((T, 2 * D_FF), CD),
                   jax.ShapeDtypeStruct((8, D_FF), F32),
                   jax.ShapeDtypeStruct((1, D_FF), F32)],
        scratch_shapes=[pltpu.VMEM((8 * (FFN_K + 1), D_FF), F32)],
        compiler_params=_params("arbitrary"),
    )(up, up, up, up, up, dact, dact, cw, cb)


def _xattn_probs(q, k):
    s = lax.dot_general(q, k, NT, preferred_element_type=F32) * XA_SCALE
    p = jnp.exp(s - jnp.max(s, axis=-1, keepdims=True))
    return p / jnp.sum(p, axis=-1, keepdims=True)


def _xattn_fwd(q, kv, *, name, tq=512):
    T = q.shape[0]
    tq = min(tq, T)

    def body(q_ref, kv_ref, o_ref):
        for h in range(XA_HEADS):
            cols = pl.ds(h * XA_DH, XA_DH)
            p = _xattn_probs(q_ref[:, cols], kv_ref[:, cols])
            v = kv_ref[:, pl.ds(D_MODEL + h * XA_DH, XA_DH)]
            o_ref[:, cols] = jnp.dot(p.astype(CD), v, preferred_element_type=F32).astype(o_ref.dtype)

    return pl.pallas_call(
        body, name=name, grid=(T // tq,),
        in_specs=[pl.BlockSpec((tq, D_MODEL), lambda i: (i, 0)),
                  pl.BlockSpec((MEM_LEN, 2 * D_MODEL), lambda i: (0, 0))],
        out_specs=pl.BlockSpec((tq, D_MODEL), lambda i: (i, 0)),
        out_shape=jax.ShapeDtypeStruct((T, D_MODEL), CD),
        compiler_params=_params("parallel"),
    )(q, kv)


def _xattn_bwd(q, kv, do, *, name, tq=512):
    T = q.shape[0]
    tq = min(tq, T)

    def body(q_ref, kv_ref, do_ref, dq_ref, dkv_ref):
        @pl.when(pl.program_id(0) == 0)
        def _():
            dkv_ref[...] = jnp.zeros_like(dkv_ref)

        for h in range(XA_HEADS):
            cols = pl.ds(h * XA_DH, XA_DH)
            vcols = pl.ds(D_MODEL + h * XA_DH, XA_DH)
            qh, kh, vh, doh = q_ref[:, cols], kv_ref[:, cols], kv_ref[:, vcols], do_ref[:, cols]
            p = _xattn_probs(qh, kh)
            dkv_ref[:, vcols] += lax.dot_general(p.astype(CD), doh, TN, preferred_element_type=F32)
            dp = lax.dot_general(doh, vh, NT, preferred_element_type=F32)
            ds = (p * (dp - jnp.sum(dp * p, axis=-1, keepdims=True)) * XA_SCALE).astype(CD)
            dq_ref[:, cols] = jnp.dot(ds, kh, preferred_element_type=F32).astype(dq_ref.dtype)
            dkv_ref[:, cols] += lax.dot_general(ds, qh, TN, preferred_element_type=F32)

    return pl.pallas_call(
        body, name=name, grid=(T // tq,),
        in_specs=[pl.BlockSpec((tq, D_MODEL), lambda i: (i, 0)),
                  pl.BlockSpec((MEM_LEN, 2 * D_MODEL), lambda i: (0, 0)),
                  pl.BlockSpec((tq, D_MODEL), lambda i: (i, 0))],
        out_specs=[pl.BlockSpec((tq, D_MODEL), lambda i: (i, 0)),
                   pl.BlockSpec((MEM_LEN, 2 * D_MODEL), lambda i: (0, 0))],
        out_shape=[jax.ShapeDtypeStruct((T, D_MODEL), CD),
                   jax.ShapeDtypeStruct((MEM_LEN, 2 * D_MODEL), F32)],
        compiler_params=_params("arbitrary"),
    )(q, kv, do)


C_W = Q_LORA + KV_LORA + LANES


def _rot(x):
    lane = lax.broadcasted_iota(jnp.int32, x.shape, x.ndim - 1)
    up = pltpu.roll(x, LANES - QK_ROPE // 2, x.ndim - 1)
    dn = pltpu.roll(x, QK_ROPE // 2, x.ndim - 1)
    lo, mid, hi = QK_NOPE, QK_NOPE + QK_ROPE // 2, QK_NOPE + QK_ROPE
    return jnp.where((lane >= lo) & (lane < mid), -up, jnp.where((lane >= mid) & (lane < hi), dn, 0.0))


def _mla_mid_fwd(c, qg, kvg, cs, sn, *, name, tt=512):
    T = c.shape[0]
    tt = min(tt, T)

    def body(c_ref, qg_ref, kg_ref, cs_ref, sn_ref, qn_ref, kn_ref, kpe_ref):
        cq = c_ref[:, pl.ds(0, Q_LORA)]
        qn_ref[...] = (cq * lax.rsqrt(jnp.mean(cq * cq, axis=-1, keepdims=True) + EPS)
                       * qg_ref[...]).astype(qn_ref.dtype)
        ck = c_ref[:, pl.ds(Q_LORA, KV_LORA)]
        kn_ref[...] = (ck * lax.rsqrt(jnp.mean(ck * ck, axis=-1, keepdims=True) + EPS)
                       * kg_ref[...]).astype(kn_ref.dtype)
        kp = c_ref[:, pl.ds(Q_LORA + KV_LORA, LANES)]
        kpe_ref[...] = kp * cs_ref[...] + _rot(kp) * sn_ref[...]

    row = lambda w: pl.BlockSpec((tt, w), lambda i: (i, 0))
    one = lambda w: pl.BlockSpec((1, w), lambda i: (0, 0))
    return pl.pallas_call(
        body, name=name, grid=(T // tt,),
        in_specs=[row(C_W), one(Q_LORA), one(KV_LORA), row(LANES), row(LANES)],
        out_specs=[row(Q_LORA), row(KV_LORA), row(LANES)],
        out_shape=[jax.ShapeDtypeStruct((T, Q_LORA), CD), jax.ShapeDtypeStruct((T, KV_LORA), CD),
                   jax.ShapeDtypeStruct((T, LANES), F32)],
        compiler_params=_params("parallel"),
    )(c, qg, kvg, cs, sn)


def _mla_mid_bwd(c, dqn, dkvn, dksum, qg, kvg, cs, sn, *, name, tt=512):
    T = c.shape[0]
    tt = min(tt, T)

    def body(c_ref, dq_ref, dk_ref, ds_ref, qg_ref, kg_ref, cs_ref, sn_ref, dc_ref, dqg_ref, dkg_ref):
        @pl.when(pl.program_id(0) == 0)
        def _():
            dqg_ref[...] = jnp.zeros_like(dqg_ref)
            dkg_ref[...] = jnp.zeros_like(dkg_ref)

        dx, dg = _rms_bwd(c_ref[:, pl.ds(0, Q_LORA)], qg_ref[...], dq_ref[...])
        dc_ref[:, pl.ds(0, Q_LORA)] = dx.astype(dc_ref.dtype)
        dqg_ref[...] += jnp.sum(dg, axis=0, keepdims=True)
        dx, dg = _rms_bwd(c_ref[:, pl.ds(Q_LORA, KV_LORA)], kg_ref[...], dk_ref[...])
        dc_ref[:, pl.ds(Q_LORA, KV_LORA)] = dx.astype(dc_ref.dtype)
        dkg_ref[...] += jnp.sum(dg, axis=0, keepdims=True)
        d = ds_ref[...]
        lane = lax.broadcasted_iota(jnp.int32, d.shape, 1)
        dkp = d * cs_ref[...] - _rot(d * sn_ref[...])
        dc_ref[:, pl.ds(Q_LORA + KV_LORA, LANES)] = jnp.where(
            (lane >= QK_NOPE) & (lane < QK_NOPE + QK_ROPE), dkp, 0.0).astype(dc_ref.dtype)

    row = lambda w: pl.BlockSpec((tt, w), lambda i: (i, 0))
    one = lambda w: pl.BlockSpec((1, w), lambda i: (0, 0))
    return pl.pallas_call(
        body, name=name, grid=(T // tt,),
        in_specs=[row(C_W), row(Q_LORA), row(KV_LORA), row(LANES), one(Q_LORA), one(KV_LORA),
                  row(LANES), row(LANES)],
        out_specs=[row(C_W), one(Q_LORA), one(KV_LORA)],
        out_shape=[jax.ShapeDtypeStruct((T, C_W), CD), jax.ShapeDtypeStruct((1, Q_LORA), F32),
                   jax.ShapeDtypeStruct((1, KV_LORA), F32)],
        compiler_params=_params("arbitrary"),
    )(c, dqn, dkvn, dksum, qg, kvg, cs, sn)


def _mla_qkv_fwd(qn, kvn, kpe, cs, sn, wq, wk, wv, *, name, tt=256):
    T = qn.shape[0]
    tt = min(tt, T)
    H = MLA_HEADS

    def body(qn_ref, kn_ref, kpe_ref, cs_ref, sn_ref, wq_ref, wk_ref, wv_ref, q_ref, k_ref, v_ref):
        qn_v, kn_v, kpe_v, cs_v, sn_v = qn_ref[...], kn_ref[...], kpe_ref[...], cs_ref[...], sn_ref[...]
        for h in range(H):
            q = jnp.dot(qn_v, wq_ref[h], preferred_element_type=F32)
            q_ref[h] = (q * cs_v + _rot(q) * sn_v).astype(q_ref.dtype)
            k_ref[h] = (jnp.dot(kn_v, wk_ref[h], preferred_element_type=F32) + kpe_v).astype(k_ref.dtype)
            v_ref[h] = jnp.dot(kn_v, wv_ref[h], preferred_element_type=F32).astype(v_ref.dtype)

    row = lambda w: pl.BlockSpec((tt, w), lambda i: (i, 0))
    wsp = lambda k: pl.BlockSpec((H, k, LANES), lambda i: (0, 0, 0))
    hsp = pl.BlockSpec((H, tt, LANES), lambda i: (0, i, 0))
    sh = jax.ShapeDtypeStruct((H, T, LANES), CD)
    return pl.pallas_call(
        body, name=name, grid=(T // tt,),
        in_specs=[row(Q_LORA), row(KV_LORA), row(LANES), row(LANES), row(LANES),
                  wsp(Q_LORA), wsp(KV_LORA), wsp(KV_LORA)],
        out_specs=[hsp, hsp, hsp], out_shape=[sh, sh, sh],
        compiler_params=_params("parallel"),
    )(qn, kvn, kpe, cs, sn, wq, wk, wv)


def _mla_qkv_bwd(dq, dk, dv, qn, kvn, cs, sn, wq, wk, wv, *, name, tt=256):
    T = qn.shape[0]
    tt = min(tt, T)
    H = MLA_HEADS

    def body(dq_ref, dk_ref, dv_ref, qn_ref, kn_ref, cs_ref, sn_ref, wq_ref, wk_ref, wv_ref,
             dqn_ref, dkn_ref, dks_ref, dwq_ref, dwk_ref, dwv_ref):
        @pl.when(pl.program_id(0) == 0)
        def _():
            for r in (dwq_ref, dwk_ref, dwv_ref):
                r[...] = jnp.zeros_like(r)

        qn_v, kn_v, cs_v, sn_v = qn_ref[...], kn_ref[...], cs_ref[...], sn_ref[...]
        dqn = jnp.zeros((tt, Q_LORA), F32)
        dkn = jnp.zeros((tt, KV_LORA), F32)
        dks = jnp.zeros((tt, LANES), F32)
        for h in range(H):
            d = dq_ref[h]
            dqh = (d * cs_v - _rot(d * sn_v)).astype(CD)
            dkh, dvh = dk_ref[h], dv_ref[h]
            dqn = dqn + lax.dot_general(dqh, wq_ref[h], NT, preferred_element_type=F32)
            dkn = dkn + lax.dot_general(dkh, wk_ref[h], NT, preferred_element_type=F32)
            dkn = dkn + lax.dot_general(dvh, wv_ref[h], NT, preferred_element_type=F32)
            dks = dks + dkh.astype(F32)
            dwq_ref[h] += lax.dot_general(qn_v, dqh, TN, preferred_element_type=F32)
            dwk_ref[h] += lax.dot_general(kn_v, dkh, TN, preferred_element_type=F32)
            dwv_ref[h] += lax.dot_general(kn_v, dvh, TN, preferred_element_type=F32)
        dqn_ref[...] = dqn
        dkn_ref[...] = dkn
        dks_ref[...] = dks

    row = lambda w: pl.BlockSpec((tt, w), lambda i: (i, 0))
    wsp = lambda k: pl.BlockSpec((H, k, LANES), lambda i: (0, 0, 0))
    hsp = pl.BlockSpec((H, tt, LANES), lambda i: (0, i, 0))
    return pl.pallas_call(
        body, name=name, grid=(T // tt,),
        in_specs=[hsp, hsp, hsp, row(Q_LORA), row(KV_LORA), row(LANES), row(LANES),
                  wsp(Q_LORA), wsp(KV_LORA), wsp(KV_LORA)],
        out_specs=[row(Q_LORA), row(KV_LORA), row(LANES), wsp(Q_LORA), wsp(KV_LORA), wsp(KV_LORA)],
        out_shape=[jax.ShapeDtypeStruct((T, Q_LORA), F32), jax.ShapeDtypeStruct((T, KV_LORA), F32),
                   jax.ShapeDtypeStruct((T, LANES), F32),
                   jax.ShapeDtypeStruct((H, Q_LORA, LANES), F32),
                   jax.ShapeDtypeStruct((H, KV_LORA, LANES), F32),
                   jax.ShapeDtypeStruct((H, KV_LORA, LANES), F32)],
        compiler_params=_params("arbitrary"),
    )(dq, dk, dv, qn, kvn, cs, sn, wq, wk, wv)


FLASH_BLOCK = 1024
EXP2_SCALE = MLA_SCALE * math.log2(math.e)


def _causal_steps(nq, by_key):
    pairs = [(i, j) for j in range(nq) for i in range(j, nq)] if by_key else \
            [(i, j) for i in range(nq) for j in range(i + 1)]
    return (jnp.asarray([p[0] for p in pairs], jnp.int32), jnp.asarray([p[1] for p in pairs], jnp.int32))


def _raw_scores(q, k, masked):
    s = lax.dot_general(q, k, NT, preferred_element_type=F32)
    if masked:
        row = lax.broadcasted_iota(jnp.int32, s.shape, 0)
        col = lax.broadcasted_iota(jnp.int32, s.shape, 1)
        s = jnp.where(col <= row, s, NEG)
    return s


def _flash_fwd(q, k, v, *, name):
    H, T, _ = q.shape
    tq = min(FLASH_BLOCK, T)
    nq = T // tq
    i_tab, j_tab = _causal_steps(nq, by_key=False)

    rb = min(128, tq)

    def body(i_tab, j_tab, q_ref, k_ref, v_ref, o_ref, lse_ref, m_sc, l_sc, acc, s_sc, p_sc):
        t = pl.program_id(1)
        i, j = i_tab[t], j_tab[t]

        @pl.when(j == 0)
        def _():
            m_sc[...] = jnp.full_like(m_sc, NEG)
            l_sc[...] = jnp.zeros_like(l_sc)
            acc[...] = jnp.zeros_like(acc)

        def step(masked):
            lane = lax.broadcasted_iota(jnp.int32, (tq, LANES), 1)
            alphas, pvs = [], []
            for h in range(2):
                s_sc[h] = _raw_scores(q_ref[h], k_ref[h], masked)
                m_prev = m_sc[h]
                m_new = jnp.maximum(m_prev, jnp.max(s_sc[h], axis=-1, keepdims=True))
                alpha = jnp.exp2((m_prev - m_new) * EXP2_SCALE)
                m_sc[h] = m_new
                for r in range(tq // rb):
                    rows = pl.ds(r * rb, rb)
                    m_r = m_sc[h, rows, :]
                    part = jnp.zeros((rb, LANES), F32)
                    for c in range(tq // LANES):
                        cols = pl.ds(c * LANES, LANES)
                        p = jnp.exp2((s_sc[h, rows, cols] - m_r) * EXP2_SCALE)
                        part = part + p
                        p_sc[h, rows, cols] = p.astype(CD)
                    l_sc[h, rows, :] = (alpha[r * rb:(r + 1) * rb] * l_sc[h, rows, :]
                                        + jnp.sum(part, axis=-1, keepdims=True))
                alphas.append(alpha)
                pvs.append(jnp.dot(p_sc[h], v_ref[h], preferred_element_type=F32))
            acc[...] = acc[...] * jnp.where(lane < V_HEAD, alphas[0], alphas[1]) + pvs[0] + pvs[1]

        @pl.when(j < i)
        def _():
            step(False)

        @pl.when(j == i)
        def _():
            step(True)
            lane = lax.broadcasted_iota(jnp.int32, (tq, LANES), 1)
            o_ref[...] = (acc[...] / jnp.where(lane < V_HEAD, l_sc[0], l_sc[1])).astype(o_ref.dtype)
            for h in range(2):
                lse_ref[h] = m_sc[h] * EXP2_SCALE + jnp.log2(l_sc[h])

    qsp = pl.BlockSpec((2, tq, LANES), lambda p, t, it, jt: (p, it[t], 0))
    ksp = pl.BlockSpec((2, tq, LANES), lambda p, t, it, jt: (p, jt[t], 0))
    return pl.pallas_call(
        body, name=name,
        grid_spec=pltpu.PrefetchScalarGridSpec(
            num_scalar_prefetch=2, grid=(H // 2, int(i_tab.shape[0])),
            in_specs=[qsp, ksp, ksp],
            out_specs=[pl.BlockSpec((tq, LANES), lambda p, t, it, jt: (it[t], p)), qsp],
            scratch_shapes=[pltpu.VMEM((2, tq, LANES), F32), pltpu.VMEM((2, tq, LANES), F32),
                            pltpu.VMEM((tq, LANES), F32),
                            pltpu.VMEM((2, tq, tq), F32), pltpu.VMEM((2, tq, tq), CD)]),
        out_shape=[jax.ShapeDtypeStruct((T, H * V_HEAD), CD), jax.ShapeDtypeStruct((H, T, LANES), F32)],
        compiler_params=_params("parallel", "arbitrary"),
    )(i_tab, j_tab, q, k, v)


def _flash_delta(o, do, *, name, tt=512):
    T = o.shape[0]
    tt = min(tt, T)
    H = MLA_HEADS

    def body(o_ref, do_ref, dl_ref):
        lane = lax.broadcasted_iota(jnp.int32, (tt, LANES), 1)
        for p in range(H // 2):
            cols = pl.ds(p * LANES, LANES)
            prod = do_ref[:, cols].astype(F32) * o_ref[:, cols].astype(F32)
            d0 = jnp.sum(jnp.where(lane < V_HEAD, prod, 0.0), axis=-1, keepdims=True)
            d1 = jnp.sum(jnp.where(lane < V_HEAD, 0.0, prod), axis=-1, keepdims=True)
            dl_ref[2 * p] = jnp.broadcast_to(d0, (tt, LANES))
            dl_ref[2 * p + 1] = jnp.broadcast_to(d1, (tt, LANES))

    row = pl.BlockSpec((tt, H * V_HEAD), lambda i: (i, 0))
    return pl.pallas_call(
        body, name=name, grid=(T // tt,), in_specs=[row, row],
        out_specs=pl.BlockSpec((H, tt, LANES), lambda i: (0, i, 0)),
        out_shape=jax.ShapeDtypeStruct((H, T, LANES), F32),
        compiler_params=_params("parallel"),
    )(o, do)


def _flash_bwd(q, k, v, do, lse, delta, *, name):
    H, T, _ = q.shape
    tq = min(FLASH_BLOCK, T)
    nq = T // tq
    i_tab, j_tab = _causal_steps(nq, by_key=True)

    def body(i_tab, j_tab, q_ref, k_ref, v_ref, do_ref, lse_ref, dl_ref, dq_ref, dk_ref, dv_ref, dk_acc, dv_acc):
        t = pl.program_id(1)
        i, j = i_tab[t], j_tab[t]
        rows = pl.ds(pl.multiple_of(i * tq, tq), tq)

        @pl.when(t == 0)
        def _():
            dq_ref[...] = jnp.zeros_like(dq_ref)

        def step(masked):
            do_v = do_ref[...]
            for h in range(2):
                s = _raw_scores(q_ref[h], k_ref[h], masked)
                p = jnp.exp2(s * EXP2_SCALE - lse_ref[h][:, :1])
                dv_acc[h] += lax.dot_general(p.astype(CD), do_v, TN, preferred_element_type=F32)
                dp = lax.dot_general(do_v, v_ref[h], NT, preferred_element_type=F32)
                ds = (p * (dp - dl_ref[h][:, :1]) * MLA_SCALE).astype(CD)
                dk_acc[h] += lax.dot_general(ds, q_ref[h], TN, preferred_element_type=F32)
                dq_ref[h, rows, :] += jnp.dot(ds, k_ref[h], preferred_element_type=F32)

        @pl.when(i == j)
        def _():
            dk_acc[...] = jnp.zeros_like(dk_acc)
            dv_acc[...] = jnp.zeros_like(dv_acc)
            step(True)

        @pl.when(i > j)
        def _():
            step(False)

        @pl.when(i == nq - 1)
        def _():
            lane = lax.broadcasted_iota(jnp.int32, (tq, LANES), 1)
            dk_ref[...] = dk_acc[...].astype(dk_ref.dtype)
            dv_ref[0] = jnp.where(lane < V_HEAD, dv_acc[0], 0.0).astype(dv_ref.dtype)
            dv_ref[1] = jnp.where(lane < V_HEAD, 0.0, dv_acc[1]).astype(dv_ref.dtype)

    qsp = pl.BlockSpec((2, tq, LANES), lambda p, t, it, jt: (p, it[t], 0))
    ksp = pl.BlockSpec((2, tq, LANES), lambda p, t, it, jt: (p, jt[t], 0))
    osp = pl.BlockSpec((tq, LANES), lambda p, t, it, jt: (it[t], p))
    sh = jax.ShapeDtypeStruct((H, T, LANES), CD)
    return pl.pallas_call(
        body, name=name,
        grid_spec=pltpu.PrefetchScalarGridSpec(
            num_scalar_prefetch=2, grid=(H // 2, int(i_tab.shape[0])),
            in_specs=[qsp, ksp, ksp, osp, qsp, qsp],
            out_specs=[pl.BlockSpec((2, T, LANES), lambda p, t, it, jt: (p, 0, 0)), ksp, ksp],
            scratch_shapes=[pltpu.VMEM((2, tq, LANES), F32), pltpu.VMEM((2, tq, LANES), F32)]),
        out_shape=[jax.ShapeDtypeStruct((H, T, LANES), F32), sh, sh],
        compiler_params=_params("parallel", "arbitrary"),
    )(i_tab, j_tab, q, k, v, do, lse, delta)


def _loss_head(x, g, target, *, name, tt=512):
    T, D = x.shape
    tt = min(tt, T)

    def body(x_ref, g_ref, t_ref, dx_ref, dg_ref, loss_ref):
        @pl.when(pl.program_id(0) == 0)
        def _():
            dg_ref[...] = jnp.zeros_like(dg_ref)
            loss_ref[...] = jnp.zeros_like(loss_ref)

        xv, gv = x_ref[...], g_ref[...]
        r = lax.rsqrt(jnp.mean(xv * xv, axis=-1, keepdims=True) + EPS)
        err = xv * r * gv - t_ref[...]
        tok = jnp.mean(err * err, axis=-1, keepdims=True)
        loss_ref[...] += 0.5 * jnp.sum(tok, axis=0, keepdims=True)
        dx, dg_rows = _rms_bwd(xv, gv, err * (1.0 / D))
        dx_ref[...] = dx
        dg_ref[...] += jnp.sum(dg_rows, axis=0, keepdims=True)

    return pl.pallas_call(
        body, name=name, grid=(T // tt,),
        in_specs=[pl.BlockSpec((tt, D), lambda i: (i, 0)), pl.BlockSpec((1, D), lambda i: (0, 0)),
                  pl.BlockSpec((tt, D), lambda i: (i, 0))],
        out_specs=[pl.BlockSpec((tt, D), lambda i: (i, 0)), pl.BlockSpec((1, D), lambda i: (0, 0)),
                   pl.BlockSpec((1, LANES), lambda i: (0, 0))],
        out_shape=[jax.ShapeDtypeStruct((T, D), F32), jax.ShapeDtypeStruct((1, D), F32),
                   jax.ShapeDtypeStruct((1, LANES), F32)],
        compiler_params=_params("arbitrary"),
    )(x, g, target)


def _rope_tables(positions):
    inv = 1.0 / (ROPE_THETA ** (jnp.arange(0, QK_ROPE, 2, dtype=F32) / QK_ROPE))
    ang = positions.astype(F32)[:, None] * inv
    c, s = jnp.cos(ang), jnp.sin(ang)
    T = positions.shape[0]
    cs = jnp.concatenate([jnp.ones((T, QK_NOPE), F32), c, c, jnp.zeros((T, LANES - QK_NOPE - QK_ROPE), F32)], 1)
    sn = jnp.concatenate([jnp.zeros((T, QK_NOPE), F32), s, s, jnp.zeros((T, LANES - QK_NOPE - QK_ROPE), F32)], 1)
    return cs, sn


def _pad_rows(w, rows):
    return jnp.concatenate([w, jnp.zeros((rows - w.shape[0],) + w.shape[1:], w.dtype)], 0)


def _mla_weights(w_dq_dkv, w_uq, w_ukv):
    K = w_dq_dkv.shape[0]
    z = lambda n: jnp.zeros((K, n), w_dq_dkv.dtype)
    wc = jnp.concatenate([w_dq_dkv[:, :Q_LORA + KV_LORA], z(QK_NOPE), w_dq_dkv[:, Q_LORA + KV_LORA:],
                          z(LANES - QK_NOPE - QK_ROPE)], 1)
    wq = w_uq.reshape(Q_LORA, MLA_HEADS, QK_NOPE + QK_ROPE).transpose(1, 0, 2)
    wq = jnp.concatenate([wq, jnp.zeros((MLA_HEADS, Q_LORA, LANES - QK_NOPE - QK_ROPE), wq.dtype)], 2)
    wkv = w_ukv.reshape(KV_LORA, MLA_HEADS, QK_NOPE + V_HEAD).transpose(1, 0, 2)
    zero = jnp.zeros_like(wkv[:, :, :QK_NOPE])
    wk = jnp.concatenate([wkv[:, :, :QK_NOPE], zero], 2)
    wv_lo = jnp.concatenate([wkv[:, :, QK_NOPE:], zero], 2)
    wv_hi = jnp.concatenate([zero, wkv[:, :, QK_NOPE:]], 2)
    odd = (jnp.arange(MLA_HEADS) % 2 == 1)[:, None, None]
    wv = jnp.where(odd, wv_hi, wv_lo)
    return wc, wq, wk, wv


def _mla_weight_grads(dwc, dwq, dwk, dwv):
    d_dq = jnp.concatenate([dwc[:, :Q_LORA + KV_LORA],
                            dwc[:, Q_LORA + KV_LORA + QK_NOPE:Q_LORA + KV_LORA + QK_NOPE + QK_ROPE]], 1)
    d_uq = dwq[:, :, :QK_NOPE + QK_ROPE].transpose(1, 0, 2).reshape(Q_LORA, MLA_HEADS * (QK_NOPE + QK_ROPE))
    odd = (jnp.arange(MLA_HEADS) % 2 == 1)[:, None, None]
    dv = jnp.where(odd, dwv[:, :, V_HEAD:], dwv[:, :, :V_HEAD])
    d_ukv = jnp.concatenate([dwk[:, :, :QK_NOPE], dv], 2).transpose(1, 0, 2).reshape(
        KV_LORA, MLA_HEADS * (QK_NOPE + V_HEAD))
    return d_dq, d_uq, d_ukv


def _local_step(x, mem, positions, target, W):
    G = {}
    row = lambda v: v.reshape(1, -1)
    cs, sn = _rope_tables(positions)
    saved = []
    for l in range(DEPTH):
        L = f"l{l}"
        s = {"x0": x}
        if l % 2 == 0:
            e = l // 2
            s["z"], s["h"] = _nmm(x, row(W["norm_mix_g"][l]), (W["pc_w_in"], e), name=f"{L}_mix_in", out_dtype=F32)
            s["dw_w"] = _pad_rows(W["conv_dw_w"][e], CONV_K + 1)
            s["mix_p"] = (W["pool_w"][e], row(W["pool_scale"][e]), s["dw_w"], row(W["conv_dw_b"][e]),
                          row(W["conv_ln_g"][e]), row(W["conv_ln_b"][e]))
            s["ycat"] = _mixer_fwd(s["z"], *s["mix_p"], name=f"{L}_mix_mid")
            x = _mm_res(s["ycat"], (W["pc_w_out"], e), x, name=f"{L}_mix_out")
        else:
            o = l // 2
            wc, wq, wk, wv = _mla_weights(W["mla_w_dq_dkv"][o], W["mla_w_uq"][o], W["mla_w_ukv"][o])
            s["mla_w"] = (wc, wq, wk, wv)
            s["c"], s["h"] = _nmm(x, row(W["norm_mix_g"][l]), wc, name=f"{L}_mla_down", out_dtype=F32)
            s["qg"], s["kvg"] = row(W["mla_q_norm_g"][o]), row(W["mla_kv_norm_g"][o])
            s["qn"], s["kvn"], kpe = _mla_mid_fwd(s["c"], s["qg"], s["kvg"], cs, sn, name=f"{L}_mla_mid")
            s["q"], s["k"], s["v"] = _mla_qkv_fwd(s["qn"], s["kvn"], kpe, cs, sn, wq, wk, wv, name=f"{L}_mla_qkv")
            s["o"], s["lse"] = _flash_fwd(s["q"], s["k"], s["v"], name=f"{L}_mla_attn")
            x = _mm_res(s["o"], (W["mla_w_o"], o), x, name=f"{L}_mla_out")
        s["x1"] = x
        s["xq"], s["hx"] = _nmm(x, row(W["norm_xa_g"][l]), (W["xa_wq"], l), name=f"{L}_xa_q", out_dtype=CD)
        s["xkv"], s["hm"] = _nmm(mem, row(W["norm_mem_g"][l]), (W["xa_wkv"], l), name=f"{L}_xa_kv", out_dtype=CD)
        s["xo"] = _xattn_fwd(s["xq"], s["xkv"], name=f"{L}_xa_attn")
        x = _mm_res(s["xo"], (W["xa_wo"], l), x, name=f"{L}_xa_out")
        s["x2"] = x
        s["up"], s["hf"] = _nmm(x, row(W["norm_ffn_g"][l]), (W["ffn_w_up"], l), name=f"{L}_ffn_up", out_dtype=CD,
                                tn_target=1408)
        s["cw"], s["cb"] = _pad_rows(W["ffn_conv_w"][l], 8), row(W["ffn_conv_b"][l])
        s["act"] = _ffn_mid_fwd(s["up"], s["cw"], s["cb"], name=f"{L}_ffn_mid")
        x = _mm_res(s["act"], (W["ffn_w_down"], l), x, name=f"{L}_ffn_down")
        saved.append(s)
    dx, G["final_norm_g"], loss = _loss_head(x, row(W["final_norm_g"]), target, name="loss_head")
    G["final_norm_g"] = G["final_norm_g"].reshape(-1)

    per_layer = {}

    def put(name, l, val):
        per_layer.setdefault(name, {})[l] = val

    for l in reversed(range(DEPTH)):
        L = f"l{l}"
        s = saved[l]
        put("ffn_w_down", l, _mm_tn(s["act"], dx, name=f"{L}_ffn_down_dw", tk_target=1408))
        dact = _mm_nt(dx, (W["ffn_w_down"], l), name=f"{L}_ffn_down_dx", out_dtype=CD, tn_target=1408)
        dup, dcw, dcb = _ffn_mid_bwd(s["up"], dact, s["cw"], s["cb"], name=f"{L}_ffn_mid_bwd")
        put("ffn_conv_w", l, dcw[:FFN_K])
        put("ffn_conv_b", l, dcb[0])
        put("ffn_w_up", l, _mm_tn(s["hf"], dup, name=f"{L}_ffn_up_dw", tn_target=1408))
        dx, dg = _mm_nt_normbwd(dup, (W["ffn_w_up"], l), s["x2"], row(W["norm_ffn_g"][l]), dx, name=f"{L}_ffn_up_dx")
        put("norm_ffn_g", l, dg[0])
        put("xa_wo", l, _mm_tn(s["xo"], dx, name=f"{L}_xa_out_dw"))
        do = _mm_nt(dx, (W["xa_wo"], l), name=f"{L}_xa_out_dx", out_dtype=CD)
        dq, dkv = _xattn_bwd(s["xq"], s["xkv"], do, name=f"{L}_xa_attn_bwd")
        put("xa_wq", l, _mm_tn(s["hx"], dq, name=f"{L}_xa_q_dw"))
        dx, dg = _mm_nt_normbwd(dq, (W["xa_wq"], l), s["x1"], row(W["norm_xa_g"][l]), dx, name=f"{L}_xa_q_dx")
        put("norm_xa_g", l, dg[0])
        put("xa_wkv", l, _mm_tn(s["hm"], dkv, name=f"{L}_xa_kv_dw", tt=MEM_LEN))
        _, dg = _mm_nt_normbwd(dkv, (W["xa_wkv"], l), mem, row(W["norm_mem_g"][l]), jnp.zeros_like(mem),
                               name=f"{L}_xa_kv_dx", tm=MEM_LEN)
        put("norm_mem_g", l, dg[0])
        if l % 2 == 0:
            e = l // 2
            put("pc_w_out", e, _mm_tn(s["ycat"], dx, name=f"{L}_mix_out_dw"))
            dy = _mm_nt(dx, (W["pc_w_out"], e), name=f"{L}_mix_out_dx", out_dtype=F32)
            dz, dpw, dps, ddw, ddb, dlg, dlb = _mixer_bwd(s["z"], dy, *s["mix_p"], name=f"{L}_mix_mid_bwd")
            put("pool_w", e, dpw)
            put("pool_scale", e, dps[0])
            put("conv_dw_w", e, ddw[:CONV_K])
            put("conv_dw_b", e, ddb[0])
            put("conv_ln_g", e, dlg[0])
            put("conv_ln_b", e, dlb[0])
            put("pc_w_in", e, _mm_tn(s["h"], dz, name=f"{L}_mix_in_dw"))
            dx, dg = _mm_nt_normbwd(dz, (W["pc_w_in"], e), s["x0"], row(W["norm_mix_g"][l]), dx, name=f"{L}_mix_in_dx")
        else:
            o = l // 2
            wc, wq, wk, wv = s["mla_w"]
            put("mla_w_o", o, _mm_tn(s["o"], dx, name=f"{L}_mla_out_dw"))
            do = _mm_nt(dx, (W["mla_w_o"], o), name=f"{L}_mla_out_dx", out_dtype=CD)
            delta = _flash_delta(s["o"], do, name=f"{L}_mla_attn_delta")
            dq, dk, dv = _flash_bwd(s["q"], s["k"], s["v"], do, s["lse"], delta, name=f"{L}_mla_attn_bwd")
            dqn, dkvn, dks, dwq, dwk, dwv = _mla_qkv_bwd(dq, dk, dv, s["qn"], s["kvn"], cs, sn, wq, wk, wv,
                                                         name=f"{L}_mla_qkv_bwd")
            dc, dqg, dkg = _mla_mid_bwd(s["c"], dqn, dkvn, dks, s["qg"], s["kvg"], cs, sn, name=f"{L}_mla_mid_bwd")
            put("mla_q_norm_g", o, dqg[0])
            put("mla_kv_norm_g", o, dkg[0])
            dwc = _mm_tn(s["h"], dc, name=f"{L}_mla_down_dw")
            d_dq, d_uq, d_ukv = _mla_weight_grads(dwc, dwq, dwk, dwv)
            put("mla_w_dq_dkv", o, d_dq)
            put("mla_w_uq", o, d_uq)
            put("mla_w_ukv", o, d_ukv)
            dx, dg = _mm_nt_normbwd(dc, wc, s["x0"], row(W["norm_mix_g"][l]), dx, name=f"{L}_mla_down_dx",
                                    tk_target=768)
        put("norm_mix_g", l, dg[0])
    for name, d in per_layer.items():
        G[name] = jnp.stack([d[i] for i in sorted(d)], 0)
    return loss, dx, G


_ANY = pl.BlockSpec(memory_space=pl.ANY)


def _all_gather(xs, *, name):
    n = len(xs)

    def body(*refs):
        x_refs, out_refs = refs[:n], refs[n:2 * n]
        send_sems, recv_sems, local_sems = refs[2 * n:]
        mx, my, mc = lax.axis_index("x"), lax.axis_index("y"), lax.axis_index("c")
        me, sibling = (mx, my, mc), (mx, my, 1 - mc)
        chips = [(1 - mx, my), (mx, 1 - my), (1 - mx, 1 - my)]

        def copy(a, k, block, to, own=False):
            px, py, pc = block
            dst = out_refs[a].at[4 * px + 2 * py + pc]
            return pltpu.make_async_remote_copy(
                src_ref=x_refs[a] if own else dst, dst_ref=dst,
                send_sem=send_sems.at[7 * a + k], recv_sem=recv_sems.at[7 * a + k],
                device_id=to, device_id_type=MESH)

        mine = [pltpu.make_async_copy(x_refs[a], out_refs[a].at[4 * mx + 2 * my + mc], local_sems.at[a])
                for a in range(n)]
        for cp in mine:
            cp.start()
        first = []
        for j, chip in enumerate(chips):
            first += [copy(a, 1 + j, me, (*chip, mc), own=True) for a in range(n)]
        first += [copy(a, 0, me, sibling, own=True) for a in range(n)]
        for cp in first:
            cp.start()
        passed = []
        for j, chip in enumerate(chips):
            for a in range(n):
                copy(a, 1 + j, (*chip, mc), me).wait_recv()
                passed.append(copy(a, 4 + j, (*chip, mc), sibling))
                passed[-1].start()
        for a in range(n):
            copy(a, 0, sibling, me).wait_recv()
        for j, chip in enumerate(chips):
            for a in range(n):
                copy(a, 4 + j, (*chip, 1 - mc), me).wait_recv()
        for cp in first + passed:
            cp.wait_send()
        for cp in mine:
            cp.wait()

    return pl.pallas_call(
        body, name=name, in_specs=[_ANY] * n, out_specs=[_ANY] * n,
        out_shape=[jax.ShapeDtypeStruct((N_DEV,) + x.shape, x.dtype) for x in xs],
        scratch_shapes=[pltpu.SemaphoreType.DMA((7 * n,)), pltpu.SemaphoreType.DMA((7 * n,)),
                        pltpu.SemaphoreType.DMA((n,))],
    )(*xs)


def _all_to_all(ps, *, name):
    n = len(ps)

    def body(*refs):
        p_refs, out_refs = refs[:n], refs[n:2 * n]
        send_sems, recv_sems, local_sems = refs[2 * n:]
        mx, my, mc = lax.axis_index("x"), lax.axis_index("y"), lax.axis_index("c")
        me = 4 * mx + 2 * my + mc
        mine = [pltpu.make_async_copy(p_refs[a].at[me], out_refs[a].at[me], local_sems.at[a]) for a in range(n)]
        for cp in mine:
            cp.start()
        copies = []
        for k in range(1, N_DEV):
            px, py, pc = mx ^ ((k >> 2) & 1), my ^ ((k >> 1) & 1), mc ^ (k & 1)
            for a in range(n):
                copies.append(pltpu.make_async_remote_copy(
                    src_ref=p_refs[a].at[4 * px + 2 * py + pc], dst_ref=out_refs[a].at[me],
                    send_sem=send_sems.at[7 * a + k - 1], recv_sem=recv_sems.at[7 * a + k - 1],
                    device_id=(px, py, pc), device_id_type=MESH))
        for cp in copies:
            cp.start()
        for cp in copies:
            cp.wait()
        for cp in mine:
            cp.wait()

    return pl.pallas_call(
        body, name=name, in_specs=[_ANY] * n, out_specs=[_ANY] * n,
        out_shape=[jax.ShapeDtypeStruct(p.shape, p.dtype) for p in ps],
        scratch_shapes=[pltpu.SemaphoreType.DMA((7 * n,)), pltpu.SemaphoreType.DMA((7 * n,)),
                        pltpu.SemaphoreType.DMA((n,))],
    )(*ps)


ROW_TILE_ELEMS = 256 * 1024


def _row_tile(R, C):
    for t in (2048, 1024, 512, 256, 128, 64, 32, 16):
        if R % t == 0 and t * C <= ROW_TILE_ELEMS:
            return t
    raise ValueError((R, C))


def _sum_slots(gs, *, name):
    S, R, C = gs.shape
    tr = _row_tile(R, C)

    def body(g_ref, o_ref):
        g = g_ref[0].astype(F32)
        for s in range(1, S):
            g = g + g_ref[s].astype(F32)
        o_ref[...] = g

    return pl.pallas_call(
        body, name=name, grid=(R // tr,),
        in_specs=[pl.BlockSpec((S, tr, C), lambda i: (0, i, 0))],
        out_specs=pl.BlockSpec((tr, C), lambda i: (i, 0)),
        out_shape=jax.ShapeDtypeStruct((R, C), F32),
        compiler_params=_params("parallel"),
    )(gs)


def _adamw(gs, w, m, v, *, name):
    S, R, C = gs.shape
    tr = _row_tile(R, C)

    def body(g_ref, w_ref, m_ref, v_ref, g_out, d_out, m_out, v_out):
        g = g_ref[0].astype(F32)
        for s in range(1, S):
            g = g + g_ref[s].astype(F32)
        m_new = ADAM_B1 * m_ref[...] + (1.0 - ADAM_B1) * g
        v_new = ADAM_B2 * v_ref[...] + (1.0 - ADAM_B2) * (g * g)
        m_hat = m_new / (1.0 - ADAM_B1 ** ADAM_STEP)
        v_hat = v_new / (1.0 - ADAM_B2 ** ADAM_STEP)
        g_out[...] = g
        d_out[...] = -ADAM_LR * (m_hat / (jnp.sqrt(v_hat) + ADAM_EPS) + ADAM_WD * w_ref[...])
        m_out[...] = m_new
        v_out[...] = v_new

    blk = pl.BlockSpec((tr, C), lambda i: (i, 0))
    sh = jax.ShapeDtypeStruct((R, C), F32)
    return pl.pallas_call(
        body, name=name, grid=(R // tr,),
        in_specs=[pl.BlockSpec((S, tr, C), lambda i: (0, i, 0)), blk, blk, blk],
        out_specs=[blk, blk, blk, blk], out_shape=[sh, sh, sh, sh],
        compiler_params=_params("parallel"),
    )(gs, w, m, v)


PIECE = 16 * LANES


def _pack(arrs, dtype, lead, row_mult):
    lead_shape = arrs[0].shape[:lead]
    parts, meta, off = [], [], 0
    for a in arrs:
        size = math.prod(a.shape[lead:])
        padded = -(-size // PIECE) * PIECE
        flat = a.astype(dtype).reshape(lead_shape + (size,))
        if padded != size:
            flat = jnp.concatenate([flat, jnp.zeros(lead_shape + (padded - size,), dtype)], -1)
        parts.append(flat)
        meta.append((off, size, a.shape[lead:]))
        off += padded
    total = -(-off // (row_mult * LANES)) * (row_mult * LANES)
    if total != off:
        parts.append(jnp.zeros(lead_shape + (total - off,), dtype))
    return jnp.concatenate(parts, -1).reshape(lead_shape + (total // LANES, LANES)), meta


def _unpack(packed, meta, lead):
    lead_shape = packed.shape[:lead]
    flat = packed.reshape(lead_shape + (-1,))
    return [flat[..., off:off + size].reshape(lead_shape + shape) for off, size, shape in meta]


ARG_NAMES = ['x', 'mem', 'positions', 'norm_mix_g', 'norm_xa_g', 'norm_mem_g', 'xa_wq', 'xa_wkv', 'xa_wo', 'norm_ffn_g', 'ffn_w_up', 'ffn_conv_w', 'ffn_conv_b', 'ffn_w_down', 'pc_w_in', 'pool_w', 'pool_scale', 'conv_dw_w', 'conv_dw_b', 'conv_ln_g', 'conv_ln_b', 'pc_w_out', 'mla_w_dq_dkv', 'mla_q_norm_g', 'mla_w_uq', 'mla_kv_norm_g', 'mla_w_ukv', 'mla_w_o', 'final_norm_g', 'loss_target']
WEIGHTS = ARG_NAMES[3:29]
BIG = {'xa_wq': 1, 'xa_wkv': 2, 'xa_wo': 1, 'ffn_w_up': 2, 'ffn_w_down': 1, 'pc_w_in': 2, 'pc_w_out': 1,
       'mla_w_dq_dkv': 1, 'mla_w_uq': 2, 'mla_w_ukv': 2, 'mla_w_o': 1}
SMALL_SHARDED = {'ffn_conv_w': 2, 'conv_dw_w': 2, 'mla_q_norm_g': 1, 'mla_kv_norm_g': 1}
REPLICATED = [n for n in WEIGHTS if n not in BIG and n not in SMALL_SHARDED]


def _from_slots(g, axis):
    t = jnp.moveaxis(g, 0, axis)
    return t.reshape(t.shape[:axis] + (t.shape[axis] * t.shape[axis + 1],) + t.shape[axis + 2:])


def _to_slots(full, axis):
    n = full.shape[axis] // N_DEV
    t = full.reshape(full.shape[:axis] + (N_DEV, n) + full.shape[axis + 1:])
    return jnp.moveaxis(t, axis, 0)


def kernel(x, mem, positions, norm_mix_g, norm_xa_g, norm_mem_g, xa_wq, xa_wkv, xa_wo, norm_ffn_g, ffn_w_up, ffn_conv_w, ffn_conv_b, ffn_w_down, pc_w_in, pool_w, pool_scale, conv_dw_w, conv_dw_b, conv_ln_g, conv_ln_b, pc_w_out, mla_w_dq_dkv, mla_q_norm_g, mla_w_uq, mla_kv_norm_g, mla_w_ukv, mla_w_o, final_norm_g, loss_target, m_norm_mix_g, m_norm_xa_g, m_norm_mem_g, m_xa_wq, m_xa_wkv, m_xa_wo, m_norm_ffn_g, m_ffn_w_up, m_ffn_conv_w, m_ffn_conv_b, m_ffn_w_down, m_pc_w_in, m_pool_w, m_pool_scale, m_conv_dw_w, m_conv_dw_b, m_conv_ln_g, m_conv_ln_b, m_pc_w_out, m_mla_w_dq_dkv, m_mla_q_norm_g, m_mla_w_uq, m_mla_kv_norm_g, m_mla_w_ukv, m_mla_w_o, m_final_norm_g, v_norm_mix_g, v_norm_xa_g, v_norm_mem_g, v_xa_wq, v_xa_wkv, v_xa_wo, v_norm_ffn_g, v_ffn_w_up, v_ffn_conv_w, v_ffn_conv_b, v_ffn_w_down, v_pc_w_in, v_pool_w, v_pool_scale, v_conv_dw_w, v_conv_dw_b, v_conv_ln_g, v_conv_ln_b, v_pc_w_out, v_mla_w_dq_dkv, v_mla_q_norm_g, v_mla_w_uq, v_mla_kv_norm_g, v_mla_w_ukv, v_mla_w_o, v_final_norm_g):
    args = (x, mem, positions, norm_mix_g, norm_xa_g, norm_mem_g, xa_wq, xa_wkv, xa_wo, norm_ffn_g, ffn_w_up, ffn_conv_w, ffn_conv_b, ffn_w_down, pc_w_in, pool_w, pool_scale, conv_dw_w, conv_dw_b, conv_ln_g, conv_ln_b, pc_w_out, mla_w_dq_dkv, mla_q_norm_g, mla_w_uq, mla_kv_norm_g, mla_w_ukv, mla_w_o, final_norm_g, loss_target)
    a = dict(zip(ARG_NAMES, args))
    mom = dict(zip(WEIGHTS, (m_norm_mix_g, m_norm_xa_g, m_norm_mem_g, m_xa_wq, m_xa_wkv, m_xa_wo, m_norm_ffn_g, m_ffn_w_up, m_ffn_conv_w, m_ffn_conv_b, m_ffn_w_down, m_pc_w_in, m_pool_w, m_pool_scale, m_conv_dw_w, m_conv_dw_b, m_conv_ln_g, m_conv_ln_b, m_pc_w_out, m_mla_w_dq_dkv, m_mla_q_norm_g, m_mla_w_uq, m_mla_kv_norm_g, m_mla_w_ukv, m_mla_w_o, m_final_norm_g)))
    var = dict(zip(WEIGHTS, (v_norm_mix_g, v_norm_xa_g, v_norm_mem_g, v_xa_wq, v_xa_wkv, v_xa_wo, v_norm_ffn_g, v_ffn_w_up, v_ffn_conv_w, v_ffn_conv_b, v_ffn_w_down, v_pc_w_in, v_pool_w, v_pool_scale, v_conv_dw_w, v_conv_dw_b, v_conv_ln_g, v_conv_ln_b, v_pc_w_out, v_mla_w_dq_dkv, v_mla_q_norm_g, v_mla_w_uq, v_mla_kv_norm_g, v_mla_w_ukv, v_mla_w_o, v_final_norm_g)))
    me = 4 * lax.axis_index("x") + 2 * lax.axis_index("y") + lax.axis_index("c")

    big_all = _all_gather([a[n].astype(CD) for n in BIG], name="gather_weights")
    sm_pack, sm_meta = _pack([a[n] for n in SMALL_SHARDED], F32, 0, 8)
    sm_all = _unpack(_all_gather([sm_pack], name="gather_small")[0], sm_meta, 1)
    W = {n: a[n] for n in REPLICATED}
    for (n, ax), g in zip(BIG.items(), big_all):
        W[n] = _from_slots(g, ax)
    for (n, ax), g in zip(SMALL_SHARDED.items(), sm_all):
        W[n] = _from_slots(g, ax)

    loss, dx, G = _local_step(x[0], mem[0], positions[0], loss_target[0], W)

    recv = _all_to_all([_to_slots(G[n], ax).astype(CD) for n, ax in BIG.items()], name="scatter_grads")
    out = {}
    for n, r in zip(BIG, recv):
        shape = a[n].shape
        rows = lambda t: t.reshape(-1, shape[-1])
        res = _adamw(r.reshape(N_DEV, -1, shape[-1]), rows(a[n]), rows(mom[n]), rows(var[n]), name=f"adamw_{n}")
        out[n] = tuple(t.reshape(shape) for t in res)

    small_names = REPLICATED + list(SMALL_SHARDED)
    spack, smeta = _pack([G[n] for n in small_names] + [loss], F32, 0, 256)
    stot = _unpack(_sum_slots(_all_gather([spack], name="gather_small_grads")[0], name="sum_small_grads"), smeta, 0)
    loss_total = stot[-1][0, 0]
    gsm = dict(zip(small_names, stot[:-1]))
    for n, ax in SMALL_SHARDED.items():
        width = a[n].shape[ax]
        gsm[n] = lax.dynamic_slice_in_dim(gsm[n], me * width, width, ax)
    g1, meta1 = _pack([gsm[n] for n in small_names], F32, 0, 256)
    w1, _ = _pack([a[n] for n in small_names], F32, 0, 256)
    m1, _ = _pack([mom[n] for n in small_names], F32, 0, 256)
    v1, _ = _pack([var[n] for n in small_names], F32, 0, 256)
    res = [_unpack(r, meta1, 0) for r in _adamw(g1[None], w1, m1, v1, name="adamw_small")]
    for i, n in enumerate(small_names):
        out[n] = tuple(r[i] for r in res)

    return (loss_total, dx[None],
            *[out[n][0] for n in WEIGHTS], *[out[n][1] for n in WEIGHTS],
            *[out[n][2] for n in WEIGHTS], *[out[n][3] for n in WEIGHTS])
```

```python
import functools
import math

import jax
import jax.numpy as jnp
from jax import lax
from jax.experimental import pallas as pl
from jax.experimental.pallas import tpu as pltpu

F32 = jnp.float32
CD = jnp.bfloat16
EPS = 1e-6
NEG = -1e30
N_DEV = 8
LANES = 128
HALO = 32

D_MODEL = 1024
DEPTH = 4
XA_HEADS = 4
XA_DH = 256
MEM_LEN = 256
POOL_WINDOWS = (2, 4, 8, 16)
CONV_K = 31
FFN_K = 3
D_FF = 2816
MLA_HEADS = 16
QK_NOPE = 64
QK_ROPE = 32
V_HEAD = 64
Q_LORA = 384
KV_LORA = 256
ROPE_THETA = 10000.0
MLA_SCALE = 1.0 / math.sqrt(QK_NOPE + QK_ROPE)
XA_SCALE = XA_DH ** -0.5

ADAM_LR = 0.001
ADAM_B1 = 0.9
ADAM_B2 = 0.999
ADAM_EPS = 1e-08
ADAM_WD = 0.01
ADAM_STEP = 10

NT = (((1,), (1,)), ((), ()))
TN = (((0,), (0,)), ((), ()))
MESH = pl.DeviceIdType.MESH


def _tile(n, target):
    if n <= target:
        return n
    best = None
    for t in range(LANES, target + 1, LANES):
        if n % t == 0:
            best = t
    assert best is not None, (n, target)
    return best


def _params(*sem):
    return pltpu.CompilerParams(dimension_semantics=sem)


def _sigmoid(v):
    return 0.5 * jnp.tanh(0.5 * v) + 0.5


def _rms_bwd(x, gain, dh):
    r = lax.rsqrt(jnp.mean(x * x, axis=-1, keepdims=True) + EPS)
    xhat = x * r
    dxhat = dh * gain
    dx = r * (dxhat - xhat * jnp.mean(dxhat * xhat, axis=-1, keepdims=True))
    return dx, dh * xhat


def _weight(w):
    if not isinstance(w, tuple):
        return w, w.shape, pl.BlockSpec
    arr, layer = w

    def spec(block, imap):
        return pl.BlockSpec((None,) + tuple(block), lambda *a: (layer,) + tuple(imap(*a)))

    return arr, arr.shape[1:], spec


def _nmm(x, g, w, *, name, out_dtype, tm=512, tn_target=1024):
    M, K = x.shape
    w, (_, N), wspec = _weight(w)
    tm = min(tm, M)
    tn = _tile(N, tn_target)

    def body(x_ref, g_ref, w_ref, z_ref, h_ref):
        @pl.when(pl.program_id(1) == 0)
        def _():
            xf = x_ref[...]
            r = lax.rsqrt(jnp.mean(xf * xf, axis=-1, keepdims=True) + EPS)
            h_ref[...] = (xf * r * g_ref[...]).astype(h_ref.dtype)

        z_ref[...] = jnp.dot(h_ref[...], w_ref[...], preferred_element_type=F32).astype(z_ref.dtype)

    return pl.pallas_call(
        body, name=name, grid=(M // tm, N // tn),
        in_specs=[pl.BlockSpec((tm, K), lambda i, j: (i, 0)),
                  pl.BlockSpec((1, K), lambda i, j: (0, 0)),
                  wspec((K, tn), lambda i, j: (0, j))],
        out_specs=[pl.BlockSpec((tm, tn), lambda i, j: (i, j)),
                   pl.BlockSpec((tm, K), lambda i, j: (i, 0))],
        out_shape=[jax.ShapeDtypeStruct((M, N), out_dtype), jax.ShapeDtypeStruct((M, K), CD)],
        compiler_params=_params("parallel", "arbitrary"),
    )(x, g, w)


def _mm_res(a, w, res, *, name, tm=512, tn_target=1024):
    M, K = a.shape
    w, (_, N), wspec = _weight(w)
    tm = min(tm, M)
    tn = _tile(N, tn_target)

    def body(a_ref, w_ref, r_ref, o_ref):
        o_ref[...] = r_ref[...] + jnp.dot(a_ref[...].astype(CD), w_ref[...], preferred_element_type=F32)

    return pl.pallas_call(
        body, name=name, grid=(M // tm, N // tn),
        in_specs=[pl.BlockSpec((tm, K), lambda i, j: (i, 0)),
                  wspec((K, tn), lambda i, j: (0, j)),
                  pl.BlockSpec((tm, tn), lambda i, j: (i, j))],
        out_specs=pl.BlockSpec((tm, tn), lambda i, j: (i, j)),
        out_shape=jax.ShapeDtypeStruct((M, N), F32),
        compiler_params=_params("parallel", "arbitrary"),
    )(a, w, res)


def _mm_nt(a, w, *, name, out_dtype, tm=512, tn_target=1024):
    M, K = a.shape
    w, (N, _), wspec = _weight(w)
    tm = min(tm, M)
    tn = _tile(N, tn_target)

    def body(a_ref, w_ref, o_ref):
        o_ref[...] = lax.dot_general(a_ref[...].astype(CD), w_ref[...], NT,
                                     preferred_element_type=F32).astype(o_ref.dtype)

    return pl.pallas_call(
        body, name=name, grid=(M // tm, N // tn),
        in_specs=[pl.BlockSpec((tm, K), lambda i, j: (i, 0)),
                  wspec((tn, K), lambda i, j: (j, 0))],
        out_specs=pl.BlockSpec((tm, tn), lambda i, j: (i, j)),
        out_shape=jax.ShapeDtypeStruct((M, N), out_dtype),
        compiler_params=_params("parallel", "arbitrary"),
    )(a, w)


def _mm_nt_normbwd(gy, w, x, gain, dres, *, name, tm=512, tk_target=1408):
    M, K = gy.shape
    w, (D, _), wspec = _weight(w)
    tm = min(tm, M)
    tk = _tile(K, tk_target)
    nk = K // tk

    def body(g_ref, w_ref, x_ref, gain_ref, dres_ref, dx_ref, dg_ref, acc):
        i, k = pl.program_id(0), pl.program_id(1)

        @pl.when(k == 0)
        def _():
            acc[...] = jnp.zeros_like(acc)

        acc[...] += lax.dot_general(g_ref[...].astype(CD), w_ref[...], NT, preferred_element_type=F32)

        @pl.when(k == nk - 1)
        def _():
            dx, dg_rows = _rms_bwd(x_ref[...], gain_ref[...], acc[...])
            dx_ref[...] = dres_ref[...] + dx

            @pl.when(i == 0)
            def _():
                dg_ref[...] = jnp.zeros_like(dg_ref)

            dg_ref[...] += jnp.sum(dg_rows, axis=0, keepdims=True)

    return pl.pallas_call(
        body, name=name, grid=(M // tm, nk),
        in_specs=[pl.BlockSpec((tm, tk), lambda i, k: (i, k)),
                  wspec((D, tk), lambda i, k: (0, k)),
                  pl.BlockSpec((tm, D), lambda i, k: (i, 0)),
                  pl.BlockSpec((1, D), lambda i, k: (0, 0)),
                  pl.BlockSpec((tm, D), lambda i, k: (i, 0))],
        out_specs=[pl.BlockSpec((tm, D), lambda i, k: (i, 0)),
                   pl.BlockSpec((1, D), lambda i, k: (0, 0))],
        out_shape=[jax.ShapeDtypeStruct((M, D), F32), jax.ShapeDtypeStruct((1, D), F32)],
        scratch_shapes=[pltpu.VMEM((tm, D), F32)],
        compiler_params=_params("arbitrary", "arbitrary"),
    )(gy, w, x, gain, dres)


def _mm_tn(a, g, *, name, tt=512, tk_target=1024, tn_target=1024):
    T, K = a.shape
    N = g.shape[1]
    tt = min(tt, T)
    tk = _tile(K, tk_target)
    tn = _tile(N, tn_target)

    def body(a_ref, g_ref, o_ref):
        @pl.when(pl.program_id(2) == 0)
        def _():
            o_ref[...] = jnp.zeros_like(o_ref)

        o_ref[...] += lax.dot_general(a_ref[...].astype(CD), g_ref[...].astype(CD), TN,
                                      preferred_element_type=F32)

    return pl.pallas_call(
        body, name=name, grid=(K // tk, N // tn, T // tt),
        in_specs=[pl.BlockSpec((tt, tk), lambda i, j, t: (t, i)),
                  pl.BlockSpec((tt, tn), lambda i, j, t: (t, j))],
        out_specs=pl.BlockSpec((tk, tn), lambda i, j, t: (i, j)),
        out_shape=jax.ShapeDtypeStruct((K, N), F32),
        compiler_params=_params("parallel", "parallel", "arbitrary"),
    )(a, g)


POOL_W = 512
CONV_W = 512
POOL_GROUP = 128


MIX_ROWS = 64
LN_ROWS = 32
SUB = 8


def _shifted(sh_sc, x, n_rows):
    for b in range(1, SUB):
        sh_sc[b, pl.ds(0, n_rows), :] = x[b:b + n_rows]


def _tap(sh_sc, src, r0, cols, start, rows):
    a, b = divmod(start, SUB)
    if b == 0:
        return src[pl.ds(r0 + SUB * a, rows), cols]
    return sh_sc[b, pl.ds(SUB * a, rows), :]


def _pool_rows(zp_ref, z_ref, cols, win, i, tt, first, pooled_sc):
    RB = min(MIX_ROWS, tt)
    hb = 2 * SUB
    for r in range(tt // RB):
        if r == 0:
            p = zp_ref[pl.ds(HALO - hb, hb), cols]
            v = jnp.concatenate([jnp.where(first, jnp.zeros_like(p), p), z_ref[pl.ds(0, RB), cols]], axis=0)
        else:
            v = z_ref[pl.ds(r * RB - hb, RB + hb), cols]
        u = v[hb:hb + RB]
        s = u
        for j in range(1, win):
            s = s + v[hb - j:hb - j + RB]
        t_glob = i * tt + r * RB + lax.broadcasted_iota(jnp.int32, (RB, 1), 0)
        cnt = jnp.minimum(t_glob + 1, win).astype(F32)
        pooled_sc[pl.ds(r * RB, RB), :] = (s / cnt - u).astype(pooled_sc.dtype)


def _fill_gl(gl_sc, zp_ref, z_ref, zn_ref, tt, first, last):
    ca, cb = pl.ds(POOL_W, CONV_W), pl.ds(POOL_W + CONV_W, CONV_W)
    g = zp_ref[:, ca] * _sigmoid(zp_ref[:, cb])
    gl_sc[pl.ds(0, HALO), :] = jnp.where(first, jnp.zeros_like(g), g)

    def rows(r, carry):
        r0 = pl.multiple_of(r * LN_ROWS, LN_ROWS)
        gl_sc[pl.ds(HALO + r0, LN_ROWS), :] = z_ref[pl.ds(r0, LN_ROWS), ca] * _sigmoid(z_ref[pl.ds(r0, LN_ROWS), cb])
        return carry

    lax.fori_loop(0, tt // LN_ROWS, rows, 0)
    if zn_ref is not None:
        g = zn_ref[:, ca] * _sigmoid(zn_ref[:, cb])
        gl_sc[pl.ds(HALO + tt, HALO), :] = jnp.where(last, jnp.zeros_like(g), g)


def _conv_rows(gl_sc, cv_sc, sh_sc, w_ref, b_ref, n_rows):
    RB = min(MIX_ROWS, n_rows)
    for c in range(CONV_W // LANES):
        cols = pl.ds(c * LANES, LANES)
        bias = b_ref[:, cols]

        def chunk(r0, rb):
            g = gl_sc[pl.ds(r0, rb + HALO), cols]
            _shifted(sh_sc, g, rb + HALO - SUB)
            cv = jnp.zeros((rb, LANES), F32) + bias
            for j in range(CONV_K):
                cv = cv + w_ref[pl.ds(j, 1), cols] * _tap(sh_sc, gl_sc, r0, cols, HALO - (CONV_K - 1) + j, rb)
            cv_sc[pl.ds(r0, rb), cols] = cv

        def body(r, carry):
            chunk(pl.multiple_of(r * RB, RB), RB)
            return carry

        lax.fori_loop(0, n_rows // RB, body, 0)
        if n_rows % RB:
            chunk((n_rows // RB) * RB, n_rows % RB)


def _mixer_fwd(z, pool_w, pool_scale, dw_w, dw_b, ln_g, ln_b, *, name, tt=512):
    T, C = z.shape
    tt = min(tt, T)
    n = T // tt
    hb = tt // HALO

    def body(zp_ref, z_ref, pw_ref, ps_ref, w_ref, b_ref, g_ref, bb_ref, o_ref, pooled_sc, gl_sc, cv_sc, sh_sc):
        i = pl.program_id(0)
        first = i == 0
        for gi, win in enumerate(POOL_WINDOWS):
            cols = pl.ds(gi * POOL_GROUP, POOL_GROUP)
            _pool_rows(zp_ref, z_ref, cols, win, i, tt, first, pooled_sc)
            ya = jnp.dot(pooled_sc[...], pw_ref[gi].astype(CD), preferred_element_type=F32)
            o_ref[:, cols] = (ya * ps_ref[:, cols]).astype(o_ref.dtype)
        _fill_gl(gl_sc, zp_ref, z_ref, None, tt, first, None)
        _conv_rows(gl_sc, cv_sc, sh_sc, w_ref, b_ref, tt)

        def ln_rows(r, carry):
            rows = pl.ds(pl.multiple_of(r * LN_ROWS, LN_ROWS), LN_ROWS)
            cv = cv_sc[rows, :]
            xc = cv - jnp.mean(cv, axis=-1, keepdims=True)
            yn = xc * lax.rsqrt(jnp.mean(xc * xc, axis=-1, keepdims=True) + EPS) * g_ref[...] + bb_ref[...]
            o_ref[rows, pl.ds(POOL_W, CONV_W)] = (yn * _sigmoid(yn)).astype(o_ref.dtype)
            return carry

        lax.fori_loop(0, tt // LN_ROWS, ln_rows, 0, unroll=4)

    full = lambda shape: pl.BlockSpec(shape, lambda i: (0,) * len(shape))
    return pl.pallas_call(
        body, name=name, grid=(n,),
        in_specs=[pl.BlockSpec((HALO, C), lambda i: (jnp.maximum(i * hb - 1, 0), 0)),
                  pl.BlockSpec((tt, C), lambda i: (i, 0)),
                  full((4, POOL_GROUP, POOL_GROUP)), full((1, POOL_W)), full((CONV_K + 1, CONV_W)),
                  full((1, CONV_W)), full((1, CONV_W)), full((1, CONV_W))],
        out_specs=pl.BlockSpec((tt, POOL_W + CONV_W), lambda i: (i, 0)),
        out_shape=jax.ShapeDtypeStruct((T, POOL_W + CONV_W), CD),
        scratch_shapes=[pltpu.VMEM((tt, POOL_GROUP), CD), pltpu.VMEM((tt + HALO, CONV_W), F32),
                        pltpu.VMEM((tt, CONV_W), F32), pltpu.VMEM((SUB, MIX_ROWS + HALO, LANES), F32)],
        compiler_params=_params("parallel"),
    )(z, z, pool_w, pool_scale, dw_w, dw_b, ln_g, ln_b)


def _mixer_bwd(z, dy, pool_w, pool_scale, dw_w, dw_b, ln_g, ln_b, *, name, tt=512):
    T, C = z.shape
    tt = min(tt, T)
    n = T // tt
    hb = tt // HALO
    R = tt + HALO
    RB = min(MIX_ROWS, tt)

    def body(zp_ref, z_ref, zn_ref, dy_ref, dyn_ref, pw_ref, ps_ref, w_ref, b_ref, g_ref, bb_ref,
             dz_ref, dpw_ref, dps_ref, dw_ref, db_ref, dg_ref, dbb_ref,
             pooled_sc, dm_sc, dpool_sc, dpe_sc, gl_sc, cv_sc, accw, accl, sh_sc, shd_sc):
        i = pl.program_id(0)
        first, last = i == 0, i == n - 1

        @pl.when(first)
        def _():
            for r in (dpw_ref, dps_ref, dw_ref, db_ref, dg_ref, dbb_ref):
                r[...] = jnp.zeros_like(r)

        def dy_rows(cols):
            nxt = dyn_ref[:, cols]
            return jnp.concatenate([dy_ref[:, cols], jnp.where(last, jnp.zeros_like(nxt), nxt)], axis=0)

        t_all = i * tt + lax.broadcasted_iota(jnp.int32, (R, 1), 0)
        for gi, win in enumerate(POOL_WINDOWS):
            cols = pl.ds(gi * POOL_GROUP, POOL_GROUP)
            _pool_rows(zp_ref, z_ref, cols, win, i, tt, first, pooled_sc)
            pw = pw_ref[gi].astype(CD)
            dya = dy_rows(cols)
            mm = jnp.dot(pooled_sc[...], pw, preferred_element_type=F32)
            dps_ref[:, cols] += jnp.sum(dya[:tt] * mm, axis=0, keepdims=True)
            dm_sc[...] = (dya * ps_ref[:, cols]).astype(CD)
            dpw_ref[gi] += lax.dot_general(pooled_sc[...], dm_sc[pl.ds(0, tt), :], TN, preferred_element_type=F32)
            dpool = lax.dot_general(dm_sc[...], pw, NT, preferred_element_type=F32)
            dpool_sc[...] = dpool
            dpe_sc[...] = dpool / jnp.minimum(t_all + 1, win).astype(F32)

            def du_rows(r, carry):
                r0 = pl.multiple_of(r * RB, RB)
                e = dpe_sc[pl.ds(r0, RB + 2 * SUB), :]
                du = -dpool_sc[pl.ds(r0, RB), :]
                for j in range(win):
                    du = du + e[j:j + RB]
                dz_ref[pl.ds(r0, RB), cols] = du.astype(dz_ref.dtype)
                return carry

            lax.fori_loop(0, tt // RB, du_rows, 0)

        _fill_gl(gl_sc, zp_ref, z_ref, zn_ref, tt, first, last)
        _conv_rows(gl_sc, cv_sc, sh_sc, w_ref, b_ref, R)
        accl[...] = jnp.zeros_like(accl)

        def ln_rows(r0, in_tile):
            rows = pl.ds(r0, LN_ROWS)
            cv = cv_sc[rows, :]
            xc = cv - jnp.mean(cv, axis=-1, keepdims=True)
            rstd = lax.rsqrt(jnp.mean(xc * xc, axis=-1, keepdims=True) + EPS)
            xhat = xc * rstd
            yn = xhat * g_ref[...] + bb_ref[...]
            sy = _sigmoid(yn)
            if in_tile:
                dyv = dy_ref[rows, pl.ds(POOL_W, CONV_W)]
            else:
                nxt = dyn_ref[:, pl.ds(POOL_W, CONV_W)]
                dyv = jnp.where(last, jnp.zeros_like(nxt), nxt)
            dyn = dyv * (sy * (1.0 + yn * (1.0 - sy)))
            if in_tile:
                accl[pl.ds(0, SUB), :] += jnp.sum((dyn * xhat).reshape(LN_ROWS // SUB, SUB, CONV_W), axis=0)
                accl[pl.ds(SUB, SUB), :] += jnp.sum(dyn.reshape(LN_ROWS // SUB, SUB, CONV_W), axis=0)
            dxh = dyn * g_ref[...]
            dcv = rstd * (dxh - jnp.mean(dxh, axis=-1, keepdims=True)
                          - xhat * jnp.mean(dxh * xhat, axis=-1, keepdims=True))
            cv_sc[rows, :] = dcv
            if in_tile:
                accl[pl.ds(2 * SUB, SUB), :] += jnp.sum(dcv.reshape(LN_ROWS // SUB, SUB, CONV_W), axis=0)

        def ln_body(r, carry):
            ln_rows(pl.multiple_of(r * LN_ROWS, LN_ROWS), True)
            return carry

        lax.fori_loop(0, tt // LN_ROWS, ln_body, 0, unroll=2)
        ln_rows(tt, False)
        dg_ref[...] += jnp.sum(accl[pl.ds(0, SUB), :], axis=0, keepdims=True)
        dbb_ref[...] += jnp.sum(accl[pl.ds(SUB, SUB), :], axis=0, keepdims=True)
        db_ref[...] += jnp.sum(accl[pl.ds(2 * SUB, SUB), :], axis=0, keepdims=True)

        accw[...] = jnp.zeros_like(accw)
        for c in range(CONV_W // LANES):
            cols = pl.ds(c * LANES, LANES)

            def chunk(r, carry):
                r0 = pl.multiple_of(r * RB, RB)
                d = cv_sc[pl.ds(r0, RB + HALO), cols]
                g = gl_sc[pl.ds(r0, RB + HALO), cols]
                _shifted(shd_sc, d, RB + HALO - SUB)
                _shifted(sh_sc, g, RB + HALO - SUB)
                d_t = d[:RB]
                dgl = jnp.zeros((RB, LANES), F32)
                for j in range(CONV_K):
                    dgl = dgl + w_ref[pl.ds(j, 1), cols] * _tap(shd_sc, cv_sc, r0, cols, CONV_K - 1 - j, RB)
                    prod = d_t * _tap(sh_sc, gl_sc, r0, cols, HALO - (CONV_K - 1) + j, RB)
                    accw[pl.ds(SUB * j, SUB), cols] += jnp.sum(prod.reshape(RB // SUB, SUB, LANES), axis=0)
                a_t = z_ref[pl.ds(r0, RB), pl.ds(POOL_W + c * LANES, LANES)]
                sb = _sigmoid(z_ref[pl.ds(r0, RB), pl.ds(POOL_W + CONV_W + c * LANES, LANES)])
                dz_ref[pl.ds(r0, RB), pl.ds(POOL_W + c * LANES, LANES)] = (dgl * sb).astype(dz_ref.dtype)
                dz_ref[pl.ds(r0, RB), pl.ds(POOL_W + CONV_W + c * LANES, LANES)] = (
                    dgl * a_t * sb * (1.0 - sb)).astype(dz_ref.dtype)
                return carry

            lax.fori_loop(0, tt // RB, chunk, 0)
        for j in range(CONV_K):
            dw_ref[pl.ds(j, 1), :] += jnp.sum(accw[pl.ds(SUB * j, SUB), :], axis=0, keepdims=True)

    full = lambda shape: pl.BlockSpec(shape, lambda i: (0,) * len(shape))
    nb = T // HALO
    outs = pl.pallas_call(
        body, name=name, grid=(n,),
        in_specs=[pl.BlockSpec((HALO, C), lambda i: (jnp.maximum(i * hb - 1, 0), 0)),
                  pl.BlockSpec((tt, C), lambda i: (i, 0)),
                  pl.BlockSpec((HALO, C), lambda i: (jnp.minimum((i + 1) * hb, nb - 1), 0)),
                  pl.BlockSpec((tt, 2 * POOL_W), lambda i: (i, 0)),
                  pl.BlockSpec((HALO, 2 * POOL_W), lambda i: (jnp.minimum((i + 1) * hb, nb - 1), 0)),
                  full((4, POOL_GROUP, POOL_GROUP)), full((1, POOL_W)), full((CONV_K + 1, CONV_W)),
                  full((1, CONV_W)), full((1, CONV_W)), full((1, CONV_W))],
        out_specs=[pl.BlockSpec((tt, C), lambda i: (i, 0)),
                   full((4, POOL_GROUP, POOL_GROUP)), full((1, POOL_W)), full((CONV_K + 1, CONV_W)),
                   full((1, CONV_W)), full((1, CONV_W)), full((1, CONV_W))],
        out_shape=[jax.ShapeDtypeStruct((T, C), CD),
                   jax.ShapeDtypeStruct((4, POOL_GROUP, POOL_GROUP), F32),
                   jax.ShapeDtypeStruct((1, POOL_W), F32),
                   jax.ShapeDtypeStruct((CONV_K + 1, CONV_W), F32),
                   jax.ShapeDtypeStruct((1, CONV_W), F32),
                   jax.ShapeDtypeStruct((1, CONV_W), F32),
                   jax.ShapeDtypeStruct((1, CONV_W), F32)],
        scratch_shapes=[pltpu.VMEM((tt, POOL_GROUP), CD), pltpu.VMEM((R, POOL_GROUP), CD),
                        pltpu.VMEM((R, POOL_GROUP), F32), pltpu.VMEM((R, POOL_GROUP), F32),
                        pltpu.VMEM((tt + 2 * HALO, CONV_W), F32), pltpu.VMEM((R, CONV_W), F32),
                        pltpu.VMEM((SUB * (CONV_K + 1), CONV_W), F32), pltpu.VMEM((3 * SUB, CONV_W), F32),
                        pltpu.VMEM((SUB, MIX_ROWS + HALO, LANES), F32),
                        pltpu.VMEM((SUB, MIX_ROWS + HALO, LANES), F32)],
        compiler_params=_params("arbitrary"),
    )(z, z, z, dy, dy, pool_w, pool_scale, dw_w, dw_b, ln_g, ln_b)
    return outs


CHUNK_HALO = 16
FFN_ROWS = 64
FFN_LANES = 128


def _rows(cur, prev, nxt, r, rb, before, after, cols, n_r, first, last):
    lo, hi = r * rb - before, r * rb + rb + after
    tt = n_r * rb
    parts = []
    if lo < 0:
        p = prev[pl.ds(HALO + lo, -lo), cols]
        parts.append(jnp.where(first, jnp.zeros_like(p), p))
        lo = 0
    parts.append(cur[pl.ds(lo, min(hi, tt) - lo), cols])
    if hi > tt:
        p = nxt[pl.ds(0, hi - tt), cols]
        parts.append(jnp.where(last, jnp.zeros_like(p), p))
    return parts[0] if len(parts) == 1 else jnp.concatenate(parts, axis=0)


def _ffn_mid_fwd(up, cw, cb, *, name, tt=512):
    T = up.shape[0]
    tt = min(tt, T)
    n = T // tt
    hb = tt // HALO
    RB, CW, HB = min(FFN_ROWS, tt), FFN_LANES, CHUNK_HALO
    n_r = tt // RB

    def body(a_ref, gp_ref, g_ref, w_ref, b_ref, o_ref):
        first = pl.program_id(0) == 0

        def col_chunk(c, carry):
            cols = pl.ds(pl.multiple_of(c * CW, CW), CW)
            w = w_ref[:, cols]
            b = b_ref[:, cols]
            for r in range(n_r):
                v = _rows(g_ref, gp_ref, None, r, RB, HB, 0, cols, n_r, first, None).astype(F32)
                gc = b + w[0:1] * v[HB - 2:HB - 2 + RB] + w[1:2] * v[HB - 1:HB - 1 + RB] + w[2:3] * v[HB:HB + RB]
                a = a_ref[pl.ds(r * RB, RB), cols].astype(F32)
                o_ref[pl.ds(r * RB, RB), cols] = (gc * _sigmoid(gc) * a).astype(o_ref.dtype)
            return carry

        lax.fori_loop(0, D_FF // CW, col_chunk, 0)

    return pl.pallas_call(
        body, name=name, grid=(n,),
        in_specs=[pl.BlockSpec((tt, D_FF), lambda i: (i, 0)),
                  pl.BlockSpec((HALO, D_FF), lambda i: (jnp.maximum(i * hb - 1, 0), 1)),
                  pl.BlockSpec((tt, D_FF), lambda i: (i, 1)),
                  pl.BlockSpec((8, D_FF), lambda i: (0, 0)),
                  pl.BlockSpec((1, D_FF), lambda i: (0, 0))],
        out_specs=pl.BlockSpec((tt, D_FF), lambda i: (i, 0)),
        out_shape=jax.ShapeDtypeStruct((T, D_FF), CD),
        compiler_params=_params("parallel"),
    )(up, up, up, cw, cb)


def _ffn_mid_bwd(up, dact, cw, cb, *, name, tt=512):
    T = up.shape[0]
    tt = min(tt, T)
    n = T // tt
    hb = tt // HALO
    nb = T // HALO
    RB, CW, HB = min(FFN_ROWS, tt), FFN_LANES, CHUNK_HALO
    n_r = tt // RB
    RE = RB + 8

    def body(a_ref, an_ref, gp_ref, g_ref, gn_ref, d_ref, dn_ref, w_ref, b_ref, dup_ref, dw_ref, db_ref, acc):
        i = pl.program_id(0)
        first, last = i == 0, i == n - 1

        @pl.when(first)
        def _():
            dw_ref[...] = jnp.zeros_like(dw_ref)
            db_ref[...] = jnp.zeros_like(db_ref)

        def col_chunk(c, carry):
            cols = pl.ds(pl.multiple_of(c * CW, CW), CW)
            w = w_ref[:, cols]
            b = b_ref[:, cols]
            part = [jnp.zeros((8, CW), F32) for _ in range(FFN_K + 1)]
            for r in range(n_r):
                v = _rows(g_ref, gp_ref, gn_ref, r, RB, HB, HB, cols, n_r, first, last).astype(F32)
                gs = [v[HB - 2 + j:HB - 2 + j + RE] for j in range(FFN_K)]
                gc = b + w[0:1] * gs[0] + w[1:2] * gs[1] + w[2:3] * gs[2]
                sg = _sigmoid(gc)
                d = _rows(d_ref, None, dn_ref, r, RB, 0, HB, cols, n_r, None, last).astype(F32)[:RE]
                a = _rows(a_ref, None, an_ref, r, RB, 0, HB, cols, n_r, None, last).astype(F32)[:RE]
                silu = gc * sg
                dgc = d * a * (sg + silu - silu * sg)
                dup_ref[pl.ds(r * RB, RB), cols] = (d[:RB] * silu[:RB]).astype(dup_ref.dtype)
                dg = w[2:3] * dgc[0:RB] + w[1:2] * dgc[1:RB + 1] + w[0:1] * dgc[2:RB + 2]
                dup_ref[pl.ds(r * RB, RB), pl.ds(pl.multiple_of(D_FF + c * CW, CW), CW)] = dg.astype(dup_ref.dtype)
                dgc_t = dgc[:RB]
                for j in range(FFN_K):
                    part[j] = part[j] + jnp.sum((dgc_t * gs[j][:RB]).reshape(RB // 8, 8, CW), axis=0)
                part[FFN_K] = part[FFN_K] + jnp.sum(dgc_t.reshape(RB // 8, 8, CW), axis=0)
            for j in range(FFN_K + 1):
                acc[pl.ds(8 * j, 8), cols] = part[j]
            return carry

        lax.fori_loop(0, D_FF // CW, col_chunk, 0)
        for j in range(FFN_K):
            dw_ref[pl.ds(j, 1), :] += jnp.sum(acc[pl.ds(8 * j, 8), :], axis=0, keepdims=True)
        db_ref[...] += jnp.sum(acc[pl.ds(8 * FFN_K, 8), :], axis=0, keepdims=True)

    nxt = lambda i: jnp.minimum((i + 1) * hb, nb - 1)
    return pl.pallas_call(
        body, name=name, grid=(n,),
        in_specs=[pl.BlockSpec((tt, D_FF), lambda i: (i, 0)),
                  pl.BlockSpec((HALO, D_FF), lambda i: (nxt(i), 0)),
                  pl.BlockSpec((HALO, D_FF), lambda i: (jnp.maximum(i * hb - 1, 0), 1)),
                  pl.BlockSpec((tt, D_FF), lambda i: (i, 1)),
                  pl.BlockSpec((HALO, D_FF), lambda i: (nxt(i), 1)),
                  pl.BlockSpec((tt, D_FF), lambda i: (i, 0)),
                  pl.BlockSpec((HALO, D_FF), lambda i: (nxt(i), 0)),
                  pl.BlockSpec((8, D_FF), lambda i: (0, 0)),
                  pl.BlockSpec((1, D_FF), lambda i: (0, 0))],
        out_specs=[pl.BlockSpec((tt, 2 * D_FF), lambda i: (i, 0)),
                   pl.BlockSpec((8, D_FF), lambda i: (0, 0)),
                   pl.BlockSpec((1, D_FF), lambda i: (0, 0))],
        out_shape=[jax.ShapeDtypeStruct((T, 2 * D_FF), CD),
                   jax.ShapeDtypeStruct((8, D_FF), F32),
                   jax.ShapeDtypeStruct((1, D_FF), F32)],
        scratch_shapes=[pltpu.VMEM((8 * (FFN_K + 1), D_FF), F32)],
        compiler_params=_params("arbitrary"),
    )(up, up, up, up, up, dact, dact, cw, cb)


def _xattn_probs(q, k):
    s = lax.dot_general(q, k, NT, preferred_element_type=F32) * XA_SCALE
    p = jnp.exp(s - jnp.max(s, axis=-1, keepdims=True))
    return p / jnp.sum(p, axis=-1, keepdims=True)


def _xattn_fwd(q, kv, *, name, tq=512):
    T = q.shape[0]
    tq = min(tq, T)

    def body(q_ref, kv_ref, o_ref):
        for h in range(XA_HEADS):
            cols = pl.ds(h * XA_DH, XA_DH)
            p = _xattn_probs(q_ref[:, cols], kv_ref[:, cols])
            v = kv_ref[:, pl.ds(D_MODEL + h * XA_DH, XA_DH)]
            o_ref[:, cols] = jnp.dot(p.astype(CD), v, preferred_element_type=F32).astype(o_ref.dtype)

    return pl.pallas_call(
        body, name=name, grid=(T // tq,),
        in_specs=[pl.BlockSpec((tq, D_MODEL), lambda i: (i, 0)),
                  pl.BlockSpec((MEM_LEN, 2 * D_MODEL), lambda i: (0, 0))],
        out_specs=pl.BlockSpec((tq, D_MODEL), lambda i: (i, 0)),
        out_shape=jax.ShapeDtypeStruct((T, D_MODEL), CD),
        compiler_params=_params("parallel"),
    )(q, kv)


def _xattn_bwd(q, kv, do, *, name, tq=512):
    T = q.shape[0]
    tq = min(tq, T)

    def body(q_ref, kv_ref, do_ref, dq_ref, dkv_ref):
        @pl.when(pl.program_id(0) == 0)
        def _():
            dkv_ref[...] = jnp.zeros_like(dkv_ref)

        for h in range(XA_HEADS):
            cols = pl.ds(h * XA_DH, XA_DH)
            vcols = pl.ds(D_MODEL + h * XA_DH, XA_DH)
            qh, kh, vh, doh = q_ref[:, cols], kv_ref[:, cols], kv_ref[:, vcols], do_ref[:, cols]
            p = _xattn_probs(qh, kh)
            dkv_ref[:, vcols] += lax.dot_general(p.astype(CD), doh, TN, preferred_element_type=F32)
            dp = lax.dot_general(doh, vh, NT, preferred_element_type=F32)
            ds = (p * (dp - jnp.sum(dp * p, axis=-1, keepdims=True)) * XA_SCALE).astype(CD)
            dq_ref[:, cols] = jnp.dot(ds, kh, preferred_element_type=F32).astype(dq_ref.dtype)
            dkv_ref[:, cols] += lax.dot_general(ds, qh, TN, preferred_element_type=F32)

    return pl.pallas_call(
        body, name=name, grid=(T // tq,),
        in_specs=[pl.BlockSpec((tq, D_MODEL), lambda i: (i, 0)),
                  pl.BlockSpec((MEM_LEN, 2 * D_MODEL), lambda i: (0, 0)),
                  pl.BlockSpec((tq, D_MODEL), lambda i: (i, 0))],
        out_specs=[pl.BlockSpec((tq, D_MODEL), lambda i: (i, 0)),
                   pl.BlockSpec((MEM_LEN, 2 * D_MODEL), lambda i: (0, 0))],
        out_shape=[jax.ShapeDtypeStruct((T, D_MODEL), CD),
                   jax.ShapeDtypeStruct((MEM_LEN, 2 * D_MODEL), F32)],
        compiler_params=_params("arbitrary"),
    )(q, kv, do)


C_W = Q_LORA + KV_LORA + LANES


def _rot(x):
    lane = lax.broadcasted_iota(jnp.int32, x.shape, x.ndim - 1)
    up = pltpu.roll(x, LANES - QK_ROPE // 2, x.ndim - 1)
    dn = pltpu.roll(x, QK_ROPE // 2, x.ndim - 1)
    lo, mid, hi = QK_NOPE, QK_NOPE + QK_ROPE // 2, QK_NOPE + QK_ROPE
    return jnp.where((lane >= lo) & (lane < mid), -up, jnp.where((lane >= mid) & (lane < hi), dn, 0.0))


def _mla_mid_fwd(c, qg, kvg, cs, sn, *, name, tt=512):
    T = c.shape[0]
    tt = min(tt, T)

    def body(c_ref, qg_ref, kg_ref, cs_ref, sn_ref, qn_ref, kn_ref, kpe_ref):
        cq = c_ref[:, pl.ds(0, Q_LORA)]
        qn_ref[...] = (cq * lax.rsqrt(jnp.mean(cq * cq, axis=-1, keepdims=True) + EPS)
                       * qg_ref[...]).astype(qn_ref.dtype)
        ck = c_ref[:, pl.ds(Q_LORA, KV_LORA)]
        kn_ref[...] = (ck * lax.rsqrt(jnp.mean(ck * ck, axis=-1, keepdims=True) + EPS)
                       * kg_ref[...]).astype(kn_ref.dtype)
        kp = c_ref[:, pl.ds(Q_LORA + KV_LORA, LANES)]
        kpe_ref[...] = kp * cs_ref[...] + _rot(kp) * sn_ref[...]

    row = lambda w: pl.BlockSpec((tt, w), lambda i: (i, 0))
    one = lambda w: pl.BlockSpec((1, w), lambda i: (0, 0))
    return pl.pallas_call(
        body, name=name, grid=(T // tt,),
        in_specs=[row(C_W), one(Q_LORA), one(KV_LORA), row(LANES), row(LANES)],
        out_specs=[row(Q_LORA), row(KV_LORA), row(LANES)],
        out_shape=[jax.ShapeDtypeStruct((T, Q_LORA), CD), jax.ShapeDtypeStruct((T, KV_LORA), CD),
                   jax.ShapeDtypeStruct((T, LANES), F32)],
        compiler_params=_params("parallel"),
    )(c, qg, kvg, cs, sn)


def _mla_mid_bwd(c, dqn, dkvn, dksum, qg, kvg, cs, sn, *, name, tt=512):
    T = c.shape[0]
    tt = min(tt, T)

    def body(c_ref, dq_ref, dk_ref, ds_ref, qg_ref, kg_ref, cs_ref, sn_ref, dc_ref, dqg_ref, dkg_ref):
        @pl.when(pl.program_id(0) == 0)
        def _():
            dqg_ref[...] = jnp.zeros_like(dqg_ref)
            dkg_ref[...] = jnp.zeros_like(dkg_ref)

        dx, dg = _rms_bwd(c_ref[:, pl.ds(0, Q_LORA)], qg_ref[...], dq_ref[...])
        dc_ref[:, pl.ds(0, Q_LORA)] = dx.astype(dc_ref.dtype)
        dqg_ref[...] += jnp.sum(dg, axis=0, keepdims=True)
        dx, dg = _rms_bwd(c_ref[:, pl.ds(Q_LORA, KV_LORA)], kg_ref[...], dk_ref[...])
        dc_ref[:, pl.ds(Q_LORA, KV_LORA)] = dx.astype(dc_ref.dtype)
        dkg_ref[...] += jnp.sum(dg, axis=0, keepdims=True)
        d = ds_ref[...]
        lane = lax.broadcasted_iota(jnp.int32, d.shape, 1)
        dkp = d * cs_ref[...] - _rot(d * sn_ref[...])
        dc_ref[:, pl.ds(Q_LORA + KV_LORA, LANES)] = jnp.where(
            (lane >= QK_NOPE) & (lane < QK_NOPE + QK_ROPE), dkp, 0.0).astype(dc_ref.dtype)

    row = lambda w: pl.BlockSpec((tt, w), lambda i: (i, 0))
    one = lambda w: pl.BlockSpec((1, w), lambda i: (0, 0))
    return pl.pallas_call(
        body, name=name, grid=(T // tt,),
        in_specs=[row(C_W), row(Q_LORA), row(KV_LORA), row(LANES), one(Q_LORA), one(KV_LORA),
                  row(LANES), row(LANES)],
        out_specs=[row(C_W), one(Q_LORA), one(KV_LORA)],
        out_shape=[jax.ShapeDtypeStruct((T, C_W), CD), jax.ShapeDtypeStruct((1, Q_LORA), F32),
                   jax.ShapeDtypeStruct((1, KV_LORA), F32)],
        compiler_params=_params("arbitrary"),
    )(c, dqn, dkvn, dksum, qg, kvg, cs, sn)


def _mla_qkv_fwd(qn, kvn, kpe, cs, sn, wq, wk, wv, *, name, tt=256):
    T = qn.shape[0]
    tt = min(tt, T)
    H = MLA_HEADS

    def body(qn_ref, kn_ref, kpe_ref, cs_ref, sn_ref, wq_ref, wk_ref, wv_ref, q_ref, k_ref, v_ref):
        qn_v, kn_v, kpe_v, cs_v, sn_v = qn_ref[...], kn_ref[...], kpe_ref[...], cs_ref[...], sn_ref[...]
        for h in range(H):
            q = jnp.dot(qn_v, wq_ref[h], preferred_element_type=F32)
            q_ref[h] = (q * cs_v + _rot(q) * sn_v).astype(q_ref.dtype)
            k_ref[h] = (jnp.dot(kn_v, wk_ref[h], preferred_element_type=F32) + kpe_v).astype(k_ref.dtype)
            v_ref[h] = jnp.dot(kn_v, wv_ref[h], preferred_element_type=F32).astype(v_ref.dtype)

    row = lambda w: pl.BlockSpec((tt, w), lambda i: (i, 0))
    wsp = lambda k: pl.BlockSpec((H, k, LANES), lambda i: (0, 0, 0))
    hsp = pl.BlockSpec((H, tt, LANES), lambda i: (0, i, 0))
    sh = jax.ShapeDtypeStruct((H, T, LANES), CD)
    return pl.pallas_call(
        body, name=name, grid=(T // tt,),
        in_specs=[row(Q_LORA), row(KV_LORA), row(LANES), row(LANES), row(LANES),
                  wsp(Q_LORA), wsp(KV_LORA), wsp(KV_LORA)],
        out_specs=[hsp, hsp, hsp], out_shape=[sh, sh, sh],
        compiler_params=_params("parallel"),
    )(qn, kvn, kpe, cs, sn, wq, wk, wv)


def _mla_qkv_bwd(dq, dk, dv, qn, kvn, cs, sn, wq, wk, wv, *, name, tt=256):
    T = qn.shape[0]
    tt = min(tt, T)
    H = MLA_HEADS

    def body(dq_ref, dk_ref, dv_ref, qn_ref, kn_ref, cs_ref, sn_ref, wq_ref, wk_ref, wv_ref,
             dqn_ref, dkn_ref, dks_ref, dwq_ref, dwk_ref, dwv_ref):
        @pl.when(pl.program_id(0) == 0)
        def _():
            for r in (dwq_ref, dwk_ref, dwv_ref):
                r[...] = jnp.zeros_like(r)

        qn_v, kn_v, cs_v, sn_v = qn_ref[...], kn_ref[...], cs_ref[...], sn_ref[...]
        dqn = jnp.zeros((tt, Q_LORA), F32)
        dkn = jnp.zeros((tt, KV_LORA), F32)
        dks = jnp.zeros((tt, LANES), F32)
        for h in range(H):
            d = dq_ref[h]
            dqh = (d * cs_v - _rot(d * sn_v)).astype(CD)
            dkh, dvh = dk_ref[h], dv_ref[h]
            dqn = dqn + lax.dot_general(dqh, wq_ref[h], NT, preferred_element_type=F32)
            dkn = dkn + lax.dot_general(dkh, wk_ref[h], NT, preferred_element_type=F32)
            dkn = dkn + lax.dot_general(dvh, wv_ref[h], NT, preferred_element_type=F32)
            dks = dks + dkh.astype(F32)
            dwq_ref[h] += lax.dot_general(qn_v, dqh, TN, preferred_element_type=F32)
            dwk_ref[h] += lax.dot_general(kn_v, dkh, TN, preferred_element_type=F32)
            dwv_ref[h] += lax.dot_general(kn_v, dvh, TN, preferred_element_type=F32)
        dqn_ref[...] = dqn
        dkn_ref[...] = dkn
        dks_ref[...] = dks

    row = lambda w: pl.BlockSpec((tt, w), lambda i: (i, 0))
    wsp = lambda k: pl.BlockSpec((H, k, LANES), lambda i: (0, 0, 0))
    hsp = pl.BlockSpec((H, tt, LANES), lambda i: (0, i, 0))
    return pl.pallas_call(
        body, name=name, grid=(T // tt,),
        in_specs=[hsp, hsp, hsp, row(Q_LORA), row(KV_LORA), row(LANES), row(LANES),
                  wsp(Q_LORA), wsp(KV_LORA), wsp(KV_LORA)],
        out_specs=[row(Q_LORA), row(KV_LORA), row(LANES), wsp(Q_LORA), wsp(KV_LORA), wsp(KV_LORA)],
        out_shape=[jax.ShapeDtypeStruct((T, Q_LORA), F32), jax.ShapeDtypeStruct((T, KV_LORA), F32),
                   jax.ShapeDtypeStruct((T, LANES), F32),
                   jax.ShapeDtypeStruct((H, Q_LORA, LANES), F32),
                   jax.ShapeDtypeStruct((H, KV_LORA, LANES), F32),
                   jax.ShapeDtypeStruct((H, KV_LORA, LANES), F32)],
        compiler_params=_params("arbitrary"),
    )(dq, dk, dv, qn, kvn, cs, sn, wq, wk, wv)


FLASH_BLOCK = 1024
EXP2_SCALE = MLA_SCALE * math.log2(math.e)


def _causal_steps(nq, by_key):
    pairs = [(i, j) for j in range(nq) for i in range(j, nq)] if by_key else \
            [(i, j) for i in range(nq) for j in range(i + 1)]
    return (jnp.asarray([p[0] for p in pairs], jnp.int32), jnp.asarray([p[1] for p in pairs], jnp.int32))


def _raw_scores(q, k, masked):
    s = lax.dot_general(q, k, NT, preferred_element_type=F32)
    if masked:
        row = lax.broadcasted_iota(jnp.int32, s.shape, 0)
        col = lax.broadcasted_iota(jnp.int32, s.shape, 1)
        s = jnp.where(col <= row, s, NEG)
    return s


def _flash_fwd(q, k, v, *, name):
    H, T, _ = q.shape
    tq = min(FLASH_BLOCK, T)
    nq = T // tq
    i_tab, j_tab = _causal_steps(nq, by_key=False)

    rb = min(128, tq)

    def body(i_tab, j_tab, q_ref, k_ref, v_ref, o_ref, lse_ref, m_sc, l_sc, acc, s_sc, p_sc):
        t = pl.program_id(1)
        i, j = i_tab[t], j_tab[t]

        @pl.when(j == 0)
        def _():
            m_sc[...] = jnp.full_like(m_sc, NEG)
            l_sc[...] = jnp.zeros_like(l_sc)
            acc[...] = jnp.zeros_like(acc)

        def step(masked):
            lane = lax.broadcasted_iota(jnp.int32, (tq, LANES), 1)
            alphas, pvs = [], []
            for h in range(2):
                s_sc[h] = _raw_scores(q_ref[h], k_ref[h], masked)
                m_prev = m_sc[h]
                m_new = jnp.maximum(m_prev, jnp.max(s_sc[h], axis=-1, keepdims=True))
                alpha = jnp.exp2((m_prev - m_new) * EXP2_SCALE)
                m_sc[h] = m_new
                for r in range(tq // rb):
                    rows = pl.ds(r * rb, rb)
                    m_r = m_sc[h, rows, :]
                    part = jnp.zeros((rb, LANES), F32)
                    for c in range(tq // LANES):
                        cols = pl.ds(c * LANES, LANES)
                        p = jnp.exp2((s_sc[h, rows, cols] - m_r) * EXP2_SCALE)
                        part = part + p
                        p_sc[h, rows, cols] = p.astype(CD)
                    l_sc[h, rows, :] = (alpha[r * rb:(r + 1) * rb] * l_sc[h, rows, :]
                                        + jnp.sum(part, axis=-1, keepdims=True))
                alphas.append(alpha)
                pvs.append(jnp.dot(p_sc[h], v_ref[h], preferred_element_type=F32))
            acc[...] = acc[...] * jnp.where(lane < V_HEAD, alphas[0], alphas[1]) + pvs[0] + pvs[1]

        @pl.when(j < i)
        def _():
            step(False)

        @pl.when(j == i)
        def _():
            step(True)
            lane = lax.broadcasted_iota(jnp.int32, (tq, LANES), 1)
            o_ref[...] = (acc[...] / jnp.where(lane < V_HEAD, l_sc[0], l_sc[1])).astype(o_ref.dtype)
            for h in range(2):
                lse_ref[h] = m_sc[h] * EXP2_SCALE + jnp.log2(l_sc[h])

    qsp = pl.BlockSpec((2, tq, LANES), lambda p, t, it, jt: (p, it[t], 0))
    ksp = pl.BlockSpec((2, tq, LANES), lambda p, t, it, jt: (p, jt[t], 0))
    return pl.pallas_call(
        body, name=name,
        grid_spec=pltpu.PrefetchScalarGridSpec(
            num_scalar_prefetch=2, grid=(H // 2, int(i_tab.shape[0])),
            in_specs=[qsp, ksp, ksp],
            out_specs=[pl.BlockSpec((tq, LANES), lambda p, t, it, jt: (it[t], p)), qsp],
            scratch_shapes=[pltpu.VMEM((2, tq, LANES), F32), pltpu.VMEM((2, tq, LANES), F32),
                            pltpu.VMEM((tq, LANES), F32),
                            pltpu.VMEM((2, tq, tq), F32), pltpu.VMEM((2, tq, tq), CD)]),
        out_shape=[jax.ShapeDtypeStruct((T, H * V_HEAD), CD), jax.ShapeDtypeStruct((H, T, LANES), F32)],
        compiler_params=_params("parallel", "arbitrary"),
    )(i_tab, j_tab, q, k, v)


def _flash_delta(o, do, *, name, tt=512):
    T = o.shape[0]
    tt = min(tt, T)
    H = MLA_HEADS

    def body(o_ref, do_ref, dl_ref):
        lane = lax.broadcasted_iota(jnp.int32, (tt, LANES), 1)
        for p in range(H // 2):
            cols = pl.ds(p * LANES, LANES)
            prod = do_ref[:, cols].astype(F32) * o_ref[:, cols].astype(F32)
            d0 = jnp.sum(jnp.where(lane < V_HEAD, prod, 0.0), axis=-1, keepdims=True)
            d1 = jnp.sum(jnp.where(lane < V_HEAD, 0.0, prod), axis=-1, keepdims=True)
            dl_ref[2 * p] = jnp.broadcast_to(d0, (tt, LANES))
            dl_ref[2 * p + 1] = jnp.broadcast_to(d1, (tt, LANES))

    row = pl.BlockSpec((tt, H * V_HEAD), lambda i: (i, 0))
    return pl.pallas_call(
        body, name=name, grid=(T // tt,), in_specs=[row, row],
        out_specs=pl.BlockSpec((H, tt, LANES), lambda i: (0, i, 0)),
        out_shape=jax.ShapeDtypeStruct((H, T, LANES), F32),
        compiler_params=_params("parallel"),
    )(o, do)


def _flash_bwd(q, k, v, do, lse, delta, *, name):
    H, T, _ = q.shape
    tq = min(FLASH_BLOCK, T)
    nq = T // tq
    i_tab, j_tab = _causal_steps(nq, by_key=True)

    def body(i_tab, j_tab, q_ref, k_ref, v_ref, do_ref, lse_ref, dl_ref, dq_ref, dk_ref, dv_ref, dk_acc, dv_acc):
        t = pl.program_id(1)
        i, j = i_tab[t], j_tab[t]
        rows = pl.ds(pl.multiple_of(i * tq, tq), tq)

        @pl.when(t == 0)
        def _():
            dq_ref[...] = jnp.zeros_like(dq_ref)

        def step(masked):
            do_v = do_ref[...]
            for h in range(2):
                s = _raw_scores(q_ref[h], k_ref[h], masked)
                p = jnp.exp2(s * EXP2_SCALE - lse_ref[h][:, :1])
                dv_acc[h] += lax.dot_general(p.astype(CD), do_v, TN, preferred_element_type=F32)
                dp = lax.dot_general(do_v, v_ref[h], NT, preferred_element_type=F32)
                ds = (p * (dp - dl_ref[h][:, :1]) * MLA_SCALE).astype(CD)
                dk_acc[h] += lax.dot_general(ds, q_ref[h], TN, preferred_element_type=F32)
                dq_ref[h, rows, :] += jnp.dot(ds, k_ref[h], preferred_element_type=F32)

        @pl.when(i == j)
        def _():
            dk_acc[...] = jnp.zeros_like(dk_acc)
            dv_acc[...] = jnp.zeros_like(dv_acc)
            step(True)

        @pl.when(i > j)
        def _():
            step(False)

        @pl.when(i == nq - 1)
        def _():
            lane = lax.broadcasted_iota(jnp.int32, (tq, LANES), 1)
            dk_ref[...] = dk_acc[...].astype(dk_ref.dtype)
            dv_ref[0] = jnp.where(lane < V_HEAD, dv_acc[0], 0.0).astype(dv_ref.dtype)
            dv_ref[1] = jnp.where(lane < V_HEAD, 0.0, dv_acc[1]).astype(dv_ref.dtype)

    qsp = pl.BlockSpec((2, tq, LANES), lambda p, t, it, jt: (p, it[t], 0))
    ksp = pl.BlockSpec((2, tq, LANES), lambda p, t, it, jt: (p, jt[t], 0))
    osp = pl.BlockSpec((tq, LANES), lambda p, t, it, jt: (it[t], p))
    sh = jax.ShapeDtypeStruct((H, T, LANES), CD)
    return pl.pallas_call(
        body, name=name,
        grid_spec=pltpu.PrefetchScalarGridSpec(
            num_scalar_prefetch=2, grid=(H // 2, int(i_tab.shape[0])),
            in_specs=[qsp, ksp, ksp, osp, qsp, qsp],
            out_specs=[pl.BlockSpec((2, T, LANES), lambda p, t, it, jt: (p, 0, 0)), ksp, ksp],
            scratch_shapes=[pltpu.VMEM((2, tq, LANES), F32), pltpu.VMEM((2, tq, LANES), F32)]),
        out_shape=[jax.ShapeDtypeStruct((H, T, LANES), F32), sh, sh],
        compiler_params=_params("parallel", "arbitrary"),
    )(i_tab, j_tab, q, k, v, do, lse, delta)


def _loss_head(x, g, target, *, name, tt=512):
    T, D = x.shape
    tt = min(tt, T)

    def body(x_ref, g_ref, t_ref, dx_ref, dg_ref, loss_ref):
        @pl.when(pl.program_id(0) == 0)
        def _():
            dg_ref[...] = jnp.zeros_like(dg_ref)
            loss_ref[...] = jnp.zeros_like(loss_ref)

        xv, gv = x_ref[...], g_ref[...]
        r = lax.rsqrt(jnp.mean(xv * xv, axis=-1, keepdims=True) + EPS)
        err = xv * r * gv - t_ref[...]
        tok = jnp.mean(err * err, axis=-1, keepdims=True)
        loss_ref[...] += 0.5 * jnp.sum(tok, axis=0, keepdims=True)
        dx, dg_rows = _rms_bwd(xv, gv, err * (1.0 / D))
        dx_ref[...] = dx
        dg_ref[...] += jnp.sum(dg_rows, axis=0, keepdims=True)

    return pl.pallas_call(
        body, name=name, grid=(T // tt,),
        in_specs=[pl.BlockSpec((tt, D), lambda i: (i, 0)), pl.BlockSpec((1, D), lambda i: (0, 0)),
                  pl.BlockSpec((tt, D), lambda i: (i, 0))],
        out_specs=[pl.BlockSpec((tt, D), lambda i: (i, 0)), pl.BlockSpec((1, D), lambda i: (0, 0)),
                   pl.BlockSpec((1, LANES), lambda i: (0, 0))],
        out_shape=[jax.ShapeDtypeStruct((T, D), F32), jax.ShapeDtypeStruct((1, D), F32),
                   jax.ShapeDtypeStruct((1, LANES), F32)],
        compiler_params=_params("arbitrary"),
    )(x, g, target)


def _rope_tables(positions):
    inv = 1.0 / (ROPE_THETA ** (jnp.arange(0, QK_ROPE, 2, dtype=F32) / QK_ROPE))
    ang = positions.astype(F32)[:, None] * inv
    c, s = jnp.cos(ang), jnp.sin(ang)
    T = positions.shape[0]
    cs = jnp.concatenate([jnp.ones((T, QK_NOPE), F32), c, c, jnp.zeros((T, LANES - QK_NOPE - QK_ROPE), F32)], 1)
    sn = jnp.concatenate([jnp.zeros((T, QK_NOPE), F32), s, s, jnp.zeros((T, LANES - QK_NOPE - QK_ROPE), F32)], 1)
    return cs, sn


def _pad_rows(w, rows):
    return jnp.concatenate([w, jnp.zeros((rows - w.shape[0],) + w.shape[1:], w.dtype)], 0)


def _mla_weights(w_dq_dkv, w_uq, w_ukv):
    K = w_dq_dkv.shape[0]
    z = lambda n: jnp.zeros((K, n), w_dq_dkv.dtype)
    wc = jnp.concatenate([w_dq_dkv[:, :Q_LORA + KV_LORA], z(QK_NOPE), w_dq_dkv[:, Q_LORA + KV_LORA:],
                          z(LANES - QK_NOPE - QK_ROPE)], 1)
    wq = w_uq.reshape(Q_LORA, MLA_HEADS, QK_NOPE + QK_ROPE).transpose(1, 0, 2)
    wq = jnp.concatenate([wq, jnp.zeros((MLA_HEADS, Q_LORA, LANES - QK_NOPE - QK_ROPE), wq.dtype)], 2)
    wkv = w_ukv.reshape(KV_LORA, MLA_HEADS, QK_NOPE + V_HEAD).transpose(1, 0, 2)
    zero = jnp.zeros_like(wkv[:, :, :QK_NOPE])
    wk = jnp.concatenate([wkv[:, :, :QK_NOPE], zero], 2)
    wv_lo = jnp.concatenate([wkv[:, :, QK_NOPE:], zero], 2)
    wv_hi = jnp.concatenate([zero, wkv[:, :, QK_NOPE:]], 2)
    odd = (jnp.arange(MLA_HEADS) % 2 == 1)[:, None, None]
    wv = jnp.where(odd, wv_hi, wv_lo)
    return wc, wq, wk, wv


def _mla_weight_grads(dwc, dwq, dwk, dwv):
    d_dq = jnp.concatenate([dwc[:, :Q_LORA + KV_LORA],
                            dwc[:, Q_LORA + KV_LORA + QK_NOPE:Q_LORA + KV_LORA + QK_NOPE + QK_ROPE]], 1)
    d_uq = dwq[:, :, :QK_NOPE + QK_ROPE].transpose(1, 0, 2).reshape(Q_LORA, MLA_HEADS * (QK_NOPE + QK_ROPE))
    odd = (jnp.arange(MLA_HEADS) % 2 == 1)[:, None, None]
    dv = jnp.where(odd, dwv[:, :, V_HEAD:], dwv[:, :, :V_HEAD])
    d_ukv = jnp.concatenate([dwk[:, :, :QK_NOPE], dv], 2).transpose(1, 0, 2).reshape(
        KV_LORA, MLA_HEADS * (QK_NOPE + V_HEAD))
    return d_dq, d_uq, d_ukv


def _local_step(x, mem, positions, target, W):
    G = {}
    row = lambda v: v.reshape(1, -1)
    cs, sn = _rope_tables(positions)
    saved = []
    for l in range(DEPTH):
        L = f"l{l}"
        s = {"x0": x}
        if l % 2 == 0:
            e = l // 2
            s["z"], s["h"] = _nmm(x, row(W["norm_mix_g"][l]), (W["pc_w_in"], e), name=f"{L}_mix_in", out_dtype=F32)
            s["dw_w"] = _pad_rows(W["conv_dw_w"][e], CONV_K + 1)
            s["mix_p"] = (W["pool_w"][e], row(W["pool_scale"][e]), s["dw_w"], row(W["conv_dw_b"][e]),
                          row(W["conv_ln_g"][e]), row(W["conv_ln_b"][e]))
            s["ycat"] = _mixer_fwd(s["z"], *s["mix_p"], name=f"{L}_mix_mid")
            x = _mm_res(s["ycat"], (W["pc_w_out"], e), x, name=f"{L}_mix_out")
        else:
            o = l // 2
            wc, wq, wk, wv = _mla_weights(W["mla_w_dq_dkv"][o], W["mla_w_uq"][o], W["mla_w_ukv"][o])
            s["mla_w"] = (wc, wq, wk, wv)
            s["c"], s["h"] = _nmm(x, row(W["norm_mix_g"][l]), wc, name=f"{L}_mla_down", out_dtype=F32)
            s["qg"], s["kvg"] = row(W["mla_q_norm_g"][o]), row(W["mla_kv_norm_g"][o])
            s["qn"], s["kvn"], kpe = _mla_mid_fwd(s["c"], s["qg"], s["kvg"], cs, sn, name=f"{L}_mla_mid")
            s["q"], s["k"], s["v"] = _mla_qkv_fwd(s["qn"], s["kvn"], kpe, cs, sn, wq, wk, wv, name=f"{L}_mla_qkv")
            s["o"], s["lse"] = _flash_fwd(s["q"], s["k"], s["v"], name=f"{L}_mla_attn")
            x = _mm_res(s["o"], (W["mla_w_o"], o), x, name=f"{L}_mla_out")
        s["x1"] = x
        s["xq"], s["hx"] = _nmm(x, row(W["norm_xa_g"][l]), (W["xa_wq"], l), name=f"{L}_xa_q", out_dtype=CD)
        s["xkv"], s["hm"] = _nmm(mem, row(W["norm_mem_g"][l]), (W["xa_wkv"], l), name=f"{L}_xa_kv", out_dtype=CD)
        s["xo"] = _xattn_fwd(s["xq"], s["xkv"], name=f"{L}_xa_attn")
        x = _mm_res(s["xo"], (W["xa_wo"], l), x, name=f"{L}_xa_out")
        s["x2"] = x
        s["up"], s["hf"] = _nmm(x, row(W["norm_ffn_g"][l]), (W["ffn_w_up"], l), name=f"{L}_ffn_up", out_dtype=CD,
                                tn_target=1408)
        s["cw"], s["cb"] = _pad_rows(W["ffn_conv_w"][l], 8), row(W["ffn_conv_b"][l])
        s["act"] = _ffn_mid_fwd(s["up"], s["cw"], s["cb"], name=f"{L}_ffn_mid")
        x = _mm_res(s["act"], (W["ffn_w_down"], l), x, name=f"{L}_ffn_down")
        saved.append(s)
    dx, G["final_norm_g"], loss = _loss_head(x, row(W["final_norm_g"]), target, name="loss_head")
    G["final_norm_g"] = G["final_norm_g"].reshape(-1)

    per_layer = {}

    def put(name, l, val):
        per_layer.setdefault(name, {})[l] = val

    for l in reversed(range(DEPTH)):
        L = f"l{l}"
        s = saved[l]
        put("ffn_w_down", l, _mm_tn(s["act"], dx, name=f"{L}_ffn_down_dw", tk_target=1408))
        dact = _mm_nt(dx, (W["ffn_w_down"], l), name=f"{L}_ffn_down_dx", out_dtype=CD, tn_target=1408)
        dup, dcw, dcb = _ffn_mid_bwd(s["up"], dact, s["cw"], s["cb"], name=f"{L}_ffn_mid_bwd")
        put("ffn_conv_w", l, dcw[:FFN_K])
        put("ffn_conv_b", l, dcb[0])
        put("ffn_w_up", l, _mm_tn(s["hf"], dup, name=f"{L}_ffn_up_dw", tn_target=1408))
        dx, dg = _mm_nt_normbwd(dup, (W["ffn_w_up"], l), s["x2"], row(W["norm_ffn_g"][l]), dx, name=f"{L}_ffn_up_dx")
        put("norm_ffn_g", l, dg[0])
        put("xa_wo", l, _mm_tn(s["xo"], dx, name=f"{L}_xa_out_dw"))
        do = _mm_nt(dx, (W["xa_wo"], l), name=f"{L}_xa_out_dx", out_dtype=CD)
        dq, dkv = _xattn_bwd(s["xq"], s["xkv"], do, name=f"{L}_xa_attn_bwd")
        put("xa_wq", l, _mm_tn(s["hx"], dq, name=f"{L}_xa_q_dw"))
        dx, dg = _mm_nt_normbwd(dq, (W["xa_wq"], l), s["x1"], row(W["norm_xa_g"][l]), dx, name=f"{L}_xa_q_dx")
        put("norm_xa_g", l, dg[0])
        put("xa_wkv", l, _mm_tn(s["hm"], dkv, name=f"{L}_xa_kv_dw", tt=MEM_LEN))
        _, dg = _mm_nt_normbwd(dkv, (W["xa_wkv"], l), mem, row(W["norm_mem_g"][l]), jnp.zeros_like(mem),
                               name=f"{L}_xa_kv_dx", tm=MEM_LEN)
        put("norm_mem_g", l, dg[0])
        if l % 2 == 0:
            e = l // 2
            put("pc_w_out", e, _mm_tn(s["ycat"], dx, name=f"{L}_mix_out_dw"))
            dy = _mm_nt(dx, (W["pc_w_out"], e), name=f"{L}_mix_out_dx", out_dtype=F32)
            dz, dpw, dps, ddw, ddb, dlg, dlb = _mixer_bwd(s["z"], dy, *s["mix_p"], name=f"{L}_mix_mid_bwd")
            put("pool_w", e, dpw)
            put("pool_scale", e, dps[0])
            put("conv_dw_w", e, ddw[:CONV_K])
            put("conv_dw_b", e, ddb[0])
            put("conv_ln_g", e, dlg[0])
            put("conv_ln_b", e, dlb[0])
            put("pc_w_in", e, _mm_tn(s["h"], dz, name=f"{L}_mix_in_dw"))
            dx, dg = _mm_nt_normbwd(dz, (W["pc_w_in"], e), s["x0"], row(W["norm_mix_g"][l]), dx, name=f"{L}_mix_in_dx")
        else:
            o = l // 2
            wc, wq, wk, wv = s["mla_w"]
            put("mla_w_o", o, _mm_tn(s["o"], dx, name=f"{L}_mla_out_dw"))
            do = _mm_nt(dx, (W["mla_w_o"], o), name=f"{L}_mla_out_dx", out_dtype=CD)
            delta = _flash_delta(s["o"], do, name=f"{L}_mla_attn_delta")
            dq, dk, dv = _flash_bwd(s["q"], s["k"], s["v"], do, s["lse"], delta, name=f"{L}_mla_attn_bwd")
            dqn, dkvn, dks, dwq, dwk, dwv = _mla_qkv_bwd(dq, dk, dv, s["qn"], s["kvn"], cs, sn, wq, wk, wv,
                                                         name=f"{L}_mla_qkv_bwd")
            dc, dqg, dkg = _mla_mid_bwd(s["c"], dqn, dkvn, dks, s["qg"], s["kvg"], cs, sn, name=f"{L}_mla_mid_bwd")
            put("mla_q_norm_g", o, dqg[0])
            put("mla_kv_norm_g", o, dkg[0])
            dwc = _mm_tn(s["h"], dc, name=f"{L}_mla_down_dw")
            d_dq, d_uq, d_ukv = _mla_weight_grads(dwc, dwq, dwk, dwv)
            put("mla_w_dq_dkv", o, d_dq)
            put("mla_w_uq", o, d_uq)
            put("mla_w_ukv", o, d_ukv)
            dx, dg = _mm_nt_normbwd(dc, wc, s["x0"], row(W["norm_mix_g"][l]), dx, name=f"{L}_mla_down_dx",
                                    tk_target=768)
        put("norm_mix_g", l, dg[0])
    for name, d in per_layer.items():
        G[name] = jnp.stack([d[i] for i in sorted(d)], 0)
    return loss, dx, G


_ANY = pl.BlockSpec(memory_space=pl.ANY)


def _all_gather(xs, *, name):
    n = len(xs)

    def body(*refs):
        x_refs, out_refs = refs[:n], refs[n:2 * n]
        send_sems, recv_sems, local_sems = refs[2 * n:]
        mx, my, mc = lax.axis_index("x"), lax.axis_index("y"), lax.axis_index("c")
        me, sibling = (mx, my, mc), (mx, my, 1 - mc)
        chips = [(1 - mx, my), (mx, 1 - my), (1 - mx, 1 - my)]

        def copy(a, k, block, to, own=False):
            px, py, pc = block
            dst = out_refs[a].at[4 * px + 2 * py + pc]
            return pltpu.make_async_remote_copy(
                src_ref=x_refs[a] if own else dst, dst_ref=dst,
                send_sem=send_sems.at[7 * a + k], recv_sem=recv_sems.at[7 * a + k],
                device_id=to, device_id_type=MESH)

        mine = [pltpu.make_async_copy(x_refs[a], out_refs[a].at[4 * mx + 2 * my + mc], local_sems.at[a])
                for a in range(n)]
        for cp in mine:
            cp.start()
        first = []
        for j, chip in enumerate(chips):
            first += [copy(a, 1 + j, me, (*chip, mc), own=True) for a in range(n)]
        first += [copy(a, 0, me, sibling, own=True) for a in range(n)]
        for cp in first:
            cp.start()
        passed = []
        for j, chip in enumerate(chips):
            for a in range(n):
                copy(a, 1 + j, (*chip, mc), me).wait_recv()
                passed.append(copy(a, 4 + j, (*chip, mc), sibling))
                passed[-1].start()
        for a in range(n):
            copy(a, 0, sibling, me).wait_recv()
        for j, chip in enumerate(chips):
            for a in range(n):
                copy(a, 4 + j, (*chip, 1 - mc), me).wait_recv()
        for cp in first + passed:
            cp.wait_send()
        for cp in mine:
            cp.wait()

    return pl.pallas_call(
        body, name=name, in_specs=[_ANY] * n, out_specs=[_ANY] * n,
        out_shape=[jax.ShapeDtypeStruct((N_DEV,) + x.shape, x.dtype) for x in xs],
        scratch_shapes=[pltpu.SemaphoreType.DMA((7 * n,)), pltpu.SemaphoreType.DMA((7 * n,)),
                        pltpu.SemaphoreType.DMA((n,))],
    )(*xs)


N_CHIP = 4


def _pair_exchange(ps, *, name):
    n = len(ps)

    def body(*refs):
        p_refs, out_refs = refs[:n], refs[n:2 * n]
        send_sems, recv_sems = refs[2 * n:]
        mx, my, mc = lax.axis_index("x"), lax.axis_index("y"), lax.axis_index("c")
        copies = []
        for a in range(n):
            for chip in range(N_CHIP):
                copies.append(pltpu.make_async_remote_copy(
                    src_ref=p_refs[a].at[2 * chip + (1 - mc)], dst_ref=out_refs[a].at[chip],
                    send_sem=send_sems.at[N_CHIP * a + chip], recv_sem=recv_sems.at[N_CHIP * a + chip],
                    device_id=(mx, my, 1 - mc), device_id_type=MESH))
        for cp in copies:
            cp.start()
        for cp in copies:
            cp.wait()

    return pl.pallas_call(
        body, name=name, in_specs=[_ANY] * n, out_specs=[_ANY] * n,
        out_shape=[jax.ShapeDtypeStruct((N_CHIP,) + p.shape[1:], p.dtype) for p in ps],
        scratch_shapes=[pltpu.SemaphoreType.DMA((N_CHIP * n,)), pltpu.SemaphoreType.DMA((N_CHIP * n,))],
    )(*ps)


def _pair_sum(p, recv, core, *, name):
    _, R, C = p.shape
    tr = _row_tile(R, C)
    p4 = p.reshape(N_CHIP, 2, R, C)

    def body(core_ref, a_ref, b_ref, o_ref):
        o_ref[...] = (a_ref[...].astype(F32) + b_ref[...].astype(F32)).astype(o_ref.dtype)

    return pl.pallas_call(
        body, name=name,
        grid_spec=pltpu.PrefetchScalarGridSpec(
            num_scalar_prefetch=1, grid=(N_CHIP, R // tr),
            in_specs=[pl.BlockSpec((None, None, tr, C), lambda ch, i, core: (ch, core[0], i, 0)),
                      pl.BlockSpec((None, tr, C), lambda ch, i, core: (ch, i, 0))],
            out_specs=pl.BlockSpec((None, tr, C), lambda ch, i, core: (ch, i, 0))),
        out_shape=jax.ShapeDtypeStruct((N_CHIP, R, C), p.dtype),
        compiler_params=_params("parallel", "parallel"),
    )(core, p4, recv)


def _chip_exchange(ss, *, name):
    n = len(ss)

    def body(*refs):
        s_refs, out_refs = refs[:n], refs[n:2 * n]
        send_sems, recv_sems, local_sems = refs[2 * n:]
        mx, my, mc = lax.axis_index("x"), lax.axis_index("y"), lax.axis_index("c")
        chip = 2 * mx + my
        mine = [pltpu.make_async_copy(s_refs[a].at[chip], out_refs[a].at[chip], local_sems.at[a]) for a in range(n)]
        for cp in mine:
            cp.start()
        copies = []
        for k in range(1, N_CHIP):
            px, py = mx ^ ((k >> 1) & 1), my ^ (k & 1)
            for a in range(n):
                copies.append(pltpu.make_async_remote_copy(
                    src_ref=s_refs[a].at[2 * px + py], dst_ref=out_refs[a].at[chip],
                    send_sem=send_sems.at[3 * a + k - 1], recv_sem=recv_sems.at[3 * a + k - 1],
                    device_id=(px, py, mc), device_id_type=MESH))
        for cp in copies:
            cp.start()
        for cp in copies:
            cp.wait()
        for cp in mine:
            cp.wait()

    return pl.pallas_call(
        body, name=name, in_specs=[_ANY] * n, out_specs=[_ANY] * n,
        out_shape=[jax.ShapeDtypeStruct(s.shape, s.dtype) for s in ss],
        scratch_shapes=[pltpu.SemaphoreType.DMA((3 * n,)), pltpu.SemaphoreType.DMA((3 * n,)),
                        pltpu.SemaphoreType.DMA((n,))],
    )(*ss)


ROW_TILE_ELEMS = 256 * 1024


def _row_tile(R, C):
    for t in (2048, 1024, 512, 256, 128, 64, 32, 16):
        if R % t == 0 and t * C <= ROW_TILE_ELEMS:
            return t
    raise ValueError((R, C))


def _sum_slots(gs, *, name):
    S, R, C = gs.shape
    tr = _row_tile(R, C)

    def body(g_ref, o_ref):
        g = g_ref[0].astype(F32)
        for s in range(1, S):
            g = g + g_ref[s].astype(F32)
        o_ref[...] = g

    return pl.pallas_call(
        body, name=name, grid=(R // tr,),
        in_specs=[pl.BlockSpec((S, tr, C), lambda i: (0, i, 0))],
        out_specs=pl.BlockSpec((tr, C), lambda i: (i, 0)),
        out_shape=jax.ShapeDtypeStruct((R, C), F32),
        compiler_params=_params("parallel"),
    )(gs)


def _adamw(gs, w, m, v, *, name):
    S, R, C = gs.shape
    tr = _row_tile(R, C)

    def body(g_ref, w_ref, m_ref, v_ref, g_out, d_out, m_out, v_out):
        g = g_ref[0].astype(F32)
        for s in range(1, S):
            g = g + g_ref[s].astype(F32)
        m_new = ADAM_B1 * m_ref[...] + (1.0 - ADAM_B1) * g
        v_new = ADAM_B2 * v_ref[...] + (1.0 - ADAM_B2) * (g * g)
        m_hat = m_new / (1.0 - ADAM_B1 ** ADAM_STEP)
        v_hat = v_new / (1.0 - ADAM_B2 ** ADAM_STEP)
        g_out[...] = g
        d_out[...] = -ADAM_LR * (m_hat / (jnp.sqrt(v_hat) + ADAM_EPS) + ADAM_WD * w_ref[...])
        m_out[...] = m_new
        v_out[...] = v_new

    blk = pl.BlockSpec((tr, C), lambda i: (i, 0))
    sh = jax.ShapeDtypeStruct((R, C), F32)
    return pl.pallas_call(
        body, name=name, grid=(R // tr,),
        in_specs=[pl.BlockSpec((S, tr, C), lambda i: (0, i, 0)), blk, blk, blk],
        out_specs=[blk, blk, blk, blk], out_shape=[sh, sh, sh, sh],
        compiler_params=_params("parallel"),
    )(gs, w, m, v)


PIECE = 16 * LANES


def _pack(arrs, dtype, lead, row_mult):
    lead_shape = arrs[0].shape[:lead]
    parts, meta, off = [], [], 0
    for a in arrs:
        size = math.prod(a.shape[lead:])
        padded = -(-size // PIECE) * PIECE
        flat = a.astype(dtype).reshape(lead_shape + (size,))
        if padded != size:
            flat = jnp.concatenate([flat, jnp.zeros(lead_shape + (padded - size,), dtype)], -1)
        parts.append(flat)
        meta.append((off, size, a.shape[lead:]))
        off += padded
    total = -(-off // (row_mult * LANES)) * (row_mult * LANES)
    if total != off:
        parts.append(jnp.zeros(lead_shape + (total - off,), dtype))
    return jnp.concatenate(parts, -1).reshape(lead_shape + (total // LANES, LANES)), meta


def _unpack(packed, meta, lead):
    lead_shape = packed.shape[:lead]
    flat = packed.reshape(lead_shape + (-1,))
    return [flat[..., off:off + size].reshape(lead_shape + shape) for off, size, shape in meta]


ARG_NAMES = ['x', 'mem', 'positions', 'norm_mix_g', 'norm_xa_g', 'norm_mem_g', 'xa_wq', 'xa_wkv', 'xa_wo', 'norm_ffn_g', 'ffn_w_up', 'ffn_conv_w', 'ffn_conv_b', 'ffn_w_down', 'pc_w_in', 'pool_w', 'pool_scale', 'conv_dw_w', 'conv_dw_b', 'conv_ln_g', 'conv_ln_b', 'pc_w_out', 'mla_w_dq_dkv', 'mla_q_norm_g', 'mla_w_uq', 'mla_kv_norm_g', 'mla_w_ukv', 'mla_w_o', 'final_norm_g', 'loss_target']
WEIGHTS = ARG_NAMES[3:29]
BIG = {'xa_wq': 1, 'xa_wkv': 2, 'xa_wo': 1, 'ffn_w_up': 2, 'ffn_w_down': 1, 'pc_w_in': 2, 'pc_w_out': 1,
       'mla_w_dq_dkv': 1, 'mla_w_uq': 2, 'mla_w_ukv': 2, 'mla_w_o': 1}
SMALL_SHARDED = {'ffn_conv_w': 2, 'conv_dw_w': 2, 'mla_q_norm_g': 1, 'mla_kv_norm_g': 1}
REPLICATED = [n for n in WEIGHTS if n not in BIG and n not in SMALL_SHARDED]


def _from_slots(g, axis):
    t = jnp.moveaxis(g, 0, axis)
    return t.reshape(t.shape[:axis] + (t.shape[axis] * t.shape[axis + 1],) + t.shape[axis + 2:])


def _to_slots(full, axis):
    n = full.shape[axis] // N_DEV
    t = full.reshape(full.shape[:axis] + (N_DEV, n) + full.shape[axis + 1:])
    return jnp.moveaxis(t, axis, 0)


def kernel(x, mem, positions, norm_mix_g, norm_xa_g, norm_mem_g, xa_wq, xa_wkv, xa_wo, norm_ffn_g, ffn_w_up, ffn_conv_w, ffn_conv_b, ffn_w_down, pc_w_in, pool_w, pool_scale, conv_dw_w, conv_dw_b, conv_ln_g, conv_ln_b, pc_w_out, mla_w_dq_dkv, mla_q_norm_g, mla_w_uq, mla_kv_norm_g, mla_w_ukv, mla_w_o, final_norm_g, loss_target, m_norm_mix_g, m_norm_xa_g, m_norm_mem_g, m_xa_wq, m_xa_wkv, m_xa_wo, m_norm_ffn_g, m_ffn_w_up, m_ffn_conv_w, m_ffn_conv_b, m_ffn_w_down, m_pc_w_in, m_pool_w, m_pool_scale, m_conv_dw_w, m_conv_dw_b, m_conv_ln_g, m_conv_ln_b, m_pc_w_out, m_mla_w_dq_dkv, m_mla_q_norm_g, m_mla_w_uq, m_mla_kv_norm_g, m_mla_w_ukv, m_mla_w_o, m_final_norm_g, v_norm_mix_g, v_norm_xa_g, v_norm_mem_g, v_xa_wq, v_xa_wkv, v_xa_wo, v_norm_ffn_g, v_ffn_w_up, v_ffn_conv_w, v_ffn_conv_b, v_ffn_w_down, v_pc_w_in, v_pool_w, v_pool_scale, v_conv_dw_w, v_conv_dw_b, v_conv_ln_g, v_conv_ln_b, v_pc_w_out, v_mla_w_dq_dkv, v_mla_q_norm_g, v_mla_w_uq, v_mla_kv_norm_g, v_mla_w_ukv, v_mla_w_o, v_final_norm_g):
    args = (x, mem, positions, norm_mix_g, norm_xa_g, norm_mem_g, xa_wq, xa_wkv, xa_wo, norm_ffn_g, ffn_w_up, ffn_conv_w, ffn_conv_b, ffn_w_down, pc_w_in, pool_w, pool_scale, conv_dw_w, conv_dw_b, conv_ln_g, conv_ln_b, pc_w_out, mla_w_dq_dkv, mla_q_norm_g, mla_w_uq, mla_kv_norm_g, mla_w_ukv, mla_w_o, final_norm_g, loss_target)
    a = dict(zip(ARG_NAMES, args))
    mom = dict(zip(WEIGHTS, (m_norm_mix_g, m_norm_xa_g, m_norm_mem_g, m_xa_wq, m_xa_wkv, m_xa_wo, m_norm_ffn_g, m_ffn_w_up, m_ffn_conv_w, m_ffn_conv_b, m_ffn_w_down, m_pc_w_in, m_pool_w, m_pool_scale, m_conv_dw_w, m_conv_dw_b, m_conv_ln_g, m_conv_ln_b, m_pc_w_out, m_mla_w_dq_dkv, m_mla_q_norm_g, m_mla_w_uq, m_mla_kv_norm_g, m_mla_w_ukv, m_mla_w_o, m_final_norm_g)))
    var = dict(zip(WEIGHTS, (v_norm_mix_g, v_norm_xa_g, v_norm_mem_g, v_xa_wq, v_xa_wkv, v_xa_wo, v_norm_ffn_g, v_ffn_w_up, v_ffn_conv_w, v_ffn_conv_b, v_ffn_w_down, v_pc_w_in, v_pool_w, v_pool_scale, v_conv_dw_w, v_conv_dw_b, v_conv_ln_g, v_conv_ln_b, v_pc_w_out, v_mla_w_dq_dkv, v_mla_q_norm_g, v_mla_w_uq, v_mla_kv_norm_g, v_mla_w_ukv, v_mla_w_o, v_final_norm_g)))
    me = 4 * lax.axis_index("x") + 2 * lax.axis_index("y") + lax.axis_index("c")

    big_all = _all_gather([a[n].astype(CD) for n in BIG], name="gather_weights")
    sm_pack, sm_meta = _pack([a[n] for n in SMALL_SHARDED], F32, 0, 8)
    sm_all = _unpack(_all_gather([sm_pack], name="gather_small")[0], sm_meta, 1)
    W = {n: a[n] for n in REPLICATED}
    for (n, ax), g in zip(BIG.items(), big_all):
        W[n] = _from_slots(g, ax)
    for (n, ax), g in zip(SMALL_SHARDED.items(), sm_all):
        W[n] = _from_slots(g, ax)

    loss, dx, G = _local_step(x[0], mem[0], positions[0], loss_target[0], W)

    parts = [_to_slots(G[n], ax).astype(CD) for n, ax in BIG.items()]
    from_sibling = _pair_exchange(parts, name="grads_to_sibling")
    core = lax.axis_index("c").astype(jnp.int32).reshape(1)
    sums = []
    for n, p, r in zip(BIG, parts, from_sibling):
        cols = p.shape[-1]
        s = _pair_sum(p.reshape(N_DEV, -1, cols), r.reshape(N_CHIP, -1, cols), core, name=f"pair_sum_{n}")
        sums.append(s.reshape((N_CHIP,) + p.shape[1:]))
    recv = _chip_exchange(sums, name="scatter_grads")
    out = {}
    for n, r in zip(BIG, recv):
        shape = a[n].shape
        rows = lambda t: t.reshape(-1, shape[-1])
        res = _adamw(r.reshape(N_CHIP, -1, shape[-1]), rows(a[n]), rows(mom[n]), rows(var[n]), name=f"adamw_{n}")
        out[n] = tuple(t.reshape(shape) for t in res)

    small_names = REPLICATED + list(SMALL_SHARDED)
    spack, smeta = _pack([G[n] for n in small_names] + [loss], F32, 0, 256)
    stot = _unpack(_sum_slots(_all_gather([spack], name="gather_small_grads")[0], name="sum_small_grads"), smeta, 0)
    loss_total = stot[-1][0, 0]
    gsm = dict(zip(small_names, stot[:-1]))
    for n, ax in SMALL_SHARDED.items():
        width = a[n].shape[ax]
        gsm[n] = lax.dynamic_slice_in_dim(gsm[n], me * width, width, ax)
    g1, meta1 = _pack([gsm[n] for n in small_names], F32, 0, 256)
    w1, _ = _pack([a[n] for n in small_names], F32, 0, 256)
    m1, _ = _pack([mom[n] for n in small_names], F32, 0, 256)
    v1, _ = _pack([var[n] for n in small_names], F32, 0, 256)
    res = [_unpack(r, meta1, 0) for r in _adamw(g1[None], w1, m1, v1, name="adamw_small")]
    for i, n in enumerate(small_names):
        out[n] = tuple(r[i] for r in res)

    return (loss_total, dx[None],
            *[out[n][0] for n in WEIGHTS], *[out[n][1] for n in WEIGHTS],
            *[out[n][2] for n in WEIGHTS], *[out[n][3] for n in WEIGHTS])
```

```python
import functools
import math

import jax
import jax.numpy as jnp
from jax import lax
from jax.experimental import pallas as pl
from jax.experimental.pallas import tpu as pltpu

F32 = jnp.float32
CD = jnp.bfloat16
EPS = 1e-6
NEG = -1e30
N_DEV = 8
LANES = 128
HALO = 32

D_MODEL = 1024
DEPTH = 4
XA_HEADS = 4
XA_DH = 256
MEM_LEN = 256
POOL_WINDOWS = (2, 4, 8, 16)
CONV_K = 31
FFN_K = 3
D_FF = 2816
MLA_HEADS = 16
QK_NOPE = 64
QK_ROPE = 32
V_HEAD = 64
Q_LORA = 384
KV_LORA = 256
ROPE_THETA = 10000.0
MLA_SCALE = 1.0 / math.sqrt(QK_NOPE + QK_ROPE)
XA_SCALE = XA_DH ** -0.5

ADAM_LR = 0.001
ADAM_B1 = 0.9
ADAM_B2 = 0.999
ADAM_EPS = 1e-08
ADAM_WD = 0.01
ADAM_STEP = 10

NT = (((1,), (1,)), ((), ()))
TN = (((0,), (0,)), ((), ()))
MESH = pl.DeviceIdType.MESH


def _tile(n, target):
    if n <= target:
        return n
    best = None
    for t in range(LANES, target + 1, LANES):
        if n % t == 0:
            best = t
    assert best is not None, (n, target)
    return best


def _params(*sem):
    return pltpu.CompilerParams(dimension_semantics=sem)


def _sigmoid(v):
    return 0.5 * jnp.tanh(0.5 * v) + 0.5


def _rms_bwd(x, gain, dh):
    r = lax.rsqrt(jnp.mean(x * x, axis=-1, keepdims=True) + EPS)
    xhat = x * r
    dxhat = dh * gain
    dx = r * (dxhat - xhat * jnp.mean(dxhat * xhat, axis=-1, keepdims=True))
    return dx, dh * xhat


def _weight(w):
    if not isinstance(w, tuple):
        return w, w.shape, pl.BlockSpec
    arr, layer = w

    def spec(block, imap):
        return pl.BlockSpec((None,) + tuple(block), lambda *a: (layer,) + tuple(imap(*a)))

    return arr, arr.shape[1:], spec


def _nmm(x, g, w, *, name, out_dtype, tm=1024, tn_target=1024):
    M, K = x.shape
    w, (_, N), wspec = _weight(w)
    tm = min(tm, M)
    tn = _tile(N, tn_target)

    def body(x_ref, g_ref, w_ref, z_ref, h_ref):
        @pl.when(pl.program_id(1) == 0)
        def _():
            xf = x_ref[...]
            r = lax.rsqrt(jnp.mean(xf * xf, axis=-1, keepdims=True) + EPS)
            h_ref[...] = (xf * r * g_ref[...]).astype(h_ref.dtype)

        z_ref[...] = jnp.dot(h_ref[...], w_ref[...], preferred_element_type=F32).astype(z_ref.dtype)

    return pl.pallas_call(
        body, name=name, grid=(M // tm, N // tn),
        in_specs=[pl.BlockSpec((tm, K), lambda i, j: (i, 0)),
                  pl.BlockSpec((1, K), lambda i, j: (0, 0)),
                  wspec((K, tn), lambda i, j: (0, j))],
        out_specs=[pl.BlockSpec((tm, tn), lambda i, j: (i, j)),
                   pl.BlockSpec((tm, K), lambda i, j: (i, 0))],
        out_shape=[jax.ShapeDtypeStruct((M, N), out_dtype), jax.ShapeDtypeStruct((M, K), CD)],
        compiler_params=_params("parallel", "arbitrary"),
    )(x, g, w)


def _mm_res(a, w, res, *, name, tm=1024, tn_target=1024):
    M, K = a.shape
    w, (_, N), wspec = _weight(w)
    tm = min(tm, M)
    tn = _tile(N, tn_target)

    def body(a_ref, w_ref, r_ref, o_ref):
        o_ref[...] = r_ref[...] + jnp.dot(a_ref[...].astype(CD), w_ref[...], preferred_element_type=F32)

    return pl.pallas_call(
        body, name=name, grid=(M // tm, N // tn),
        in_specs=[pl.BlockSpec((tm, K), lambda i, j: (i, 0)),
                  wspec((K, tn), lambda i, j: (0, j)),
                  pl.BlockSpec((tm, tn), lambda i, j: (i, j))],
        out_specs=pl.BlockSpec((tm, tn), lambda i, j: (i, j)),
        out_shape=jax.ShapeDtypeStruct((M, N), F32),
        compiler_params=_params("parallel", "arbitrary"),
    )(a, w, res)


def _mm_nt(a, w, *, name, out_dtype, tm=1024, tn_target=1024):
    M, K = a.shape
    w, (N, _), wspec = _weight(w)
    tm = min(tm, M)
    tn = _tile(N, tn_target)

    def body(a_ref, w_ref, o_ref):
        o_ref[...] = lax.dot_general(a_ref[...].astype(CD), w_ref[...], NT,
                                     preferred_element_type=F32).astype(o_ref.dtype)

    return pl.pallas_call(
        body, name=name, grid=(M // tm, N // tn),
        in_specs=[pl.BlockSpec((tm, K), lambda i, j: (i, 0)),
                  wspec((tn, K), lambda i, j: (j, 0))],
        out_specs=pl.BlockSpec((tm, tn), lambda i, j: (i, j)),
        out_shape=jax.ShapeDtypeStruct((M, N), out_dtype),
        compiler_params=_params("parallel", "arbitrary"),
    )(a, w)


def _mm_nt_normbwd(gy, w, x, gain, dres, *, name, tm=1024, tk_target=1408):
    M, K = gy.shape
    w, (D, _), wspec = _weight(w)
    tm = min(tm, M)
    tk = _tile(K, tk_target)
    nk = K // tk

    def body(g_ref, w_ref, x_ref, gain_ref, dres_ref, dx_ref, dg_ref, acc):
        i, k = pl.program_id(0), pl.program_id(1)

        @pl.when(k == 0)
        def _():
            acc[...] = jnp.zeros_like(acc)

        acc[...] += lax.dot_general(g_ref[...].astype(CD), w_ref[...], NT, preferred_element_type=F32)

        @pl.when(k == nk - 1)
        def _():
            dx, dg_rows = _rms_bwd(x_ref[...], gain_ref[...], acc[...])
            dx_ref[...] = dres_ref[...] + dx

            @pl.when(i == 0)
            def _():
                dg_ref[...] = jnp.zeros_like(dg_ref)

            dg_ref[...] += jnp.sum(dg_rows, axis=0, keepdims=True)

    return pl.pallas_call(
        body, name=name, grid=(M // tm, nk),
        in_specs=[pl.BlockSpec((tm, tk), lambda i, k: (i, k)),
                  wspec((D, tk), lambda i, k: (0, k)),
                  pl.BlockSpec((tm, D), lambda i, k: (i, 0)),
                  pl.BlockSpec((1, D), lambda i, k: (0, 0)),
                  pl.BlockSpec((tm, D), lambda i, k: (i, 0))],
        out_specs=[pl.BlockSpec((tm, D), lambda i, k: (i, 0)),
                   pl.BlockSpec((1, D), lambda i, k: (0, 0))],
        out_shape=[jax.ShapeDtypeStruct((M, D), F32), jax.ShapeDtypeStruct((1, D), F32)],
        scratch_shapes=[pltpu.VMEM((tm, D), F32)],
        compiler_params=_params("arbitrary", "arbitrary"),
    )(gy, w, x, gain, dres)


def _mm_tn(a, g, *, name, tt=1024, tk_target=1024, tn_target=1024):
    T, K = a.shape
    N = g.shape[1]
    tt = min(tt, T)
    tk = _tile(K, tk_target)
    tn = _tile(N, tn_target)

    def body(a_ref, g_ref, o_ref):
        @pl.when(pl.program_id(2) == 0)
        def _():
            o_ref[...] = jnp.zeros_like(o_ref)

        o_ref[...] += lax.dot_general(a_ref[...].astype(CD), g_ref[...].astype(CD), TN,
                                      preferred_element_type=F32)

    return pl.pallas_call(
        body, name=name, grid=(K // tk, N // tn, T // tt),
        in_specs=[pl.BlockSpec((tt, tk), lambda i, j, t: (t, i)),
                  pl.BlockSpec((tt, tn), lambda i, j, t: (t, j))],
        out_specs=pl.BlockSpec((tk, tn), lambda i, j, t: (i, j)),
        out_shape=jax.ShapeDtypeStruct((K, N), F32),
        compiler_params=_params("parallel", "parallel", "arbitrary"),
    )(a, g)


POOL_W = 512
CONV_W = 512
POOL_GROUP = 128


MIX_ROWS = 64
LN_ROWS = 32
SUB = 8


def _shifted(sh_sc, x, n_rows):
    for b in range(1, SUB):
        sh_sc[b, pl.ds(0, n_rows), :] = x[b:b + n_rows]


def _tap(sh_sc, src, r0, cols, start, rows):
    a, b = divmod(start, SUB)
    if b == 0:
        return src[pl.ds(r0 + SUB * a, rows), cols]
    return sh_sc[b, pl.ds(SUB * a, rows), :]


def _pool_rows(zp_ref, z_ref, cols, win, i, tt, first, pooled_sc):
    RB = min(MIX_ROWS, tt)
    hb = 2 * SUB
    for r in range(tt // RB):
        if r == 0:
            p = zp_ref[pl.ds(HALO - hb, hb), cols]
            v = jnp.concatenate([jnp.where(first, jnp.zeros_like(p), p), z_ref[pl.ds(0, RB), cols]], axis=0)
        else:
            v = z_ref[pl.ds(r * RB - hb, RB + hb), cols]
        u = v[hb:hb + RB]
        s = u
        for j in range(1, win):
            s = s + v[hb - j:hb - j + RB]
        t_glob = i * tt + r * RB + lax.broadcasted_iota(jnp.int32, (RB, 1), 0)
        cnt = jnp.minimum(t_glob + 1, win).astype(F32)
        pooled_sc[pl.ds(r * RB, RB), :] = (s / cnt - u).astype(pooled_sc.dtype)


def _fill_gl(gl_sc, zp_ref, z_ref, zn_ref, tt, first, last):
    ca, cb = pl.ds(POOL_W, CONV_W), pl.ds(POOL_W + CONV_W, CONV_W)
    g = zp_ref[:, ca] * _sigmoid(zp_ref[:, cb])
    gl_sc[pl.ds(0, HALO), :] = jnp.where(first, jnp.zeros_like(g), g)

    def rows(r, carry):
        r0 = pl.multiple_of(r * LN_ROWS, LN_ROWS)
        gl_sc[pl.ds(HALO + r0, LN_ROWS), :] = z_ref[pl.ds(r0, LN_ROWS), ca] * _sigmoid(z_ref[pl.ds(r0, LN_ROWS), cb])
        return carry

    lax.fori_loop(0, tt // LN_ROWS, rows, 0)
    if zn_ref is not None:
        g = zn_ref[:, ca] * _sigmoid(zn_ref[:, cb])
        gl_sc[pl.ds(HALO + tt, HALO), :] = jnp.where(last, jnp.zeros_like(g), g)


def _conv_rows(gl_sc, cv_sc, sh_sc, w_ref, b_ref, n_rows):
    RB = min(MIX_ROWS, n_rows)
    for c in range(CONV_W // LANES):
        cols = pl.ds(c * LANES, LANES)
        bias = b_ref[:, cols]

        def chunk(r0, rb):
            g = gl_sc[pl.ds(r0, rb + HALO), cols]
            _shifted(sh_sc, g, rb + HALO - SUB)
            cv = jnp.zeros((rb, LANES), F32) + bias
            for j in range(CONV_K):
                cv = cv + w_ref[pl.ds(j, 1), cols] * _tap(sh_sc, gl_sc, r0, cols, HALO - (CONV_K - 1) + j, rb)
            cv_sc[pl.ds(r0, rb), cols] = cv

        def body(r, carry):
            chunk(pl.multiple_of(r * RB, RB), RB)
            return carry

        lax.fori_loop(0, n_rows // RB, body, 0)
        if n_rows % RB:
            chunk((n_rows // RB) * RB, n_rows % RB)


def _mixer_fwd(z, pool_w, pool_scale, dw_w, dw_b, ln_g, ln_b, *, name, tt=512):
    T, C = z.shape
    tt = min(tt, T)
    n = T // tt
    hb = tt // HALO

    def body(zp_ref, z_ref, pw_ref, ps_ref, w_ref, b_ref, g_ref, bb_ref, o_ref, pooled_sc, gl_sc, cv_sc, sh_sc):
        i = pl.program_id(0)
        first = i == 0
        for gi, win in enumerate(POOL_WINDOWS):
            cols = pl.ds(gi * POOL_GROUP, POOL_GROUP)
            _pool_rows(zp_ref, z_ref, cols, win, i, tt, first, pooled_sc)
            ya = jnp.dot(pooled_sc[...], pw_ref[gi].astype(CD), preferred_element_type=F32)
            o_ref[:, cols] = (ya * ps_ref[:, cols]).astype(o_ref.dtype)
        _fill_gl(gl_sc, zp_ref, z_ref, None, tt, first, None)
        _conv_rows(gl_sc, cv_sc, sh_sc, w_ref, b_ref, tt)

        def ln_rows(r, carry):
            rows = pl.ds(pl.multiple_of(r * LN_ROWS, LN_ROWS), LN_ROWS)
            cv = cv_sc[rows, :]
            xc = cv - jnp.mean(cv, axis=-1, keepdims=True)
            yn = xc * lax.rsqrt(jnp.mean(xc * xc, axis=-1, keepdims=True) + EPS) * g_ref[...] + bb_ref[...]
            o_ref[rows, pl.ds(POOL_W, CONV_W)] = (yn * _sigmoid(yn)).astype(o_ref.dtype)
            return carry

        lax.fori_loop(0, tt // LN_ROWS, ln_rows, 0, unroll=4)

    full = lambda shape: pl.BlockSpec(shape, lambda i: (0,) * len(shape))
    return pl.pallas_call(
        body, name=name, grid=(n,),
        in_specs=[pl.BlockSpec((HALO, C), lambda i: (jnp.maximum(i * hb - 1, 0), 0)),
                  pl.BlockSpec((tt, C), lambda i: (i, 0)),
                  full((4, POOL_GROUP, POOL_GROUP)), full((1, POOL_W)), full((CONV_K + 1, CONV_W)),
                  full((1, CONV_W)), full((1, CONV_W)), full((1, CONV_W))],
        out_specs=pl.BlockSpec((tt, POOL_W + CONV_W), lambda i: (i, 0)),
        out_shape=jax.ShapeDtypeStruct((T, POOL_W + CONV_W), CD),
        scratch_shapes=[pltpu.VMEM((tt, POOL_GROUP), CD), pltpu.VMEM((tt + HALO, CONV_W), F32),
                        pltpu.VMEM((tt, CONV_W), F32), pltpu.VMEM((SUB, MIX_ROWS + HALO, LANES), F32)],
        compiler_params=_params("parallel"),
    )(z, z, pool_w, pool_scale, dw_w, dw_b, ln_g, ln_b)


def _mixer_bwd(z, dy, pool_w, pool_scale, dw_w, dw_b, ln_g, ln_b, *, name, tt=512):
    T, C = z.shape
    tt = min(tt, T)
    n = T // tt
    hb = tt // HALO
    R = tt + HALO
    RB = min(MIX_ROWS, tt)

    def body(zp_ref, z_ref, zn_ref, dy_ref, dyn_ref, pw_ref, ps_ref, w_ref, b_ref, g_ref, bb_ref,
             dz_ref, dpw_ref, dps_ref, dw_ref, db_ref, dg_ref, dbb_ref,
             pooled_sc, dm_sc, dpool_sc, dpe_sc, gl_sc, cv_sc, accw, accl, sh_sc, shd_sc):
        i = pl.program_id(0)
        first, last = i == 0, i == n - 1

        @pl.when(first)
        def _():
            for r in (dpw_ref, dps_ref, dw_ref, db_ref, dg_ref, dbb_ref):
                r[...] = jnp.zeros_like(r)

        def dy_rows(cols):
            nxt = dyn_ref[:, cols]
            return jnp.concatenate([dy_ref[:, cols], jnp.where(last, jnp.zeros_like(nxt), nxt)], axis=0)

        t_all = i * tt + lax.broadcasted_iota(jnp.int32, (R, 1), 0)
        for gi, win in enumerate(POOL_WINDOWS):
            cols = pl.ds(gi * POOL_GROUP, POOL_GROUP)
            _pool_rows(zp_ref, z_ref, cols, win, i, tt, first, pooled_sc)
            pw = pw_ref[gi].astype(CD)
            dya = dy_rows(cols)
            mm = jnp.dot(pooled_sc[...], pw, preferred_element_type=F32)
            dps_ref[:, cols] += jnp.sum(dya[:tt] * mm, axis=0, keepdims=True)
            dm_sc[...] = (dya * ps_ref[:, cols]).astype(CD)
            dpw_ref[gi] += lax.dot_general(pooled_sc[...], dm_sc[pl.ds(0, tt), :], TN, preferred_element_type=F32)
            dpool = lax.dot_general(dm_sc[...], pw, NT, preferred_element_type=F32)
            dpool_sc[...] = dpool
            dpe_sc[...] = dpool / jnp.minimum(t_all + 1, win).astype(F32)

            def du_rows(r, carry):
                r0 = pl.multiple_of(r * RB, RB)
                e = dpe_sc[pl.ds(r0, RB + 2 * SUB), :]
                du = -dpool_sc[pl.ds(r0, RB), :]
                for j in range(win):
                    du = du + e[j:j + RB]
                dz_ref[pl.ds(r0, RB), cols] = du.astype(dz_ref.dtype)
                return carry

            lax.fori_loop(0, tt // RB, du_rows, 0)

        _fill_gl(gl_sc, zp_ref, z_ref, zn_ref, tt, first, last)
        _conv_rows(gl_sc, cv_sc, sh_sc, w_ref, b_ref, R)
        accl[...] = jnp.zeros_like(accl)

        def ln_rows(r0, in_tile):
            rows = pl.ds(r0, LN_ROWS)
            cv = cv_sc[rows, :]
            xc = cv - jnp.mean(cv, axis=-1, keepdims=True)
            rstd = lax.rsqrt(jnp.mean(xc * xc, axis=-1, keepdims=True) + EPS)
            xhat = xc * rstd
            yn = xhat * g_ref[...] + bb_ref[...]
            sy = _sigmoid(yn)
            if in_tile:
                dyv = dy_ref[rows, pl.ds(POOL_W, CONV_W)]
            else:
                nxt = dyn_ref[:, pl.ds(POOL_W, CONV_W)]
                dyv = jnp.where(last, jnp.zeros_like(nxt), nxt)
            dyn = dyv * (sy * (1.0 + yn * (1.0 - sy)))
            if in_tile:
                accl[pl.ds(0, SUB), :] += jnp.sum((dyn * xhat).reshape(LN_ROWS // SUB, SUB, CONV_W), axis=0)
                accl[pl.ds(SUB, SUB), :] += jnp.sum(dyn.reshape(LN_ROWS // SUB, SUB, CONV_W), axis=0)
            dxh = dyn * g_ref[...]
            dcv = rstd * (dxh - jnp.mean(dxh, axis=-1, keepdims=True)
                          - xhat * jnp.mean(dxh * xhat, axis=-1, keepdims=True))
            cv_sc[rows, :] = dcv
            if in_tile:
                accl[pl.ds(2 * SUB, SUB), :] += jnp.sum(dcv.reshape(LN_ROWS // SUB, SUB, CONV_W), axis=0)

        def ln_body(r, carry):
            ln_rows(pl.multiple_of(r * LN_ROWS, LN_ROWS), True)
            return carry

        lax.fori_loop(0, tt // LN_ROWS, ln_body, 0, unroll=2)
        ln_rows(tt, False)
        dg_ref[...] += jnp.sum(accl[pl.ds(0, SUB), :], axis=0, keepdims=True)
        dbb_ref[...] += jnp.sum(accl[pl.ds(SUB, SUB), :], axis=0, keepdims=True)
        db_ref[...] += jnp.sum(accl[pl.ds(2 * SUB, SUB), :], axis=0, keepdims=True)

        accw[...] = jnp.zeros_like(accw)
        for c in range(CONV_W // LANES):
            cols = pl.ds(c * LANES, LANES)

            def chunk(r, carry):
                r0 = pl.multiple_of(r * RB, RB)
                d = cv_sc[pl.ds(r0, RB + HALO), cols]
                g = gl_sc[pl.ds(r0, RB + HALO), cols]
                _shifted(shd_sc, d, RB + HALO - SUB)
                _shifted(sh_sc, g, RB + HALO - SUB)
                d_t = d[:RB]
                dgl = jnp.zeros((RB, LANES), F32)
                for j in range(CONV_K):
                    dgl = dgl + w_ref[pl.ds(j, 1), cols] * _tap(shd_sc, cv_sc, r0, cols, CONV_K - 1 - j, RB)
                    prod = d_t * _tap(sh_sc, gl_sc, r0, cols, HALO - (CONV_K - 1) + j, RB)
                    accw[pl.ds(SUB * j, SUB), cols] += jnp.sum(prod.reshape(RB // SUB, SUB, LANES), axis=0)
                a_t = z_ref[pl.ds(r0, RB), pl.ds(POOL_W + c * LANES, LANES)]
                sb = _sigmoid(z_ref[pl.ds(r0, RB), pl.ds(POOL_W + CONV_W + c * LANES, LANES)])
                dz_ref[pl.ds(r0, RB), pl.ds(POOL_W + c * LANES, LANES)] = (dgl * sb).astype(dz_ref.dtype)
                dz_ref[pl.ds(r0, RB), pl.ds(POOL_W + CONV_W + c * LANES, LANES)] = (
                    dgl * a_t * sb * (1.0 - sb)).astype(dz_ref.dtype)
                return carry

            lax.fori_loop(0, tt // RB, chunk, 0)
        for j in range(CONV_K):
            dw_ref[pl.ds(j, 1), :] += jnp.sum(accw[pl.ds(SUB * j, SUB), :], axis=0, keepdims=True)

    full = lambda shape: pl.BlockSpec(shape, lambda i: (0,) * len(shape))
    nb = T // HALO
    outs = pl.pallas_call(
        body, name=name, grid=(n,),
        in_specs=[pl.BlockSpec((HALO, C), lambda i: (jnp.maximum(i * hb - 1, 0), 0)),
                  pl.BlockSpec((tt, C), lambda i: (i, 0)),
                  pl.BlockSpec((HALO, C), lambda i: (jnp.minimum((i + 1) * hb, nb - 1), 0)),
                  pl.BlockSpec((tt, 2 * POOL_W), lambda i: (i, 0)),
                  pl.BlockSpec((HALO, 2 * POOL_W), lambda i: (jnp.minimum((i + 1) * hb, nb - 1), 0)),
                  full((4, POOL_GROUP, POOL_GROUP)), full((1, POOL_W)), full((CONV_K + 1, CONV_W)),
                  full((1, CONV_W)), full((1, CONV_W)), full((1, CONV_W))],
        out_specs=[pl.BlockSpec((tt, C), lambda i: (i, 0)),
                   full((4, POOL_GROUP, POOL_GROUP)), full((1, POOL_W)), full((CONV_K + 1, CONV_W)),
                   full((1, CONV_W)), full((1, CONV_W)), full((1, CONV_W))],
        out_shape=[jax.ShapeDtypeStruct((T, C), CD),
                   jax.ShapeDtypeStruct((4, POOL_GROUP, POOL_GROUP), F32),
                   jax.ShapeDtypeStruct((1, POOL_W), F32),
                   jax.ShapeDtypeStruct((CONV_K + 1, CONV_W), F32),
                   jax.ShapeDtypeStruct((1, CONV_W), F32),
                   jax.ShapeDtypeStruct((1, CONV_W), F32),
                   jax.ShapeDtypeStruct((1, CONV_W), F32)],
        scratch_shapes=[pltpu.VMEM((tt, POOL_GROUP), CD), pltpu.VMEM((R, POOL_GROUP), CD),
                        pltpu.VMEM((R, POOL_GROUP), F32), pltpu.VMEM((R, POOL_GROUP), F32),
                        pltpu.VMEM((tt + 2 * HALO, CONV_W), F32), pltpu.VMEM((R, CONV_W), F32),
                        pltpu.VMEM((SUB * (CONV_K + 1), CONV_W), F32), pltpu.VMEM((3 * SUB, CONV_W), F32),
                        pltpu.VMEM((SUB, MIX_ROWS + HALO, LANES), F32),
                        pltpu.VMEM((SUB, MIX_ROWS + HALO, LANES), F32)],
        compiler_params=_params("arbitrary"),
    )(z, z, z, dy, dy, pool_w, pool_scale, dw_w, dw_b, ln_g, ln_b)
    return outs


CHUNK_HALO = 16
FFN_ROWS = 64
FFN_LANES = 128


def _rows(cur, prev, nxt, r, rb, before, after, cols, n_r, first, last):
    lo, hi = r * rb - before, r * rb + rb + after
    tt = n_r * rb
    parts = []
    if lo < 0:
        p = prev[pl.ds(HALO + lo, -lo), cols]
        parts.append(jnp.where(first, jnp.zeros_like(p), p))
        lo = 0
    parts.append(cur[pl.ds(lo, min(hi, tt) - lo), cols])
    if hi > tt:
        p = nxt[pl.ds(0, hi - tt), cols]
        parts.append(jnp.where(last, jnp.zeros_like(p), p))
    return parts[0] if len(parts) == 1 else jnp.concatenate(parts, axis=0)


def _ffn_mid_fwd(up, cw, cb, *, name, tt=512):
    T = up.shape[0]
    tt = min(tt, T)
    n = T // tt
    hb = tt // HALO
    RB, CW, HB = min(FFN_ROWS, tt), FFN_LANES, CHUNK_HALO
    n_r = tt // RB

    def body(a_ref, gp_ref, g_ref, w_ref, b_ref, o_ref):
        first = pl.program_id(0) == 0

        def col_chunk(c, carry):
            cols = pl.ds(pl.multiple_of(c * CW, CW), CW)
            w = w_ref[:, cols]
            b = b_ref[:, cols]
            for r in range(n_r):
                v = _rows(g_ref, gp_ref, None, r, RB, HB, 0, cols, n_r, first, None).astype(F32)
                gc = b + w[0:1] * v[HB - 2:HB - 2 + RB] + w[1:2] * v[HB - 1:HB - 1 + RB] + w[2:3] * v[HB:HB + RB]
                a = a_ref[pl.ds(r * RB, RB), cols].astype(F32)
                o_ref[pl.ds(r * RB, RB), cols] = (gc * _sigmoid(gc) * a).astype(o_ref.dtype)
            return carry

        lax.fori_loop(0, D_FF // CW, col_chunk, 0)

    return pl.pallas_call(
        body, name=name, grid=(n,),
        in_specs=[pl.BlockSpec((tt, D_FF), lambda i: (i, 0)),
                  pl.BlockSpec((HALO, D_FF), lambda i: (jnp.maximum(i * hb - 1, 0), 1)),
                  pl.BlockSpec((tt, D_FF), lambda i: (i, 1)),
                  pl.BlockSpec((8, D_FF), lambda i: (0, 0)),
                  pl.BlockSpec((1, D_FF), lambda i: (0, 0))],
        out_specs=pl.BlockSpec((tt, D_FF), lambda i: (i, 0)),
        out_shape=jax.ShapeDtypeStruct((T, D_FF), CD),
        compiler_params=_params("parallel"),
    )(up, up, up, cw, cb)


def _ffn_mid_bwd(up, dact, cw, cb, *, name, tt=512):
    T = up.shape[0]
    tt = min(tt, T)
    n = T // tt
    hb = tt // HALO
    nb = T // HALO
    RB, CW, HB = min(FFN_ROWS, tt), FFN_LANES, CHUNK_HALO
    n_r = tt // RB
    RE = RB + 8

    def body(a_ref, an_ref, gp_ref, g_ref, gn_ref, d_ref, dn_ref, w_ref, b_ref, dup_ref, dw_ref, db_ref, acc):
        i = pl.program_id(0)
        first, last = i == 0, i == n - 1

        @pl.when(first)
        def _():
            dw_ref[...] = jnp.zeros_like(dw_ref)
            db_ref[...] = jnp.zeros_like(db_ref)

        def col_chunk(c, carry):
            cols = pl.ds(pl.multiple_of(c * CW, CW), CW)
            w = w_ref[:, cols]
            b = b_ref[:, cols]
            part = [jnp.zeros((8, CW), F32) for _ in range(FFN_K + 1)]
            for r in range(n_r):
                v = _rows(g_ref, gp_ref, gn_ref, r, RB, HB, HB, cols, n_r, first, last).astype(F32)
                gs = [v[HB - 2 + j:HB - 2 + j + RE] for j in range(FFN_K)]
                gc = b + w[0:1] * gs[0] + w[1:2] * gs[1] + w[2:3] * gs[2]
                sg = _sigmoid(gc)
                d = _rows(d_ref, None, dn_ref, r, RB, 0, HB, cols, n_r, None, last).astype(F32)[:RE]
                a = _rows(a_ref, None, an_ref, r, RB, 0, HB, cols, n_r, None, last).astype(F32)[:RE]
                silu = gc * sg
                dgc = d * a * (sg + silu - silu * sg)
                dup_ref[pl.ds(r * RB, RB), cols] = (d[:RB] * silu[:RB]).astype(dup_ref.dtype)
                dg = w[2:3] * dgc[0:RB] + w[1:2] * dgc[1:RB + 1] + w[0:1] * dgc[2:RB + 2]
                dup_ref[pl.ds(r * RB, RB), pl.ds(pl.multiple_of(D_FF + c * CW, CW), CW)] = dg.astype(dup_ref.dtype)
                dgc_t = dgc[:RB]
                for j in range(FFN_K):
                    part[j] = part[j] + jnp.sum((dgc_t * gs[j][:RB]).reshape(RB // 8, 8, CW), axis=0)
                part[FFN_K] = part[FFN_K] + jnp.sum(dgc_t.reshape(RB // 8, 8, CW), axis=0)
            for j in range(FFN_K + 1):
                acc[pl.ds(8 * j, 8), cols] = part[j]
            return carry

        lax.fori_loop(0, D_FF // CW, col_chunk, 0)
        for j in range(FFN_K):
            dw_ref[pl.ds(j, 1), :] += jnp.sum(acc[pl.ds(8 * j, 8), :], axis=0, keepdims=True)
        db_ref[...] += jnp.sum(acc[pl.ds(8 * FFN_K, 8), :], axis=0, keepdims=True)

    nxt = lambda i: jnp.minimum((i + 1) * hb, nb - 1)
    return pl.pallas_call(
        body, name=name, grid=(n,),
        in_specs=[pl.BlockSpec((tt, D_FF), lambda i: (i, 0)),
                  pl.BlockSpec((HALO, D_FF), lambda i: (nxt(i), 0)),
                  pl.BlockSpec((HALO, D_FF), lambda i: (jnp.maximum(i * hb - 1, 0), 1)),
                  pl.BlockSpec((tt, D_FF), lambda i: (i, 1)),
                  pl.BlockSpec((HALO, D_FF), lambda i: (nxt(i), 1)),
                  pl.BlockSpec((tt, D_FF), lambda i: (i, 0)),
                  pl.BlockSpec((HALO, D_FF), lambda i: (nxt(i), 0)),
                  pl.BlockSpec((8, D_FF), lambda i: (0, 0)),
                  pl.BlockSpec((1, D_FF), lambda i: (0, 0))],
        out_specs=[pl.BlockSpec((tt, 2 * D_FF), lambda i: (i, 0)),
                   pl.BlockSpec((8, D_FF), lambda i: (0, 0)),
                   pl.BlockSpec((1, D_FF), lambda i: (0, 0))],
        out_shape=[jax.ShapeDtypeStruct((T, 2 * D_FF), CD),
                   jax.ShapeDtypeStruct((8, D_FF), F32),
                   jax.ShapeDtypeStruct((1, D_FF), F32)],
        scratch_shapes=[pltpu.VMEM((8 * (FFN_K + 1), D_FF), F32)],
        compiler_params=_params("arbitrary"),
    )(up, up, up, up, up, dact, dact, cw, cb)


def _xattn_probs(q, k):
    s = lax.dot_general(q, k, NT, preferred_element_type=F32) * XA_SCALE
    p = jnp.exp(s - jnp.max(s, axis=-1, keepdims=True))
    return p / jnp.sum(p, axis=-1, keepdims=True)


def _xattn_fwd(q, kv, *, name, tq=512):
    T = q.shape[0]
    tq = min(tq, T)

    def body(q_ref, kv_ref, o_ref):
        for h in range(XA_HEADS):
            cols = pl.ds(h * XA_DH, XA_DH)
            p = _xattn_probs(q_ref[:, cols], kv_ref[:, cols])
            v = kv_ref[:, pl.ds(D_MODEL + h * XA_DH, XA_DH)]
            o_ref[:, cols] = jnp.dot(p.astype(CD), v, preferred_element_type=F32).astype(o_ref.dtype)

    return pl.pallas_call(
        body, name=name, grid=(T // tq,),
        in_specs=[pl.BlockSpec((tq, D_MODEL), lambda i: (i, 0)),
                  pl.BlockSpec((MEM_LEN, 2 * D_MODEL), lambda i: (0, 0))],
        out_specs=pl.BlockSpec((tq, D_MODEL), lambda i: (i, 0)),
        out_shape=jax.ShapeDtypeStruct((T, D_MODEL), CD),
        compiler_params=_params("parallel"),
    )(q, kv)


def _xattn_bwd(q, kv, do, *, name, tq=512):
    T = q.shape[0]
    tq = min(tq, T)

    def body(q_ref, kv_ref, do_ref, dq_ref, dkv_ref):
        @pl.when(pl.program_id(0) == 0)
        def _():
            dkv_ref[...] = jnp.zeros_like(dkv_ref)

        for h in range(XA_HEADS):
            cols = pl.ds(h * XA_DH, XA_DH)
            vcols = pl.ds(D_MODEL + h * XA_DH, XA_DH)
            qh, kh, vh, doh = q_ref[:, cols], kv_ref[:, cols], kv_ref[:, vcols], do_ref[:, cols]
            p = _xattn_probs(qh, kh)
            dkv_ref[:, vcols] += lax.dot_general(p.astype(CD), doh, TN, preferred_element_type=F32)
            dp = lax.dot_general(doh, vh, NT, preferred_element_type=F32)
            ds = (p * (dp - jnp.sum(dp * p, axis=-1, keepdims=True)) * XA_SCALE).astype(CD)
            dq_ref[:, cols] = jnp.dot(ds, kh, preferred_element_type=F32).astype(dq_ref.dtype)
            dkv_ref[:, cols] += lax.dot_general(ds, qh, TN, preferred_element_type=F32)

    return pl.pallas_call(
        body, name=name, grid=(T // tq,),
        in_specs=[pl.BlockSpec((tq, D_MODEL), lambda i: (i, 0)),
                  pl.BlockSpec((MEM_LEN, 2 * D_MODEL), lambda i: (0, 0)),
                  pl.BlockSpec((tq, D_MODEL), lambda i: (i, 0))],
        out_specs=[pl.BlockSpec((tq, D_MODEL), lambda i: (i, 0)),
                   pl.BlockSpec((MEM_LEN, 2 * D_MODEL), lambda i: (0, 0))],
        out_shape=[jax.ShapeDtypeStruct((T, D_MODEL), CD),
                   jax.ShapeDtypeStruct((MEM_LEN, 2 * D_MODEL), F32)],
        compiler_params=_params("arbitrary"),
    )(q, kv, do)


C_W = Q_LORA + KV_LORA + LANES


def _rot(x):
    lane = lax.broadcasted_iota(jnp.int32, x.shape, x.ndim - 1)
    up = pltpu.roll(x, LANES - QK_ROPE // 2, x.ndim - 1)
    dn = pltpu.roll(x, QK_ROPE // 2, x.ndim - 1)
    lo, mid, hi = QK_NOPE, QK_NOPE + QK_ROPE // 2, QK_NOPE + QK_ROPE
    return jnp.where((lane >= lo) & (lane < mid), -up, jnp.where((lane >= mid) & (lane < hi), dn, 0.0))


def _mla_mid_fwd(c, qg, kvg, cs, sn, *, name, tt=512):
    T = c.shape[0]
    tt = min(tt, T)

    def body(c_ref, qg_ref, kg_ref, cs_ref, sn_ref, qn_ref, kn_ref, kpe_ref):
        cq = c_ref[:, pl.ds(0, Q_LORA)]
        qn_ref[...] = (cq * lax.rsqrt(jnp.mean(cq * cq, axis=-1, keepdims=True) + EPS)
                       * qg_ref[...]).astype(qn_ref.dtype)
        ck = c_ref[:, pl.ds(Q_LORA, KV_LORA)]
        kn_ref[...] = (ck * lax.rsqrt(jnp.mean(ck * ck, axis=-1, keepdims=True) + EPS)
                       * kg_ref[...]).astype(kn_ref.dtype)
        kp = c_ref[:, pl.ds(Q_LORA + KV_LORA, LANES)]
        kpe_ref[...] = kp * cs_ref[...] + _rot(kp) * sn_ref[...]

    row = lambda w: pl.BlockSpec((tt, w), lambda i: (i, 0))
    one = lambda w: pl.BlockSpec((1, w), lambda i: (0, 0))
    return pl.pallas_call(
        body, name=name, grid=(T // tt,),
        in_specs=[row(C_W), one(Q_LORA), one(KV_LORA), row(LANES), row(LANES)],
        out_specs=[row(Q_LORA), row(KV_LORA), row(LANES)],
        out_shape=[jax.ShapeDtypeStruct((T, Q_LORA), CD), jax.ShapeDtypeStruct((T, KV_LORA), CD),
                   jax.ShapeDtypeStruct((T, LANES), F32)],
        compiler_params=_params("parallel"),
    )(c, qg, kvg, cs, sn)


def _mla_mid_bwd(c, dqn, dkvn, dksum, qg, kvg, cs, sn, *, name, tt=512):
    T = c.shape[0]
    tt = min(tt, T)

    def body(c_ref, dq_ref, dk_ref, ds_ref, qg_ref, kg_ref, cs_ref, sn_ref, dc_ref, dqg_ref, dkg_ref):
        @pl.when(pl.program_id(0) == 0)
        def _():
            dqg_ref[...] = jnp.zeros_like(dqg_ref)
            dkg_ref[...] = jnp.zeros_like(dkg_ref)

        dx, dg = _rms_bwd(c_ref[:, pl.ds(0, Q_LORA)], qg_ref[...], dq_ref[...])
        dc_ref[:, pl.ds(0, Q_LORA)] = dx.astype(dc_ref.dtype)
        dqg_ref[...] += jnp.sum(dg, axis=0, keepdims=True)
        dx, dg = _rms_bwd(c_ref[:, pl.ds(Q_LORA, KV_LORA)], kg_ref[...], dk_ref[...])
        dc_ref[:, pl.ds(Q_LORA, KV_LORA)] = dx.astype(dc_ref.dtype)
        dkg_ref[...] += jnp.sum(dg, axis=0, keepdims=True)
        d = ds_ref[...]
        lane = lax.broadcasted_iota(jnp.int32, d.shape, 1)
        dkp = d * cs_ref[...] - _rot(d * sn_ref[...])
        dc_ref[:, pl.ds(Q_LORA + KV_LORA, LANES)] = jnp.where(
            (lane >= QK_NOPE) & (lane < QK_NOPE + QK_ROPE), dkp, 0.0).astype(dc_ref.dtype)

    row = lambda w: pl.BlockSpec((tt, w), lambda i: (i, 0))
    one = lambda w: pl.BlockSpec((1, w), lambda i: (0, 0))
    return pl.pallas_call(
        body, name=name, grid=(T // tt,),
        in_specs=[row(C_W), row(Q_LORA), row(KV_LORA), row(LANES), one(Q_LORA), one(KV_LORA),
                  row(LANES), row(LANES)],
        out_specs=[row(C_W), one(Q_LORA), one(KV_LORA)],
        out_shape=[jax.ShapeDtypeStruct((T, C_W), CD), jax.ShapeDtypeStruct((1, Q_LORA), F32),
                   jax.ShapeDtypeStruct((1, KV_LORA), F32)],
        compiler_params=_params("arbitrary"),
    )(c, dqn, dkvn, dksum, qg, kvg, cs, sn)


def _mla_qkv_fwd(qn, kvn, kpe, cs, sn, wq, wk, wv, *, name, tt=256):
    T = qn.shape[0]
    tt = min(tt, T)
    H = MLA_HEADS

    def body(qn_ref, kn_ref, kpe_ref, cs_ref, sn_ref, wq_ref, wk_ref, wv_ref, q_ref, k_ref, v_ref):
        qn_v, kn_v, kpe_v, cs_v, sn_v = qn_ref[...], kn_ref[...], kpe_ref[...], cs_ref[...], sn_ref[...]
        for h in range(H):
            q = jnp.dot(qn_v, wq_ref[h], preferred_element_type=F32)
            q_ref[h] = (q * cs_v + _rot(q) * sn_v).astype(q_ref.dtype)
            k_ref[h] = (jnp.dot(kn_v, wk_ref[h], preferred_element_type=F32) + kpe_v).astype(k_ref.dtype)
            v_ref[h] = jnp.dot(kn_v, wv_ref[h], preferred_element_type=F32).astype(v_ref.dtype)

    row = lambda w: pl.BlockSpec((tt, w), lambda i: (i, 0))
    wsp = lambda k: pl.BlockSpec((H, k, LANES), lambda i: (0, 0, 0))
    hsp = pl.BlockSpec((H, tt, LANES), lambda i: (0, i, 0))
    sh = jax.ShapeDtypeStruct((H, T, LANES), CD)
    return pl.pallas_call(
        body, name=name, grid=(T // tt,),
        in_specs=[row(Q_LORA), row(KV_LORA), row(LANES), row(LANES), row(LANES),
                  wsp(Q_LORA), wsp(KV_LORA), wsp(KV_LORA)],
        out_specs=[hsp, hsp, hsp], out_shape=[sh, sh, sh],
        compiler_params=_params("parallel"),
    )(qn, kvn, kpe, cs, sn, wq, wk, wv)


def _mla_qkv_bwd(dq, dk, dv, qn, kvn, cs, sn, wq, wk, wv, *, name, tt=256):
    T = qn.shape[0]
    tt = min(tt, T)
    H = MLA_HEADS

    def body(dq_ref, dk_ref, dv_ref, qn_ref, kn_ref, cs_ref, sn_ref, wq_ref, wk_ref, wv_ref,
             dqn_ref, dkn_ref, dks_ref, dwq_ref, dwk_ref, dwv_ref):
        @pl.when(pl.program_id(0) == 0)
        def _():
            for r in (dwq_ref, dwk_ref, dwv_ref):
                r[...] = jnp.zeros_like(r)

        qn_v, kn_v, cs_v, sn_v = qn_ref[...], kn_ref[...], cs_ref[...], sn_ref[...]
        dqn = jnp.zeros((tt, Q_LORA), F32)
        dkn = jnp.zeros((tt, KV_LORA), F32)
        dks = jnp.zeros((tt, LANES), F32)
        for h in range(H):
            d = dq_ref[h]
            dqh = (d * cs_v - _rot(d * sn_v)).astype(CD)
            dkh, dvh = dk_ref[h], dv_ref[h]
            dqn = dqn + lax.dot_general(dqh, wq_ref[h], NT, preferred_element_type=F32)
            dkn = dkn + lax.dot_general(dkh, wk_ref[h], NT, preferred_element_type=F32)
            dkn = dkn + lax.dot_general(dvh, wv_ref[h], NT, preferred_element_type=F32)
            dks = dks + dkh.astype(F32)
            dwq_ref[h] += lax.dot_general(qn_v, dqh, TN, preferred_element_type=F32)
            dwk_ref[h] += lax.dot_general(kn_v, dkh, TN, preferred_element_type=F32)
            dwv_ref[h] += lax.dot_general(kn_v, dvh, TN, preferred_element_type=F32)
        dqn_ref[...] = dqn
        dkn_ref[...] = dkn
        dks_ref[...] = dks

    row = lambda w: pl.BlockSpec((tt, w), lambda i: (i, 0))
    wsp = lambda k: pl.BlockSpec((H, k, LANES), lambda i: (0, 0, 0))
    hsp = pl.BlockSpec((H, tt, LANES), lambda i: (0, i, 0))
    return pl.pallas_call(
        body, name=name, grid=(T // tt,),
        in_specs=[hsp, hsp, hsp, row(Q_LORA), row(KV_LORA), row(LANES), row(LANES),
                  wsp(Q_LORA), wsp(KV_LORA), wsp(KV_LORA)],
        out_specs=[row(Q_LORA), row(KV_LORA), row(LANES), wsp(Q_LORA), wsp(KV_LORA), wsp(KV_LORA)],
        out_shape=[jax.ShapeDtypeStruct((T, Q_LORA), F32), jax.ShapeDtypeStruct((T, KV_LORA), F32),
                   jax.ShapeDtypeStruct((T, LANES), F32),
                   jax.ShapeDtypeStruct((H, Q_LORA, LANES), F32),
                   jax.ShapeDtypeStruct((H, KV_LORA, LANES), F32),
                   jax.ShapeDtypeStruct((H, KV_LORA, LANES), F32)],
        compiler_params=_params("arbitrary"),
    )(dq, dk, dv, qn, kvn, cs, sn, wq, wk, wv)


FLASH_BLOCK = 1024
EXP2_SCALE = MLA_SCALE * math.log2(math.e)


def _causal_steps(nq, by_key):
    pairs = [(i, j) for j in range(nq) for i in range(j, nq)] if by_key else \
            [(i, j) for i in range(nq) for j in range(i + 1)]
    return (jnp.asarray([p[0] for p in pairs], jnp.int32), jnp.asarray([p[1] for p in pairs], jnp.int32))


def _raw_scores(q, k, masked, first_row=0):
    s = lax.dot_general(q, k, NT, preferred_element_type=F32)
    if masked:
        row = lax.broadcasted_iota(jnp.int32, s.shape, 0) + first_row
        col = lax.broadcasted_iota(jnp.int32, s.shape, 1)
        s = jnp.where(col <= row, s, NEG)
    return s


def _flash_fwd(q, k, v, *, name):
    H, T, _ = q.shape
    tq = min(FLASH_BLOCK, T)
    nq = T // tq
    i_tab, j_tab = _causal_steps(nq, by_key=False)

    rb = min(128, tq)

    def body(i_tab, j_tab, q_ref, k_ref, v_ref, o_ref, lse_ref, m_sc, l_sc, acc, s_sc, p_sc):
        t = pl.program_id(1)
        i, j = i_tab[t], j_tab[t]

        @pl.when(j == 0)
        def _():
            m_sc[...] = jnp.full_like(m_sc, NEG)
            l_sc[...] = jnp.zeros_like(l_sc)
            acc[...] = jnp.zeros_like(acc)

        hb = tq // 2

        def step(masked):
            lane = lax.broadcasted_iota(jnp.int32, (tq, LANES), 1)
            top, bot = pl.ds(0, hb), pl.ds(hb, hb)
            alphas, pvs = [], []
            for h in range(2):
                if masked:
                    s_sc[h, top, top] = _raw_scores(q_ref[h, top, :], k_ref[h, top, :], True)
                    s_sc[h, bot, :] = _raw_scores(q_ref[h, bot, :], k_ref[h], True, first_row=hb)
                    m_cur = jnp.concatenate([jnp.max(s_sc[h, top, top], axis=-1, keepdims=True),
                                             jnp.max(s_sc[h, bot, :], axis=-1, keepdims=True)], axis=0)
                else:
                    s_sc[h] = _raw_scores(q_ref[h], k_ref[h], False)
                    m_cur = jnp.max(s_sc[h], axis=-1, keepdims=True)
                m_prev = m_sc[h]
                m_new = jnp.maximum(m_prev, m_cur)
                alpha = jnp.exp2((m_prev - m_new) * EXP2_SCALE)
                m_sc[h] = m_new
                for r in range(tq // rb):
                    rows = pl.ds(r * rb, rb)
                    m_r = m_sc[h, rows, :]
                    part = jnp.zeros((rb, LANES), F32)
                    keys = hb if masked and r * rb < hb else tq
                    for c in range(keys // LANES):
                        cols = pl.ds(c * LANES, LANES)
                        p = jnp.exp2((s_sc[h, rows, cols] - m_r) * EXP2_SCALE)
                        part = part + p
                        p_sc[h, rows, cols] = p.astype(CD)
                    l_sc[h, rows, :] = (alpha[r * rb:(r + 1) * rb] * l_sc[h, rows, :]
                                        + jnp.sum(part, axis=-1, keepdims=True))
                alphas.append(alpha)
                if masked:
                    pvs.append(jnp.concatenate(
                        [jnp.dot(p_sc[h, top, top], v_ref[h, top, :], preferred_element_type=F32),
                         jnp.dot(p_sc[h, bot, :], v_ref[h], preferred_element_type=F32)], axis=0))
                else:
                    pvs.append(jnp.dot(p_sc[h], v_ref[h], preferred_element_type=F32))
            acc[...] = acc[...] * jnp.where(lane < V_HEAD, alphas[0], alphas[1]) + pvs[0] + pvs[1]

        @pl.when(j < i)
        def _():
            step(False)

        @pl.when(j == i)
        def _():
            step(True)
            lane = lax.broadcasted_iota(jnp.int32, (tq, LANES), 1)
            o_ref[...] = (acc[...] / jnp.where(lane < V_HEAD, l_sc[0], l_sc[1])).astype(o_ref.dtype)
            for h in range(2):
                lse_ref[h] = m_sc[h] * EXP2_SCALE + jnp.log2(l_sc[h])

    qsp = pl.BlockSpec((2, tq, LANES), lambda p, t, it, jt: (p, it[t], 0))
    ksp = pl.BlockSpec((2, tq, LANES), lambda p, t, it, jt: (p, jt[t], 0))
    return pl.pallas_call(
        body, name=name,
        grid_spec=pltpu.PrefetchScalarGridSpec(
            num_scalar_prefetch=2, grid=(H // 2, int(i_tab.shape[0])),
            in_specs=[qsp, ksp, ksp],
            out_specs=[pl.BlockSpec((tq, LANES), lambda p, t, it, jt: (it[t], p)), qsp],
            scratch_shapes=[pltpu.VMEM((2, tq, LANES), F32), pltpu.VMEM((2, tq, LANES), F32),
                            pltpu.VMEM((tq, LANES), F32),
                            pltpu.VMEM((2, tq, tq), F32), pltpu.VMEM((2, tq, tq), CD)]),
        out_shape=[jax.ShapeDtypeStruct((T, H * V_HEAD), CD), jax.ShapeDtypeStruct((H, T, LANES), F32)],
        compiler_params=_params("parallel", "arbitrary"),
    )(i_tab, j_tab, q, k, v)


def _flash_delta(o, do, *, name, tt=512):
    T = o.shape[0]
    tt = min(tt, T)
    H = MLA_HEADS

    def body(o_ref, do_ref, dl_ref):
        lane = lax.broadcasted_iota(jnp.int32, (tt, LANES), 1)
        for p in range(H // 2):
            cols = pl.ds(p * LANES, LANES)
            prod = do_ref[:, cols].astype(F32) * o_ref[:, cols].astype(F32)
            d0 = jnp.sum(jnp.where(lane < V_HEAD, prod, 0.0), axis=-1, keepdims=True)
            d1 = jnp.sum(jnp.where(lane < V_HEAD, 0.0, prod), axis=-1, keepdims=True)
            dl_ref[2 * p] = jnp.broadcast_to(d0, (tt, LANES))
            dl_ref[2 * p + 1] = jnp.broadcast_to(d1, (tt, LANES))

    row = pl.BlockSpec((tt, H * V_HEAD), lambda i: (i, 0))
    return pl.pallas_call(
        body, name=name, grid=(T // tt,), in_specs=[row, row],
        out_specs=pl.BlockSpec((H, tt, LANES), lambda i: (0, i, 0)),
        out_shape=jax.ShapeDtypeStruct((H, T, LANES), F32),
        compiler_params=_params("parallel"),
    )(o, do)


def _flash_bwd(q, k, v, do, lse, delta, *, name):
    H, T, _ = q.shape
    tq = min(FLASH_BLOCK, T)
    nq = T // tq
    i_tab, j_tab = _causal_steps(nq, by_key=True)

    def body(i_tab, j_tab, q_ref, k_ref, v_ref, do_ref, lse_ref, dl_ref, dq_ref, dk_ref, dv_ref, dk_acc, dv_acc):
        t = pl.program_id(1)
        i, j = i_tab[t], j_tab[t]
        rows = pl.ds(pl.multiple_of(i * tq, tq), tq)

        @pl.when(t == 0)
        def _():
            dq_ref[...] = jnp.zeros_like(dq_ref)

        def block(h, qr, kr, first_row, masked):
            qh, kh, vh, do_v = q_ref[h, qr, :], k_ref[h, kr, :], v_ref[h, kr, :], do_ref[qr, :]
            s = _raw_scores(qh, kh, masked, first_row)
            p = jnp.exp2(s * EXP2_SCALE - lse_ref[h, qr, :][:, :1])
            dv_acc[h, kr, :] += lax.dot_general(p.astype(CD), do_v, TN, preferred_element_type=F32)
            dp = lax.dot_general(do_v, vh, NT, preferred_element_type=F32)
            ds = (p * (dp - dl_ref[h, qr, :][:, :1]) * MLA_SCALE).astype(CD)
            dk_acc[h, kr, :] += lax.dot_general(ds, qh, TN, preferred_element_type=F32)
            dq_rows = pl.ds(pl.multiple_of(i * tq + qr.start, qr.size), qr.size)
            dq_ref[h, dq_rows, :] += jnp.dot(ds, kh, preferred_element_type=F32)

        def step(masked):
            hb = tq // 2
            for h in range(2):
                if masked:
                    block(h, pl.ds(0, hb), pl.ds(0, hb), 0, True)
                    block(h, pl.ds(hb, hb), pl.ds(0, tq), hb, True)
                else:
                    block(h, pl.ds(0, tq), pl.ds(0, tq), 0, False)

        @pl.when(i == j)
        def _():
            dk_acc[...] = jnp.zeros_like(dk_acc)
            dv_acc[...] = jnp.zeros_like(dv_acc)
            step(True)

        @pl.when(i > j)
        def _():
            step(False)

        @pl.when(i == nq - 1)
        def _():
            lane = lax.broadcasted_iota(jnp.int32, (tq, LANES), 1)
            dk_ref[...] = dk_acc[...].astype(dk_ref.dtype)
            dv_ref[0] = jnp.where(lane < V_HEAD, dv_acc[0], 0.0).astype(dv_ref.dtype)
            dv_ref[1] = jnp.where(lane < V_HEAD, 0.0, dv_acc[1]).astype(dv_ref.dtype)

    qsp = pl.BlockSpec((2, tq, LANES), lambda p, t, it, jt: (p, it[t], 0))
    ksp = pl.BlockSpec((2, tq, LANES), lambda p, t, it, jt: (p, jt[t], 0))
    osp = pl.BlockSpec((tq, LANES), lambda p, t, it, jt: (it[t], p))
    sh = jax.ShapeDtypeStruct((H, T, LANES), CD)
    return pl.pallas_call(
        body, name=name,
        grid_spec=pltpu.PrefetchScalarGridSpec(
            num_scalar_prefetch=2, grid=(H // 2, int(i_tab.shape[0])),
            in_specs=[qsp, ksp, ksp, osp, qsp, qsp],
            out_specs=[pl.BlockSpec((2, T, LANES), lambda p, t, it, jt: (p, 0, 0)), ksp, ksp],
            scratch_shapes=[pltpu.VMEM((2, tq, LANES), F32), pltpu.VMEM((2, tq, LANES), F32)]),
        out_shape=[jax.ShapeDtypeStruct((H, T, LANES), F32), sh, sh],
        compiler_params=_params("parallel", "arbitrary"),
    )(i_tab, j_tab, q, k, v, do, lse, delta)


def _loss_head(x, g, target, *, name, tt=512):
    T, D = x.shape
    tt = min(tt, T)

    def body(x_ref, g_ref, t_ref, dx_ref, dg_ref, loss_ref):
        @pl.when(pl.program_id(0) == 0)
        def _():
            dg_ref[...] = jnp.zeros_like(dg_ref)
            loss_ref[...] = jnp.zeros_like(loss_ref)

        xv, gv = x_ref[...], g_ref[...]
        r = lax.rsqrt(jnp.mean(xv * xv, axis=-1, keepdims=True) + EPS)
        err = xv * r * gv - t_ref[...]
        tok = jnp.mean(err * err, axis=-1, keepdims=True)
        loss_ref[...] += 0.5 * jnp.sum(tok, axis=0, keepdims=True)
        dx, dg_rows = _rms_bwd(xv, gv, err * (1.0 / D))
        dx_ref[...] = dx
        dg_ref[...] += jnp.sum(dg_rows, axis=0, keepdims=True)

    return pl.pallas_call(
        body, name=name, grid=(T // tt,),
        in_specs=[pl.BlockSpec((tt, D), lambda i: (i, 0)), pl.BlockSpec((1, D), lambda i: (0, 0)),
                  pl.BlockSpec((tt, D), lambda i: (i, 0))],
        out_specs=[pl.BlockSpec((tt, D), lambda i: (i, 0)), pl.BlockSpec((1, D), lambda i: (0, 0)),
                   pl.BlockSpec((1, LANES), lambda i: (0, 0))],
        out_shape=[jax.ShapeDtypeStruct((T, D), F32), jax.ShapeDtypeStruct((1, D), F32),
                   jax.ShapeDtypeStruct((1, LANES), F32)],
        compiler_params=_params("arbitrary"),
    )(x, g, target)


def _rope_tables(positions):
    inv = 1.0 / (ROPE_THETA ** (jnp.arange(0, QK_ROPE, 2, dtype=F32) / QK_ROPE))
    ang = positions.astype(F32)[:, None] * inv
    c, s = jnp.cos(ang), jnp.sin(ang)
    T = positions.shape[0]
    cs = jnp.concatenate([jnp.ones((T, QK_NOPE), F32), c, c, jnp.zeros((T, LANES - QK_NOPE - QK_ROPE), F32)], 1)
    sn = jnp.concatenate([jnp.zeros((T, QK_NOPE), F32), s, s, jnp.zeros((T, LANES - QK_NOPE - QK_ROPE), F32)], 1)
    return cs, sn


def _pad_rows(w, rows):
    return jnp.concatenate([w, jnp.zeros((rows - w.shape[0],) + w.shape[1:], w.dtype)], 0)


def _mla_weights(w_dq_dkv, w_uq, w_ukv):
    K = w_dq_dkv.shape[0]
    z = lambda n: jnp.zeros((K, n), w_dq_dkv.dtype)
    wc = jnp.concatenate([w_dq_dkv[:, :Q_LORA + KV_LORA], z(QK_NOPE), w_dq_dkv[:, Q_LORA + KV_LORA:],
                          z(LANES - QK_NOPE - QK_ROPE)], 1)
    wq = w_uq.reshape(Q_LORA, MLA_HEADS, QK_NOPE + QK_ROPE).transpose(1, 0, 2)
    wq = jnp.concatenate([wq, jnp.zeros((MLA_HEADS, Q_LORA, LANES - QK_NOPE - QK_ROPE), wq.dtype)], 2)
    wkv = w_ukv.reshape(KV_LORA, MLA_HEADS, QK_NOPE + V_HEAD).transpose(1, 0, 2)
    zero = jnp.zeros_like(wkv[:, :, :QK_NOPE])
    wk = jnp.concatenate([wkv[:, :, :QK_NOPE], zero], 2)
    wv_lo = jnp.concatenate([wkv[:, :, QK_NOPE:], zero], 2)
    wv_hi = jnp.concatenate([zero, wkv[:, :, QK_NOPE:]], 2)
    odd = (jnp.arange(MLA_HEADS) % 2 == 1)[:, None, None]
    wv = jnp.where(odd, wv_hi, wv_lo)
    return wc, wq, wk, wv


def _mla_weight_grads(dwc, dwq, dwk, dwv):
    d_dq = jnp.concatenate([dwc[:, :Q_LORA + KV_LORA],
                            dwc[:, Q_LORA + KV_LORA + QK_NOPE:Q_LORA + KV_LORA + QK_NOPE + QK_ROPE]], 1)
    d_uq = dwq[:, :, :QK_NOPE + QK_ROPE].transpose(1, 0, 2).reshape(Q_LORA, MLA_HEADS * (QK_NOPE + QK_ROPE))
    odd = (jnp.arange(MLA_HEADS) % 2 == 1)[:, None, None]
    dv = jnp.where(odd, dwv[:, :, V_HEAD:], dwv[:, :, :V_HEAD])
    d_ukv = jnp.concatenate([dwk[:, :, :QK_NOPE], dv], 2).transpose(1, 0, 2).reshape(
        KV_LORA, MLA_HEADS * (QK_NOPE + V_HEAD))
    return d_dq, d_uq, d_ukv


def _local_step(x, mem, positions, target, W):
    G = {}
    row = lambda v: v.reshape(1, -1)
    cs, sn = _rope_tables(positions)
    saved = []
    for l in range(DEPTH):
        L = f"l{l}"
        s = {"x0": x}
        if l % 2 == 0:
            e = l // 2
            s["z"], s["h"] = _nmm(x, row(W["norm_mix_g"][l]), (W["pc_w_in"], e), name=f"{L}_mix_in", out_dtype=F32)
            s["dw_w"] = _pad_rows(W["conv_dw_w"][e], CONV_K + 1)
            s["mix_p"] = (W["pool_w"][e], row(W["pool_scale"][e]), s["dw_w"], row(W["conv_dw_b"][e]),
                          row(W["conv_ln_g"][e]), row(W["conv_ln_b"][e]))
            s["ycat"] = _mixer_fwd(s["z"], *s["mix_p"], name=f"{L}_mix_mid")
            x = _mm_res(s["ycat"], (W["pc_w_out"], e), x, name=f"{L}_mix_out")
        else:
            o = l // 2
            wc, wq, wk, wv = _mla_weights(W["mla_w_dq_dkv"][o], W["mla_w_uq"][o], W["mla_w_ukv"][o])
            s["mla_w"] = (wc, wq, wk, wv)
            s["c"], s["h"] = _nmm(x, row(W["norm_mix_g"][l]), wc, name=f"{L}_mla_down", out_dtype=F32)
            s["qg"], s["kvg"] = row(W["mla_q_norm_g"][o]), row(W["mla_kv_norm_g"][o])
            s["qn"], s["kvn"], kpe = _mla_mid_fwd(s["c"], s["qg"], s["kvg"], cs, sn, name=f"{L}_mla_mid")
            s["q"], s["k"], s["v"] = _mla_qkv_fwd(s["qn"], s["kvn"], kpe, cs, sn, wq, wk, wv, name=f"{L}_mla_qkv")
            s["o"], s["lse"] = _flash_fwd(s["q"], s["k"], s["v"], name=f"{L}_mla_attn")
            x = _mm_res(s["o"], (W["mla_w_o"], o), x, name=f"{L}_mla_out")
        s["x1"] = x
        s["xq"], s["hx"] = _nmm(x, row(W["norm_xa_g"][l]), (W["xa_wq"], l), name=f"{L}_xa_q", out_dtype=CD)
        s["xkv"], s["hm"] = _nmm(mem, row(W["norm_mem_g"][l]), (W["xa_wkv"], l), name=f"{L}_xa_kv", out_dtype=CD)
        s["xo"] = _xattn_fwd(s["xq"], s["xkv"], name=f"{L}_xa_attn")
        x = _mm_res(s["xo"], (W["xa_wo"], l), x, name=f"{L}_xa_out")
        s["x2"] = x
        s["up"], s["hf"] = _nmm(x, row(W["norm_ffn_g"][l]), (W["ffn_w_up"], l), name=f"{L}_ffn_up", out_dtype=CD,
                                tn_target=1408)
        s["cw"], s["cb"] = _pad_rows(W["ffn_conv_w"][l], 8), row(W["ffn_conv_b"][l])
        s["act"] = _ffn_mid_fwd(s["up"], s["cw"], s["cb"], name=f"{L}_ffn_mid")
        x = _mm_res(s["act"], (W["ffn_w_down"], l), x, name=f"{L}_ffn_down")
        saved.append(s)
    dx, G["final_norm_g"], loss = _loss_head(x, row(W["final_norm_g"]), target, name="loss_head")
    G["final_norm_g"] = G["final_norm_g"].reshape(-1)

    per_layer = {}

    def put(name, l, val):
        per_layer.setdefault(name, {})[l] = val

    for l in reversed(range(DEPTH)):
        L = f"l{l}"
        s = saved[l]
        put("ffn_w_down", l, _mm_tn(s["act"], dx, name=f"{L}_ffn_down_dw", tk_target=1408))
        dact = _mm_nt(dx, (W["ffn_w_down"], l), name=f"{L}_ffn_down_dx", out_dtype=CD, tn_target=1408)
        dup, dcw, dcb = _ffn_mid_bwd(s["up"], dact, s["cw"], s["cb"], name=f"{L}_ffn_mid_bwd")
        put("ffn_conv_w", l, dcw[:FFN_K])
        put("ffn_conv_b", l, dcb[0])
        put("ffn_w_up", l, _mm_tn(s["hf"], dup, name=f"{L}_ffn_up_dw", tn_target=1408))
        dx, dg = _mm_nt_normbwd(dup, (W["ffn_w_up"], l), s["x2"], row(W["norm_ffn_g"][l]), dx, name=f"{L}_ffn_up_dx")
        put("norm_ffn_g", l, dg[0])
        put("xa_wo", l, _mm_tn(s["xo"], dx, name=f"{L}_xa_out_dw"))
        do = _mm_nt(dx, (W["xa_wo"], l), name=f"{L}_xa_out_dx", out_dtype=CD)
        dq, dkv = _xattn_bwd(s["xq"], s["xkv"], do, name=f"{L}_xa_attn_bwd")
        put("xa_wq", l, _mm_tn(s["hx"], dq, name=f"{L}_xa_q_dw"))
        dx, dg = _mm_nt_normbwd(dq, (W["xa_wq"], l), s["x1"], row(W["norm_xa_g"][l]), dx, name=f"{L}_xa_q_dx")
        put("norm_xa_g", l, dg[0])
        put("xa_wkv", l, _mm_tn(s["hm"], dkv, name=f"{L}_xa_kv_dw", tt=MEM_LEN))
        _, dg = _mm_nt_normbwd(dkv, (W["xa_wkv"], l), mem, row(W["norm_mem_g"][l]), jnp.zeros_like(mem),
                               name=f"{L}_xa_kv_dx", tm=MEM_LEN)
        put("norm_mem_g", l, dg[0])
        if l % 2 == 0:
            e = l // 2
            put("pc_w_out", e, _mm_tn(s["ycat"], dx, name=f"{L}_mix_out_dw"))
            dy = _mm_nt(dx, (W["pc_w_out"], e), name=f"{L}_mix_out_dx", out_dtype=F32)
            dz, dpw, dps, ddw, ddb, dlg, dlb = _mixer_bwd(s["z"], dy, *s["mix_p"], name=f"{L}_mix_mid_bwd")
            put("pool_w", e, dpw)
            put("pool_scale", e, dps[0])
            put("conv_dw_w", e, ddw[:CONV_K])
            put("conv_dw_b", e, ddb[0])
            put("conv_ln_g", e, dlg[0])
            put("conv_ln_b", e, dlb[0])
            put("pc_w_in", e, _mm_tn(s["h"], dz, name=f"{L}_mix_in_dw"))
            dx, dg = _mm_nt_normbwd(dz, (W["pc_w_in"], e), s["x0"], row(W["norm_mix_g"][l]), dx, name=f"{L}_mix_in_dx")
        else:
            o = l // 2
            wc, wq, wk, wv = s["mla_w"]
            put("mla_w_o", o, _mm_tn(s["o"], dx, name=f"{L}_mla_out_dw"))
            do = _mm_nt(dx, (W["mla_w_o"], o), name=f"{L}_mla_out_dx", out_dtype=CD)
            delta = _flash_delta(s["o"], do, name=f"{L}_mla_attn_delta")
            dq, dk, dv = _flash_bwd(s["q"], s["k"], s["v"], do, s["lse"], delta, name=f"{L}_mla_attn_bwd")
            dqn, dkvn, dks, dwq, dwk, dwv = _mla_qkv_bwd(dq, dk, dv, s["qn"], s["kvn"], cs, sn, wq, wk, wv,
                                                         name=f"{L}_mla_qkv_bwd")
            dc, dqg, dkg = _mla_mid_bwd(s["c"], dqn, dkvn, dks, s["qg"], s["kvg"], cs, sn, name=f"{L}_mla_mid_bwd")
            put("mla_q_norm_g", o, dqg[0])
            put("mla_kv_norm_g", o, dkg[0])
            dwc = _mm_tn(s["h"], dc, name=f"{L}_mla_down_dw")
            d_dq, d_uq, d_ukv = _mla_weight_grads(dwc, dwq, dwk, dwv)
            put("mla_w_dq_dkv", o, d_dq)
            put("mla_w_uq", o, d_uq)
            put("mla_w_ukv", o, d_ukv)
            dx, dg = _mm_nt_normbwd(dc, wc, s["x0"], row(W["norm_mix_g"][l]), dx, name=f"{L}_mla_down_dx",
                                    tk_target=768)
        put("norm_mix_g", l, dg[0])
    for name, d in per_layer.items():
        G[name] = jnp.stack([d[i] for i in sorted(d)], 0)
    return loss, dx, G


_ANY = pl.BlockSpec(memory_space=pl.ANY)


def _all_gather(xs, *, name):
    n = len(xs)

    def body(*refs):
        x_refs, out_refs = refs[:n], refs[n:2 * n]
        send_sems, recv_sems, local_sems = refs[2 * n:]
        mx, my, mc = lax.axis_index("x"), lax.axis_index("y"), lax.axis_index("c")
        me, sibling = (mx, my, mc), (mx, my, 1 - mc)
        chips = [(1 - mx, my), (mx, 1 - my), (1 - mx, 1 - my)]

        def copy(a, k, block, to, own=False):
            px, py, pc = block
            dst = out_refs[a].at[4 * px + 2 * py + pc]
            return pltpu.make_async_remote_copy(
                src_ref=x_refs[a] if own else dst, dst_ref=dst,
                send_sem=send_sems.at[7 * a + k], recv_sem=recv_sems.at[7 * a + k],
                device_id=to, device_id_type=MESH)

        mine = [pltpu.make_async_copy(x_refs[a], out_refs[a].at[4 * mx + 2 * my + mc], local_sems.at[a])
                for a in range(n)]
        for cp in mine:
            cp.start()
        first = []
        for j, chip in enumerate(chips):
            first += [copy(a, 1 + j, me, (*chip, mc), own=True) for a in range(n)]
        first += [copy(a, 0, me, sibling, own=True) for a in range(n)]
        for cp in first:
            cp.start()
        passed = []
        for j, chip in enumerate(chips):
            for a in range(n):
                copy(a, 1 + j, (*chip, mc), me).wait_recv()
                passed.append(copy(a, 4 + j, (*chip, mc), sibling))
                passed[-1].start()
        for a in range(n):
            copy(a, 0, sibling, me).wait_recv()
        for j, chip in enumerate(chips):
            for a in range(n):
                copy(a, 4 + j, (*chip, 1 - mc), me).wait_recv()
        for cp in first + passed:
            cp.wait_send()
        for cp in mine:
            cp.wait()

    return pl.pallas_call(
        body, name=name, in_specs=[_ANY] * n, out_specs=[_ANY] * n,
        out_shape=[jax.ShapeDtypeStruct((N_DEV,) + x.shape, x.dtype) for x in xs],
        scratch_shapes=[pltpu.SemaphoreType.DMA((7 * n,)), pltpu.SemaphoreType.DMA((7 * n,)),
                        pltpu.SemaphoreType.DMA((n,))],
    )(*xs)


N_CHIP = 4


def _pair_exchange(ps, *, name):
    n = len(ps)

    def body(*refs):
        p_refs, out_refs = refs[:n], refs[n:2 * n]
        send_sems, recv_sems = refs[2 * n:]
        mx, my, mc = lax.axis_index("x"), lax.axis_index("y"), lax.axis_index("c")
        copies = []
        for a in range(n):
            for chip in range(N_CHIP):
                copies.append(pltpu.make_async_remote_copy(
                    src_ref=p_refs[a].at[2 * chip + (1 - mc)], dst_ref=out_refs[a].at[chip],
                    send_sem=send_sems.at[N_CHIP * a + chip], recv_sem=recv_sems.at[N_CHIP * a + chip],
                    device_id=(mx, my, 1 - mc), device_id_type=MESH))
        for cp in copies:
            cp.start()
        for cp in copies:
            cp.wait()

    return pl.pallas_call(
        body, name=name, in_specs=[_ANY] * n, out_specs=[_ANY] * n,
        out_shape=[jax.ShapeDtypeStruct((N_CHIP,) + p.shape[1:], p.dtype) for p in ps],
        scratch_shapes=[pltpu.SemaphoreType.DMA((N_CHIP * n,)), pltpu.SemaphoreType.DMA((N_CHIP * n,))],
    )(*ps)


def _pair_sum(p, recv, core, *, name):
    _, R, C = p.shape
    tr = _row_tile(R, C, 4 * ROW_TILE_ELEMS)
    p4 = p.reshape(N_CHIP, 2, R, C)

    def body(core_ref, a_ref, b_ref, o_ref):
        o_ref[...] = (a_ref[...].astype(F32) + b_ref[...].astype(F32)).astype(o_ref.dtype)

    return pl.pallas_call(
        body, name=name,
        grid_spec=pltpu.PrefetchScalarGridSpec(
            num_scalar_prefetch=1, grid=(N_CHIP, R // tr),
            in_specs=[pl.BlockSpec((None, None, tr, C), lambda ch, i, core: (ch, core[0], i, 0)),
                      pl.BlockSpec((None, tr, C), lambda ch, i, core: (ch, i, 0))],
            out_specs=pl.BlockSpec((None, tr, C), lambda ch, i, core: (ch, i, 0))),
        out_shape=jax.ShapeDtypeStruct((N_CHIP, R, C), p.dtype),
        compiler_params=_params("parallel", "parallel"),
    )(core, p4, recv)


def _chip_exchange(ss, *, name):
    n = len(ss)

    def body(*refs):
        s_refs, out_refs = refs[:n], refs[n:2 * n]
        send_sems, recv_sems, local_sems = refs[2 * n:]
        mx, my, mc = lax.axis_index("x"), lax.axis_index("y"), lax.axis_index("c")
        chip = 2 * mx + my
        mine = [pltpu.make_async_copy(s_refs[a].at[chip], out_refs[a].at[chip], local_sems.at[a]) for a in range(n)]
        for cp in mine:
            cp.start()
        copies = []
        for k in range(1, N_CHIP):
            px, py = mx ^ ((k >> 1) & 1), my ^ (k & 1)
            for a in range(n):
                copies.append(pltpu.make_async_remote_copy(
                    src_ref=s_refs[a].at[2 * px + py], dst_ref=out_refs[a].at[chip],
                    send_sem=send_sems.at[3 * a + k - 1], recv_sem=recv_sems.at[3 * a + k - 1],
                    device_id=(px, py, mc), device_id_type=MESH))
        for cp in copies:
            cp.start()
        for cp in copies:
            cp.wait()
        for cp in mine:
            cp.wait()

    return pl.pallas_call(
        body, name=name, in_specs=[_ANY] * n, out_specs=[_ANY] * n,
        out_shape=[jax.ShapeDtypeStruct(s.shape, s.dtype) for s in ss],
        scratch_shapes=[pltpu.SemaphoreType.DMA((3 * n,)), pltpu.SemaphoreType.DMA((3 * n,)),
                        pltpu.SemaphoreType.DMA((n,))],
    )(*ss)


ROW_TILE_ELEMS = 256 * 1024


def _row_tile(R, C, elems=None):
    elems = ROW_TILE_ELEMS if elems is None else elems
    for t in (4096, 2048, 1024, 512, 256, 128, 64, 32, 16):
        if R % t == 0 and t * C <= elems:
            return t
    raise ValueError((R, C))


def _sum_slots(gs, *, name):
    S, R, C = gs.shape
    tr = _row_tile(R, C)

    def body(g_ref, o_ref):
        g = g_ref[0].astype(F32)
        for s in range(1, S):
            g = g + g_ref[s].astype(F32)
        o_ref[...] = g

    return pl.pallas_call(
        body, name=name, grid=(R // tr,),
        in_specs=[pl.BlockSpec((S, tr, C), lambda i: (0, i, 0))],
        out_specs=pl.BlockSpec((tr, C), lambda i: (i, 0)),
        out_shape=jax.ShapeDtypeStruct((R, C), F32),
        compiler_params=_params("parallel"),
    )(gs)


def _adamw(gs, w, m, v, *, name):
    S, R, C = gs.shape
    tr = _row_tile(R, C, 2 * ROW_TILE_ELEMS)

    def body(g_ref, w_ref, m_ref, v_ref, g_out, d_out, m_out, v_out):
        g = g_ref[0].astype(F32)
        for s in range(1, S):
            g = g + g_ref[s].astype(F32)
        m_new = ADAM_B1 * m_ref[...] + (1.0 - ADAM_B1) * g
        v_new = ADAM_B2 * v_ref[...] + (1.0 - ADAM_B2) * (g * g)
        m_hat = m_new / (1.0 - ADAM_B1 ** ADAM_STEP)
        v_hat = v_new / (1.0 - ADAM_B2 ** ADAM_STEP)
        g_out[...] = g
        d_out[...] = -ADAM_LR * (m_hat / (jnp.sqrt(v_hat) + ADAM_EPS) + ADAM_WD * w_ref[...])
        m_out[...] = m_new
        v_out[...] = v_new

    blk = pl.BlockSpec((tr, C), lambda i: (i, 0))
    sh = jax.ShapeDtypeStruct((R, C), F32)
    return pl.pallas_call(
        body, name=name, grid=(R // tr,),
        in_specs=[pl.BlockSpec((S, tr, C), lambda i: (0, i, 0)), blk, blk, blk],
        out_specs=[blk, blk, blk, blk], out_shape=[sh, sh, sh, sh],
        compiler_params=_params("parallel"),
    )(gs, w, m, v)


PIECE = 16 * LANES


def _pack(arrs, dtype, lead, row_mult):
    lead_shape = arrs[0].shape[:lead]
    parts, meta, off = [], [], 0
    for a in arrs:
        size = math.prod(a.shape[lead:])
        padded = -(-size // PIECE) * PIECE
        flat = a.astype(dtype).reshape(lead_shape + (size,))
        if padded != size:
            flat = jnp.concatenate([flat, jnp.zeros(lead_shape + (padded - size,), dtype)], -1)
        parts.append(flat)
        meta.append((off, size, a.shape[lead:]))
        off += padded
    total = -(-off // (row_mult * LANES)) * (row_mult * LANES)
    if total != off:
        parts.append(jnp.zeros(lead_shape + (total - off,), dtype))
    return jnp.concatenate(parts, -1).reshape(lead_shape + (total // LANES, LANES)), meta


def _unpack(packed, meta, lead):
    lead_shape = packed.shape[:lead]
    flat = packed.reshape(lead_shape + (-1,))
    return [flat[..., off:off + size].reshape(lead_shape + shape) for off, size, shape in meta]


ARG_NAMES = ['x', 'mem', 'positions', 'norm_mix_g', 'norm_xa_g', 'norm_mem_g', 'xa_wq', 'xa_wkv', 'xa_wo', 'norm_ffn_g', 'ffn_w_up', 'ffn_conv_w', 'ffn_conv_b', 'ffn_w_down', 'pc_w_in', 'pool_w', 'pool_scale', 'conv_dw_w', 'conv_dw_b', 'conv_ln_g', 'conv_ln_b', 'pc_w_out', 'mla_w_dq_dkv', 'mla_q_norm_g', 'mla_w_uq', 'mla_kv_norm_g', 'mla_w_ukv', 'mla_w_o', 'final_norm_g', 'loss_target']
WEIGHTS = ARG_NAMES[3:29]
BIG = {'xa_wq': 1, 'xa_wkv': 2, 'xa_wo': 1, 'ffn_w_up': 2, 'ffn_w_down': 1, 'pc_w_in': 2, 'pc_w_out': 1,
       'mla_w_dq_dkv': 1, 'mla_w_uq': 2, 'mla_w_ukv': 2, 'mla_w_o': 1}
SMALL_SHARDED = {'ffn_conv_w': 2, 'conv_dw_w': 2, 'mla_q_norm_g': 1, 'mla_kv_norm_g': 1}
REPLICATED = [n for n in WEIGHTS if n not in BIG and n not in SMALL_SHARDED]


def _from_slots(g, axis):
    t = jnp.moveaxis(g, 0, axis)
    return t.reshape(t.shape[:axis] + (t.shape[axis] * t.shape[axis + 1],) + t.shape[axis + 2:])


def _to_slots(full, axis):
    n = full.shape[axis] // N_DEV
    t = full.reshape(full.shape[:axis] + (N_DEV, n) + full.shape[axis + 1:])
    return jnp.moveaxis(t, axis, 0)


def kernel(x, mem, positions, norm_mix_g, norm_xa_g, norm_mem_g, xa_wq, xa_wkv, xa_wo, norm_ffn_g, ffn_w_up, ffn_conv_w, ffn_conv_b, ffn_w_down, pc_w_in, pool_w, pool_scale, conv_dw_w, conv_dw_b, conv_ln_g, conv_ln_b, pc_w_out, mla_w_dq_dkv, mla_q_norm_g, mla_w_uq, mla_kv_norm_g, mla_w_ukv, mla_w_o, final_norm_g, loss_target, m_norm_mix_g, m_norm_xa_g, m_norm_mem_g, m_xa_wq, m_xa_wkv, m_xa_wo, m_norm_ffn_g, m_ffn_w_up, m_ffn_conv_w, m_ffn_conv_b, m_ffn_w_down, m_pc_w_in, m_pool_w, m_pool_scale, m_conv_dw_w, m_conv_dw_b, m_conv_ln_g, m_conv_ln_b, m_pc_w_out, m_mla_w_dq_dkv, m_mla_q_norm_g, m_mla_w_uq, m_mla_kv_norm_g, m_mla_w_ukv, m_mla_w_o, m_final_norm_g, v_norm_mix_g, v_norm_xa_g, v_norm_mem_g, v_xa_wq, v_xa_wkv, v_xa_wo, v_norm_ffn_g, v_ffn_w_up, v_ffn_conv_w, v_ffn_conv_b, v_ffn_w_down, v_pc_w_in, v_pool_w, v_pool_scale, v_conv_dw_w, v_conv_dw_b, v_conv_ln_g, v_conv_ln_b, v_pc_w_out, v_mla_w_dq_dkv, v_mla_q_norm_g, v_mla_w_uq, v_mla_kv_norm_g, v_mla_w_ukv, v_mla_w_o, v_final_norm_g):
    args = (x, mem, positions, norm_mix_g, norm_xa_g, norm_mem_g, xa_wq, xa_wkv, xa_wo, norm_ffn_g, ffn_w_up, ffn_conv_w, ffn_conv_b, ffn_w_down, pc_w_in, pool_w, pool_scale, conv_dw_w, conv_dw_b, conv_ln_g, conv_ln_b, pc_w_out, mla_w_dq_dkv, mla_q_norm_g, mla_w_uq, mla_kv_norm_g, mla_w_ukv, mla_w_o, final_norm_g, loss_target)
    a = dict(zip(ARG_NAMES, args))
    mom = dict(zip(WEIGHTS, (m_norm_mix_g, m_norm_xa_g, m_norm_mem_g, m_xa_wq, m_xa_wkv, m_xa_wo, m_norm_ffn_g, m_ffn_w_up, m_ffn_conv_w, m_ffn_conv_b, m_ffn_w_down, m_pc_w_in, m_pool_w, m_pool_scale, m_conv_dw_w, m_conv_dw_b, m_conv_ln_g, m_conv_ln_b, m_pc_w_out, m_mla_w_dq_dkv, m_mla_q_norm_g, m_mla_w_uq, m_mla_kv_norm_g, m_mla_w_ukv, m_mla_w_o, m_final_norm_g)))
    var = dict(zip(WEIGHTS, (v_norm_mix_g, v_norm_xa_g, v_norm_mem_g, v_xa_wq, v_xa_wkv, v_xa_wo, v_norm_ffn_g, v_ffn_w_up, v_ffn_conv_w, v_ffn_conv_b, v_ffn_w_down, v_pc_w_in, v_pool_w, v_pool_scale, v_conv_dw_w, v_conv_dw_b, v_conv_ln_g, v_conv_ln_b, v_pc_w_out, v_mla_w_dq_dkv, v_mla_q_norm_g, v_mla_w_uq, v_mla_kv_norm_g, v_mla_w_ukv, v_mla_w_o, v_final_norm_g)))
    me = 4 * lax.axis_index("x") + 2 * lax.axis_index("y") + lax.axis_index("c")

    big_all = _all_gather([a[n].astype(CD) for n in BIG], name="gather_weights")
    sm_pack, sm_meta = _pack([a[n] for n in SMALL_SHARDED], F32, 0, 8)
    sm_all = _unpack(_all_gather([sm_pack], name="gather_small")[0], sm_meta, 1)
    W = {n: a[n] for n in REPLICATED}
    for (n, ax), g in zip(BIG.items(), big_all):
        W[n] = _from_slots(g, ax)
    for (n, ax), g in zip(SMALL_SHARDED.items(), sm_all):
        W[n] = _from_slots(g, ax)

    loss, dx, G = _local_step(x[0], mem[0], positions[0], loss_target[0], W)

    parts = [_to_slots(G[n], ax).astype(CD) for n, ax in BIG.items()]
    from_sibling = _pair_exchange(parts, name="grads_to_sibling")
    core = lax.axis_index("c").astype(jnp.int32).reshape(1)
    sums = []
    for n, p, r in zip(BIG, parts, from_sibling):
        cols = p.shape[-1]
        s = _pair_sum(p.reshape(N_DEV, -1, cols), r.reshape(N_CHIP, -1, cols), core, name=f"pair_sum_{n}")
        sums.append(s.reshape((N_CHIP,) + p.shape[1:]))
    recv = _chip_exchange(sums, name="scatter_grads")
    out = {}
    for n, r in zip(BIG, recv):
        shape = a[n].shape
        rows = lambda t: t.reshape(-1, shape[-1])
        res = _adamw(r.reshape(N_CHIP, -1, shape[-1]), rows(a[n]), rows(mom[n]), rows(var[n]), name=f"adamw_{n}")
        out[n] = tuple(t.reshape(shape) for t in res)

    small_names = REPLICATED + list(SMALL_SHARDED)
    spack, smeta = _pack([G[n] for n in small_names] + [loss], F32, 0, 256)
    stot = _unpack(_sum_slots(_all_gather([spack], name="gather_small_grads")[0], name="sum_small_grads"), smeta, 0)
    loss_total = stot[-1][0, 0]
    gsm = dict(zip(small_names, stot[:-1]))
    for n, ax in SMALL_SHARDED.items():
        width = a[n].shape[ax]
        gsm[n] = lax.dynamic_slice_in_dim(gsm[n], me * width, width, ax)
    g1, meta1 = _pack([gsm[n] for n in small_names], F32, 0, 256)
    w1, _ = _pack([a[n] for n in small_names], F32, 0, 256)
    m1, _ = _pack([mom[n] for n in small_names], F32, 0, 256)
    v1, _ = _pack([var[n] for n in small_names], F32, 0, 256)
    res = [_unpack(r, meta1, 0) for r in _adamw(g1[None], w1, m1, v1, name="adamw_small")]
    for i, n in enumerate(small_names):
        out[n] = tuple(r[i] for r in res)

    return (loss_total, dx[None],
            *[out[n][0] for n in WEIGHTS], *[out[n][1] for n in WEIGHTS],
            *[out[n][2] for n in WEIGHTS], *[out[n][3] for n in WEIGHTS])
```

```python
import functools
import math

import jax
import jax.numpy as jnp
from jax import lax
from jax.experimental import pallas as pl
from jax.experimental.pallas import tpu as pltpu

F32 = jnp.float32
CD = jnp.bfloat16
EPS = 1e-6
NEG = -1e30
N_DEV = 8
LANES = 128
HALO = 32

D_MODEL = 1024
DEPTH = 4
XA_HEADS = 4
XA_DH = 256
MEM_LEN = 256
POOL_WINDOWS = (2, 4, 8, 16)
CONV_K = 31
FFN_K = 3
D_FF = 2816
MLA_HEADS = 16
QK_NOPE = 64
QK_ROPE = 32
V_HEAD = 64
Q_LORA = 384
KV_LORA = 256
ROPE_THETA = 10000.0
MLA_SCALE = 1.0 / math.sqrt(QK_NOPE + QK_ROPE)
XA_SCALE = XA_DH ** -0.5

ADAM_LR = 0.001
ADAM_B1 = 0.9
ADAM_B2 = 0.999
ADAM_EPS = 1e-08
ADAM_WD = 0.01
ADAM_STEP = 10

NT = (((1,), (1,)), ((), ()))
TN = (((0,), (0,)), ((), ()))
MESH = pl.DeviceIdType.MESH


def _tile(n, target):
    if n <= target:
        return n
    best = None
    for t in range(LANES, target + 1, LANES):
        if n % t == 0:
            best = t
    assert best is not None, (n, target)
    return best


def _params(*sem):
    return pltpu.CompilerParams(dimension_semantics=sem)


def _sigmoid(v):
    return 0.5 * jnp.tanh(0.5 * v) + 0.5


def _rms_bwd(x, gain, dh):
    r = lax.rsqrt(jnp.mean(x * x, axis=-1, keepdims=True) + EPS)
    xhat = x * r
    dxhat = dh * gain
    dx = r * (dxhat - xhat * jnp.mean(dxhat * xhat, axis=-1, keepdims=True))
    return dx, dh * xhat


def _weight(w):
    if not isinstance(w, tuple):
        return w, w.shape, pl.BlockSpec
    arr, layer = w

    def spec(block, imap):
        return pl.BlockSpec((None,) + tuple(block), lambda *a: (layer,) + tuple(imap(*a)))

    return arr, arr.shape[1:], spec


def _nmm(x, g, w, *, name, out_dtype, tm=1024, tn_target=1024):
    M, K = x.shape
    w, (_, N), wspec = _weight(w)
    tm = min(tm, M)
    tn = _tile(N, tn_target)

    def body(x_ref, g_ref, w_ref, z_ref, h_ref):
        @pl.when(pl.program_id(1) == 0)
        def _():
            xf = x_ref[...]
            r = lax.rsqrt(jnp.mean(xf * xf, axis=-1, keepdims=True) + EPS)
            h_ref[...] = (xf * r * g_ref[...]).astype(h_ref.dtype)

        z_ref[...] = jnp.dot(h_ref[...], w_ref[...], preferred_element_type=F32).astype(z_ref.dtype)

    return pl.pallas_call(
        body, name=name, grid=(M // tm, N // tn),
        in_specs=[pl.BlockSpec((tm, K), lambda i, j: (i, 0)),
                  pl.BlockSpec((1, K), lambda i, j: (0, 0)),
                  wspec((K, tn), lambda i, j: (0, j))],
        out_specs=[pl.BlockSpec((tm, tn), lambda i, j: (i, j)),
                   pl.BlockSpec((tm, K), lambda i, j: (i, 0))],
        out_shape=[jax.ShapeDtypeStruct((M, N), out_dtype), jax.ShapeDtypeStruct((M, K), CD)],
        compiler_params=_params("parallel", "arbitrary"),
    )(x, g, w)


def _mm_res(a, w, res, *, name, tm=1024, tn_target=1024):
    M, K = a.shape
    w, (_, N), wspec = _weight(w)
    tm = min(tm, M)
    tn = _tile(N, tn_target)

    def body(a_ref, w_ref, r_ref, o_ref):
        o_ref[...] = r_ref[...] + jnp.dot(a_ref[...].astype(CD), w_ref[...], preferred_element_type=F32)

    return pl.pallas_call(
        body, name=name, grid=(M // tm, N // tn),
        in_specs=[pl.BlockSpec((tm, K), lambda i, j: (i, 0)),
                  wspec((K, tn), lambda i, j: (0, j)),
                  pl.BlockSpec((tm, tn), lambda i, j: (i, j))],
        out_specs=pl.BlockSpec((tm, tn), lambda i, j: (i, j)),
        out_shape=jax.ShapeDtypeStruct((M, N), F32),
        compiler_params=_params("parallel", "arbitrary"),
    )(a, w, res)


def _mm_nt(a, w, *, name, out_dtype, tm=1024, tn_target=1024):
    M, K = a.shape
    w, (N, _), wspec = _weight(w)
    tm = min(tm, M)
    tn = _tile(N, tn_target)

    def body(a_ref, w_ref, o_ref):
        o_ref[...] = lax.dot_general(a_ref[...].astype(CD), w_ref[...], NT,
                                     preferred_element_type=F32).astype(o_ref.dtype)

    return pl.pallas_call(
        body, name=name, grid=(M // tm, N // tn),
        in_specs=[pl.BlockSpec((tm, K), lambda i, j: (i, 0)),
                  wspec((tn, K), lambda i, j: (j, 0))],
        out_specs=pl.BlockSpec((tm, tn), lambda i, j: (i, j)),
        out_shape=jax.ShapeDtypeStruct((M, N), out_dtype),
        compiler_params=_params("parallel", "arbitrary"),
    )(a, w)


def _mm_nt_normbwd(gy, w, x, gain, dres, *, name, tm=1024, tk_target=1408):
    M, K = gy.shape
    w, (D, _), wspec = _weight(w)
    tm = min(tm, M)
    tk = _tile(K, tk_target)
    nk = K // tk

    def body(g_ref, w_ref, x_ref, gain_ref, dres_ref, dx_ref, dg_ref, acc):
        i, k = pl.program_id(0), pl.program_id(1)

        @pl.when(k == 0)
        def _():
            acc[...] = jnp.zeros_like(acc)

        acc[...] += lax.dot_general(g_ref[...].astype(CD), w_ref[...], NT, preferred_element_type=F32)

        @pl.when(k == nk - 1)
        def _():
            dx, dg_rows = _rms_bwd(x_ref[...], gain_ref[...], acc[...])
            dx_ref[...] = dres_ref[...] + dx

            @pl.when(i == 0)
            def _():
                dg_ref[...] = jnp.zeros_like(dg_ref)

            dg_ref[...] += jnp.sum(dg_rows, axis=0, keepdims=True)

    return pl.pallas_call(
        body, name=name, grid=(M // tm, nk),
        in_specs=[pl.BlockSpec((tm, tk), lambda i, k: (i, k)),
                  wspec((D, tk), lambda i, k: (0, k)),
                  pl.BlockSpec((tm, D), lambda i, k: (i, 0)),
                  pl.BlockSpec((1, D), lambda i, k: (0, 0)),
                  pl.BlockSpec((tm, D), lambda i, k: (i, 0))],
        out_specs=[pl.BlockSpec((tm, D), lambda i, k: (i, 0)),
                   pl.BlockSpec((1, D), lambda i, k: (0, 0))],
        out_shape=[jax.ShapeDtypeStruct((M, D), F32), jax.ShapeDtypeStruct((1, D), F32)],
        scratch_shapes=[pltpu.VMEM((tm, D), F32)],
        compiler_params=_params("arbitrary", "arbitrary"),
    )(gy, w, x, gain, dres)


def _mm_tn(a, g, *, name, tt=2048, tk_target=1024, tn_target=1024):
    T, K = a.shape
    N = g.shape[1]
    tt = min(tt, T)
    tk = _tile(K, tk_target)
    tn = _tile(N, tn_target)

    def body(a_ref, g_ref, o_ref):
        @pl.when(pl.program_id(2) == 0)
        def _():
            o_ref[...] = jnp.zeros_like(o_ref)

        o_ref[...] += lax.dot_general(a_ref[...].astype(CD), g_ref[...].astype(CD), TN,
                                      preferred_element_type=F32)

    return pl.pallas_call(
        body, name=name, grid=(K // tk, N // tn, T // tt),
        in_specs=[pl.BlockSpec((tt, tk), lambda i, j, t: (t, i)),
                  pl.BlockSpec((tt, tn), lambda i, j, t: (t, j))],
        out_specs=pl.BlockSpec((tk, tn), lambda i, j, t: (i, j)),
        out_shape=jax.ShapeDtypeStruct((K, N), F32),
        compiler_params=_params("parallel", "parallel", "arbitrary"),
    )(a, g)


POOL_W = 512
CONV_W = 512
POOL_GROUP = 128


MIX_ROWS = 64
LN_ROWS = 32
SUB = 8


def _shifted(sh_sc, x, n_rows):
    for b in range(1, SUB):
        sh_sc[b, pl.ds(0, n_rows), :] = x[b:b + n_rows]


def _tap(sh_sc, src, r0, cols, start, rows):
    a, b = divmod(start, SUB)
    if b == 0:
        return src[pl.ds(r0 + SUB * a, rows), cols]
    return sh_sc[b, pl.ds(SUB * a, rows), :]


def _pool_rows(zp_ref, z_ref, cols, win, i, tt, first, pooled_sc):
    RB = min(MIX_ROWS, tt)
    hb = 2 * SUB
    for r in range(tt // RB):
        if r == 0:
            p = zp_ref[pl.ds(HALO - hb, hb), cols]
            v = jnp.concatenate([jnp.where(first, jnp.zeros_like(p), p), z_ref[pl.ds(0, RB), cols]], axis=0)
        else:
            v = z_ref[pl.ds(r * RB - hb, RB + hb), cols]
        u = v[hb:hb + RB]
        s = u
        for j in range(1, win):
            s = s + v[hb - j:hb - j + RB]
        t_glob = i * tt + r * RB + lax.broadcasted_iota(jnp.int32, (RB, 1), 0)
        cnt = jnp.minimum(t_glob + 1, win).astype(F32)
        pooled_sc[pl.ds(r * RB, RB), :] = (s / cnt - u).astype(pooled_sc.dtype)


def _fill_gl(gl_sc, zp_ref, z_ref, zn_ref, tt, first, last):
    ca, cb = pl.ds(POOL_W, CONV_W), pl.ds(POOL_W + CONV_W, CONV_W)
    g = zp_ref[:, ca] * _sigmoid(zp_ref[:, cb])
    gl_sc[pl.ds(0, HALO), :] = jnp.where(first, jnp.zeros_like(g), g)

    def rows(r, carry):
        r0 = pl.multiple_of(r * LN_ROWS, LN_ROWS)
        gl_sc[pl.ds(HALO + r0, LN_ROWS), :] = z_ref[pl.ds(r0, LN_ROWS), ca] * _sigmoid(z_ref[pl.ds(r0, LN_ROWS), cb])
        return carry

    lax.fori_loop(0, tt // LN_ROWS, rows, 0)
    if zn_ref is not None:
        g = zn_ref[:, ca] * _sigmoid(zn_ref[:, cb])
        gl_sc[pl.ds(HALO + tt, HALO), :] = jnp.where(last, jnp.zeros_like(g), g)


def _conv_rows(gl_sc, cv_sc, sh_sc, w_ref, b_ref, n_rows):
    RB = min(MIX_ROWS, n_rows)
    for c in range(CONV_W // LANES):
        cols = pl.ds(c * LANES, LANES)
        bias = b_ref[:, cols]

        def chunk(r0, rb):
            g = gl_sc[pl.ds(r0, rb + HALO), cols]
            _shifted(sh_sc, g, rb + HALO - SUB)
            cv = jnp.zeros((rb, LANES), F32) + bias
            for j in range(CONV_K):
                cv = cv + w_ref[pl.ds(j, 1), cols] * _tap(sh_sc, gl_sc, r0, cols, HALO - (CONV_K - 1) + j, rb)
            cv_sc[pl.ds(r0, rb), cols] = cv

        def body(r, carry):
            chunk(pl.multiple_of(r * RB, RB), RB)
            return carry

        lax.fori_loop(0, n_rows // RB, body, 0)
        if n_rows % RB:
            chunk((n_rows // RB) * RB, n_rows % RB)


def _mixer_fwd(z, pool_w, pool_scale, dw_w, dw_b, ln_g, ln_b, *, name, tt=512):
    T, C = z.shape
    tt = min(tt, T)
    n = T // tt
    hb = tt // HALO

    def body(zp_ref, z_ref, pw_ref, ps_ref, w_ref, b_ref, g_ref, bb_ref, o_ref, pooled_sc, gl_sc, cv_sc, sh_sc):
        i = pl.program_id(0)
        first = i == 0
        for gi, win in enumerate(POOL_WINDOWS):
            cols = pl.ds(gi * POOL_GROUP, POOL_GROUP)
            _pool_rows(zp_ref, z_ref, cols, win, i, tt, first, pooled_sc)
            ya = jnp.dot(pooled_sc[...], pw_ref[gi].astype(CD), preferred_element_type=F32)
            o_ref[:, cols] = (ya * ps_ref[:, cols]).astype(o_ref.dtype)
        _fill_gl(gl_sc, zp_ref, z_ref, None, tt, first, None)
        _conv_rows(gl_sc, cv_sc, sh_sc, w_ref, b_ref, tt)

        def ln_rows(r, carry):
            rows = pl.ds(pl.multiple_of(r * LN_ROWS, LN_ROWS), LN_ROWS)
            cv = cv_sc[rows, :]
            xc = cv - jnp.mean(cv, axis=-1, keepdims=True)
            yn = xc * lax.rsqrt(jnp.mean(xc * xc, axis=-1, keepdims=True) + EPS) * g_ref[...] + bb_ref[...]
            o_ref[rows, pl.ds(POOL_W, CONV_W)] = (yn * _sigmoid(yn)).astype(o_ref.dtype)
            return carry

        lax.fori_loop(0, tt // LN_ROWS, ln_rows, 0, unroll=4)

    full = lambda shape: pl.BlockSpec(shape, lambda i: (0,) * len(shape))
    return pl.pallas_call(
        body, name=name, grid=(n,),
        in_specs=[pl.BlockSpec((HALO, C), lambda i: (jnp.maximum(i * hb - 1, 0), 0)),
                  pl.BlockSpec((tt, C), lambda i: (i, 0)),
                  full((4, POOL_GROUP, POOL_GROUP)), full((1, POOL_W)), full((CONV_K + 1, CONV_W)),
                  full((1, CONV_W)), full((1, CONV_W)), full((1, CONV_W))],
        out_specs=pl.BlockSpec((tt, POOL_W + CONV_W), lambda i: (i, 0)),
        out_shape=jax.ShapeDtypeStruct((T, POOL_W + CONV_W), CD),
        scratch_shapes=[pltpu.VMEM((tt, POOL_GROUP), CD), pltpu.VMEM((tt + HALO, CONV_W), F32),
                        pltpu.VMEM((tt, CONV_W), F32), pltpu.VMEM((SUB, MIX_ROWS + HALO, LANES), F32)],
        compiler_params=_params("parallel"),
    )(z, z, pool_w, pool_scale, dw_w, dw_b, ln_g, ln_b)


def _mixer_bwd(z, dy, pool_w, pool_scale, dw_w, dw_b, ln_g, ln_b, *, name, tt=512):
    T, C = z.shape
    tt = min(tt, T)
    n = T // tt
    hb = tt // HALO
    R = tt + HALO
    RB = min(MIX_ROWS, tt)

    def body(zp_ref, z_ref, zn_ref, dy_ref, dyn_ref, pw_ref, ps_ref, w_ref, b_ref, g_ref, bb_ref,
             dz_ref, dpw_ref, dps_ref, dw_ref, db_ref, dg_ref, dbb_ref,
             pooled_sc, dm_sc, dpool_sc, dpe_sc, gl_sc, cv_sc, accw, accl, sh_sc, shd_sc):
        i = pl.program_id(0)
        first, last = i == 0, i == n - 1

        @pl.when(first)
        def _():
            for r in (dpw_ref, dps_ref, dw_ref, db_ref, dg_ref, dbb_ref):
                r[...] = jnp.zeros_like(r)

        def dy_rows(cols):
            nxt = dyn_ref[:, cols]
            return jnp.concatenate([dy_ref[:, cols], jnp.where(last, jnp.zeros_like(nxt), nxt)], axis=0)

        t_all = i * tt + lax.broadcasted_iota(jnp.int32, (R, 1), 0)
        for gi, win in enumerate(POOL_WINDOWS):
            cols = pl.ds(gi * POOL_GROUP, POOL_GROUP)
            _pool_rows(zp_ref, z_ref, cols, win, i, tt, first, pooled_sc)
            pw = pw_ref[gi].astype(CD)
            dya = dy_rows(cols)
            mm = jnp.dot(pooled_sc[...], pw, preferred_element_type=F32)
            dps_ref[:, cols] += jnp.sum(dya[:tt] * mm, axis=0, keepdims=True)
            dm_sc[...] = (dya * ps_ref[:, cols]).astype(CD)
            dpw_ref[gi] += lax.dot_general(pooled_sc[...], dm_sc[pl.ds(0, tt), :], TN, preferred_element_type=F32)
            dpool = lax.dot_general(dm_sc[...], pw, NT, preferred_element_type=F32)
            dpool_sc[...] = dpool
            dpe_sc[...] = dpool / jnp.minimum(t_all + 1, win).astype(F32)

            def du_rows(r, carry):
                r0 = pl.multiple_of(r * RB, RB)
                e = dpe_sc[pl.ds(r0, RB + 2 * SUB), :]
                du = -dpool_sc[pl.ds(r0, RB), :]
                for j in range(win):
                    du = du + e[j:j + RB]
                dz_ref[pl.ds(r0, RB), cols] = du.astype(dz_ref.dtype)
                return carry

            lax.fori_loop(0, tt // RB, du_rows, 0)

        _fill_gl(gl_sc, zp_ref, z_ref, zn_ref, tt, first, last)
        _conv_rows(gl_sc, cv_sc, sh_sc, w_ref, b_ref, R)
        accl[...] = jnp.zeros_like(accl)

        def ln_rows(r0, in_tile):
            rows = pl.ds(r0, LN_ROWS)
            cv = cv_sc[rows, :]
            xc = cv - jnp.mean(cv, axis=-1, keepdims=True)
            rstd = lax.rsqrt(jnp.mean(xc * xc, axis=-1, keepdims=True) + EPS)
            xhat = xc * rstd
            yn = xhat * g_ref[...] + bb_ref[...]
            sy = _sigmoid(yn)
            if in_tile:
                dyv = dy_ref[rows, pl.ds(POOL_W, CONV_W)]
            else:
                nxt = dyn_ref[:, pl.ds(POOL_W, CONV_W)]
                dyv = jnp.where(last, jnp.zeros_like(nxt), nxt)
            dyn = dyv * (sy * (1.0 + yn * (1.0 - sy)))
            if in_tile:
                accl[pl.ds(0, SUB), :] += jnp.sum((dyn * xhat).reshape(LN_ROWS // SUB, SUB, CONV_W), axis=0)
                accl[pl.ds(SUB, SUB), :] += jnp.sum(dyn.reshape(LN_ROWS // SUB, SUB, CONV_W), axis=0)
            dxh = dyn * g_ref[...]
            dcv = rstd * (dxh - jnp.mean(dxh, axis=-1, keepdims=True)
                          - xhat * jnp.mean(dxh * xhat, axis=-1, keepdims=True))
            cv_sc[rows, :] = dcv
            if in_tile:
                accl[pl.ds(2 * SUB, SUB), :] += jnp.sum(dcv.reshape(LN_ROWS // SUB, SUB, CONV_W), axis=0)

        def ln_body(r, carry):
            ln_rows(pl.multiple_of(r * LN_ROWS, LN_ROWS), True)
            return carry

        lax.fori_loop(0, tt // LN_ROWS, ln_body, 0, unroll=2)
        ln_rows(tt, False)
        dg_ref[...] += jnp.sum(accl[pl.ds(0, SUB), :], axis=0, keepdims=True)
        dbb_ref[...] += jnp.sum(accl[pl.ds(SUB, SUB), :], axis=0, keepdims=True)
        db_ref[...] += jnp.sum(accl[pl.ds(2 * SUB, SUB), :], axis=0, keepdims=True)

        accw[...] = jnp.zeros_like(accw)
        for c in range(CONV_W // LANES):
            cols = pl.ds(c * LANES, LANES)

            def chunk(r, carry):
                r0 = pl.multiple_of(r * RB, RB)
                d = cv_sc[pl.ds(r0, RB + HALO), cols]
                g = gl_sc[pl.ds(r0, RB + HALO), cols]
                _shifted(shd_sc, d, RB + HALO - SUB)
                _shifted(sh_sc, g, RB + HALO - SUB)
                d_t = d[:RB]
                dgl = jnp.zeros((RB, LANES), F32)
                for j in range(CONV_K):
                    dgl = dgl + w_ref[pl.ds(j, 1), cols] * _tap(shd_sc, cv_sc, r0, cols, CONV_K - 1 - j, RB)
                    prod = d_t * _tap(sh_sc, gl_sc, r0, cols, HALO - (CONV_K - 1) + j, RB)
                    accw[pl.ds(SUB * j, SUB), cols] += jnp.sum(prod.reshape(RB // SUB, SUB, LANES), axis=0)
                a_t = z_ref[pl.ds(r0, RB), pl.ds(POOL_W + c * LANES, LANES)]
                sb = _sigmoid(z_ref[pl.ds(r0, RB), pl.ds(POOL_W + CONV_W + c * LANES, LANES)])
                dz_ref[pl.ds(r0, RB), pl.ds(POOL_W + c * LANES, LANES)] = (dgl * sb).astype(dz_ref.dtype)
                dz_ref[pl.ds(r0, RB), pl.ds(POOL_W + CONV_W + c * LANES, LANES)] = (
                    dgl * a_t * sb * (1.0 - sb)).astype(dz_ref.dtype)
                return carry

            lax.fori_loop(0, tt // RB, chunk, 0)
        for j in range(CONV_K):
            dw_ref[pl.ds(j, 1), :] += jnp.sum(accw[pl.ds(SUB * j, SUB), :], axis=0, keepdims=True)

    full = lambda shape: pl.BlockSpec(shape, lambda i: (0,) * len(shape))
    nb = T // HALO
    outs = pl.pallas_call(
        body, name=name, grid=(n,),
        in_specs=[pl.BlockSpec((HALO, C), lambda i: (jnp.maximum(i * hb - 1, 0), 0)),
                  pl.BlockSpec((tt, C), lambda i: (i, 0)),
                  pl.BlockSpec((HALO, C), lambda i: (jnp.minimum((i + 1) * hb, nb - 1), 0)),
                  pl.BlockSpec((tt, 2 * POOL_W), lambda i: (i, 0)),
                  pl.BlockSpec((HALO, 2 * POOL_W), lambda i: (jnp.minimum((i + 1) * hb, nb - 1), 0)),
                  full((4, POOL_GROUP, POOL_GROUP)), full((1, POOL_W)), full((CONV_K + 1, CONV_W)),
                  full((1, CONV_W)), full((1, CONV_W)), full((1, CONV_W))],
        out_specs=[pl.BlockSpec((tt, C), lambda i: (i, 0)),
                   full((4, POOL_GROUP, POOL_GROUP)), full((1, POOL_W)), full((CONV_K + 1, CONV_W)),
                   full((1, CONV_W)), full((1, CONV_W)), full((1, CONV_W))],
        out_shape=[jax.ShapeDtypeStruct((T, C), CD),
                   jax.ShapeDtypeStruct((4, POOL_GROUP, POOL_GROUP), F32),
                   jax.ShapeDtypeStruct((1, POOL_W), F32),
                   jax.ShapeDtypeStruct((CONV_K + 1, CONV_W), F32),
                   jax.ShapeDtypeStruct((1, CONV_W), F32),
                   jax.ShapeDtypeStruct((1, CONV_W), F32),
                   jax.ShapeDtypeStruct((1, CONV_W), F32)],
        scratch_shapes=[pltpu.VMEM((tt, POOL_GROUP), CD), pltpu.VMEM((R, POOL_GROUP), CD),
                        pltpu.VMEM((R, POOL_GROUP), F32), pltpu.VMEM((R, POOL_GROUP), F32),
                        pltpu.VMEM((tt + 2 * HALO, CONV_W), F32), pltpu.VMEM((R, CONV_W), F32),
                        pltpu.VMEM((SUB * (CONV_K + 1), CONV_W), F32), pltpu.VMEM((3 * SUB, CONV_W), F32),
                        pltpu.VMEM((SUB, MIX_ROWS + HALO, LANES), F32),
                        pltpu.VMEM((SUB, MIX_ROWS + HALO, LANES), F32)],
        compiler_params=_params("arbitrary"),
    )(z, z, z, dy, dy, pool_w, pool_scale, dw_w, dw_b, ln_g, ln_b)
    return outs


CHUNK_HALO = 16
FFN_ROWS = 64
FFN_LANES = 128


def _rows(cur, prev, nxt, r, rb, before, after, cols, n_r, first, last):
    lo, hi = r * rb - before, r * rb + rb + after
    tt = n_r * rb
    parts = []
    if lo < 0:
        p = prev[pl.ds(HALO + lo, -lo), cols]
        parts.append(jnp.where(first, jnp.zeros_like(p), p))
        lo = 0
    parts.append(cur[pl.ds(lo, min(hi, tt) - lo), cols])
    if hi > tt:
        p = nxt[pl.ds(0, hi - tt), cols]
        parts.append(jnp.where(last, jnp.zeros_like(p), p))
    return parts[0] if len(parts) == 1 else jnp.concatenate(parts, axis=0)


def _ffn_mid_fwd(up, cw, cb, *, name, tt=512):
    T = up.shape[0]
    tt = min(tt, T)
    n = T // tt
    hb = tt // HALO
    RB, CW, HB = min(FFN_ROWS, tt), FFN_LANES, CHUNK_HALO
    n_r = tt // RB

    def body(a_ref, gp_ref, g_ref, w_ref, b_ref, o_ref):
        first = pl.program_id(0) == 0

        def col_chunk(c, carry):
            cols = pl.ds(pl.multiple_of(c * CW, CW), CW)
            w = w_ref[:, cols]
            b = b_ref[:, cols]
            for r in range(n_r):
                v = _rows(g_ref, gp_ref, None, r, RB, HB, 0, cols, n_r, first, None).astype(F32)
                gc = b + w[0:1] * v[HB - 2:HB - 2 + RB] + w[1:2] * v[HB - 1:HB - 1 + RB] + w[2:3] * v[HB:HB + RB]
                a = a_ref[pl.ds(r * RB, RB), cols].astype(F32)
                o_ref[pl.ds(r * RB, RB), cols] = (gc * _sigmoid(gc) * a).astype(o_ref.dtype)
            return carry

        lax.fori_loop(0, D_FF // CW, col_chunk, 0)

    return pl.pallas_call(
        body, name=name, grid=(n,),
        in_specs=[pl.BlockSpec((tt, D_FF), lambda i: (i, 0)),
                  pl.BlockSpec((HALO, D_FF), lambda i: (jnp.maximum(i * hb - 1, 0), 1)),
                  pl.BlockSpec((tt, D_FF), lambda i: (i, 1)),
                  pl.BlockSpec((8, D_FF), lambda i: (0, 0)),
                  pl.BlockSpec((1, D_FF), lambda i: (0, 0))],
        out_specs=pl.BlockSpec((tt, D_FF), lambda i: (i, 0)),
        out_shape=jax.ShapeDtypeStruct((T, D_FF), CD),
        compiler_params=_params("parallel"),
    )(up, up, up, cw, cb)


def _ffn_mid_bwd(up, dact, cw, cb, *, name, tt=512):
    T = up.shape[0]
    tt = min(tt, T)
    n = T // tt
    hb = tt // HALO
    nb = T // HALO
    RB, CW, HB = min(FFN_ROWS, tt), FFN_LANES, CHUNK_HALO
    n_r = tt // RB
    RE = RB + 8

    def body(a_ref, an_ref, gp_ref, g_ref, gn_ref, d_ref, dn_ref, w_ref, b_ref, dup_ref, dw_ref, db_ref, acc):
        i = pl.program_id(0)
        first, last = i == 0, i == n - 1

        @pl.when(first)
        def _():
            dw_ref[...] = jnp.zeros_like(dw_ref)
            db_ref[...] = jnp.zeros_like(db_ref)

        def col_chunk(c, carry):
            cols = pl.ds(pl.multiple_of(c * CW, CW), CW)
            w = w_ref[:, cols]
            b = b_ref[:, cols]
            part = [jnp.zeros((8, CW), F32) for _ in range(FFN_K + 1)]
            for r in range(n_r):
                v = _rows(g_ref, gp_ref, gn_ref, r, RB, HB, HB, cols, n_r, first, last).astype(F32)
                gs = [v[HB - 2 + j:HB - 2 + j + RE] for j in range(FFN_K)]
                gc = b + w[0:1] * gs[0] + w[1:2] * gs[1] + w[2:3] * gs[2]
                sg = _sigmoid(gc)
                d = _rows(d_ref, None, dn_ref, r, RB, 0, HB, cols, n_r, None, last).astype(F32)[:RE]
                a = _rows(a_ref, None, an_ref, r, RB, 0, HB, cols, n_r, None, last).astype(F32)[:RE]
                silu = gc * sg
                dgc = d * a * (sg + silu - silu * sg)
                dup_ref[pl.ds(r * RB, RB), cols] = (d[:RB] * silu[:RB]).astype(dup_ref.dtype)
                dg = w[2:3] * dgc[0:RB] + w[1:2] * dgc[1:RB + 1] + w[0:1] * dgc[2:RB + 2]
                dup_ref[pl.ds(r * RB, RB), pl.ds(pl.multiple_of(D_FF + c * CW, CW), CW)] = dg.astype(dup_ref.dtype)
                dgc_t = dgc[:RB]
                for j in range(FFN_K):
                    part[j] = part[j] + jnp.sum((dgc_t * gs[j][:RB]).reshape(RB // 8, 8, CW), axis=0)
                part[FFN_K] = part[FFN_K] + jnp.sum(dgc_t.reshape(RB // 8, 8, CW), axis=0)
            for j in range(FFN_K + 1):
                acc[pl.ds(8 * j, 8), cols] = part[j]
            return carry

        lax.fori_loop(0, D_FF // CW, col_chunk, 0)
        for j in range(FFN_K):
            dw_ref[pl.ds(j, 1), :] += jnp.sum(acc[pl.ds(8 * j, 8), :], axis=0, keepdims=True)
        db_ref[...] += jnp.sum(acc[pl.ds(8 * FFN_K, 8), :], axis=0, keepdims=True)

    nxt = lambda i: jnp.minimum((i + 1) * hb, nb - 1)
    return pl.pallas_call(
        body, name=name, grid=(n,),
        in_specs=[pl.BlockSpec((tt, D_FF), lambda i: (i, 0)),
                  pl.BlockSpec((HALO, D_FF), lambda i: (nxt(i), 0)),
                  pl.BlockSpec((HALO, D_FF), lambda i: (jnp.maximum(i * hb - 1, 0), 1)),
                  pl.BlockSpec((tt, D_FF), lambda i: (i, 1)),
                  pl.BlockSpec((HALO, D_FF), lambda i: (nxt(i), 1)),
                  pl.BlockSpec((tt, D_FF), lambda i: (i, 0)),
                  pl.BlockSpec((HALO, D_FF), lambda i: (nxt(i), 0)),
                  pl.BlockSpec((8, D_FF), lambda i: (0, 0)),
                  pl.BlockSpec((1, D_FF), lambda i: (0, 0))],
        out_specs=[pl.BlockSpec((tt, 2 * D_FF), lambda i: (i, 0)),
                   pl.BlockSpec((8, D_FF), lambda i: (0, 0)),
                   pl.BlockSpec((1, D_FF), lambda i: (0, 0))],
        out_shape=[jax.ShapeDtypeStruct((T, 2 * D_FF), CD),
                   jax.ShapeDtypeStruct((8, D_FF), F32),
                   jax.ShapeDtypeStruct((1, D_FF), F32)],
        scratch_shapes=[pltpu.VMEM((8 * (FFN_K + 1), D_FF), F32)],
        compiler_params=_params("arbitrary"),
    )(up, up, up, up, up, dact, dact, cw, cb)


def _xattn_probs(q, k):
    s = lax.dot_general(q, k, NT, preferred_element_type=F32) * XA_SCALE
    p = jnp.exp(s - jnp.max(s, axis=-1, keepdims=True))
    return p / jnp.sum(p, axis=-1, keepdims=True)


def _xattn_fwd(q, kv, *, name, tq=512):
    T = q.shape[0]
    tq = min(tq, T)

    def body(q_ref, kv_ref, o_ref):
        for h in range(XA_HEADS):
            cols = pl.ds(h * XA_DH, XA_DH)
            p = _xattn_probs(q_ref[:, cols], kv_ref[:, cols])
            v = kv_ref[:, pl.ds(D_MODEL + h * XA_DH, XA_DH)]
            o_ref[:, cols] = jnp.dot(p.astype(CD), v, preferred_element_type=F32).astype(o_ref.dtype)

    return pl.pallas_call(
        body, name=name, grid=(T // tq,),
        in_specs=[pl.BlockSpec((tq, D_MODEL), lambda i: (i, 0)),
                  pl.BlockSpec((MEM_LEN, 2 * D_MODEL), lambda i: (0, 0))],
        out_specs=pl.BlockSpec((tq, D_MODEL), lambda i: (i, 0)),
        out_shape=jax.ShapeDtypeStruct((T, D_MODEL), CD),
        compiler_params=_params("parallel"),
    )(q, kv)


def _xattn_bwd(q, kv, do, *, name, tq=512):
    T = q.shape[0]
    tq = min(tq, T)

    def body(q_ref, kv_ref, do_ref, dq_ref, dkv_ref):
        @pl.when(pl.program_id(0) == 0)
        def _():
            dkv_ref[...] = jnp.zeros_like(dkv_ref)

        for h in range(XA_HEADS):
            cols = pl.ds(h * XA_DH, XA_DH)
            vcols = pl.ds(D_MODEL + h * XA_DH, XA_DH)
            qh, kh, vh, doh = q_ref[:, cols], kv_ref[:, cols], kv_ref[:, vcols], do_ref[:, cols]
            p = _xattn_probs(qh, kh)
            dkv_ref[:, vcols] += lax.dot_general(p.astype(CD), doh, TN, preferred_element_type=F32)
            dp = lax.dot_general(doh, vh, NT, preferred_element_type=F32)
            ds = (p * (dp - jnp.sum(dp * p, axis=-1, keepdims=True)) * XA_SCALE).astype(CD)
            dq_ref[:, cols] = jnp.dot(ds, kh, preferred_element_type=F32).astype(dq_ref.dtype)
            dkv_ref[:, cols] += lax.dot_general(ds, qh, TN, preferred_element_type=F32)

    return pl.pallas_call(
        body, name=name, grid=(T // tq,),
        in_specs=[pl.BlockSpec((tq, D_MODEL), lambda i: (i, 0)),
                  pl.BlockSpec((MEM_LEN, 2 * D_MODEL), lambda i: (0, 0)),
                  pl.BlockSpec((tq, D_MODEL), lambda i: (i, 0))],
        out_specs=[pl.BlockSpec((tq, D_MODEL), lambda i: (i, 0)),
                   pl.BlockSpec((MEM_LEN, 2 * D_MODEL), lambda i: (0, 0))],
        out_shape=[jax.ShapeDtypeStruct((T, D_MODEL), CD),
                   jax.ShapeDtypeStruct((MEM_LEN, 2 * D_MODEL), F32)],
        compiler_params=_params("arbitrary"),
    )(q, kv, do)


C_W = Q_LORA + KV_LORA + LANES


def _rot(x):
    lane = lax.broadcasted_iota(jnp.int32, x.shape, x.ndim - 1)
    up = pltpu.roll(x, LANES - QK_ROPE // 2, x.ndim - 1)
    dn = pltpu.roll(x, QK_ROPE // 2, x.ndim - 1)
    lo, mid, hi = QK_NOPE, QK_NOPE + QK_ROPE // 2, QK_NOPE + QK_ROPE
    return jnp.where((lane >= lo) & (lane < mid), -up, jnp.where((lane >= mid) & (lane < hi), dn, 0.0))


def _mla_mid_fwd(c, qg, kvg, cs, sn, *, name, tt=512):
    T = c.shape[0]
    tt = min(tt, T)

    def body(c_ref, qg_ref, kg_ref, cs_ref, sn_ref, qn_ref, kn_ref, kpe_ref):
        cq = c_ref[:, pl.ds(0, Q_LORA)]
        qn_ref[...] = (cq * lax.rsqrt(jnp.mean(cq * cq, axis=-1, keepdims=True) + EPS)
                       * qg_ref[...]).astype(qn_ref.dtype)
        ck = c_ref[:, pl.ds(Q_LORA, KV_LORA)]
        kn_ref[...] = (ck * lax.rsqrt(jnp.mean(ck * ck, axis=-1, keepdims=True) + EPS)
                       * kg_ref[...]).astype(kn_ref.dtype)
        kp = c_ref[:, pl.ds(Q_LORA + KV_LORA, LANES)]
        kpe_ref[...] = kp * cs_ref[...] + _rot(kp) * sn_ref[...]

    row = lambda w: pl.BlockSpec((tt, w), lambda i: (i, 0))
    one = lambda w: pl.BlockSpec((1, w), lambda i: (0, 0))
    return pl.pallas_call(
        body, name=name, grid=(T // tt,),
        in_specs=[row(C_W), one(Q_LORA), one(KV_LORA), row(LANES), row(LANES)],
        out_specs=[row(Q_LORA), row(KV_LORA), row(LANES)],
        out_shape=[jax.ShapeDtypeStruct((T, Q_LORA), CD), jax.ShapeDtypeStruct((T, KV_LORA), CD),
                   jax.ShapeDtypeStruct((T, LANES), F32)],
        compiler_params=_params("parallel"),
    )(c, qg, kvg, cs, sn)


def _mla_mid_bwd(c, dqn, dkvn, dksum, qg, kvg, cs, sn, *, name, tt=512):
    T = c.shape[0]
    tt = min(tt, T)

    def body(c_ref, dq_ref, dk_ref, ds_ref, qg_ref, kg_ref, cs_ref, sn_ref, dc_ref, dqg_ref, dkg_ref):
        @pl.when(pl.program_id(0) == 0)
        def _():
            dqg_ref[...] = jnp.zeros_like(dqg_ref)
            dkg_ref[...] = jnp.zeros_like(dkg_ref)

        dx, dg = _rms_bwd(c_ref[:, pl.ds(0, Q_LORA)], qg_ref[...], dq_ref[...])
        dc_ref[:, pl.ds(0, Q_LORA)] = dx.astype(dc_ref.dtype)
        dqg_ref[...] += jnp.sum(dg, axis=0, keepdims=True)
        dx, dg = _rms_bwd(c_ref[:, pl.ds(Q_LORA, KV_LORA)], kg_ref[...], dk_ref[...])
        dc_ref[:, pl.ds(Q_LORA, KV_LORA)] = dx.astype(dc_ref.dtype)
        dkg_ref[...] += jnp.sum(dg, axis=0, keepdims=True)
        d = ds_ref[...]
        lane = lax.broadcasted_iota(jnp.int32, d.shape, 1)
        dkp = d * cs_ref[...] - _rot(d * sn_ref[...])
        dc_ref[:, pl.ds(Q_LORA + KV_LORA, LANES)] = jnp.where(
            (lane >= QK_NOPE) & (lane < QK_NOPE + QK_ROPE), dkp, 0.0).astype(dc_ref.dtype)

    row = lambda w: pl.BlockSpec((tt, w), lambda i: (i, 0))
    one = lambda w: pl.BlockSpec((1, w), lambda i: (0, 0))
    return pl.pallas_call(
        body, name=name, grid=(T // tt,),
        in_specs=[row(C_W), row(Q_LORA), row(KV_LORA), row(LANES), one(Q_LORA), one(KV_LORA),
                  row(LANES), row(LANES)],
        out_specs=[row(C_W), one(Q_LORA), one(KV_LORA)],
        out_shape=[jax.ShapeDtypeStruct((T, C_W), CD), jax.ShapeDtypeStruct((1, Q_LORA), F32),
                   jax.ShapeDtypeStruct((1, KV_LORA), F32)],
        compiler_params=_params("arbitrary"),
    )(c, dqn, dkvn, dksum, qg, kvg, cs, sn)


def _mla_qkv_fwd(qn, kvn, kpe, cs, sn, wq, wk, wv, *, name, tt=256):
    T = qn.shape[0]
    tt = min(tt, T)
    H = MLA_HEADS

    def body(qn_ref, kn_ref, kpe_ref, cs_ref, sn_ref, wq_ref, wk_ref, wv_ref, q_ref, k_ref, v_ref):
        qn_v, kn_v, kpe_v, cs_v, sn_v = qn_ref[...], kn_ref[...], kpe_ref[...], cs_ref[...], sn_ref[...]
        for h in range(H):
            q = jnp.dot(qn_v, wq_ref[h], preferred_element_type=F32)
            q_ref[h] = (q * cs_v + _rot(q) * sn_v).astype(q_ref.dtype)
            k_ref[h] = (jnp.dot(kn_v, wk_ref[h], preferred_element_type=F32) + kpe_v).astype(k_ref.dtype)
            v_ref[h] = jnp.dot(kn_v, wv_ref[h], preferred_element_type=F32).astype(v_ref.dtype)

    row = lambda w: pl.BlockSpec((tt, w), lambda i: (i, 0))
    wsp = lambda k: pl.BlockSpec((H, k, LANES), lambda i: (0, 0, 0))
    hsp = pl.BlockSpec((H, tt, LANES), lambda i: (0, i, 0))
    sh = jax.ShapeDtypeStruct((H, T, LANES), CD)
    return pl.pallas_call(
        body, name=name, grid=(T // tt,),
        in_specs=[row(Q_LORA), row(KV_LORA), row(LANES), row(LANES), row(LANES),
                  wsp(Q_LORA), wsp(KV_LORA), wsp(KV_LORA)],
        out_specs=[hsp, hsp, hsp], out_shape=[sh, sh, sh],
        compiler_params=_params("parallel"),
    )(qn, kvn, kpe, cs, sn, wq, wk, wv)


def _mla_qkv_bwd(dq, dk, dv, qn, kvn, cs, sn, wq, wk, wv, *, name, tt=256):
    T = qn.shape[0]
    tt = min(tt, T)
    H = MLA_HEADS

    def body(dq_ref, dk_ref, dv_ref, qn_ref, kn_ref, cs_ref, sn_ref, wq_ref, wk_ref, wv_ref,
             dqn_ref, dkn_ref, dks_ref, dwq_ref, dwk_ref, dwv_ref):
        @pl.when(pl.program_id(0) == 0)
        def _():
            for r in (dwq_ref, dwk_ref, dwv_ref):
                r[...] = jnp.zeros_like(r)

        qn_v, kn_v, cs_v, sn_v = qn_ref[...], kn_ref[...], cs_ref[...], sn_ref[...]
        dqn = jnp.zeros((tt, Q_LORA), F32)
        dkn = jnp.zeros((tt, KV_LORA), F32)
        dks = jnp.zeros((tt, LANES), F32)
        for h in range(H):
            d = dq_ref[h]
            dqh = (d * cs_v - _rot(d * sn_v)).astype(CD)
            dkh, dvh = dk_ref[h], dv_ref[h]
            dqn = dqn + lax.dot_general(dqh, wq_ref[h], NT, preferred_element_type=F32)
            dkn = dkn + lax.dot_general(dkh, wk_ref[h], NT, preferred_element_type=F32)
            dkn = dkn + lax.dot_general(dvh, wv_ref[h], NT, preferred_element_type=F32)
            dks = dks + dkh.astype(F32)
            dwq_ref[h] += lax.dot_general(qn_v, dqh, TN, preferred_element_type=F32)
            dwk_ref[h] += lax.dot_general(kn_v, dkh, TN, preferred_element_type=F32)
            dwv_ref[h] += lax.dot_general(kn_v, dvh, TN, preferred_element_type=F32)
        dqn_ref[...] = dqn
        dkn_ref[...] = dkn
        dks_ref[...] = dks

    row = lambda w: pl.BlockSpec((tt, w), lambda i: (i, 0))
    wsp = lambda k: pl.BlockSpec((H, k, LANES), lambda i: (0, 0, 0))
    hsp = pl.BlockSpec((H, tt, LANES), lambda i: (0, i, 0))
    return pl.pallas_call(
        body, name=name, grid=(T // tt,),
        in_specs=[hsp, hsp, hsp, row(Q_LORA), row(KV_LORA), row(LANES), row(LANES),
                  wsp(Q_LORA), wsp(KV_LORA), wsp(KV_LORA)],
        out_specs=[row(Q_LORA), row(KV_LORA), row(LANES), wsp(Q_LORA), wsp(KV_LORA), wsp(KV_LORA)],
        out_shape=[jax.ShapeDtypeStruct((T, Q_LORA), F32), jax.ShapeDtypeStruct((T, KV_LORA), F32),
                   jax.ShapeDtypeStruct((T, LANES), F32),
                   jax.ShapeDtypeStruct((H, Q_LORA, LANES), F32),
                   jax.ShapeDtypeStruct((H, KV_LORA, LANES), F32),
                   jax.ShapeDtypeStruct((H, KV_LORA, LANES), F32)],
        compiler_params=_params("arbitrary"),
    )(dq, dk, dv, qn, kvn, cs, sn, wq, wk, wv)


FLASH_BLOCK = 1024
EXP2_SCALE = MLA_SCALE * math.log2(math.e)


def _causal_steps(nq, by_key):
    pairs = [(i, j) for j in range(nq) for i in range(j, nq)] if by_key else \
            [(i, j) for i in range(nq) for j in range(i + 1)]
    return (jnp.asarray([p[0] for p in pairs], jnp.int32), jnp.asarray([p[1] for p in pairs], jnp.int32))


def _raw_scores(q, k, masked, first_row=0):
    s = lax.dot_general(q, k, NT, preferred_element_type=F32)
    if masked:
        row = lax.broadcasted_iota(jnp.int32, s.shape, 0) + first_row
        col = lax.broadcasted_iota(jnp.int32, s.shape, 1)
        s = jnp.where(col <= row, s, NEG)
    return s


def _flash_fwd(q, k, v, *, name):
    H, T, _ = q.shape
    tq = min(FLASH_BLOCK, T)
    nq = T // tq
    i_tab, j_tab = _causal_steps(nq, by_key=False)

    rb = min(128, tq)

    def body(i_tab, j_tab, q_ref, k_ref, v_ref, o_ref, lse_ref, m_sc, l_sc, acc, s_sc, p_sc):
        t = pl.program_id(1)
        i, j = i_tab[t], j_tab[t]

        @pl.when(j == 0)
        def _():
            m_sc[...] = jnp.full_like(m_sc, NEG)
            l_sc[...] = jnp.zeros_like(l_sc)
            acc[...] = jnp.zeros_like(acc)

        hb = tq // 2

        def step(masked):
            lane = lax.broadcasted_iota(jnp.int32, (tq, LANES), 1)
            top, bot = pl.ds(0, hb), pl.ds(hb, hb)
            alphas, pvs = [], []
            for h in range(2):
                if masked:
                    s_sc[h, top, top] = _raw_scores(q_ref[h, top, :], k_ref[h, top, :], True)
                    s_sc[h, bot, :] = _raw_scores(q_ref[h, bot, :], k_ref[h], True, first_row=hb)
                    m_cur = jnp.concatenate([jnp.max(s_sc[h, top, top], axis=-1, keepdims=True),
                                             jnp.max(s_sc[h, bot, :], axis=-1, keepdims=True)], axis=0)
                else:
                    s_sc[h] = _raw_scores(q_ref[h], k_ref[h], False)
                    m_cur = jnp.max(s_sc[h], axis=-1, keepdims=True)
                m_prev = m_sc[h]
                m_new = jnp.maximum(m_prev, m_cur)
                alpha = jnp.exp2((m_prev - m_new) * EXP2_SCALE)
                m_sc[h] = m_new
                for r in range(tq // rb):
                    rows = pl.ds(r * rb, rb)
                    m_r = m_sc[h, rows, :]
                    part = jnp.zeros((rb, LANES), F32)
                    keys = hb if masked and r * rb < hb else tq
                    for c in range(keys // LANES):
                        cols = pl.ds(c * LANES, LANES)
                        p = jnp.exp2((s_sc[h, rows, cols] - m_r) * EXP2_SCALE)
                        part = part + p
                        p_sc[h, rows, cols] = p.astype(CD)
                    l_sc[h, rows, :] = (alpha[r * rb:(r + 1) * rb] * l_sc[h, rows, :]
                                        + jnp.sum(part, axis=-1, keepdims=True))
                alphas.append(alpha)
                if masked:
                    pvs.append(jnp.concatenate(
                        [jnp.dot(p_sc[h, top, top], v_ref[h, top, :], preferred_element_type=F32),
                         jnp.dot(p_sc[h, bot, :], v_ref[h], preferred_element_type=F32)], axis=0))
                else:
                    pvs.append(jnp.dot(p_sc[h], v_ref[h], preferred_element_type=F32))
            acc[...] = acc[...] * jnp.where(lane < V_HEAD, alphas[0], alphas[1]) + pvs[0] + pvs[1]

        @pl.when(j < i)
        def _():
            step(False)

        @pl.when(j == i)
        def _():
            step(True)
            lane = lax.broadcasted_iota(jnp.int32, (tq, LANES), 1)
            o_ref[...] = (acc[...] / jnp.where(lane < V_HEAD, l_sc[0], l_sc[1])).astype(o_ref.dtype)
            for h in range(2):
                lse_ref[h] = m_sc[h] * EXP2_SCALE + jnp.log2(l_sc[h])

    qsp = pl.BlockSpec((2, tq, LANES), lambda p, t, it, jt: (p, it[t], 0))
    ksp = pl.BlockSpec((2, tq, LANES), lambda p, t, it, jt: (p, jt[t], 0))
    return pl.pallas_call(
        body, name=name,
        grid_spec=pltpu.PrefetchScalarGridSpec(
            num_scalar_prefetch=2, grid=(H // 2, int(i_tab.shape[0])),
            in_specs=[qsp, ksp, ksp],
            out_specs=[pl.BlockSpec((tq, LANES), lambda p, t, it, jt: (it[t], p)), qsp],
            scratch_shapes=[pltpu.VMEM((2, tq, LANES), F32), pltpu.VMEM((2, tq, LANES), F32),
                            pltpu.VMEM((tq, LANES), F32),
                            pltpu.VMEM((2, tq, tq), F32), pltpu.VMEM((2, tq, tq), CD)]),
        out_shape=[jax.ShapeDtypeStruct((T, H * V_HEAD), CD), jax.ShapeDtypeStruct((H, T, LANES), F32)],
        compiler_params=_params("parallel", "arbitrary"),
    )(i_tab, j_tab, q, k, v)


def _flash_delta(o, do, *, name, tt=512):
    T = o.shape[0]
    tt = min(tt, T)
    H = MLA_HEADS

    def body(o_ref, do_ref, dl_ref):
        lane = lax.broadcasted_iota(jnp.int32, (tt, LANES), 1)
        for p in range(H // 2):
            cols = pl.ds(p * LANES, LANES)
            prod = do_ref[:, cols].astype(F32) * o_ref[:, cols].astype(F32)
            d0 = jnp.sum(jnp.where(lane < V_HEAD, prod, 0.0), axis=-1, keepdims=True)
            d1 = jnp.sum(jnp.where(lane < V_HEAD, 0.0, prod), axis=-1, keepdims=True)
            dl_ref[2 * p] = jnp.broadcast_to(d0, (tt, LANES))
            dl_ref[2 * p + 1] = jnp.broadcast_to(d1, (tt, LANES))

    row = pl.BlockSpec((tt, H * V_HEAD), lambda i: (i, 0))
    return pl.pallas_call(
        body, name=name, grid=(T // tt,), in_specs=[row, row],
        out_specs=pl.BlockSpec((H, tt, LANES), lambda i: (0, i, 0)),
        out_shape=jax.ShapeDtypeStruct((H, T, LANES), F32),
        compiler_params=_params("parallel"),
    )(o, do)


def _flash_bwd(q, k, v, do, lse, delta, *, name):
    H, T, _ = q.shape
    tq = min(FLASH_BLOCK, T)
    nq = T // tq
    i_tab, j_tab = _causal_steps(nq, by_key=True)

    def body(i_tab, j_tab, q_ref, k_ref, v_ref, do_ref, lse_ref, dl_ref, dq_ref, dk_ref, dv_ref, dk_acc, dv_acc):
        t = pl.program_id(1)
        i, j = i_tab[t], j_tab[t]
        rows = pl.ds(pl.multiple_of(i * tq, tq), tq)

        @pl.when(t == 0)
        def _():
            dq_ref[...] = jnp.zeros_like(dq_ref)

        def block(h, qr, kr, first_row, masked):
            qh, kh, vh, do_v = q_ref[h, qr, :], k_ref[h, kr, :], v_ref[h, kr, :], do_ref[qr, :]
            s = _raw_scores(qh, kh, masked, first_row)
            p = jnp.exp2(s * EXP2_SCALE - lse_ref[h, qr, :][:, :1])
            dv_acc[h, kr, :] += lax.dot_general(p.astype(CD), do_v, TN, preferred_element_type=F32)
            dp = lax.dot_general(do_v, vh, NT, preferred_element_type=F32)
            ds = (p * (dp - dl_ref[h, qr, :][:, :1]) * MLA_SCALE).astype(CD)
            dk_acc[h, kr, :] += lax.dot_general(ds, qh, TN, preferred_element_type=F32)
            dq_rows = pl.ds(pl.multiple_of(i * tq + qr.start, qr.size), qr.size)
            dq_ref[h, dq_rows, :] += jnp.dot(ds, kh, preferred_element_type=F32)

        def step(masked):
            hb = tq // 2
            for h in range(2):
                if masked:
                    block(h, pl.ds(0, hb), pl.ds(0, hb), 0, True)
                    block(h, pl.ds(hb, hb), pl.ds(0, tq), hb, True)
                else:
                    block(h, pl.ds(0, tq), pl.ds(0, tq), 0, False)

        @pl.when(i == j)
        def _():
            dk_acc[...] = jnp.zeros_like(dk_acc)
            dv_acc[...] = jnp.zeros_like(dv_acc)
            step(True)

        @pl.when(i > j)
        def _():
            step(False)

        @pl.when(i == nq - 1)
        def _():
            lane = lax.broadcasted_iota(jnp.int32, (tq, LANES), 1)
            dk_ref[...] = dk_acc[...].astype(dk_ref.dtype)
            dv_ref[0] = jnp.where(lane < V_HEAD, dv_acc[0], 0.0).astype(dv_ref.dtype)
            dv_ref[1] = jnp.where(lane < V_HEAD, 0.0, dv_acc[1]).astype(dv_ref.dtype)

    qsp = pl.BlockSpec((2, tq, LANES), lambda p, t, it, jt: (p, it[t], 0))
    ksp = pl.BlockSpec((2, tq, LANES), lambda p, t, it, jt: (p, jt[t], 0))
    osp = pl.BlockSpec((tq, LANES), lambda p, t, it, jt: (it[t], p))
    sh = jax.ShapeDtypeStruct((H, T, LANES), CD)
    return pl.pallas_call(
        body, name=name,
        grid_spec=pltpu.PrefetchScalarGridSpec(
            num_scalar_prefetch=2, grid=(H // 2, int(i_tab.shape[0])),
            in_specs=[qsp, ksp, ksp, osp, qsp, qsp],
            out_specs=[pl.BlockSpec((2, T, LANES), lambda p, t, it, jt: (p, 0, 0)), ksp, ksp],
            scratch_shapes=[pltpu.VMEM((2, tq, LANES), F32), pltpu.VMEM((2, tq, LANES), F32)]),
        out_shape=[jax.ShapeDtypeStruct((H, T, LANES), F32), sh, sh],
        compiler_params=_params("parallel", "arbitrary"),
    )(i_tab, j_tab, q, k, v, do, lse, delta)


def _loss_head(x, g, target, *, name, tt=512):
    T, D = x.shape
    tt = min(tt, T)

    def body(x_ref, g_ref, t_ref, dx_ref, dg_ref, loss_ref):
        @pl.when(pl.program_id(0) == 0)
        def _():
            dg_ref[...] = jnp.zeros_like(dg_ref)
            loss_ref[...] = jnp.zeros_like(loss_ref)

        xv, gv = x_ref[...], g_ref[...]
        r = lax.rsqrt(jnp.mean(xv * xv, axis=-1, keepdims=True) + EPS)
        err = xv * r * gv - t_ref[...]
        tok = jnp.mean(err * err, axis=-1, keepdims=True)
        loss_ref[...] += 0.5 * jnp.sum(tok, axis=0, keepdims=True)
        dx, dg_rows = _rms_bwd(xv, gv, err * (1.0 / D))
        dx_ref[...] = dx
        dg_ref[...] += jnp.sum(dg_rows, axis=0, keepdims=True)

    return pl.pallas_call(
        body, name=name, grid=(T // tt,),
        in_specs=[pl.BlockSpec((tt, D), lambda i: (i, 0)), pl.BlockSpec((1, D), lambda i: (0, 0)),
                  pl.BlockSpec((tt, D), lambda i: (i, 0))],
        out_specs=[pl.BlockSpec((tt, D), lambda i: (i, 0)), pl.BlockSpec((1, D), lambda i: (0, 0)),
                   pl.BlockSpec((1, LANES), lambda i: (0, 0))],
        out_shape=[jax.ShapeDtypeStruct((T, D), F32), jax.ShapeDtypeStruct((1, D), F32),
                   jax.ShapeDtypeStruct((1, LANES), F32)],
        compiler_params=_params("arbitrary"),
    )(x, g, target)


def _rope_tables(positions):
    inv = 1.0 / (ROPE_THETA ** (jnp.arange(0, QK_ROPE, 2, dtype=F32) / QK_ROPE))
    ang = positions.astype(F32)[:, None] * inv
    c, s = jnp.cos(ang), jnp.sin(ang)
    T = positions.shape[0]
    cs = jnp.concatenate([jnp.ones((T, QK_NOPE), F32), c, c, jnp.zeros((T, LANES - QK_NOPE - QK_ROPE), F32)], 1)
    sn = jnp.concatenate([jnp.zeros((T, QK_NOPE), F32), s, s, jnp.zeros((T, LANES - QK_NOPE - QK_ROPE), F32)], 1)
    return cs, sn


def _pad_rows(w, rows):
    return jnp.concatenate([w, jnp.zeros((rows - w.shape[0],) + w.shape[1:], w.dtype)], 0)


def _mla_weights(w_dq_dkv, w_uq, w_ukv):
    K = w_dq_dkv.shape[0]
    z = lambda n: jnp.zeros((K, n), w_dq_dkv.dtype)
    wc = jnp.concatenate([w_dq_dkv[:, :Q_LORA + KV_LORA], z(QK_NOPE), w_dq_dkv[:, Q_LORA + KV_LORA:],
                          z(LANES - QK_NOPE - QK_ROPE)], 1)
    wq = w_uq.reshape(Q_LORA, MLA_HEADS, QK_NOPE + QK_ROPE).transpose(1, 0, 2)
    wq = jnp.concatenate([wq, jnp.zeros((MLA_HEADS, Q_LORA, LANES - QK_NOPE - QK_ROPE), wq.dtype)], 2)
    wkv = w_ukv.reshape(KV_LORA, MLA_HEADS, QK_NOPE + V_HEAD).transpose(1, 0, 2)
    zero = jnp.zeros_like(wkv[:, :, :QK_NOPE])
    wk = jnp.concatenate([wkv[:, :, :QK_NOPE], zero], 2)
    wv_lo = jnp.concatenate([wkv[:, :, QK_NOPE:], zero], 2)
    wv_hi = jnp.concatenate([zero, wkv[:, :, QK_NOPE:]], 2)
    odd = (jnp.arange(MLA_HEADS) % 2 == 1)[:, None, None]
    wv = jnp.where(odd, wv_hi, wv_lo)
    return wc, wq, wk, wv


def _mla_weight_grads(dwc, dwq, dwk, dwv):
    d_dq = jnp.concatenate([dwc[:, :Q_LORA + KV_LORA],
                            dwc[:, Q_LORA + KV_LORA + QK_NOPE:Q_LORA + KV_LORA + QK_NOPE + QK_ROPE]], 1)
    d_uq = dwq[:, :, :QK_NOPE + QK_ROPE].transpose(1, 0, 2).reshape(Q_LORA, MLA_HEADS * (QK_NOPE + QK_ROPE))
    odd = (jnp.arange(MLA_HEADS) % 2 == 1)[:, None, None]
    dv = jnp.where(odd, dwv[:, :, V_HEAD:], dwv[:, :, :V_HEAD])
    d_ukv = jnp.concatenate([dwk[:, :, :QK_NOPE], dv], 2).transpose(1, 0, 2).reshape(
        KV_LORA, MLA_HEADS * (QK_NOPE + V_HEAD))
    return d_dq, d_uq, d_ukv


def _local_step(x, mem, positions, target, W):
    G = {}
    row = lambda v: v.reshape(1, -1)
    cs, sn = _rope_tables(positions)
    saved = []
    for l in range(DEPTH):
        L = f"l{l}"
        s = {"x0": x}
        if l % 2 == 0:
            e = l // 2
            s["z"], s["h"] = _nmm(x, row(W["norm_mix_g"][l]), (W["pc_w_in"], e), name=f"{L}_mix_in", out_dtype=F32)
            s["dw_w"] = _pad_rows(W["conv_dw_w"][e], CONV_K + 1)
            s["mix_p"] = (W["pool_w"][e], row(W["pool_scale"][e]), s["dw_w"], row(W["conv_dw_b"][e]),
                          row(W["conv_ln_g"][e]), row(W["conv_ln_b"][e]))
            s["ycat"] = _mixer_fwd(s["z"], *s["mix_p"], name=f"{L}_mix_mid")
            x = _mm_res(s["ycat"], (W["pc_w_out"], e), x, name=f"{L}_mix_out")
        else:
            o = l // 2
            wc, wq, wk, wv = _mla_weights(W["mla_w_dq_dkv"][o], W["mla_w_uq"][o], W["mla_w_ukv"][o])
            s["mla_w"] = (wc, wq, wk, wv)
            s["c"], s["h"] = _nmm(x, row(W["norm_mix_g"][l]), wc, name=f"{L}_mla_down", out_dtype=F32)
            s["qg"], s["kvg"] = row(W["mla_q_norm_g"][o]), row(W["mla_kv_norm_g"][o])
            s["qn"], s["kvn"], kpe = _mla_mid_fwd(s["c"], s["qg"], s["kvg"], cs, sn, name=f"{L}_mla_mid")
            s["q"], s["k"], s["v"] = _mla_qkv_fwd(s["qn"], s["kvn"], kpe, cs, sn, wq, wk, wv, name=f"{L}_mla_qkv")
            s["o"], s["lse"] = _flash_fwd(s["q"], s["k"], s["v"], name=f"{L}_mla_attn")
            x = _mm_res(s["o"], (W["mla_w_o"], o), x, name=f"{L}_mla_out")
        s["x1"] = x
        s["xq"], s["hx"] = _nmm(x, row(W["norm_xa_g"][l]), (W["xa_wq"], l), name=f"{L}_xa_q", out_dtype=CD)
        s["xkv"], s["hm"] = _nmm(mem, row(W["norm_mem_g"][l]), (W["xa_wkv"], l), name=f"{L}_xa_kv", out_dtype=CD)
        s["xo"] = _xattn_fwd(s["xq"], s["xkv"], name=f"{L}_xa_attn")
        x = _mm_res(s["xo"], (W["xa_wo"], l), x, name=f"{L}_xa_out")
        s["x2"] = x
        s["up"], s["hf"] = _nmm(x, row(W["norm_ffn_g"][l]), (W["ffn_w_up"], l), name=f"{L}_ffn_up", out_dtype=CD,
                                tn_target=1408)
        s["cw"], s["cb"] = _pad_rows(W["ffn_conv_w"][l], 8), row(W["ffn_conv_b"][l])
        s["act"] = _ffn_mid_fwd(s["up"], s["cw"], s["cb"], name=f"{L}_ffn_mid")
        x = _mm_res(s["act"], (W["ffn_w_down"], l), x, name=f"{L}_ffn_down")
        saved.append(s)
    dx, G["final_norm_g"], loss = _loss_head(x, row(W["final_norm_g"]), target, name="loss_head")
    G["final_norm_g"] = G["final_norm_g"].reshape(-1)

    per_layer = {}

    def put(name, l, val):
        per_layer.setdefault(name, {})[l] = val

    for l in reversed(range(DEPTH)):
        L = f"l{l}"
        s = saved[l]
        put("ffn_w_down", l, _mm_tn(s["act"], dx, name=f"{L}_ffn_down_dw", tk_target=1408))
        dact = _mm_nt(dx, (W["ffn_w_down"], l), name=f"{L}_ffn_down_dx", out_dtype=CD, tn_target=1408)
        dup, dcw, dcb = _ffn_mid_bwd(s["up"], dact, s["cw"], s["cb"], name=f"{L}_ffn_mid_bwd")
        put("ffn_conv_w", l, dcw[:FFN_K])
        put("ffn_conv_b", l, dcb[0])
        put("ffn_w_up", l, _mm_tn(s["hf"], dup, name=f"{L}_ffn_up_dw", tn_target=1408))
        dx, dg = _mm_nt_normbwd(dup, (W["ffn_w_up"], l), s["x2"], row(W["norm_ffn_g"][l]), dx, name=f"{L}_ffn_up_dx")
        put("norm_ffn_g", l, dg[0])
        put("xa_wo", l, _mm_tn(s["xo"], dx, name=f"{L}_xa_out_dw"))
        do = _mm_nt(dx, (W["xa_wo"], l), name=f"{L}_xa_out_dx", out_dtype=CD)
        dq, dkv = _xattn_bwd(s["xq"], s["xkv"], do, name=f"{L}_xa_attn_bwd")
        put("xa_wq", l, _mm_tn(s["hx"], dq, name=f"{L}_xa_q_dw"))
        dx, dg = _mm_nt_normbwd(dq, (W["xa_wq"], l), s["x1"], row(W["norm_xa_g"][l]), dx, name=f"{L}_xa_q_dx")
        put("norm_xa_g", l, dg[0])
        put("xa_wkv", l, _mm_tn(s["hm"], dkv, name=f"{L}_xa_kv_dw", tt=MEM_LEN))
        _, dg = _mm_nt_normbwd(dkv, (W["xa_wkv"], l), mem, row(W["norm_mem_g"][l]), jnp.zeros_like(mem),
                               name=f"{L}_xa_kv_dx", tm=MEM_LEN)
        put("norm_mem_g", l, dg[0])
        if l % 2 == 0:
            e = l // 2
            put("pc_w_out", e, _mm_tn(s["ycat"], dx, name=f"{L}_mix_out_dw"))
            dy = _mm_nt(dx, (W["pc_w_out"], e), name=f"{L}_mix_out_dx", out_dtype=F32)
            dz, dpw, dps, ddw, ddb, dlg, dlb = _mixer_bwd(s["z"], dy, *s["mix_p"], name=f"{L}_mix_mid_bwd")
            put("pool_w", e, dpw)
            put("pool_scale", e, dps[0])
            put("conv_dw_w", e, ddw[:CONV_K])
            put("conv_dw_b", e, ddb[0])
            put("conv_ln_g", e, dlg[0])
            put("conv_ln_b", e, dlb[0])
            put("pc_w_in", e, _mm_tn(s["h"], dz, name=f"{L}_mix_in_dw"))
            dx, dg = _mm_nt_normbwd(dz, (W["pc_w_in"], e), s["x0"], row(W["norm_mix_g"][l]), dx, name=f"{L}_mix_in_dx")
        else:
            o = l // 2
            wc, wq, wk, wv = s["mla_w"]
            put("mla_w_o", o, _mm_tn(s["o"], dx, name=f"{L}_mla_out_dw"))
            do = _mm_nt(dx, (W["mla_w_o"], o), name=f"{L}_mla_out_dx", out_dtype=CD)
            delta = _flash_delta(s["o"], do, name=f"{L}_mla_attn_delta")
            dq, dk, dv = _flash_bwd(s["q"], s["k"], s["v"], do, s["lse"], delta, name=f"{L}_mla_attn_bwd")
            dqn, dkvn, dks, dwq, dwk, dwv = _mla_qkv_bwd(dq, dk, dv, s["qn"], s["kvn"], cs, sn, wq, wk, wv,
                                                         name=f"{L}_mla_qkv_bwd")
            dc, dqg, dkg = _mla_mid_bwd(s["c"], dqn, dkvn, dks, s["qg"], s["kvg"], cs, sn, name=f"{L}_mla_mid_bwd")
            put("mla_q_norm_g", o, dqg[0])
            put("mla_kv_norm_g", o, dkg[0])
            dwc = _mm_tn(s["h"], dc, name=f"{L}_mla_down_dw")
            d_dq, d_uq, d_ukv = _mla_weight_grads(dwc, dwq, dwk, dwv)
            put("mla_w_dq_dkv", o, d_dq)
            put("mla_w_uq", o, d_uq)
            put("mla_w_ukv", o, d_ukv)
            dx, dg = _mm_nt_normbwd(dc, wc, s["x0"], row(W["norm_mix_g"][l]), dx, name=f"{L}_mla_down_dx",
                                    tk_target=768)
        put("norm_mix_g", l, dg[0])
    for name, d in per_layer.items():
        G[name] = jnp.stack([d[i] for i in sorted(d)], 0)
    return loss, dx, G


_ANY = pl.BlockSpec(memory_space=pl.ANY)


def _all_gather(xs, *, name):
    n = len(xs)

    def body(*refs):
        x_refs, out_refs = refs[:n], refs[n:2 * n]
        send_sems, recv_sems, local_sems = refs[2 * n:]
        mx, my, mc = lax.axis_index("x"), lax.axis_index("y"), lax.axis_index("c")
        me, sibling = (mx, my, mc), (mx, my, 1 - mc)
        xn, yn, dg = (1 - mx, my), (mx, 1 - my), (1 - mx, 1 - my)
        src = (mx + (1 - mc) * (1 - 2 * mx), my + mc * (1 - 2 * my))
        dst = (mx + mc * (1 - 2 * mx), my + (1 - mc) * (1 - 2 * my))
        SIB, XN, YN, DG, PASS = 0, 1, 2, 3, 4

        def copy(a, k, block, to, own=False):
            px, py, pc = block
            slot = out_refs[a].at[4 * px + 2 * py + pc]
            return pltpu.make_async_remote_copy(
                src_ref=x_refs[a] if own else slot, dst_ref=slot,
                send_sem=send_sems.at[7 * a + k], recv_sem=recv_sems.at[7 * a + k],
                device_id=to, device_id_type=MESH)

        mine = [pltpu.make_async_copy(x_refs[a], out_refs[a].at[4 * mx + 2 * my + mc], local_sems.at[a])
                for a in range(n)]
        for cp in mine:
            cp.start()
        sent = [copy(a, XN, me, (*xn, mc), own=True) for a in range(n)]
        sent += [copy(a, YN, me, (*yn, mc), own=True) for a in range(n)]
        sent += [copy(a, SIB, me, sibling, own=True) for a in range(n)]
        for cp in sent:
            cp.start()
        for a in range(n):
            for k, chip in ((XN, xn), (YN, yn)):
                copy(a, k, (*chip, mc), me).wait_recv()
                sent.append(copy(a, PASS + k - 1, (*chip, mc), sibling))
                sent[-1].start()
            sent.append(copy(a, DG, (*src, mc), (*dst, mc)))
            sent[-1].start()
        for a in range(n):
            copy(a, DG, (*dg, mc), me).wait_recv()
            sent.append(copy(a, PASS + DG - 1, (*dg, mc), sibling))
            sent[-1].start()
        for a in range(n):
            copy(a, SIB, sibling, me).wait_recv()
            for k, chip in ((XN, xn), (YN, yn), (DG, dg)):
                copy(a, PASS + k - 1, (*chip, 1 - mc), me).wait_recv()
        for cp in sent:
            cp.wait_send()
        for cp in mine:
            cp.wait()

    return pl.pallas_call(
        body, name=name, in_specs=[_ANY] * n, out_specs=[_ANY] * n,
        out_shape=[jax.ShapeDtypeStruct((N_DEV,) + x.shape, x.dtype) for x in xs],
        scratch_shapes=[pltpu.SemaphoreType.DMA((7 * n,)), pltpu.SemaphoreType.DMA((7 * n,)),
                        pltpu.SemaphoreType.DMA((n,))],
    )(*xs)


N_CHIP = 4


def _pair_exchange(ps, *, name):
    n = len(ps)

    def body(*refs):
        p_refs, out_refs = refs[:n], refs[n:2 * n]
        send_sems, recv_sems = refs[2 * n:]
        mx, my, mc = lax.axis_index("x"), lax.axis_index("y"), lax.axis_index("c")
        copies = []
        for a in range(n):
            for chip in range(N_CHIP):
                copies.append(pltpu.make_async_remote_copy(
                    src_ref=p_refs[a].at[2 * chip + (1 - mc)], dst_ref=out_refs[a].at[chip],
                    send_sem=send_sems.at[N_CHIP * a + chip], recv_sem=recv_sems.at[N_CHIP * a + chip],
                    device_id=(mx, my, 1 - mc), device_id_type=MESH))
        for cp in copies:
            cp.start()
        for cp in copies:
            cp.wait()

    return pl.pallas_call(
        body, name=name, in_specs=[_ANY] * n, out_specs=[_ANY] * n,
        out_shape=[jax.ShapeDtypeStruct((N_CHIP,) + p.shape[1:], p.dtype) for p in ps],
        scratch_shapes=[pltpu.SemaphoreType.DMA((N_CHIP * n,)), pltpu.SemaphoreType.DMA((N_CHIP * n,))],
    )(*ps)


def _pair_sum(p, recv, core, *, name):
    _, R, C = p.shape
    tr = _row_tile(R, C, 4 * ROW_TILE_ELEMS)
    p4 = p.reshape(N_CHIP, 2, R, C)

    def body(core_ref, a_ref, b_ref, o_ref):
        o_ref[...] = (a_ref[...].astype(F32) + b_ref[...].astype(F32)).astype(o_ref.dtype)

    return pl.pallas_call(
        body, name=name,
        grid_spec=pltpu.PrefetchScalarGridSpec(
            num_scalar_prefetch=1, grid=(N_CHIP, R // tr),
            in_specs=[pl.BlockSpec((None, None, tr, C), lambda ch, i, core: (ch, core[0], i, 0)),
                      pl.BlockSpec((None, tr, C), lambda ch, i, core: (ch, i, 0))],
            out_specs=pl.BlockSpec((None, tr, C), lambda ch, i, core: (ch, i, 0))),
        out_shape=jax.ShapeDtypeStruct((N_CHIP, R, C), p.dtype),
        compiler_params=_params("parallel", "parallel"),
    )(core, p4, recv)


def _chip_exchange(ss, *, name):
    n = len(ss)

    def body(*refs):
        s_refs, out_refs, stage_refs = refs[:n], refs[n:2 * n], refs[2 * n:3 * n]
        send_sems, recv_sems, local_sems = refs[3 * n:]
        mx, my, mc = lax.axis_index("x"), lax.axis_index("y"), lax.axis_index("c")
        chip = 2 * mx + my
        xn, yn, dg = (1 - mx, my), (mx, 1 - my), (1 - mx, 1 - my)
        via = (mx + (1 - mc) * (1 - 2 * mx), my + mc * (1 - 2 * my))
        onward = (mx + mc * (1 - 2 * mx), my + (1 - mc) * (1 - 2 * my))
        XN, YN, STAGE, ONWARD = 0, 1, 2, 3

        def copy(a, k, src, dst, to):
            return pltpu.make_async_remote_copy(
                src_ref=src, dst_ref=dst, send_sem=send_sems.at[4 * a + k], recv_sem=recv_sems.at[4 * a + k],
                device_id=(*to, mc), device_id_type=MESH)

        def slot(ref, c):
            return ref.at[2 * c[0] + c[1]]

        mine = [pltpu.make_async_copy(s_refs[a].at[chip], out_refs[a].at[chip], local_sems.at[a]) for a in range(n)]
        for cp in mine:
            cp.start()
        first = []
        for a in range(n):
            first.append(copy(a, STAGE, slot(s_refs[a], dg), stage_refs[a], via))
            first.append(copy(a, XN, slot(s_refs[a], xn), out_refs[a].at[chip], xn))
            first.append(copy(a, YN, slot(s_refs[a], yn), out_refs[a].at[chip], yn))
        for cp in first:
            cp.start()
        onwards = []
        for a in range(n):
            first[3 * a].wait_recv()
            onwards.append(copy(a, ONWARD, stage_refs[a], slot(out_refs[a], via), onward))
            onwards[-1].start()
        for a in range(n):
            first[3 * a + 1].wait_recv()
            first[3 * a + 2].wait_recv()
            onwards[a].wait_recv()
        for cp in first + onwards:
            cp.wait_send()
        for cp in mine:
            cp.wait()

    outs = pl.pallas_call(
        body, name=name, in_specs=[_ANY] * n, out_specs=[_ANY] * (2 * n),
        out_shape=[jax.ShapeDtypeStruct(s.shape, s.dtype) for s in ss]
                  + [jax.ShapeDtypeStruct(s.shape[1:], s.dtype) for s in ss],
        scratch_shapes=[pltpu.SemaphoreType.DMA((4 * n,)), pltpu.SemaphoreType.DMA((4 * n,)),
                        pltpu.SemaphoreType.DMA((n,))],
    )(*ss)
    return outs[:n]


ROW_TILE_ELEMS = 256 * 1024


def _row_tile(R, C, elems=None):
    elems = ROW_TILE_ELEMS if elems is None else elems
    for t in (4096, 2048, 1024, 512, 256, 128, 64, 32, 16):
        if R % t == 0 and t * C <= elems:
            return t
    raise ValueError((R, C))


def _sum_slots(gs, *, name):
    S, R, C = gs.shape
    tr = _row_tile(R, C)

    def body(g_ref, o_ref):
        g = g_ref[0].astype(F32)
        for s in range(1, S):
            g = g + g_ref[s].astype(F32)
        o_ref[...] = g

    return pl.pallas_call(
        body, name=name, grid=(R // tr,),
        in_specs=[pl.BlockSpec((S, tr, C), lambda i: (0, i, 0))],
        out_specs=pl.BlockSpec((tr, C), lambda i: (i, 0)),
        out_shape=jax.ShapeDtypeStruct((R, C), F32),
        compiler_params=_params("parallel"),
    )(gs)


def _adamw(gs, w, m, v, *, name):
    S, R, C = gs.shape
    tr = _row_tile(R, C, 2 * ROW_TILE_ELEMS)

    def body(g_ref, w_ref, m_ref, v_ref, g_out, d_out, m_out, v_out):
        g = g_ref[0].astype(F32)
        for s in range(1, S):
            g = g + g_ref[s].astype(F32)
        m_new = ADAM_B1 * m_ref[...] + (1.0 - ADAM_B1) * g
        v_new = ADAM_B2 * v_ref[...] + (1.0 - ADAM_B2) * (g * g)
        m_hat = m_new / (1.0 - ADAM_B1 ** ADAM_STEP)
        v_hat = v_new / (1.0 - ADAM_B2 ** ADAM_STEP)
        g_out[...] = g
        d_out[...] = -ADAM_LR * (m_hat / (jnp.sqrt(v_hat) + ADAM_EPS) + ADAM_WD * w_ref[...])
        m_out[...] = m_new
        v_out[...] = v_new

    blk = pl.BlockSpec((tr, C), lambda i: (i, 0))
    sh = jax.ShapeDtypeStruct((R, C), F32)
    return pl.pallas_call(
        body, name=name, grid=(R // tr,),
        in_specs=[pl.BlockSpec((S, tr, C), lambda i: (0, i, 0)), blk, blk, blk],
        out_specs=[blk, blk, blk, blk], out_shape=[sh, sh, sh, sh],
        compiler_params=_params("parallel"),
    )(gs, w, m, v)


PIECE = 16 * LANES


def _pack(arrs, dtype, lead, row_mult):
    lead_shape = arrs[0].shape[:lead]
    parts, meta, off = [], [], 0
    for a in arrs:
        size = math.prod(a.shape[lead:])
        padded = -(-size // PIECE) * PIECE
        flat = a.astype(dtype).reshape(lead_shape + (size,))
        if padded != size:
            flat = jnp.concatenate([flat, jnp.zeros(lead_shape + (padded - size,), dtype)], -1)
        parts.append(flat)
        meta.append((off, size, a.shape[lead:]))
        off += padded
    total = -(-off // (row_mult * LANES)) * (row_mult * LANES)
    if total != off:
        parts.append(jnp.zeros(lead_shape + (total - off,), dtype))
    return jnp.concatenate(parts, -1).reshape(lead_shape + (total // LANES, LANES)), meta


def _unpack(packed, meta, lead):
    lead_shape = packed.shape[:lead]
    flat = packed.reshape(lead_shape + (-1,))
    return [flat[..., off:off + size].reshape(lead_shape + shape) for off, size, shape in meta]


ARG_NAMES = ['x', 'mem', 'positions', 'norm_mix_g', 'norm_xa_g', 'norm_mem_g', 'xa_wq', 'xa_wkv', 'xa_wo', 'norm_ffn_g', 'ffn_w_up', 'ffn_conv_w', 'ffn_conv_b', 'ffn_w_down', 'pc_w_in', 'pool_w', 'pool_scale', 'conv_dw_w', 'conv_dw_b', 'conv_ln_g', 'conv_ln_b', 'pc_w_out', 'mla_w_dq_dkv', 'mla_q_norm_g', 'mla_w_uq', 'mla_kv_norm_g', 'mla_w_ukv', 'mla_w_o', 'final_norm_g', 'loss_target']
WEIGHTS = ARG_NAMES[3:29]
BIG = {'xa_wq': 1, 'xa_wkv': 2, 'xa_wo': 1, 'ffn_w_up': 2, 'ffn_w_down': 1, 'pc_w_in': 2, 'pc_w_out': 1,
       'mla_w_dq_dkv': 1, 'mla_w_uq': 2, 'mla_w_ukv': 2, 'mla_w_o': 1}
SMALL_SHARDED = {'ffn_conv_w': 2, 'conv_dw_w': 2, 'mla_q_norm_g': 1, 'mla_kv_norm_g': 1}
REPLICATED = [n for n in WEIGHTS if n not in BIG and n not in SMALL_SHARDED]


def _from_slots(g, axis):
    t = jnp.moveaxis(g, 0, axis)
    return t.reshape(t.shape[:axis] + (t.shape[axis] * t.shape[axis + 1],) + t.shape[axis + 2:])


def _to_slots(full, axis):
    n = full.shape[axis] // N_DEV
    t = full.reshape(full.shape[:axis] + (N_DEV, n) + full.shape[axis + 1:])
    return jnp.moveaxis(t, axis, 0)


def kernel(x, mem, positions, norm_mix_g, norm_xa_g, norm_mem_g, xa_wq, xa_wkv, xa_wo, norm_ffn_g, ffn_w_up, ffn_conv_w, ffn_conv_b, ffn_w_down, pc_w_in, pool_w, pool_scale, conv_dw_w, conv_dw_b, conv_ln_g, conv_ln_b, pc_w_out, mla_w_dq_dkv, mla_q_norm_g, mla_w_uq, mla_kv_norm_g, mla_w_ukv, mla_w_o, final_norm_g, loss_target, m_norm_mix_g, m_norm_xa_g, m_norm_mem_g, m_xa_wq, m_xa_wkv, m_xa_wo, m_norm_ffn_g, m_ffn_w_up, m_ffn_conv_w, m_ffn_conv_b, m_ffn_w_down, m_pc_w_in, m_pool_w, m_pool_scale, m_conv_dw_w, m_conv_dw_b, m_conv_ln_g, m_conv_ln_b, m_pc_w_out, m_mla_w_dq_dkv, m_mla_q_norm_g, m_mla_w_uq, m_mla_kv_norm_g, m_mla_w_ukv, m_mla_w_o, m_final_norm_g, v_norm_mix_g, v_norm_xa_g, v_norm_mem_g, v_xa_wq, v_xa_wkv, v_xa_wo, v_norm_ffn_g, v_ffn_w_up, v_ffn_conv_w, v_ffn_conv_b, v_ffn_w_down, v_pc_w_in, v_pool_w, v_pool_scale, v_conv_dw_w, v_conv_dw_b, v_conv_ln_g, v_conv_ln_b, v_pc_w_out, v_mla_w_dq_dkv, v_mla_q_norm_g, v_mla_w_uq, v_mla_kv_norm_g, v_mla_w_ukv, v_mla_w_o, v_final_norm_g):
    args = (x, mem, positions, norm_mix_g, norm_xa_g, norm_mem_g, xa_wq, xa_wkv, xa_wo, norm_ffn_g, ffn_w_up, ffn_conv_w, ffn_conv_b, ffn_w_down, pc_w_in, pool_w, pool_scale, conv_dw_w, conv_dw_b, conv_ln_g, conv_ln_b, pc_w_out, mla_w_dq_dkv, mla_q_norm_g, mla_w_uq, mla_kv_norm_g, mla_w_ukv, mla_w_o, final_norm_g, loss_target)
    a = dict(zip(ARG_NAMES, args))
    mom = dict(zip(WEIGHTS, (m_norm_mix_g, m_norm_xa_g, m_norm_mem_g, m_xa_wq, m_xa_wkv, m_xa_wo, m_norm_ffn_g, m_ffn_w_up, m_ffn_conv_w, m_ffn_conv_b, m_ffn_w_down, m_pc_w_in, m_pool_w, m_pool_scale, m_conv_dw_w, m_conv_dw_b, m_conv_ln_g, m_conv_ln_b, m_pc_w_out, m_mla_w_dq_dkv, m_mla_q_norm_g, m_mla_w_uq, m_mla_kv_norm_g, m_mla_w_ukv, m_mla_w_o, m_final_norm_g)))
    var = dict(zip(WEIGHTS, (v_norm_mix_g, v_norm_xa_g, v_norm_mem_g, v_xa_wq, v_xa_wkv, v_xa_wo, v_norm_ffn_g, v_ffn_w_up, v_ffn_conv_w, v_ffn_conv_b, v_ffn_w_down, v_pc_w_in, v_pool_w, v_pool_scale, v_conv_dw_w, v_conv_dw_b, v_conv_ln_g, v_conv_ln_b, v_pc_w_out, v_mla_w_dq_dkv, v_mla_q_norm_g, v_mla_w_uq, v_mla_kv_norm_g, v_mla_w_ukv, v_mla_w_o, v_final_norm_g)))
    me = 4 * lax.axis_index("x") + 2 * lax.axis_index("y") + lax.axis_index("c")

    big_all = _all_gather([a[n].astype(CD) for n in BIG], name="gather_weights")
    sm_pack, sm_meta = _pack([a[n] for n in SMALL_SHARDED], F32, 0, 8)
    sm_all = _unpack(_all_gather([sm_pack], name="gather_small")[0], sm_meta, 1)
    W = {n: a[n] for n in REPLICATED}
    for (n, ax), g in zip(BIG.items(), big_all):
        W[n] = _from_slots(g, ax)
    for (n, ax), g in zip(SMALL_SHARDED.items(), sm_all):
        W[n] = _from_slots(g, ax)

    loss, dx, G = _local_step(x[0], mem[0], positions[0], loss_target[0], W)

    parts = [_to_slots(G[n], ax).astype(CD) for n, ax in BIG.items()]
    from_sibling = _pair_exchange(parts, name="grads_to_sibling")
    core = lax.axis_index("c").astype(jnp.int32).reshape(1)
    sums = []
    for n, p, r in zip(BIG, parts, from_sibling):
        cols = p.shape[-1]
        s = _pair_sum(p.reshape(N_DEV, -1, cols), r.reshape(N_CHIP, -1, cols), core, name=f"pair_sum_{n}")
        sums.append(s.reshape((N_CHIP,) + p.shape[1:]))
    recv = _chip_exchange(sums, name="scatter_grads")
    out = {}
    for n, r in zip(BIG, recv):
        shape = a[n].shape
        rows = lambda t: t.reshape(-1, shape[-1])
        res = _adamw(r.reshape(N_CHIP, -1, shape[-1]), rows(a[n]), rows(mom[n]), rows(var[n]), name=f"adamw_{n}")
        out[n] = tuple(t.reshape(shape) for t in res)

    small_names = REPLICATED + list(SMALL_SHARDED)
    spack, smeta = _pack([G[n] for n in small_names] + [loss], F32, 0, 256)
    stot = _unpack(_sum_slots(_all_gather([spack], name="gather_small_grads")[0], name="sum_small_grads"), smeta, 0)
    loss_total = stot[-1][0, 0]
    gsm = dict(zip(small_names, stot[:-1]))
    for n, ax in SMALL_SHARDED.items():
        width = a[n].shape[ax]
        gsm[n] = lax.dynamic_slice_in_dim(gsm[n], me * width, width, ax)
    g1, meta1 = _pack([gsm[n] for n in small_names], F32, 0, 256)
    w1, _ = _pack([a[n] for n in small_names], F32, 0, 256)
    m1, _ = _pack([mom[n] for n in small_names], F32, 0, 256)
    v1, _ = _pack([var[n] for n in small_names], F32, 0, 256)
    res = [_unpack(r, meta1, 0) for r in _adamw(g1[None], w1, m1, v1, name="adamw_small")]
    for i, n in enumerate(small_names):
        out[n] = tuple(r[i] for r in res)

    return (loss_total, dx[None],
            *[out[n][0] for n in WEIGHTS], *[out[n][1] for n in WEIGHTS],
            *[out[n][2] for n in WEIGHTS], *[out[n][3] for n in WEIGHTS])
```

```python
import functools
import math

import jax
import jax.numpy as jnp
from jax import lax
from jax.experimental import pallas as pl
from jax.experimental.pallas import tpu as pltpu

F32 = jnp.float32
CD = jnp.bfloat16
EPS = 1e-6
NEG = -1e30
N_DEV = 8
LANES = 128
HALO = 32

D_MODEL = 1024
DEPTH = 4
XA_HEADS = 4
XA_DH = 256
MEM_LEN = 256
POOL_WINDOWS = (2, 4, 8, 16)
CONV_K = 31
FFN_K = 3
D_FF = 2816
MLA_HEADS = 16
QK_NOPE = 64
QK_ROPE = 32
V_HEAD = 64
Q_LORA = 384
KV_LORA = 256
ROPE_THETA = 10000.0
MLA_SCALE = 1.0 / math.sqrt(QK_NOPE + QK_ROPE)
XA_SCALE = XA_DH ** -0.5

ADAM_LR = 0.001
ADAM_B1 = 0.9
ADAM_B2 = 0.999
ADAM_EPS = 1e-08
ADAM_WD = 0.01
ADAM_STEP = 10

NT = (((1,), (1,)), ((), ()))
TN = (((0,), (0,)), ((), ()))
MESH = pl.DeviceIdType.MESH


def _tile(n, target):
    if n <= target:
        return n
    best = None
    for t in range(LANES, target + 1, LANES):
        if n % t == 0:
            best = t
    assert best is not None, (n, target)
    return best


def _params(*sem):
    return pltpu.CompilerParams(dimension_semantics=sem)


def _sigmoid(v):
    return 0.5 * jnp.tanh(0.5 * v) + 0.5


def _rms_bwd(x, gain, dh):
    r = lax.rsqrt(jnp.mean(x * x, axis=-1, keepdims=True) + EPS)
    xhat = x * r
    dxhat = dh * gain
    dx = r * (dxhat - xhat * jnp.mean(dxhat * xhat, axis=-1, keepdims=True))
    return dx, dh * xhat


def _weight(w):
    if not isinstance(w, tuple):
        return w, w.shape, pl.BlockSpec
    arr, layer = w

    def spec(block, imap):
        return pl.BlockSpec((None,) + tuple(block), lambda *a: (layer,) + tuple(imap(*a)))

    return arr, arr.shape[1:], spec


def _nmm(x, g, w, *, name, out_dtype, tm=1024, tn_target=1024):
    M, K = x.shape
    w, (_, N), wspec = _weight(w)
    tm = min(tm, M)
    tn = _tile(N, tn_target)

    def body(x_ref, g_ref, w_ref, z_ref, h_ref):
        @pl.when(pl.program_id(1) == 0)
        def _():
            xf = x_ref[...]
            r = lax.rsqrt(jnp.mean(xf * xf, axis=-1, keepdims=True) + EPS)
            h_ref[...] = (xf * r * g_ref[...]).astype(h_ref.dtype)

        z_ref[...] = jnp.dot(h_ref[...], w_ref[...], preferred_element_type=F32).astype(z_ref.dtype)

    return pl.pallas_call(
        body, name=name, grid=(M // tm, N // tn),
        in_specs=[pl.BlockSpec((tm, K), lambda i, j: (i, 0)),
                  pl.BlockSpec((1, K), lambda i, j: (0, 0)),
                  wspec((K, tn), lambda i, j: (0, j))],
        out_specs=[pl.BlockSpec((tm, tn), lambda i, j: (i, j)),
                   pl.BlockSpec((tm, K), lambda i, j: (i, 0))],
        out_shape=[jax.ShapeDtypeStruct((M, N), out_dtype), jax.ShapeDtypeStruct((M, K), CD)],
        compiler_params=_params("parallel", "arbitrary"),
    )(x, g, w)


def _mm_res(a, w, res, *, name, tm=1024, tn_target=1024):
    M, K = a.shape
    w, (_, N), wspec = _weight(w)
    tm = min(tm, M)
    tn = _tile(N, tn_target)

    def body(a_ref, w_ref, r_ref, o_ref):
        o_ref[...] = r_ref[...] + jnp.dot(a_ref[...].astype(CD), w_ref[...], preferred_element_type=F32)

    return pl.pallas_call(
        body, name=name, grid=(M // tm, N // tn),
        in_specs=[pl.BlockSpec((tm, K), lambda i, j: (i, 0)),
                  wspec((K, tn), lambda i, j: (0, j)),
                  pl.BlockSpec((tm, tn), lambda i, j: (i, j))],
        out_specs=pl.BlockSpec((tm, tn), lambda i, j: (i, j)),
        out_shape=jax.ShapeDtypeStruct((M, N), F32),
        compiler_params=_params("parallel", "arbitrary"),
    )(a, w, res)


def _mm_nt(a, w, *, name, out_dtype, tm=1024, tn_target=1024):
    M, K = a.shape
    w, (N, _), wspec = _weight(w)
    tm = min(tm, M)
    tn = _tile(N, tn_target)

    def body(a_ref, w_ref, o_ref):
        o_ref[...] = lax.dot_general(a_ref[...].astype(CD), w_ref[...], NT,
                                     preferred_element_type=F32).astype(o_ref.dtype)

    return pl.pallas_call(
        body, name=name, grid=(M // tm, N // tn),
        in_specs=[pl.BlockSpec((tm, K), lambda i, j: (i, 0)),
                  wspec((tn, K), lambda i, j: (j, 0))],
        out_specs=pl.BlockSpec((tm, tn), lambda i, j: (i, j)),
        out_shape=jax.ShapeDtypeStruct((M, N), out_dtype),
        compiler_params=_params("parallel", "arbitrary"),
    )(a, w)


def _mm_nt_normbwd(gy, w, x, gain, dres, *, name, tm=1024, tk_target=1408):
    M, K = gy.shape
    w, (D, _), wspec = _weight(w)
    tm = min(tm, M)
    tk = _tile(K, tk_target)
    nk = K // tk

    def body(g_ref, w_ref, x_ref, gain_ref, dres_ref, dx_ref, dg_ref, acc):
        i, k = pl.program_id(0), pl.program_id(1)

        @pl.when(k == 0)
        def _():
            acc[...] = jnp.zeros_like(acc)

        acc[...] += lax.dot_general(g_ref[...].astype(CD), w_ref[...], NT, preferred_element_type=F32)

        @pl.when(k == nk - 1)
        def _():
            dx, dg_rows = _rms_bwd(x_ref[...], gain_ref[...], acc[...])
            dx_ref[...] = dres_ref[...] + dx

            @pl.when(i == 0)
            def _():
                dg_ref[...] = jnp.zeros_like(dg_ref)

            dg_ref[...] += jnp.sum(dg_rows, axis=0, keepdims=True)

    return pl.pallas_call(
        body, name=name, grid=(M // tm, nk),
        in_specs=[pl.BlockSpec((tm, tk), lambda i, k: (i, k)),
                  wspec((D, tk), lambda i, k: (0, k)),
                  pl.BlockSpec((tm, D), lambda i, k: (i, 0)),
                  pl.BlockSpec((1, D), lambda i, k: (0, 0)),
                  pl.BlockSpec((tm, D), lambda i, k: (i, 0))],
        out_specs=[pl.BlockSpec((tm, D), lambda i, k: (i, 0)),
                   pl.BlockSpec((1, D), lambda i, k: (0, 0))],
        out_shape=[jax.ShapeDtypeStruct((M, D), F32), jax.ShapeDtypeStruct((1, D), F32)],
        scratch_shapes=[pltpu.VMEM((tm, D), F32)],
        compiler_params=_params("arbitrary", "arbitrary"),
    )(gy, w, x, gain, dres)


def _mm_tn(a, g, *, name, tt=2048, tk_target=1024, tn_target=1024):
    T, K = a.shape
    N = g.shape[1]
    tt = min(tt, T)
    tk = _tile(K, tk_target)
    tn = _tile(N, tn_target)

    def body(a_ref, g_ref, o_ref):
        @pl.when(pl.program_id(2) == 0)
        def _():
            o_ref[...] = jnp.zeros_like(o_ref)

        o_ref[...] += lax.dot_general(a_ref[...].astype(CD), g_ref[...].astype(CD), TN,
                                      preferred_element_type=F32)

    return pl.pallas_call(
        body, name=name, grid=(K // tk, N // tn, T // tt),
        in_specs=[pl.BlockSpec((tt, tk), lambda i, j, t: (t, i)),
                  pl.BlockSpec((tt, tn), lambda i, j, t: (t, j))],
        out_specs=pl.BlockSpec((tk, tn), lambda i, j, t: (i, j)),
        out_shape=jax.ShapeDtypeStruct((K, N), F32),
        compiler_params=_params("parallel", "parallel", "arbitrary"),
    )(a, g)


POOL_W = 512
CONV_W = 512
POOL_GROUP = 128


MIX_ROWS = 64
LN_ROWS = 256
LN_BWD_ROWS = 512
SUB = 8


def _shifted(sh_sc, x, n_rows):
    for b in range(1, SUB):
        sh_sc[b, pl.ds(0, n_rows), :] = x[b:b + n_rows]


def _tap(sh_sc, src, r0, cols, start, rows):
    a, b = divmod(start, SUB)
    if b == 0:
        return src[pl.ds(r0 + SUB * a, rows), cols]
    return sh_sc[b, pl.ds(SUB * a, rows), :]


def _pool_rows(zp_ref, z_ref, cols, win, i, tt, first, pooled_sc):
    RB = min(MIX_ROWS, tt)
    hb = 2 * SUB
    for r in range(tt // RB):
        if r == 0:
            p = zp_ref[pl.ds(HALO - hb, hb), cols]
            v = jnp.concatenate([jnp.where(first, jnp.zeros_like(p), p), z_ref[pl.ds(0, RB), cols]], axis=0)
        else:
            v = z_ref[pl.ds(r * RB - hb, RB + hb), cols]
        u = v[hb:hb + RB]
        s = u
        for j in range(1, win):
            s = s + v[hb - j:hb - j + RB]
        t_glob = i * tt + r * RB + lax.broadcasted_iota(jnp.int32, (RB, 1), 0)
        cnt = jnp.minimum(t_glob + 1, win).astype(F32)
        pooled_sc[pl.ds(r * RB, RB), :] = (s / cnt - u).astype(pooled_sc.dtype)


def _fill_gl(gl_sc, zp_ref, z_ref, zn_ref, tt, first, last):
    ca, cb = pl.ds(POOL_W, CONV_W), pl.ds(POOL_W + CONV_W, CONV_W)
    g = zp_ref[:, ca] * _sigmoid(zp_ref[:, cb])
    gl_sc[pl.ds(0, HALO), :] = jnp.where(first, jnp.zeros_like(g), g)

    def rows(r, carry):
        r0 = pl.multiple_of(r * LN_ROWS, LN_ROWS)
        gl_sc[pl.ds(HALO + r0, LN_ROWS), :] = z_ref[pl.ds(r0, LN_ROWS), ca] * _sigmoid(z_ref[pl.ds(r0, LN_ROWS), cb])
        return carry

    lax.fori_loop(0, tt // LN_ROWS, rows, 0)
    if zn_ref is not None:
        g = zn_ref[:, ca] * _sigmoid(zn_ref[:, cb])
        gl_sc[pl.ds(HALO + tt, HALO), :] = jnp.where(last, jnp.zeros_like(g), g)


def _conv_rows(gl_sc, cv_sc, sh_sc, w_ref, b_ref, n_rows):
    RB = min(MIX_ROWS, n_rows)
    for c in range(CONV_W // LANES):
        cols = pl.ds(c * LANES, LANES)
        bias = b_ref[:, cols]

        def chunk(r0, rb):
            g = gl_sc[pl.ds(r0, rb + HALO), cols]
            _shifted(sh_sc, g, rb + HALO - SUB)
            cv = jnp.zeros((rb, LANES), F32) + bias
            for j in range(CONV_K):
                cv = cv + w_ref[pl.ds(j, 1), cols] * _tap(sh_sc, gl_sc, r0, cols, HALO - (CONV_K - 1) + j, rb)
            cv_sc[pl.ds(r0, rb), cols] = cv

        def body(r, carry):
            chunk(pl.multiple_of(r * RB, RB), RB)
            return carry

        lax.fori_loop(0, n_rows // RB, body, 0)
        if n_rows % RB:
            chunk((n_rows // RB) * RB, n_rows % RB)


def _mixer_fwd(z, pool_w, pool_scale, dw_w, dw_b, ln_g, ln_b, *, name, tt=512):
    T, C = z.shape
    tt = min(tt, T)
    n = T // tt
    hb = tt // HALO

    def body(zp_ref, z_ref, pw_ref, ps_ref, w_ref, b_ref, g_ref, bb_ref, o_ref, pooled_sc, gl_sc, cv_sc, sh_sc):
        i = pl.program_id(0)
        first = i == 0
        for gi, win in enumerate(POOL_WINDOWS):
            cols = pl.ds(gi * POOL_GROUP, POOL_GROUP)
            _pool_rows(zp_ref, z_ref, cols, win, i, tt, first, pooled_sc)
            ya = jnp.dot(pooled_sc[...], pw_ref[gi].astype(CD), preferred_element_type=F32)
            o_ref[:, cols] = (ya * ps_ref[:, cols]).astype(o_ref.dtype)
        _fill_gl(gl_sc, zp_ref, z_ref, None, tt, first, None)
        _conv_rows(gl_sc, cv_sc, sh_sc, w_ref, b_ref, tt)

        def ln_rows(r, carry):
            rows = pl.ds(pl.multiple_of(r * LN_ROWS, LN_ROWS), LN_ROWS)
            cv = cv_sc[rows, :]
            xc = cv - jnp.mean(cv, axis=-1, keepdims=True)
            yn = xc * lax.rsqrt(jnp.mean(xc * xc, axis=-1, keepdims=True) + EPS) * g_ref[...] + bb_ref[...]
            o_ref[rows, pl.ds(POOL_W, CONV_W)] = (yn * _sigmoid(yn)).astype(o_ref.dtype)
            return carry

        lax.fori_loop(0, tt // LN_ROWS, ln_rows, 0, unroll=4)

    full = lambda shape: pl.BlockSpec(shape, lambda i: (0,) * len(shape))
    return pl.pallas_call(
        body, name=name, grid=(n,),
        in_specs=[pl.BlockSpec((HALO, C), lambda i: (jnp.maximum(i * hb - 1, 0), 0)),
                  pl.BlockSpec((tt, C), lambda i: (i, 0)),
                  full((4, POOL_GROUP, POOL_GROUP)), full((1, POOL_W)), full((CONV_K + 1, CONV_W)),
                  full((1, CONV_W)), full((1, CONV_W)), full((1, CONV_W))],
        out_specs=pl.BlockSpec((tt, POOL_W + CONV_W), lambda i: (i, 0)),
        out_shape=jax.ShapeDtypeStruct((T, POOL_W + CONV_W), CD),
        scratch_shapes=[pltpu.VMEM((tt, POOL_GROUP), CD), pltpu.VMEM((tt + HALO, CONV_W), F32),
                        pltpu.VMEM((tt, CONV_W), F32), pltpu.VMEM((SUB, MIX_ROWS + HALO, LANES), F32)],
        compiler_params=_params("parallel"),
    )(z, z, pool_w, pool_scale, dw_w, dw_b, ln_g, ln_b)


def _mixer_bwd(z, dy, pool_w, pool_scale, dw_w, dw_b, ln_g, ln_b, *, name, tt=512):
    T, C = z.shape
    tt = min(tt, T)
    n = T // tt
    hb = tt // HALO
    R = tt + HALO
    RB = min(MIX_ROWS, tt)

    def body(zp_ref, z_ref, zn_ref, dy_ref, dyn_ref, pw_ref, ps_ref, w_ref, b_ref, g_ref, bb_ref,
             dz_ref, dpw_ref, dps_ref, dw_ref, db_ref, dg_ref, dbb_ref,
             pooled_sc, dm_sc, dpool_sc, dpe_sc, gl_sc, cv_sc, accw, accl, sh_sc, shd_sc):
        i = pl.program_id(0)
        first, last = i == 0, i == n - 1

        @pl.when(first)
        def _():
            for r in (dpw_ref, dps_ref, dw_ref, db_ref, dg_ref, dbb_ref):
                r[...] = jnp.zeros_like(r)

        def dy_rows(cols):
            nxt = dyn_ref[:, cols]
            return jnp.concatenate([dy_ref[:, cols], jnp.where(last, jnp.zeros_like(nxt), nxt)], axis=0)

        t_all = i * tt + lax.broadcasted_iota(jnp.int32, (R, 1), 0)
        for gi, win in enumerate(POOL_WINDOWS):
            cols = pl.ds(gi * POOL_GROUP, POOL_GROUP)
            _pool_rows(zp_ref, z_ref, cols, win, i, tt, first, pooled_sc)
            pw = pw_ref[gi].astype(CD)
            dya = dy_rows(cols)
            mm = jnp.dot(pooled_sc[...], pw, preferred_element_type=F32)
            dps_ref[:, cols] += jnp.sum(dya[:tt] * mm, axis=0, keepdims=True)
            dm_sc[...] = (dya * ps_ref[:, cols]).astype(CD)
            dpw_ref[gi] += lax.dot_general(pooled_sc[...], dm_sc[pl.ds(0, tt), :], TN, preferred_element_type=F32)
            dpool = lax.dot_general(dm_sc[...], pw, NT, preferred_element_type=F32)
            dpool_sc[...] = dpool
            dpe_sc[...] = dpool / jnp.minimum(t_all + 1, win).astype(F32)

            def du_rows(r, carry):
                r0 = pl.multiple_of(r * RB, RB)
                e = dpe_sc[pl.ds(r0, RB + 2 * SUB), :]
                du = -dpool_sc[pl.ds(r0, RB), :]
                for j in range(win):
                    du = du + e[j:j + RB]
                dz_ref[pl.ds(r0, RB), cols] = du.astype(dz_ref.dtype)
                return carry

            lax.fori_loop(0, tt // RB, du_rows, 0)

        _fill_gl(gl_sc, zp_ref, z_ref, zn_ref, tt, first, last)
        _conv_rows(gl_sc, cv_sc, sh_sc, w_ref, b_ref, R)
        accl[...] = jnp.zeros_like(accl)

        def ln_rows(r0, nr, in_tile):
            rows = pl.ds(r0, nr)
            cv = cv_sc[rows, :]
            xc = cv - jnp.mean(cv, axis=-1, keepdims=True)
            rstd = lax.rsqrt(jnp.mean(xc * xc, axis=-1, keepdims=True) + EPS)
            xhat = xc * rstd
            yn = xhat * g_ref[...] + bb_ref[...]
            sy = _sigmoid(yn)
            if in_tile:
                dyv = dy_ref[rows, pl.ds(POOL_W, CONV_W)]
            else:
                nxt = dyn_ref[:, pl.ds(POOL_W, CONV_W)]
                dyv = jnp.where(last, jnp.zeros_like(nxt), nxt)
            dyn = dyv * (sy * (1.0 + yn * (1.0 - sy)))
            if in_tile:
                accl[pl.ds(0, SUB), :] += jnp.sum((dyn * xhat).reshape(nr // SUB, SUB, CONV_W), axis=0)
                accl[pl.ds(SUB, SUB), :] += jnp.sum(dyn.reshape(nr // SUB, SUB, CONV_W), axis=0)
            dxh = dyn * g_ref[...]
            dcv = rstd * (dxh - jnp.mean(dxh, axis=-1, keepdims=True)
                          - xhat * jnp.mean(dxh * xhat, axis=-1, keepdims=True))
            cv_sc[rows, :] = dcv
            if in_tile:
                accl[pl.ds(2 * SUB, SUB), :] += jnp.sum(dcv.reshape(nr // SUB, SUB, CONV_W), axis=0)

        lnb = min(LN_BWD_ROWS, tt)

        def ln_body(r, carry):
            ln_rows(pl.multiple_of(r * lnb, lnb), lnb, True)
            return carry

        lax.fori_loop(0, tt // lnb, ln_body, 0)
        ln_rows(tt, HALO, False)
        dg_ref[...] += jnp.sum(accl[pl.ds(0, SUB), :], axis=0, keepdims=True)
        dbb_ref[...] += jnp.sum(accl[pl.ds(SUB, SUB), :], axis=0, keepdims=True)
        db_ref[...] += jnp.sum(accl[pl.ds(2 * SUB, SUB), :], axis=0, keepdims=True)

        accw[...] = jnp.zeros_like(accw)
        for c in range(CONV_W // LANES):
            cols = pl.ds(c * LANES, LANES)

            def chunk(r, carry):
                r0 = pl.multiple_of(r * RB, RB)
                d = cv_sc[pl.ds(r0, RB + HALO), cols]
                g = gl_sc[pl.ds(r0, RB + HALO), cols]
                _shifted(shd_sc, d, RB + HALO - SUB)
                _shifted(sh_sc, g, RB + HALO - SUB)
                d_t = d[:RB]
                dgl = jnp.zeros((RB, LANES), F32)
                for j in range(CONV_K):
                    dgl = dgl + w_ref[pl.ds(j, 1), cols] * _tap(shd_sc, cv_sc, r0, cols, CONV_K - 1 - j, RB)
                    prod = d_t * _tap(sh_sc, gl_sc, r0, cols, HALO - (CONV_K - 1) + j, RB)
                    accw[pl.ds(SUB * j, SUB), cols] += jnp.sum(prod.reshape(RB // SUB, SUB, LANES), axis=0)
                a_t = z_ref[pl.ds(r0, RB), pl.ds(POOL_W + c * LANES, LANES)]
                sb = _sigmoid(z_ref[pl.ds(r0, RB), pl.ds(POOL_W + CONV_W + c * LANES, LANES)])
                dz_ref[pl.ds(r0, RB), pl.ds(POOL_W + c * LANES, LANES)] = (dgl * sb).astype(dz_ref.dtype)
                dz_ref[pl.ds(r0, RB), pl.ds(POOL_W + CONV_W + c * LANES, LANES)] = (
                    dgl * a_t * sb * (1.0 - sb)).astype(dz_ref.dtype)
                return carry

            lax.fori_loop(0, tt // RB, chunk, 0)
        for j in range(CONV_K):
            dw_ref[pl.ds(j, 1), :] += jnp.sum(accw[pl.ds(SUB * j, SUB), :], axis=0, keepdims=True)

    full = lambda shape: pl.BlockSpec(shape, lambda i: (0,) * len(shape))
    nb = T // HALO
    outs = pl.pallas_call(
        body, name=name, grid=(n,),
        in_specs=[pl.BlockSpec((HALO, C), lambda i: (jnp.maximum(i * hb - 1, 0), 0)),
                  pl.BlockSpec((tt, C), lambda i: (i, 0)),
                  pl.BlockSpec((HALO, C), lambda i: (jnp.minimum((i + 1) * hb, nb - 1), 0)),
                  pl.BlockSpec((tt, 2 * POOL_W), lambda i: (i, 0)),
                  pl.BlockSpec((HALO, 2 * POOL_W), lambda i: (jnp.minimum((i + 1) * hb, nb - 1), 0)),
                  full((4, POOL_GROUP, POOL_GROUP)), full((1, POOL_W)), full((CONV_K + 1, CONV_W)),
                  full((1, CONV_W)), full((1, CONV_W)), full((1, CONV_W))],
        out_specs=[pl.BlockSpec((tt, C), lambda i: (i, 0)),
                   full((4, POOL_GROUP, POOL_GROUP)), full((1, POOL_W)), full((CONV_K + 1, CONV_W)),
                   full((1, CONV_W)), full((1, CONV_W)), full((1, CONV_W))],
        out_shape=[jax.ShapeDtypeStruct((T, C), CD),
                   jax.ShapeDtypeStruct((4, POOL_GROUP, POOL_GROUP), F32),
                   jax.ShapeDtypeStruct((1, POOL_W), F32),
                   jax.ShapeDtypeStruct((CONV_K + 1, CONV_W), F32),
                   jax.ShapeDtypeStruct((1, CONV_W), F32),
                   jax.ShapeDtypeStruct((1, CONV_W), F32),
                   jax.ShapeDtypeStruct((1, CONV_W), F32)],
        scratch_shapes=[pltpu.VMEM((tt, POOL_GROUP), CD), pltpu.VMEM((R, POOL_GROUP), CD),
                        pltpu.VMEM((R, POOL_GROUP), F32), pltpu.VMEM((R, POOL_GROUP), F32),
                        pltpu.VMEM((tt + 2 * HALO, CONV_W), F32), pltpu.VMEM((R, CONV_W), F32),
                        pltpu.VMEM((SUB * (CONV_K + 1), CONV_W), F32), pltpu.VMEM((3 * SUB, CONV_W), F32),
                        pltpu.VMEM((SUB, MIX_ROWS + HALO, LANES), F32),
                        pltpu.VMEM((SUB, MIX_ROWS + HALO, LANES), F32)],
        compiler_params=_params("arbitrary"),
    )(z, z, z, dy, dy, pool_w, pool_scale, dw_w, dw_b, ln_g, ln_b)
    return outs


CHUNK_HALO = 16
FFN_ROWS = 64
FFN_LANES = 128


def _rows(cur, prev, nxt, r, rb, before, after, cols, n_r, first, last):
    lo, hi = r * rb - before, r * rb + rb + after
    tt = n_r * rb
    parts = []
    if lo < 0:
        p = prev[pl.ds(HALO + lo, -lo), cols]
        parts.append(jnp.where(first, jnp.zeros_like(p), p))
        lo = 0
    parts.append(cur[pl.ds(lo, min(hi, tt) - lo), cols])
    if hi > tt:
        p = nxt[pl.ds(0, hi - tt), cols]
        parts.append(jnp.where(last, jnp.zeros_like(p), p))
    return parts[0] if len(parts) == 1 else jnp.concatenate(parts, axis=0)


def _ffn_mid_fwd(up, cw, cb, *, name, tt=512):
    T = up.shape[0]
    tt = min(tt, T)
    n = T // tt
    hb = tt // HALO
    RB, CW, HB = min(FFN_ROWS, tt), FFN_LANES, CHUNK_HALO
    n_r = tt // RB

    def body(a_ref, gp_ref, g_ref, w_ref, b_ref, o_ref):
        first = pl.program_id(0) == 0

        def col_chunk(c, carry):
            cols = pl.ds(pl.multiple_of(c * CW, CW), CW)
            w = w_ref[:, cols]
            b = b_ref[:, cols]
            for r in range(n_r):
                v = _rows(g_ref, gp_ref, None, r, RB, HB, 0, cols, n_r, first, None).astype(F32)
                gc = b + w[0:1] * v[HB - 2:HB - 2 + RB] + w[1:2] * v[HB - 1:HB - 1 + RB] + w[2:3] * v[HB:HB + RB]
                a = a_ref[pl.ds(r * RB, RB), cols].astype(F32)
                o_ref[pl.ds(r * RB, RB), cols] = (gc * _sigmoid(gc) * a).astype(o_ref.dtype)
            return carry

        lax.fori_loop(0, D_FF // CW, col_chunk, 0)

    return pl.pallas_call(
        body, name=name, grid=(n,),
        in_specs=[pl.BlockSpec((tt, D_FF), lambda i: (i, 0)),
                  pl.BlockSpec((HALO, D_FF), lambda i: (jnp.maximum(i * hb - 1, 0), 1)),
                  pl.BlockSpec((tt, D_FF), lambda i: (i, 1)),
                  pl.BlockSpec((8, D_FF), lambda i: (0, 0)),
                  pl.BlockSpec((1, D_FF), lambda i: (0, 0))],
        out_specs=pl.BlockSpec((tt, D_FF), lambda i: (i, 0)),
        out_shape=jax.ShapeDtypeStruct((T, D_FF), CD),
        compiler_params=_params("parallel"),
    )(up, up, up, cw, cb)


def _ffn_mid_bwd(up, dact, cw, cb, *, name, tt=512):
    T = up.shape[0]
    tt = min(tt, T)
    n = T // tt
    hb = tt // HALO
    nb = T // HALO
    RB, CW, HB = min(FFN_ROWS, tt), FFN_LANES, CHUNK_HALO
    n_r = tt // RB
    RE = RB + 8

    def body(a_ref, an_ref, gp_ref, g_ref, gn_ref, d_ref, dn_ref, w_ref, b_ref, dup_ref, dw_ref, db_ref, acc):
        i = pl.program_id(0)
        first, last = i == 0, i == n - 1

        @pl.when(first)
        def _():
            dw_ref[...] = jnp.zeros_like(dw_ref)
            db_ref[...] = jnp.zeros_like(db_ref)

        def col_chunk(c, carry):
            cols = pl.ds(pl.multiple_of(c * CW, CW), CW)
            w = w_ref[:, cols]
            b = b_ref[:, cols]
            part = [jnp.zeros((8, CW), F32) for _ in range(FFN_K + 1)]
            for r in range(n_r):
                v = _rows(g_ref, gp_ref, gn_ref, r, RB, HB, HB, cols, n_r, first, last).astype(F32)
                gs = [v[HB - 2 + j:HB - 2 + j + RE] for j in range(FFN_K)]
                gc = b + w[0:1] * gs[0] + w[1:2] * gs[1] + w[2:3] * gs[2]
                sg = _sigmoid(gc)
                d = _rows(d_ref, None, dn_ref, r, RB, 0, HB, cols, n_r, None, last).astype(F32)[:RE]
                a = _rows(a_ref, None, an_ref, r, RB, 0, HB, cols, n_r, None, last).astype(F32)[:RE]
                silu = gc * sg
                dgc = d * a * (sg + silu - silu * sg)
                dup_ref[pl.ds(r * RB, RB), cols] = (d[:RB] * silu[:RB]).astype(dup_ref.dtype)
                dg = w[2:3] * dgc[0:RB] + w[1:2] * dgc[1:RB + 1] + w[0:1] * dgc[2:RB + 2]
                dup_ref[pl.ds(r * RB, RB), pl.ds(pl.multiple_of(D_FF + c * CW, CW), CW)] = dg.astype(dup_ref.dtype)
                dgc_t = dgc[:RB]
                for j in range(FFN_K):
                    part[j] = part[j] + jnp.sum((dgc_t * gs[j][:RB]).reshape(RB // 8, 8, CW), axis=0)
                part[FFN_K] = part[FFN_K] + jnp.sum(dgc_t.reshape(RB // 8, 8, CW), axis=0)
            for j in range(FFN_K + 1):
                acc[pl.ds(8 * j, 8), cols] = part[j]
            return carry

        lax.fori_loop(0, D_FF // CW, col_chunk, 0)
        for j in range(FFN_K):
            dw_ref[pl.ds(j, 1), :] += jnp.sum(acc[pl.ds(8 * j, 8), :], axis=0, keepdims=True)
        db_ref[...] += jnp.sum(acc[pl.ds(8 * FFN_K, 8), :], axis=0, keepdims=True)

    nxt = lambda i: jnp.minimum((i + 1) * hb, nb - 1)
    return pl.pallas_call(
        body, name=name, grid=(n,),
        in_specs=[pl.BlockSpec((tt, D_FF), lambda i: (i, 0)),
                  pl.BlockSpec((HALO, D_FF), lambda i: (nxt(i), 0)),
                  pl.BlockSpec((HALO, D_FF), lambda i: (jnp.maximum(i * hb - 1, 0), 1)),
                  pl.BlockSpec((tt, D_FF), lambda i: (i, 1)),
                  pl.BlockSpec((HALO, D_FF), lambda i: (nxt(i), 1)),
                  pl.BlockSpec((tt, D_FF), lambda i: (i, 0)),
                  pl.BlockSpec((HALO, D_FF), lambda i: (nxt(i), 0)),
                  pl.BlockSpec((8, D_FF), lambda i: (0, 0)),
                  pl.BlockSpec((1, D_FF), lambda i: (0, 0))],
        out_specs=[pl.BlockSpec((tt, 2 * D_FF), lambda i: (i, 0)),
                   pl.BlockSpec((8, D_FF), lambda i: (0, 0)),
                   pl.BlockSpec((1, D_FF), lambda i: (0, 0))],
        out_shape=[jax.ShapeDtypeStruct((T, 2 * D_FF), CD),
                   jax.ShapeDtypeStruct((8, D_FF), F32),
                   jax.ShapeDtypeStruct((1, D_FF), F32)],
        scratch_shapes=[pltpu.VMEM((8 * (FFN_K + 1), D_FF), F32)],
        compiler_params=_params("arbitrary"),
    )(up, up, up, up, up, dact, dact, cw, cb)


def _xattn_probs(q, k):
    s = lax.dot_general(q, k, NT, preferred_element_type=F32) * XA_SCALE
    p = jnp.exp(s - jnp.max(s, axis=-1, keepdims=True))
    return p / jnp.sum(p, axis=-1, keepdims=True)


def _xattn_fwd(q, kv, *, name, tq=512):
    T = q.shape[0]
    tq = min(tq, T)

    def body(q_ref, kv_ref, o_ref):
        for h in range(XA_HEADS):
            cols = pl.ds(h * XA_DH, XA_DH)
            p = _xattn_probs(q_ref[:, cols], kv_ref[:, cols])
            v = kv_ref[:, pl.ds(D_MODEL + h * XA_DH, XA_DH)]
            o_ref[:, cols] = jnp.dot(p.astype(CD), v, preferred_element_type=F32).astype(o_ref.dtype)

    return pl.pallas_call(
        body, name=name, grid=(T // tq,),
        in_specs=[pl.BlockSpec((tq, D_MODEL), lambda i: (i, 0)),
                  pl.BlockSpec((MEM_LEN, 2 * D_MODEL), lambda i: (0, 0))],
        out_specs=pl.BlockSpec((tq, D_MODEL), lambda i: (i, 0)),
        out_shape=jax.ShapeDtypeStruct((T, D_MODEL), CD),
        compiler_params=_params("parallel"),
    )(q, kv)


def _xattn_bwd(q, kv, do, *, name, tq=512):
    T = q.shape[0]
    tq = min(tq, T)

    def body(q_ref, kv_ref, do_ref, dq_ref, dkv_ref):
        @pl.when(pl.program_id(0) == 0)
        def _():
            dkv_ref[...] = jnp.zeros_like(dkv_ref)

        for h in range(XA_HEADS):
            cols = pl.ds(h * XA_DH, XA_DH)
            vcols = pl.ds(D_MODEL + h * XA_DH, XA_DH)
            qh, kh, vh, doh = q_ref[:, cols], kv_ref[:, cols], kv_ref[:, vcols], do_ref[:, cols]
            p = _xattn_probs(qh, kh)
            dkv_ref[:, vcols] += lax.dot_general(p.astype(CD), doh, TN, preferred_element_type=F32)
            dp = lax.dot_general(doh, vh, NT, preferred_element_type=F32)
            ds = (p * (dp - jnp.sum(dp * p, axis=-1, keepdims=True)) * XA_SCALE).astype(CD)
            dq_ref[:, cols] = jnp.dot(ds, kh, preferred_element_type=F32).astype(dq_ref.dtype)
            dkv_ref[:, cols] += lax.dot_general(ds, qh, TN, preferred_element_type=F32)

    return pl.pallas_call(
        body, name=name, grid=(T // tq,),
        in_specs=[pl.BlockSpec((tq, D_MODEL), lambda i: (i, 0)),
                  pl.BlockSpec((MEM_LEN, 2 * D_MODEL), lambda i: (0, 0)),
                  pl.BlockSpec((tq, D_MODEL), lambda i: (i, 0))],
        out_specs=[pl.BlockSpec((tq, D_MODEL), lambda i: (i, 0)),
                   pl.BlockSpec((MEM_LEN, 2 * D_MODEL), lambda i: (0, 0))],
        out_shape=[jax.ShapeDtypeStruct((T, D_MODEL), CD),
                   jax.ShapeDtypeStruct((MEM_LEN, 2 * D_MODEL), F32)],
        compiler_params=_params("arbitrary"),
    )(q, kv, do)


C_W = Q_LORA + KV_LORA + LANES


def _rot(x):
    lane = lax.broadcasted_iota(jnp.int32, x.shape, x.ndim - 1)
    up = pltpu.roll(x, LANES - QK_ROPE // 2, x.ndim - 1)
    dn = pltpu.roll(x, QK_ROPE // 2, x.ndim - 1)
    lo, mid, hi = QK_NOPE, QK_NOPE + QK_ROPE // 2, QK_NOPE + QK_ROPE
    return jnp.where((lane >= lo) & (lane < mid), -up, jnp.where((lane >= mid) & (lane < hi), dn, 0.0))


def _mla_mid_fwd(c, qg, kvg, cs, sn, *, name, tt=512):
    T = c.shape[0]
    tt = min(tt, T)

    def body(c_ref, qg_ref, kg_ref, cs_ref, sn_ref, qn_ref, kn_ref, kpe_ref):
        cq = c_ref[:, pl.ds(0, Q_LORA)]
        qn_ref[...] = (cq * lax.rsqrt(jnp.mean(cq * cq, axis=-1, keepdims=True) + EPS)
                       * qg_ref[...]).astype(qn_ref.dtype)
        ck = c_ref[:, pl.ds(Q_LORA, KV_LORA)]
        kn_ref[...] = (ck * lax.rsqrt(jnp.mean(ck * ck, axis=-1, keepdims=True) + EPS)
                       * kg_ref[...]).astype(kn_ref.dtype)
        kp = c_ref[:, pl.ds(Q_LORA + KV_LORA, LANES)]
        kpe_ref[...] = kp * cs_ref[...] + _rot(kp) * sn_ref[...]

    row = lambda w: pl.BlockSpec((tt, w), lambda i: (i, 0))
    one = lambda w: pl.BlockSpec((1, w), lambda i: (0, 0))
    return pl.pallas_call(
        body, name=name, grid=(T // tt,),
        in_specs=[row(C_W), one(Q_LORA), one(KV_LORA), row(LANES), row(LANES)],
        out_specs=[row(Q_LORA), row(KV_LORA), row(LANES)],
        out_shape=[jax.ShapeDtypeStruct((T, Q_LORA), CD), jax.ShapeDtypeStruct((T, KV_LORA), CD),
                   jax.ShapeDtypeStruct((T, LANES), F32)],
        compiler_params=_params("parallel"),
    )(c, qg, kvg, cs, sn)


def _mla_mid_bwd(c, dqn, dkvn, dksum, qg, kvg, cs, sn, *, name, tt=512):
    T = c.shape[0]
    tt = min(tt, T)

    def body(c_ref, dq_ref, dk_ref, ds_ref, qg_ref, kg_ref, cs_ref, sn_ref, dc_ref, dqg_ref, dkg_ref):
        @pl.when(pl.program_id(0) == 0)
        def _():
            dqg_ref[...] = jnp.zeros_like(dqg_ref)
            dkg_ref[...] = jnp.zeros_like(dkg_ref)

        dx, dg = _rms_bwd(c_ref[:, pl.ds(0, Q_LORA)], qg_ref[...], dq_ref[...])
        dc_ref[:, pl.ds(0, Q_LORA)] = dx.astype(dc_ref.dtype)
        dqg_ref[...] += jnp.sum(dg, axis=0, keepdims=True)
        dx, dg = _rms_bwd(c_ref[:, pl.ds(Q_LORA, KV_LORA)], kg_ref[...], dk_ref[...])
        dc_ref[:, pl.ds(Q_LORA, KV_LORA)] = dx.astype(dc_ref.dtype)
        dkg_ref[...] += jnp.sum(dg, axis=0, keepdims=True)
        d = ds_ref[...]
        lane = lax.broadcasted_iota(jnp.int32, d.shape, 1)
        dkp = d * cs_ref[...] - _rot(d * sn_ref[...])
        dc_ref[:, pl.ds(Q_LORA + KV_LORA, LANES)] = jnp.where(
            (lane >= QK_NOPE) & (lane < QK_NOPE + QK_ROPE), dkp, 0.0).astype(dc_ref.dtype)

    row = lambda w: pl.BlockSpec((tt, w), lambda i: (i, 0))
    one = lambda w: pl.BlockSpec((1, w), lambda i: (0, 0))
    return pl.pallas_call(
        body, name=name, grid=(T // tt,),
        in_specs=[row(C_W), row(Q_LORA), row(KV_LORA), row(LANES), one(Q_LORA), one(KV_LORA),
                  row(LANES), row(LANES)],
        out_specs=[row(C_W), one(Q_LORA), one(KV_LORA)],
        out_shape=[jax.ShapeDtypeStruct((T, C_W), CD), jax.ShapeDtypeStruct((1, Q_LORA), F32),
                   jax.ShapeDtypeStruct((1, KV_LORA), F32)],
        compiler_params=_params("arbitrary"),
    )(c, dqn, dkvn, dksum, qg, kvg, cs, sn)


def _mla_qkv_fwd(qn, kvn, kpe, cs, sn, wq, wk, wv, *, name, tt=512):
    T = qn.shape[0]
    tt = min(tt, T)
    H = MLA_HEADS

    def body(qn_ref, kn_ref, kpe_ref, cs_ref, sn_ref, wq_ref, wk_ref, wv_ref, q_ref, k_ref, v_ref):
        qn_v, kn_v, kpe_v, cs_v, sn_v = qn_ref[...], kn_ref[...], kpe_ref[...], cs_ref[...], sn_ref[...]
        for h in range(H):
            q = jnp.dot(qn_v, wq_ref[h], preferred_element_type=F32)
            q_ref[h] = (q * cs_v + _rot(q) * sn_v).astype(q_ref.dtype)
            k_ref[h] = (jnp.dot(kn_v, wk_ref[h], preferred_element_type=F32) + kpe_v).astype(k_ref.dtype)
            v_ref[h] = jnp.dot(kn_v, wv_ref[h], preferred_element_type=F32).astype(v_ref.dtype)

    row = lambda w: pl.BlockSpec((tt, w), lambda i: (i, 0))
    wsp = lambda k: pl.BlockSpec((H, k, LANES), lambda i: (0, 0, 0))
    hsp = pl.BlockSpec((H, tt, LANES), lambda i: (0, i, 0))
    sh = jax.ShapeDtypeStruct((H, T, LANES), CD)
    return pl.pallas_call(
        body, name=name, grid=(T // tt,),
        in_specs=[row(Q_LORA), row(KV_LORA), row(LANES), row(LANES), row(LANES),
                  wsp(Q_LORA), wsp(KV_LORA), wsp(KV_LORA)],
        out_specs=[hsp, hsp, hsp], out_shape=[sh, sh, sh],
        compiler_params=_params("parallel"),
    )(qn, kvn, kpe, cs, sn, wq, wk, wv)


def _mla_qkv_bwd(dq, dk, dv, qn, kvn, cs, sn, wq, wk, wv, *, name, tt=512):
    T = qn.shape[0]
    tt = min(tt, T)
    H = MLA_HEADS

    def body(dq_ref, dk_ref, dv_ref, qn_ref, kn_ref, cs_ref, sn_ref, wq_ref, wk_ref, wv_ref,
             dqn_ref, dkn_ref, dks_ref, dwq_ref, dwk_ref, dwv_ref):
        @pl.when(pl.program_id(0) == 0)
        def _():
            for r in (dwq_ref, dwk_ref, dwv_ref):
                r[...] = jnp.zeros_like(r)

        qn_v, kn_v, cs_v, sn_v = qn_ref[...], kn_ref[...], cs_ref[...], sn_ref[...]
        dqn = jnp.zeros((tt, Q_LORA), F32)
        dkn = jnp.zeros((tt, KV_LORA), F32)
        dks = jnp.zeros((tt, LANES), F32)
        for h in range(H):
            d = dq_ref[h]
            dqh = (d * cs_v - _rot(d * sn_v)).astype(CD)
            dkh, dvh = dk_ref[h], dv_ref[h]
            dqn = dqn + lax.dot_general(dqh, wq_ref[h], NT, preferred_element_type=F32)
            dkn = dkn + lax.dot_general(dkh, wk_ref[h], NT, preferred_element_type=F32)
            dkn = dkn + lax.dot_general(dvh, wv_ref[h], NT, preferred_element_type=F32)
            dks = dks + dkh.astype(F32)
            dwq_ref[h] += lax.dot_general(qn_v, dqh, TN, preferred_element_type=F32)
            dwk_ref[h] += lax.dot_general(kn_v, dkh, TN, preferred_element_type=F32)
            dwv_ref[h] += lax.dot_general(kn_v, dvh, TN, preferred_element_type=F32)
        dqn_ref[...] = dqn
        dkn_ref[...] = dkn
        dks_ref[...] = dks

    row = lambda w: pl.BlockSpec((tt, w), lambda i: (i, 0))
    wsp = lambda k: pl.BlockSpec((H, k, LANES), lambda i: (0, 0, 0))
    hsp = pl.BlockSpec((H, tt, LANES), lambda i: (0, i, 0))
    return pl.pallas_call(
        body, name=name, grid=(T // tt,),
        in_specs=[hsp, hsp, hsp, row(Q_LORA), row(KV_LORA), row(LANES), row(LANES),
                  wsp(Q_LORA), wsp(KV_LORA), wsp(KV_LORA)],
        out_specs=[row(Q_LORA), row(KV_LORA), row(LANES), wsp(Q_LORA), wsp(KV_LORA), wsp(KV_LORA)],
        out_shape=[jax.ShapeDtypeStruct((T, Q_LORA), F32), jax.ShapeDtypeStruct((T, KV_LORA), F32),
                   jax.ShapeDtypeStruct((T, LANES), F32),
                   jax.ShapeDtypeStruct((H, Q_LORA, LANES), F32),
                   jax.ShapeDtypeStruct((H, KV_LORA, LANES), F32),
                   jax.ShapeDtypeStruct((H, KV_LORA, LANES), F32)],
        compiler_params=_params("arbitrary"),
    )(dq, dk, dv, qn, kvn, cs, sn, wq, wk, wv)


FLASH_BLOCK = 1024
EXP2_SCALE = MLA_SCALE * math.log2(math.e)


def _causal_steps(nq, by_key):
    pairs = [(i, j) for j in range(nq) for i in range(j, nq)] if by_key else \
            [(i, j) for i in range(nq) for j in range(i + 1)]
    return (jnp.asarray([p[0] for p in pairs], jnp.int32), jnp.asarray([p[1] for p in pairs], jnp.int32))


def _raw_scores(q, k, masked, first_row=0):
    s = lax.dot_general(q, k, NT, preferred_element_type=F32)
    if masked:
        row = lax.broadcasted_iota(jnp.int32, s.shape, 0) + first_row
        col = lax.broadcasted_iota(jnp.int32, s.shape, 1)
        s = jnp.where(col <= row, s, NEG)
    return s


def _flash_fwd(q, k, v, *, name):
    H, T, _ = q.shape
    tq = min(FLASH_BLOCK, T)
    nq = T // tq
    i_tab, j_tab = _causal_steps(nq, by_key=False)

    rb = min(128, tq)

    def body(i_tab, j_tab, q_ref, k_ref, v_ref, o_ref, lse_ref, m_sc, l_sc, acc, s_sc, p_sc):
        t = pl.program_id(1)
        i, j = i_tab[t], j_tab[t]

        @pl.when(j == 0)
        def _():
            m_sc[...] = jnp.full_like(m_sc, NEG)
            l_sc[...] = jnp.zeros_like(l_sc)
            acc[...] = jnp.zeros_like(acc)

        hb = tq // 2

        def step(masked):
            lane = lax.broadcasted_iota(jnp.int32, (tq, LANES), 1)
            top, bot = pl.ds(0, hb), pl.ds(hb, hb)
            alphas, pvs = [], []
            for h in range(2):
                if masked:
                    s_sc[h, top, top] = _raw_scores(q_ref[h, top, :], k_ref[h, top, :], True)
                    s_sc[h, bot, :] = _raw_scores(q_ref[h, bot, :], k_ref[h], True, first_row=hb)
                    m_cur = jnp.concatenate([jnp.max(s_sc[h, top, top], axis=-1, keepdims=True),
                                             jnp.max(s_sc[h, bot, :], axis=-1, keepdims=True)], axis=0)
                else:
                    s_sc[h] = _raw_scores(q_ref[h], k_ref[h], False)
                    m_cur = jnp.max(s_sc[h], axis=-1, keepdims=True)
                m_prev = m_sc[h]
                m_new = jnp.maximum(m_prev, m_cur)
                alpha = jnp.exp2((m_prev - m_new) * EXP2_SCALE)
                m_sc[h] = m_new
                for r in range(tq // rb):
                    rows = pl.ds(r * rb, rb)
                    m_r = m_sc[h, rows, :]
                    part = jnp.zeros((rb, LANES), F32)
                    keys = hb if masked and r * rb < hb else tq
                    for c in range(keys // LANES):
                        cols = pl.ds(c * LANES, LANES)
                        p = jnp.exp2((s_sc[h, rows, cols] - m_r) * EXP2_SCALE)
                        part = part + p
                        p_sc[h, rows, cols] = p.astype(CD)
                    l_sc[h, rows, :] = (alpha[r * rb:(r + 1) * rb] * l_sc[h, rows, :]
                                        + jnp.sum(part, axis=-1, keepdims=True))
                alphas.append(alpha)
                if masked:
                    pvs.append(jnp.concatenate(
                        [jnp.dot(p_sc[h, top, top], v_ref[h, top, :], preferred_element_type=F32),
                         jnp.dot(p_sc[h, bot, :], v_ref[h], preferred_element_type=F32)], axis=0))
                else:
                    pvs.append(jnp.dot(p_sc[h], v_ref[h], preferred_element_type=F32))
            acc[...] = acc[...] * jnp.where(lane < V_HEAD, alphas[0], alphas[1]) + pvs[0] + pvs[1]

        @pl.when(j < i)
        def _():
            step(False)

        @pl.when(j == i)
        def _():
            step(True)
            lane = lax.broadcasted_iota(jnp.int32, (tq, LANES), 1)
            o_ref[...] = (acc[...] / jnp.where(lane < V_HEAD, l_sc[0], l_sc[1])).astype(o_ref.dtype)
            for h in range(2):
                lse_ref[h] = m_sc[h] * EXP2_SCALE + jnp.log2(l_sc[h])

    qsp = pl.BlockSpec((2, tq, LANES), lambda p, t, it, jt: (p, it[t], 0))
    ksp = pl.BlockSpec((2, tq, LANES), lambda p, t, it, jt: (p, jt[t], 0))
    return pl.pallas_call(
        body, name=name,
        grid_spec=pltpu.PrefetchScalarGridSpec(
            num_scalar_prefetch=2, grid=(H // 2, int(i_tab.shape[0])),
            in_specs=[qsp, ksp, ksp],
            out_specs=[pl.BlockSpec((tq, LANES), lambda p, t, it, jt: (it[t], p)), qsp],
            scratch_shapes=[pltpu.VMEM((2, tq, LANES), F32), pltpu.VMEM((2, tq, LANES), F32),
                            pltpu.VMEM((tq, LANES), F32),
                            pltpu.VMEM((2, tq, tq), F32), pltpu.VMEM((2, tq, tq), CD)]),
        out_shape=[jax.ShapeDtypeStruct((T, H * V_HEAD), CD), jax.ShapeDtypeStruct((H, T, LANES), F32)],
        compiler_params=_params("parallel", "arbitrary"),
    )(i_tab, j_tab, q, k, v)


def _flash_delta(o, do, *, name, tt=512):
    T = o.shape[0]
    tt = min(tt, T)
    H = MLA_HEADS

    def body(o_ref, do_ref, dl_ref):
        lane = lax.broadcasted_iota(jnp.int32, (tt, LANES), 1)
        for p in range(H // 2):
            cols = pl.ds(p * LANES, LANES)
            prod = do_ref[:, cols].astype(F32) * o_ref[:, cols].astype(F32)
            d0 = jnp.sum(jnp.where(lane < V_HEAD, prod, 0.0), axis=-1, keepdims=True)
            d1 = jnp.sum(jnp.where(lane < V_HEAD, 0.0, prod), axis=-1, keepdims=True)
            dl_ref[2 * p] = jnp.broadcast_to(d0, (tt, LANES))
            dl_ref[2 * p + 1] = jnp.broadcast_to(d1, (tt, LANES))

    row = pl.BlockSpec((tt, H * V_HEAD), lambda i: (i, 0))
    return pl.pallas_call(
        body, name=name, grid=(T // tt,), in_specs=[row, row],
        out_specs=pl.BlockSpec((H, tt, LANES), lambda i: (0, i, 0)),
        out_shape=jax.ShapeDtypeStruct((H, T, LANES), F32),
        compiler_params=_params("parallel"),
    )(o, do)


def _flash_bwd(q, k, v, do, lse, delta, *, name):
    H, T, _ = q.shape
    tq = min(FLASH_BLOCK, T)
    nq = T // tq
    i_tab, j_tab = _causal_steps(nq, by_key=True)

    def body(i_tab, j_tab, q_ref, k_ref, v_ref, do_ref, lse_ref, dl_ref, dq_ref, dk_ref, dv_ref, dk_acc, dv_acc):
        t = pl.program_id(1)
        i, j = i_tab[t], j_tab[t]
        rows = pl.ds(pl.multiple_of(i * tq, tq), tq)

        @pl.when(t == 0)
        def _():
            dq_ref[...] = jnp.zeros_like(dq_ref)

        def block(h, qr, kr, first_row, masked):
            qh, kh, vh, do_v = q_ref[h, qr, :], k_ref[h, kr, :], v_ref[h, kr, :], do_ref[qr, :]
            s = _raw_scores(qh, kh, masked, first_row)
            p = jnp.exp2(s * EXP2_SCALE - lse_ref[h, qr, :][:, :1])
            dv_acc[h, kr, :] += lax.dot_general(p.astype(CD), do_v, TN, preferred_element_type=F32)
            dp = lax.dot_general(do_v, vh, NT, preferred_element_type=F32)
            ds = (p * (dp - dl_ref[h, qr, :][:, :1]) * MLA_SCALE).astype(CD)
            dk_acc[h, kr, :] += lax.dot_general(ds, qh, TN, preferred_element_type=F32)
            dq_rows = pl.ds(pl.multiple_of(i * tq + qr.start, qr.size), qr.size)
            dq_ref[h, dq_rows, :] += jnp.dot(ds, kh, preferred_element_type=F32)

        def step(masked):
            hb = tq // 2
            for h in range(2):
                if masked:
                    block(h, pl.ds(0, hb), pl.ds(0, hb), 0, True)
                    block(h, pl.ds(hb, hb), pl.ds(0, tq), hb, True)
                else:
                    block(h, pl.ds(0, tq), pl.ds(0, tq), 0, False)

        @pl.when(i == j)
        def _():
            dk_acc[...] = jnp.zeros_like(dk_acc)
            dv_acc[...] = jnp.zeros_like(dv_acc)
            step(True)

        @pl.when(i > j)
        def _():
            step(False)

        @pl.when(i == nq - 1)
        def _():
            lane = lax.broadcasted_iota(jnp.int32, (tq, LANES), 1)
            dk_ref[...] = dk_acc[...].astype(dk_ref.dtype)
            dv_ref[0] = jnp.where(lane < V_HEAD, dv_acc[0], 0.0).astype(dv_ref.dtype)
            dv_ref[1] = jnp.where(lane < V_HEAD, 0.0, dv_acc[1]).astype(dv_ref.dtype)

    qsp = pl.BlockSpec((2, tq, LANES), lambda p, t, it, jt: (p, it[t], 0))
    ksp = pl.BlockSpec((2, tq, LANES), lambda p, t, it, jt: (p, jt[t], 0))
    osp = pl.BlockSpec((tq, LANES), lambda p, t, it, jt: (it[t], p))
    sh = jax.ShapeDtypeStruct((H, T, LANES), CD)
    return pl.pallas_call(
        body, name=name,
        grid_spec=pltpu.PrefetchScalarGridSpec(
            num_scalar_prefetch=2, grid=(H // 2, int(i_tab.shape[0])),
            in_specs=[qsp, ksp, ksp, osp, qsp, qsp],
            out_specs=[pl.BlockSpec((2, T, LANES), lambda p, t, it, jt: (p, 0, 0)), ksp, ksp],
            scratch_shapes=[pltpu.VMEM((2, tq, LANES), F32), pltpu.VMEM((2, tq, LANES), F32)]),
        out_shape=[jax.ShapeDtypeStruct((H, T, LANES), F32), sh, sh],
        compiler_params=_params("parallel", "arbitrary"),
    )(i_tab, j_tab, q, k, v, do, lse, delta)


def _loss_head(x, g, target, *, name, tt=512):
    T, D = x.shape
    tt = min(tt, T)

    def body(x_ref, g_ref, t_ref, dx_ref, dg_ref, loss_ref):
        @pl.when(pl.program_id(0) == 0)
        def _():
            dg_ref[...] = jnp.zeros_like(dg_ref)
            loss_ref[...] = jnp.zeros_like(loss_ref)

        xv, gv = x_ref[...], g_ref[...]
        r = lax.rsqrt(jnp.mean(xv * xv, axis=-1, keepdims=True) + EPS)
        err = xv * r * gv - t_ref[...]
        tok = jnp.mean(err * err, axis=-1, keepdims=True)
        loss_ref[...] += 0.5 * jnp.sum(tok, axis=0, keepdims=True)
        dx, dg_rows = _rms_bwd(xv, gv, err * (1.0 / D))
        dx_ref[...] = dx
        dg_ref[...] += jnp.sum(dg_rows, axis=0, keepdims=True)

    return pl.pallas_call(
        body, name=name, grid=(T // tt,),
        in_specs=[pl.BlockSpec((tt, D), lambda i: (i, 0)), pl.BlockSpec((1, D), lambda i: (0, 0)),
                  pl.BlockSpec((tt, D), lambda i: (i, 0))],
        out_specs=[pl.BlockSpec((tt, D), lambda i: (i, 0)), pl.BlockSpec((1, D), lambda i: (0, 0)),
                   pl.BlockSpec((1, LANES), lambda i: (0, 0))],
        out_shape=[jax.ShapeDtypeStruct((T, D), F32), jax.ShapeDtypeStruct((1, D), F32),
                   jax.ShapeDtypeStruct((1, LANES), F32)],
        compiler_params=_params("arbitrary"),
    )(x, g, target)


def _rope_tables(positions):
    inv = 1.0 / (ROPE_THETA ** (jnp.arange(0, QK_ROPE, 2, dtype=F32) / QK_ROPE))
    ang = positions.astype(F32)[:, None] * inv
    c, s = jnp.cos(ang), jnp.sin(ang)
    T = positions.shape[0]
    cs = jnp.concatenate([jnp.ones((T, QK_NOPE), F32), c, c, jnp.zeros((T, LANES - QK_NOPE - QK_ROPE), F32)], 1)
    sn = jnp.concatenate([jnp.zeros((T, QK_NOPE), F32), s, s, jnp.zeros((T, LANES - QK_NOPE - QK_ROPE), F32)], 1)
    return cs, sn


def _pad_rows(w, rows):
    return jnp.concatenate([w, jnp.zeros((rows - w.shape[0],) + w.shape[1:], w.dtype)], 0)


def _mla_weights(w_dq_dkv, w_uq, w_ukv):
    K = w_dq_dkv.shape[0]
    z = lambda n: jnp.zeros((K, n), w_dq_dkv.dtype)
    wc = jnp.concatenate([w_dq_dkv[:, :Q_LORA + KV_LORA], z(QK_NOPE), w_dq_dkv[:, Q_LORA + KV_LORA:],
                          z(LANES - QK_NOPE - QK_ROPE)], 1)
    wq = w_uq.reshape(Q_LORA, MLA_HEADS, QK_NOPE + QK_ROPE).transpose(1, 0, 2)
    wq = jnp.concatenate([wq, jnp.zeros((MLA_HEADS, Q_LORA, LANES - QK_NOPE - QK_ROPE), wq.dtype)], 2)
    wkv = w_ukv.reshape(KV_LORA, MLA_HEADS, QK_NOPE + V_HEAD).transpose(1, 0, 2)
    zero = jnp.zeros_like(wkv[:, :, :QK_NOPE])
    wk = jnp.concatenate([wkv[:, :, :QK_NOPE], zero], 2)
    wv_lo = jnp.concatenate([wkv[:, :, QK_NOPE:], zero], 2)
    wv_hi = jnp.concatenate([zero, wkv[:, :, QK_NOPE:]], 2)
    odd = (jnp.arange(MLA_HEADS) % 2 == 1)[:, None, None]
    wv = jnp.where(odd, wv_hi, wv_lo)
    return wc, wq, wk, wv


def _mla_weight_grads(dwc, dwq, dwk, dwv):
    d_dq = jnp.concatenate([dwc[:, :Q_LORA + KV_LORA],
                            dwc[:, Q_LORA + KV_LORA + QK_NOPE:Q_LORA + KV_LORA + QK_NOPE + QK_ROPE]], 1)
    d_uq = dwq[:, :, :QK_NOPE + QK_ROPE].transpose(1, 0, 2).reshape(Q_LORA, MLA_HEADS * (QK_NOPE + QK_ROPE))
    odd = (jnp.arange(MLA_HEADS) % 2 == 1)[:, None, None]
    dv = jnp.where(odd, dwv[:, :, V_HEAD:], dwv[:, :, :V_HEAD])
    d_ukv = jnp.concatenate([dwk[:, :, :QK_NOPE], dv], 2).transpose(1, 0, 2).reshape(
        KV_LORA, MLA_HEADS * (QK_NOPE + V_HEAD))
    return d_dq, d_uq, d_ukv


def _local_step(x, mem, positions, target, W):
    G = {}
    row = lambda v: v.reshape(1, -1)
    cs, sn = _rope_tables(positions)
    saved = []
    for l in range(DEPTH):
        L = f"l{l}"
        s = {"x0": x}
        if l % 2 == 0:
            e = l // 2
            s["z"], s["h"] = _nmm(x, row(W["norm_mix_g"][l]), (W["pc_w_in"], e), name=f"{L}_mix_in", out_dtype=F32)
            s["dw_w"] = _pad_rows(W["conv_dw_w"][e], CONV_K + 1)
            s["mix_p"] = (W["pool_w"][e], row(W["pool_scale"][e]), s["dw_w"], row(W["conv_dw_b"][e]),
                          row(W["conv_ln_g"][e]), row(W["conv_ln_b"][e]))
            s["ycat"] = _mixer_fwd(s["z"], *s["mix_p"], name=f"{L}_mix_mid")
            x = _mm_res(s["ycat"], (W["pc_w_out"], e), x, name=f"{L}_mix_out")
        else:
            o = l // 2
            wc, wq, wk, wv = _mla_weights(W["mla_w_dq_dkv"][o], W["mla_w_uq"][o], W["mla_w_ukv"][o])
            s["mla_w"] = (wc, wq, wk, wv)
            s["c"], s["h"] = _nmm(x, row(W["norm_mix_g"][l]), wc, name=f"{L}_mla_down", out_dtype=F32)
            s["qg"], s["kvg"] = row(W["mla_q_norm_g"][o]), row(W["mla_kv_norm_g"][o])
            s["qn"], s["kvn"], kpe = _mla_mid_fwd(s["c"], s["qg"], s["kvg"], cs, sn, name=f"{L}_mla_mid")
            s["q"], s["k"], s["v"] = _mla_qkv_fwd(s["qn"], s["kvn"], kpe, cs, sn, wq, wk, wv, name=f"{L}_mla_qkv")
            s["o"], s["lse"] = _flash_fwd(s["q"], s["k"], s["v"], name=f"{L}_mla_attn")
            x = _mm_res(s["o"], (W["mla_w_o"], o), x, name=f"{L}_mla_out")
        s["x1"] = x
        s["xq"], s["hx"] = _nmm(x, row(W["norm_xa_g"][l]), (W["xa_wq"], l), name=f"{L}_xa_q", out_dtype=CD)
        s["xkv"], s["hm"] = _nmm(mem, row(W["norm_mem_g"][l]), (W["xa_wkv"], l), name=f"{L}_xa_kv", out_dtype=CD)
        s["xo"] = _xattn_fwd(s["xq"], s["xkv"], name=f"{L}_xa_attn")
        x = _mm_res(s["xo"], (W["xa_wo"], l), x, name=f"{L}_xa_out")
        s["x2"] = x
        s["up"], s["hf"] = _nmm(x, row(W["norm_ffn_g"][l]), (W["ffn_w_up"], l), name=f"{L}_ffn_up", out_dtype=CD,
                                tn_target=1408)
        s["cw"], s["cb"] = _pad_rows(W["ffn_conv_w"][l], 8), row(W["ffn_conv_b"][l])
        s["act"] = _ffn_mid_fwd(s["up"], s["cw"], s["cb"], name=f"{L}_ffn_mid")
        x = _mm_res(s["act"], (W["ffn_w_down"], l), x, name=f"{L}_ffn_down")
        saved.append(s)
    dx, G["final_norm_g"], loss = _loss_head(x, row(W["final_norm_g"]), target, name="loss_head")
    G["final_norm_g"] = G["final_norm_g"].reshape(-1)

    per_layer = {}

    def put(name, l, val):
        per_layer.setdefault(name, {})[l] = val

    for l in reversed(range(DEPTH)):
        L = f"l{l}"
        s = saved[l]
        put("ffn_w_down", l, _mm_tn(s["act"], dx, name=f"{L}_ffn_down_dw", tk_target=1408))
        dact = _mm_nt(dx, (W["ffn_w_down"], l), name=f"{L}_ffn_down_dx", out_dtype=CD, tn_target=1408)
        dup, dcw, dcb = _ffn_mid_bwd(s["up"], dact, s["cw"], s["cb"], name=f"{L}_ffn_mid_bwd")
        put("ffn_conv_w", l, dcw[:FFN_K])
        put("ffn_conv_b", l, dcb[0])
        put("ffn_w_up", l, _mm_tn(s["hf"], dup, name=f"{L}_ffn_up_dw", tn_target=1408))
        dx, dg = _mm_nt_normbwd(dup, (W["ffn_w_up"], l), s["x2"], row(W["norm_ffn_g"][l]), dx, name=f"{L}_ffn_up_dx")
        put("norm_ffn_g", l, dg[0])
        put("xa_wo", l, _mm_tn(s["xo"], dx, name=f"{L}_xa_out_dw"))
        do = _mm_nt(dx, (W["xa_wo"], l), name=f"{L}_xa_out_dx", out_dtype=CD)
        dq, dkv = _xattn_bwd(s["xq"], s["xkv"], do, name=f"{L}_xa_attn_bwd")
        put("xa_wq", l, _mm_tn(s["hx"], dq, name=f"{L}_xa_q_dw"))
        dx, dg = _mm_nt_normbwd(dq, (W["xa_wq"], l), s["x1"], row(W["norm_xa_g"][l]), dx, name=f"{L}_xa_q_dx")
        put("norm_xa_g", l, dg[0])
        put("xa_wkv", l, _mm_tn(s["hm"], dkv, name=f"{L}_xa_kv_dw", tt=MEM_LEN))
        _, dg = _mm_nt_normbwd(dkv, (W["xa_wkv"], l), mem, row(W["norm_mem_g"][l]), jnp.zeros_like(mem),
                               name=f"{L}_xa_kv_dx", tm=MEM_LEN)
        put("norm_mem_g", l, dg[0])
        if l % 2 == 0:
            e = l // 2
            put("pc_w_out", e, _mm_tn(s["ycat"], dx, name=f"{L}_mix_out_dw"))
            dy = _mm_nt(dx, (W["pc_w_out"], e), name=f"{L}_mix_out_dx", out_dtype=F32)
            dz, dpw, dps, ddw, ddb, dlg, dlb = _mixer_bwd(s["z"], dy, *s["mix_p"], name=f"{L}_mix_mid_bwd")
            put("pool_w", e, dpw)
            put("pool_scale", e, dps[0])
            put("conv_dw_w", e, ddw[:CONV_K])
            put("conv_dw_b", e, ddb[0])
            put("conv_ln_g", e, dlg[0])
            put("conv_ln_b", e, dlb[0])
            put("pc_w_in", e, _mm_tn(s["h"], dz, name=f"{L}_mix_in_dw"))
            dx, dg = _mm_nt_normbwd(dz, (W["pc_w_in"], e), s["x0"], row(W["norm_mix_g"][l]), dx, name=f"{L}_mix_in_dx")
        else:
            o = l // 2
            wc, wq, wk, wv = s["mla_w"]
            put("mla_w_o", o, _mm_tn(s["o"], dx, name=f"{L}_mla_out_dw"))
            do = _mm_nt(dx, (W["mla_w_o"], o), name=f"{L}_mla_out_dx", out_dtype=CD)
            delta = _flash_delta(s["o"], do, name=f"{L}_mla_attn_delta")
            dq, dk, dv = _flash_bwd(s["q"], s["k"], s["v"], do, s["lse"], delta, name=f"{L}_mla_attn_bwd")
            dqn, dkvn, dks, dwq, dwk, dwv = _mla_qkv_bwd(dq, dk, dv, s["qn"], s["kvn"], cs, sn, wq, wk, wv,
                                                         name=f"{L}_mla_qkv_bwd")
            dc, dqg, dkg = _mla_mid_bwd(s["c"], dqn, dkvn, dks, s["qg"], s["kvg"], cs, sn, name=f"{L}_mla_mid_bwd")
            put("mla_q_norm_g", o, dqg[0])
            put("mla_kv_norm_g", o, dkg[0])
            dwc = _mm_tn(s["h"], dc, name=f"{L}_mla_down_dw")
            d_dq, d_uq, d_ukv = _mla_weight_grads(dwc, dwq, dwk, dwv)
            put("mla_w_dq_dkv", o, d_dq)
            put("mla_w_uq", o, d_uq)
            put("mla_w_ukv", o, d_ukv)
            dx, dg = _mm_nt_normbwd(dc, wc, s["x0"], row(W["norm_mix_g"][l]), dx, name=f"{L}_mla_down_dx",
                                    tk_target=768)
        put("norm_mix_g", l, dg[0])
    for name, d in per_layer.items():
        G[name] = jnp.stack([d[i] for i in sorted(d)], 0)
    return loss, dx, G


_ANY = pl.BlockSpec(memory_space=pl.ANY)


def _all_gather(xs, *, name):
    n = len(xs)

    def body(*refs):
        x_refs, out_refs = refs[:n], refs[n:2 * n]
        send_sems, recv_sems, local_sems = refs[2 * n:]
        mx, my, mc = lax.axis_index("x"), lax.axis_index("y"), lax.axis_index("c")
        me, sibling = (mx, my, mc), (mx, my, 1 - mc)
        xn, yn, dg = (1 - mx, my), (mx, 1 - my), (1 - mx, 1 - my)
        src = (mx + (1 - mc) * (1 - 2 * mx), my + mc * (1 - 2 * my))
        dst = (mx + mc * (1 - 2 * mx), my + (1 - mc) * (1 - 2 * my))
        SIB, XN, YN, DG, PASS = 0, 1, 2, 3, 4

        def copy(a, k, block, to, own=False):
            px, py, pc = block
            slot = out_refs[a].at[4 * px + 2 * py + pc]
            return pltpu.make_async_remote_copy(
                src_ref=x_refs[a] if own else slot, dst_ref=slot,
                send_sem=send_sems.at[7 * a + k], recv_sem=recv_sems.at[7 * a + k],
                device_id=to, device_id_type=MESH)

        mine = [pltpu.make_async_copy(x_refs[a], out_refs[a].at[4 * mx + 2 * my + mc], local_sems.at[a])
                for a in range(n)]
        for cp in mine:
            cp.start()
        sent = [copy(a, XN, me, (*xn, mc), own=True) for a in range(n)]
        sent += [copy(a, YN, me, (*yn, mc), own=True) for a in range(n)]
        sent += [copy(a, SIB, me, sibling, own=True) for a in range(n)]
        for cp in sent:
            cp.start()
        for a in range(n):
            for k, chip in ((XN, xn), (YN, yn)):
                copy(a, k, (*chip, mc), me).wait_recv()
                sent.append(copy(a, PASS + k - 1, (*chip, mc), sibling))
                sent[-1].start()
            sent.append(copy(a, DG, (*src, mc), (*dst, mc)))
            sent[-1].start()
        for a in range(n):
            copy(a, DG, (*dg, mc), me).wait_recv()
            sent.append(copy(a, PASS + DG - 1, (*dg, mc), sibling))
            sent[-1].start()
        for a in range(n):
            copy(a, SIB, sibling, me).wait_recv()
            for k, chip in ((XN, xn), (YN, yn), (DG, dg)):
                copy(a, PASS + k - 1, (*chip, 1 - mc), me).wait_recv()
        for cp in sent:
            cp.wait_send()
        for cp in mine:
            cp.wait()

    return pl.pallas_call(
        body, name=name, in_specs=[_ANY] * n, out_specs=[_ANY] * n,
        out_shape=[jax.ShapeDtypeStruct((N_DEV,) + x.shape, x.dtype) for x in xs],
        scratch_shapes=[pltpu.SemaphoreType.DMA((7 * n,)), pltpu.SemaphoreType.DMA((7 * n,)),
                        pltpu.SemaphoreType.DMA((n,))],
    )(*xs)


N_CHIP = 4


def _pair_exchange(ps, *, name):
    n = len(ps)

    def body(*refs):
        p_refs, out_refs = refs[:n], refs[n:2 * n]
        send_sems, recv_sems = refs[2 * n:]
        mx, my, mc = lax.axis_index("x"), lax.axis_index("y"), lax.axis_index("c")
        copies = []
        for a in range(n):
            for chip in range(N_CHIP):
                copies.append(pltpu.make_async_remote_copy(
                    src_ref=p_refs[a].at[2 * chip + (1 - mc)], dst_ref=out_refs[a].at[chip],
                    send_sem=send_sems.at[N_CHIP * a + chip], recv_sem=recv_sems.at[N_CHIP * a + chip],
                    device_id=(mx, my, 1 - mc), device_id_type=MESH))
        for cp in copies:
            cp.start()
        for cp in copies:
            cp.wait()

    return pl.pallas_call(
        body, name=name, in_specs=[_ANY] * n, out_specs=[_ANY] * n,
        out_shape=[jax.ShapeDtypeStruct((N_CHIP,) + p.shape[1:], p.dtype) for p in ps],
        scratch_shapes=[pltpu.SemaphoreType.DMA((N_CHIP * n,)), pltpu.SemaphoreType.DMA((N_CHIP * n,))],
    )(*ps)


def _pair_sum(p, recv, core, *, name):
    _, R, C = p.shape
    tr = _row_tile(R, C, 4 * ROW_TILE_ELEMS)
    p4 = p.reshape(N_CHIP, 2, R, C)

    def body(core_ref, a_ref, b_ref, o_ref):
        o_ref[...] = (a_ref[...].astype(F32) + b_ref[...].astype(F32)).astype(o_ref.dtype)

    return pl.pallas_call(
        body, name=name,
        grid_spec=pltpu.PrefetchScalarGridSpec(
            num_scalar_prefetch=1, grid=(N_CHIP, R // tr),
            in_specs=[pl.BlockSpec((None, None, tr, C), lambda ch, i, core: (ch, core[0], i, 0)),
                      pl.BlockSpec((None, tr, C), lambda ch, i, core: (ch, i, 0))],
            out_specs=pl.BlockSpec((None, tr, C), lambda ch, i, core: (ch, i, 0))),
        out_shape=jax.ShapeDtypeStruct((N_CHIP, R, C), p.dtype),
        compiler_params=_params("parallel", "parallel"),
    )(core, p4, recv)


def _chip_exchange(ss, *, name):
    n = len(ss)

    def body(*refs):
        s_refs, out_refs, stage_refs = refs[:n], refs[n:2 * n], refs[2 * n:3 * n]
        send_sems, recv_sems, local_sems = refs[3 * n:]
        mx, my, mc = lax.axis_index("x"), lax.axis_index("y"), lax.axis_index("c")
        chip = 2 * mx + my
        xn, yn, dg = (1 - mx, my), (mx, 1 - my), (1 - mx, 1 - my)
        via = (mx + (1 - mc) * (1 - 2 * mx), my + mc * (1 - 2 * my))
        onward = (mx + mc * (1 - 2 * mx), my + (1 - mc) * (1 - 2 * my))
        XN, YN, STAGE, ONWARD = 0, 1, 2, 3

        def copy(a, k, src, dst, to):
            return pltpu.make_async_remote_copy(
                src_ref=src, dst_ref=dst, send_sem=send_sems.at[4 * a + k], recv_sem=recv_sems.at[4 * a + k],
                device_id=(*to, mc), device_id_type=MESH)

        def slot(ref, c):
            return ref.at[2 * c[0] + c[1]]

        mine = [pltpu.make_async_copy(s_refs[a].at[chip], out_refs[a].at[chip], local_sems.at[a]) for a in range(n)]
        for cp in mine:
            cp.start()
        first = []
        for a in range(n):
            first.append(copy(a, STAGE, slot(s_refs[a], dg), stage_refs[a], via))
            first.append(copy(a, XN, slot(s_refs[a], xn), out_refs[a].at[chip], xn))
            first.append(copy(a, YN, slot(s_refs[a], yn), out_refs[a].at[chip], yn))
        for cp in first:
            cp.start()
        onwards = []
        for a in range(n):
            first[3 * a].wait_recv()
            onwards.append(copy(a, ONWARD, stage_refs[a], slot(out_refs[a], via), onward))
            onwards[-1].start()
        for a in range(n):
            first[3 * a + 1].wait_recv()
            first[3 * a + 2].wait_recv()
            onwards[a].wait_recv()
        for cp in first + onwards:
            cp.wait_send()
        for cp in mine:
            cp.wait()

    outs = pl.pallas_call(
        body, name=name, in_specs=[_ANY] * n, out_specs=[_ANY] * (2 * n),
        out_shape=[jax.ShapeDtypeStruct(s.shape, s.dtype) for s in ss]
                  + [jax.ShapeDtypeStruct(s.shape[1:], s.dtype) for s in ss],
        scratch_shapes=[pltpu.SemaphoreType.DMA((4 * n,)), pltpu.SemaphoreType.DMA((4 * n,)),
                        pltpu.SemaphoreType.DMA((n,))],
    )(*ss)
    return outs[:n]


ROW_TILE_ELEMS = 256 * 1024


def _row_tile(R, C, elems=None):
    elems = ROW_TILE_ELEMS if elems is None else elems
    for t in (4096, 2048, 1024, 512, 256, 128, 64, 32, 16):
        if R % t == 0 and t * C <= elems:
            return t
    raise ValueError((R, C))


def _sum_slots(gs, *, name):
    S, R, C = gs.shape
    tr = _row_tile(R, C)

    def body(g_ref, o_ref):
        g = g_ref[0].astype(F32)
        for s in range(1, S):
            g = g + g_ref[s].astype(F32)
        o_ref[...] = g

    return pl.pallas_call(
        body, name=name, grid=(R // tr,),
        in_specs=[pl.BlockSpec((S, tr, C), lambda i: (0, i, 0))],
        out_specs=pl.BlockSpec((tr, C), lambda i: (i, 0)),
        out_shape=jax.ShapeDtypeStruct((R, C), F32),
        compiler_params=_params("parallel"),
    )(gs)


def _adamw(gs, w, m, v, *, name):
    S, R, C = gs.shape
    tr = _row_tile(R, C, 2 * ROW_TILE_ELEMS)

    def body(g_ref, w_ref, m_ref, v_ref, g_out, d_out, m_out, v_out):
        g = g_ref[0].astype(F32)
        for s in range(1, S):
            g = g + g_ref[s].astype(F32)
        m_new = ADAM_B1 * m_ref[...] + (1.0 - ADAM_B1) * g
        v_new = ADAM_B2 * v_ref[...] + (1.0 - ADAM_B2) * (g * g)
        m_hat = m_new / (1.0 - ADAM_B1 ** ADAM_STEP)
        v_hat = v_new / (1.0 - ADAM_B2 ** ADAM_STEP)
        g_out[...] = g
        d_out[...] = -ADAM_LR * (m_hat / (jnp.sqrt(v_hat) + ADAM_EPS) + ADAM_WD * w_ref[...])
        m_out[...] = m_new
        v_out[...] = v_new

    blk = pl.BlockSpec((tr, C), lambda i: (i, 0))
    sh = jax.ShapeDtypeStruct((R, C), F32)
    return pl.pallas_call(
        body, name=name, grid=(R // tr,),
        in_specs=[pl.BlockSpec((S, tr, C), lambda i: (0, i, 0)), blk, blk, blk],
        out_specs=[blk, blk, blk, blk], out_shape=[sh, sh, sh, sh],
        compiler_params=_params("parallel"),
    )(gs, w, m, v)


PIECE = 16 * LANES


def _pack(arrs, dtype, lead, row_mult):
    lead_shape = arrs[0].shape[:lead]
    parts, meta, off = [], [], 0
    for a in arrs:
        size = math.prod(a.shape[lead:])
        padded = -(-size // PIECE) * PIECE
        flat = a.astype(dtype).reshape(lead_shape + (size,))
        if padded != size:
            flat = jnp.concatenate([flat, jnp.zeros(lead_shape + (padded - size,), dtype)], -1)
        parts.append(flat)
        meta.append((off, size, a.shape[lead:]))
        off += padded
    total = -(-off // (row_mult * LANES)) * (row_mult * LANES)
    if total != off:
        parts.append(jnp.zeros(lead_shape + (total - off,), dtype))
    return jnp.concatenate(parts, -1).reshape(lead_shape + (total // LANES, LANES)), meta


def _unpack(packed, meta, lead):
    lead_shape = packed.shape[:lead]
    flat = packed.reshape(lead_shape + (-1,))
    return [flat[..., off:off + size].reshape(lead_shape + shape) for off, size, shape in meta]


ARG_NAMES = ['x', 'mem', 'positions', 'norm_mix_g', 'norm_xa_g', 'norm_mem_g', 'xa_wq', 'xa_wkv', 'xa_wo', 'norm_ffn_g', 'ffn_w_up', 'ffn_conv_w', 'ffn_conv_b', 'ffn_w_down', 'pc_w_in', 'pool_w', 'pool_scale', 'conv_dw_w', 'conv_dw_b', 'conv_ln_g', 'conv_ln_b', 'pc_w_out', 'mla_w_dq_dkv', 'mla_q_norm_g', 'mla_w_uq', 'mla_kv_norm_g', 'mla_w_ukv', 'mla_w_o', 'final_norm_g', 'loss_target']
WEIGHTS = ARG_NAMES[3:29]
BIG = {'xa_wq': 1, 'xa_wkv': 2, 'xa_wo': 1, 'ffn_w_up': 2, 'ffn_w_down': 1, 'pc_w_in': 2, 'pc_w_out': 1,
       'mla_w_dq_dkv': 1, 'mla_w_uq': 2, 'mla_w_ukv': 2, 'mla_w_o': 1}
SMALL_SHARDED = {'ffn_conv_w': 2, 'conv_dw_w': 2, 'mla_q_norm_g': 1, 'mla_kv_norm_g': 1}
REPLICATED = [n for n in WEIGHTS if n not in BIG and n not in SMALL_SHARDED]


def _from_slots(g, axis):
    t = jnp.moveaxis(g, 0, axis)
    return t.reshape(t.shape[:axis] + (t.shape[axis] * t.shape[axis + 1],) + t.shape[axis + 2:])


def _to_slots(full, axis):
    n = full.shape[axis] // N_DEV
    t = full.reshape(full.shape[:axis] + (N_DEV, n) + full.shape[axis + 1:])
    return jnp.moveaxis(t, axis, 0)


def kernel(x, mem, positions, norm_mix_g, norm_xa_g, norm_mem_g, xa_wq, xa_wkv, xa_wo, norm_ffn_g, ffn_w_up, ffn_conv_w, ffn_conv_b, ffn_w_down, pc_w_in, pool_w, pool_scale, conv_dw_w, conv_dw_b, conv_ln_g, conv_ln_b, pc_w_out, mla_w_dq_dkv, mla_q_norm_g, mla_w_uq, mla_kv_norm_g, mla_w_ukv, mla_w_o, final_norm_g, loss_target, m_norm_mix_g, m_norm_xa_g, m_norm_mem_g, m_xa_wq, m_xa_wkv, m_xa_wo, m_norm_ffn_g, m_ffn_w_up, m_ffn_conv_w, m_ffn_conv_b, m_ffn_w_down, m_pc_w_in, m_pool_w, m_pool_scale, m_conv_dw_w, m_conv_dw_b, m_conv_ln_g, m_conv_ln_b, m_pc_w_out, m_mla_w_dq_dkv, m_mla_q_norm_g, m_mla_w_uq, m_mla_kv_norm_g, m_mla_w_ukv, m_mla_w_o, m_final_norm_g, v_norm_mix_g, v_norm_xa_g, v_norm_mem_g, v_xa_wq, v_xa_wkv, v_xa_wo, v_norm_ffn_g, v_ffn_w_up, v_ffn_conv_w, v_ffn_conv_b, v_ffn_w_down, v_pc_w_in, v_pool_w, v_pool_scale, v_conv_dw_w, v_conv_dw_b, v_conv_ln_g, v_conv_ln_b, v_pc_w_out, v_mla_w_dq_dkv, v_mla_q_norm_g, v_mla_w_uq, v_mla_kv_norm_g, v_mla_w_ukv, v_mla_w_o, v_final_norm_g):
    args = (x, mem, positions, norm_mix_g, norm_xa_g, norm_mem_g, xa_wq, xa_wkv, xa_wo, norm_ffn_g, ffn_w_up, ffn_conv_w, ffn_conv_b, ffn_w_down, pc_w_in, pool_w, pool_scale, conv_dw_w, conv_dw_b, conv_ln_g, conv_ln_b, pc_w_out, mla_w_dq_dkv, mla_q_norm_g, mla_w_uq, mla_kv_norm_g, mla_w_ukv, mla_w_o, final_norm_g, loss_target)
    a = dict(zip(ARG_NAMES, args))
    mom = dict(zip(WEIGHTS, (m_norm_mix_g, m_norm_xa_g, m_norm_mem_g, m_xa_wq, m_xa_wkv, m_xa_wo, m_norm_ffn_g, m_ffn_w_up, m_ffn_conv_w, m_ffn_conv_b, m_ffn_w_down, m_pc_w_in, m_pool_w, m_pool_scale, m_conv_dw_w, m_conv_dw_b, m_conv_ln_g, m_conv_ln_b, m_pc_w_out, m_mla_w_dq_dkv, m_mla_q_norm_g, m_mla_w_uq, m_mla_kv_norm_g, m_mla_w_ukv, m_mla_w_o, m_final_norm_g)))
    var = dict(zip(WEIGHTS, (v_norm_mix_g, v_norm_xa_g, v_norm_mem_g, v_xa_wq, v_xa_wkv, v_xa_wo, v_norm_ffn_g, v_ffn_w_up, v_ffn_conv_w, v_ffn_conv_b, v_ffn_w_down, v_pc_w_in, v_pool_w, v_pool_scale, v_conv_dw_w, v_conv_dw_b, v_conv_ln_g, v_conv_ln_b, v_pc_w_out, v_mla_w_dq_dkv, v_mla_q_norm_g, v_mla_w_uq, v_mla_kv_norm_g, v_mla_w_ukv, v_mla_w_o, v_final_norm_g)))
    me = 4 * lax.axis_index("x") + 2 * lax.axis_index("y") + lax.axis_index("c")

    big_all = _all_gather([a[n].astype(CD) for n in BIG], name="gather_weights")
    sm_pack, sm_meta = _pack([a[n] for n in SMALL_SHARDED], F32, 0, 8)
    sm_all = _unpack(_all_gather([sm_pack], name="gather_small")[0], sm_meta, 1)
    W = {n: a[n] for n in REPLICATED}
    for (n, ax), g in zip(BIG.items(), big_all):
        W[n] = _from_slots(g, ax)
    for (n, ax), g in zip(SMALL_SHARDED.items(), sm_all):
        W[n] = _from_slots(g, ax)

    loss, dx, G = _local_step(x[0], mem[0], positions[0], loss_target[0], W)

    parts = [_to_slots(G[n], ax).astype(CD) for n, ax in BIG.items()]
    from_sibling = _pair_exchange(parts, name="grads_to_sibling")
    core = lax.axis_index("c").astype(jnp.int32).reshape(1)
    sums = []
    for n, p, r in zip(BIG, parts, from_sibling):
        cols = p.shape[-1]
        s = _pair_sum(p.reshape(N_DEV, -1, cols), r.reshape(N_CHIP, -1, cols), core, name=f"pair_sum_{n}")
        sums.append(s.reshape((N_CHIP,) + p.shape[1:]))
    recv = _chip_exchange(sums, name="scatter_grads")
    out = {}
    for n, r in zip(BIG, recv):
        shape = a[n].shape
        rows = lambda t: t.reshape(-1, shape[-1])
        res = _adamw(r.reshape(N_CHIP, -1, shape[-1]), rows(a[n]), rows(mom[n]), rows(var[n]), name=f"adamw_{n}")
        out[n] = tuple(t.reshape(shape) for t in res)

    small_names = REPLICATED + list(SMALL_SHARDED)
    spack, smeta = _pack([G[n] for n in small_names] + [loss], F32, 0, 256)
    stot = _unpack(_sum_slots(_all_gather([spack], name="gather_small_grads")[0], name="sum_small_grads"), smeta, 0)
    loss_total = stot[-1][0, 0]
    gsm = dict(zip(small_names, stot[:-1]))
    for n, ax in SMALL_SHARDED.items():
        width = a[n].shape[ax]
        gsm[n] = lax.dynamic_slice_in_dim(gsm[n], me * width, width, ax)
    g1, meta1 = _pack([gsm[n] for n in small_names], F32, 0, 256)
    w1, _ = _pack([a[n] for n in small_names], F32, 0, 256)
    m1, _ = _pack([mom[n] for n in small_names], F32, 0, 256)
    v1, _ = _pack([var[n] for n in small_names], F32, 0, 256)
    res = [_unpack(r, meta1, 0) for r in _adamw(g1[None], w1, m1, v1, name="adamw_small")]
    for i, n in enumerate(small_names):
        out[n] = tuple(r[i] for r in res)

    return (loss_total, dx[None],
            *[out[n][0] for n in WEIGHTS], *[out[n][1] for n in WEIGHTS],
            *[out[n][2] for n in WEIGHTS], *[out[n][3] for n in WEIGHTS])
```

```python
import functools
import math

import jax
import jax.numpy as jnp
from jax import lax
from jax.experimental import pallas as pl
from jax.experimental.pallas import tpu as pltpu

F32 = jnp.float32
CD = jnp.bfloat16
EPS = 1e-6
NEG = -1e30
N_DEV = 8
LANES = 128
HALO = 32

D_MODEL = 1024
DEPTH = 4
XA_HEADS = 4
XA_DH = 256
MEM_LEN = 256
POOL_WINDOWS = (2, 4, 8, 16)
CONV_K = 31
FFN_K = 3
D_FF = 2816
MLA_HEADS = 16
QK_NOPE = 64
QK_ROPE = 32
V_HEAD = 64
Q_LORA = 384
KV_LORA = 256
ROPE_THETA = 10000.0
MLA_SCALE = 1.0 / math.sqrt(QK_NOPE + QK_ROPE)
XA_SCALE = XA_DH ** -0.5

ADAM_LR = 0.001
ADAM_B1 = 0.9
ADAM_B2 = 0.999
ADAM_EPS = 1e-08
ADAM_WD = 0.01
ADAM_STEP = 10

NT = (((1,), (1,)), ((), ()))
TN = (((0,), (0,)), ((), ()))
MESH = pl.DeviceIdType.MESH


def _tile(n, target):
    if n <= target:
        return n
    best = None
    for t in range(LANES, target + 1, LANES):
        if n % t == 0:
            best = t
    assert best is not None, (n, target)
    return best


def _params(*sem):
    return pltpu.CompilerParams(dimension_semantics=sem)


def _sigmoid(v):
    return 0.5 * jnp.tanh(0.5 * v) + 0.5


def _rms_bwd(x, gain, dh):
    r = lax.rsqrt(jnp.mean(x * x, axis=-1, keepdims=True) + EPS)
    xhat = x * r
    dxhat = dh * gain
    dx = r * (dxhat - xhat * jnp.mean(dxhat * xhat, axis=-1, keepdims=True))
    return dx, dh * xhat


def _weight(w):
    if not isinstance(w, tuple):
        return w, w.shape, pl.BlockSpec
    arr, layer = w

    def spec(block, imap):
        return pl.BlockSpec((None,) + tuple(block), lambda *a: (layer,) + tuple(imap(*a)))

    return arr, arr.shape[1:], spec


def _nmm(x, g, w, *, name, out_dtype, tm=1024, tn_target=1024):
    M, K = x.shape
    w, (_, N), wspec = _weight(w)
    tm = min(tm, M)
    tn = _tile(N, tn_target)

    def body(x_ref, g_ref, w_ref, z_ref, h_ref):
        @pl.when(pl.program_id(1) == 0)
        def _():
            xf = x_ref[...]
            r = lax.rsqrt(jnp.mean(xf * xf, axis=-1, keepdims=True) + EPS)
            h_ref[...] = (xf * r * g_ref[...]).astype(h_ref.dtype)

        z_ref[...] = jnp.dot(h_ref[...], w_ref[...], preferred_element_type=F32).astype(z_ref.dtype)

    return pl.pallas_call(
        body, name=name, grid=(M // tm, N // tn),
        in_specs=[pl.BlockSpec((tm, K), lambda i, j: (i, 0)),
                  pl.BlockSpec((1, K), lambda i, j: (0, 0)),
                  wspec((K, tn), lambda i, j: (0, j))],
        out_specs=[pl.BlockSpec((tm, tn), lambda i, j: (i, j)),
                   pl.BlockSpec((tm, K), lambda i, j: (i, 0))],
        out_shape=[jax.ShapeDtypeStruct((M, N), out_dtype), jax.ShapeDtypeStruct((M, K), CD)],
        compiler_params=_params("parallel", "arbitrary"),
    )(x, g, w)


def _mm_res(a, w, res, *, name, tm=1024, tn_target=1024):
    M, K = a.shape
    w, (_, N), wspec = _weight(w)
    tm = min(tm, M)
    tn = _tile(N, tn_target)

    def body(a_ref, w_ref, r_ref, o_ref):
        o_ref[...] = r_ref[...] + jnp.dot(a_ref[...].astype(CD), w_ref[...], preferred_element_type=F32)

    return pl.pallas_call(
        body, name=name, grid=(M // tm, N // tn),
        in_specs=[pl.BlockSpec((tm, K), lambda i, j: (i, 0)),
                  wspec((K, tn), lambda i, j: (0, j)),
                  pl.BlockSpec((tm, tn), lambda i, j: (i, j))],
        out_specs=pl.BlockSpec((tm, tn), lambda i, j: (i, j)),
        out_shape=jax.ShapeDtypeStruct((M, N), F32),
        compiler_params=_params("parallel", "arbitrary"),
    )(a, w, res)


def _mm_nt(a, w, *, name, out_dtype, tm=1024, tn_target=1024):
    M, K = a.shape
    w, (N, _), wspec = _weight(w)
    tm = min(tm, M)
    tn = _tile(N, tn_target)

    def body(a_ref, w_ref, o_ref):
        o_ref[...] = lax.dot_general(a_ref[...].astype(CD), w_ref[...], NT,
                                     preferred_element_type=F32).astype(o_ref.dtype)

    return pl.pallas_call(
        body, name=name, grid=(M // tm, N // tn),
        in_specs=[pl.BlockSpec((tm, K), lambda i, j: (i, 0)),
                  wspec((tn, K), lambda i, j: (j, 0))],
        out_specs=pl.BlockSpec((tm, tn), lambda i, j: (i, j)),
        out_shape=jax.ShapeDtypeStruct((M, N), out_dtype),
        compiler_params=_params("parallel", "arbitrary"),
    )(a, w)


def _mm_nt_normbwd(gy, w, x, gain, dres, *, name, tm=1024, tk_target=1408):
    M, K = gy.shape
    w, (D, _), wspec = _weight(w)
    tm = min(tm, M)
    tk = _tile(K, tk_target)
    nk = K // tk

    def body(g_ref, w_ref, x_ref, gain_ref, dres_ref, dx_ref, dg_ref, acc):
        i, k = pl.program_id(0), pl.program_id(1)

        @pl.when(k == 0)
        def _():
            acc[...] = jnp.zeros_like(acc)

        acc[...] += lax.dot_general(g_ref[...].astype(CD), w_ref[...], NT, preferred_element_type=F32)

        @pl.when(k == nk - 1)
        def _():
            dx, dg_rows = _rms_bwd(x_ref[...], gain_ref[...], acc[...])
            dx_ref[...] = dres_ref[...] + dx

            @pl.when(i == 0)
            def _():
                dg_ref[...] = jnp.zeros_like(dg_ref)

            dg_ref[...] += jnp.sum(dg_rows, axis=0, keepdims=True)

    return pl.pallas_call(
        body, name=name, grid=(M // tm, nk),
        in_specs=[pl.BlockSpec((tm, tk), lambda i, k: (i, k)),
                  wspec((D, tk), lambda i, k: (0, k)),
                  pl.BlockSpec((tm, D), lambda i, k: (i, 0)),
                  pl.BlockSpec((1, D), lambda i, k: (0, 0)),
                  pl.BlockSpec((tm, D), lambda i, k: (i, 0))],
        out_specs=[pl.BlockSpec((tm, D), lambda i, k: (i, 0)),
                   pl.BlockSpec((1, D), lambda i, k: (0, 0))],
        out_shape=[jax.ShapeDtypeStruct((M, D), F32), jax.ShapeDtypeStruct((1, D), F32)],
        scratch_shapes=[pltpu.VMEM((tm, D), F32)],
        compiler_params=_params("arbitrary", "arbitrary"),
    )(gy, w, x, gain, dres)


def _mm_tn(a, g, *, name, tt=2048, tk_target=1024, tn_target=1024, into=None):
    T, K = a.shape
    N = g.shape[1]
    tt = min(tt, T)
    tk = _tile(K, tk_target)
    tn = _tile(N, tn_target)

    def body(a_ref, g_ref, *rest):
        o_ref = rest[-1]

        @pl.when(pl.program_id(2) == 0)
        def _():
            o_ref[...] = jnp.zeros_like(o_ref)

        o_ref[...] += lax.dot_general(a_ref[...].astype(CD), g_ref[...].astype(CD), TN,
                                      preferred_element_type=F32)

    in_specs = [pl.BlockSpec((tt, tk), lambda i, j, t: (t, i)),
                pl.BlockSpec((tt, tn), lambda i, j, t: (t, j))]
    if into is None:
        operands, aliases = (a, g), {}
        out_spec = pl.BlockSpec((tk, tn), lambda i, j, t: (i, j))
        out_shape = jax.ShapeDtypeStruct((K, N), F32)
    else:
        stack, layer = into
        operands, aliases = (a, g, stack), {2: 0}
        in_specs.append(pl.BlockSpec(memory_space=pl.ANY))
        out_spec = pl.BlockSpec((None, tk, tn), lambda i, j, t: (layer, i, j))
        out_shape = jax.ShapeDtypeStruct(stack.shape, F32)
    return pl.pallas_call(
        body, name=name, grid=(K // tk, N // tn, T // tt),
        in_specs=in_specs, out_specs=out_spec, out_shape=out_shape, input_output_aliases=aliases,
        compiler_params=_params("parallel", "parallel", "arbitrary"),
    )(*operands)


POOL_W = 512
CONV_W = 512
POOL_GROUP = 128


MIX_ROWS = 64
LN_ROWS = 256
LN_BWD_ROWS = 512
SUB = 8


def _shifted(sh_sc, x, n_rows):
    for b in range(1, SUB):
        sh_sc[b, pl.ds(0, n_rows), :] = x[b:b + n_rows]


def _tap(sh_sc, src, r0, cols, start, rows):
    a, b = divmod(start, SUB)
    if b == 0:
        return src[pl.ds(r0 + SUB * a, rows), cols]
    return sh_sc[b, pl.ds(SUB * a, rows), :]


def _pool_rows(zp_ref, z_ref, cols, win, i, tt, first, pooled_sc):
    RB = min(MIX_ROWS, tt)
    hb = 2 * SUB
    for r in range(tt // RB):
        if r == 0:
            p = zp_ref[pl.ds(HALO - hb, hb), cols]
            v = jnp.concatenate([jnp.where(first, jnp.zeros_like(p), p), z_ref[pl.ds(0, RB), cols]], axis=0)
        else:
            v = z_ref[pl.ds(r * RB - hb, RB + hb), cols]
        u = v[hb:hb + RB]
        s = u
        for j in range(1, win):
            s = s + v[hb - j:hb - j + RB]
        t_glob = i * tt + r * RB + lax.broadcasted_iota(jnp.int32, (RB, 1), 0)
        cnt = jnp.minimum(t_glob + 1, win).astype(F32)
        pooled_sc[pl.ds(r * RB, RB), :] = (s / cnt - u).astype(pooled_sc.dtype)


def _fill_gl(gl_sc, zp_ref, z_ref, zn_ref, tt, first, last):
    ca, cb = pl.ds(POOL_W, CONV_W), pl.ds(POOL_W + CONV_W, CONV_W)
    g = zp_ref[:, ca] * _sigmoid(zp_ref[:, cb])
    gl_sc[pl.ds(0, HALO), :] = jnp.where(first, jnp.zeros_like(g), g)

    def rows(r, carry):
        r0 = pl.multiple_of(r * LN_ROWS, LN_ROWS)
        gl_sc[pl.ds(HALO + r0, LN_ROWS), :] = z_ref[pl.ds(r0, LN_ROWS), ca] * _sigmoid(z_ref[pl.ds(r0, LN_ROWS), cb])
        return carry

    lax.fori_loop(0, tt // LN_ROWS, rows, 0)
    if zn_ref is not None:
        g = zn_ref[:, ca] * _sigmoid(zn_ref[:, cb])
        gl_sc[pl.ds(HALO + tt, HALO), :] = jnp.where(last, jnp.zeros_like(g), g)


def _conv_rows(gl_sc, cv_sc, sh_sc, w_ref, b_ref, n_rows):
    RB = min(MIX_ROWS, n_rows)
    for c in range(CONV_W // LANES):
        cols = pl.ds(c * LANES, LANES)
        bias = b_ref[:, cols]

        def chunk(r0, rb):
            g = gl_sc[pl.ds(r0, rb + HALO), cols]
            _shifted(sh_sc, g, rb + HALO - SUB)
            cv = jnp.zeros((rb, LANES), F32) + bias
            for j in range(CONV_K):
                cv = cv + w_ref[pl.ds(j, 1), cols] * _tap(sh_sc, gl_sc, r0, cols, HALO - (CONV_K - 1) + j, rb)
            cv_sc[pl.ds(r0, rb), cols] = cv

        def body(r, carry):
            chunk(pl.multiple_of(r * RB, RB), RB)
            return carry

        lax.fori_loop(0, n_rows // RB, body, 0)
        if n_rows % RB:
            chunk((n_rows // RB) * RB, n_rows % RB)


def _mixer_fwd(z, pool_w, pool_scale, dw_w, dw_b, ln_g, ln_b, *, name, tt=512):
    T, C = z.shape
    tt = min(tt, T)
    n = T // tt
    hb = tt // HALO

    def body(zp_ref, z_ref, pw_ref, ps_ref, w_ref, b_ref, g_ref, bb_ref, o_ref, pooled_sc, gl_sc, cv_sc, sh_sc):
        i = pl.program_id(0)
        first = i == 0
        for gi, win in enumerate(POOL_WINDOWS):
            cols = pl.ds(gi * POOL_GROUP, POOL_GROUP)
            _pool_rows(zp_ref, z_ref, cols, win, i, tt, first, pooled_sc)
            ya = jnp.dot(pooled_sc[...], pw_ref[gi].astype(CD), preferred_element_type=F32)
            o_ref[:, cols] = (ya * ps_ref[:, cols]).astype(o_ref.dtype)
        _fill_gl(gl_sc, zp_ref, z_ref, None, tt, first, None)
        _conv_rows(gl_sc, cv_sc, sh_sc, w_ref, b_ref, tt)

        def ln_rows(r, carry):
            rows = pl.ds(pl.multiple_of(r * LN_ROWS, LN_ROWS), LN_ROWS)
            cv = cv_sc[rows, :]
            xc = cv - jnp.mean(cv, axis=-1, keepdims=True)
            yn = xc * lax.rsqrt(jnp.mean(xc * xc, axis=-1, keepdims=True) + EPS) * g_ref[...] + bb_ref[...]
            o_ref[rows, pl.ds(POOL_W, CONV_W)] = (yn * _sigmoid(yn)).astype(o_ref.dtype)
            return carry

        lax.fori_loop(0, tt // LN_ROWS, ln_rows, 0, unroll=4)

    full = lambda shape: pl.BlockSpec(shape, lambda i: (0,) * len(shape))
    return pl.pallas_call(
        body, name=name, grid=(n,),
        in_specs=[pl.BlockSpec((HALO, C), lambda i: (jnp.maximum(i * hb - 1, 0), 0)),
                  pl.BlockSpec((tt, C), lambda i: (i, 0)),
                  full((4, POOL_GROUP, POOL_GROUP)), full((1, POOL_W)), full((CONV_K + 1, CONV_W)),
                  full((1, CONV_W)), full((1, CONV_W)), full((1, CONV_W))],
        out_specs=pl.BlockSpec((tt, POOL_W + CONV_W), lambda i: (i, 0)),
        out_shape=jax.ShapeDtypeStruct((T, POOL_W + CONV_W), CD),
        scratch_shapes=[pltpu.VMEM((tt, POOL_GROUP), CD), pltpu.VMEM((tt + HALO, CONV_W), F32),
                        pltpu.VMEM((tt, CONV_W), F32), pltpu.VMEM((SUB, MIX_ROWS + HALO, LANES), F32)],
        compiler_params=_params("parallel"),
    )(z, z, pool_w, pool_scale, dw_w, dw_b, ln_g, ln_b)


def _mixer_bwd(z, dy, pool_w, pool_scale, dw_w, dw_b, ln_g, ln_b, *, name, tt=512):
    T, C = z.shape
    tt = min(tt, T)
    n = T // tt
    hb = tt // HALO
    R = tt + HALO
    RB = min(MIX_ROWS, tt)

    def body(zp_ref, z_ref, zn_ref, dy_ref, dyn_ref, pw_ref, ps_ref, w_ref, b_ref, g_ref, bb_ref,
             dz_ref, dpw_ref, dps_ref, dw_ref, db_ref, dg_ref, dbb_ref,
             pooled_sc, dm_sc, dpool_sc, dpe_sc, gl_sc, cv_sc, accw, accl, sh_sc, shd_sc):
        i = pl.program_id(0)
        first, last = i == 0, i == n - 1

        @pl.when(first)
        def _():
            for r in (dpw_ref, dps_ref, dw_ref, db_ref, dg_ref, dbb_ref):
                r[...] = jnp.zeros_like(r)

        def dy_rows(cols):
            nxt = dyn_ref[:, cols]
            return jnp.concatenate([dy_ref[:, cols], jnp.where(last, jnp.zeros_like(nxt), nxt)], axis=0)

        t_all = i * tt + lax.broadcasted_iota(jnp.int32, (R, 1), 0)
        for gi, win in enumerate(POOL_WINDOWS):
            cols = pl.ds(gi * POOL_GROUP, POOL_GROUP)
            _pool_rows(zp_ref, z_ref, cols, win, i, tt, first, pooled_sc)
            pw = pw_ref[gi].astype(CD)
            dya = dy_rows(cols)
            mm = jnp.dot(pooled_sc[...], pw, preferred_element_type=F32)
            dps_ref[:, cols] += jnp.sum(dya[:tt] * mm, axis=0, keepdims=True)
            dm_sc[...] = (dya * ps_ref[:, cols]).astype(CD)
            dpw_ref[gi] += lax.dot_general(pooled_sc[...], dm_sc[pl.ds(0, tt), :], TN, preferred_element_type=F32)
            dpool = lax.dot_general(dm_sc[...], pw, NT, preferred_element_type=F32)
            dpool_sc[...] = dpool
            dpe_sc[...] = dpool / jnp.minimum(t_all + 1, win).astype(F32)

            def du_rows(r, carry):
                r0 = pl.multiple_of(r * RB, RB)
                e = dpe_sc[pl.ds(r0, RB + 2 * SUB), :]
                du = -dpool_sc[pl.ds(r0, RB), :]
                for j in range(win):
                    du = du + e[j:j + RB]
                dz_ref[pl.ds(r0, RB), cols] = du.astype(dz_ref.dtype)
                return carry

            lax.fori_loop(0, tt // RB, du_rows, 0)

        _fill_gl(gl_sc, zp_ref, z_ref, zn_ref, tt, first, last)
        _conv_rows(gl_sc, cv_sc, sh_sc, w_ref, b_ref, R)
        accl[...] = jnp.zeros_like(accl)

        def ln_rows(r0, nr, in_tile):
            rows = pl.ds(r0, nr)
            cv = cv_sc[rows, :]
            xc = cv - jnp.mean(cv, axis=-1, keepdims=True)
            rstd = lax.rsqrt(jnp.mean(xc * xc, axis=-1, keepdims=True) + EPS)
            xhat = xc * rstd
            yn = xhat * g_ref[...] + bb_ref[...]
            sy = _sigmoid(yn)
            if in_tile:
                dyv = dy_ref[rows, pl.ds(POOL_W, CONV_W)]
            else:
                nxt = dyn_ref[:, pl.ds(POOL_W, CONV_W)]
                dyv = jnp.where(last, jnp.zeros_like(nxt), nxt)
            dyn = dyv * (sy * (1.0 + yn * (1.0 - sy)))
            if in_tile:
                accl[pl.ds(0, SUB), :] += jnp.sum((dyn * xhat).reshape(nr // SUB, SUB, CONV_W), axis=0)
                accl[pl.ds(SUB, SUB), :] += jnp.sum(dyn.reshape(nr // SUB, SUB, CONV_W), axis=0)
            dxh = dyn * g_ref[...]
            dcv = rstd * (dxh - jnp.mean(dxh, axis=-1, keepdims=True)
                          - xhat * jnp.mean(dxh * xhat, axis=-1, keepdims=True))
            cv_sc[rows, :] = dcv
            if in_tile:
                accl[pl.ds(2 * SUB, SUB), :] += jnp.sum(dcv.reshape(nr // SUB, SUB, CONV_W), axis=0)

        lnb = min(LN_BWD_ROWS, tt)

        def ln_body(r, carry):
            ln_rows(pl.multiple_of(r * lnb, lnb), lnb, True)
            return carry

        lax.fori_loop(0, tt // lnb, ln_body, 0)
        ln_rows(tt, HALO, False)
        dg_ref[...] += jnp.sum(accl[pl.ds(0, SUB), :], axis=0, keepdims=True)
        dbb_ref[...] += jnp.sum(accl[pl.ds(SUB, SUB), :], axis=0, keepdims=True)
        db_ref[...] += jnp.sum(accl[pl.ds(2 * SUB, SUB), :], axis=0, keepdims=True)

        accw[...] = jnp.zeros_like(accw)
        for c in range(CONV_W // LANES):
            cols = pl.ds(c * LANES, LANES)

            def chunk(r, carry):
                r0 = pl.multiple_of(r * RB, RB)
                d = cv_sc[pl.ds(r0, RB + HALO), cols]
                g = gl_sc[pl.ds(r0, RB + HALO), cols]
                _shifted(shd_sc, d, RB + HALO - SUB)
                _shifted(sh_sc, g, RB + HALO - SUB)
                d_t = d[:RB]
                dgl = jnp.zeros((RB, LANES), F32)
                for j in range(CONV_K):
                    dgl = dgl + w_ref[pl.ds(j, 1), cols] * _tap(shd_sc, cv_sc, r0, cols, CONV_K - 1 - j, RB)
                    prod = d_t * _tap(sh_sc, gl_sc, r0, cols, HALO - (CONV_K - 1) + j, RB)
                    accw[pl.ds(SUB * j, SUB), cols] += jnp.sum(prod.reshape(RB // SUB, SUB, LANES), axis=0)
                a_t = z_ref[pl.ds(r0, RB), pl.ds(POOL_W + c * LANES, LANES)]
                sb = _sigmoid(z_ref[pl.ds(r0, RB), pl.ds(POOL_W + CONV_W + c * LANES, LANES)])
                dz_ref[pl.ds(r0, RB), pl.ds(POOL_W + c * LANES, LANES)] = (dgl * sb).astype(dz_ref.dtype)
                dz_ref[pl.ds(r0, RB), pl.ds(POOL_W + CONV_W + c * LANES, LANES)] = (
                    dgl * a_t * sb * (1.0 - sb)).astype(dz_ref.dtype)
                return carry

            lax.fori_loop(0, tt // RB, chunk, 0)
        for j in range(CONV_K):
            dw_ref[pl.ds(j, 1), :] += jnp.sum(accw[pl.ds(SUB * j, SUB), :], axis=0, keepdims=True)

    full = lambda shape: pl.BlockSpec(shape, lambda i: (0,) * len(shape))
    nb = T // HALO
    outs = pl.pallas_call(
        body, name=name, grid=(n,),
        in_specs=[pl.BlockSpec((HALO, C), lambda i: (jnp.maximum(i * hb - 1, 0), 0)),
                  pl.BlockSpec((tt, C), lambda i: (i, 0)),
                  pl.BlockSpec((HALO, C), lambda i: (jnp.minimum((i + 1) * hb, nb - 1), 0)),
                  pl.BlockSpec((tt, 2 * POOL_W), lambda i: (i, 0)),
                  pl.BlockSpec((HALO, 2 * POOL_W), lambda i: (jnp.minimum((i + 1) * hb, nb - 1), 0)),
                  full((4, POOL_GROUP, POOL_GROUP)), full((1, POOL_W)), full((CONV_K + 1, CONV_W)),
                  full((1, CONV_W)), full((1, CONV_W)), full((1, CONV_W))],
        out_specs=[pl.BlockSpec((tt, C), lambda i: (i, 0)),
                   full((4, POOL_GROUP, POOL_GROUP)), full((1, POOL_W)), full((CONV_K + 1, CONV_W)),
                   full((1, CONV_W)), full((1, CONV_W)), full((1, CONV_W))],
        out_shape=[jax.ShapeDtypeStruct((T, C), CD),
                   jax.ShapeDtypeStruct((4, POOL_GROUP, POOL_GROUP), F32),
                   jax.ShapeDtypeStruct((1, POOL_W), F32),
                   jax.ShapeDtypeStruct((CONV_K + 1, CONV_W), F32),
                   jax.ShapeDtypeStruct((1, CONV_W), F32),
                   jax.ShapeDtypeStruct((1, CONV_W), F32),
                   jax.ShapeDtypeStruct((1, CONV_W), F32)],
        scratch_shapes=[pltpu.VMEM((tt, POOL_GROUP), CD), pltpu.VMEM((R, POOL_GROUP), CD),
                        pltpu.VMEM((R, POOL_GROUP), F32), pltpu.VMEM((R, POOL_GROUP), F32),
                        pltpu.VMEM((tt + 2 * HALO, CONV_W), F32), pltpu.VMEM((R, CONV_W), F32),
                        pltpu.VMEM((SUB * (CONV_K + 1), CONV_W), F32), pltpu.VMEM((3 * SUB, CONV_W), F32),
                        pltpu.VMEM((SUB, MIX_ROWS + HALO, LANES), F32),
                        pltpu.VMEM((SUB, MIX_ROWS + HALO, LANES), F32)],
        compiler_params=_params("arbitrary"),
    )(z, z, z, dy, dy, pool_w, pool_scale, dw_w, dw_b, ln_g, ln_b)
    return outs


CHUNK_HALO = 16
FFN_ROWS = 64
FFN_LANES = 128


def _rows(cur, prev, nxt, r, rb, before, after, cols, n_r, first, last):
    lo, hi = r * rb - before, r * rb + rb + after
    tt = n_r * rb
    parts = []
    if lo < 0:
        p = prev[pl.ds(HALO + lo, -lo), cols]
        parts.append(jnp.where(first, jnp.zeros_like(p), p))
        lo = 0
    parts.append(cur[pl.ds(lo, min(hi, tt) - lo), cols])
    if hi > tt:
        p = nxt[pl.ds(0, hi - tt), cols]
        parts.append(jnp.where(last, jnp.zeros_like(p), p))
    return parts[0] if len(parts) == 1 else jnp.concatenate(parts, axis=0)


def _ffn_mid_fwd(up, cw, cb, *, name, tt=512):
    T = up.shape[0]
    tt = min(tt, T)
    n = T // tt
    hb = tt // HALO
    RB, CW, HB = min(FFN_ROWS, tt), FFN_LANES, CHUNK_HALO
    n_r = tt // RB

    def body(a_ref, gp_ref, g_ref, w_ref, b_ref, o_ref):
        first = pl.program_id(0) == 0

        def col_chunk(c, carry):
            cols = pl.ds(pl.multiple_of(c * CW, CW), CW)
            w = w_ref[:, cols]
            b = b_ref[:, cols]
            for r in range(n_r):
                v = _rows(g_ref, gp_ref, None, r, RB, HB, 0, cols, n_r, first, None).astype(F32)
                gc = b + w[0:1] * v[HB - 2:HB - 2 + RB] + w[1:2] * v[HB - 1:HB - 1 + RB] + w[2:3] * v[HB:HB + RB]
                a = a_ref[pl.ds(r * RB, RB), cols].astype(F32)
                o_ref[pl.ds(r * RB, RB), cols] = (gc * _sigmoid(gc) * a).astype(o_ref.dtype)
            return carry

        lax.fori_loop(0, D_FF // CW, col_chunk, 0)

    return pl.pallas_call(
        body, name=name, grid=(n,),
        in_specs=[pl.BlockSpec((tt, D_FF), lambda i: (i, 0)),
                  pl.BlockSpec((HALO, D_FF), lambda i: (jnp.maximum(i * hb - 1, 0), 1)),
                  pl.BlockSpec((tt, D_FF), lambda i: (i, 1)),
                  pl.BlockSpec((8, D_FF), lambda i: (0, 0)),
                  pl.BlockSpec((1, D_FF), lambda i: (0, 0))],
        out_specs=pl.BlockSpec((tt, D_FF), lambda i: (i, 0)),
        out_shape=jax.ShapeDtypeStruct((T, D_FF), CD),
        compiler_params=_params("parallel"),
    )(up, up, up, cw, cb)


def _ffn_mid_bwd(up, dact, cw, cb, *, name, tt=512):
    T = up.shape[0]
    tt = min(tt, T)
    n = T // tt
    hb = tt // HALO
    nb = T // HALO
    RB, CW, HB = min(FFN_ROWS, tt), FFN_LANES, CHUNK_HALO
    n_r = tt // RB
    RE = RB + 8

    def body(a_ref, an_ref, gp_ref, g_ref, gn_ref, d_ref, dn_ref, w_ref, b_ref, dup_ref, dw_ref, db_ref, acc):
        i = pl.program_id(0)
        first, last = i == 0, i == n - 1

        @pl.when(first)
        def _():
            dw_ref[...] = jnp.zeros_like(dw_ref)
            db_ref[...] = jnp.zeros_like(db_ref)

        def col_chunk(c, carry):
            cols = pl.ds(pl.multiple_of(c * CW, CW), CW)
            w = w_ref[:, cols]
            b = b_ref[:, cols]
            part = [jnp.zeros((8, CW), F32) for _ in range(FFN_K + 1)]
            for r in range(n_r):
                v = _rows(g_ref, gp_ref, gn_ref, r, RB, HB, HB, cols, n_r, first, last).astype(F32)
                gs = [v[HB - 2 + j:HB - 2 + j + RE] for j in range(FFN_K)]
                gc = b + w[0:1] * gs[0] + w[1:2] * gs[1] + w[2:3] * gs[2]
                sg = _sigmoid(gc)
                d = _rows(d_ref, None, dn_ref, r, RB, 0, HB, cols, n_r, None, last).astype(F32)[:RE]
                a = _rows(a_ref, None, an_ref, r, RB, 0, HB, cols, n_r, None, last).astype(F32)[:RE]
                silu = gc * sg
                dgc = d * a * (sg + silu - silu * sg)
                dup_ref[pl.ds(r * RB, RB), cols] = (d[:RB] * silu[:RB]).astype(dup_ref.dtype)
                dg = w[2:3] * dgc[0:RB] + w[1:2] * dgc[1:RB + 1] + w[0:1] * dgc[2:RB + 2]
                dup_ref[pl.ds(r * RB, RB), pl.ds(pl.multiple_of(D_FF + c * CW, CW), CW)] = dg.astype(dup_ref.dtype)
                dgc_t = dgc[:RB]
                for j in range(FFN_K):
                    part[j] = part[j] + jnp.sum((dgc_t * gs[j][:RB]).reshape(RB // 8, 8, CW), axis=0)
                part[FFN_K] = part[FFN_K] + jnp.sum(dgc_t.reshape(RB // 8, 8, CW), axis=0)
            for j in range(FFN_K + 1):
                acc[pl.ds(8 * j, 8), cols] = part[j]
            return carry

        lax.fori_loop(0, D_FF // CW, col_chunk, 0)
        for j in range(FFN_K):
            dw_ref[pl.ds(j, 1), :] += jnp.sum(acc[pl.ds(8 * j, 8), :], axis=0, keepdims=True)
        db_ref[...] += jnp.sum(acc[pl.ds(8 * FFN_K, 8), :], axis=0, keepdims=True)

    nxt = lambda i: jnp.minimum((i + 1) * hb, nb - 1)
    return pl.pallas_call(
        body, name=name, grid=(n,),
        in_specs=[pl.BlockSpec((tt, D_FF), lambda i: (i, 0)),
                  pl.BlockSpec((HALO, D_FF), lambda i: (nxt(i), 0)),
                  pl.BlockSpec((HALO, D_FF), lambda i: (jnp.maximum(i * hb - 1, 0), 1)),
                  pl.BlockSpec((tt, D_FF), lambda i: (i, 1)),
                  pl.BlockSpec((HALO, D_FF), lambda i: (nxt(i), 1)),
                  pl.BlockSpec((tt, D_FF), lambda i: (i, 0)),
                  pl.BlockSpec((HALO, D_FF), lambda i: (nxt(i), 0)),
                  pl.BlockSpec((8, D_FF), lambda i: (0, 0)),
                  pl.BlockSpec((1, D_FF), lambda i: (0, 0))],
        out_specs=[pl.BlockSpec((tt, 2 * D_FF), lambda i: (i, 0)),
                   pl.BlockSpec((8, D_FF), lambda i: (0, 0)),
                   pl.BlockSpec((1, D_FF), lambda i: (0, 0))],
        out_shape=[jax.ShapeDtypeStruct((T, 2 * D_FF), CD),
                   jax.ShapeDtypeStruct((8, D_FF), F32),
                   jax.ShapeDtypeStruct((1, D_FF), F32)],
        scratch_shapes=[pltpu.VMEM((8 * (FFN_K + 1), D_FF), F32)],
        compiler_params=_params("arbitrary"),
    )(up, up, up, up, up, dact, dact, cw, cb)


def _xattn_probs(q, k):
    s = lax.dot_general(q, k, NT, preferred_element_type=F32) * XA_SCALE
    p = jnp.exp(s - jnp.max(s, axis=-1, keepdims=True))
    return p / jnp.sum(p, axis=-1, keepdims=True)


def _xattn_fwd(q, kv, *, name, tq=1024):
    T = q.shape[0]
    tq = min(tq, T)

    def body(q_ref, kv_ref, o_ref):
        for h in range(XA_HEADS):
            cols = pl.ds(h * XA_DH, XA_DH)
            p = _xattn_probs(q_ref[:, cols], kv_ref[:, cols])
            v = kv_ref[:, pl.ds(D_MODEL + h * XA_DH, XA_DH)]
            o_ref[:, cols] = jnp.dot(p.astype(CD), v, preferred_element_type=F32).astype(o_ref.dtype)

    return pl.pallas_call(
        body, name=name, grid=(T // tq,),
        in_specs=[pl.BlockSpec((tq, D_MODEL), lambda i: (i, 0)),
                  pl.BlockSpec((MEM_LEN, 2 * D_MODEL), lambda i: (0, 0))],
        out_specs=pl.BlockSpec((tq, D_MODEL), lambda i: (i, 0)),
        out_shape=jax.ShapeDtypeStruct((T, D_MODEL), CD),
        compiler_params=_params("parallel"),
    )(q, kv)


def _xattn_bwd(q, kv, do, *, name, tq=1024):
    T = q.shape[0]
    tq = min(tq, T)

    def body(q_ref, kv_ref, do_ref, dq_ref, dkv_ref):
        @pl.when(pl.program_id(0) == 0)
        def _():
            dkv_ref[...] = jnp.zeros_like(dkv_ref)

        for h in range(XA_HEADS):
            cols = pl.ds(h * XA_DH, XA_DH)
            vcols = pl.ds(D_MODEL + h * XA_DH, XA_DH)
            qh, kh, vh, doh = q_ref[:, cols], kv_ref[:, cols], kv_ref[:, vcols], do_ref[:, cols]
            p = _xattn_probs(qh, kh)
            dkv_ref[:, vcols] += lax.dot_general(p.astype(CD), doh, TN, preferred_element_type=F32)
            dp = lax.dot_general(doh, vh, NT, preferred_element_type=F32)
            ds = (p * (dp - jnp.sum(dp * p, axis=-1, keepdims=True)) * XA_SCALE).astype(CD)
            dq_ref[:, cols] = jnp.dot(ds, kh, preferred_element_type=F32).astype(dq_ref.dtype)
            dkv_ref[:, cols] += lax.dot_general(ds, qh, TN, preferred_element_type=F32)

    return pl.pallas_call(
        body, name=name, grid=(T // tq,),
        in_specs=[pl.BlockSpec((tq, D_MODEL), lambda i: (i, 0)),
                  pl.BlockSpec((MEM_LEN, 2 * D_MODEL), lambda i: (0, 0)),
                  pl.BlockSpec((tq, D_MODEL), lambda i: (i, 0))],
        out_specs=[pl.BlockSpec((tq, D_MODEL), lambda i: (i, 0)),
                   pl.BlockSpec((MEM_LEN, 2 * D_MODEL), lambda i: (0, 0))],
        out_shape=[jax.ShapeDtypeStruct((T, D_MODEL), CD),
                   jax.ShapeDtypeStruct((MEM_LEN, 2 * D_MODEL), F32)],
        compiler_params=_params("arbitrary"),
    )(q, kv, do)


C_W = Q_LORA + KV_LORA + LANES


def _rot(x):
    lane = lax.broadcasted_iota(jnp.int32, x.shape, x.ndim - 1)
    up = pltpu.roll(x, LANES - QK_ROPE // 2, x.ndim - 1)
    dn = pltpu.roll(x, QK_ROPE // 2, x.ndim - 1)
    lo, mid, hi = QK_NOPE, QK_NOPE + QK_ROPE // 2, QK_NOPE + QK_ROPE
    return jnp.where((lane >= lo) & (lane < mid), -up, jnp.where((lane >= mid) & (lane < hi), dn, 0.0))


def _mla_mid_fwd(c, qg, kvg, cs, sn, *, name, tt=512):
    T = c.shape[0]
    tt = min(tt, T)

    def body(c_ref, qg_ref, kg_ref, cs_ref, sn_ref, qn_ref, kn_ref, kpe_ref):
        cq = c_ref[:, pl.ds(0, Q_LORA)]
        qn_ref[...] = (cq * lax.rsqrt(jnp.mean(cq * cq, axis=-1, keepdims=True) + EPS)
                       * qg_ref[...]).astype(qn_ref.dtype)
        ck = c_ref[:, pl.ds(Q_LORA, KV_LORA)]
        kn_ref[...] = (ck * lax.rsqrt(jnp.mean(ck * ck, axis=-1, keepdims=True) + EPS)
                       * kg_ref[...]).astype(kn_ref.dtype)
        kp = c_ref[:, pl.ds(Q_LORA + KV_LORA, LANES)]
        kpe_ref[...] = kp * cs_ref[...] + _rot(kp) * sn_ref[...]

    row = lambda w: pl.BlockSpec((tt, w), lambda i: (i, 0))
    one = lambda w: pl.BlockSpec((1, w), lambda i: (0, 0))
    return pl.pallas_call(
        body, name=name, grid=(T // tt,),
        in_specs=[row(C_W), one(Q_LORA), one(KV_LORA), row(LANES), row(LANES)],
        out_specs=[row(Q_LORA), row(KV_LORA), row(LANES)],
        out_shape=[jax.ShapeDtypeStruct((T, Q_LORA), CD), jax.ShapeDtypeStruct((T, KV_LORA), CD),
                   jax.ShapeDtypeStruct((T, LANES), F32)],
        compiler_params=_params("parallel"),
    )(c, qg, kvg, cs, sn)


def _mla_mid_bwd(c, dqn, dkvn, dksum, qg, kvg, cs, sn, *, name, tt=512):
    T = c.shape[0]
    tt = min(tt, T)

    def body(c_ref, dq_ref, dk_ref, ds_ref, qg_ref, kg_ref, cs_ref, sn_ref, dc_ref, dqg_ref, dkg_ref):
        @pl.when(pl.program_id(0) == 0)
        def _():
            dqg_ref[...] = jnp.zeros_like(dqg_ref)
            dkg_ref[...] = jnp.zeros_like(dkg_ref)

        dx, dg = _rms_bwd(c_ref[:, pl.ds(0, Q_LORA)], qg_ref[...], dq_ref[...])
        dc_ref[:, pl.ds(0, Q_LORA)] = dx.astype(dc_ref.dtype)
        dqg_ref[...] += jnp.sum(dg, axis=0, keepdims=True)
        dx, dg = _rms_bwd(c_ref[:, pl.ds(Q_LORA, KV_LORA)], kg_ref[...], dk_ref[...])
        dc_ref[:, pl.ds(Q_LORA, KV_LORA)] = dx.astype(dc_ref.dtype)
        dkg_ref[...] += jnp.sum(dg, axis=0, keepdims=True)
        d = ds_ref[...]
        lane = lax.broadcasted_iota(jnp.int32, d.shape, 1)
        dkp = d * cs_ref[...] - _rot(d * sn_ref[...])
        dc_ref[:, pl.ds(Q_LORA + KV_LORA, LANES)] = jnp.where(
            (lane >= QK_NOPE) & (lane < QK_NOPE + QK_ROPE), dkp, 0.0).astype(dc_ref.dtype)

    row = lambda w: pl.BlockSpec((tt, w), lambda i: (i, 0))
    one = lambda w: pl.BlockSpec((1, w), lambda i: (0, 0))
    return pl.pallas_call(
        body, name=name, grid=(T // tt,),
        in_specs=[row(C_W), row(Q_LORA), row(KV_LORA), row(LANES), one(Q_LORA), one(KV_LORA),
                  row(LANES), row(LANES)],
        out_specs=[row(C_W), one(Q_LORA), one(KV_LORA)],
        out_shape=[jax.ShapeDtypeStruct((T, C_W), CD), jax.ShapeDtypeStruct((1, Q_LORA), F32),
                   jax.ShapeDtypeStruct((1, KV_LORA), F32)],
        compiler_params=_params("arbitrary"),
    )(c, dqn, dkvn, dksum, qg, kvg, cs, sn)


def _mla_qkv_fwd(qn, kvn, kpe, cs, sn, wq, wk, wv, *, name, tt=512):
    T = qn.shape[0]
    tt = min(tt, T)
    H = MLA_HEADS

    def body(qn_ref, kn_ref, kpe_ref, cs_ref, sn_ref, wq_ref, wk_ref, wv_ref, q_ref, k_ref, v_ref):
        qn_v, kn_v, kpe_v, cs_v, sn_v = qn_ref[...], kn_ref[...], kpe_ref[...], cs_ref[...], sn_ref[...]
        for h in range(H):
            q = jnp.dot(qn_v, wq_ref[h], preferred_element_type=F32)
            q_ref[h] = (q * cs_v + _rot(q) * sn_v).astype(q_ref.dtype)
            k_ref[h] = (jnp.dot(kn_v, wk_ref[h], preferred_element_type=F32) + kpe_v).astype(k_ref.dtype)
            v_ref[h] = jnp.dot(kn_v, wv_ref[h], preferred_element_type=F32).astype(v_ref.dtype)

    row = lambda w: pl.BlockSpec((tt, w), lambda i: (i, 0))
    wsp = lambda k: pl.BlockSpec((H, k, LANES), lambda i: (0, 0, 0))
    hsp = pl.BlockSpec((H, tt, LANES), lambda i: (0, i, 0))
    sh = jax.ShapeDtypeStruct((H, T, LANES), CD)
    return pl.pallas_call(
        body, name=name, grid=(T // tt,),
        in_specs=[row(Q_LORA), row(KV_LORA), row(LANES), row(LANES), row(LANES),
                  wsp(Q_LORA), wsp(KV_LORA), wsp(KV_LORA)],
        out_specs=[hsp, hsp, hsp], out_shape=[sh, sh, sh],
        compiler_params=_params("parallel"),
    )(qn, kvn, kpe, cs, sn, wq, wk, wv)


def _mla_qkv_bwd(dq, dk, dv, qn, kvn, cs, sn, wq, wk, wv, *, name, tt=512):
    T = qn.shape[0]
    tt = min(tt, T)
    H = MLA_HEADS

    def body(dq_ref, dk_ref, dv_ref, qn_ref, kn_ref, cs_ref, sn_ref, wq_ref, wk_ref, wv_ref,
             dqn_ref, dkn_ref, dks_ref, dwq_ref, dwk_ref, dwv_ref):
        @pl.when(pl.program_id(0) == 0)
        def _():
            for r in (dwq_ref, dwk_ref, dwv_ref):
                r[...] = jnp.zeros_like(r)

        qn_v, kn_v, cs_v, sn_v = qn_ref[...], kn_ref[...], cs_ref[...], sn_ref[...]
        dqn = jnp.zeros((tt, Q_LORA), F32)
        dkn = jnp.zeros((tt, KV_LORA), F32)
        dks = jnp.zeros((tt, LANES), F32)
        for h in range(H):
            d = dq_ref[h]
            dqh = (d * cs_v - _rot(d * sn_v)).astype(CD)
            dkh, dvh = dk_ref[h], dv_ref[h]
            dqn = dqn + lax.dot_general(dqh, wq_ref[h], NT, preferred_element_type=F32)
            dkn = dkn + lax.dot_general(dkh, wk_ref[h], NT, preferred_element_type=F32)
            dkn = dkn + lax.dot_general(dvh, wv_ref[h], NT, preferred_element_type=F32)
            dks = dks + dkh.astype(F32)
            dwq_ref[h] += lax.dot_general(qn_v, dqh, TN, preferred_element_type=F32)
            dwk_ref[h] += lax.dot_general(kn_v, dkh, TN, preferred_element_type=F32)
            dwv_ref[h] += lax.dot_general(kn_v, dvh, TN, preferred_element_type=F32)
        dqn_ref[...] = dqn
        dkn_ref[...] = dkn
        dks_ref[...] = dks

    row = lambda w: pl.BlockSpec((tt, w), lambda i: (i, 0))
    wsp = lambda k: pl.BlockSpec((H, k, LANES), lambda i: (0, 0, 0))
    hsp = pl.BlockSpec((H, tt, LANES), lambda i: (0, i, 0))
    return pl.pallas_call(
        body, name=name, grid=(T // tt,),
        in_specs=[hsp, hsp, hsp, row(Q_LORA), row(KV_LORA), row(LANES), row(LANES),
                  wsp(Q_LORA), wsp(KV_LORA), wsp(KV_LORA)],
        out_specs=[row(Q_LORA), row(KV_LORA), row(LANES), wsp(Q_LORA), wsp(KV_LORA), wsp(KV_LORA)],
        out_shape=[jax.ShapeDtypeStruct((T, Q_LORA), F32), jax.ShapeDtypeStruct((T, KV_LORA), F32),
                   jax.ShapeDtypeStruct((T, LANES), F32),
                   jax.ShapeDtypeStruct((H, Q_LORA, LANES), F32),
                   jax.ShapeDtypeStruct((H, KV_LORA, LANES), F32),
                   jax.ShapeDtypeStruct((H, KV_LORA, LANES), F32)],
        compiler_params=_params("arbitrary"),
    )(dq, dk, dv, qn, kvn, cs, sn, wq, wk, wv)


FLASH_BLOCK = 1024
FLASH_ROWS = 128
EXP2_SCALE = MLA_SCALE * math.log2(math.e)


def _causal_steps(nq, by_key):
    pairs = [(i, j) for j in range(nq) for i in range(j, nq)] if by_key else \
            [(i, j) for i in range(nq) for j in range(i + 1)]
    return (jnp.asarray([p[0] for p in pairs], jnp.int32), jnp.asarray([p[1] for p in pairs], jnp.int32))


def _raw_scores(q, k, masked, first_row=0):
    s = lax.dot_general(q, k, NT, preferred_element_type=F32)
    if masked:
        row = lax.broadcasted_iota(jnp.int32, s.shape, 0) + first_row
        col = lax.broadcasted_iota(jnp.int32, s.shape, 1)
        s = jnp.where(col <= row, s, NEG)
    return s


def _flash_fwd(q, k, v, *, name):
    H, T, _ = q.shape
    tq = min(FLASH_BLOCK, T)
    nq = T // tq
    i_tab, j_tab = _causal_steps(nq, by_key=False)

    rb = min(FLASH_ROWS, tq)

    def body(i_tab, j_tab, q_ref, k_ref, v_ref, o_ref, lse_ref, m_sc, l_sc, acc, s_sc, p_sc):
        t = pl.program_id(1)
        i, j = i_tab[t], j_tab[t]

        @pl.when(j == 0)
        def _():
            m_sc[...] = jnp.full_like(m_sc, NEG)
            l_sc[...] = jnp.zeros_like(l_sc)
            acc[...] = jnp.zeros_like(acc)

        hb = tq // 2

        def step(masked):
            lane = lax.broadcasted_iota(jnp.int32, (tq, LANES), 1)
            top, bot = pl.ds(0, hb), pl.ds(hb, hb)
            alphas, pvs = [], []
            for h in range(2):
                if masked:
                    s_sc[h, top, top] = _raw_scores(q_ref[h, top, :], k_ref[h, top, :], True)
                    s_sc[h, bot, :] = _raw_scores(q_ref[h, bot, :], k_ref[h], True, first_row=hb)
                    m_cur = jnp.concatenate([jnp.max(s_sc[h, top, top], axis=-1, keepdims=True),
                                             jnp.max(s_sc[h, bot, :], axis=-1, keepdims=True)], axis=0)
                else:
                    s_sc[h] = _raw_scores(q_ref[h], k_ref[h], False)
                    m_cur = jnp.max(s_sc[h], axis=-1, keepdims=True)
                m_prev = m_sc[h]
                m_new = jnp.maximum(m_prev, m_cur)
                alpha = jnp.exp2((m_prev - m_new) * EXP2_SCALE)
                m_sc[h] = m_new
                for r in range(tq // rb):
                    rows = pl.ds(r * rb, rb)
                    m_r = m_sc[h, rows, :]
                    part = jnp.zeros((rb, LANES), F32)
                    keys = hb if masked and r * rb < hb else tq
                    for c in range(keys // LANES):
                        cols = pl.ds(c * LANES, LANES)
                        p = jnp.exp2((s_sc[h, rows, cols] - m_r) * EXP2_SCALE)
                        part = part + p
                        p_sc[h, rows, cols] = p.astype(CD)
                    l_sc[h, rows, :] = (alpha[r * rb:(r + 1) * rb] * l_sc[h, rows, :]
                                        + jnp.sum(part, axis=-1, keepdims=True))
                alphas.append(alpha)
                if masked:
                    pvs.append(jnp.concatenate(
                        [jnp.dot(p_sc[h, top, top], v_ref[h, top, :], preferred_element_type=F32),
                         jnp.dot(p_sc[h, bot, :], v_ref[h], preferred_element_type=F32)], axis=0))
                else:
                    pvs.append(jnp.dot(p_sc[h], v_ref[h], preferred_element_type=F32))
            acc[...] = acc[...] * jnp.where(lane < V_HEAD, alphas[0], alphas[1]) + pvs[0] + pvs[1]

        @pl.when(j < i)
        def _():
            step(False)

        @pl.when(j == i)
        def _():
            step(True)
            lane = lax.broadcasted_iota(jnp.int32, (tq, LANES), 1)
            o_ref[...] = (acc[...] / jnp.where(lane < V_HEAD, l_sc[0], l_sc[1])).astype(o_ref.dtype)
            for h in range(2):
                lse_ref[h] = m_sc[h] * EXP2_SCALE + jnp.log2(l_sc[h])

    qsp = pl.BlockSpec((2, tq, LANES), lambda p, t, it, jt: (p, it[t], 0))
    ksp = pl.BlockSpec((2, tq, LANES), lambda p, t, it, jt: (p, jt[t], 0))
    return pl.pallas_call(
        body, name=name,
        grid_spec=pltpu.PrefetchScalarGridSpec(
            num_scalar_prefetch=2, grid=(H // 2, int(i_tab.shape[0])),
            in_specs=[qsp, ksp, ksp],
            out_specs=[pl.BlockSpec((tq, LANES), lambda p, t, it, jt: (it[t], p)), qsp],
            scratch_shapes=[pltpu.VMEM((2, tq, LANES), F32), pltpu.VMEM((2, tq, LANES), F32),
                            pltpu.VMEM((tq, LANES), F32),
                            pltpu.VMEM((2, tq, tq), F32), pltpu.VMEM((2, tq, tq), CD)]),
        out_shape=[jax.ShapeDtypeStruct((T, H * V_HEAD), CD), jax.ShapeDtypeStruct((H, T, LANES), F32)],
        compiler_params=_params("parallel", "arbitrary"),
    )(i_tab, j_tab, q, k, v)


def _flash_delta(o, do, *, name, tt=512):
    T = o.shape[0]
    tt = min(tt, T)
    H = MLA_HEADS

    def body(o_ref, do_ref, dl_ref):
        lane = lax.broadcasted_iota(jnp.int32, (tt, LANES), 1)
        for p in range(H // 2):
            cols = pl.ds(p * LANES, LANES)
            prod = do_ref[:, cols].astype(F32) * o_ref[:, cols].astype(F32)
            d0 = jnp.sum(jnp.where(lane < V_HEAD, prod, 0.0), axis=-1, keepdims=True)
            d1 = jnp.sum(jnp.where(lane < V_HEAD, 0.0, prod), axis=-1, keepdims=True)
            dl_ref[2 * p] = jnp.broadcast_to(d0, (tt, LANES))
            dl_ref[2 * p + 1] = jnp.broadcast_to(d1, (tt, LANES))

    row = pl.BlockSpec((tt, H * V_HEAD), lambda i: (i, 0))
    return pl.pallas_call(
        body, name=name, grid=(T // tt,), in_specs=[row, row],
        out_specs=pl.BlockSpec((H, tt, LANES), lambda i: (0, i, 0)),
        out_shape=jax.ShapeDtypeStruct((H, T, LANES), F32),
        compiler_params=_params("parallel"),
    )(o, do)


def _flash_bwd(q, k, v, do, lse, delta, *, name):
    H, T, _ = q.shape
    tq = min(FLASH_BLOCK, T)
    nq = T // tq
    i_tab, j_tab = _causal_steps(nq, by_key=True)

    def body(i_tab, j_tab, q_ref, k_ref, v_ref, do_ref, lse_ref, dl_ref, dq_ref, dk_ref, dv_ref, dk_acc, dv_acc):
        t = pl.program_id(1)
        i, j = i_tab[t], j_tab[t]
        rows = pl.ds(pl.multiple_of(i * tq, tq), tq)

        @pl.when(t == 0)
        def _():
            dq_ref[...] = jnp.zeros_like(dq_ref)

        def block(h, qr, kr, first_row, masked):
            qh, kh, vh, do_v = q_ref[h, qr, :], k_ref[h, kr, :], v_ref[h, kr, :], do_ref[qr, :]
            s = _raw_scores(qh, kh, masked, first_row)
            p = jnp.exp2(s * EXP2_SCALE - lse_ref[h, qr, :][:, :1])
            dv_acc[h, kr, :] += lax.dot_general(p.astype(CD), do_v, TN, preferred_element_type=F32)
            dp = lax.dot_general(do_v, vh, NT, preferred_element_type=F32)
            ds = (p * (dp - dl_ref[h, qr, :][:, :1]) * MLA_SCALE).astype(CD)
            dk_acc[h, kr, :] += lax.dot_general(ds, qh, TN, preferred_element_type=F32)
            dq_rows = pl.ds(pl.multiple_of(i * tq + qr.start, qr.size), qr.size)
            dq_ref[h, dq_rows, :] += jnp.dot(ds, kh, preferred_element_type=F32)

        def step(masked):
            hb = tq // 2
            for h in range(2):
                if masked:
                    block(h, pl.ds(0, hb), pl.ds(0, hb), 0, True)
                    block(h, pl.ds(hb, hb), pl.ds(0, tq), hb, True)
                else:
                    block(h, pl.ds(0, tq), pl.ds(0, tq), 0, False)

        @pl.when(i == j)
        def _():
            dk_acc[...] = jnp.zeros_like(dk_acc)
            dv_acc[...] = jnp.zeros_like(dv_acc)
            step(True)

        @pl.when(i > j)
        def _():
            step(False)

        @pl.when(i == nq - 1)
        def _():
            lane = lax.broadcasted_iota(jnp.int32, (tq, LANES), 1)
            dk_ref[...] = dk_acc[...].astype(dk_ref.dtype)
            dv_ref[0] = jnp.where(lane < V_HEAD, dv_acc[0], 0.0).astype(dv_ref.dtype)
            dv_ref[1] = jnp.where(lane < V_HEAD, 0.0, dv_acc[1]).astype(dv_ref.dtype)

    qsp = pl.BlockSpec((2, tq, LANES), lambda p, t, it, jt: (p, it[t], 0))
    ksp = pl.BlockSpec((2, tq, LANES), lambda p, t, it, jt: (p, jt[t], 0))
    osp = pl.BlockSpec((tq, LANES), lambda p, t, it, jt: (it[t], p))
    sh = jax.ShapeDtypeStruct((H, T, LANES), CD)
    return pl.pallas_call(
        body, name=name,
        grid_spec=pltpu.PrefetchScalarGridSpec(
            num_scalar_prefetch=2, grid=(H // 2, int(i_tab.shape[0])),
            in_specs=[qsp, ksp, ksp, osp, qsp, qsp],
            out_specs=[pl.BlockSpec((2, T, LANES), lambda p, t, it, jt: (p, 0, 0)), ksp, ksp],
            scratch_shapes=[pltpu.VMEM((2, tq, LANES), F32), pltpu.VMEM((2, tq, LANES), F32)]),
        out_shape=[jax.ShapeDtypeStruct((H, T, LANES), F32), sh, sh],
        compiler_params=_params("parallel", "arbitrary"),
    )(i_tab, j_tab, q, k, v, do, lse, delta)


def _loss_head(x, g, target, *, name, tt=1024):
    T, D = x.shape
    tt = min(tt, T)

    def body(x_ref, g_ref, t_ref, dx_ref, dg_ref, loss_ref):
        @pl.when(pl.program_id(0) == 0)
        def _():
            dg_ref[...] = jnp.zeros_like(dg_ref)
            loss_ref[...] = jnp.zeros_like(loss_ref)

        xv, gv = x_ref[...], g_ref[...]
        r = lax.rsqrt(jnp.mean(xv * xv, axis=-1, keepdims=True) + EPS)
        err = xv * r * gv - t_ref[...]
        tok = jnp.mean(err * err, axis=-1, keepdims=True)
        loss_ref[...] += 0.5 * jnp.sum(tok, axis=0, keepdims=True)
        dx, dg_rows = _rms_bwd(xv, gv, err * (1.0 / D))
        dx_ref[...] = dx
        dg_ref[...] += jnp.sum(dg_rows, axis=0, keepdims=True)

    return pl.pallas_call(
        body, name=name, grid=(T // tt,),
        in_specs=[pl.BlockSpec((tt, D), lambda i: (i, 0)), pl.BlockSpec((1, D), lambda i: (0, 0)),
                  pl.BlockSpec((tt, D), lambda i: (i, 0))],
        out_specs=[pl.BlockSpec((tt, D), lambda i: (i, 0)), pl.BlockSpec((1, D), lambda i: (0, 0)),
                   pl.BlockSpec((1, LANES), lambda i: (0, 0))],
        out_shape=[jax.ShapeDtypeStruct((T, D), F32), jax.ShapeDtypeStruct((1, D), F32),
                   jax.ShapeDtypeStruct((1, LANES), F32)],
        compiler_params=_params("arbitrary"),
    )(x, g, target)


def _rope_tables(positions):
    inv = 1.0 / (ROPE_THETA ** (jnp.arange(0, QK_ROPE, 2, dtype=F32) / QK_ROPE))
    ang = positions.astype(F32)[:, None] * inv
    c, s = jnp.cos(ang), jnp.sin(ang)
    T = positions.shape[0]
    cs = jnp.concatenate([jnp.ones((T, QK_NOPE), F32), c, c, jnp.zeros((T, LANES - QK_NOPE - QK_ROPE), F32)], 1)
    sn = jnp.concatenate([jnp.zeros((T, QK_NOPE), F32), s, s, jnp.zeros((T, LANES - QK_NOPE - QK_ROPE), F32)], 1)
    return cs, sn


def _pad_rows(w, rows):
    return jnp.concatenate([w, jnp.zeros((rows - w.shape[0],) + w.shape[1:], w.dtype)], 0)


def _mla_weights(w_dq_dkv, w_uq, w_ukv):
    K = w_dq_dkv.shape[0]
    z = lambda n: jnp.zeros((K, n), w_dq_dkv.dtype)
    wc = jnp.concatenate([w_dq_dkv[:, :Q_LORA + KV_LORA], z(QK_NOPE), w_dq_dkv[:, Q_LORA + KV_LORA:],
                          z(LANES - QK_NOPE - QK_ROPE)], 1)
    wq = w_uq.reshape(Q_LORA, MLA_HEADS, QK_NOPE + QK_ROPE).transpose(1, 0, 2)
    wq = jnp.concatenate([wq, jnp.zeros((MLA_HEADS, Q_LORA, LANES - QK_NOPE - QK_ROPE), wq.dtype)], 2)
    wkv = w_ukv.reshape(KV_LORA, MLA_HEADS, QK_NOPE + V_HEAD).transpose(1, 0, 2)
    zero = jnp.zeros_like(wkv[:, :, :QK_NOPE])
    wk = jnp.concatenate([wkv[:, :, :QK_NOPE], zero], 2)
    wv_lo = jnp.concatenate([wkv[:, :, QK_NOPE:], zero], 2)
    wv_hi = jnp.concatenate([zero, wkv[:, :, QK_NOPE:]], 2)
    odd = (jnp.arange(MLA_HEADS) % 2 == 1)[:, None, None]
    wv = jnp.where(odd, wv_hi, wv_lo)
    return wc, wq, wk, wv


def _mla_weight_grads(dwc, dwq, dwk, dwv):
    d_dq = jnp.concatenate([dwc[:, :Q_LORA + KV_LORA],
                            dwc[:, Q_LORA + KV_LORA + QK_NOPE:Q_LORA + KV_LORA + QK_NOPE + QK_ROPE]], 1)
    d_uq = dwq[:, :, :QK_NOPE + QK_ROPE].transpose(1, 0, 2).reshape(Q_LORA, MLA_HEADS * (QK_NOPE + QK_ROPE))
    odd = (jnp.arange(MLA_HEADS) % 2 == 1)[:, None, None]
    dv = jnp.where(odd, dwv[:, :, V_HEAD:], dwv[:, :, :V_HEAD])
    d_ukv = jnp.concatenate([dwk[:, :, :QK_NOPE], dv], 2).transpose(1, 0, 2).reshape(
        KV_LORA, MLA_HEADS * (QK_NOPE + V_HEAD))
    return d_dq, d_uq, d_ukv


def _local_step(x, mem, positions, target, W):
    G = {}
    row = lambda v: v.reshape(1, -1)
    cs, sn = _rope_tables(positions)
    saved = []
    for l in range(DEPTH):
        L = f"l{l}"
        s = {"x0": x}
        if l % 2 == 0:
            e = l // 2
            s["z"], s["h"] = _nmm(x, row(W["norm_mix_g"][l]), (W["pc_w_in"], e), name=f"{L}_mix_in", out_dtype=F32)
            s["dw_w"] = _pad_rows(W["conv_dw_w"][e], CONV_K + 1)
            s["mix_p"] = (W["pool_w"][e], row(W["pool_scale"][e]), s["dw_w"], row(W["conv_dw_b"][e]),
                          row(W["conv_ln_g"][e]), row(W["conv_ln_b"][e]))
            s["ycat"] = _mixer_fwd(s["z"], *s["mix_p"], name=f"{L}_mix_mid")
            x = _mm_res(s["ycat"], (W["pc_w_out"], e), x, name=f"{L}_mix_out")
        else:
            o = l // 2
            wc, wq, wk, wv = _mla_weights(W["mla_w_dq_dkv"][o], W["mla_w_uq"][o], W["mla_w_ukv"][o])
            s["mla_w"] = (wc, wq, wk, wv)
            s["c"], s["h"] = _nmm(x, row(W["norm_mix_g"][l]), wc, name=f"{L}_mla_down", out_dtype=F32)
            s["qg"], s["kvg"] = row(W["mla_q_norm_g"][o]), row(W["mla_kv_norm_g"][o])
            s["qn"], s["kvn"], kpe = _mla_mid_fwd(s["c"], s["qg"], s["kvg"], cs, sn, name=f"{L}_mla_mid")
            s["q"], s["k"], s["v"] = _mla_qkv_fwd(s["qn"], s["kvn"], kpe, cs, sn, wq, wk, wv, name=f"{L}_mla_qkv")
            s["o"], s["lse"] = _flash_fwd(s["q"], s["k"], s["v"], name=f"{L}_mla_attn")
            x = _mm_res(s["o"], (W["mla_w_o"], o), x, name=f"{L}_mla_out")
        s["x1"] = x
        s["xq"], s["hx"] = _nmm(x, row(W["norm_xa_g"][l]), (W["xa_wq"], l), name=f"{L}_xa_q", out_dtype=CD)
        s["xkv"], s["hm"] = _nmm(mem, row(W["norm_mem_g"][l]), (W["xa_wkv"], l), name=f"{L}_xa_kv", out_dtype=CD)
        s["xo"] = _xattn_fwd(s["xq"], s["xkv"], name=f"{L}_xa_attn")
        x = _mm_res(s["xo"], (W["xa_wo"], l), x, name=f"{L}_xa_out")
        s["x2"] = x
        s["up"], s["hf"] = _nmm(x, row(W["norm_ffn_g"][l]), (W["ffn_w_up"], l), name=f"{L}_ffn_up", out_dtype=CD,
                                tn_target=1408)
        s["cw"], s["cb"] = _pad_rows(W["ffn_conv_w"][l], 8), row(W["ffn_conv_b"][l])
        s["act"] = _ffn_mid_fwd(s["up"], s["cw"], s["cb"], name=f"{L}_ffn_mid")
        x = _mm_res(s["act"], (W["ffn_w_down"], l), x, name=f"{L}_ffn_down")
        saved.append(s)
    dx, G["final_norm_g"], loss = _loss_head(x, row(W["final_norm_g"]), target, name="loss_head")
    G["final_norm_g"] = G["final_norm_g"].reshape(-1)

    per_layer = {}

    def put(name, l, val):
        per_layer.setdefault(name, {})[l] = val

    def put_dw(weight, l, a, g, **kw):
        stack = G[weight] if weight in G else lax.empty(W[weight].shape, F32)
        G[weight] = _mm_tn(a, g, into=(stack, l), **kw)

    for l in reversed(range(DEPTH)):
        L = f"l{l}"
        s = saved[l]
        put_dw("ffn_w_down", l, s["act"], dx, name=f"{L}_ffn_down_dw", tk_target=1408)
        dact = _mm_nt(dx, (W["ffn_w_down"], l), name=f"{L}_ffn_down_dx", out_dtype=CD, tn_target=1408)
        dup, dcw, dcb = _ffn_mid_bwd(s["up"], dact, s["cw"], s["cb"], name=f"{L}_ffn_mid_bwd")
        put("ffn_conv_w", l, dcw[:FFN_K])
        put("ffn_conv_b", l, dcb[0])
        put_dw("ffn_w_up", l, s["hf"], dup, name=f"{L}_ffn_up_dw", tn_target=1408)
        dx, dg = _mm_nt_normbwd(dup, (W["ffn_w_up"], l), s["x2"], row(W["norm_ffn_g"][l]), dx, name=f"{L}_ffn_up_dx")
        put("norm_ffn_g", l, dg[0])
        put_dw("xa_wo", l, s["xo"], dx, name=f"{L}_xa_out_dw")
        do = _mm_nt(dx, (W["xa_wo"], l), name=f"{L}_xa_out_dx", out_dtype=CD)
        dq, dkv = _xattn_bwd(s["xq"], s["xkv"], do, name=f"{L}_xa_attn_bwd")
        put_dw("xa_wq", l, s["hx"], dq, name=f"{L}_xa_q_dw")
        dx, dg = _mm_nt_normbwd(dq, (W["xa_wq"], l), s["x1"], row(W["norm_xa_g"][l]), dx, name=f"{L}_xa_q_dx")
        put("norm_xa_g", l, dg[0])
        put_dw("xa_wkv", l, s["hm"], dkv, name=f"{L}_xa_kv_dw", tt=MEM_LEN)
        _, dg = _mm_nt_normbwd(dkv, (W["xa_wkv"], l), mem, row(W["norm_mem_g"][l]), jnp.zeros_like(mem),
                               name=f"{L}_xa_kv_dx", tm=MEM_LEN)
        put("norm_mem_g", l, dg[0])
        if l % 2 == 0:
            e = l // 2
            put_dw("pc_w_out", e, s["ycat"], dx, name=f"{L}_mix_out_dw")
            dy = _mm_nt(dx, (W["pc_w_out"], e), name=f"{L}_mix_out_dx", out_dtype=F32)
            dz, dpw, dps, ddw, ddb, dlg, dlb = _mixer_bwd(s["z"], dy, *s["mix_p"], name=f"{L}_mix_mid_bwd")
            put("pool_w", e, dpw)
            put("pool_scale", e, dps[0])
            put("conv_dw_w", e, ddw[:CONV_K])
            put("conv_dw_b", e, ddb[0])
            put("conv_ln_g", e, dlg[0])
            put("conv_ln_b", e, dlb[0])
            put_dw("pc_w_in", e, s["h"], dz, name=f"{L}_mix_in_dw")
            dx, dg = _mm_nt_normbwd(dz, (W["pc_w_in"], e), s["x0"], row(W["norm_mix_g"][l]), dx, name=f"{L}_mix_in_dx")
        else:
            o = l // 2
            wc, wq, wk, wv = s["mla_w"]
            put_dw("mla_w_o", o, s["o"], dx, name=f"{L}_mla_out_dw")
            do = _mm_nt(dx, (W["mla_w_o"], o), name=f"{L}_mla_out_dx", out_dtype=CD)
            delta = _flash_delta(s["o"], do, name=f"{L}_mla_attn_delta")
            dq, dk, dv = _flash_bwd(s["q"], s["k"], s["v"], do, s["lse"], delta, name=f"{L}_mla_attn_bwd")
            dqn, dkvn, dks, dwq, dwk, dwv = _mla_qkv_bwd(dq, dk, dv, s["qn"], s["kvn"], cs, sn, wq, wk, wv,
                                                         name=f"{L}_mla_qkv_bwd")
            dc, dqg, dkg = _mla_mid_bwd(s["c"], dqn, dkvn, dks, s["qg"], s["kvg"], cs, sn, name=f"{L}_mla_mid_bwd")
            put("mla_q_norm_g", o, dqg[0])
            put("mla_kv_norm_g", o, dkg[0])
            dwc = _mm_tn(s["h"], dc, name=f"{L}_mla_down_dw")
            d_dq, d_uq, d_ukv = _mla_weight_grads(dwc, dwq, dwk, dwv)
            put("mla_w_dq_dkv", o, d_dq)
            put("mla_w_uq", o, d_uq)
            put("mla_w_ukv", o, d_ukv)
            dx, dg = _mm_nt_normbwd(dc, wc, s["x0"], row(W["norm_mix_g"][l]), dx, name=f"{L}_mla_down_dx",
                                    tk_target=768)
        put("norm_mix_g", l, dg[0])
    for name, d in per_layer.items():
        G[name] = jnp.stack([d[i] for i in sorted(d)], 0)
    return loss, dx, G


_ANY = pl.BlockSpec(memory_space=pl.ANY)


def _all_gather(xs, *, name):
    n = len(xs)

    def body(*refs):
        x_refs, out_refs = refs[:n], refs[n:2 * n]
        send_sems, recv_sems, local_sems = refs[2 * n:]
        mx, my, mc = lax.axis_index("x"), lax.axis_index("y"), lax.axis_index("c")
        me, sibling = (mx, my, mc), (mx, my, 1 - mc)
        xn, yn, dg = (1 - mx, my), (mx, 1 - my), (1 - mx, 1 - my)
        src = (mx + (1 - mc) * (1 - 2 * mx), my + mc * (1 - 2 * my))
        dst = (mx + mc * (1 - 2 * mx), my + (1 - mc) * (1 - 2 * my))
        SIB, XN, YN, DG, PASS = 0, 1, 2, 3, 4

        def copy(a, k, block, to, own=False):
            px, py, pc = block
            slot = out_refs[a].at[4 * px + 2 * py + pc]
            return pltpu.make_async_remote_copy(
                src_ref=x_refs[a] if own else slot, dst_ref=slot,
                send_sem=send_sems.at[7 * a + k], recv_sem=recv_sems.at[7 * a + k],
                device_id=to, device_id_type=MESH)

        mine = [pltpu.make_async_copy(x_refs[a], out_refs[a].at[4 * mx + 2 * my + mc], local_sems.at[a])
                for a in range(n)]
        for cp in mine:
            cp.start()
        sent = [copy(a, XN, me, (*xn, mc), own=True) for a in range(n)]
        sent += [copy(a, YN, me, (*yn, mc), own=True) for a in range(n)]
        sent += [copy(a, SIB, me, sibling, own=True) for a in range(n)]
        for cp in sent:
            cp.start()
        for a in range(n):
            for k, chip in ((XN, xn), (YN, yn)):
                copy(a, k, (*chip, mc), me).wait_recv()
                sent.append(copy(a, PASS + k - 1, (*chip, mc), sibling))
                sent[-1].start()
            sent.append(copy(a, DG, (*src, mc), (*dst, mc)))
            sent[-1].start()
        for a in range(n):
            copy(a, DG, (*dg, mc), me).wait_recv()
            sent.append(copy(a, PASS + DG - 1, (*dg, mc), sibling))
            sent[-1].start()
        for a in range(n):
            copy(a, SIB, sibling, me).wait_recv()
            for k, chip in ((XN, xn), (YN, yn), (DG, dg)):
                copy(a, PASS + k - 1, (*chip, 1 - mc), me).wait_recv()
        for cp in sent:
            cp.wait_send()
        for cp in mine:
            cp.wait()

    return pl.pallas_call(
        body, name=name, in_specs=[_ANY] * n, out_specs=[_ANY] * n,
        out_shape=[jax.ShapeDtypeStruct((N_DEV,) + x.shape, x.dtype) for x in xs],
        scratch_shapes=[pltpu.SemaphoreType.DMA((7 * n,)), pltpu.SemaphoreType.DMA((7 * n,)),
                        pltpu.SemaphoreType.DMA((n,))],
    )(*xs)


N_CHIP = 4


def _pair_exchange(ps, *, name):
    n = len(ps)

    def body(*refs):
        p_refs, out_refs = refs[:n], refs[n:2 * n]
        send_sems, recv_sems = refs[2 * n:]
        mx, my, mc = lax.axis_index("x"), lax.axis_index("y"), lax.axis_index("c")
        copies = []
        for a in range(n):
            for chip in range(N_CHIP):
                copies.append(pltpu.make_async_remote_copy(
                    src_ref=p_refs[a].at[2 * chip + (1 - mc)], dst_ref=out_refs[a].at[chip],
                    send_sem=send_sems.at[N_CHIP * a + chip], recv_sem=recv_sems.at[N_CHIP * a + chip],
                    device_id=(mx, my, 1 - mc), device_id_type=MESH))
        for cp in copies:
            cp.start()
        for cp in copies:
            cp.wait()

    return pl.pallas_call(
        body, name=name, in_specs=[_ANY] * n, out_specs=[_ANY] * n,
        out_shape=[jax.ShapeDtypeStruct((N_CHIP,) + p.shape[1:], p.dtype) for p in ps],
        scratch_shapes=[pltpu.SemaphoreType.DMA((N_CHIP * n,)), pltpu.SemaphoreType.DMA((N_CHIP * n,))],
    )(*ps)


def _pair_sum(p, recv, core, *, name):
    _, R, C = p.shape
    tr = _row_tile(R, C, 4 * ROW_TILE_ELEMS)
    p4 = p.reshape(N_CHIP, 2, R, C)

    def body(core_ref, a_ref, b_ref, o_ref):
        o_ref[...] = (a_ref[...].astype(F32) + b_ref[...].astype(F32)).astype(o_ref.dtype)

    return pl.pallas_call(
        body, name=name,
        grid_spec=pltpu.PrefetchScalarGridSpec(
            num_scalar_prefetch=1, grid=(N_CHIP, R // tr),
            in_specs=[pl.BlockSpec((None, None, tr, C), lambda ch, i, core: (ch, core[0], i, 0)),
                      pl.BlockSpec((None, tr, C), lambda ch, i, core: (ch, i, 0))],
            out_specs=pl.BlockSpec((None, tr, C), lambda ch, i, core: (ch, i, 0))),
        out_shape=jax.ShapeDtypeStruct((N_CHIP, R, C), p.dtype),
        compiler_params=_params("parallel", "parallel"),
    )(core, p4, recv)


def _chip_exchange(ss, *, name):
    n = len(ss)

    def body(*refs):
        s_refs, out_refs, stage_refs = refs[:n], refs[n:2 * n], refs[2 * n:3 * n]
        send_sems, recv_sems, local_sems = refs[3 * n:]
        mx, my, mc = lax.axis_index("x"), lax.axis_index("y"), lax.axis_index("c")
        chip = 2 * mx + my
        xn, yn, dg = (1 - mx, my), (mx, 1 - my), (1 - mx, 1 - my)
        via = (mx + (1 - mc) * (1 - 2 * mx), my + mc * (1 - 2 * my))
        onward = (mx + mc * (1 - 2 * mx), my + (1 - mc) * (1 - 2 * my))
        XN, YN, STAGE, ONWARD = 0, 1, 2, 3

        def copy(a, k, src, dst, to):
            return pltpu.make_async_remote_copy(
                src_ref=src, dst_ref=dst, send_sem=send_sems.at[4 * a + k], recv_sem=recv_sems.at[4 * a + k],
                device_id=(*to, mc), device_id_type=MESH)

        def slot(ref, c):
            return ref.at[2 * c[0] + c[1]]

        mine = [pltpu.make_async_copy(s_refs[a].at[chip], out_refs[a].at[chip], local_sems.at[a]) for a in range(n)]
        for cp in mine:
            cp.start()
        first = []
        for a in range(n):
            first.append(copy(a, STAGE, slot(s_refs[a], dg), stage_refs[a], via))
            first.append(copy(a, XN, slot(s_refs[a], xn), out_refs[a].at[chip], xn))
            first.append(copy(a, YN, slot(s_refs[a], yn), out_refs[a].at[chip], yn))
        for cp in first:
            cp.start()
        onwards = []
        for a in range(n):
            first[3 * a].wait_recv()
            onwards.append(copy(a, ONWARD, stage_refs[a], slot(out_refs[a], via), onward))
            onwards[-1].start()
        for a in range(n):
            first[3 * a + 1].wait_recv()
            first[3 * a + 2].wait_recv()
            onwards[a].wait_recv()
        for cp in first + onwards:
            cp.wait_send()
        for cp in mine:
            cp.wait()

    outs = pl.pallas_call(
        body, name=name, in_specs=[_ANY] * n, out_specs=[_ANY] * (2 * n),
        out_shape=[jax.ShapeDtypeStruct(s.shape, s.dtype) for s in ss]
                  + [jax.ShapeDtypeStruct(s.shape[1:], s.dtype) for s in ss],
        scratch_shapes=[pltpu.SemaphoreType.DMA((4 * n,)), pltpu.SemaphoreType.DMA((4 * n,)),
                        pltpu.SemaphoreType.DMA((n,))],
    )(*ss)
    return outs[:n]


ROW_TILE_ELEMS = 256 * 1024


def _row_tile(R, C, elems=None):
    elems = ROW_TILE_ELEMS if elems is None else elems
    for t in (4096, 2048, 1024, 512, 256, 128, 64, 32, 16):
        if R % t == 0 and t * C <= elems:
            return t
    raise ValueError((R, C))


def _sum_slots(gs, *, name):
    S, R, C = gs.shape
    tr = _row_tile(R, C)

    def body(g_ref, o_ref):
        g = g_ref[0].astype(F32)
        for s in range(1, S):
            g = g + g_ref[s].astype(F32)
        o_ref[...] = g

    return pl.pallas_call(
        body, name=name, grid=(R // tr,),
        in_specs=[pl.BlockSpec((S, tr, C), lambda i: (0, i, 0))],
        out_specs=pl.BlockSpec((tr, C), lambda i: (i, 0)),
        out_shape=jax.ShapeDtypeStruct((R, C), F32),
        compiler_params=_params("parallel"),
    )(gs)


def _adamw(gs, w, m, v, *, name):
    S, R, C = gs.shape
    tr = _row_tile(R, C, 2 * ROW_TILE_ELEMS)

    def body(g_ref, w_ref, m_ref, v_ref, g_out, d_out, m_out, v_out):
        g = g_ref[0].astype(F32)
        for s in range(1, S):
            g = g + g_ref[s].astype(F32)
        m_new = ADAM_B1 * m_ref[...] + (1.0 - ADAM_B1) * g
        v_new = ADAM_B2 * v_ref[...] + (1.0 - ADAM_B2) * (g * g)
        m_hat = m_new / (1.0 - ADAM_B1 ** ADAM_STEP)
        v_hat = v_new / (1.0 - ADAM_B2 ** ADAM_STEP)
        g_out[...] = g
        d_out[...] = -ADAM_LR * (m_hat / (jnp.sqrt(v_hat) + ADAM_EPS) + ADAM_WD * w_ref[...])
        m_out[...] = m_new
        v_out[...] = v_new

    blk = pl.BlockSpec((tr, C), lambda i: (i, 0))
    sh = jax.ShapeDtypeStruct((R, C), F32)
    return pl.pallas_call(
        body, name=name, grid=(R // tr,),
        in_specs=[pl.BlockSpec((S, tr, C), lambda i: (0, i, 0)), blk, blk, blk],
        out_specs=[blk, blk, blk, blk], out_shape=[sh, sh, sh, sh],
        compiler_params=_params("parallel"),
    )(gs, w, m, v)


PIECE = 16 * LANES


def _pack(arrs, dtype, lead, row_mult):
    lead_shape = arrs[0].shape[:lead]
    parts, meta, off = [], [], 0
    for a in arrs:
        size = math.prod(a.shape[lead:])
        padded = -(-size // PIECE) * PIECE
        flat = a.astype(dtype).reshape(lead_shape + (size,))
        if padded != size:
            flat = jnp.concatenate([flat, jnp.zeros(lead_shape + (padded - size,), dtype)], -1)
        parts.append(flat)
        meta.append((off, size, a.shape[lead:]))
        off += padded
    total = -(-off // (row_mult * LANES)) * (row_mult * LANES)
    if total != off:
        parts.append(jnp.zeros(lead_shape + (total - off,), dtype))
    return jnp.concatenate(parts, -1).reshape(lead_shape + (total // LANES, LANES)), meta


def _unpack(packed, meta, lead):
    lead_shape = packed.shape[:lead]
    flat = packed.reshape(lead_shape + (-1,))
    return [flat[..., off:off + size].reshape(lead_shape + shape) for off, size, shape in meta]


ARG_NAMES = ['x', 'mem', 'positions', 'norm_mix_g', 'norm_xa_g', 'norm_mem_g', 'xa_wq', 'xa_wkv', 'xa_wo', 'norm_ffn_g', 'ffn_w_up', 'ffn_conv_w', 'ffn_conv_b', 'ffn_w_down', 'pc_w_in', 'pool_w', 'pool_scale', 'conv_dw_w', 'conv_dw_b', 'conv_ln_g', 'conv_ln_b', 'pc_w_out', 'mla_w_dq_dkv', 'mla_q_norm_g', 'mla_w_uq', 'mla_kv_norm_g', 'mla_w_ukv', 'mla_w_o', 'final_norm_g', 'loss_target']
WEIGHTS = ARG_NAMES[3:29]
BIG = {'xa_wq': 1, 'xa_wkv': 2, 'xa_wo': 1, 'ffn_w_up': 2, 'ffn_w_down': 1, 'pc_w_in': 2, 'pc_w_out': 1,
       'mla_w_dq_dkv': 1, 'mla_w_uq': 2, 'mla_w_ukv': 2, 'mla_w_o': 1}
SMALL_SHARDED = {'ffn_conv_w': 2, 'conv_dw_w': 2, 'mla_q_norm_g': 1, 'mla_kv_norm_g': 1}
REPLICATED = [n for n in WEIGHTS if n not in BIG and n not in SMALL_SHARDED]


def _from_slots(g, axis):
    t = jnp.moveaxis(g, 0, axis)
    return t.reshape(t.shape[:axis] + (t.shape[axis] * t.shape[axis + 1],) + t.shape[axis + 2:])


def _to_slots(full, axis):
    n = full.shape[axis] // N_DEV
    t = full.reshape(full.shape[:axis] + (N_DEV, n) + full.shape[axis + 1:])
    return jnp.moveaxis(t, axis, 0)


def kernel(x, mem, positions, norm_mix_g, norm_xa_g, norm_mem_g, xa_wq, xa_wkv, xa_wo, norm_ffn_g, ffn_w_up, ffn_conv_w, ffn_conv_b, ffn_w_down, pc_w_in, pool_w, pool_scale, conv_dw_w, conv_dw_b, conv_ln_g, conv_ln_b, pc_w_out, mla_w_dq_dkv, mla_q_norm_g, mla_w_uq, mla_kv_norm_g, mla_w_ukv, mla_w_o, final_norm_g, loss_target, m_norm_mix_g, m_norm_xa_g, m_norm_mem_g, m_xa_wq, m_xa_wkv, m_xa_wo, m_norm_ffn_g, m_ffn_w_up, m_ffn_conv_w, m_ffn_conv_b, m_ffn_w_down, m_pc_w_in, m_pool_w, m_pool_scale, m_conv_dw_w, m_conv_dw_b, m_conv_ln_g, m_conv_ln_b, m_pc_w_out, m_mla_w_dq_dkv, m_mla_q_norm_g, m_mla_w_uq, m_mla_kv_norm_g, m_mla_w_ukv, m_mla_w_o, m_final_norm_g, v_norm_mix_g, v_norm_xa_g, v_norm_mem_g, v_xa_wq, v_xa_wkv, v_xa_wo, v_norm_ffn_g, v_ffn_w_up, v_ffn_conv_w, v_ffn_conv_b, v_ffn_w_down, v_pc_w_in, v_pool_w, v_pool_scale, v_conv_dw_w, v_conv_dw_b, v_conv_ln_g, v_conv_ln_b, v_pc_w_out, v_mla_w_dq_dkv, v_mla_q_norm_g, v_mla_w_uq, v_mla_kv_norm_g, v_mla_w_ukv, v_mla_w_o, v_final_norm_g):
    args = (x, mem, positions, norm_mix_g, norm_xa_g, norm_mem_g, xa_wq, xa_wkv, xa_wo, norm_ffn_g, ffn_w_up, ffn_conv_w, ffn_conv_b, ffn_w_down, pc_w_in, pool_w, pool_scale, conv_dw_w, conv_dw_b, conv_ln_g, conv_ln_b, pc_w_out, mla_w_dq_dkv, mla_q_norm_g, mla_w_uq, mla_kv_norm_g, mla_w_ukv, mla_w_o, final_norm_g, loss_target)
    a = dict(zip(ARG_NAMES, args))
    mom = dict(zip(WEIGHTS, (m_norm_mix_g, m_norm_xa_g, m_norm_mem_g, m_xa_wq, m_xa_wkv, m_xa_wo, m_norm_ffn_g, m_ffn_w_up, m_ffn_conv_w, m_ffn_conv_b, m_ffn_w_down, m_pc_w_in, m_pool_w, m_pool_scale, m_conv_dw_w, m_conv_dw_b, m_conv_ln_g, m_conv_ln_b, m_pc_w_out, m_mla_w_dq_dkv, m_mla_q_norm_g, m_mla_w_uq, m_mla_kv_norm_g, m_mla_w_ukv, m_mla_w_o, m_final_norm_g)))
    var = dict(zip(WEIGHTS, (v_norm_mix_g, v_norm_xa_g, v_norm_mem_g, v_xa_wq, v_xa_wkv, v_xa_wo, v_norm_ffn_g, v_ffn_w_up, v_ffn_conv_w, v_ffn_conv_b, v_ffn_w_down, v_pc_w_in, v_pool_w, v_pool_scale, v_conv_dw_w, v_conv_dw_b, v_conv_ln_g, v_conv_ln_b, v_pc_w_out, v_mla_w_dq_dkv, v_mla_q_norm_g, v_mla_w_uq, v_mla_kv_norm_g, v_mla_w_ukv, v_mla_w_o, v_final_norm_g)))
    me = 4 * lax.axis_index("x") + 2 * lax.axis_index("y") + lax.axis_index("c")

    big_all = _all_gather([a[n].astype(CD) for n in BIG], name="gather_weights")
    sm_pack, sm_meta = _pack([a[n] for n in SMALL_SHARDED], F32, 0, 8)
    sm_all = _unpack(_all_gather([sm_pack], name="gather_small")[0], sm_meta, 1)
    W = {n: a[n] for n in REPLICATED}
    for (n, ax), g in zip(BIG.items(), big_all):
        W[n] = _from_slots(g, ax)
    for (n, ax), g in zip(SMALL_SHARDED.items(), sm_all):
        W[n] = _from_slots(g, ax)

    loss, dx, G = _local_step(x[0], mem[0], positions[0], loss_target[0], W)

    parts = [_to_slots(G[n], ax).astype(CD) for n, ax in BIG.items()]
    from_sibling = _pair_exchange(parts, name="grads_to_sibling")
    core = lax.axis_index("c").astype(jnp.int32).reshape(1)
    sums = []
    for n, p, r in zip(BIG, parts, from_sibling):
        cols = p.shape[-1]
        s = _pair_sum(p.reshape(N_DEV, -1, cols), r.reshape(N_CHIP, -1, cols), core, name=f"pair_sum_{n}")
        sums.append(s.reshape((N_CHIP,) + p.shape[1:]))
    recv = _chip_exchange(sums, name="scatter_grads")
    out = {}
    for n, r in zip(BIG, recv):
        shape = a[n].shape
        rows = lambda t: t.reshape(-1, shape[-1])
        res = _adamw(r.reshape(N_CHIP, -1, shape[-1]), rows(a[n]), rows(mom[n]), rows(var[n]), name=f"adamw_{n}")
        out[n] = tuple(t.reshape(shape) for t in res)

    small_names = REPLICATED + list(SMALL_SHARDED)
    spack, smeta = _pack([G[n] for n in small_names] + [loss], F32, 0, 256)
    stot = _unpack(_sum_slots(_all_gather([spack], name="gather_small_grads")[0], name="sum_small_grads"), smeta, 0)
    loss_total = stot[-1][0, 0]
    gsm = dict(zip(small_names, stot[:-1]))
    for n, ax in SMALL_SHARDED.items():
        width = a[n].shape[ax]
        gsm[n] = lax.dynamic_slice_in_dim(gsm[n], me * width, width, ax)
    g1, meta1 = _pack([gsm[n] for n in small_names], F32, 0, 256)
    w1, _ = _pack([a[n] for n in small_names], F32, 0, 256)
    m1, _ = _pack([mom[n] for n in small_names], F32, 0, 256)
    v1, _ = _pack([var[n] for n in small_names], F32, 0, 256)
    res = [_unpack(r, meta1, 0) for r in _adamw(g1[None], w1, m1, v1, name="adamw_small")]
    for i, n in enumerate(small_names):
        out[n] = tuple(r[i] for r in res)

    return (loss_total, dx[None],
            *[out[n][0] for n in WEIGHTS], *[out[n][1] for n in WEIGHTS],
            *[out[n][2] for n in WEIGHTS], *[out[n][3] for n in WEIGHTS])
```

```python
import functools
import math

import jax
import jax.numpy as jnp
from jax import lax
from jax.experimental import pallas as pl
from jax.experimental.pallas import tpu as pltpu

F32 = jnp.float32
CD = jnp.bfloat16
EPS = 1e-6
NEG = -1e30
N_DEV = 8
LANES = 128
HALO = 32

D_MODEL = 1024
DEPTH = 4
XA_HEADS = 4
XA_DH = 256
MEM_LEN = 256
POOL_WINDOWS = (2, 4, 8, 16)
CONV_K = 31
FFN_K = 3
D_FF = 2816
MLA_HEADS = 16
QK_NOPE = 64
QK_ROPE = 32
V_HEAD = 64
Q_LORA = 384
KV_LORA = 256
ROPE_THETA = 10000.0
MLA_SCALE = 1.0 / math.sqrt(QK_NOPE + QK_ROPE)
XA_SCALE = XA_DH ** -0.5

ADAM_LR = 0.001
ADAM_B1 = 0.9
ADAM_B2 = 0.999
ADAM_EPS = 1e-08
ADAM_WD = 0.01
ADAM_STEP = 10

NT = (((1,), (1,)), ((), ()))
TN = (((0,), (0,)), ((), ()))
MESH = pl.DeviceIdType.MESH


def _tile(n, target):
    if n <= target:
        return n
    best = None
    for t in range(LANES, target + 1, LANES):
        if n % t == 0:
            best = t
    assert best is not None, (n, target)
    return best


def _params(*sem):
    return pltpu.CompilerParams(dimension_semantics=sem)


def _sigmoid(v):
    return 0.5 * jnp.tanh(0.5 * v) + 0.5


def _rms_bwd(x, gain, dh):
    r = lax.rsqrt(jnp.mean(x * x, axis=-1, keepdims=True) + EPS)
    xhat = x * r
    dxhat = dh * gain
    dx = r * (dxhat - xhat * jnp.mean(dxhat * xhat, axis=-1, keepdims=True))
    return dx, dh * xhat


def _weight(w):
    if not isinstance(w, tuple):
        return w, w.shape, pl.BlockSpec
    arr, layer = w

    def spec(block, imap):
        return pl.BlockSpec((None,) + tuple(block), lambda *a: (layer,) + tuple(imap(*a)))

    return arr, arr.shape[1:], spec


def _nmm(x, g, w, *, name, out_dtype, tm=1024, tn_target=1024):
    M, K = x.shape
    w, (_, N), wspec = _weight(w)
    tm = min(tm, M)
    tn = _tile(N, tn_target)

    def body(x_ref, g_ref, w_ref, z_ref, h_ref):
        @pl.when(pl.program_id(1) == 0)
        def _():
            xf = x_ref[...]
            r = lax.rsqrt(jnp.mean(xf * xf, axis=-1, keepdims=True) + EPS)
            h_ref[...] = (xf * r * g_ref[...]).astype(h_ref.dtype)

        z_ref[...] = jnp.dot(h_ref[...], w_ref[...], preferred_element_type=F32).astype(z_ref.dtype)

    return pl.pallas_call(
        body, name=name, grid=(M // tm, N // tn),
        in_specs=[pl.BlockSpec((tm, K), lambda i, j: (i, 0)),
                  pl.BlockSpec((1, K), lambda i, j: (0, 0)),
                  wspec((K, tn), lambda i, j: (0, j))],
        out_specs=[pl.BlockSpec((tm, tn), lambda i, j: (i, j)),
                   pl.BlockSpec((tm, K), lambda i, j: (i, 0))],
        out_shape=[jax.ShapeDtypeStruct((M, N), out_dtype), jax.ShapeDtypeStruct((M, K), CD)],
        compiler_params=_params("parallel", "arbitrary"),
    )(x, g, w)


def _mm_res(a, w, res, *, name, tm=1024, tn_target=1024):
    M, K = a.shape
    w, (_, N), wspec = _weight(w)
    tm = min(tm, M)
    tn = _tile(N, tn_target)

    def body(a_ref, w_ref, r_ref, o_ref):
        o_ref[...] = r_ref[...] + jnp.dot(a_ref[...].astype(CD), w_ref[...], preferred_element_type=F32)

    return pl.pallas_call(
        body, name=name, grid=(M // tm, N // tn),
        in_specs=[pl.BlockSpec((tm, K), lambda i, j: (i, 0)),
                  wspec((K, tn), lambda i, j: (0, j)),
                  pl.BlockSpec((tm, tn), lambda i, j: (i, j))],
        out_specs=pl.BlockSpec((tm, tn), lambda i, j: (i, j)),
        out_shape=jax.ShapeDtypeStruct((M, N), F32),
        compiler_params=_params("parallel", "arbitrary"),
    )(a, w, res)


def _mm_nt(a, w, *, name, out_dtype, tm=1024, tn_target=1024):
    M, K = a.shape
    w, (N, _), wspec = _weight(w)
    tm = min(tm, M)
    tn = _tile(N, tn_target)

    def body(a_ref, w_ref, o_ref):
        o_ref[...] = lax.dot_general(a_ref[...].astype(CD), w_ref[...], NT,
                                     preferred_element_type=F32).astype(o_ref.dtype)

    return pl.pallas_call(
        body, name=name, grid=(M // tm, N // tn),
        in_specs=[pl.BlockSpec((tm, K), lambda i, j: (i, 0)),
                  wspec((tn, K), lambda i, j: (j, 0))],
        out_specs=pl.BlockSpec((tm, tn), lambda i, j: (i, j)),
        out_shape=jax.ShapeDtypeStruct((M, N), out_dtype),
        compiler_params=_params("parallel", "arbitrary"),
    )(a, w)


def _mm_nt_normbwd(gy, w, x, gain, dres, *, name, tm=1024, tk_target=1408):
    M, K = gy.shape
    w, (D, _), wspec = _weight(w)
    tm = min(tm, M)
    tk = _tile(K, tk_target)
    nk = K // tk

    def body(g_ref, w_ref, x_ref, gain_ref, dres_ref, dx_ref, dg_ref, acc):
        i, k = pl.program_id(0), pl.program_id(1)

        @pl.when(k == 0)
        def _():
            acc[...] = jnp.zeros_like(acc)

        acc[...] += lax.dot_general(g_ref[...].astype(CD), w_ref[...], NT, preferred_element_type=F32)

        @pl.when(k == nk - 1)
        def _():
            dx, dg_rows = _rms_bwd(x_ref[...], gain_ref[...], acc[...])
            dx_ref[...] = dres_ref[...] + dx

            @pl.when(i == 0)
            def _():
                dg_ref[...] = jnp.zeros_like(dg_ref)

            dg_ref[...] += jnp.sum(dg_rows, axis=0, keepdims=True)

    return pl.pallas_call(
        body, name=name, grid=(M // tm, nk),
        in_specs=[pl.BlockSpec((tm, tk), lambda i, k: (i, k)),
                  wspec((D, tk), lambda i, k: (0, k)),
                  pl.BlockSpec((tm, D), lambda i, k: (i, 0)),
                  pl.BlockSpec((1, D), lambda i, k: (0, 0)),
                  pl.BlockSpec((tm, D), lambda i, k: (i, 0))],
        out_specs=[pl.BlockSpec((tm, D), lambda i, k: (i, 0)),
                   pl.BlockSpec((1, D), lambda i, k: (0, 0))],
        out_shape=[jax.ShapeDtypeStruct((M, D), F32), jax.ShapeDtypeStruct((1, D), F32)],
        scratch_shapes=[pltpu.VMEM((tm, D), F32)],
        compiler_params=_params("arbitrary", "arbitrary"),
    )(gy, w, x, gain, dres)


def _mm_tn(a, g, *, name, tt=2048, tk_target=1024, tn_target=1024, into=None):
    T, K = a.shape
    N = g.shape[1]
    tt = min(tt, T)
    tk = _tile(K, tk_target)
    tn = _tile(N, tn_target)

    def body(a_ref, g_ref, *rest):
        o_ref = rest[-1]

        @pl.when(pl.program_id(2) == 0)
        def _():
            o_ref[...] = jnp.zeros_like(o_ref)

        o_ref[...] += lax.dot_general(a_ref[...].astype(CD), g_ref[...].astype(CD), TN,
                                      preferred_element_type=F32)

    in_specs = [pl.BlockSpec((tt, tk), lambda i, j, t: (t, i)),
                pl.BlockSpec((tt, tn), lambda i, j, t: (t, j))]
    if into is None:
        operands, aliases = (a, g), {}
        out_spec = pl.BlockSpec((tk, tn), lambda i, j, t: (i, j))
        out_shape = jax.ShapeDtypeStruct((K, N), F32)
    else:
        stack, layer = into
        operands, aliases = (a, g, stack), {2: 0}
        in_specs.append(pl.BlockSpec(memory_space=pl.ANY))
        out_spec = pl.BlockSpec((None, tk, tn), lambda i, j, t: (layer, i, j))
        out_shape = jax.ShapeDtypeStruct(stack.shape, F32)
    return pl.pallas_call(
        body, name=name, grid=(K // tk, N // tn, T // tt),
        in_specs=in_specs, out_specs=out_spec, out_shape=out_shape, input_output_aliases=aliases,
        compiler_params=_params("parallel", "parallel", "arbitrary"),
    )(*operands)


POOL_W = 512
CONV_W = 512
POOL_GROUP = 128


MIX_ROWS = 64
LN_ROWS = 256
LN_BWD_ROWS = 512
SUB = 8


def _shifted(sh_sc, x, n_rows):
    for b in range(1, SUB):
        sh_sc[b, pl.ds(0, n_rows), :] = x[b:b + n_rows]


def _tap(sh_sc, src, r0, cols, start, rows):
    a, b = divmod(start, SUB)
    if b == 0:
        return src[pl.ds(r0 + SUB * a, rows), cols]
    return sh_sc[b, pl.ds(SUB * a, rows), :]


def _pool_rows(zp_ref, z_ref, cols, win, i, tt, first, pooled_sc):
    RB = min(MIX_ROWS, tt)
    hb = 2 * SUB
    for r in range(tt // RB):
        if r == 0:
            p = zp_ref[pl.ds(HALO - hb, hb), cols]
            v = jnp.concatenate([jnp.where(first, jnp.zeros_like(p), p), z_ref[pl.ds(0, RB), cols]], axis=0)
        else:
            v = z_ref[pl.ds(r * RB - hb, RB + hb), cols]
        u = v[hb:hb + RB]
        s = u
        for j in range(1, win):
            s = s + v[hb - j:hb - j + RB]
        t_glob = i * tt + r * RB + lax.broadcasted_iota(jnp.int32, (RB, 1), 0)
        cnt = jnp.minimum(t_glob + 1, win).astype(F32)
        pooled_sc[pl.ds(r * RB, RB), :] = (s / cnt - u).astype(pooled_sc.dtype)


def _fill_gl(gl_sc, zp_ref, z_ref, zn_ref, tt, first, last):
    ca, cb = pl.ds(POOL_W, CONV_W), pl.ds(POOL_W + CONV_W, CONV_W)
    g = zp_ref[:, ca] * _sigmoid(zp_ref[:, cb])
    gl_sc[pl.ds(0, HALO), :] = jnp.where(first, jnp.zeros_like(g), g)

    def rows(r, carry):
        r0 = pl.multiple_of(r * LN_ROWS, LN_ROWS)
        gl_sc[pl.ds(HALO + r0, LN_ROWS), :] = z_ref[pl.ds(r0, LN_ROWS), ca] * _sigmoid(z_ref[pl.ds(r0, LN_ROWS), cb])
        return carry

    lax.fori_loop(0, tt // LN_ROWS, rows, 0)
    if zn_ref is not None:
        g = zn_ref[:, ca] * _sigmoid(zn_ref[:, cb])
        gl_sc[pl.ds(HALO + tt, HALO), :] = jnp.where(last, jnp.zeros_like(g), g)


def _conv_rows(gl_sc, cv_sc, sh_sc, w_ref, b_ref, n_rows):
    RB = min(MIX_ROWS, n_rows)
    for c in range(CONV_W // LANES):
        cols = pl.ds(c * LANES, LANES)
        bias = b_ref[:, cols]

        def chunk(r0, rb):
            g = gl_sc[pl.ds(r0, rb + HALO), cols]
            _shifted(sh_sc, g, rb + HALO - SUB)
            cv = jnp.zeros((rb, LANES), F32) + bias
            for j in range(CONV_K):
                cv = cv + w_ref[pl.ds(j, 1), cols] * _tap(sh_sc, gl_sc, r0, cols, HALO - (CONV_K - 1) + j, rb)
            cv_sc[pl.ds(r0, rb), cols] = cv

        def body(r, carry):
            chunk(pl.multiple_of(r * RB, RB), RB)
            return carry

        lax.fori_loop(0, n_rows // RB, body, 0)
        if n_rows % RB:
            chunk((n_rows // RB) * RB, n_rows % RB)


def _mixer_fwd(z, pool_w, pool_scale, dw_w, dw_b, ln_g, ln_b, *, name, tt=512):
    T, C = z.shape
    tt = min(tt, T)
    n = T // tt
    hb = tt // HALO

    def body(zp_ref, z_ref, pw_ref, ps_ref, w_ref, b_ref, g_ref, bb_ref, o_ref, pooled_sc, gl_sc, cv_sc, sh_sc):
        i = pl.program_id(0)
        first = i == 0
        for gi, win in enumerate(POOL_WINDOWS):
            cols = pl.ds(gi * POOL_GROUP, POOL_GROUP)
            _pool_rows(zp_ref, z_ref, cols, win, i, tt, first, pooled_sc)
            ya = jnp.dot(pooled_sc[...], pw_ref[gi].astype(CD), preferred_element_type=F32)
            o_ref[:, cols] = (ya * ps_ref[:, cols]).astype(o_ref.dtype)
        _fill_gl(gl_sc, zp_ref, z_ref, None, tt, first, None)
        _conv_rows(gl_sc, cv_sc, sh_sc, w_ref, b_ref, tt)

        def ln_rows(r, carry):
            rows = pl.ds(pl.multiple_of(r * LN_ROWS, LN_ROWS), LN_ROWS)
            cv = cv_sc[rows, :]
            xc = cv - jnp.mean(cv, axis=-1, keepdims=True)
            yn = xc * lax.rsqrt(jnp.mean(xc * xc, axis=-1, keepdims=True) + EPS) * g_ref[...] + bb_ref[...]
            o_ref[rows, pl.ds(POOL_W, CONV_W)] = (yn * _sigmoid(yn)).astype(o_ref.dtype)
            return carry

        lax.fori_loop(0, tt // LN_ROWS, ln_rows, 0, unroll=4)

    full = lambda shape: pl.BlockSpec(shape, lambda i: (0,) * len(shape))
    return pl.pallas_call(
        body, name=name, grid=(n,),
        in_specs=[pl.BlockSpec((HALO, C), lambda i: (jnp.maximum(i * hb - 1, 0), 0)),
                  pl.BlockSpec((tt, C), lambda i: (i, 0)),
                  full((4, POOL_GROUP, POOL_GROUP)), full((1, POOL_W)), full((CONV_K + 1, CONV_W)),
                  full((1, CONV_W)), full((1, CONV_W)), full((1, CONV_W))],
        out_specs=pl.BlockSpec((tt, POOL_W + CONV_W), lambda i: (i, 0)),
        out_shape=jax.ShapeDtypeStruct((T, POOL_W + CONV_W), CD),
        scratch_shapes=[pltpu.VMEM((tt, POOL_GROUP), CD), pltpu.VMEM((tt + HALO, CONV_W), F32),
                        pltpu.VMEM((tt, CONV_W), F32), pltpu.VMEM((SUB, MIX_ROWS + HALO, LANES), F32)],
        compiler_params=_params("parallel"),
    )(z, z, pool_w, pool_scale, dw_w, dw_b, ln_g, ln_b)


def _mixer_bwd(z, dy, pool_w, pool_scale, dw_w, dw_b, ln_g, ln_b, *, name, tt=512):
    T, C = z.shape
    tt = min(tt, T)
    n = T // tt
    hb = tt // HALO
    R = tt + HALO
    RB = min(MIX_ROWS, tt)

    def body(zp_ref, z_ref, zn_ref, dy_ref, dyn_ref, pw_ref, ps_ref, w_ref, b_ref, g_ref, bb_ref,
             dz_ref, dpw_ref, dps_ref, dw_ref, db_ref, dg_ref, dbb_ref,
             pooled_sc, dm_sc, dpool_sc, dpe_sc, gl_sc, cv_sc, accw, accl, sh_sc, shd_sc):
        i = pl.program_id(0)
        first, last = i == 0, i == n - 1

        @pl.when(first)
        def _():
            for r in (dpw_ref, dps_ref, dw_ref, db_ref, dg_ref, dbb_ref):
                r[...] = jnp.zeros_like(r)

        def dy_rows(cols):
            nxt = dyn_ref[:, cols]
            return jnp.concatenate([dy_ref[:, cols], jnp.where(last, jnp.zeros_like(nxt), nxt)], axis=0)

        t_all = i * tt + lax.broadcasted_iota(jnp.int32, (R, 1), 0)
        for gi, win in enumerate(POOL_WINDOWS):
            cols = pl.ds(gi * POOL_GROUP, POOL_GROUP)
            _pool_rows(zp_ref, z_ref, cols, win, i, tt, first, pooled_sc)
            pw = pw_ref[gi].astype(CD)
            dya = dy_rows(cols)
            mm = jnp.dot(pooled_sc[...], pw, preferred_element_type=F32)
            dps_ref[:, cols] += jnp.sum(dya[:tt] * mm, axis=0, keepdims=True)
            dm_sc[...] = (dya * ps_ref[:, cols]).astype(CD)
            dpw_ref[gi] += lax.dot_general(pooled_sc[...], dm_sc[pl.ds(0, tt), :], TN, preferred_element_type=F32)
            dpool = lax.dot_general(dm_sc[...], pw, NT, preferred_element_type=F32)
            dpool_sc[...] = dpool
            dpe_sc[...] = dpool / jnp.minimum(t_all + 1, win).astype(F32)

            def du_rows(r, carry):
                r0 = pl.multiple_of(r * RB, RB)
                e = dpe_sc[pl.ds(r0, RB + 2 * SUB), :]
                du = -dpool_sc[pl.ds(r0, RB), :]
                for j in range(win):
                    du = du + e[j:j + RB]
                dz_ref[pl.ds(r0, RB), cols] = du.astype(dz_ref.dtype)
                return carry

            lax.fori_loop(0, tt // RB, du_rows, 0)

        _fill_gl(gl_sc, zp_ref, z_ref, zn_ref, tt, first, last)
        _conv_rows(gl_sc, cv_sc, sh_sc, w_ref, b_ref, R)
        accl[...] = jnp.zeros_like(accl)

        def ln_rows(r0, nr, in_tile):
            rows = pl.ds(r0, nr)
            cv = cv_sc[rows, :]
            xc = cv - jnp.mean(cv, axis=-1, keepdims=True)
            rstd = lax.rsqrt(jnp.mean(xc * xc, axis=-1, keepdims=True) + EPS)
            xhat = xc * rstd
            yn = xhat * g_ref[...] + bb_ref[...]
            sy = _sigmoid(yn)
            if in_tile:
                dyv = dy_ref[rows, pl.ds(POOL_W, CONV_W)]
            else:
                nxt = dyn_ref[:, pl.ds(POOL_W, CONV_W)]
                dyv = jnp.where(last, jnp.zeros_like(nxt), nxt)
            dyn = dyv * (sy * (1.0 + yn * (1.0 - sy)))
            if in_tile:
                accl[pl.ds(0, SUB), :] += jnp.sum((dyn * xhat).reshape(nr // SUB, SUB, CONV_W), axis=0)
                accl[pl.ds(SUB, SUB), :] += jnp.sum(dyn.reshape(nr // SUB, SUB, CONV_W), axis=0)
            dxh = dyn * g_ref[...]
            dcv = rstd * (dxh - jnp.mean(dxh, axis=-1, keepdims=True)
                          - xhat * jnp.mean(dxh * xhat, axis=-1, keepdims=True))
            cv_sc[rows, :] = dcv
            if in_tile:
                accl[pl.ds(2 * SUB, SUB), :] += jnp.sum(dcv.reshape(nr // SUB, SUB, CONV_W), axis=0)

        lnb = min(LN_BWD_ROWS, tt)

        def ln_body(r, carry):
            ln_rows(pl.multiple_of(r * lnb, lnb), lnb, True)
            return carry

        lax.fori_loop(0, tt // lnb, ln_body, 0)
        ln_rows(tt, HALO, False)
        dg_ref[...] += jnp.sum(accl[pl.ds(0, SUB), :], axis=0, keepdims=True)
        dbb_ref[...] += jnp.sum(accl[pl.ds(SUB, SUB), :], axis=0, keepdims=True)
        db_ref[...] += jnp.sum(accl[pl.ds(2 * SUB, SUB), :], axis=0, keepdims=True)

        accw[...] = jnp.zeros_like(accw)
        for c in range(CONV_W // LANES):
            cols = pl.ds(c * LANES, LANES)

            def chunk(r, carry):
                r0 = pl.multiple_of(r * RB, RB)
                d = cv_sc[pl.ds(r0, RB + HALO), cols]
                g = gl_sc[pl.ds(r0, RB + HALO), cols]
                _shifted(shd_sc, d, RB + HALO - SUB)
                _shifted(sh_sc, g, RB + HALO - SUB)
                d_t = d[:RB]
                dgl = jnp.zeros((RB, LANES), F32)
                for j in range(CONV_K):
                    dgl = dgl + w_ref[pl.ds(j, 1), cols] * _tap(shd_sc, cv_sc, r0, cols, CONV_K - 1 - j, RB)
                    prod = d_t * _tap(sh_sc, gl_sc, r0, cols, HALO - (CONV_K - 1) + j, RB)
                    accw[pl.ds(SUB * j, SUB), cols] += jnp.sum(prod.reshape(RB // SUB, SUB, LANES), axis=0)
                a_t = z_ref[pl.ds(r0, RB), pl.ds(POOL_W + c * LANES, LANES)]
                sb = _sigmoid(z_ref[pl.ds(r0, RB), pl.ds(POOL_W + CONV_W + c * LANES, LANES)])
                dz_ref[pl.ds(r0, RB), pl.ds(POOL_W + c * LANES, LANES)] = (dgl * sb).astype(dz_ref.dtype)
                dz_ref[pl.ds(r0, RB), pl.ds(POOL_W + CONV_W + c * LANES, LANES)] = (
                    dgl * a_t * sb * (1.0 - sb)).astype(dz_ref.dtype)
                return carry

            lax.fori_loop(0, tt // RB, chunk, 0)
        for j in range(CONV_K):
            dw_ref[pl.ds(j, 1), :] += jnp.sum(accw[pl.ds(SUB * j, SUB), :], axis=0, keepdims=True)

    full = lambda shape: pl.BlockSpec(shape, lambda i: (0,) * len(shape))
    nb = T // HALO
    outs = pl.pallas_call(
        body, name=name, grid=(n,),
        in_specs=[pl.BlockSpec((HALO, C), lambda i: (jnp.maximum(i * hb - 1, 0), 0)),
                  pl.BlockSpec((tt, C), lambda i: (i, 0)),
                  pl.BlockSpec((HALO, C), lambda i: (jnp.minimum((i + 1) * hb, nb - 1), 0)),
                  pl.BlockSpec((tt, 2 * POOL_W), lambda i: (i, 0)),
                  pl.BlockSpec((HALO, 2 * POOL_W), lambda i: (jnp.minimum((i + 1) * hb, nb - 1), 0)),
                  full((4, POOL_GROUP, POOL_GROUP)), full((1, POOL_W)), full((CONV_K + 1, CONV_W)),
                  full((1, CONV_W)), full((1, CONV_W)), full((1, CONV_W))],
        out_specs=[pl.BlockSpec((tt, C), lambda i: (i, 0)),
                   full((4, POOL_GROUP, POOL_GROUP)), full((1, POOL_W)), full((CONV_K + 1, CONV_W)),
                   full((1, CONV_W)), full((1, CONV_W)), full((1, CONV_W))],
        out_shape=[jax.ShapeDtypeStruct((T, C), CD),
                   jax.ShapeDtypeStruct((4, POOL_GROUP, POOL_GROUP), F32),
                   jax.ShapeDtypeStruct((1, POOL_W), F32),
                   jax.ShapeDtypeStruct((CONV_K + 1, CONV_W), F32),
                   jax.ShapeDtypeStruct((1, CONV_W), F32),
                   jax.ShapeDtypeStruct((1, CONV_W), F32),
                   jax.ShapeDtypeStruct((1, CONV_W), F32)],
        scratch_shapes=[pltpu.VMEM((tt, POOL_GROUP), CD), pltpu.VMEM((R, POOL_GROUP), CD),
                        pltpu.VMEM((R, POOL_GROUP), F32), pltpu.VMEM((R, POOL_GROUP), F32),
                        pltpu.VMEM((tt + 2 * HALO, CONV_W), F32), pltpu.VMEM((R, CONV_W), F32),
                        pltpu.VMEM((SUB * (CONV_K + 1), CONV_W), F32), pltpu.VMEM((3 * SUB, CONV_W), F32),
                        pltpu.VMEM((SUB, MIX_ROWS + HALO, LANES), F32),
                        pltpu.VMEM((SUB, MIX_ROWS + HALO, LANES), F32)],
        compiler_params=_params("arbitrary"),
    )(z, z, z, dy, dy, pool_w, pool_scale, dw_w, dw_b, ln_g, ln_b)
    return outs


CHUNK_HALO = 16
FFN_ROWS = 64
FFN_LANES = 128


def _rows(cur, prev, nxt, r, rb, before, after, cols, n_r, first, last):
    lo, hi = r * rb - before, r * rb + rb + after
    tt = n_r * rb
    parts = []
    if lo < 0:
        p = prev[pl.ds(HALO + lo, -lo), cols]
        parts.append(jnp.where(first, jnp.zeros_like(p), p))
        lo = 0
    parts.append(cur[pl.ds(lo, min(hi, tt) - lo), cols])
    if hi > tt:
        p = nxt[pl.ds(0, hi - tt), cols]
        parts.append(jnp.where(last, jnp.zeros_like(p), p))
    return parts[0] if len(parts) == 1 else jnp.concatenate(parts, axis=0)


def _ffn_mid_fwd(up, cw, cb, *, name, tt=512):
    T = up.shape[0]
    tt = min(tt, T)
    n = T // tt
    hb = tt // HALO
    RB, CW, HB = min(FFN_ROWS, tt), FFN_LANES, CHUNK_HALO
    n_r = tt // RB

    def body(a_ref, gp_ref, g_ref, w_ref, b_ref, o_ref):
        first = pl.program_id(0) == 0

        def col_chunk(c, carry):
            cols = pl.ds(pl.multiple_of(c * CW, CW), CW)
            w = w_ref[:, cols]
            b = b_ref[:, cols]
            for r in range(n_r):
                v = _rows(g_ref, gp_ref, None, r, RB, HB, 0, cols, n_r, first, None).astype(F32)
                gc = b + w[0:1] * v[HB - 2:HB - 2 + RB] + w[1:2] * v[HB - 1:HB - 1 + RB] + w[2:3] * v[HB:HB + RB]
                a = a_ref[pl.ds(r * RB, RB), cols].astype(F32)
                o_ref[pl.ds(r * RB, RB), cols] = (gc * _sigmoid(gc) * a).astype(o_ref.dtype)
            return carry

        lax.fori_loop(0, D_FF // CW, col_chunk, 0)

    return pl.pallas_call(
        body, name=name, grid=(n,),
        in_specs=[pl.BlockSpec((tt, D_FF), lambda i: (i, 0)),
                  pl.BlockSpec((HALO, D_FF), lambda i: (jnp.maximum(i * hb - 1, 0), 1)),
                  pl.BlockSpec((tt, D_FF), lambda i: (i, 1)),
                  pl.BlockSpec((8, D_FF), lambda i: (0, 0)),
                  pl.BlockSpec((1, D_FF), lambda i: (0, 0))],
        out_specs=pl.BlockSpec((tt, D_FF), lambda i: (i, 0)),
        out_shape=jax.ShapeDtypeStruct((T, D_FF), CD),
        compiler_params=_params("parallel"),
    )(up, up, up, cw, cb)


def _ffn_mid_bwd(up, dact, cw, cb, *, name, tt=512):
    T = up.shape[0]
    tt = min(tt, T)
    n = T // tt
    hb = tt // HALO
    nb = T // HALO
    RB, CW, HB = min(FFN_ROWS, tt), FFN_LANES, CHUNK_HALO
    n_r = tt // RB
    RE = RB + 8

    def body(a_ref, an_ref, gp_ref, g_ref, gn_ref, d_ref, dn_ref, w_ref, b_ref, dup_ref, dw_ref, db_ref, acc):
        i = pl.program_id(0)
        first, last = i == 0, i == n - 1

        @pl.when(first)
        def _():
            dw_ref[...] = jnp.zeros_like(dw_ref)
            db_ref[...] = jnp.zeros_like(db_ref)

        def col_chunk(c, carry):
            cols = pl.ds(pl.multiple_of(c * CW, CW), CW)
            w = w_ref[:, cols]
            b = b_ref[:, cols]
            part = [jnp.zeros((8, CW), F32) for _ in range(FFN_K + 1)]
            for r in range(n_r):
                v = _rows(g_ref, gp_ref, gn_ref, r, RB, HB, HB, cols, n_r, first, last).astype(F32)
                gs = [v[HB - 2 + j:HB - 2 + j + RE] for j in range(FFN_K)]
                gc = b + w[0:1] * gs[0] + w[1:2] * gs[1] + w[2:3] * gs[2]
                sg = _sigmoid(gc)
                d = _rows(d_ref, None, dn_ref, r, RB, 0, HB, cols, n_r, None, last).astype(F32)[:RE]
                a = _rows(a_ref, None, an_ref, r, RB, 0, HB, cols, n_r, None, last).astype(F32)[:RE]
                silu = gc * sg
                dgc = d * a * (sg + silu - silu * sg)
                dup_ref[pl.ds(r * RB, RB), cols] = (d[:RB] * silu[:RB]).astype(dup_ref.dtype)
                dg = w[2:3] * dgc[0:RB] + w[1:2] * dgc[1:RB + 1] + w[0:1] * dgc[2:RB + 2]
                dup_ref[pl.ds(r * RB, RB), pl.ds(pl.multiple_of(D_FF + c * CW, CW), CW)] = dg.astype(dup_ref.dtype)
                dgc_t = dgc[:RB]
                for j in range(FFN_K):
                    part[j] = part[j] + jnp.sum((dgc_t * gs[j][:RB]).reshape(RB // 8, 8, CW), axis=0)
                part[FFN_K] = part[FFN_K] + jnp.sum(dgc_t.reshape(RB // 8, 8, CW), axis=0)
            for j in range(FFN_K + 1):
                acc[pl.ds(8 * j, 8), cols] = part[j]
            return carry

        lax.fori_loop(0, D_FF // CW, col_chunk, 0)
        for j in range(FFN_K):
            dw_ref[pl.ds(j, 1), :] += jnp.sum(acc[pl.ds(8 * j, 8), :], axis=0, keepdims=True)
        db_ref[...] += jnp.sum(acc[pl.ds(8 * FFN_K, 8), :], axis=0, keepdims=True)

    nxt = lambda i: jnp.minimum((i + 1) * hb, nb - 1)
    return pl.pallas_call(
        body, name=name, grid=(n,),
        in_specs=[pl.BlockSpec((tt, D_FF), lambda i: (i, 0)),
                  pl.BlockSpec((HALO, D_FF), lambda i: (nxt(i), 0)),
                  pl.BlockSpec((HALO, D_FF), lambda i: (jnp.maximum(i * hb - 1, 0), 1)),
                  pl.BlockSpec((tt, D_FF), lambda i: (i, 1)),
                  pl.BlockSpec((HALO, D_FF), lambda i: (nxt(i), 1)),
                  pl.BlockSpec((tt, D_FF), lambda i: (i, 0)),
                  pl.BlockSpec((HALO, D_FF), lambda i: (nxt(i), 0)),
                  pl.BlockSpec((8, D_FF), lambda i: (0, 0)),
                  pl.BlockSpec((1, D_FF), lambda i: (0, 0))],
        out_specs=[pl.BlockSpec((tt, 2 * D_FF), lambda i: (i, 0)),
                   pl.BlockSpec((8, D_FF), lambda i: (0, 0)),
                   pl.BlockSpec((1, D_FF), lambda i: (0, 0))],
        out_shape=[jax.ShapeDtypeStruct((T, 2 * D_FF), CD),
                   jax.ShapeDtypeStruct((8, D_FF), F32),
                   jax.ShapeDtypeStruct((1, D_FF), F32)],
        scratch_shapes=[pltpu.VMEM((8 * (FFN_K + 1), D_FF), F32)],
        compiler_params=_params("arbitrary"),
    )(up, up, up, up, up, dact, dact, cw, cb)


def _xattn_probs(q, k):
    s = lax.dot_general(q, k, NT, preferred_element_type=F32) * XA_SCALE
    p = jnp.exp(s - jnp.max(s, axis=-1, keepdims=True))
    return p / jnp.sum(p, axis=-1, keepdims=True)


def _xattn_fwd(q, kv, *, name, tq=1024):
    T = q.shape[0]
    tq = min(tq, T)

    def body(q_ref, kv_ref, o_ref):
        for h in range(XA_HEADS):
            cols = pl.ds(h * XA_DH, XA_DH)
            p = _xattn_probs(q_ref[:, cols], kv_ref[:, cols])
            v = kv_ref[:, pl.ds(D_MODEL + h * XA_DH, XA_DH)]
            o_ref[:, cols] = jnp.dot(p.astype(CD), v, preferred_element_type=F32).astype(o_ref.dtype)

    return pl.pallas_call(
        body, name=name, grid=(T // tq,),
        in_specs=[pl.BlockSpec((tq, D_MODEL), lambda i: (i, 0)),
                  pl.BlockSpec((MEM_LEN, 2 * D_MODEL), lambda i: (0, 0))],
        out_specs=pl.BlockSpec((tq, D_MODEL), lambda i: (i, 0)),
        out_shape=jax.ShapeDtypeStruct((T, D_MODEL), CD),
        compiler_params=_params("parallel"),
    )(q, kv)


def _xattn_bwd(q, kv, do, *, name, tq=1024):
    T = q.shape[0]
    tq = min(tq, T)

    def body(q_ref, kv_ref, do_ref, dq_ref, dkv_ref):
        @pl.when(pl.program_id(0) == 0)
        def _():
            dkv_ref[...] = jnp.zeros_like(dkv_ref)

        for h in range(XA_HEADS):
            cols = pl.ds(h * XA_DH, XA_DH)
            vcols = pl.ds(D_MODEL + h * XA_DH, XA_DH)
            qh, kh, vh, doh = q_ref[:, cols], kv_ref[:, cols], kv_ref[:, vcols], do_ref[:, cols]
            p = _xattn_probs(qh, kh)
            dkv_ref[:, vcols] += lax.dot_general(p.astype(CD), doh, TN, preferred_element_type=F32)
            dp = lax.dot_general(doh, vh, NT, preferred_element_type=F32)
            ds = (p * (dp - jnp.sum(dp * p, axis=-1, keepdims=True)) * XA_SCALE).astype(CD)
            dq_ref[:, cols] = jnp.dot(ds, kh, preferred_element_type=F32).astype(dq_ref.dtype)
            dkv_ref[:, cols] += lax.dot_general(ds, qh, TN, preferred_element_type=F32)

    return pl.pallas_call(
        body, name=name, grid=(T // tq,),
        in_specs=[pl.BlockSpec((tq, D_MODEL), lambda i: (i, 0)),
                  pl.BlockSpec((MEM_LEN, 2 * D_MODEL), lambda i: (0, 0)),
                  pl.BlockSpec((tq, D_MODEL), lambda i: (i, 0))],
        out_specs=[pl.BlockSpec((tq, D_MODEL), lambda i: (i, 0)),
                   pl.BlockSpec((MEM_LEN, 2 * D_MODEL), lambda i: (0, 0))],
        out_shape=[jax.ShapeDtypeStruct((T, D_MODEL), CD),
                   jax.ShapeDtypeStruct((MEM_LEN, 2 * D_MODEL), F32)],
        compiler_params=_params("arbitrary"),
    )(q, kv, do)


C_W = Q_LORA + KV_LORA + LANES


def _rot(x):
    lane = lax.broadcasted_iota(jnp.int32, x.shape, x.ndim - 1)
    up = pltpu.roll(x, LANES - QK_ROPE // 2, x.ndim - 1)
    dn = pltpu.roll(x, QK_ROPE // 2, x.ndim - 1)
    lo, mid, hi = QK_NOPE, QK_NOPE + QK_ROPE // 2, QK_NOPE + QK_ROPE
    return jnp.where((lane >= lo) & (lane < mid), -up, jnp.where((lane >= mid) & (lane < hi), dn, 0.0))


def _mla_mid_fwd(c, qg, kvg, cs, sn, *, name, tt=512):
    T = c.shape[0]
    tt = min(tt, T)

    def body(c_ref, qg_ref, kg_ref, cs_ref, sn_ref, qn_ref, kn_ref, kpe_ref):
        cq = c_ref[:, pl.ds(0, Q_LORA)]
        qn_ref[...] = (cq * lax.rsqrt(jnp.mean(cq * cq, axis=-1, keepdims=True) + EPS)
                       * qg_ref[...]).astype(qn_ref.dtype)
        ck = c_ref[:, pl.ds(Q_LORA, KV_LORA)]
        kn_ref[...] = (ck * lax.rsqrt(jnp.mean(ck * ck, axis=-1, keepdims=True) + EPS)
                       * kg_ref[...]).astype(kn_ref.dtype)
        kp = c_ref[:, pl.ds(Q_LORA + KV_LORA, LANES)]
        kpe_ref[...] = kp * cs_ref[...] + _rot(kp) * sn_ref[...]

    row = lambda w: pl.BlockSpec((tt, w), lambda i: (i, 0))
    one = lambda w: pl.BlockSpec((1, w), lambda i: (0, 0))
    return pl.pallas_call(
        body, name=name, grid=(T // tt,),
        in_specs=[row(C_W), one(Q_LORA), one(KV_LORA), row(LANES), row(LANES)],
        out_specs=[row(Q_LORA), row(KV_LORA), row(LANES)],
        out_shape=[jax.ShapeDtypeStruct((T, Q_LORA), CD), jax.ShapeDtypeStruct((T, KV_LORA), CD),
                   jax.ShapeDtypeStruct((T, LANES), F32)],
        compiler_params=_params("parallel"),
    )(c, qg, kvg, cs, sn)


def _mla_mid_bwd(c, dqn, dkvn, dksum, qg, kvg, cs, sn, *, name, tt=512):
    T = c.shape[0]
    tt = min(tt, T)

    def body(c_ref, dq_ref, dk_ref, ds_ref, qg_ref, kg_ref, cs_ref, sn_ref, dc_ref, dqg_ref, dkg_ref):
        @pl.when(pl.program_id(0) == 0)
        def _():
            dqg_ref[...] = jnp.zeros_like(dqg_ref)
            dkg_ref[...] = jnp.zeros_like(dkg_ref)

        dx, dg = _rms_bwd(c_ref[:, pl.ds(0, Q_LORA)], qg_ref[...], dq_ref[...])
        dc_ref[:, pl.ds(0, Q_LORA)] = dx.astype(dc_ref.dtype)
        dqg_ref[...] += jnp.sum(dg, axis=0, keepdims=True)
        dx, dg = _rms_bwd(c_ref[:, pl.ds(Q_LORA, KV_LORA)], kg_ref[...], dk_ref[...])
        dc_ref[:, pl.ds(Q_LORA, KV_LORA)] = dx.astype(dc_ref.dtype)
        dkg_ref[...] += jnp.sum(dg, axis=0, keepdims=True)
        d = ds_ref[...]
        lane = lax.broadcasted_iota(jnp.int32, d.shape, 1)
        dkp = d * cs_ref[...] - _rot(d * sn_ref[...])
        dc_ref[:, pl.ds(Q_LORA + KV_LORA, LANES)] = jnp.where(
            (lane >= QK_NOPE) & (lane < QK_NOPE + QK_ROPE), dkp, 0.0).astype(dc_ref.dtype)

    row = lambda w: pl.BlockSpec((tt, w), lambda i: (i, 0))
    one = lambda w: pl.BlockSpec((1, w), lambda i: (0, 0))
    return pl.pallas_call(
        body, name=name, grid=(T // tt,),
        in_specs=[row(C_W), row(Q_LORA), row(KV_LORA), row(LANES), one(Q_LORA), one(KV_LORA),
                  row(LANES), row(LANES)],
        out_specs=[row(C_W), one(Q_LORA), one(KV_LORA)],
        out_shape=[jax.ShapeDtypeStruct((T, C_W), CD), jax.ShapeDtypeStruct((1, Q_LORA), F32),
                   jax.ShapeDtypeStruct((1, KV_LORA), F32)],
        compiler_params=_params("arbitrary"),
    )(c, dqn, dkvn, dksum, qg, kvg, cs, sn)


def _mla_qkv_fwd(qn, kvn, kpe, cs, sn, wq, wk, wv, *, name, tt=512):
    T = qn.shape[0]
    tt = min(tt, T)
    H = MLA_HEADS

    def body(qn_ref, kn_ref, kpe_ref, cs_ref, sn_ref, wq_ref, wk_ref, wv_ref, q_ref, k_ref, v_ref):
        qn_v, kn_v, kpe_v, cs_v, sn_v = qn_ref[...], kn_ref[...], kpe_ref[...], cs_ref[...], sn_ref[...]
        for h in range(H):
            q = jnp.dot(qn_v, wq_ref[h], preferred_element_type=F32)
            q_ref[h] = (q * cs_v + _rot(q) * sn_v).astype(q_ref.dtype)
            k_ref[h] = (jnp.dot(kn_v, wk_ref[h], preferred_element_type=F32) + kpe_v).astype(k_ref.dtype)
            v_ref[h] = jnp.dot(kn_v, wv_ref[h], preferred_element_type=F32).astype(v_ref.dtype)

    row = lambda w: pl.BlockSpec((tt, w), lambda i: (i, 0))
    wsp = lambda k: pl.BlockSpec((H, k, LANES), lambda i: (0, 0, 0))
    hsp = pl.BlockSpec((H, tt, LANES), lambda i: (0, i, 0))
    sh = jax.ShapeDtypeStruct((H, T, LANES), CD)
    return pl.pallas_call(
        body, name=name, grid=(T // tt,),
        in_specs=[row(Q_LORA), row(KV_LORA), row(LANES), row(LANES), row(LANES),
                  wsp(Q_LORA), wsp(KV_LORA), wsp(KV_LORA)],
        out_specs=[hsp, hsp, hsp], out_shape=[sh, sh, sh],
        compiler_params=_params("parallel"),
    )(qn, kvn, kpe, cs, sn, wq, wk, wv)


def _mla_qkv_bwd(dq, dk, dv, qn, kvn, cs, sn, wq, wk, wv, *, name, tt=512):
    T = qn.shape[0]
    tt = min(tt, T)
    H = MLA_HEADS

    def body(dq_ref, dk_ref, dv_ref, qn_ref, kn_ref, cs_ref, sn_ref, wq_ref, wk_ref, wv_ref,
             dqn_ref, dkn_ref, dks_ref, dwq_ref, dwk_ref, dwv_ref):
        @pl.when(pl.program_id(0) == 0)
        def _():
            for r in (dwq_ref, dwk_ref, dwv_ref):
                r[...] = jnp.zeros_like(r)

        qn_v, kn_v, cs_v, sn_v = qn_ref[...], kn_ref[...], cs_ref[...], sn_ref[...]
        dqn = jnp.zeros((tt, Q_LORA), F32)
        dkn = jnp.zeros((tt, KV_LORA), F32)
        dks = jnp.zeros((tt, LANES), F32)
        for h in range(H):
            d = dq_ref[h]
            dqh = (d * cs_v - _rot(d * sn_v)).astype(CD)
            dkh, dvh = dk_ref[h], dv_ref[h]
            dqn = dqn + lax.dot_general(dqh, wq_ref[h], NT, preferred_element_type=F32)
            dkn = dkn + lax.dot_general(dkh, wk_ref[h], NT, preferred_element_type=F32)
            dkn = dkn + lax.dot_general(dvh, wv_ref[h], NT, preferred_element_type=F32)
            dks = dks + dkh.astype(F32)
            dwq_ref[h] += lax.dot_general(qn_v, dqh, TN, preferred_element_type=F32)
            dwk_ref[h] += lax.dot_general(kn_v, dkh, TN, preferred_element_type=F32)
            dwv_ref[h] += lax.dot_general(kn_v, dvh, TN, preferred_element_type=F32)
        dqn_ref[...] = dqn
        dkn_ref[...] = dkn
        dks_ref[...] = dks

    row = lambda w: pl.BlockSpec((tt, w), lambda i: (i, 0))
    wsp = lambda k: pl.BlockSpec((H, k, LANES), lambda i: (0, 0, 0))
    hsp = pl.BlockSpec((H, tt, LANES), lambda i: (0, i, 0))
    return pl.pallas_call(
        body, name=name, grid=(T // tt,),
        in_specs=[hsp, hsp, hsp, row(Q_LORA), row(KV_LORA), row(LANES), row(LANES),
                  wsp(Q_LORA), wsp(KV_LORA), wsp(KV_LORA)],
        out_specs=[row(Q_LORA), row(KV_LORA), row(LANES), wsp(Q_LORA), wsp(KV_LORA), wsp(KV_LORA)],
        out_shape=[jax.ShapeDtypeStruct((T, Q_LORA), F32), jax.ShapeDtypeStruct((T, KV_LORA), F32),
                   jax.ShapeDtypeStruct((T, LANES), F32),
                   jax.ShapeDtypeStruct((H, Q_LORA, LANES), F32),
                   jax.ShapeDtypeStruct((H, KV_LORA, LANES), F32),
                   jax.ShapeDtypeStruct((H, KV_LORA, LANES), F32)],
        compiler_params=_params("arbitrary"),
    )(dq, dk, dv, qn, kvn, cs, sn, wq, wk, wv)


FLASH_BLOCK = 1024
FLASH_ROWS = 128
EXP2_SCALE = MLA_SCALE * math.log2(math.e)


def _causal_steps(nq, by_key):
    pairs = [(i, j) for j in range(nq) for i in range(j, nq)] if by_key else \
            [(i, j) for i in range(nq) for j in range(i + 1)]
    return (jnp.asarray([p[0] for p in pairs], jnp.int32), jnp.asarray([p[1] for p in pairs], jnp.int32))


def _raw_scores(q, k, masked, first_row=0):
    s = lax.dot_general(q, k, NT, preferred_element_type=F32)
    if masked:
        row = lax.broadcasted_iota(jnp.int32, s.shape, 0) + first_row
        col = lax.broadcasted_iota(jnp.int32, s.shape, 1)
        s = jnp.where(col <= row, s, NEG)
    return s


def _flash_fwd(q, k, v, *, name):
    H, T, _ = q.shape
    tq = min(FLASH_BLOCK, T)
    nq = T // tq
    i_tab, j_tab = _causal_steps(nq, by_key=False)

    rb = min(FLASH_ROWS, tq)

    def body(i_tab, j_tab, q_ref, k_ref, v_ref, o_ref, lse_ref, m_sc, l_sc, acc, s_sc, p_sc):
        t = pl.program_id(1)
        i, j = i_tab[t], j_tab[t]

        @pl.when(j == 0)
        def _():
            m_sc[...] = jnp.full_like(m_sc, NEG)
            l_sc[...] = jnp.zeros_like(l_sc)
            acc[...] = jnp.zeros_like(acc)

        hb = tq // 2

        def step(masked):
            lane = lax.broadcasted_iota(jnp.int32, (tq, LANES), 1)
            top, bot = pl.ds(0, hb), pl.ds(hb, hb)
            alphas, pvs = [], []
            for h in range(2):
                if masked:
                    s_sc[h, top, top] = _raw_scores(q_ref[h, top, :], k_ref[h, top, :], True)
                    s_sc[h, bot, :] = _raw_scores(q_ref[h, bot, :], k_ref[h], True, first_row=hb)
                    m_cur = jnp.concatenate([jnp.max(s_sc[h, top, top], axis=-1, keepdims=True),
                                             jnp.max(s_sc[h, bot, :], axis=-1, keepdims=True)], axis=0)
                else:
                    s_sc[h] = _raw_scores(q_ref[h], k_ref[h], False)
                    m_cur = jnp.max(s_sc[h], axis=-1, keepdims=True)
                m_prev = m_sc[h]
                m_new = jnp.maximum(m_prev, m_cur)
                alpha = jnp.exp2((m_prev - m_new) * EXP2_SCALE)
                m_sc[h] = m_new
                for r in range(tq // rb):
                    rows = pl.ds(r * rb, rb)
                    m_r = m_sc[h, rows, :]
                    part = jnp.zeros((rb, LANES), F32)
                    keys = hb if masked and r * rb < hb else tq
                    for c in range(keys // LANES):
                        cols = pl.ds(c * LANES, LANES)
                        p = jnp.exp2((s_sc[h, rows, cols] - m_r) * EXP2_SCALE)
                        part = part + p
                        p_sc[h, rows, cols] = p.astype(CD)
                    l_sc[h, rows, :] = (alpha[r * rb:(r + 1) * rb] * l_sc[h, rows, :]
                                        + jnp.sum(part, axis=-1, keepdims=True))
                alphas.append(alpha)
                if masked:
                    pvs.append(jnp.concatenate(
                        [jnp.dot(p_sc[h, top, top], v_ref[h, top, :], preferred_element_type=F32),
                         jnp.dot(p_sc[h, bot, :], v_ref[h], preferred_element_type=F32)], axis=0))
                else:
                    pvs.append(jnp.dot(p_sc[h], v_ref[h], preferred_element_type=F32))
            acc[...] = acc[...] * jnp.where(lane < V_HEAD, alphas[0], alphas[1]) + pvs[0] + pvs[1]

        @pl.when(j < i)
        def _():
            step(False)

        @pl.when(j == i)
        def _():
            step(True)
            lane = lax.broadcasted_iota(jnp.int32, (tq, LANES), 1)
            o_ref[...] = (acc[...] / jnp.where(lane < V_HEAD, l_sc[0], l_sc[1])).astype(o_ref.dtype)
            for h in range(2):
                lse_ref[h] = m_sc[h] * EXP2_SCALE + jnp.log2(l_sc[h])

    qsp = pl.BlockSpec((2, tq, LANES), lambda p, t, it, jt: (p, it[t], 0))
    ksp = pl.BlockSpec((2, tq, LANES), lambda p, t, it, jt: (p, jt[t], 0))
    return pl.pallas_call(
        body, name=name,
        grid_spec=pltpu.PrefetchScalarGridSpec(
            num_scalar_prefetch=2, grid=(H // 2, int(i_tab.shape[0])),
            in_specs=[qsp, ksp, ksp],
            out_specs=[pl.BlockSpec((tq, LANES), lambda p, t, it, jt: (it[t], p)), qsp],
            scratch_shapes=[pltpu.VMEM((2, tq, LANES), F32), pltpu.VMEM((2, tq, LANES), F32),
                            pltpu.VMEM((tq, LANES), F32),
                            pltpu.VMEM((2, tq, tq), F32), pltpu.VMEM((2, tq, tq), CD)]),
        out_shape=[jax.ShapeDtypeStruct((T, H * V_HEAD), CD), jax.ShapeDtypeStruct((H, T, LANES), F32)],
        compiler_params=_params("parallel", "arbitrary"),
    )(i_tab, j_tab, q, k, v)


def _flash_delta(o, do, *, name, tt=512):
    T = o.shape[0]
    tt = min(tt, T)
    H = MLA_HEADS

    def body(o_ref, do_ref, dl_ref):
        lane = lax.broadcasted_iota(jnp.int32, (tt, LANES), 1)
        for p in range(H // 2):
            cols = pl.ds(p * LANES, LANES)
            prod = do_ref[:, cols].astype(F32) * o_ref[:, cols].astype(F32)
            d0 = jnp.sum(jnp.where(lane < V_HEAD, prod, 0.0), axis=-1, keepdims=True)
            d1 = jnp.sum(jnp.where(lane < V_HEAD, 0.0, prod), axis=-1, keepdims=True)
            dl_ref[2 * p] = jnp.broadcast_to(d0, (tt, LANES))
            dl_ref[2 * p + 1] = jnp.broadcast_to(d1, (tt, LANES))

    row = pl.BlockSpec((tt, H * V_HEAD), lambda i: (i, 0))
    return pl.pallas_call(
        body, name=name, grid=(T // tt,), in_specs=[row, row],
        out_specs=pl.BlockSpec((H, tt, LANES), lambda i: (0, i, 0)),
        out_shape=jax.ShapeDtypeStruct((H, T, LANES), F32),
        compiler_params=_params("parallel"),
    )(o, do)


def _flash_bwd(q, k, v, do, lse, delta, *, name):
    H, T, _ = q.shape
    tq = min(FLASH_BLOCK, T)
    nq = T // tq
    i_tab, j_tab = _causal_steps(nq, by_key=True)

    def body(i_tab, j_tab, q_ref, k_ref, v_ref, do_ref, lse_ref, dl_ref, dq_ref, dk_ref, dv_ref, dk_acc, dv_acc):
        t = pl.program_id(1)
        i, j = i_tab[t], j_tab[t]
        rows = pl.ds(pl.multiple_of(i * tq, tq), tq)

        @pl.when(t == 0)
        def _():
            dq_ref[...] = jnp.zeros_like(dq_ref)

        def block(h, qr, kr, first_row, masked):
            qh, kh, vh, do_v = q_ref[h, qr, :], k_ref[h, kr, :], v_ref[h, kr, :], do_ref[qr, :]
            s = _raw_scores(qh, kh, masked, first_row)
            p = jnp.exp2(s * EXP2_SCALE - lse_ref[h, qr, :][:, :1])
            dv_acc[h, kr, :] += lax.dot_general(p.astype(CD), do_v, TN, preferred_element_type=F32)
            dp = lax.dot_general(do_v, vh, NT, preferred_element_type=F32)
            ds = (p * (dp - dl_ref[h, qr, :][:, :1]) * MLA_SCALE).astype(CD)
            dk_acc[h, kr, :] += lax.dot_general(ds, qh, TN, preferred_element_type=F32)
            dq_rows = pl.ds(pl.multiple_of(i * tq + qr.start, qr.size), qr.size)
            dq_ref[h, dq_rows, :] += jnp.dot(ds, kh, preferred_element_type=F32)

        def step(masked):
            hb = tq // 2
            for h in range(2):
                if masked:
                    block(h, pl.ds(0, hb), pl.ds(0, hb), 0, True)
                    block(h, pl.ds(hb, hb), pl.ds(0, tq), hb, True)
                else:
                    block(h, pl.ds(0, tq), pl.ds(0, tq), 0, False)

        @pl.when(i == j)
        def _():
            dk_acc[...] = jnp.zeros_like(dk_acc)
            dv_acc[...] = jnp.zeros_like(dv_acc)
            step(True)

        @pl.when(i > j)
        def _():
            step(False)

        @pl.when(i == nq - 1)
        def _():
            lane = lax.broadcasted_iota(jnp.int32, (tq, LANES), 1)
            dk_ref[...] = dk_acc[...].astype(dk_ref.dtype)
            dv_ref[0] = jnp.where(lane < V_HEAD, dv_acc[0], 0.0).astype(dv_ref.dtype)
            dv_ref[1] = jnp.where(lane < V_HEAD, 0.0, dv_acc[1]).astype(dv_ref.dtype)

    qsp = pl.BlockSpec((2, tq, LANES), lambda p, t, it, jt: (p, it[t], 0))
    ksp = pl.BlockSpec((2, tq, LANES), lambda p, t, it, jt: (p, jt[t], 0))
    osp = pl.BlockSpec((tq, LANES), lambda p, t, it, jt: (it[t], p))
    sh = jax.ShapeDtypeStruct((H, T, LANES), CD)
    return pl.pallas_call(
        body, name=name,
        grid_spec=pltpu.PrefetchScalarGridSpec(
            num_scalar_prefetch=2, grid=(H // 2, int(i_tab.shape[0])),
            in_specs=[qsp, ksp, ksp, osp, qsp, qsp],
            out_specs=[pl.BlockSpec((2, T, LANES), lambda p, t, it, jt: (p, 0, 0)), ksp, ksp],
            scratch_shapes=[pltpu.VMEM((2, tq, LANES), F32), pltpu.VMEM((2, tq, LANES), F32)]),
        out_shape=[jax.ShapeDtypeStruct((H, T, LANES), F32), sh, sh],
        compiler_params=_params("parallel", "arbitrary"),
    )(i_tab, j_tab, q, k, v, do, lse, delta)


def _loss_head(x, g, target, *, name, tt=1024):
    T, D = x.shape
    tt = min(tt, T)

    def body(x_ref, g_ref, t_ref, dx_ref, dg_ref, loss_ref):
        @pl.when(pl.program_id(0) == 0)
        def _():
            dg_ref[...] = jnp.zeros_like(dg_ref)
            loss_ref[...] = jnp.zeros_like(loss_ref)

        xv, gv = x_ref[...], g_ref[...]
        r = lax.rsqrt(jnp.mean(xv * xv, axis=-1, keepdims=True) + EPS)
        err = xv * r * gv - t_ref[...]
        tok = jnp.mean(err * err, axis=-1, keepdims=True)
        loss_ref[...] += 0.5 * jnp.sum(tok, axis=0, keepdims=True)
        dx, dg_rows = _rms_bwd(xv, gv, err * (1.0 / D))
        dx_ref[...] = dx
        dg_ref[...] += jnp.sum(dg_rows, axis=0, keepdims=True)

    return pl.pallas_call(
        body, name=name, grid=(T // tt,),
        in_specs=[pl.BlockSpec((tt, D), lambda i: (i, 0)), pl.BlockSpec((1, D), lambda i: (0, 0)),
                  pl.BlockSpec((tt, D), lambda i: (i, 0))],
        out_specs=[pl.BlockSpec((tt, D), lambda i: (i, 0)), pl.BlockSpec((1, D), lambda i: (0, 0)),
                   pl.BlockSpec((1, LANES), lambda i: (0, 0))],
        out_shape=[jax.ShapeDtypeStruct((T, D), F32), jax.ShapeDtypeStruct((1, D), F32),
                   jax.ShapeDtypeStruct((1, LANES), F32)],
        compiler_params=_params("arbitrary"),
    )(x, g, target)


def _rope_tables(positions):
    inv = 1.0 / (ROPE_THETA ** (jnp.arange(0, QK_ROPE, 2, dtype=F32) / QK_ROPE))
    ang = positions.astype(F32)[:, None] * inv
    c, s = jnp.cos(ang), jnp.sin(ang)
    T = positions.shape[0]
    cs = jnp.concatenate([jnp.ones((T, QK_NOPE), F32), c, c, jnp.zeros((T, LANES - QK_NOPE - QK_ROPE), F32)], 1)
    sn = jnp.concatenate([jnp.zeros((T, QK_NOPE), F32), s, s, jnp.zeros((T, LANES - QK_NOPE - QK_ROPE), F32)], 1)
    return cs, sn


def _pad_rows(w, rows):
    return jnp.concatenate([w, jnp.zeros((rows - w.shape[0],) + w.shape[1:], w.dtype)], 0)


def _mla_weights(w_dq_dkv, w_uq, w_ukv):
    K = w_dq_dkv.shape[0]
    z = lambda n: jnp.zeros((K, n), w_dq_dkv.dtype)
    wc = jnp.concatenate([w_dq_dkv[:, :Q_LORA + KV_LORA], z(QK_NOPE), w_dq_dkv[:, Q_LORA + KV_LORA:],
                          z(LANES - QK_NOPE - QK_ROPE)], 1)
    wq = w_uq.reshape(Q_LORA, MLA_HEADS, QK_NOPE + QK_ROPE).transpose(1, 0, 2)
    wq = jnp.concatenate([wq, jnp.zeros((MLA_HEADS, Q_LORA, LANES - QK_NOPE - QK_ROPE), wq.dtype)], 2)
    wkv = w_ukv.reshape(KV_LORA, MLA_HEADS, QK_NOPE + V_HEAD).transpose(1, 0, 2)
    zero = jnp.zeros_like(wkv[:, :, :QK_NOPE])
    wk = jnp.concatenate([wkv[:, :, :QK_NOPE], zero], 2)
    wv_lo = jnp.concatenate([wkv[:, :, QK_NOPE:], zero], 2)
    wv_hi = jnp.concatenate([zero, wkv[:, :, QK_NOPE:]], 2)
    odd = (jnp.arange(MLA_HEADS) % 2 == 1)[:, None, None]
    wv = jnp.where(odd, wv_hi, wv_lo)
    return wc, wq, wk, wv


def _mla_weight_grads(dwc, dwq, dwk, dwv):
    d_dq = jnp.concatenate([dwc[:, :Q_LORA + KV_LORA],
                            dwc[:, Q_LORA + KV_LORA + QK_NOPE:Q_LORA + KV_LORA + QK_NOPE + QK_ROPE]], 1)
    d_uq = dwq[:, :, :QK_NOPE + QK_ROPE].transpose(1, 0, 2).reshape(Q_LORA, MLA_HEADS * (QK_NOPE + QK_ROPE))
    odd = (jnp.arange(MLA_HEADS) % 2 == 1)[:, None, None]
    dv = jnp.where(odd, dwv[:, :, V_HEAD:], dwv[:, :, :V_HEAD])
    d_ukv = jnp.concatenate([dwk[:, :, :QK_NOPE], dv], 2).transpose(1, 0, 2).reshape(
        KV_LORA, MLA_HEADS * (QK_NOPE + V_HEAD))
    return d_dq, d_uq, d_ukv


def _local_step(x, mem, positions, target, W):
    G = {}
    row = lambda v: v.reshape(1, -1)
    cs, sn = _rope_tables(positions)
    saved = []
    for l in range(DEPTH):
        L = f"l{l}"
        s = {"x0": x}
        if l % 2 == 0:
            e = l // 2
            s["z"], s["h"] = _nmm(x, row(W["norm_mix_g"][l]), (W["pc_w_in"], e), name=f"{L}_mix_in", out_dtype=F32)
            s["dw_w"] = _pad_rows(W["conv_dw_w"][e], CONV_K + 1)
            s["mix_p"] = (W["pool_w"][e], row(W["pool_scale"][e]), s["dw_w"], row(W["conv_dw_b"][e]),
                          row(W["conv_ln_g"][e]), row(W["conv_ln_b"][e]))
            s["ycat"] = _mixer_fwd(s["z"], *s["mix_p"], name=f"{L}_mix_mid")
            x = _mm_res(s["ycat"], (W["pc_w_out"], e), x, name=f"{L}_mix_out")
        else:
            o = l // 2
            wc, wq, wk, wv = _mla_weights(W["mla_w_dq_dkv"][o], W["mla_w_uq"][o], W["mla_w_ukv"][o])
            s["mla_w"] = (wc, wq, wk, wv)
            s["c"], s["h"] = _nmm(x, row(W["norm_mix_g"][l]), wc, name=f"{L}_mla_down", out_dtype=F32)
            s["qg"], s["kvg"] = row(W["mla_q_norm_g"][o]), row(W["mla_kv_norm_g"][o])
            s["qn"], s["kvn"], kpe = _mla_mid_fwd(s["c"], s["qg"], s["kvg"], cs, sn, name=f"{L}_mla_mid")
            s["q"], s["k"], s["v"] = _mla_qkv_fwd(s["qn"], s["kvn"], kpe, cs, sn, wq, wk, wv, name=f"{L}_mla_qkv")
            s["o"], s["lse"] = _flash_fwd(s["q"], s["k"], s["v"], name=f"{L}_mla_attn")
            x = _mm_res(s["o"], (W["mla_w_o"], o), x, name=f"{L}_mla_out")
        s["x1"] = x
        s["xq"], s["hx"] = _nmm(x, row(W["norm_xa_g"][l]), (W["xa_wq"], l), name=f"{L}_xa_q", out_dtype=CD)
        s["xkv"], s["hm"] = _nmm(mem, row(W["norm_mem_g"][l]), (W["xa_wkv"], l), name=f"{L}_xa_kv", out_dtype=CD)
        s["xo"] = _xattn_fwd(s["xq"], s["xkv"], name=f"{L}_xa_attn")
        x = _mm_res(s["xo"], (W["xa_wo"], l), x, name=f"{L}_xa_out")
        s["x2"] = x
        s["up"], s["hf"] = _nmm(x, row(W["norm_ffn_g"][l]), (W["ffn_w_up"], l), name=f"{L}_ffn_up", out_dtype=CD,
                                tn_target=1408)
        s["cw"], s["cb"] = _pad_rows(W["ffn_conv_w"][l], 8), row(W["ffn_conv_b"][l])
        s["act"] = _ffn_mid_fwd(s["up"], s["cw"], s["cb"], name=f"{L}_ffn_mid")
        x = _mm_res(s["act"], (W["ffn_w_down"], l), x, name=f"{L}_ffn_down")
        saved.append(s)
    dx, G["final_norm_g"], loss = _loss_head(x, row(W["final_norm_g"]), target, name="loss_head")
    G["final_norm_g"] = G["final_norm_g"].reshape(-1)

    per_layer = {}

    def put(name, l, val):
        per_layer.setdefault(name, {})[l] = val

    def put_dw(weight, l, a, g, **kw):
        stack = G[weight] if weight in G else lax.empty(W[weight].shape, F32)
        G[weight] = _mm_tn(a, g, into=(stack, l), **kw)

    for l in reversed(range(DEPTH)):
        L = f"l{l}"
        s = saved[l]
        put_dw("ffn_w_down", l, s["act"], dx, name=f"{L}_ffn_down_dw", tk_target=1408)
        dact = _mm_nt(dx, (W["ffn_w_down"], l), name=f"{L}_ffn_down_dx", out_dtype=CD, tn_target=1408)
        dup, dcw, dcb = _ffn_mid_bwd(s["up"], dact, s["cw"], s["cb"], name=f"{L}_ffn_mid_bwd")
        put("ffn_conv_w", l, dcw[:FFN_K])
        put("ffn_conv_b", l, dcb[0])
        put_dw("ffn_w_up", l, s["hf"], dup, name=f"{L}_ffn_up_dw", tn_target=1408)
        dx, dg = _mm_nt_normbwd(dup, (W["ffn_w_up"], l), s["x2"], row(W["norm_ffn_g"][l]), dx, name=f"{L}_ffn_up_dx")
        put("norm_ffn_g", l, dg[0])
        put_dw("xa_wo", l, s["xo"], dx, name=f"{L}_xa_out_dw")
        do = _mm_nt(dx, (W["xa_wo"], l), name=f"{L}_xa_out_dx", out_dtype=CD)
        dq, dkv = _xattn_bwd(s["xq"], s["xkv"], do, name=f"{L}_xa_attn_bwd")
        put_dw("xa_wq", l, s["hx"], dq, name=f"{L}_xa_q_dw")
        dx, dg = _mm_nt_normbwd(dq, (W["xa_wq"], l), s["x1"], row(W["norm_xa_g"][l]), dx, name=f"{L}_xa_q_dx")
        put("norm_xa_g", l, dg[0])
        put_dw("xa_wkv", l, s["hm"], dkv, name=f"{L}_xa_kv_dw", tt=MEM_LEN)
        _, dg = _mm_nt_normbwd(dkv, (W["xa_wkv"], l), mem, row(W["norm_mem_g"][l]), jnp.zeros_like(mem),
                               name=f"{L}_xa_kv_dx", tm=MEM_LEN)
        put("norm_mem_g", l, dg[0])
        if l % 2 == 0:
            e = l // 2
            put_dw("pc_w_out", e, s["ycat"], dx, name=f"{L}_mix_out_dw")
            dy = _mm_nt(dx, (W["pc_w_out"], e), name=f"{L}_mix_out_dx", out_dtype=F32)
            dz, dpw, dps, ddw, ddb, dlg, dlb = _mixer_bwd(s["z"], dy, *s["mix_p"], name=f"{L}_mix_mid_bwd")
            put("pool_w", e, dpw)
            put("pool_scale", e, dps[0])
            put("conv_dw_w", e, ddw[:CONV_K])
            put("conv_dw_b", e, ddb[0])
            put("conv_ln_g", e, dlg[0])
            put("conv_ln_b", e, dlb[0])
            put_dw("pc_w_in", e, s["h"], dz, name=f"{L}_mix_in_dw")
            dx, dg = _mm_nt_normbwd(dz, (W["pc_w_in"], e), s["x0"], row(W["norm_mix_g"][l]), dx, name=f"{L}_mix_in_dx")
        else:
            o = l // 2
            wc, wq, wk, wv = s["mla_w"]
            put_dw("mla_w_o", o, s["o"], dx, name=f"{L}_mla_out_dw")
            do = _mm_nt(dx, (W["mla_w_o"], o), name=f"{L}_mla_out_dx", out_dtype=CD)
            delta = _flash_delta(s["o"], do, name=f"{L}_mla_attn_delta")
            dq, dk, dv = _flash_bwd(s["q"], s["k"], s["v"], do, s["lse"], delta, name=f"{L}_mla_attn_bwd")
            dqn, dkvn, dks, dwq, dwk, dwv = _mla_qkv_bwd(dq, dk, dv, s["qn"], s["kvn"], cs, sn, wq, wk, wv,
                                                         name=f"{L}_mla_qkv_bwd")
            dc, dqg, dkg = _mla_mid_bwd(s["c"], dqn, dkvn, dks, s["qg"], s["kvg"], cs, sn, name=f"{L}_mla_mid_bwd")
            put("mla_q_norm_g", o, dqg[0])
            put("mla_kv_norm_g", o, dkg[0])
            dwc = _mm_tn(s["h"], dc, name=f"{L}_mla_down_dw")
            d_dq, d_uq, d_ukv = _mla_weight_grads(dwc, dwq, dwk, dwv)
            put("mla_w_dq_dkv", o, d_dq)
            put("mla_w_uq", o, d_uq)
            put("mla_w_ukv", o, d_ukv)
            dx, dg = _mm_nt_normbwd(dc, wc, s["x0"], row(W["norm_mix_g"][l]), dx, name=f"{L}_mla_down_dx",
                                    tk_target=768)
        put("norm_mix_g", l, dg[0])
    for name, d in per_layer.items():
        G[name] = jnp.stack([d[i] for i in sorted(d)], 0)
    return loss, dx, G


_ANY = pl.BlockSpec(memory_space=pl.ANY)


def _all_gather(xs, *, name):
    n = len(xs)

    def body(*refs):
        x_refs, out_refs = refs[:n], refs[n:2 * n]
        send_sems, recv_sems, local_sems = refs[2 * n:]
        mx, my, mc = lax.axis_index("x"), lax.axis_index("y"), lax.axis_index("c")
        me, sibling = (mx, my, mc), (mx, my, 1 - mc)
        xn, yn, dg = (1 - mx, my), (mx, 1 - my), (1 - mx, 1 - my)
        src = (mx + (1 - mc) * (1 - 2 * mx), my + mc * (1 - 2 * my))
        dst = (mx + mc * (1 - 2 * mx), my + (1 - mc) * (1 - 2 * my))
        SIB, XN, YN, DG, PASS = 0, 1, 2, 3, 4

        def copy(a, k, block, to, own=False):
            px, py, pc = block
            slot = out_refs[a].at[4 * px + 2 * py + pc]
            return pltpu.make_async_remote_copy(
                src_ref=x_refs[a] if own else slot, dst_ref=slot,
                send_sem=send_sems.at[7 * a + k], recv_sem=recv_sems.at[7 * a + k],
                device_id=to, device_id_type=MESH)

        mine = [pltpu.make_async_copy(x_refs[a], out_refs[a].at[4 * mx + 2 * my + mc], local_sems.at[a])
                for a in range(n)]
        for cp in mine:
            cp.start()
        sent = [copy(a, XN, me, (*xn, mc), own=True) for a in range(n)]
        sent += [copy(a, YN, me, (*yn, mc), own=True) for a in range(n)]
        sent += [copy(a, SIB, me, sibling, own=True) for a in range(n)]
        for cp in sent:
            cp.start()
        for a in range(n):
            for k, chip in ((XN, xn), (YN, yn)):
                copy(a, k, (*chip, mc), me).wait_recv()
                sent.append(copy(a, PASS + k - 1, (*chip, mc), sibling))
                sent[-1].start()
            sent.append(copy(a, DG, (*src, mc), (*dst, mc)))
            sent[-1].start()
        for a in range(n):
            copy(a, DG, (*dg, mc), me).wait_recv()
            sent.append(copy(a, PASS + DG - 1, (*dg, mc), sibling))
            sent[-1].start()
        for a in range(n):
            copy(a, SIB, sibling, me).wait_recv()
            for k, chip in ((XN, xn), (YN, yn), (DG, dg)):
                copy(a, PASS + k - 1, (*chip, 1 - mc), me).wait_recv()
        for cp in sent:
            cp.wait_send()
        for cp in mine:
            cp.wait()

    return pl.pallas_call(
        body, name=name, in_specs=[_ANY] * n, out_specs=[_ANY] * n,
        out_shape=[jax.ShapeDtypeStruct((N_DEV,) + x.shape, x.dtype) for x in xs],
        scratch_shapes=[pltpu.SemaphoreType.DMA((7 * n,)), pltpu.SemaphoreType.DMA((7 * n,)),
                        pltpu.SemaphoreType.DMA((n,))],
    )(*xs)


N_CHIP = 4


def _pair_exchange(ps, *, name):
    n = len(ps)

    def body(*refs):
        p_refs, out_refs = refs[:n], refs[n:2 * n]
        send_sems, recv_sems = refs[2 * n:]
        mx, my, mc = lax.axis_index("x"), lax.axis_index("y"), lax.axis_index("c")
        copies = []
        for a in range(n):
            for chip in range(N_CHIP):
                copies.append(pltpu.make_async_remote_copy(
                    src_ref=p_refs[a].at[2 * chip + (1 - mc)], dst_ref=out_refs[a].at[chip],
                    send_sem=send_sems.at[N_CHIP * a + chip], recv_sem=recv_sems.at[N_CHIP * a + chip],
                    device_id=(mx, my, 1 - mc), device_id_type=MESH))
        for cp in copies:
            cp.start()
        for cp in copies:
            cp.wait()

    return pl.pallas_call(
        body, name=name, in_specs=[_ANY] * n, out_specs=[_ANY] * n,
        out_shape=[jax.ShapeDtypeStruct((N_CHIP,) + p.shape[1:], p.dtype) for p in ps],
        scratch_shapes=[pltpu.SemaphoreType.DMA((N_CHIP * n,)), pltpu.SemaphoreType.DMA((N_CHIP * n,))],
    )(*ps)


def _pair_sum(p, recv, core, *, name):
    _, R, C = p.shape
    tr = _row_tile(R, C, 4 * ROW_TILE_ELEMS)
    p4 = p.reshape(N_CHIP, 2, R, C)

    def body(core_ref, a_ref, b_ref, o_ref):
        o_ref[...] = (a_ref[...].astype(F32) + b_ref[...].astype(F32)).astype(o_ref.dtype)

    return pl.pallas_call(
        body, name=name,
        grid_spec=pltpu.PrefetchScalarGridSpec(
            num_scalar_prefetch=1, grid=(N_CHIP, R // tr),
            in_specs=[pl.BlockSpec((None, None, tr, C), lambda ch, i, core: (ch, core[0], i, 0)),
                      pl.BlockSpec((None, tr, C), lambda ch, i, core: (ch, i, 0))],
            out_specs=pl.BlockSpec((None, tr, C), lambda ch, i, core: (ch, i, 0))),
        out_shape=jax.ShapeDtypeStruct((N_CHIP, R, C), p.dtype),
        compiler_params=_params("parallel", "parallel"),
    )(core, p4, recv)


def _chip_exchange(ss, *, name):
    n = len(ss)

    def body(*refs):
        s_refs, out_refs, stage_refs = refs[:n], refs[n:2 * n], refs[2 * n:3 * n]
        send_sems, recv_sems, local_sems = refs[3 * n:]
        mx, my, mc = lax.axis_index("x"), lax.axis_index("y"), lax.axis_index("c")
        chip = 2 * mx + my
        xn, yn, dg = (1 - mx, my), (mx, 1 - my), (1 - mx, 1 - my)
        via = (mx + (1 - mc) * (1 - 2 * mx), my + mc * (1 - 2 * my))
        onward = (mx + mc * (1 - 2 * mx), my + (1 - mc) * (1 - 2 * my))
        XN, YN, STAGE, ONWARD = 0, 1, 2, 3

        def copy(a, k, src, dst, to):
            return pltpu.make_async_remote_copy(
                src_ref=src, dst_ref=dst, send_sem=send_sems.at[4 * a + k], recv_sem=recv_sems.at[4 * a + k],
                device_id=(*to, mc), device_id_type=MESH)

        def slot(ref, c):
            return ref.at[2 * c[0] + c[1]]

        mine = [pltpu.make_async_copy(s_refs[a].at[chip], out_refs[a].at[chip], local_sems.at[a]) for a in range(n)]
        for cp in mine:
            cp.start()
        first = []
        for a in range(n):
            first.append(copy(a, STAGE, slot(s_refs[a], dg), stage_refs[a], via))
            first.append(copy(a, XN, slot(s_refs[a], xn), out_refs[a].at[chip], xn))
            first.append(copy(a, YN, slot(s_refs[a], yn), out_refs[a].at[chip], yn))
        for cp in first:
            cp.start()
        onwards = []
        for a in range(n):
            first[3 * a].wait_recv()
            onwards.append(copy(a, ONWARD, stage_refs[a], slot(out_refs[a], via), onward))
            onwards[-1].start()
        for a in range(n):
            first[3 * a + 1].wait_recv()
            first[3 * a + 2].wait_recv()
            onwards[a].wait_recv()
        for cp in first + onwards:
            cp.wait_send()
        for cp in mine:
            cp.wait()

    outs = pl.pallas_call(
        body, name=name, in_specs=[_ANY] * n, out_specs=[_ANY] * (2 * n),
        out_shape=[jax.ShapeDtypeStruct(s.shape, s.dtype) for s in ss]
                  + [jax.ShapeDtypeStruct(s.shape[1:], s.dtype) for s in ss],
        scratch_shapes=[pltpu.SemaphoreType.DMA((4 * n,)), pltpu.SemaphoreType.DMA((4 * n,)),
                        pltpu.SemaphoreType.DMA((n,))],
    )(*ss)
    return outs[:n]


ROW_TILE_ELEMS = 256 * 1024


def _row_tile(R, C, elems=None):
    elems = ROW_TILE_ELEMS if elems is None else elems
    for t in (4096, 2048, 1024, 512, 256, 128, 64, 32, 16):
        if R % t == 0 and t * C <= elems:
            return t
    raise ValueError((R, C))


def _sum_slots(gs, *, name):
    S, R, C = gs.shape
    tr = _row_tile(R, C)

    def body(g_ref, o_ref):
        g = g_ref[0].astype(F32)
        for s in range(1, S):
            g = g + g_ref[s].astype(F32)
        o_ref[...] = g

    return pl.pallas_call(
        body, name=name, grid=(R // tr,),
        in_specs=[pl.BlockSpec((S, tr, C), lambda i: (0, i, 0))],
        out_specs=pl.BlockSpec((tr, C), lambda i: (i, 0)),
        out_shape=jax.ShapeDtypeStruct((R, C), F32),
        compiler_params=_params("parallel"),
    )(gs)


def _adamw(gs, w, m, v, *, name):
    S, R, C = gs.shape
    tr = _row_tile(R, C, 2 * ROW_TILE_ELEMS)

    def body(g_ref, w_ref, m_ref, v_ref, g_out, d_out, m_out, v_out):
        g = g_ref[0].astype(F32)
        for s in range(1, S):
            g = g + g_ref[s].astype(F32)
        m_new = ADAM_B1 * m_ref[...] + (1.0 - ADAM_B1) * g
        v_new = ADAM_B2 * v_ref[...] + (1.0 - ADAM_B2) * (g * g)
        m_hat = m_new / (1.0 - ADAM_B1 ** ADAM_STEP)
        v_hat = v_new / (1.0 - ADAM_B2 ** ADAM_STEP)
        g_out[...] = g
        d_out[...] = -ADAM_LR * (m_hat / (jnp.sqrt(v_hat) + ADAM_EPS) + ADAM_WD * w_ref[...])
        m_out[...] = m_new
        v_out[...] = v_new

    blk = pl.BlockSpec((tr, C), lambda i: (i, 0))
    sh = jax.ShapeDtypeStruct((R, C), F32)
    return pl.pallas_call(
        body, name=name, grid=(R // tr,),
        in_specs=[pl.BlockSpec((S, tr, C), lambda i: (0, i, 0)), blk, blk, blk],
        out_specs=[blk, blk, blk, blk], out_shape=[sh, sh, sh, sh],
        compiler_params=_params("parallel"),
    )(gs, w, m, v)


PACK_ROWS = 8


def _pack(arrs, dtype, lead, row_mult):
    lead_shape = arrs[0].shape[:lead]
    parts, meta, off = [], [], 0
    for a in arrs:
        size = math.prod(a.shape[lead:])
        rows = -(-size // LANES)
        x = a.astype(dtype)
        if size % LANES:
            x = jnp.concatenate([x.reshape(lead_shape + (size,)),
                                 jnp.zeros(lead_shape + (rows * LANES - size,), dtype)], -1)
        x = x.reshape(lead_shape + (rows, LANES))
        padded = -(-rows // PACK_ROWS) * PACK_ROWS
        if padded != rows:
            x = jnp.concatenate([x, jnp.zeros(lead_shape + (padded - rows, LANES), dtype)], lead)
        parts.append(x)
        meta.append((off, size, a.shape[lead:]))
        off += padded
    total = -(-off // row_mult) * row_mult
    if total != off:
        parts.append(jnp.zeros(lead_shape + (total - off, LANES), dtype))
    return jnp.concatenate(parts, lead), meta


def _unpack(packed, meta, lead):
    lead_shape = packed.shape[:lead]
    out = []
    for off, size, shape in meta:
        rows = -(-size // LANES)
        x = lax.slice_in_dim(packed, off, off + rows, axis=lead)
        if size % LANES:
            x = x.reshape(lead_shape + (rows * LANES,))[..., :size]
        out.append(x.reshape(lead_shape + shape))
    return out


ARG_NAMES = ['x', 'mem', 'positions', 'norm_mix_g', 'norm_xa_g', 'norm_mem_g', 'xa_wq', 'xa_wkv', 'xa_wo', 'norm_ffn_g', 'ffn_w_up', 'ffn_conv_w', 'ffn_conv_b', 'ffn_w_down', 'pc_w_in', 'pool_w', 'pool_scale', 'conv_dw_w', 'conv_dw_b', 'conv_ln_g', 'conv_ln_b', 'pc_w_out', 'mla_w_dq_dkv', 'mla_q_norm_g', 'mla_w_uq', 'mla_kv_norm_g', 'mla_w_ukv', 'mla_w_o', 'final_norm_g', 'loss_target']
WEIGHTS = ARG_NAMES[3:29]
BIG = {'xa_wq': 1, 'xa_wkv': 2, 'xa_wo': 1, 'ffn_w_up': 2, 'ffn_w_down': 1, 'pc_w_in': 2, 'pc_w_out': 1,
       'mla_w_dq_dkv': 1, 'mla_w_uq': 2, 'mla_w_ukv': 2, 'mla_w_o': 1}
SMALL_SHARDED = {'ffn_conv_w': 2, 'conv_dw_w': 2, 'mla_q_norm_g': 1, 'mla_kv_norm_g': 1}
REPLICATED = [n for n in WEIGHTS if n not in BIG and n not in SMALL_SHARDED]


def _from_slots(g, axis):
    t = jnp.moveaxis(g, 0, axis)
    return t.reshape(t.shape[:axis] + (t.shape[axis] * t.shape[axis + 1],) + t.shape[axis + 2:])


def _to_slots(full, axis):
    n = full.shape[axis] // N_DEV
    t = full.reshape(full.shape[:axis] + (N_DEV, n) + full.shape[axis + 1:])
    return jnp.moveaxis(t, axis, 0)


def kernel(x, mem, positions, norm_mix_g, norm_xa_g, norm_mem_g, xa_wq, xa_wkv, xa_wo, norm_ffn_g, ffn_w_up, ffn_conv_w, ffn_conv_b, ffn_w_down, pc_w_in, pool_w, pool_scale, conv_dw_w, conv_dw_b, conv_ln_g, conv_ln_b, pc_w_out, mla_w_dq_dkv, mla_q_norm_g, mla_w_uq, mla_kv_norm_g, mla_w_ukv, mla_w_o, final_norm_g, loss_target, m_norm_mix_g, m_norm_xa_g, m_norm_mem_g, m_xa_wq, m_xa_wkv, m_xa_wo, m_norm_ffn_g, m_ffn_w_up, m_ffn_conv_w, m_ffn_conv_b, m_ffn_w_down, m_pc_w_in, m_pool_w, m_pool_scale, m_conv_dw_w, m_conv_dw_b, m_conv_ln_g, m_conv_ln_b, m_pc_w_out, m_mla_w_dq_dkv, m_mla_q_norm_g, m_mla_w_uq, m_mla_kv_norm_g, m_mla_w_ukv, m_mla_w_o, m_final_norm_g, v_norm_mix_g, v_norm_xa_g, v_norm_mem_g, v_xa_wq, v_xa_wkv, v_xa_wo, v_norm_ffn_g, v_ffn_w_up, v_ffn_conv_w, v_ffn_conv_b, v_ffn_w_down, v_pc_w_in, v_pool_w, v_pool_scale, v_conv_dw_w, v_conv_dw_b, v_conv_ln_g, v_conv_ln_b, v_pc_w_out, v_mla_w_dq_dkv, v_mla_q_norm_g, v_mla_w_uq, v_mla_kv_norm_g, v_mla_w_ukv, v_mla_w_o, v_final_norm_g):
    args = (x, mem, positions, norm_mix_g, norm_xa_g, norm_mem_g, xa_wq, xa_wkv, xa_wo, norm_ffn_g, ffn_w_up, ffn_conv_w, ffn_conv_b, ffn_w_down, pc_w_in, pool_w, pool_scale, conv_dw_w, conv_dw_b, conv_ln_g, conv_ln_b, pc_w_out, mla_w_dq_dkv, mla_q_norm_g, mla_w_uq, mla_kv_norm_g, mla_w_ukv, mla_w_o, final_norm_g, loss_target)
    a = dict(zip(ARG_NAMES, args))
    mom = dict(zip(WEIGHTS, (m_norm_mix_g, m_norm_xa_g, m_norm_mem_g, m_xa_wq, m_xa_wkv, m_xa_wo, m_norm_ffn_g, m_ffn_w_up, m_ffn_conv_w, m_ffn_conv_b, m_ffn_w_down, m_pc_w_in, m_pool_w, m_pool_scale, m_conv_dw_w, m_conv_dw_b, m_conv_ln_g, m_conv_ln_b, m_pc_w_out, m_mla_w_dq_dkv, m_mla_q_norm_g, m_mla_w_uq, m_mla_kv_norm_g, m_mla_w_ukv, m_mla_w_o, m_final_norm_g)))
    var = dict(zip(WEIGHTS, (v_norm_mix_g, v_norm_xa_g, v_norm_mem_g, v_xa_wq, v_xa_wkv, v_xa_wo, v_norm_ffn_g, v_ffn_w_up, v_ffn_conv_w, v_ffn_conv_b, v_ffn_w_down, v_pc_w_in, v_pool_w, v_pool_scale, v_conv_dw_w, v_conv_dw_b, v_conv_ln_g, v_conv_ln_b, v_pc_w_out, v_mla_w_dq_dkv, v_mla_q_norm_g, v_mla_w_uq, v_mla_kv_norm_g, v_mla_w_ukv, v_mla_w_o, v_final_norm_g)))
    me = 4 * lax.axis_index("x") + 2 * lax.axis_index("y") + lax.axis_index("c")

    big_all = _all_gather([a[n].astype(CD) for n in BIG], name="gather_weights")
    sm_pack, sm_meta = _pack([a[n] for n in SMALL_SHARDED], F32, 0, 8)
    sm_all = _unpack(_all_gather([sm_pack], name="gather_small")[0], sm_meta, 1)
    W = {n: a[n] for n in REPLICATED}
    for (n, ax), g in zip(BIG.items(), big_all):
        W[n] = _from_slots(g, ax)
    for (n, ax), g in zip(SMALL_SHARDED.items(), sm_all):
        W[n] = _from_slots(g, ax)

    loss, dx, G = _local_step(x[0], mem[0], positions[0], loss_target[0], W)

    parts = [_to_slots(G[n], ax).astype(CD) for n, ax in BIG.items()]
    from_sibling = _pair_exchange(parts, name="grads_to_sibling")
    core = lax.axis_index("c").astype(jnp.int32).reshape(1)
    sums = []
    for n, p, r in zip(BIG, parts, from_sibling):
        cols = p.shape[-1]
        s = _pair_sum(p.reshape(N_DEV, -1, cols), r.reshape(N_CHIP, -1, cols), core, name=f"pair_sum_{n}")
        sums.append(s.reshape((N_CHIP,) + p.shape[1:]))
    recv = _chip_exchange(sums, name="scatter_grads")
    out = {}
    for n, r in zip(BIG, recv):
        shape = a[n].shape
        rows = lambda t: t.reshape(-1, shape[-1])
        res = _adamw(r.reshape(N_CHIP, -1, shape[-1]), rows(a[n]), rows(mom[n]), rows(var[n]), name=f"adamw_{n}")
        out[n] = tuple(t.reshape(shape) for t in res)

    small_names = REPLICATED + list(SMALL_SHARDED)
    spack, smeta = _pack([G[n] for n in small_names] + [loss], F32, 0, 256)
    stot = _unpack(_sum_slots(_all_gather([spack], name="gather_small_grads")[0], name="sum_small_grads"), smeta, 0)
    loss_total = stot[-1][0, 0]
    gsm = dict(zip(small_names, stot[:-1]))
    for n, ax in SMALL_SHARDED.items():
        width = a[n].shape[ax]
        gsm[n] = lax.dynamic_slice_in_dim(gsm[n], me * width, width, ax)
    g1, meta1 = _pack([gsm[n] for n in small_names], F32, 0, 256)
    w1, _ = _pack([a[n] for n in small_names], F32, 0, 256)
    m1, _ = _pack([mom[n] for n in small_names], F32, 0, 256)
    v1, _ = _pack([var[n] for n in small_names], F32, 0, 256)
    res = [_unpack(r, meta1, 0) for r in _adamw(g1[None], w1, m1, v1, name="adamw_small")]
    for i, n in enumerate(small_names):
        out[n] = tuple(r[i] for r in res)

    return (loss_total, dx[None],
            *[out[n][0] for n in WEIGHTS], *[out[n][1] for n in WEIGHTS],
            *[out[n][2] for n in WEIGHTS], *[out[n][3] for n in WEIGHTS])
```

```python
import functools
import math

import jax
import jax.numpy as jnp
from jax import lax
from jax.experimental import pallas as pl
from jax.experimental.pallas import tpu as pltpu

F32 = jnp.float32
CD = jnp.bfloat16
EPS = 1e-6
NEG = -1e30
N_DEV = 8
LANES = 128
HALO = 32

D_MODEL = 1024
DEPTH = 4
XA_HEADS = 4
XA_DH = 256
MEM_LEN = 256
POOL_WINDOWS = (2, 4, 8, 16)
CONV_K = 31
FFN_K = 3
D_FF = 2816
MLA_HEADS = 16
QK_NOPE = 64
QK_ROPE = 32
V_HEAD = 64
Q_LORA = 384
KV_LORA = 256
ROPE_THETA = 10000.0
MLA_SCALE = 1.0 / math.sqrt(QK_NOPE + QK_ROPE)
XA_SCALE = XA_DH ** -0.5

ADAM_LR = 0.001
ADAM_B1 = 0.9
ADAM_B2 = 0.999
ADAM_EPS = 1e-08
ADAM_WD = 0.01
ADAM_STEP = 10

NT = (((1,), (1,)), ((), ()))
TN = (((0,), (0,)), ((), ()))
MESH = pl.DeviceIdType.MESH


def _tile(n, target):
    if n <= target:
        return n
    best = None
    for t in range(LANES, target + 1, LANES):
        if n % t == 0:
            best = t
    assert best is not None, (n, target)
    return best


def _params(*sem):
    return pltpu.CompilerParams(dimension_semantics=sem)


def _sigmoid(v):
    return 0.5 * jnp.tanh(0.5 * v) + 0.5


def _rms_bwd(x, gain, dh):
    r = lax.rsqrt(jnp.mean(x * x, axis=-1, keepdims=True) + EPS)
    xhat = x * r
    dxhat = dh * gain
    dx = r * (dxhat - xhat * jnp.mean(dxhat * xhat, axis=-1, keepdims=True))
    return dx, dh * xhat


def _weight(w):
    if not isinstance(w, tuple):
        return w, w.shape, pl.BlockSpec
    arr, layer = w

    def spec(block, imap):
        return pl.BlockSpec((None,) + tuple(block), lambda *a: (layer,) + tuple(imap(*a)))

    return arr, arr.shape[1:], spec


def _nmm(x, g, w, *, name, out_dtype, tm=1024, tn_target=1024):
    M, K = x.shape
    w, (_, N), wspec = _weight(w)
    tm = min(tm, M)
    tn = _tile(N, tn_target)

    def body(x_ref, g_ref, w_ref, z_ref, h_ref):
        @pl.when(pl.program_id(1) == 0)
        def _():
            xf = x_ref[...]
            r = lax.rsqrt(jnp.mean(xf * xf, axis=-1, keepdims=True) + EPS)
            h_ref[...] = (xf * r * g_ref[...]).astype(h_ref.dtype)

        z_ref[...] = jnp.dot(h_ref[...], w_ref[...], preferred_element_type=F32).astype(z_ref.dtype)

    return pl.pallas_call(
        body, name=name, grid=(M // tm, N // tn),
        in_specs=[pl.BlockSpec((tm, K), lambda i, j: (i, 0)),
                  pl.BlockSpec((1, K), lambda i, j: (0, 0)),
                  wspec((K, tn), lambda i, j: (0, j))],
        out_specs=[pl.BlockSpec((tm, tn), lambda i, j: (i, j)),
                   pl.BlockSpec((tm, K), lambda i, j: (i, 0))],
        out_shape=[jax.ShapeDtypeStruct((M, N), out_dtype), jax.ShapeDtypeStruct((M, K), CD)],
        compiler_params=_params("parallel", "arbitrary"),
    )(x, g, w)


def _mm_res(a, w, res, *, name, tm=1024, tn_target=1024):
    M, K = a.shape
    w, (_, N), wspec = _weight(w)
    tm = min(tm, M)
    tn = _tile(N, tn_target)

    def body(a_ref, w_ref, r_ref, o_ref):
        o_ref[...] = r_ref[...] + jnp.dot(a_ref[...].astype(CD), w_ref[...], preferred_element_type=F32)

    return pl.pallas_call(
        body, name=name, grid=(M // tm, N // tn),
        in_specs=[pl.BlockSpec((tm, K), lambda i, j: (i, 0)),
                  wspec((K, tn), lambda i, j: (0, j)),
                  pl.BlockSpec((tm, tn), lambda i, j: (i, j))],
        out_specs=pl.BlockSpec((tm, tn), lambda i, j: (i, j)),
        out_shape=jax.ShapeDtypeStruct((M, N), F32),
        compiler_params=_params("parallel", "arbitrary"),
    )(a, w, res)


def _mm_nt(a, w, *, name, out_dtype, tm=1024, tn_target=1024):
    M, K = a.shape
    w, (N, _), wspec = _weight(w)
    tm = min(tm, M)
    tn = _tile(N, tn_target)

    def body(a_ref, w_ref, o_ref):
        o_ref[...] = lax.dot_general(a_ref[...].astype(CD), w_ref[...], NT,
                                     preferred_element_type=F32).astype(o_ref.dtype)

    return pl.pallas_call(
        body, name=name, grid=(M // tm, N // tn),
        in_specs=[pl.BlockSpec((tm, K), lambda i, j: (i, 0)),
                  wspec((tn, K), lambda i, j: (j, 0))],
        out_specs=pl.BlockSpec((tm, tn), lambda i, j: (i, j)),
        out_shape=jax.ShapeDtypeStruct((M, N), out_dtype),
        compiler_params=_params("parallel", "arbitrary"),
    )(a, w)


def _mm_nt_normbwd(gy, w, x, gain, dres, *, name, tm=1024, tk_target=1408):
    M, K = gy.shape
    w, (D, _), wspec = _weight(w)
    tm = min(tm, M)
    tk = _tile(K, tk_target)
    nk = K // tk

    def body(g_ref, w_ref, x_ref, gain_ref, dres_ref, dx_ref, dg_ref, acc):
        i, k = pl.program_id(0), pl.program_id(1)

        @pl.when(k == 0)
        def _():
            acc[...] = jnp.zeros_like(acc)

        acc[...] += lax.dot_general(g_ref[...].astype(CD), w_ref[...], NT, preferred_element_type=F32)

        @pl.when(k == nk - 1)
        def _():
            dx, dg_rows = _rms_bwd(x_ref[...], gain_ref[...], acc[...])
            dx_ref[...] = dres_ref[...] + dx

            @pl.when(i == 0)
            def _():
                dg_ref[...] = jnp.zeros_like(dg_ref)

            dg_ref[...] += jnp.sum(dg_rows, axis=0, keepdims=True)

    return pl.pallas_call(
        body, name=name, grid=(M // tm, nk),
        in_specs=[pl.BlockSpec((tm, tk), lambda i, k: (i, k)),
                  wspec((D, tk), lambda i, k: (0, k)),
                  pl.BlockSpec((tm, D), lambda i, k: (i, 0)),
                  pl.BlockSpec((1, D), lambda i, k: (0, 0)),
                  pl.BlockSpec((tm, D), lambda i, k: (i, 0))],
        out_specs=[pl.BlockSpec((tm, D), lambda i, k: (i, 0)),
                   pl.BlockSpec((1, D), lambda i, k: (0, 0))],
        out_shape=[jax.ShapeDtypeStruct((M, D), F32), jax.ShapeDtypeStruct((1, D), F32)],
        scratch_shapes=[pltpu.VMEM((tm, D), F32)],
        compiler_params=_params("arbitrary", "arbitrary"),
    )(gy, w, x, gain, dres)


def _mm_tn(a, g, *, name, tt=2048, tk_target=1024, tn_target=1024, into=None):
    T, K = a.shape
    N = g.shape[1]
    tt = min(tt, T)
    tk = _tile(K, tk_target)
    tn = _tile(N, tn_target)

    def body(a_ref, g_ref, *rest):
        o_ref = rest[-1]

        @pl.when(pl.program_id(2) == 0)
        def _():
            o_ref[...] = jnp.zeros_like(o_ref)

        o_ref[...] += lax.dot_general(a_ref[...].astype(CD), g_ref[...].astype(CD), TN,
                                      preferred_element_type=F32)

    in_specs = [pl.BlockSpec((tt, tk), lambda i, j, t: (t, i)),
                pl.BlockSpec((tt, tn), lambda i, j, t: (t, j))]
    if into is None:
        operands, aliases = (a, g), {}
        out_spec = pl.BlockSpec((tk, tn), lambda i, j, t: (i, j))
        out_shape = jax.ShapeDtypeStruct((K, N), F32)
    else:
        stack, layer = into
        operands, aliases = (a, g, stack), {2: 0}
        in_specs.append(pl.BlockSpec(memory_space=pl.ANY))
        out_spec = pl.BlockSpec((None, tk, tn), lambda i, j, t: (layer, i, j))
        out_shape = jax.ShapeDtypeStruct(stack.shape, F32)
    return pl.pallas_call(
        body, name=name, grid=(K // tk, N // tn, T // tt),
        in_specs=in_specs, out_specs=out_spec, out_shape=out_shape, input_output_aliases=aliases,
        compiler_params=_params("parallel", "parallel", "arbitrary"),
    )(*operands)


POOL_W = 512
CONV_W = 512
POOL_GROUP = 128


MIX_ROWS = 64
LN_ROWS = 256
LN_BWD_ROWS = 512
SUB = 8


def _shifted(sh_sc, x, n_rows):
    for b in range(1, SUB):
        sh_sc[b, pl.ds(0, n_rows), :] = x[b:b + n_rows]


def _tap(sh_sc, src, r0, cols, start, rows):
    a, b = divmod(start, SUB)
    if b == 0:
        return src[pl.ds(r0 + SUB * a, rows), cols]
    return sh_sc[b, pl.ds(SUB * a, rows), :]


def _pool_rows(zp_ref, z_ref, cols, win, i, tt, first, pooled_sc):
    RB = min(MIX_ROWS, tt)
    hb = 2 * SUB
    for r in range(tt // RB):
        if r == 0:
            p = zp_ref[pl.ds(HALO - hb, hb), cols]
            v = jnp.concatenate([jnp.where(first, jnp.zeros_like(p), p), z_ref[pl.ds(0, RB), cols]], axis=0)
        else:
            v = z_ref[pl.ds(r * RB - hb, RB + hb), cols]
        u = v[hb:hb + RB]
        s = u
        for j in range(1, win):
            s = s + v[hb - j:hb - j + RB]
        t_glob = i * tt + r * RB + lax.broadcasted_iota(jnp.int32, (RB, 1), 0)
        cnt = jnp.minimum(t_glob + 1, win).astype(F32)
        pooled_sc[pl.ds(r * RB, RB), :] = (s / cnt - u).astype(pooled_sc.dtype)


def _fill_gl(gl_sc, zp_ref, z_ref, zn_ref, tt, first, last):
    ca, cb = pl.ds(POOL_W, CONV_W), pl.ds(POOL_W + CONV_W, CONV_W)
    g = zp_ref[:, ca] * _sigmoid(zp_ref[:, cb])
    gl_sc[pl.ds(0, HALO), :] = jnp.where(first, jnp.zeros_like(g), g)

    def rows(r, carry):
        r0 = pl.multiple_of(r * LN_ROWS, LN_ROWS)
        gl_sc[pl.ds(HALO + r0, LN_ROWS), :] = z_ref[pl.ds(r0, LN_ROWS), ca] * _sigmoid(z_ref[pl.ds(r0, LN_ROWS), cb])
        return carry

    lax.fori_loop(0, tt // LN_ROWS, rows, 0)
    if zn_ref is not None:
        g = zn_ref[:, ca] * _sigmoid(zn_ref[:, cb])
        gl_sc[pl.ds(HALO + tt, HALO), :] = jnp.where(last, jnp.zeros_like(g), g)


def _conv_rows(gl_sc, cv_sc, sh_sc, w_ref, b_ref, n_rows):
    RB = min(MIX_ROWS, n_rows)
    for c in range(CONV_W // LANES):
        cols = pl.ds(c * LANES, LANES)
        bias = b_ref[:, cols]

        def chunk(r0, rb):
            g = gl_sc[pl.ds(r0, rb + HALO), cols]
            _shifted(sh_sc, g, rb + HALO - SUB)
            cv = jnp.zeros((rb, LANES), F32) + bias
            for j in range(CONV_K):
                cv = cv + w_ref[pl.ds(j, 1), cols] * _tap(sh_sc, gl_sc, r0, cols, HALO - (CONV_K - 1) + j, rb)
            cv_sc[pl.ds(r0, rb), cols] = cv

        def body(r, carry):
            chunk(pl.multiple_of(r * RB, RB), RB)
            return carry

        lax.fori_loop(0, n_rows // RB, body, 0)
        if n_rows % RB:
            chunk((n_rows // RB) * RB, n_rows % RB)


def _mixer_fwd(z, pool_w, pool_scale, dw_w, dw_b, ln_g, ln_b, *, name, tt=512):
    T, C = z.shape
    tt = min(tt, T)
    n = T // tt
    hb = tt // HALO

    def body(zp_ref, z_ref, pw_ref, ps_ref, w_ref, b_ref, g_ref, bb_ref, o_ref, pooled_sc, gl_sc, cv_sc, sh_sc):
        i = pl.program_id(0)
        first = i == 0
        for gi, win in enumerate(POOL_WINDOWS):
            cols = pl.ds(gi * POOL_GROUP, POOL_GROUP)
            _pool_rows(zp_ref, z_ref, cols, win, i, tt, first, pooled_sc)
            ya = jnp.dot(pooled_sc[...], pw_ref[gi].astype(CD), preferred_element_type=F32)
            o_ref[:, cols] = (ya * ps_ref[:, cols]).astype(o_ref.dtype)
        _fill_gl(gl_sc, zp_ref, z_ref, None, tt, first, None)
        _conv_rows(gl_sc, cv_sc, sh_sc, w_ref, b_ref, tt)

        def ln_rows(r, carry):
            rows = pl.ds(pl.multiple_of(r * LN_ROWS, LN_ROWS), LN_ROWS)
            cv = cv_sc[rows, :]
            xc = cv - jnp.mean(cv, axis=-1, keepdims=True)
            yn = xc * lax.rsqrt(jnp.mean(xc * xc, axis=-1, keepdims=True) + EPS) * g_ref[...] + bb_ref[...]
            o_ref[rows, pl.ds(POOL_W, CONV_W)] = (yn * _sigmoid(yn)).astype(o_ref.dtype)
            return carry

        lax.fori_loop(0, tt // LN_ROWS, ln_rows, 0, unroll=4)

    full = lambda shape: pl.BlockSpec(shape, lambda i: (0,) * len(shape))
    return pl.pallas_call(
        body, name=name, grid=(n,),
        in_specs=[pl.BlockSpec((HALO, C), lambda i: (jnp.maximum(i * hb - 1, 0), 0)),
                  pl.BlockSpec((tt, C), lambda i: (i, 0)),
                  full((4, POOL_GROUP, POOL_GROUP)), full((1, POOL_W)), full((CONV_K + 1, CONV_W)),
                  full((1, CONV_W)), full((1, CONV_W)), full((1, CONV_W))],
        out_specs=pl.BlockSpec((tt, POOL_W + CONV_W), lambda i: (i, 0)),
        out_shape=jax.ShapeDtypeStruct((T, POOL_W + CONV_W), CD),
        scratch_shapes=[pltpu.VMEM((tt, POOL_GROUP), CD), pltpu.VMEM((tt + HALO, CONV_W), F32),
                        pltpu.VMEM((tt, CONV_W), F32), pltpu.VMEM((SUB, MIX_ROWS + HALO, LANES), F32)],
        compiler_params=_params("parallel"),
    )(z, z, pool_w, pool_scale, dw_w, dw_b, ln_g, ln_b)


def _mixer_bwd(z, dy, pool_w, pool_scale, dw_w, dw_b, ln_g, ln_b, *, name, tt=512):
    T, C = z.shape
    tt = min(tt, T)
    n = T // tt
    hb = tt // HALO
    R = tt + HALO
    RB = min(MIX_ROWS, tt)

    def body(zp_ref, z_ref, zn_ref, dy_ref, dyn_ref, pw_ref, ps_ref, w_ref, b_ref, g_ref, bb_ref,
             dz_ref, dpw_ref, dps_ref, dw_ref, db_ref, dg_ref, dbb_ref,
             pooled_sc, dm_sc, dpool_sc, dpe_sc, gl_sc, cv_sc, accw, accl, sh_sc, shd_sc):
        i = pl.program_id(0)
        first, last = i == 0, i == n - 1

        @pl.when(first)
        def _():
            for r in (dpw_ref, dps_ref, dw_ref, db_ref, dg_ref, dbb_ref):
                r[...] = jnp.zeros_like(r)

        def dy_rows(cols):
            nxt = dyn_ref[:, cols]
            return jnp.concatenate([dy_ref[:, cols], jnp.where(last, jnp.zeros_like(nxt), nxt)], axis=0)

        t_all = i * tt + lax.broadcasted_iota(jnp.int32, (R, 1), 0)
        for gi, win in enumerate(POOL_WINDOWS):
            cols = pl.ds(gi * POOL_GROUP, POOL_GROUP)
            _pool_rows(zp_ref, z_ref, cols, win, i, tt, first, pooled_sc)
            pw = pw_ref[gi].astype(CD)
            dya = dy_rows(cols)
            mm = jnp.dot(pooled_sc[...], pw, preferred_element_type=F32)
            dps_ref[:, cols] += jnp.sum(dya[:tt] * mm, axis=0, keepdims=True)
            dm_sc[...] = (dya * ps_ref[:, cols]).astype(CD)
            dpw_ref[gi] += lax.dot_general(pooled_sc[...], dm_sc[pl.ds(0, tt), :], TN, preferred_element_type=F32)
            dpool = lax.dot_general(dm_sc[...], pw, NT, preferred_element_type=F32)
            dpool_sc[...] = dpool
            dpe_sc[...] = dpool / jnp.minimum(t_all + 1, win).astype(F32)

            def du_rows(r, carry):
                r0 = pl.multiple_of(r * RB, RB)
                e = dpe_sc[pl.ds(r0, RB + 2 * SUB), :]
                du = -dpool_sc[pl.ds(r0, RB), :]
                for j in range(win):
                    du = du + e[j:j + RB]
                dz_ref[pl.ds(r0, RB), cols] = du.astype(dz_ref.dtype)
                return carry

            lax.fori_loop(0, tt // RB, du_rows, 0)

        _fill_gl(gl_sc, zp_ref, z_ref, zn_ref, tt, first, last)
        _conv_rows(gl_sc, cv_sc, sh_sc, w_ref, b_ref, R)
        accl[...] = jnp.zeros_like(accl)

        def ln_rows(r0, nr, in_tile):
            rows = pl.ds(r0, nr)
            cv = cv_sc[rows, :]
            xc = cv - jnp.mean(cv, axis=-1, keepdims=True)
            rstd = lax.rsqrt(jnp.mean(xc * xc, axis=-1, keepdims=True) + EPS)
            xhat = xc * rstd
            yn = xhat * g_ref[...] + bb_ref[...]
            sy = _sigmoid(yn)
            if in_tile:
                dyv = dy_ref[rows, pl.ds(POOL_W, CONV_W)]
            else:
                nxt = dyn_ref[:, pl.ds(POOL_W, CONV_W)]
                dyv = jnp.where(last, jnp.zeros_like(nxt), nxt)
            dyn = dyv * (sy * (1.0 + yn * (1.0 - sy)))
            if in_tile:
                accl[pl.ds(0, SUB), :] += jnp.sum((dyn * xhat).reshape(nr // SUB, SUB, CONV_W), axis=0)
                accl[pl.ds(SUB, SUB), :] += jnp.sum(dyn.reshape(nr // SUB, SUB, CONV_W), axis=0)
            dxh = dyn * g_ref[...]
            dcv = rstd * (dxh - jnp.mean(dxh, axis=-1, keepdims=True)
                          - xhat * jnp.mean(dxh * xhat, axis=-1, keepdims=True))
            cv_sc[rows, :] = dcv
            if in_tile:
                accl[pl.ds(2 * SUB, SUB), :] += jnp.sum(dcv.reshape(nr // SUB, SUB, CONV_W), axis=0)

        lnb = min(LN_BWD_ROWS, tt)

        def ln_body(r, carry):
            ln_rows(pl.multiple_of(r * lnb, lnb), lnb, True)
            return carry

        lax.fori_loop(0, tt // lnb, ln_body, 0)
        ln_rows(tt, HALO, False)
        dg_ref[...] += jnp.sum(accl[pl.ds(0, SUB), :], axis=0, keepdims=True)
        dbb_ref[...] += jnp.sum(accl[pl.ds(SUB, SUB), :], axis=0, keepdims=True)
        db_ref[...] += jnp.sum(accl[pl.ds(2 * SUB, SUB), :], axis=0, keepdims=True)

        accw[...] = jnp.zeros_like(accw)
        for c in range(CONV_W // LANES):
            cols = pl.ds(c * LANES, LANES)

            def chunk(r, carry):
                r0 = pl.multiple_of(r * RB, RB)
                d = cv_sc[pl.ds(r0, RB + HALO), cols]
                g = gl_sc[pl.ds(r0, RB + HALO), cols]
                _shifted(shd_sc, d, RB + HALO - SUB)
                _shifted(sh_sc, g, RB + HALO - SUB)
                d_t = d[:RB]
                dgl = jnp.zeros((RB, LANES), F32)
                for j in range(CONV_K):
                    dgl = dgl + w_ref[pl.ds(j, 1), cols] * _tap(shd_sc, cv_sc, r0, cols, CONV_K - 1 - j, RB)
                    prod = d_t * _tap(sh_sc, gl_sc, r0, cols, HALO - (CONV_K - 1) + j, RB)
                    accw[pl.ds(SUB * j, SUB), cols] += jnp.sum(prod.reshape(RB // SUB, SUB, LANES), axis=0)
                a_t = z_ref[pl.ds(r0, RB), pl.ds(POOL_W + c * LANES, LANES)]
                sb = _sigmoid(z_ref[pl.ds(r0, RB), pl.ds(POOL_W + CONV_W + c * LANES, LANES)])
                dz_ref[pl.ds(r0, RB), pl.ds(POOL_W + c * LANES, LANES)] = (dgl * sb).astype(dz_ref.dtype)
                dz_ref[pl.ds(r0, RB), pl.ds(POOL_W + CONV_W + c * LANES, LANES)] = (
                    dgl * a_t * sb * (1.0 - sb)).astype(dz_ref.dtype)
                return carry

            lax.fori_loop(0, tt // RB, chunk, 0)
        for j in range(CONV_K):
            dw_ref[pl.ds(j, 1), :] += jnp.sum(accw[pl.ds(SUB * j, SUB), :], axis=0, keepdims=True)

    full = lambda shape: pl.BlockSpec(shape, lambda i: (0,) * len(shape))
    nb = T // HALO
    outs = pl.pallas_call(
        body, name=name, grid=(n,),
        in_specs=[pl.BlockSpec((HALO, C), lambda i: (jnp.maximum(i * hb - 1, 0), 0)),
                  pl.BlockSpec((tt, C), lambda i: (i, 0)),
                  pl.BlockSpec((HALO, C), lambda i: (jnp.minimum((i + 1) * hb, nb - 1), 0)),
                  pl.BlockSpec((tt, 2 * POOL_W), lambda i: (i, 0)),
                  pl.BlockSpec((HALO, 2 * POOL_W), lambda i: (jnp.minimum((i + 1) * hb, nb - 1), 0)),
                  full((4, POOL_GROUP, POOL_GROUP)), full((1, POOL_W)), full((CONV_K + 1, CONV_W)),
                  full((1, CONV_W)), full((1, CONV_W)), full((1, CONV_W))],
        out_specs=[pl.BlockSpec((tt, C), lambda i: (i, 0)),
                   full((4, POOL_GROUP, POOL_GROUP)), full((1, POOL_W)), full((CONV_K + 1, CONV_W)),
                   full((1, CONV_W)), full((1, CONV_W)), full((1, CONV_W))],
        out_shape=[jax.ShapeDtypeStruct((T, C), CD),
                   jax.ShapeDtypeStruct((4, POOL_GROUP, POOL_GROUP), F32),
                   jax.ShapeDtypeStruct((1, POOL_W), F32),
                   jax.ShapeDtypeStruct((CONV_K + 1, CONV_W), F32),
                   jax.ShapeDtypeStruct((1, CONV_W), F32),
                   jax.ShapeDtypeStruct((1, CONV_W), F32),
                   jax.ShapeDtypeStruct((1, CONV_W), F32)],
        scratch_shapes=[pltpu.VMEM((tt, POOL_GROUP), CD), pltpu.VMEM((R, POOL_GROUP), CD),
                        pltpu.VMEM((R, POOL_GROUP), F32), pltpu.VMEM((R, POOL_GROUP), F32),
                        pltpu.VMEM((tt + 2 * HALO, CONV_W), F32), pltpu.VMEM((R, CONV_W), F32),
                        pltpu.VMEM((SUB * (CONV_K + 1), CONV_W), F32), pltpu.VMEM((3 * SUB, CONV_W), F32),
                        pltpu.VMEM((SUB, MIX_ROWS + HALO, LANES), F32),
                        pltpu.VMEM((SUB, MIX_ROWS + HALO, LANES), F32)],
        compiler_params=_params("arbitrary"),
    )(z, z, z, dy, dy, pool_w, pool_scale, dw_w, dw_b, ln_g, ln_b)
    return outs


CHUNK_HALO = 16
FFN_ROWS = 64
FFN_LANES = 128


def _rows(cur, prev, nxt, r, rb, before, after, cols, n_r, first, last):
    lo, hi = r * rb - before, r * rb + rb + after
    tt = n_r * rb
    parts = []
    if lo < 0:
        p = prev[pl.ds(HALO + lo, -lo), cols]
        parts.append(jnp.where(first, jnp.zeros_like(p), p))
        lo = 0
    parts.append(cur[pl.ds(lo, min(hi, tt) - lo), cols])
    if hi > tt:
        p = nxt[pl.ds(0, hi - tt), cols]
        parts.append(jnp.where(last, jnp.zeros_like(p), p))
    return parts[0] if len(parts) == 1 else jnp.concatenate(parts, axis=0)


def _ffn_mid_fwd(up, cw, cb, *, name, tt=512):
    T = up.shape[0]
    tt = min(tt, T)
    n = T // tt
    hb = tt // HALO
    RB, CW, HB = min(FFN_ROWS, tt), FFN_LANES, CHUNK_HALO
    n_r = tt // RB

    def body(a_ref, gp_ref, g_ref, w_ref, b_ref, o_ref):
        first = pl.program_id(0) == 0

        def col_chunk(c, carry):
            cols = pl.ds(pl.multiple_of(c * CW, CW), CW)
            w = w_ref[:, cols]
            b = b_ref[:, cols]
            for r in range(n_r):
                v = _rows(g_ref, gp_ref, None, r, RB, HB, 0, cols, n_r, first, None).astype(F32)
                gc = b + w[0:1] * v[HB - 2:HB - 2 + RB] + w[1:2] * v[HB - 1:HB - 1 + RB] + w[2:3] * v[HB:HB + RB]
                a = a_ref[pl.ds(r * RB, RB), cols].astype(F32)
                o_ref[pl.ds(r * RB, RB), cols] = (gc * _sigmoid(gc) * a).astype(o_ref.dtype)
            return carry

        lax.fori_loop(0, D_FF // CW, col_chunk, 0)

    return pl.pallas_call(
        body, name=name, grid=(n,),
        in_specs=[pl.BlockSpec((tt, D_FF), lambda i: (i, 0)),
                  pl.BlockSpec((HALO, D_FF), lambda i: (jnp.maximum(i * hb - 1, 0), 1)),
                  pl.BlockSpec((tt, D_FF), lambda i: (i, 1)),
                  pl.BlockSpec((8, D_FF), lambda i: (0, 0)),
                  pl.BlockSpec((1, D_FF), lambda i: (0, 0))],
        out_specs=pl.BlockSpec((tt, D_FF), lambda i: (i, 0)),
        out_shape=jax.ShapeDtypeStruct((T, D_FF), CD),
        compiler_params=_params("parallel"),
    )(up, up, up, cw, cb)


def _ffn_mid_bwd(up, dact, cw, cb, *, name, tt=512):
    T = up.shape[0]
    tt = min(tt, T)
    n = T // tt
    hb = tt // HALO
    nb = T // HALO
    RB, CW, HB = min(FFN_ROWS, tt), FFN_LANES, CHUNK_HALO
    n_r = tt // RB
    RE = RB + 8

    def body(a_ref, an_ref, gp_ref, g_ref, gn_ref, d_ref, dn_ref, w_ref, b_ref, dup_ref, dw_ref, db_ref, acc):
        i = pl.program_id(0)
        first, last = i == 0, i == n - 1

        @pl.when(first)
        def _():
            dw_ref[...] = jnp.zeros_like(dw_ref)
            db_ref[...] = jnp.zeros_like(db_ref)

        def col_chunk(c, carry):
            cols = pl.ds(pl.multiple_of(c * CW, CW), CW)
            w = w_ref[:, cols]
            b = b_ref[:, cols]
            part = [jnp.zeros((8, CW), F32) for _ in range(FFN_K + 1)]
            for r in range(n_r):
                v = _rows(g_ref, gp_ref, gn_ref, r, RB, HB, HB, cols, n_r, first, last).astype(F32)
                gs = [v[HB - 2 + j:HB - 2 + j + RE] for j in range(FFN_K)]
                gc = b + w[0:1] * gs[0] + w[1:2] * gs[1] + w[2:3] * gs[2]
                sg = _sigmoid(gc)
                d = _rows(d_ref, None, dn_ref, r, RB, 0, HB, cols, n_r, None, last).astype(F32)[:RE]
                a = _rows(a_ref, None, an_ref, r, RB, 0, HB, cols, n_r, None, last).astype(F32)[:RE]
                silu = gc * sg
                dgc = d * a * (sg + silu - silu * sg)
                dup_ref[pl.ds(r * RB, RB), cols] = (d[:RB] * silu[:RB]).astype(dup_ref.dtype)
                dg = w[2:3] * dgc[0:RB] + w[1:2] * dgc[1:RB + 1] + w[0:1] * dgc[2:RB + 2]
                dup_ref[pl.ds(r * RB, RB), pl.ds(pl.multiple_of(D_FF + c * CW, CW), CW)] = dg.astype(dup_ref.dtype)
                dgc_t = dgc[:RB]
                for j in range(FFN_K):
                    part[j] = part[j] + jnp.sum((dgc_t * gs[j][:RB]).reshape(RB // 8, 8, CW), axis=0)
                part[FFN_K] = part[FFN_K] + jnp.sum(dgc_t.reshape(RB // 8, 8, CW), axis=0)
            for j in range(FFN_K + 1):
                acc[pl.ds(8 * j, 8), cols] = part[j]
            return carry

        lax.fori_loop(0, D_FF // CW, col_chunk, 0)
        for j in range(FFN_K):
            dw_ref[pl.ds(j, 1), :] += jnp.sum(acc[pl.ds(8 * j, 8), :], axis=0, keepdims=True)
        db_ref[...] += jnp.sum(acc[pl.ds(8 * FFN_K, 8), :], axis=0, keepdims=True)

    nxt = lambda i: jnp.minimum((i + 1) * hb, nb - 1)
    return pl.pallas_call(
        body, name=name, grid=(n,),
        in_specs=[pl.BlockSpec((tt, D_FF), lambda i: (i, 0)),
                  pl.BlockSpec((HALO, D_FF), lambda i: (nxt(i), 0)),
                  pl.BlockSpec((HALO, D_FF), lambda i: (jnp.maximum(i * hb - 1, 0), 1)),
                  pl.BlockSpec((tt, D_FF), lambda i: (i, 1)),
                  pl.BlockSpec((HALO, D_FF), lambda i: (nxt(i), 1)),
                  pl.BlockSpec((tt, D_FF), lambda i: (i, 0)),
                  pl.BlockSpec((HALO, D_FF), lambda i: (nxt(i), 0)),
                  pl.BlockSpec((8, D_FF), lambda i: (0, 0)),
                  pl.BlockSpec((1, D_FF), lambda i: (0, 0))],
        out_specs=[pl.BlockSpec((tt, 2 * D_FF), lambda i: (i, 0)),
                   pl.BlockSpec((8, D_FF), lambda i: (0, 0)),
                   pl.BlockSpec((1, D_FF), lambda i: (0, 0))],
        out_shape=[jax.ShapeDtypeStruct((T, 2 * D_FF), CD),
                   jax.ShapeDtypeStruct((8, D_FF), F32),
                   jax.ShapeDtypeStruct((1, D_FF), F32)],
        scratch_shapes=[pltpu.VMEM((8 * (FFN_K + 1), D_FF), F32)],
        compiler_params=_params("arbitrary"),
    )(up, up, up, up, up, dact, dact, cw, cb)


def _xattn_probs(q, k):
    s = lax.dot_general(q, k, NT, preferred_element_type=F32) * XA_SCALE
    p = jnp.exp(s - jnp.max(s, axis=-1, keepdims=True))
    return p / jnp.sum(p, axis=-1, keepdims=True)


def _xattn_fwd(q, kv, *, name, tq=1024):
    T = q.shape[0]
    tq = min(tq, T)

    def body(q_ref, kv_ref, o_ref):
        for h in range(XA_HEADS):
            cols = pl.ds(h * XA_DH, XA_DH)
            p = _xattn_probs(q_ref[:, cols], kv_ref[:, cols])
            v = kv_ref[:, pl.ds(D_MODEL + h * XA_DH, XA_DH)]
            o_ref[:, cols] = jnp.dot(p.astype(CD), v, preferred_element_type=F32).astype(o_ref.dtype)

    return pl.pallas_call(
        body, name=name, grid=(T // tq,),
        in_specs=[pl.BlockSpec((tq, D_MODEL), lambda i: (i, 0)),
                  pl.BlockSpec((MEM_LEN, 2 * D_MODEL), lambda i: (0, 0))],
        out_specs=pl.BlockSpec((tq, D_MODEL), lambda i: (i, 0)),
        out_shape=jax.ShapeDtypeStruct((T, D_MODEL), CD),
        compiler_params=_params("parallel"),
    )(q, kv)


def _xattn_bwd(q, kv, do, *, name, tq=1024):
    T = q.shape[0]
    tq = min(tq, T)

    def body(q_ref, kv_ref, do_ref, dq_ref, dkv_ref):
        @pl.when(pl.program_id(0) == 0)
        def _():
            dkv_ref[...] = jnp.zeros_like(dkv_ref)

        for h in range(XA_HEADS):
            cols = pl.ds(h * XA_DH, XA_DH)
            vcols = pl.ds(D_MODEL + h * XA_DH, XA_DH)
            qh, kh, vh, doh = q_ref[:, cols], kv_ref[:, cols], kv_ref[:, vcols], do_ref[:, cols]
            p = _xattn_probs(qh, kh)
            dkv_ref[:, vcols] += lax.dot_general(p.astype(CD), doh, TN, preferred_element_type=F32)
            dp = lax.dot_general(doh, vh, NT, preferred_element_type=F32)
            ds = (p * (dp - jnp.sum(dp * p, axis=-1, keepdims=True)) * XA_SCALE).astype(CD)
            dq_ref[:, cols] = jnp.dot(ds, kh, preferred_element_type=F32).astype(dq_ref.dtype)
            dkv_ref[:, cols] += lax.dot_general(ds, qh, TN, preferred_element_type=F32)

    return pl.pallas_call(
        body, name=name, grid=(T // tq,),
        in_specs=[pl.BlockSpec((tq, D_MODEL), lambda i: (i, 0)),
                  pl.BlockSpec((MEM_LEN, 2 * D_MODEL), lambda i: (0, 0)),
                  pl.BlockSpec((tq, D_MODEL), lambda i: (i, 0))],
        out_specs=[pl.BlockSpec((tq, D_MODEL), lambda i: (i, 0)),
                   pl.BlockSpec((MEM_LEN, 2 * D_MODEL), lambda i: (0, 0))],
        out_shape=[jax.ShapeDtypeStruct((T, D_MODEL), CD),
                   jax.ShapeDtypeStruct((MEM_LEN, 2 * D_MODEL), F32)],
        compiler_params=_params("arbitrary"),
    )(q, kv, do)


C_W = Q_LORA + KV_LORA + LANES


def _rot(x):
    lane = lax.broadcasted_iota(jnp.int32, x.shape, x.ndim - 1)
    up = pltpu.roll(x, LANES - QK_ROPE // 2, x.ndim - 1)
    dn = pltpu.roll(x, QK_ROPE // 2, x.ndim - 1)
    lo, mid, hi = QK_NOPE, QK_NOPE + QK_ROPE // 2, QK_NOPE + QK_ROPE
    return jnp.where((lane >= lo) & (lane < mid), -up, jnp.where((lane >= mid) & (lane < hi), dn, 0.0))


def _mla_mid_fwd(c, qg, kvg, cs, sn, *, name, tt=512):
    T = c.shape[0]
    tt = min(tt, T)

    def body(c_ref, qg_ref, kg_ref, cs_ref, sn_ref, qn_ref, kn_ref, kpe_ref):
        cq = c_ref[:, pl.ds(0, Q_LORA)]
        qn_ref[...] = (cq * lax.rsqrt(jnp.mean(cq * cq, axis=-1, keepdims=True) + EPS)
                       * qg_ref[...]).astype(qn_ref.dtype)
        ck = c_ref[:, pl.ds(Q_LORA, KV_LORA)]
        kn_ref[...] = (ck * lax.rsqrt(jnp.mean(ck * ck, axis=-1, keepdims=True) + EPS)
                       * kg_ref[...]).astype(kn_ref.dtype)
        kp = c_ref[:, pl.ds(Q_LORA + KV_LORA, LANES)]
        kpe_ref[...] = kp * cs_ref[...] + _rot(kp) * sn_ref[...]

    row = lambda w: pl.BlockSpec((tt, w), lambda i: (i, 0))
    one = lambda w: pl.BlockSpec((1, w), lambda i: (0, 0))
    return pl.pallas_call(
        body, name=name, grid=(T // tt,),
        in_specs=[row(C_W), one(Q_LORA), one(KV_LORA), row(LANES), row(LANES)],
        out_specs=[row(Q_LORA), row(KV_LORA), row(LANES)],
        out_shape=[jax.ShapeDtypeStruct((T, Q_LORA), CD), jax.ShapeDtypeStruct((T, KV_LORA), CD),
                   jax.ShapeDtypeStruct((T, LANES), F32)],
        compiler_params=_params("parallel"),
    )(c, qg, kvg, cs, sn)


def _mla_mid_bwd(c, dqn, dkvn, dksum, qg, kvg, cs, sn, *, name, tt=512):
    T = c.shape[0]
    tt = min(tt, T)

    def body(c_ref, dq_ref, dk_ref, ds_ref, qg_ref, kg_ref, cs_ref, sn_ref, dc_ref, dqg_ref, dkg_ref):
        @pl.when(pl.program_id(0) == 0)
        def _():
            dqg_ref[...] = jnp.zeros_like(dqg_ref)
            dkg_ref[...] = jnp.zeros_like(dkg_ref)

        dx, dg = _rms_bwd(c_ref[:, pl.ds(0, Q_LORA)], qg_ref[...], dq_ref[...])
        dc_ref[:, pl.ds(0, Q_LORA)] = dx.astype(dc_ref.dtype)
        dqg_ref[...] += jnp.sum(dg, axis=0, keepdims=True)
        dx, dg = _rms_bwd(c_ref[:, pl.ds(Q_LORA, KV_LORA)], kg_ref[...], dk_ref[...])
        dc_ref[:, pl.ds(Q_LORA, KV_LORA)] = dx.astype(dc_ref.dtype)
        dkg_ref[...] += jnp.sum(dg, axis=0, keepdims=True)
        d = ds_ref[...]
        lane = lax.broadcasted_iota(jnp.int32, d.shape, 1)
        dkp = d * cs_ref[...] - _rot(d * sn_ref[...])
        dc_ref[:, pl.ds(Q_LORA + KV_LORA, LANES)] = jnp.where(
            (lane >= QK_NOPE) & (lane < QK_NOPE + QK_ROPE), dkp, 0.0).astype(dc_ref.dtype)

    row = lambda w: pl.BlockSpec((tt, w), lambda i: (i, 0))
    one = lambda w: pl.BlockSpec((1, w), lambda i: (0, 0))
    return pl.pallas_call(
        body, name=name, grid=(T // tt,),
        in_specs=[row(C_W), row(Q_LORA), row(KV_LORA), row(LANES), one(Q_LORA), one(KV_LORA),
                  row(LANES), row(LANES)],
        out_specs=[row(C_W), one(Q_LORA), one(KV_LORA)],
        out_shape=[jax.ShapeDtypeStruct((T, C_W), CD), jax.ShapeDtypeStruct((1, Q_LORA), F32),
                   jax.ShapeDtypeStruct((1, KV_LORA), F32)],
        compiler_params=_params("arbitrary"),
    )(c, dqn, dkvn, dksum, qg, kvg, cs, sn)


def _mla_qkv_fwd(qn, kvn, kpe, cs, sn, wq, wk, wv, *, name, tt=512):
    T = qn.shape[0]
    tt = min(tt, T)
    H = MLA_HEADS

    def body(qn_ref, kn_ref, kpe_ref, cs_ref, sn_ref, wq_ref, wk_ref, wv_ref, q_ref, k_ref, v_ref):
        qn_v, kn_v, kpe_v, cs_v, sn_v = qn_ref[...], kn_ref[...], kpe_ref[...], cs_ref[...], sn_ref[...]
        for h in range(H):
            q = jnp.dot(qn_v, wq_ref[h], preferred_element_type=F32)
            q_ref[h] = (q * cs_v + _rot(q) * sn_v).astype(q_ref.dtype)
            k_ref[h] = (jnp.dot(kn_v, wk_ref[h], preferred_element_type=F32) + kpe_v).astype(k_ref.dtype)
            v_ref[h] = jnp.dot(kn_v, wv_ref[h], preferred_element_type=F32).astype(v_ref.dtype)

    row = lambda w: pl.BlockSpec((tt, w), lambda i: (i, 0))
    wsp = lambda k: pl.BlockSpec((H, k, LANES), lambda i: (0, 0, 0))
    hsp = pl.BlockSpec((H, tt, LANES), lambda i: (0, i, 0))
    sh = jax.ShapeDtypeStruct((H, T, LANES), CD)
    return pl.pallas_call(
        body, name=name, grid=(T // tt,),
        in_specs=[row(Q_LORA), row(KV_LORA), row(LANES), row(LANES), row(LANES),
                  wsp(Q_LORA), wsp(KV_LORA), wsp(KV_LORA)],
        out_specs=[hsp, hsp, hsp], out_shape=[sh, sh, sh],
        compiler_params=_params("parallel"),
    )(qn, kvn, kpe, cs, sn, wq, wk, wv)


def _mla_qkv_bwd(dq, dk, dv, qn, kvn, cs, sn, wq, wk, wv, *, name, tt=512):
    T = qn.shape[0]
    tt = min(tt, T)
    H = MLA_HEADS

    def body(dq_ref, dk_ref, dv_ref, qn_ref, kn_ref, cs_ref, sn_ref, wq_ref, wk_ref, wv_ref,
             dqn_ref, dkn_ref, dks_ref, dwq_ref, dwk_ref, dwv_ref):
        @pl.when(pl.program_id(0) == 0)
        def _():
            for r in (dwq_ref, dwk_ref, dwv_ref):
                r[...] = jnp.zeros_like(r)

        qn_v, kn_v, cs_v, sn_v = qn_ref[...], kn_ref[...], cs_ref[...], sn_ref[...]
        dqn = jnp.zeros((tt, Q_LORA), F32)
        dkn = jnp.zeros((tt, KV_LORA), F32)
        dks = jnp.zeros((tt, LANES), F32)
        for h in range(H):
            d = dq_ref[h]
            dqh = (d * cs_v - _rot(d * sn_v)).astype(CD)
            dkh, dvh = dk_ref[h], dv_ref[h]
            dqn = dqn + lax.dot_general(dqh, wq_ref[h], NT, preferred_element_type=F32)
            dkn = dkn + lax.dot_general(dkh, wk_ref[h], NT, preferred_element_type=F32)
            dkn = dkn + lax.dot_general(dvh, wv_ref[h], NT, preferred_element_type=F32)
            dks = dks + dkh.astype(F32)
            dwq_ref[h] += lax.dot_general(qn_v, dqh, TN, preferred_element_type=F32)
            dwk_ref[h] += lax.dot_general(kn_v, dkh, TN, preferred_element_type=F32)
            dwv_ref[h] += lax.dot_general(kn_v, dvh, TN, preferred_element_type=F32)
        dqn_ref[...] = dqn
        dkn_ref[...] = dkn
        dks_ref[...] = dks

    row = lambda w: pl.BlockSpec((tt, w), lambda i: (i, 0))
    wsp = lambda k: pl.BlockSpec((H, k, LANES), lambda i: (0, 0, 0))
    hsp = pl.BlockSpec((H, tt, LANES), lambda i: (0, i, 0))
    return pl.pallas_call(
        body, name=name, grid=(T // tt,),
        in_specs=[hsp, hsp, hsp, row(Q_LORA), row(KV_LORA), row(LANES), row(LANES),
                  wsp(Q_LORA), wsp(KV_LORA), wsp(KV_LORA)],
        out_specs=[row(Q_LORA), row(KV_LORA), row(LANES), wsp(Q_LORA), wsp(KV_LORA), wsp(KV_LORA)],
        out_shape=[jax.ShapeDtypeStruct((T, Q_LORA), F32), jax.ShapeDtypeStruct((T, KV_LORA), F32),
                   jax.ShapeDtypeStruct((T, LANES), F32),
                   jax.ShapeDtypeStruct((H, Q_LORA, LANES), F32),
                   jax.ShapeDtypeStruct((H, KV_LORA, LANES), F32),
                   jax.ShapeDtypeStruct((H, KV_LORA, LANES), F32)],
        compiler_params=_params("arbitrary"),
    )(dq, dk, dv, qn, kvn, cs, sn, wq, wk, wv)


FLASH_BLOCK = 1024
FLASH_ROWS = 128
EXP2_SCALE = MLA_SCALE * math.log2(math.e)


def _causal_steps(nq, by_key):
    pairs = [(i, j) for j in range(nq) for i in range(j, nq)] if by_key else \
            [(i, j) for i in range(nq) for j in range(i + 1)]
    return (jnp.asarray([p[0] for p in pairs], jnp.int32), jnp.asarray([p[1] for p in pairs], jnp.int32))


def _raw_scores(q, k, masked, first_row=0):
    s = lax.dot_general(q, k, NT, preferred_element_type=F32)
    if masked:
        row = lax.broadcasted_iota(jnp.int32, s.shape, 0) + first_row
        col = lax.broadcasted_iota(jnp.int32, s.shape, 1)
        s = jnp.where(col <= row, s, NEG)
    return s


def _flash_fwd(q, k, v, *, name):
    H, T, _ = q.shape
    tq = min(FLASH_BLOCK, T)
    nq = T // tq
    i_tab, j_tab = _causal_steps(nq, by_key=False)

    rb = min(FLASH_ROWS, tq)

    def body(i_tab, j_tab, q_ref, k_ref, v_ref, o_ref, lse_ref, m_sc, l_sc, acc, s_sc, p_sc):
        t = pl.program_id(1)
        i, j = i_tab[t], j_tab[t]

        @pl.when(j == 0)
        def _():
            m_sc[...] = jnp.full_like(m_sc, NEG)
            l_sc[...] = jnp.zeros_like(l_sc)
            acc[...] = jnp.zeros_like(acc)

        hb = tq // 2

        def step(masked):
            lane = lax.broadcasted_iota(jnp.int32, (tq, LANES), 1)
            top, bot = pl.ds(0, hb), pl.ds(hb, hb)
            alphas, pvs = [], []
            for h in range(2):
                if masked:
                    s_sc[h, top, top] = _raw_scores(q_ref[h, top, :], k_ref[h, top, :], True)
                    s_sc[h, bot, :] = _raw_scores(q_ref[h, bot, :], k_ref[h], True, first_row=hb)
                    m_cur = jnp.concatenate([jnp.max(s_sc[h, top, top], axis=-1, keepdims=True),
                                             jnp.max(s_sc[h, bot, :], axis=-1, keepdims=True)], axis=0)
                else:
                    s_sc[h] = _raw_scores(q_ref[h], k_ref[h], False)
                    m_cur = jnp.max(s_sc[h], axis=-1, keepdims=True)
                m_prev = m_sc[h]
                m_new = jnp.maximum(m_prev, m_cur)
                alpha = jnp.exp2((m_prev - m_new) * EXP2_SCALE)
                m_sc[h] = m_new
                for r in range(tq // rb):
                    rows = pl.ds(r * rb, rb)
                    m_r = m_sc[h, rows, :]
                    part = jnp.zeros((rb, LANES), F32)
                    keys = hb if masked and r * rb < hb else tq
                    for c in range(keys // LANES):
                        cols = pl.ds(c * LANES, LANES)
                        p = jnp.exp2((s_sc[h, rows, cols] - m_r) * EXP2_SCALE)
                        part = part + p
                        p_sc[h, rows, cols] = p.astype(CD)
                    l_sc[h, rows, :] = (alpha[r * rb:(r + 1) * rb] * l_sc[h, rows, :]
                                        + jnp.sum(part, axis=-1, keepdims=True))
                alphas.append(alpha)
                if masked:
                    pvs.append(jnp.concatenate(
                        [jnp.dot(p_sc[h, top, top], v_ref[h, top, :], preferred_element_type=F32),
                         jnp.dot(p_sc[h, bot, :], v_ref[h], preferred_element_type=F32)], axis=0))
                else:
                    pvs.append(jnp.dot(p_sc[h], v_ref[h], preferred_element_type=F32))
            acc[...] = acc[...] * jnp.where(lane < V_HEAD, alphas[0], alphas[1]) + pvs[0] + pvs[1]

        @pl.when(j < i)
        def _():
            step(False)

        @pl.when(j == i)
        def _():
            step(True)
            lane = lax.broadcasted_iota(jnp.int32, (tq, LANES), 1)
            o_ref[...] = (acc[...] / jnp.where(lane < V_HEAD, l_sc[0], l_sc[1])).astype(o_ref.dtype)
            for h in range(2):
                lse_ref[h] = m_sc[h] * EXP2_SCALE + jnp.log2(l_sc[h])

    qsp = pl.BlockSpec((2, tq, LANES), lambda p, t, it, jt: (p, it[t], 0))
    ksp = pl.BlockSpec((2, tq, LANES), lambda p, t, it, jt: (p, jt[t], 0))
    return pl.pallas_call(
        body, name=name,
        grid_spec=pltpu.PrefetchScalarGridSpec(
            num_scalar_prefetch=2, grid=(H // 2, int(i_tab.shape[0])),
            in_specs=[qsp, ksp, ksp],
            out_specs=[pl.BlockSpec((tq, LANES), lambda p, t, it, jt: (it[t], p)), qsp],
            scratch_shapes=[pltpu.VMEM((2, tq, LANES), F32), pltpu.VMEM((2, tq, LANES), F32),
                            pltpu.VMEM((tq, LANES), F32),
                            pltpu.VMEM((2, tq, tq), F32), pltpu.VMEM((2, tq, tq), CD)]),
        out_shape=[jax.ShapeDtypeStruct((T, H * V_HEAD), CD), jax.ShapeDtypeStruct((H, T, LANES), F32)],
        compiler_params=_params("parallel", "arbitrary"),
    )(i_tab, j_tab, q, k, v)


def _flash_delta(o, do, *, name, tt=512):
    T = o.shape[0]
    tt = min(tt, T)
    H = MLA_HEADS

    def body(o_ref, do_ref, dl_ref):
        lane = lax.broadcasted_iota(jnp.int32, (tt, LANES), 1)
        for p in range(H // 2):
            cols = pl.ds(p * LANES, LANES)
            prod = do_ref[:, cols].astype(F32) * o_ref[:, cols].astype(F32)
            d0 = jnp.sum(jnp.where(lane < V_HEAD, prod, 0.0), axis=-1, keepdims=True)
            d1 = jnp.sum(jnp.where(lane < V_HEAD, 0.0, prod), axis=-1, keepdims=True)
            dl_ref[2 * p] = jnp.broadcast_to(d0, (tt, LANES))
            dl_ref[2 * p + 1] = jnp.broadcast_to(d1, (tt, LANES))

    row = pl.BlockSpec((tt, H * V_HEAD), lambda i: (i, 0))
    return pl.pallas_call(
        body, name=name, grid=(T // tt,), in_specs=[row, row],
        out_specs=pl.BlockSpec((H, tt, LANES), lambda i: (0, i, 0)),
        out_shape=jax.ShapeDtypeStruct((H, T, LANES), F32),
        compiler_params=_params("parallel"),
    )(o, do)


def _flash_bwd(q, k, v, do, lse, delta, *, name):
    H, T, _ = q.shape
    tq = min(FLASH_BLOCK, T)
    nq = T // tq
    i_tab, j_tab = _causal_steps(nq, by_key=True)

    def body(i_tab, j_tab, q_ref, k_ref, v_ref, do_ref, lse_ref, dl_ref, dq_ref, dk_ref, dv_ref, dk_acc, dv_acc):
        t = pl.program_id(1)
        i, j = i_tab[t], j_tab[t]
        rows = pl.ds(pl.multiple_of(i * tq, tq), tq)

        @pl.when(t == 0)
        def _():
            dq_ref[...] = jnp.zeros_like(dq_ref)

        def block(h, qr, kr, first_row, masked):
            qh, kh, vh, do_v = q_ref[h, qr, :], k_ref[h, kr, :], v_ref[h, kr, :], do_ref[qr, :]
            s = _raw_scores(qh, kh, masked, first_row)
            p = jnp.exp2(s * EXP2_SCALE - lse_ref[h, qr, :][:, :1])
            dv_acc[h, kr, :] += lax.dot_general(p.astype(CD), do_v, TN, preferred_element_type=F32)
            dp = lax.dot_general(do_v, vh, NT, preferred_element_type=F32)
            ds = (p * (dp - dl_ref[h, qr, :][:, :1]) * MLA_SCALE).astype(CD)
            dk_acc[h, kr, :] += lax.dot_general(ds, qh, TN, preferred_element_type=F32)
            dq_rows = pl.ds(pl.multiple_of(i * tq + qr.start, qr.size), qr.size)
            dq_ref[h, dq_rows, :] += jnp.dot(ds, kh, preferred_element_type=F32)

        def step(masked):
            hb = tq // 2
            for h in range(2):
                if masked:
                    block(h, pl.ds(0, hb), pl.ds(0, hb), 0, True)
                    block(h, pl.ds(hb, hb), pl.ds(0, tq), hb, True)
                else:
                    block(h, pl.ds(0, tq), pl.ds(0, tq), 0, False)

        @pl.when(i == j)
        def _():
            dk_acc[...] = jnp.zeros_like(dk_acc)
            dv_acc[...] = jnp.zeros_like(dv_acc)
            step(True)

        @pl.when(i > j)
        def _():
            step(False)

        @pl.when(i == nq - 1)
        def _():
            lane = lax.broadcasted_iota(jnp.int32, (tq, LANES), 1)
            dk_ref[...] = dk_acc[...].astype(dk_ref.dtype)
            dv_ref[0] = jnp.where(lane < V_HEAD, dv_acc[0], 0.0).astype(dv_ref.dtype)
            dv_ref[1] = jnp.where(lane < V_HEAD, 0.0, dv_acc[1]).astype(dv_ref.dtype)

    qsp = pl.BlockSpec((2, tq, LANES), lambda p, t, it, jt: (p, it[t], 0))
    ksp = pl.BlockSpec((2, tq, LANES), lambda p, t, it, jt: (p, jt[t], 0))
    osp = pl.BlockSpec((tq, LANES), lambda p, t, it, jt: (it[t], p))
    sh = jax.ShapeDtypeStruct((H, T, LANES), CD)
    return pl.pallas_call(
        body, name=name,
        grid_spec=pltpu.PrefetchScalarGridSpec(
            num_scalar_prefetch=2, grid=(H // 2, int(i_tab.shape[0])),
            in_specs=[qsp, ksp, ksp, osp, qsp, qsp],
            out_specs=[pl.BlockSpec((2, T, LANES), lambda p, t, it, jt: (p, 0, 0)), ksp, ksp],
            scratch_shapes=[pltpu.VMEM((2, tq, LANES), F32), pltpu.VMEM((2, tq, LANES), F32)]),
        out_shape=[jax.ShapeDtypeStruct((H, T, LANES), F32), sh, sh],
        compiler_params=_params("parallel", "arbitrary"),
    )(i_tab, j_tab, q, k, v, do, lse, delta)


def _loss_head(x, g, target, *, name, tt=1024):
    T, D = x.shape
    tt = min(tt, T)

    def body(x_ref, g_ref, t_ref, dx_ref, dg_ref, loss_ref):
        @pl.when(pl.program_id(0) == 0)
        def _():
            dg_ref[...] = jnp.zeros_like(dg_ref)
            loss_ref[...] = jnp.zeros_like(loss_ref)

        xv, gv = x_ref[...], g_ref[...]
        r = lax.rsqrt(jnp.mean(xv * xv, axis=-1, keepdims=True) + EPS)
        err = xv * r * gv - t_ref[...]
        tok = jnp.mean(err * err, axis=-1, keepdims=True)
        loss_ref[...] += 0.5 * jnp.sum(tok, axis=0, keepdims=True)
        dx, dg_rows = _rms_bwd(xv, gv, err * (1.0 / D))
        dx_ref[...] = dx
        dg_ref[...] += jnp.sum(dg_rows, axis=0, keepdims=True)

    return pl.pallas_call(
        body, name=name, grid=(T // tt,),
        in_specs=[pl.BlockSpec((tt, D), lambda i: (i, 0)), pl.BlockSpec((1, D), lambda i: (0, 0)),
                  pl.BlockSpec((tt, D), lambda i: (i, 0))],
        out_specs=[pl.BlockSpec((tt, D), lambda i: (i, 0)), pl.BlockSpec((1, D), lambda i: (0, 0)),
                   pl.BlockSpec((1, LANES), lambda i: (0, 0))],
        out_shape=[jax.ShapeDtypeStruct((T, D), F32), jax.ShapeDtypeStruct((1, D), F32),
                   jax.ShapeDtypeStruct((1, LANES), F32)],
        compiler_params=_params("arbitrary"),
    )(x, g, target)


def _rope_tables(positions):
    inv = 1.0 / (ROPE_THETA ** (jnp.arange(0, QK_ROPE, 2, dtype=F32) / QK_ROPE))
    ang = positions.astype(F32)[:, None] * inv
    c, s = jnp.cos(ang), jnp.sin(ang)
    T = positions.shape[0]
    cs = jnp.concatenate([jnp.ones((T, QK_NOPE), F32), c, c, jnp.zeros((T, LANES - QK_NOPE - QK_ROPE), F32)], 1)
    sn = jnp.concatenate([jnp.zeros((T, QK_NOPE), F32), s, s, jnp.zeros((T, LANES - QK_NOPE - QK_ROPE), F32)], 1)
    return cs, sn


def _pad_rows(w, rows):
    return jnp.concatenate([w, jnp.zeros((rows - w.shape[0],) + w.shape[1:], w.dtype)], 0)


def _mla_weights(w_dq_dkv, w_uq, w_ukv):
    K = w_dq_dkv.shape[0]
    z = lambda n: jnp.zeros((K, n), w_dq_dkv.dtype)
    wc = jnp.concatenate([w_dq_dkv[:, :Q_LORA + KV_LORA], z(QK_NOPE), w_dq_dkv[:, Q_LORA + KV_LORA:],
                          z(LANES - QK_NOPE - QK_ROPE)], 1)
    wq = w_uq.reshape(Q_LORA, MLA_HEADS, QK_NOPE + QK_ROPE).transpose(1, 0, 2)
    wq = jnp.concatenate([wq, jnp.zeros((MLA_HEADS, Q_LORA, LANES - QK_NOPE - QK_ROPE), wq.dtype)], 2)
    wkv = w_ukv.reshape(KV_LORA, MLA_HEADS, QK_NOPE + V_HEAD).transpose(1, 0, 2)
    zero = jnp.zeros_like(wkv[:, :, :QK_NOPE])
    wk = jnp.concatenate([wkv[:, :, :QK_NOPE], zero], 2)
    wv_lo = jnp.concatenate([wkv[:, :, QK_NOPE:], zero], 2)
    wv_hi = jnp.concatenate([zero, wkv[:, :, QK_NOPE:]], 2)
    odd = (jnp.arange(MLA_HEADS) % 2 == 1)[:, None, None]
    wv = jnp.where(odd, wv_hi, wv_lo)
    return wc, wq, wk, wv


def _mla_weight_grads(dwc, dwq, dwk, dwv):
    d_dq = jnp.concatenate([dwc[:, :Q_LORA + KV_LORA],
                            dwc[:, Q_LORA + KV_LORA + QK_NOPE:Q_LORA + KV_LORA + QK_NOPE + QK_ROPE]], 1)
    d_uq = dwq[:, :, :QK_NOPE + QK_ROPE].transpose(1, 0, 2).reshape(Q_LORA, MLA_HEADS * (QK_NOPE + QK_ROPE))
    odd = (jnp.arange(MLA_HEADS) % 2 == 1)[:, None, None]
    dv = jnp.where(odd, dwv[:, :, V_HEAD:], dwv[:, :, :V_HEAD])
    d_ukv = jnp.concatenate([dwk[:, :, :QK_NOPE], dv], 2).transpose(1, 0, 2).reshape(
        KV_LORA, MLA_HEADS * (QK_NOPE + V_HEAD))
    return d_dq, d_uq, d_ukv


def _local_step(x, mem, positions, target, W):
    G = {}
    row = lambda v: v.reshape(1, -1)
    cs, sn = _rope_tables(positions)
    saved = []
    for l in range(DEPTH):
        L = f"l{l}"
        s = {"x0": x}
        if l % 2 == 0:
            e = l // 2
            s["z"], s["h"] = _nmm(x, row(W["norm_mix_g"][l]), (W["pc_w_in"], e), name=f"{L}_mix_in", out_dtype=F32)
            s["dw_w"] = _pad_rows(W["conv_dw_w"][e], CONV_K + 1)
            s["mix_p"] = (W["pool_w"][e], row(W["pool_scale"][e]), s["dw_w"], row(W["conv_dw_b"][e]),
                          row(W["conv_ln_g"][e]), row(W["conv_ln_b"][e]))
            s["ycat"] = _mixer_fwd(s["z"], *s["mix_p"], name=f"{L}_mix_mid")
            x = _mm_res(s["ycat"], (W["pc_w_out"], e), x, name=f"{L}_mix_out")
        else:
            o = l // 2
            wc, wq, wk, wv = _mla_weights(W["mla_w_dq_dkv"][o], W["mla_w_uq"][o], W["mla_w_ukv"][o])
            s["mla_w"] = (wc, wq, wk, wv)
            s["c"], s["h"] = _nmm(x, row(W["norm_mix_g"][l]), wc, name=f"{L}_mla_down", out_dtype=F32)
            s["qg"], s["kvg"] = row(W["mla_q_norm_g"][o]), row(W["mla_kv_norm_g"][o])
            s["qn"], s["kvn"], kpe = _mla_mid_fwd(s["c"], s["qg"], s["kvg"], cs, sn, name=f"{L}_mla_mid")
            s["q"], s["k"], s["v"] = _mla_qkv_fwd(s["qn"], s["kvn"], kpe, cs, sn, wq, wk, wv, name=f"{L}_mla_qkv")
            s["o"], s["lse"] = _flash_fwd(s["q"], s["k"], s["v"], name=f"{L}_mla_attn")
            x = _mm_res(s["o"], (W["mla_w_o"], o), x, name=f"{L}_mla_out")
        s["x1"] = x
        s["xq"], s["hx"] = _nmm(x, row(W["norm_xa_g"][l]), (W["xa_wq"], l), name=f"{L}_xa_q", out_dtype=CD)
        s["xkv"], s["hm"] = _nmm(mem, row(W["norm_mem_g"][l]), (W["xa_wkv"], l), name=f"{L}_xa_kv", out_dtype=CD)
        s["xo"] = _xattn_fwd(s["xq"], s["xkv"], name=f"{L}_xa_attn")
        x = _mm_res(s["xo"], (W["xa_wo"], l), x, name=f"{L}_xa_out")
        s["x2"] = x
        s["up"], s["hf"] = _nmm(x, row(W["norm_ffn_g"][l]), (W["ffn_w_up"], l), name=f"{L}_ffn_up", out_dtype=CD,
                                tn_target=2816)
        s["cw"], s["cb"] = _pad_rows(W["ffn_conv_w"][l], 8), row(W["ffn_conv_b"][l])
        s["act"] = _ffn_mid_fwd(s["up"], s["cw"], s["cb"], name=f"{L}_ffn_mid")
        x = _mm_res(s["act"], (W["ffn_w_down"], l), x, name=f"{L}_ffn_down")
        saved.append(s)
    dx, G["final_norm_g"], loss = _loss_head(x, row(W["final_norm_g"]), target, name="loss_head")
    G["final_norm_g"] = G["final_norm_g"].reshape(-1)

    per_layer = {}

    def put(name, l, val):
        per_layer.setdefault(name, {})[l] = val

    def put_dw(weight, l, a, g, **kw):
        stack = G[weight] if weight in G else lax.empty(W[weight].shape, F32)
        G[weight] = _mm_tn(a, g, into=(stack, l), **kw)

    for l in reversed(range(DEPTH)):
        L = f"l{l}"
        s = saved[l]
        put_dw("ffn_w_down", l, s["act"], dx, name=f"{L}_ffn_down_dw", tk_target=1408)
        dact = _mm_nt(dx, (W["ffn_w_down"], l), name=f"{L}_ffn_down_dx", out_dtype=CD, tn_target=2816)
        dup, dcw, dcb = _ffn_mid_bwd(s["up"], dact, s["cw"], s["cb"], name=f"{L}_ffn_mid_bwd")
        put("ffn_conv_w", l, dcw[:FFN_K])
        put("ffn_conv_b", l, dcb[0])
        put_dw("ffn_w_up", l, s["hf"], dup, name=f"{L}_ffn_up_dw", tn_target=1408)
        dx, dg = _mm_nt_normbwd(dup, (W["ffn_w_up"], l), s["x2"], row(W["norm_ffn_g"][l]), dx, name=f"{L}_ffn_up_dx")
        put("norm_ffn_g", l, dg[0])
        put_dw("xa_wo", l, s["xo"], dx, name=f"{L}_xa_out_dw")
        do = _mm_nt(dx, (W["xa_wo"], l), name=f"{L}_xa_out_dx", out_dtype=CD)
        dq, dkv = _xattn_bwd(s["xq"], s["xkv"], do, name=f"{L}_xa_attn_bwd")
        put_dw("xa_wq", l, s["hx"], dq, name=f"{L}_xa_q_dw")
        dx, dg = _mm_nt_normbwd(dq, (W["xa_wq"], l), s["x1"], row(W["norm_xa_g"][l]), dx, name=f"{L}_xa_q_dx")
        put("norm_xa_g", l, dg[0])
        put_dw("xa_wkv", l, s["hm"], dkv, name=f"{L}_xa_kv_dw", tt=MEM_LEN)
        _, dg = _mm_nt_normbwd(dkv, (W["xa_wkv"], l), mem, row(W["norm_mem_g"][l]), jnp.zeros_like(mem),
                               name=f"{L}_xa_kv_dx", tm=MEM_LEN)
        put("norm_mem_g", l, dg[0])
        if l % 2 == 0:
            e = l // 2
            put_dw("pc_w_out", e, s["ycat"], dx, name=f"{L}_mix_out_dw")
            dy = _mm_nt(dx, (W["pc_w_out"], e), name=f"{L}_mix_out_dx", out_dtype=F32)
            dz, dpw, dps, ddw, ddb, dlg, dlb = _mixer_bwd(s["z"], dy, *s["mix_p"], name=f"{L}_mix_mid_bwd")
            put("pool_w", e, dpw)
            put("pool_scale", e, dps[0])
            put("conv_dw_w", e, ddw[:CONV_K])
            put("conv_dw_b", e, ddb[0])
            put("conv_ln_g", e, dlg[0])
            put("conv_ln_b", e, dlb[0])
            put_dw("pc_w_in", e, s["h"], dz, name=f"{L}_mix_in_dw")
            dx, dg = _mm_nt_normbwd(dz, (W["pc_w_in"], e), s["x0"], row(W["norm_mix_g"][l]), dx, name=f"{L}_mix_in_dx")
        else:
            o = l // 2
            wc, wq, wk, wv = s["mla_w"]
            put_dw("mla_w_o", o, s["o"], dx, name=f"{L}_mla_out_dw")
            do = _mm_nt(dx, (W["mla_w_o"], o), name=f"{L}_mla_out_dx", out_dtype=CD)
            delta = _flash_delta(s["o"], do, name=f"{L}_mla_attn_delta")
            dq, dk, dv = _flash_bwd(s["q"], s["k"], s["v"], do, s["lse"], delta, name=f"{L}_mla_attn_bwd")
            dqn, dkvn, dks, dwq, dwk, dwv = _mla_qkv_bwd(dq, dk, dv, s["qn"], s["kvn"], cs, sn, wq, wk, wv,
                                                         name=f"{L}_mla_qkv_bwd")
            dc, dqg, dkg = _mla_mid_bwd(s["c"], dqn, dkvn, dks, s["qg"], s["kvg"], cs, sn, name=f"{L}_mla_mid_bwd")
            put("mla_q_norm_g", o, dqg[0])
            put("mla_kv_norm_g", o, dkg[0])
            dwc = _mm_tn(s["h"], dc, name=f"{L}_mla_down_dw")
            d_dq, d_uq, d_ukv = _mla_weight_grads(dwc, dwq, dwk, dwv)
            put("mla_w_dq_dkv", o, d_dq)
            put("mla_w_uq", o, d_uq)
            put("mla_w_ukv", o, d_ukv)
            dx, dg = _mm_nt_normbwd(dc, wc, s["x0"], row(W["norm_mix_g"][l]), dx, name=f"{L}_mla_down_dx",
                                    tk_target=768)
        put("norm_mix_g", l, dg[0])
    for name, d in per_layer.items():
        G[name] = jnp.stack([d[i] for i in sorted(d)], 0)
    return loss, dx, G


_ANY = pl.BlockSpec(memory_space=pl.ANY)


def _all_gather(xs, *, name):
    n = len(xs)

    def body(*refs):
        x_refs, out_refs = refs[:n], refs[n:2 * n]
        send_sems, recv_sems, local_sems = refs[2 * n:]
        mx, my, mc = lax.axis_index("x"), lax.axis_index("y"), lax.axis_index("c")
        me, sibling = (mx, my, mc), (mx, my, 1 - mc)
        xn, yn, dg = (1 - mx, my), (mx, 1 - my), (1 - mx, 1 - my)
        src = (mx + (1 - mc) * (1 - 2 * mx), my + mc * (1 - 2 * my))
        dst = (mx + mc * (1 - 2 * mx), my + (1 - mc) * (1 - 2 * my))
        SIB, XN, YN, DG, PASS = 0, 1, 2, 3, 4

        def copy(a, k, block, to, own=False):
            px, py, pc = block
            slot = out_refs[a].at[4 * px + 2 * py + pc]
            return pltpu.make_async_remote_copy(
                src_ref=x_refs[a] if own else slot, dst_ref=slot,
                send_sem=send_sems.at[7 * a + k], recv_sem=recv_sems.at[7 * a + k],
                device_id=to, device_id_type=MESH)

        mine = [pltpu.make_async_copy(x_refs[a], out_refs[a].at[4 * mx + 2 * my + mc], local_sems.at[a])
                for a in range(n)]
        for cp in mine:
            cp.start()
        sent = [copy(a, XN, me, (*xn, mc), own=True) for a in range(n)]
        sent += [copy(a, YN, me, (*yn, mc), own=True) for a in range(n)]
        sent += [copy(a, SIB, me, sibling, own=True) for a in range(n)]
        for cp in sent:
            cp.start()
        for a in range(n):
            for k, chip in ((XN, xn), (YN, yn)):
                copy(a, k, (*chip, mc), me).wait_recv()
                sent.append(copy(a, PASS + k - 1, (*chip, mc), sibling))
                sent[-1].start()
            sent.append(copy(a, DG, (*src, mc), (*dst, mc)))
            sent[-1].start()
        for a in range(n):
            copy(a, DG, (*dg, mc), me).wait_recv()
            sent.append(copy(a, PASS + DG - 1, (*dg, mc), sibling))
            sent[-1].start()
        for a in range(n):
            copy(a, SIB, sibling, me).wait_recv()
            for k, chip in ((XN, xn), (YN, yn), (DG, dg)):
                copy(a, PASS + k - 1, (*chip, 1 - mc), me).wait_recv()
        for cp in sent:
            cp.wait_send()
        for cp in mine:
            cp.wait()

    return pl.pallas_call(
        body, name=name, in_specs=[_ANY] * n, out_specs=[_ANY] * n,
        out_shape=[jax.ShapeDtypeStruct((N_DEV,) + x.shape, x.dtype) for x in xs],
        scratch_shapes=[pltpu.SemaphoreType.DMA((7 * n,)), pltpu.SemaphoreType.DMA((7 * n,)),
                        pltpu.SemaphoreType.DMA((n,))],
    )(*xs)


N_CHIP = 4


def _pair_exchange(ps, *, name):
    n = len(ps)

    def body(*refs):
        p_refs, out_refs = refs[:n], refs[n:2 * n]
        send_sems, recv_sems = refs[2 * n:]
        mx, my, mc = lax.axis_index("x"), lax.axis_index("y"), lax.axis_index("c")
        copies = []
        for a in range(n):
            for chip in range(N_CHIP):
                copies.append(pltpu.make_async_remote_copy(
                    src_ref=p_refs[a].at[2 * chip + (1 - mc)], dst_ref=out_refs[a].at[chip],
                    send_sem=send_sems.at[N_CHIP * a + chip], recv_sem=recv_sems.at[N_CHIP * a + chip],
                    device_id=(mx, my, 1 - mc), device_id_type=MESH))
        for cp in copies:
            cp.start()
        for cp in copies:
            cp.wait()

    return pl.pallas_call(
        body, name=name, in_specs=[_ANY] * n, out_specs=[_ANY] * n,
        out_shape=[jax.ShapeDtypeStruct((N_CHIP,) + p.shape[1:], p.dtype) for p in ps],
        scratch_shapes=[pltpu.SemaphoreType.DMA((N_CHIP * n,)), pltpu.SemaphoreType.DMA((N_CHIP * n,))],
    )(*ps)


def _pair_sum(p, recv, core, *, name):
    _, R, C = p.shape
    tr = _row_tile(R, C, 4 * ROW_TILE_ELEMS)
    p4 = p.reshape(N_CHIP, 2, R, C)

    def body(core_ref, a_ref, b_ref, o_ref):
        o_ref[...] = (a_ref[...].astype(F32) + b_ref[...].astype(F32)).astype(o_ref.dtype)

    return pl.pallas_call(
        body, name=name,
        grid_spec=pltpu.PrefetchScalarGridSpec(
            num_scalar_prefetch=1, grid=(N_CHIP, R // tr),
            in_specs=[pl.BlockSpec((None, None, tr, C), lambda ch, i, core: (ch, core[0], i, 0)),
                      pl.BlockSpec((None, tr, C), lambda ch, i, core: (ch, i, 0))],
            out_specs=pl.BlockSpec((None, tr, C), lambda ch, i, core: (ch, i, 0))),
        out_shape=jax.ShapeDtypeStruct((N_CHIP, R, C), p.dtype),
        compiler_params=_params("parallel", "parallel"),
    )(core, p4, recv)


def _chip_exchange(ss, *, name):
    n = len(ss)

    def body(*refs):
        s_refs, out_refs, stage_refs = refs[:n], refs[n:2 * n], refs[2 * n:3 * n]
        send_sems, recv_sems, local_sems = refs[3 * n:]
        mx, my, mc = lax.axis_index("x"), lax.axis_index("y"), lax.axis_index("c")
        chip = 2 * mx + my
        xn, yn, dg = (1 - mx, my), (mx, 1 - my), (1 - mx, 1 - my)
        via = (mx + (1 - mc) * (1 - 2 * mx), my + mc * (1 - 2 * my))
        onward = (mx + mc * (1 - 2 * mx), my + (1 - mc) * (1 - 2 * my))
        XN, YN, STAGE, ONWARD = 0, 1, 2, 3

        def copy(a, k, src, dst, to):
            return pltpu.make_async_remote_copy(
                src_ref=src, dst_ref=dst, send_sem=send_sems.at[4 * a + k], recv_sem=recv_sems.at[4 * a + k],
                device_id=(*to, mc), device_id_type=MESH)

        def slot(ref, c):
            return ref.at[2 * c[0] + c[1]]

        mine = [pltpu.make_async_copy(s_refs[a].at[chip], out_refs[a].at[chip], local_sems.at[a]) for a in range(n)]
        for cp in mine:
            cp.start()
        first = []
        for a in range(n):
            first.append(copy(a, STAGE, slot(s_refs[a], dg), stage_refs[a], via))
            first.append(copy(a, XN, slot(s_refs[a], xn), out_refs[a].at[chip], xn))
            first.append(copy(a, YN, slot(s_refs[a], yn), out_refs[a].at[chip], yn))
        for cp in first:
            cp.start()
        onwards = []
        for a in range(n):
            first[3 * a].wait_recv()
            onwards.append(copy(a, ONWARD, stage_refs[a], slot(out_refs[a], via), onward))
            onwards[-1].start()
        for a in range(n):
            first[3 * a + 1].wait_recv()
            first[3 * a + 2].wait_recv()
            onwards[a].wait_recv()
        for cp in first + onwards:
            cp.wait_send()
        for cp in mine:
            cp.wait()

    outs = pl.pallas_call(
        body, name=name, in_specs=[_ANY] * n, out_specs=[_ANY] * (2 * n),
        out_shape=[jax.ShapeDtypeStruct(s.shape, s.dtype) for s in ss]
                  + [jax.ShapeDtypeStruct(s.shape[1:], s.dtype) for s in ss],
        scratch_shapes=[pltpu.SemaphoreType.DMA((4 * n,)), pltpu.SemaphoreType.DMA((4 * n,)),
                        pltpu.SemaphoreType.DMA((n,))],
    )(*ss)
    return outs[:n]


ROW_TILE_ELEMS = 256 * 1024


def _row_tile(R, C, elems=None):
    elems = ROW_TILE_ELEMS if elems is None else elems
    for t in (4096, 2048, 1024, 512, 256, 128, 64, 32, 16):
        if R % t == 0 and t * C <= elems:
            return t
    raise ValueError((R, C))


def _sum_slots(gs, *, name):
    S, R, C = gs.shape
    tr = _row_tile(R, C)

    def body(g_ref, o_ref):
        g = g_ref[0].astype(F32)
        for s in range(1, S):
            g = g + g_ref[s].astype(F32)
        o_ref[...] = g

    return pl.pallas_call(
        body, name=name, grid=(R // tr,),
        in_specs=[pl.BlockSpec((S, tr, C), lambda i: (0, i, 0))],
        out_specs=pl.BlockSpec((tr, C), lambda i: (i, 0)),
        out_shape=jax.ShapeDtypeStruct((R, C), F32),
        compiler_params=_params("parallel"),
    )(gs)


def _adamw(gs, w, m, v, *, name):
    S, R, C = gs.shape
    tr = _row_tile(R, C, 2 * ROW_TILE_ELEMS)

    def body(g_ref, w_ref, m_ref, v_ref, g_out, d_out, m_out, v_out):
        g = g_ref[0].astype(F32)
        for s in range(1, S):
            g = g + g_ref[s].astype(F32)
        m_new = ADAM_B1 * m_ref[...] + (1.0 - ADAM_B1) * g
        v_new = ADAM_B2 * v_ref[...] + (1.0 - ADAM_B2) * (g * g)
        m_hat = m_new / (1.0 - ADAM_B1 ** ADAM_STEP)
        v_hat = v_new / (1.0 - ADAM_B2 ** ADAM_STEP)
        g_out[...] = g
        d_out[...] = -ADAM_LR * (m_hat / (jnp.sqrt(v_hat) + ADAM_EPS) + ADAM_WD * w_ref[...])
        m_out[...] = m_new
        v_out[...] = v_new

    blk = pl.BlockSpec((tr, C), lambda i: (i, 0))
    sh = jax.ShapeDtypeStruct((R, C), F32)
    return pl.pallas_call(
        body, name=name, grid=(R // tr,),
        in_specs=[pl.BlockSpec((S, tr, C), lambda i: (0, i, 0)), blk, blk, blk],
        out_specs=[blk, blk, blk, blk], out_shape=[sh, sh, sh, sh],
        compiler_params=_params("parallel"),
    )(gs, w, m, v)


PACK_ROWS = 8


def _pack(arrs, dtype, lead, row_mult):
    lead_shape = arrs[0].shape[:lead]
    parts, meta, off = [], [], 0
    for a in arrs:
        size = math.prod(a.shape[lead:])
        rows = -(-size // LANES)
        x = a.astype(dtype)
        if size % LANES:
            x = jnp.concatenate([x.reshape(lead_shape + (size,)),
                                 jnp.zeros(lead_shape + (rows * LANES - size,), dtype)], -1)
        x = x.reshape(lead_shape + (rows, LANES))
        padded = -(-rows // PACK_ROWS) * PACK_ROWS
        if padded != rows:
            x = jnp.concatenate([x, jnp.zeros(lead_shape + (padded - rows, LANES), dtype)], lead)
        parts.append(x)
        meta.append((off, size, a.shape[lead:]))
        off += padded
    total = -(-off // row_mult) * row_mult
    if total != off:
        parts.append(jnp.zeros(lead_shape + (total - off, LANES), dtype))
    return jnp.concatenate(parts, lead), meta


def _unpack(packed, meta, lead):
    lead_shape = packed.shape[:lead]
    out = []
    for off, size, shape in meta:
        rows = -(-size // LANES)
        x = lax.slice_in_dim(packed, off, off + rows, axis=lead)
        if size % LANES:
            x = x.reshape(lead_shape + (rows * LANES,))[..., :size]
        out.append(x.reshape(lead_shape + shape))
    return out


ARG_NAMES = ['x', 'mem', 'positions', 'norm_mix_g', 'norm_xa_g', 'norm_mem_g', 'xa_wq', 'xa_wkv', 'xa_wo', 'norm_ffn_g', 'ffn_w_up', 'ffn_conv_w', 'ffn_conv_b', 'ffn_w_down', 'pc_w_in', 'pool_w', 'pool_scale', 'conv_dw_w', 'conv_dw_b', 'conv_ln_g', 'conv_ln_b', 'pc_w_out', 'mla_w_dq_dkv', 'mla_q_norm_g', 'mla_w_uq', 'mla_kv_norm_g', 'mla_w_ukv', 'mla_w_o', 'final_norm_g', 'loss_target']
WEIGHTS = ARG_NAMES[3:29]
BIG = {'xa_wq': 1, 'xa_wkv': 2, 'xa_wo': 1, 'ffn_w_up': 2, 'ffn_w_down': 1, 'pc_w_in': 2, 'pc_w_out': 1,
       'mla_w_dq_dkv': 1, 'mla_w_uq': 2, 'mla_w_ukv': 2, 'mla_w_o': 1}
SMALL_SHARDED = {'ffn_conv_w': 2, 'conv_dw_w': 2, 'mla_q_norm_g': 1, 'mla_kv_norm_g': 1}
REPLICATED = [n for n in WEIGHTS if n not in BIG and n not in SMALL_SHARDED]


def _from_slots(g, axis):
    t = jnp.moveaxis(g, 0, axis)
    return t.reshape(t.shape[:axis] + (t.shape[axis] * t.shape[axis + 1],) + t.shape[axis + 2:])


def _to_slots(full, axis):
    n = full.shape[axis] // N_DEV
    t = full.reshape(full.shape[:axis] + (N_DEV, n) + full.shape[axis + 1:])
    return jnp.moveaxis(t, axis, 0)


def kernel(x, mem, positions, norm_mix_g, norm_xa_g, norm_mem_g, xa_wq, xa_wkv, xa_wo, norm_ffn_g, ffn_w_up, ffn_conv_w, ffn_conv_b, ffn_w_down, pc_w_in, pool_w, pool_scale, conv_dw_w, conv_dw_b, conv_ln_g, conv_ln_b, pc_w_out, mla_w_dq_dkv, mla_q_norm_g, mla_w_uq, mla_kv_norm_g, mla_w_ukv, mla_w_o, final_norm_g, loss_target, m_norm_mix_g, m_norm_xa_g, m_norm_mem_g, m_xa_wq, m_xa_wkv, m_xa_wo, m_norm_ffn_g, m_ffn_w_up, m_ffn_conv_w, m_ffn_conv_b, m_ffn_w_down, m_pc_w_in, m_pool_w, m_pool_scale, m_conv_dw_w, m_conv_dw_b, m_conv_ln_g, m_conv_ln_b, m_pc_w_out, m_mla_w_dq_dkv, m_mla_q_norm_g, m_mla_w_uq, m_mla_kv_norm_g, m_mla_w_ukv, m_mla_w_o, m_final_norm_g, v_norm_mix_g, v_norm_xa_g, v_norm_mem_g, v_xa_wq, v_xa_wkv, v_xa_wo, v_norm_ffn_g, v_ffn_w_up, v_ffn_conv_w, v_ffn_conv_b, v_ffn_w_down, v_pc_w_in, v_pool_w, v_pool_scale, v_conv_dw_w, v_conv_dw_b, v_conv_ln_g, v_conv_ln_b, v_pc_w_out, v_mla_w_dq_dkv, v_mla_q_norm_g, v_mla_w_uq, v_mla_kv_norm_g, v_mla_w_ukv, v_mla_w_o, v_final_norm_g):
    args = (x, mem, positions, norm_mix_g, norm_xa_g, norm_mem_g, xa_wq, xa_wkv, xa_wo, norm_ffn_g, ffn_w_up, ffn_conv_w, ffn_conv_b, ffn_w_down, pc_w_in, pool_w, pool_scale, conv_dw_w, conv_dw_b, conv_ln_g, conv_ln_b, pc_w_out, mla_w_dq_dkv, mla_q_norm_g, mla_w_uq, mla_kv_norm_g, mla_w_ukv, mla_w_o, final_norm_g, loss_target)
    a = dict(zip(ARG_NAMES, args))
    mom = dict(zip(WEIGHTS, (m_norm_mix_g, m_norm_xa_g, m_norm_mem_g, m_xa_wq, m_xa_wkv, m_xa_wo, m_norm_ffn_g, m_ffn_w_up, m_ffn_conv_w, m_ffn_conv_b, m_ffn_w_down, m_pc_w_in, m_pool_w, m_pool_scale, m_conv_dw_w, m_conv_dw_b, m_conv_ln_g, m_conv_ln_b, m_pc_w_out, m_mla_w_dq_dkv, m_mla_q_norm_g, m_mla_w_uq, m_mla_kv_norm_g, m_mla_w_ukv, m_mla_w_o, m_final_norm_g)))
    var = dict(zip(WEIGHTS, (v_norm_mix_g, v_norm_xa_g, v_norm_mem_g, v_xa_wq, v_xa_wkv, v_xa_wo, v_norm_ffn_g, v_ffn_w_up, v_ffn_conv_w, v_ffn_conv_b, v_ffn_w_down, v_pc_w_in, v_pool_w, v_pool_scale, v_conv_dw_w, v_conv_dw_b, v_conv_ln_g, v_conv_ln_b, v_pc_w_out, v_mla_w_dq_dkv, v_mla_q_norm_g, v_mla_w_uq, v_mla_kv_norm_g, v_mla_w_ukv, v_mla_w_o, v_final_norm_g)))
    me = 4 * lax.axis_index("x") + 2 * lax.axis_index("y") + lax.axis_index("c")

    big_all = _all_gather([a[n].astype(CD) for n in BIG], name="gather_weights")
    sm_pack, sm_meta = _pack([a[n] for n in SMALL_SHARDED], F32, 0, 8)
    sm_all = _unpack(_all_gather([sm_pack], name="gather_small")[0], sm_meta, 1)
    W = {n: a[n] for n in REPLICATED}
    for (n, ax), g in zip(BIG.items(), big_all):
        W[n] = _from_slots(g, ax)
    for (n, ax), g in zip(SMALL_SHARDED.items(), sm_all):
        W[n] = _from_slots(g, ax)

    loss, dx, G = _local_step(x[0], mem[0], positions[0], loss_target[0], W)

    parts = [_to_slots(G[n], ax).astype(CD) for n, ax in BIG.items()]
    from_sibling = _pair_exchange(parts, name="grads_to_sibling")
    core = lax.axis_index("c").astype(jnp.int32).reshape(1)
    sums = []
    for n, p, r in zip(BIG, parts, from_sibling):
        cols = p.shape[-1]
        s = _pair_sum(p.reshape(N_DEV, -1, cols), r.reshape(N_CHIP, -1, cols), core, name=f"pair_sum_{n}")
        sums.append(s.reshape((N_CHIP,) + p.shape[1:]))
    recv = _chip_exchange(sums, name="scatter_grads")
    out = {}
    for n, r in zip(BIG, recv):
        shape = a[n].shape
        rows = lambda t: t.reshape(-1, shape[-1])
        res = _adamw(r.reshape(N_CHIP, -1, shape[-1]), rows(a[n]), rows(mom[n]), rows(var[n]), name=f"adamw_{n}")
        out[n] = tuple(t.reshape(shape) for t in res)

    small_names = REPLICATED + list(SMALL_SHARDED)
    spack, smeta = _pack([G[n] for n in small_names] + [loss], F32, 0, 256)
    stot = _unpack(_sum_slots(_all_gather([spack], name="gather_small_grads")[0], name="sum_small_grads"), smeta, 0)
    loss_total = stot[-1][0, 0]
    gsm = dict(zip(small_names, stot[:-1]))
    for n, ax in SMALL_SHARDED.items():
        width = a[n].shape[ax]
        gsm[n] = lax.dynamic_slice_in_dim(gsm[n], me * width, width, ax)
    g1, meta1 = _pack([gsm[n] for n in small_names], F32, 0, 256)
    w1, _ = _pack([a[n] for n in small_names], F32, 0, 256)
    m1, _ = _pack([mom[n] for n in small_names], F32, 0, 256)
    v1, _ = _pack([var[n] for n in small_names], F32, 0, 256)
    res = [_unpack(r, meta1, 0) for r in _adamw(g1[None], w1, m1, v1, name="adamw_small")]
    for i, n in enumerate(small_names):
        out[n] = tuple(r[i] for r in res)

    return (loss_total, dx[None],
            *[out[n][0] for n in WEIGHTS], *[out[n][1] for n in WEIGHTS],
            *[out[n][2] for n in WEIGHTS], *[out[n][3] for n in WEIGHTS])
```

```python
import functools
import math

import jax
import jax.numpy as jnp
from jax import lax
from jax.experimental import pallas as pl
from jax.experimental.pallas import tpu as pltpu

F32 = jnp.float32
CD = jnp.bfloat16
EPS = 1e-6
NEG = -1e30
N_DEV = 8
LANES = 128
HALO = 32

D_MODEL = 1024
DEPTH = 4
XA_HEADS = 4
XA_DH = 256
MEM_LEN = 256
POOL_WINDOWS = (2, 4, 8, 16)
CONV_K = 31
FFN_K = 3
D_FF = 2816
MLA_HEADS = 16
QK_NOPE = 64
QK_ROPE = 32
V_HEAD = 64
Q_LORA = 384
KV_LORA = 256
ROPE_THETA = 10000.0
MLA_SCALE = 1.0 / math.sqrt(QK_NOPE + QK_ROPE)
XA_SCALE = XA_DH ** -0.5

ADAM_LR = 0.001
ADAM_B1 = 0.9
ADAM_B2 = 0.999
ADAM_EPS = 1e-08
ADAM_WD = 0.01
ADAM_STEP = 10

NT = (((1,), (1,)), ((), ()))
TN = (((0,), (0,)), ((), ()))
MESH = pl.DeviceIdType.MESH


def _tile(n, target):
    if n <= target:
        return n
    best = None
    for t in range(LANES, target + 1, LANES):
        if n % t == 0:
            best = t
    assert best is not None, (n, target)
    return best


def _params(*sem):
    return pltpu.CompilerParams(dimension_semantics=sem)


def _sigmoid(v):
    return 0.5 * jnp.tanh(0.5 * v) + 0.5


def _rms_bwd(x, gain, dh):
    r = lax.rsqrt(jnp.mean(x * x, axis=-1, keepdims=True) + EPS)
    xhat = x * r
    dxhat = dh * gain
    dx = r * (dxhat - xhat * jnp.mean(dxhat * xhat, axis=-1, keepdims=True))
    return dx, dh * xhat


def _weight(w):
    if not isinstance(w, tuple):
        return w, w.shape, pl.BlockSpec
    arr, layer = w

    def spec(block, imap):
        return pl.BlockSpec((None,) + tuple(block), lambda *a: (layer,) + tuple(imap(*a)))

    return arr, arr.shape[1:], spec


def _nmm(x, g, w, *, name, out_dtype, tm=1024, tn_target=2048):
    M, K = x.shape
    w, (_, N), wspec = _weight(w)
    tm = min(tm, M)
    tn = _tile(N, tn_target)

    def body(x_ref, g_ref, w_ref, z_ref, h_ref):
        @pl.when(pl.program_id(1) == 0)
        def _():
            xf = x_ref[...]
            r = lax.rsqrt(jnp.mean(xf * xf, axis=-1, keepdims=True) + EPS)
            h_ref[...] = (xf * r * g_ref[...]).astype(h_ref.dtype)

        z_ref[...] = jnp.dot(h_ref[...], w_ref[...], preferred_element_type=F32).astype(z_ref.dtype)

    return pl.pallas_call(
        body, name=name, grid=(M // tm, N // tn),
        in_specs=[pl.BlockSpec((tm, K), lambda i, j: (i, 0)),
                  pl.BlockSpec((1, K), lambda i, j: (0, 0)),
                  wspec((K, tn), lambda i, j: (0, j))],
        out_specs=[pl.BlockSpec((tm, tn), lambda i, j: (i, j)),
                   pl.BlockSpec((tm, K), lambda i, j: (i, 0))],
        out_shape=[jax.ShapeDtypeStruct((M, N), out_dtype), jax.ShapeDtypeStruct((M, K), CD)],
        compiler_params=_params("parallel", "arbitrary"),
    )(x, g, w)


def _mm_res(a, w, res, *, name, tm=1024, tn_target=1024):
    M, K = a.shape
    w, (_, N), wspec = _weight(w)
    tm = min(tm, M)
    tn = _tile(N, tn_target)

    def body(a_ref, w_ref, r_ref, o_ref):
        o_ref[...] = r_ref[...] + jnp.dot(a_ref[...].astype(CD), w_ref[...], preferred_element_type=F32)

    return pl.pallas_call(
        body, name=name, grid=(M // tm, N // tn),
        in_specs=[pl.BlockSpec((tm, K), lambda i, j: (i, 0)),
                  wspec((K, tn), lambda i, j: (0, j)),
                  pl.BlockSpec((tm, tn), lambda i, j: (i, j))],
        out_specs=pl.BlockSpec((tm, tn), lambda i, j: (i, j)),
        out_shape=jax.ShapeDtypeStruct((M, N), F32),
        compiler_params=_params("parallel", "arbitrary"),
    )(a, w, res)


def _mm_nt(a, w, *, name, out_dtype, tm=1024, tn_target=1024):
    M, K = a.shape
    w, (N, _), wspec = _weight(w)
    tm = min(tm, M)
    tn = _tile(N, tn_target)

    def body(a_ref, w_ref, o_ref):
        o_ref[...] = lax.dot_general(a_ref[...].astype(CD), w_ref[...], NT,
                                     preferred_element_type=F32).astype(o_ref.dtype)

    return pl.pallas_call(
        body, name=name, grid=(M // tm, N // tn),
        in_specs=[pl.BlockSpec((tm, K), lambda i, j: (i, 0)),
                  wspec((tn, K), lambda i, j: (j, 0))],
        out_specs=pl.BlockSpec((tm, tn), lambda i, j: (i, j)),
        out_shape=jax.ShapeDtypeStruct((M, N), out_dtype),
        compiler_params=_params("parallel", "arbitrary"),
    )(a, w)


def _mm_nt_normbwd(gy, w, x, gain, dres, *, name, tm=1024, tk_target=1408):
    M, K = gy.shape
    w, (D, _), wspec = _weight(w)
    tm = min(tm, M)
    tk = _tile(K, tk_target)
    nk = K // tk

    def body(g_ref, w_ref, x_ref, gain_ref, dres_ref, dx_ref, dg_ref, acc):
        i, k = pl.program_id(0), pl.program_id(1)

        @pl.when(k == 0)
        def _():
            acc[...] = jnp.zeros_like(acc)

        acc[...] += lax.dot_general(g_ref[...].astype(CD), w_ref[...], NT, preferred_element_type=F32)

        @pl.when(k == nk - 1)
        def _():
            dx, dg_rows = _rms_bwd(x_ref[...], gain_ref[...], acc[...])
            dx_ref[...] = dres_ref[...] + dx

            @pl.when(i == 0)
            def _():
                dg_ref[...] = jnp.zeros_like(dg_ref)

            dg_ref[...] += jnp.sum(dg_rows, axis=0, keepdims=True)

    return pl.pallas_call(
        body, name=name, grid=(M // tm, nk),
        in_specs=[pl.BlockSpec((tm, tk), lambda i, k: (i, k)),
                  wspec((D, tk), lambda i, k: (0, k)),
                  pl.BlockSpec((tm, D), lambda i, k: (i, 0)),
                  pl.BlockSpec((1, D), lambda i, k: (0, 0)),
                  pl.BlockSpec((tm, D), lambda i, k: (i, 0))],
        out_specs=[pl.BlockSpec((tm, D), lambda i, k: (i, 0)),
                   pl.BlockSpec((1, D), lambda i, k: (0, 0))],
        out_shape=[jax.ShapeDtypeStruct((M, D), F32), jax.ShapeDtypeStruct((1, D), F32)],
        scratch_shapes=[pltpu.VMEM((tm, D), F32)],
        compiler_params=_params("arbitrary", "arbitrary"),
    )(gy, w, x, gain, dres)


def _mm_tn(a, g, *, name, tt=2048, tk_target=1024, tn_target=1024, into=None):
    T, K = a.shape
    N = g.shape[1]
    tt = min(tt, T)
    tk = _tile(K, tk_target)
    tn = _tile(N, tn_target)

    def body(a_ref, g_ref, *rest):
        o_ref = rest[-1]

        @pl.when(pl.program_id(2) == 0)
        def _():
            o_ref[...] = jnp.zeros_like(o_ref)

        o_ref[...] += lax.dot_general(a_ref[...].astype(CD), g_ref[...].astype(CD), TN,
                                      preferred_element_type=F32)

    in_specs = [pl.BlockSpec((tt, tk), lambda i, j, t: (t, i)),
                pl.BlockSpec((tt, tn), lambda i, j, t: (t, j))]
    if into is None:
        operands, aliases = (a, g), {}
        out_spec = pl.BlockSpec((tk, tn), lambda i, j, t: (i, j))
        out_shape = jax.ShapeDtypeStruct((K, N), F32)
    else:
        stack, layer = into
        operands, aliases = (a, g, stack), {2: 0}
        in_specs.append(pl.BlockSpec(memory_space=pl.ANY))
        out_spec = pl.BlockSpec((None, tk, tn), lambda i, j, t: (layer, i, j))
        out_shape = jax.ShapeDtypeStruct(stack.shape, F32)
    return pl.pallas_call(
        body, name=name, grid=(K // tk, N // tn, T // tt),
        in_specs=in_specs, out_specs=out_spec, out_shape=out_shape, input_output_aliases=aliases,
        compiler_params=_params("parallel", "parallel", "arbitrary"),
    )(*operands)


POOL_W = 512
CONV_W = 512
POOL_GROUP = 128


MIX_ROWS = 64
LN_ROWS = 256
LN_BWD_ROWS = 512
SUB = 8


def _shifted(sh_sc, x, n_rows):
    for b in range(1, SUB):
        sh_sc[b, pl.ds(0, n_rows), :] = x[b:b + n_rows]


def _tap(sh_sc, src, r0, cols, start, rows):
    a, b = divmod(start, SUB)
    if b == 0:
        return src[pl.ds(r0 + SUB * a, rows), cols]
    return sh_sc[b, pl.ds(SUB * a, rows), :]


def _pool_rows(zp_ref, z_ref, cols, win, i, tt, first, pooled_sc):
    RB = min(MIX_ROWS, tt)
    hb = 2 * SUB
    for r in range(tt // RB):
        if r == 0:
            p = zp_ref[pl.ds(HALO - hb, hb), cols]
            v = jnp.concatenate([jnp.where(first, jnp.zeros_like(p), p), z_ref[pl.ds(0, RB), cols]], axis=0)
        else:
            v = z_ref[pl.ds(r * RB - hb, RB + hb), cols]
        u = v[hb:hb + RB]
        s = u
        for j in range(1, win):
            s = s + v[hb - j:hb - j + RB]
        t_glob = i * tt + r * RB + lax.broadcasted_iota(jnp.int32, (RB, 1), 0)
        cnt = jnp.minimum(t_glob + 1, win).astype(F32)
        pooled_sc[pl.ds(r * RB, RB), :] = (s / cnt - u).astype(pooled_sc.dtype)


def _fill_gl(gl_sc, zp_ref, z_ref, zn_ref, tt, first, last):
    ca, cb = pl.ds(POOL_W, CONV_W), pl.ds(POOL_W + CONV_W, CONV_W)
    g = zp_ref[:, ca] * _sigmoid(zp_ref[:, cb])
    gl_sc[pl.ds(0, HALO), :] = jnp.where(first, jnp.zeros_like(g), g)

    def rows(r, carry):
        r0 = pl.multiple_of(r * LN_ROWS, LN_ROWS)
        gl_sc[pl.ds(HALO + r0, LN_ROWS), :] = z_ref[pl.ds(r0, LN_ROWS), ca] * _sigmoid(z_ref[pl.ds(r0, LN_ROWS), cb])
        return carry

    lax.fori_loop(0, tt // LN_ROWS, rows, 0)
    if zn_ref is not None:
        g = zn_ref[:, ca] * _sigmoid(zn_ref[:, cb])
        gl_sc[pl.ds(HALO + tt, HALO), :] = jnp.where(last, jnp.zeros_like(g), g)


def _conv_rows(gl_sc, cv_sc, sh_sc, w_ref, b_ref, n_rows):
    RB = min(MIX_ROWS, n_rows)
    for c in range(CONV_W // LANES):
        cols = pl.ds(c * LANES, LANES)
        bias = b_ref[:, cols]

        def chunk(r0, rb):
            g = gl_sc[pl.ds(r0, rb + HALO), cols]
            _shifted(sh_sc, g, rb + HALO - SUB)
            cv = jnp.zeros((rb, LANES), F32) + bias
            for j in range(CONV_K):
                cv = cv + w_ref[pl.ds(j, 1), cols] * _tap(sh_sc, gl_sc, r0, cols, HALO - (CONV_K - 1) + j, rb)
            cv_sc[pl.ds(r0, rb), cols] = cv

        def body(r, carry):
            chunk(pl.multiple_of(r * RB, RB), RB)
            return carry

        lax.fori_loop(0, n_rows // RB, body, 0)
        if n_rows % RB:
            chunk((n_rows // RB) * RB, n_rows % RB)


def _mixer_fwd(z, pool_w, pool_scale, dw_w, dw_b, ln_g, ln_b, *, name, tt=512):
    T, C = z.shape
    tt = min(tt, T)
    n = T // tt
    hb = tt // HALO

    def body(zp_ref, z_ref, pw_ref, ps_ref, w_ref, b_ref, g_ref, bb_ref, o_ref, pooled_sc, gl_sc, cv_sc, sh_sc):
        i = pl.program_id(0)
        first = i == 0
        for gi, win in enumerate(POOL_WINDOWS):
            cols = pl.ds(gi * POOL_GROUP, POOL_GROUP)
            _pool_rows(zp_ref, z_ref, cols, win, i, tt, first, pooled_sc)
            ya = jnp.dot(pooled_sc[...], pw_ref[gi].astype(CD), preferred_element_type=F32)
            o_ref[:, cols] = (ya * ps_ref[:, cols]).astype(o_ref.dtype)
        _fill_gl(gl_sc, zp_ref, z_ref, None, tt, first, None)
        _conv_rows(gl_sc, cv_sc, sh_sc, w_ref, b_ref, tt)

        def ln_rows(r, carry):
            rows = pl.ds(pl.multiple_of(r * LN_ROWS, LN_ROWS), LN_ROWS)
            cv = cv_sc[rows, :]
            xc = cv - jnp.mean(cv, axis=-1, keepdims=True)
            yn = xc * lax.rsqrt(jnp.mean(xc * xc, axis=-1, keepdims=True) + EPS) * g_ref[...] + bb_ref[...]
            o_ref[rows, pl.ds(POOL_W, CONV_W)] = (yn * _sigmoid(yn)).astype(o_ref.dtype)
            return carry

        lax.fori_loop(0, tt // LN_ROWS, ln_rows, 0, unroll=4)

    full = lambda shape: pl.BlockSpec(shape, lambda i: (0,) * len(shape))
    return pl.pallas_call(
        body, name=name, grid=(n,),
        in_specs=[pl.BlockSpec((HALO, C), lambda i: (jnp.maximum(i * hb - 1, 0), 0)),
                  pl.BlockSpec((tt, C), lambda i: (i, 0)),
                  full((4, POOL_GROUP, POOL_GROUP)), full((1, POOL_W)), full((CONV_K + 1, CONV_W)),
                  full((1, CONV_W)), full((1, CONV_W)), full((1, CONV_W))],
        out_specs=pl.BlockSpec((tt, POOL_W + CONV_W), lambda i: (i, 0)),
        out_shape=jax.ShapeDtypeStruct((T, POOL_W + CONV_W), CD),
        scratch_shapes=[pltpu.VMEM((tt, POOL_GROUP), CD), pltpu.VMEM((tt + HALO, CONV_W), F32),
                        pltpu.VMEM((tt, CONV_W), F32), pltpu.VMEM((SUB, MIX_ROWS + HALO, LANES), F32)],
        compiler_params=_params("parallel"),
    )(z, z, pool_w, pool_scale, dw_w, dw_b, ln_g, ln_b)


def _mixer_bwd(z, dy, pool_w, pool_scale, dw_w, dw_b, ln_g, ln_b, *, name, tt=512):
    T, C = z.shape
    tt = min(tt, T)
    n = T // tt
    hb = tt // HALO
    R = tt + HALO
    RB = min(MIX_ROWS, tt)

    def body(zp_ref, z_ref, zn_ref, dy_ref, dyn_ref, pw_ref, ps_ref, w_ref, b_ref, g_ref, bb_ref,
             dz_ref, dpw_ref, dps_ref, dw_ref, db_ref, dg_ref, dbb_ref,
             pooled_sc, dm_sc, dpool_sc, dpe_sc, gl_sc, cv_sc, accw, accl, sh_sc, shd_sc):
        i = pl.program_id(0)
        first, last = i == 0, i == n - 1

        @pl.when(first)
        def _():
            for r in (dpw_ref, dps_ref, dw_ref, db_ref, dg_ref, dbb_ref):
                r[...] = jnp.zeros_like(r)

        def dy_rows(cols):
            nxt = dyn_ref[:, cols]
            return jnp.concatenate([dy_ref[:, cols], jnp.where(last, jnp.zeros_like(nxt), nxt)], axis=0)

        t_all = i * tt + lax.broadcasted_iota(jnp.int32, (R, 1), 0)
        for gi, win in enumerate(POOL_WINDOWS):
            cols = pl.ds(gi * POOL_GROUP, POOL_GROUP)
            _pool_rows(zp_ref, z_ref, cols, win, i, tt, first, pooled_sc)
            pw = pw_ref[gi].astype(CD)
            dya = dy_rows(cols)
            mm = jnp.dot(pooled_sc[...], pw, preferred_element_type=F32)
            dps_ref[:, cols] += jnp.sum(dya[:tt] * mm, axis=0, keepdims=True)
            dm_sc[...] = (dya * ps_ref[:, cols]).astype(CD)
            dpw_ref[gi] += lax.dot_general(pooled_sc[...], dm_sc[pl.ds(0, tt), :], TN, preferred_element_type=F32)
            dpool = lax.dot_general(dm_sc[...], pw, NT, preferred_element_type=F32)
            dpool_sc[...] = dpool
            dpe_sc[...] = dpool / jnp.minimum(t_all + 1, win).astype(F32)

            def du_rows(r, carry):
                r0 = pl.multiple_of(r * RB, RB)
                e = dpe_sc[pl.ds(r0, RB + 2 * SUB), :]
                du = -dpool_sc[pl.ds(r0, RB), :]
                for j in range(win):
                    du = du + e[j:j + RB]
                dz_ref[pl.ds(r0, RB), cols] = du.astype(dz_ref.dtype)
                return carry

            lax.fori_loop(0, tt // RB, du_rows, 0)

        _fill_gl(gl_sc, zp_ref, z_ref, zn_ref, tt, first, last)
        _conv_rows(gl_sc, cv_sc, sh_sc, w_ref, b_ref, R)
        accl[...] = jnp.zeros_like(accl)

        def ln_rows(r0, nr, in_tile):
            rows = pl.ds(r0, nr)
            cv = cv_sc[rows, :]
            xc = cv - jnp.mean(cv, axis=-1, keepdims=True)
            rstd = lax.rsqrt(jnp.mean(xc * xc, axis=-1, keepdims=True) + EPS)
            xhat = xc * rstd
            yn = xhat * g_ref[...] + bb_ref[...]
            sy = _sigmoid(yn)
            if in_tile:
                dyv = dy_ref[rows, pl.ds(POOL_W, CONV_W)]
            else:
                nxt = dyn_ref[:, pl.ds(POOL_W, CONV_W)]
                dyv = jnp.where(last, jnp.zeros_like(nxt), nxt)
            dyn = dyv * (sy * (1.0 + yn * (1.0 - sy)))
            if in_tile:
                accl[pl.ds(0, SUB), :] += jnp.sum((dyn * xhat).reshape(nr // SUB, SUB, CONV_W), axis=0)
                accl[pl.ds(SUB, SUB), :] += jnp.sum(dyn.reshape(nr // SUB, SUB, CONV_W), axis=0)
            dxh = dyn * g_ref[...]
            dcv = rstd * (dxh - jnp.mean(dxh, axis=-1, keepdims=True)
                          - xhat * jnp.mean(dxh * xhat, axis=-1, keepdims=True))
            cv_sc[rows, :] = dcv
            if in_tile:
                accl[pl.ds(2 * SUB, SUB), :] += jnp.sum(dcv.reshape(nr // SUB, SUB, CONV_W), axis=0)

        lnb = min(LN_BWD_ROWS, tt)

        def ln_body(r, carry):
            ln_rows(pl.multiple_of(r * lnb, lnb), lnb, True)
            return carry

        lax.fori_loop(0, tt // lnb, ln_body, 0)
        ln_rows(tt, HALO, False)
        dg_ref[...] += jnp.sum(accl[pl.ds(0, SUB), :], axis=0, keepdims=True)
        dbb_ref[...] += jnp.sum(accl[pl.ds(SUB, SUB), :], axis=0, keepdims=True)
        db_ref[...] += jnp.sum(accl[pl.ds(2 * SUB, SUB), :], axis=0, keepdims=True)

        accw[...] = jnp.zeros_like(accw)
        for c in range(CONV_W // LANES):
            cols = pl.ds(c * LANES, LANES)

            def chunk(r, carry):
                r0 = pl.multiple_of(r * RB, RB)
                d = cv_sc[pl.ds(r0, RB + HALO), cols]
                g = gl_sc[pl.ds(r0, RB + HALO), cols]
                _shifted(shd_sc, d, RB + HALO - SUB)
                _shifted(sh_sc, g, RB + HALO - SUB)
                d_t = d[:RB]
                dgl = jnp.zeros((RB, LANES), F32)
                for j in range(CONV_K):
                    dgl = dgl + w_ref[pl.ds(j, 1), cols] * _tap(shd_sc, cv_sc, r0, cols, CONV_K - 1 - j, RB)
                    prod = d_t * _tap(sh_sc, gl_sc, r0, cols, HALO - (CONV_K - 1) + j, RB)
                    accw[pl.ds(SUB * j, SUB), cols] += jnp.sum(prod.reshape(RB // SUB, SUB, LANES), axis=0)
                a_t = z_ref[pl.ds(r0, RB), pl.ds(POOL_W + c * LANES, LANES)]
                sb = _sigmoid(z_ref[pl.ds(r0, RB), pl.ds(POOL_W + CONV_W + c * LANES, LANES)])
                dz_ref[pl.ds(r0, RB), pl.ds(POOL_W + c * LANES, LANES)] = (dgl * sb).astype(dz_ref.dtype)
                dz_ref[pl.ds(r0, RB), pl.ds(POOL_W + CONV_W + c * LANES, LANES)] = (
                    dgl * a_t * sb * (1.0 - sb)).astype(dz_ref.dtype)
                return carry

            lax.fori_loop(0, tt // RB, chunk, 0)
        for j in range(CONV_K):
            dw_ref[pl.ds(j, 1), :] += jnp.sum(accw[pl.ds(SUB * j, SUB), :], axis=0, keepdims=True)

    full = lambda shape: pl.BlockSpec(shape, lambda i: (0,) * len(shape))
    nb = T // HALO
    outs = pl.pallas_call(
        body, name=name, grid=(n,),
        in_specs=[pl.BlockSpec((HALO, C), lambda i: (jnp.maximum(i * hb - 1, 0), 0)),
                  pl.BlockSpec((tt, C), lambda i: (i, 0)),
                  pl.BlockSpec((HALO, C), lambda i: (jnp.minimum((i + 1) * hb, nb - 1), 0)),
                  pl.BlockSpec((tt, 2 * POOL_W), lambda i: (i, 0)),
                  pl.BlockSpec((HALO, 2 * POOL_W), lambda i: (jnp.minimum((i + 1) * hb, nb - 1), 0)),
                  full((4, POOL_GROUP, POOL_GROUP)), full((1, POOL_W)), full((CONV_K + 1, CONV_W)),
                  full((1, CONV_W)), full((1, CONV_W)), full((1, CONV_W))],
        out_specs=[pl.BlockSpec((tt, C), lambda i: (i, 0)),
                   full((4, POOL_GROUP, POOL_GROUP)), full((1, POOL_W)), full((CONV_K + 1, CONV_W)),
                   full((1, CONV_W)), full((1, CONV_W)), full((1, CONV_W))],
        out_shape=[jax.ShapeDtypeStruct((T, C), CD),
                   jax.ShapeDtypeStruct((4, POOL_GROUP, POOL_GROUP), F32),
                   jax.ShapeDtypeStruct((1, POOL_W), F32),
                   jax.ShapeDtypeStruct((CONV_K + 1, CONV_W), F32),
                   jax.ShapeDtypeStruct((1, CONV_W), F32),
                   jax.ShapeDtypeStruct((1, CONV_W), F32),
                   jax.ShapeDtypeStruct((1, CONV_W), F32)],
        scratch_shapes=[pltpu.VMEM((tt, POOL_GROUP), CD), pltpu.VMEM((R, POOL_GROUP), CD),
                        pltpu.VMEM((R, POOL_GROUP), F32), pltpu.VMEM((R, POOL_GROUP), F32),
                        pltpu.VMEM((tt + 2 * HALO, CONV_W), F32), pltpu.VMEM((R, CONV_W), F32),
                        pltpu.VMEM((SUB * (CONV_K + 1), CONV_W), F32), pltpu.VMEM((3 * SUB, CONV_W), F32),
                        pltpu.VMEM((SUB, MIX_ROWS + HALO, LANES), F32),
                        pltpu.VMEM((SUB, MIX_ROWS + HALO, LANES), F32)],
        compiler_params=_params("arbitrary"),
    )(z, z, z, dy, dy, pool_w, pool_scale, dw_w, dw_b, ln_g, ln_b)
    return outs


CHUNK_HALO = 16
FFN_ROWS = 64
FFN_LANES = 128


def _rows(cur, prev, nxt, r, rb, before, after, cols, n_r, first, last):
    lo, hi = r * rb - before, r * rb + rb + after
    tt = n_r * rb
    parts = []
    if lo < 0:
        p = prev[pl.ds(HALO + lo, -lo), cols]
        parts.append(jnp.where(first, jnp.zeros_like(p), p))
        lo = 0
    parts.append(cur[pl.ds(lo, min(hi, tt) - lo), cols])
    if hi > tt:
        p = nxt[pl.ds(0, hi - tt), cols]
        parts.append(jnp.where(last, jnp.zeros_like(p), p))
    return parts[0] if len(parts) == 1 else jnp.concatenate(parts, axis=0)


def _ffn_mid_fwd(up, cw, cb, *, name, tt=512):
    T = up.shape[0]
    tt = min(tt, T)
    n = T // tt
    hb = tt // HALO
    RB, CW, HB = min(FFN_ROWS, tt), FFN_LANES, CHUNK_HALO
    n_r = tt // RB

    def body(a_ref, gp_ref, g_ref, w_ref, b_ref, o_ref):
        first = pl.program_id(0) == 0

        def col_chunk(c, carry):
            cols = pl.ds(pl.multiple_of(c * CW, CW), CW)
            w = w_ref[:, cols]
            b = b_ref[:, cols]
            for r in range(n_r):
                v = _rows(g_ref, gp_ref, None, r, RB, HB, 0, cols, n_r, first, None).astype(F32)
                gc = b + w[0:1] * v[HB - 2:HB - 2 + RB] + w[1:2] * v[HB - 1:HB - 1 + RB] + w[2:3] * v[HB:HB + RB]
                a = a_ref[pl.ds(r * RB, RB), cols].astype(F32)
                o_ref[pl.ds(r * RB, RB), cols] = (gc * _sigmoid(gc) * a).astype(o_ref.dtype)
            return carry

        lax.fori_loop(0, D_FF // CW, col_chunk, 0)

    return pl.pallas_call(
        body, name=name, grid=(n,),
        in_specs=[pl.BlockSpec((tt, D_FF), lambda i: (i, 0)),
                  pl.BlockSpec((HALO, D_FF), lambda i: (jnp.maximum(i * hb - 1, 0), 1)),
                  pl.BlockSpec((tt, D_FF), lambda i: (i, 1)),
                  pl.BlockSpec((8, D_FF), lambda i: (0, 0)),
                  pl.BlockSpec((1, D_FF), lambda i: (0, 0))],
        out_specs=pl.BlockSpec((tt, D_FF), lambda i: (i, 0)),
        out_shape=jax.ShapeDtypeStruct((T, D_FF), CD),
        compiler_params=_params("parallel"),
    )(up, up, up, cw, cb)


def _ffn_mid_bwd(up, dact, cw, cb, *, name, tt=512):
    T = up.shape[0]
    tt = min(tt, T)
    n = T // tt
    hb = tt // HALO
    nb = T // HALO
    RB, CW, HB = min(FFN_ROWS, tt), FFN_LANES, CHUNK_HALO
    n_r = tt // RB
    RE = RB + 8

    def body(a_ref, an_ref, gp_ref, g_ref, gn_ref, d_ref, dn_ref, w_ref, b_ref, dup_ref, dw_ref, db_ref, acc):
        i = pl.program_id(0)
        first, last = i == 0, i == n - 1

        @pl.when(first)
        def _():
            dw_ref[...] = jnp.zeros_like(dw_ref)
            db_ref[...] = jnp.zeros_like(db_ref)

        def col_chunk(c, carry):
            cols = pl.ds(pl.multiple_of(c * CW, CW), CW)
            w = w_ref[:, cols]
            b = b_ref[:, cols]
            part = [jnp.zeros((8, CW), F32) for _ in range(FFN_K + 1)]
            for r in range(n_r):
                v = _rows(g_ref, gp_ref, gn_ref, r, RB, HB, HB, cols, n_r, first, last).astype(F32)
                gs = [v[HB - 2 + j:HB - 2 + j + RE] for j in range(FFN_K)]
                gc = b + w[0:1] * gs[0] + w[1:2] * gs[1] + w[2:3] * gs[2]
                sg = _sigmoid(gc)
                d = _rows(d_ref, None, dn_ref, r, RB, 0, HB, cols, n_r, None, last).astype(F32)[:RE]
                a = _rows(a_ref, None, an_ref, r, RB, 0, HB, cols, n_r, None, last).astype(F32)[:RE]
                silu = gc * sg
                dgc = d * a * (sg + silu - silu * sg)
                dup_ref[pl.ds(r * RB, RB), cols] = (d[:RB] * silu[:RB]).astype(dup_ref.dtype)
                dg = w[2:3] * dgc[0:RB] + w[1:2] * dgc[1:RB + 1] + w[0:1] * dgc[2:RB + 2]
                dup_ref[pl.ds(r * RB, RB), pl.ds(pl.multiple_of(D_FF + c * CW, CW), CW)] = dg.astype(dup_ref.dtype)
                dgc_t = dgc[:RB]
                for j in range(FFN_K):
                    part[j] = part[j] + jnp.sum((dgc_t * gs[j][:RB]).reshape(RB // 8, 8, CW), axis=0)
                part[FFN_K] = part[FFN_K] + jnp.sum(dgc_t.reshape(RB // 8, 8, CW), axis=0)
            for j in range(FFN_K + 1):
                acc[pl.ds(8 * j, 8), cols] = part[j]
            return carry

        lax.fori_loop(0, D_FF // CW, col_chunk, 0)
        for j in range(FFN_K):
            dw_ref[pl.ds(j, 1), :] += jnp.sum(acc[pl.ds(8 * j, 8), :], axis=0, keepdims=True)
        db_ref[...] += jnp.sum(acc[pl.ds(8 * FFN_K, 8), :], axis=0, keepdims=True)

    nxt = lambda i: jnp.minimum((i + 1) * hb, nb - 1)
    return pl.pallas_call(
        body, name=name, grid=(n,),
        in_specs=[pl.BlockSpec((tt, D_FF), lambda i: (i, 0)),
                  pl.BlockSpec((HALO, D_FF), lambda i: (nxt(i), 0)),
                  pl.BlockSpec((HALO, D_FF), lambda i: (jnp.maximum(i * hb - 1, 0), 1)),
                  pl.BlockSpec((tt, D_FF), lambda i: (i, 1)),
                  pl.BlockSpec((HALO, D_FF), lambda i: (nxt(i), 1)),
                  pl.BlockSpec((tt, D_FF), lambda i: (i, 0)),
                  pl.BlockSpec((HALO, D_FF), lambda i: (nxt(i), 0)),
                  pl.BlockSpec((8, D_FF), lambda i: (0, 0)),
                  pl.BlockSpec((1, D_FF), lambda i: (0, 0))],
        out_specs=[pl.BlockSpec((tt, 2 * D_FF), lambda i: (i, 0)),
                   pl.BlockSpec((8, D_FF), lambda i: (0, 0)),
                   pl.BlockSpec((1, D_FF), lambda i: (0, 0))],
        out_shape=[jax.ShapeDtypeStruct((T, 2 * D_FF), CD),
                   jax.ShapeDtypeStruct((8, D_FF), F32),
                   jax.ShapeDtypeStruct((1, D_FF), F32)],
        scratch_shapes=[pltpu.VMEM((8 * (FFN_K + 1), D_FF), F32)],
        compiler_params=_params("arbitrary"),
    )(up, up, up, up, up, dact, dact, cw, cb)


def _xattn_probs(q, k):
    s = lax.dot_general(q, k, NT, preferred_element_type=F32) * XA_SCALE
    p = jnp.exp(s - jnp.max(s, axis=-1, keepdims=True))
    return p / jnp.sum(p, axis=-1, keepdims=True)


def _xattn_fwd(q, kv, *, name, tq=1024):
    T = q.shape[0]
    tq = min(tq, T)

    def body(q_ref, kv_ref, o_ref):
        for h in range(XA_HEADS):
            cols = pl.ds(h * XA_DH, XA_DH)
            p = _xattn_probs(q_ref[:, cols], kv_ref[:, cols])
            v = kv_ref[:, pl.ds(D_MODEL + h * XA_DH, XA_DH)]
            o_ref[:, cols] = jnp.dot(p.astype(CD), v, preferred_element_type=F32).astype(o_ref.dtype)

    return pl.pallas_call(
        body, name=name, grid=(T // tq,),
        in_specs=[pl.BlockSpec((tq, D_MODEL), lambda i: (i, 0)),
                  pl.BlockSpec((MEM_LEN, 2 * D_MODEL), lambda i: (0, 0))],
        out_specs=pl.BlockSpec((tq, D_MODEL), lambda i: (i, 0)),
        out_shape=jax.ShapeDtypeStruct((T, D_MODEL), CD),
        compiler_params=_params("parallel"),
    )(q, kv)


def _xattn_bwd(q, kv, do, *, name, tq=1024):
    T = q.shape[0]
    tq = min(tq, T)

    def body(q_ref, kv_ref, do_ref, dq_ref, dkv_ref):
        @pl.when(pl.program_id(0) == 0)
        def _():
            dkv_ref[...] = jnp.zeros_like(dkv_ref)

        for h in range(XA_HEADS):
            cols = pl.ds(h * XA_DH, XA_DH)
            vcols = pl.ds(D_MODEL + h * XA_DH, XA_DH)
            qh, kh, vh, doh = q_ref[:, cols], kv_ref[:, cols], kv_ref[:, vcols], do_ref[:, cols]
            p = _xattn_probs(qh, kh)
            dkv_ref[:, vcols] += lax.dot_general(p.astype(CD), doh, TN, preferred_element_type=F32)
            dp = lax.dot_general(doh, vh, NT, preferred_element_type=F32)
            ds = (p * (dp - jnp.sum(dp * p, axis=-1, keepdims=True)) * XA_SCALE).astype(CD)
            dq_ref[:, cols] = jnp.dot(ds, kh, preferred_element_type=F32).astype(dq_ref.dtype)
            dkv_ref[:, cols] += lax.dot_general(ds, qh, TN, preferred_element_type=F32)

    return pl.pallas_call(
        body, name=name, grid=(T // tq,),
        in_specs=[pl.BlockSpec((tq, D_MODEL), lambda i: (i, 0)),
                  pl.BlockSpec((MEM_LEN, 2 * D_MODEL), lambda i: (0, 0)),
                  pl.BlockSpec((tq, D_MODEL), lambda i: (i, 0))],
        out_specs=[pl.BlockSpec((tq, D_MODEL), lambda i: (i, 0)),
                   pl.BlockSpec((MEM_LEN, 2 * D_MODEL), lambda i: (0, 0))],
        out_shape=[jax.ShapeDtypeStruct((T, D_MODEL), CD),
                   jax.ShapeDtypeStruct((MEM_LEN, 2 * D_MODEL), F32)],
        compiler_params=_params("arbitrary"),
    )(q, kv, do)


C_W = Q_LORA + KV_LORA + LANES


def _rot(x):
    lane = lax.broadcasted_iota(jnp.int32, x.shape, x.ndim - 1)
    up = pltpu.roll(x, LANES - QK_ROPE // 2, x.ndim - 1)
    dn = pltpu.roll(x, QK_ROPE // 2, x.ndim - 1)
    lo, mid, hi = QK_NOPE, QK_NOPE + QK_ROPE // 2, QK_NOPE + QK_ROPE
    return jnp.where((lane >= lo) & (lane < mid), -up, jnp.where((lane >= mid) & (lane < hi), dn, 0.0))


def _mla_mid_fwd(c, qg, kvg, cs, sn, *, name, tt=512):
    T = c.shape[0]
    tt = min(tt, T)

    def body(c_ref, qg_ref, kg_ref, cs_ref, sn_ref, qn_ref, kn_ref, kpe_ref):
        cq = c_ref[:, pl.ds(0, Q_LORA)]
        qn_ref[...] = (cq * lax.rsqrt(jnp.mean(cq * cq, axis=-1, keepdims=True) + EPS)
                       * qg_ref[...]).astype(qn_ref.dtype)
        ck = c_ref[:, pl.ds(Q_LORA, KV_LORA)]
        kn_ref[...] = (ck * lax.rsqrt(jnp.mean(ck * ck, axis=-1, keepdims=True) + EPS)
                       * kg_ref[...]).astype(kn_ref.dtype)
        kp = c_ref[:, pl.ds(Q_LORA + KV_LORA, LANES)]
        kpe_ref[...] = kp * cs_ref[...] + _rot(kp) * sn_ref[...]

    row = lambda w: pl.BlockSpec((tt, w), lambda i: (i, 0))
    one = lambda w: pl.BlockSpec((1, w), lambda i: (0, 0))
    return pl.pallas_call(
        body, name=name, grid=(T // tt,),
        in_specs=[row(C_W), one(Q_LORA), one(KV_LORA), row(LANES), row(LANES)],
        out_specs=[row(Q_LORA), row(KV_LORA), row(LANES)],
        out_shape=[jax.ShapeDtypeStruct((T, Q_LORA), CD), jax.ShapeDtypeStruct((T, KV_LORA), CD),
                   jax.ShapeDtypeStruct((T, LANES), F32)],
        compiler_params=_params("parallel"),
    )(c, qg, kvg, cs, sn)


def _mla_mid_bwd(c, dqn, dkvn, dksum, qg, kvg, cs, sn, *, name, tt=512):
    T = c.shape[0]
    tt = min(tt, T)

    def body(c_ref, dq_ref, dk_ref, ds_ref, qg_ref, kg_ref, cs_ref, sn_ref, dc_ref, dqg_ref, dkg_ref):
        @pl.when(pl.program_id(0) == 0)
        def _():
            dqg_ref[...] = jnp.zeros_like(dqg_ref)
            dkg_ref[...] = jnp.zeros_like(dkg_ref)

        dx, dg = _rms_bwd(c_ref[:, pl.ds(0, Q_LORA)], qg_ref[...], dq_ref[...])
        dc_ref[:, pl.ds(0, Q_LORA)] = dx.astype(dc_ref.dtype)
        dqg_ref[...] += jnp.sum(dg, axis=0, keepdims=True)
        dx, dg = _rms_bwd(c_ref[:, pl.ds(Q_LORA, KV_LORA)], kg_ref[...], dk_ref[...])
        dc_ref[:, pl.ds(Q_LORA, KV_LORA)] = dx.astype(dc_ref.dtype)
        dkg_ref[...] += jnp.sum(dg, axis=0, keepdims=True)
        d = ds_ref[...]
        lane = lax.broadcasted_iota(jnp.int32, d.shape, 1)
        dkp = d * cs_ref[...] - _rot(d * sn_ref[...])
        dc_ref[:, pl.ds(Q_LORA + KV_LORA, LANES)] = jnp.where(
            (lane >= QK_NOPE) & (lane < QK_NOPE + QK_ROPE), dkp, 0.0).astype(dc_ref.dtype)

    row = lambda w: pl.BlockSpec((tt, w), lambda i: (i, 0))
    one = lambda w: pl.BlockSpec((1, w), lambda i: (0, 0))
    return pl.pallas_call(
        body, name=name, grid=(T // tt,),
        in_specs=[row(C_W), row(Q_LORA), row(KV_LORA), row(LANES), one(Q_LORA), one(KV_LORA),
                  row(LANES), row(LANES)],
        out_specs=[row(C_W), one(Q_LORA), one(KV_LORA)],
        out_shape=[jax.ShapeDtypeStruct((T, C_W), CD), jax.ShapeDtypeStruct((1, Q_LORA), F32),
                   jax.ShapeDtypeStruct((1, KV_LORA), F32)],
        compiler_params=_params("arbitrary"),
    )(c, dqn, dkvn, dksum, qg, kvg, cs, sn)


def _mla_qkv_fwd(qn, kvn, kpe, cs, sn, wq, wk, wv, *, name, tt=512):
    T = qn.shape[0]
    tt = min(tt, T)
    H = MLA_HEADS

    def body(qn_ref, kn_ref, kpe_ref, cs_ref, sn_ref, wq_ref, wk_ref, wv_ref, q_ref, k_ref, v_ref):
        qn_v, kn_v, kpe_v, cs_v, sn_v = qn_ref[...], kn_ref[...], kpe_ref[...], cs_ref[...], sn_ref[...]
        for h in range(H):
            q = jnp.dot(qn_v, wq_ref[h], preferred_element_type=F32)
            q_ref[h] = (q * cs_v + _rot(q) * sn_v).astype(q_ref.dtype)
            k_ref[h] = (jnp.dot(kn_v, wk_ref[h], preferred_element_type=F32) + kpe_v).astype(k_ref.dtype)
            v_ref[h] = jnp.dot(kn_v, wv_ref[h], preferred_element_type=F32).astype(v_ref.dtype)

    row = lambda w: pl.BlockSpec((tt, w), lambda i: (i, 0))
    wsp = lambda k: pl.BlockSpec((H, k, LANES), lambda i: (0, 0, 0))
    hsp = pl.BlockSpec((H, tt, LANES), lambda i: (0, i, 0))
    sh = jax.ShapeDtypeStruct((H, T, LANES), CD)
    return pl.pallas_call(
        body, name=name, grid=(T // tt,),
        in_specs=[row(Q_LORA), row(KV_LORA), row(LANES), row(LANES), row(LANES),
                  wsp(Q_LORA), wsp(KV_LORA), wsp(KV_LORA)],
        out_specs=[hsp, hsp, hsp], out_shape=[sh, sh, sh],
        compiler_params=_params("parallel"),
    )(qn, kvn, kpe, cs, sn, wq, wk, wv)


def _mla_qkv_bwd(dq, dk, dv, qn, kvn, cs, sn, wq, wk, wv, *, name, tt=512):
    T = qn.shape[0]
    tt = min(tt, T)
    H = MLA_HEADS

    def body(dq_ref, dk_ref, dv_ref, qn_ref, kn_ref, cs_ref, sn_ref, wq_ref, wk_ref, wv_ref,
             dqn_ref, dkn_ref, dks_ref, dwq_ref, dwk_ref, dwv_ref):
        @pl.when(pl.program_id(0) == 0)
        def _():
            for r in (dwq_ref, dwk_ref, dwv_ref):
                r[...] = jnp.zeros_like(r)

        qn_v, kn_v, cs_v, sn_v = qn_ref[...], kn_ref[...], cs_ref[...], sn_ref[...]
        dqn = jnp.zeros((tt, Q_LORA), F32)
        dkn = jnp.zeros((tt, KV_LORA), F32)
        dks = jnp.zeros((tt, LANES), F32)
        for h in range(H):
            d = dq_ref[h]
            dqh = (d * cs_v - _rot(d * sn_v)).astype(CD)
            dkh, dvh = dk_ref[h], dv_ref[h]
            dqn = dqn + lax.dot_general(dqh, wq_ref[h], NT, preferred_element_type=F32)
            dkn = dkn + lax.dot_general(dkh, wk_ref[h], NT, preferred_element_type=F32)
            dkn = dkn + lax.dot_general(dvh, wv_ref[h], NT, preferred_element_type=F32)
            dks = dks + dkh.astype(F32)
            dwq_ref[h] += lax.dot_general(qn_v, dqh, TN, preferred_element_type=F32)
            dwk_ref[h] += lax.dot_general(kn_v, dkh, TN, preferred_element_type=F32)
            dwv_ref[h] += lax.dot_general(kn_v, dvh, TN, preferred_element_type=F32)
        dqn_ref[...] = dqn
        dkn_ref[...] = dkn
        dks_ref[...] = dks

    row = lambda w: pl.BlockSpec((tt, w), lambda i: (i, 0))
    wsp = lambda k: pl.BlockSpec((H, k, LANES), lambda i: (0, 0, 0))
    hsp = pl.BlockSpec((H, tt, LANES), lambda i: (0, i, 0))
    return pl.pallas_call(
        body, name=name, grid=(T // tt,),
        in_specs=[hsp, hsp, hsp, row(Q_LORA), row(KV_LORA), row(LANES), row(LANES),
                  wsp(Q_LORA), wsp(KV_LORA), wsp(KV_LORA)],
        out_specs=[row(Q_LORA), row(KV_LORA), row(LANES), wsp(Q_LORA), wsp(KV_LORA), wsp(KV_LORA)],
        out_shape=[jax.ShapeDtypeStruct((T, Q_LORA), F32), jax.ShapeDtypeStruct((T, KV_LORA), F32),
                   jax.ShapeDtypeStruct((T, LANES), F32),
                   jax.ShapeDtypeStruct((H, Q_LORA, LANES), F32),
                   jax.ShapeDtypeStruct((H, KV_LORA, LANES), F32),
                   jax.ShapeDtypeStruct((H, KV_LORA, LANES), F32)],
        compiler_params=_params("arbitrary"),
    )(dq, dk, dv, qn, kvn, cs, sn, wq, wk, wv)


FLASH_BLOCK = 1024
FLASH_ROWS = 128
EXP2_SCALE = MLA_SCALE * math.log2(math.e)


def _causal_steps(nq, by_key):
    pairs = [(i, j) for j in range(nq) for i in range(j, nq)] if by_key else \
            [(i, j) for i in range(nq) for j in range(i + 1)]
    return (jnp.asarray([p[0] for p in pairs], jnp.int32), jnp.asarray([p[1] for p in pairs], jnp.int32))


def _raw_scores(q, k, masked, first_row=0):
    s = lax.dot_general(q, k, NT, preferred_element_type=F32)
    if masked:
        row = lax.broadcasted_iota(jnp.int32, s.shape, 0) + first_row
        col = lax.broadcasted_iota(jnp.int32, s.shape, 1)
        s = jnp.where(col <= row, s, NEG)
    return s


def _flash_fwd(q, k, v, *, name):
    H, T, _ = q.shape
    tq = min(FLASH_BLOCK, T)
    nq = T // tq
    i_tab, j_tab = _causal_steps(nq, by_key=False)

    rb = min(FLASH_ROWS, tq)

    def body(i_tab, j_tab, q_ref, k_ref, v_ref, o_ref, lse_ref, m_sc, l_sc, acc, s_sc, p_sc):
        t = pl.program_id(1)
        i, j = i_tab[t], j_tab[t]

        @pl.when(j == 0)
        def _():
            m_sc[...] = jnp.full_like(m_sc, NEG)
            l_sc[...] = jnp.zeros_like(l_sc)
            acc[...] = jnp.zeros_like(acc)

        hb = tq // 2

        def step(masked):
            lane = lax.broadcasted_iota(jnp.int32, (tq, LANES), 1)
            top, bot = pl.ds(0, hb), pl.ds(hb, hb)
            alphas, pvs = [], []
            for h in range(2):
                if masked:
                    s_sc[h, top, top] = _raw_scores(q_ref[h, top, :], k_ref[h, top, :], True)
                    s_sc[h, bot, :] = _raw_scores(q_ref[h, bot, :], k_ref[h], True, first_row=hb)
                    m_cur = jnp.concatenate([jnp.max(s_sc[h, top, top], axis=-1, keepdims=True),
                                             jnp.max(s_sc[h, bot, :], axis=-1, keepdims=True)], axis=0)
                else:
                    s_sc[h] = _raw_scores(q_ref[h], k_ref[h], False)
                    m_cur = jnp.max(s_sc[h], axis=-1, keepdims=True)
                m_prev = m_sc[h]
                m_new = jnp.maximum(m_prev, m_cur)
                alpha = jnp.exp2((m_prev - m_new) * EXP2_SCALE)
                m_sc[h] = m_new
                for r in range(tq // rb):
                    rows = pl.ds(r * rb, rb)
                    m_r = m_sc[h, rows, :]
                    part = jnp.zeros((rb, LANES), F32)
                    keys = hb if masked and r * rb < hb else tq
                    for c in range(keys // LANES):
                        cols = pl.ds(c * LANES, LANES)
                        p = jnp.exp2((s_sc[h, rows, cols] - m_r) * EXP2_SCALE)
                        part = part + p
                        p_sc[h, rows, cols] = p.astype(CD)
                    l_sc[h, rows, :] = (alpha[r * rb:(r + 1) * rb] * l_sc[h, rows, :]
                                        + jnp.sum(part, axis=-1, keepdims=True))
                alphas.append(alpha)
                if masked:
                    pvs.append(jnp.concatenate(
                        [jnp.dot(p_sc[h, top, top], v_ref[h, top, :], preferred_element_type=F32),
                         jnp.dot(p_sc[h, bot, :], v_ref[h], preferred_element_type=F32)], axis=0))
                else:
                    pvs.append(jnp.dot(p_sc[h], v_ref[h], preferred_element_type=F32))
            acc[...] = acc[...] * jnp.where(lane < V_HEAD, alphas[0], alphas[1]) + pvs[0] + pvs[1]

        @pl.when(j < i)
        def _():
            step(False)

        @pl.when(j == i)
        def _():
            step(True)
            lane = lax.broadcasted_iota(jnp.int32, (tq, LANES), 1)
            o_ref[...] = (acc[...] / jnp.where(lane < V_HEAD, l_sc[0], l_sc[1])).astype(o_ref.dtype)
            for h in range(2):
                lse_ref[h] = m_sc[h] * EXP2_SCALE + jnp.log2(l_sc[h])

    qsp = pl.BlockSpec((2, tq, LANES), lambda p, t, it, jt: (p, it[t], 0))
    ksp = pl.BlockSpec((2, tq, LANES), lambda p, t, it, jt: (p, jt[t], 0))
    return pl.pallas_call(
        body, name=name,
        grid_spec=pltpu.PrefetchScalarGridSpec(
            num_scalar_prefetch=2, grid=(H // 2, int(i_tab.shape[0])),
            in_specs=[qsp, ksp, ksp],
            out_specs=[pl.BlockSpec((tq, LANES), lambda p, t, it, jt: (it[t], p)), qsp],
            scratch_shapes=[pltpu.VMEM((2, tq, LANES), F32), pltpu.VMEM((2, tq, LANES), F32),
                            pltpu.VMEM((tq, LANES), F32),
                            pltpu.VMEM((2, tq, tq), F32), pltpu.VMEM((2, tq, tq), CD)]),
        out_shape=[jax.ShapeDtypeStruct((T, H * V_HEAD), CD), jax.ShapeDtypeStruct((H, T, LANES), F32)],
        compiler_params=_params("parallel", "arbitrary"),
    )(i_tab, j_tab, q, k, v)


def _flash_delta(o, do, *, name, tt=512):
    T = o.shape[0]
    tt = min(tt, T)
    H = MLA_HEADS

    def body(o_ref, do_ref, dl_ref):
        lane = lax.broadcasted_iota(jnp.int32, (tt, LANES), 1)
        for p in range(H // 2):
            cols = pl.ds(p * LANES, LANES)
            prod = do_ref[:, cols].astype(F32) * o_ref[:, cols].astype(F32)
            d0 = jnp.sum(jnp.where(lane < V_HEAD, prod, 0.0), axis=-1, keepdims=True)
            d1 = jnp.sum(jnp.where(lane < V_HEAD, 0.0, prod), axis=-1, keepdims=True)
            dl_ref[2 * p] = jnp.broadcast_to(d0, (tt, LANES))
            dl_ref[2 * p + 1] = jnp.broadcast_to(d1, (tt, LANES))

    row = pl.BlockSpec((tt, H * V_HEAD), lambda i: (i, 0))
    return pl.pallas_call(
        body, name=name, grid=(T // tt,), in_specs=[row, row],
        out_specs=pl.BlockSpec((H, tt, LANES), lambda i: (0, i, 0)),
        out_shape=jax.ShapeDtypeStruct((H, T, LANES), F32),
        compiler_params=_params("parallel"),
    )(o, do)


def _flash_bwd(q, k, v, do, lse, delta, *, name):
    H, T, _ = q.shape
    tq = min(FLASH_BLOCK, T)
    nq = T // tq
    i_tab, j_tab = _causal_steps(nq, by_key=True)

    def body(i_tab, j_tab, q_ref, k_ref, v_ref, do_ref, lse_ref, dl_ref, dq_ref, dk_ref, dv_ref, dk_acc, dv_acc):
        t = pl.program_id(1)
        i, j = i_tab[t], j_tab[t]
        rows = pl.ds(pl.multiple_of(i * tq, tq), tq)

        @pl.when(t == 0)
        def _():
            dq_ref[...] = jnp.zeros_like(dq_ref)

        def block(h, qr, kr, first_row, masked):
            qh, kh, vh, do_v = q_ref[h, qr, :], k_ref[h, kr, :], v_ref[h, kr, :], do_ref[qr, :]
            s = _raw_scores(qh, kh, masked, first_row)
            p = jnp.exp2(s * EXP2_SCALE - lse_ref[h, qr, :][:, :1])
            dv_acc[h, kr, :] += lax.dot_general(p.astype(CD), do_v, TN, preferred_element_type=F32)
            dp = lax.dot_general(do_v, vh, NT, preferred_element_type=F32)
            ds = (p * (dp - dl_ref[h, qr, :][:, :1]) * MLA_SCALE).astype(CD)
            dk_acc[h, kr, :] += lax.dot_general(ds, qh, TN, preferred_element_type=F32)
            dq_rows = pl.ds(pl.multiple_of(i * tq + qr.start, qr.size), qr.size)
            dq_ref[h, dq_rows, :] += jnp.dot(ds, kh, preferred_element_type=F32)

        def step(masked):
            hb = tq // 2
            for h in range(2):
                if masked:
                    block(h, pl.ds(0, hb), pl.ds(0, hb), 0, True)
                    block(h, pl.ds(hb, hb), pl.ds(0, tq), hb, True)
                else:
                    block(h, pl.ds(0, tq), pl.ds(0, tq), 0, False)

        @pl.when(i == j)
        def _():
            dk_acc[...] = jnp.zeros_like(dk_acc)
            dv_acc[...] = jnp.zeros_like(dv_acc)
            step(True)

        @pl.when(i > j)
        def _():
            step(False)

        @pl.when(i == nq - 1)
        def _():
            lane = lax.broadcasted_iota(jnp.int32, (tq, LANES), 1)
            dk_ref[...] = dk_acc[...].astype(dk_ref.dtype)
            dv_ref[0] = jnp.where(lane < V_HEAD, dv_acc[0], 0.0).astype(dv_ref.dtype)
            dv_ref[1] = jnp.where(lane < V_HEAD, 0.0, dv_acc[1]).astype(dv_ref.dtype)

    qsp = pl.BlockSpec((2, tq, LANES), lambda p, t, it, jt: (p, it[t], 0))
    ksp = pl.BlockSpec((2, tq, LANES), lambda p, t, it, jt: (p, jt[t], 0))
    osp = pl.BlockSpec((tq, LANES), lambda p, t, it, jt: (it[t], p))
    sh = jax.ShapeDtypeStruct((H, T, LANES), CD)
    return pl.pallas_call(
        body, name=name,
        grid_spec=pltpu.PrefetchScalarGridSpec(
            num_scalar_prefetch=2, grid=(H // 2, int(i_tab.shape[0])),
            in_specs=[qsp, ksp, ksp, osp, qsp, qsp],
            out_specs=[pl.BlockSpec((2, T, LANES), lambda p, t, it, jt: (p, 0, 0)), ksp, ksp],
            scratch_shapes=[pltpu.VMEM((2, tq, LANES), F32), pltpu.VMEM((2, tq, LANES), F32)]),
        out_shape=[jax.ShapeDtypeStruct((H, T, LANES), F32), sh, sh],
        compiler_params=_params("parallel", "arbitrary"),
    )(i_tab, j_tab, q, k, v, do, lse, delta)


def _loss_head(x, g, target, *, name, tt=1024):
    T, D = x.shape
    tt = min(tt, T)

    def body(x_ref, g_ref, t_ref, dx_ref, dg_ref, loss_ref):
        @pl.when(pl.program_id(0) == 0)
        def _():
            dg_ref[...] = jnp.zeros_like(dg_ref)
            loss_ref[...] = jnp.zeros_like(loss_ref)

        xv, gv = x_ref[...], g_ref[...]
        r = lax.rsqrt(jnp.mean(xv * xv, axis=-1, keepdims=True) + EPS)
        err = xv * r * gv - t_ref[...]
        tok = jnp.mean(err * err, axis=-1, keepdims=True)
        loss_ref[...] += 0.5 * jnp.sum(tok, axis=0, keepdims=True)
        dx, dg_rows = _rms_bwd(xv, gv, err * (1.0 / D))
        dx_ref[...] = dx
        dg_ref[...] += jnp.sum(dg_rows, axis=0, keepdims=True)

    return pl.pallas_call(
        body, name=name, grid=(T // tt,),
        in_specs=[pl.BlockSpec((tt, D), lambda i: (i, 0)), pl.BlockSpec((1, D), lambda i: (0, 0)),
                  pl.BlockSpec((tt, D), lambda i: (i, 0))],
        out_specs=[pl.BlockSpec((tt, D), lambda i: (i, 0)), pl.BlockSpec((1, D), lambda i: (0, 0)),
                   pl.BlockSpec((1, LANES), lambda i: (0, 0))],
        out_shape=[jax.ShapeDtypeStruct((T, D), F32), jax.ShapeDtypeStruct((1, D), F32),
                   jax.ShapeDtypeStruct((1, LANES), F32)],
        compiler_params=_params("arbitrary"),
    )(x, g, target)


def _rope_tables(positions):
    inv = 1.0 / (ROPE_THETA ** (jnp.arange(0, QK_ROPE, 2, dtype=F32) / QK_ROPE))
    ang = positions.astype(F32)[:, None] * inv
    c, s = jnp.cos(ang), jnp.sin(ang)
    T = positions.shape[0]
    cs = jnp.concatenate([jnp.ones((T, QK_NOPE), F32), c, c, jnp.zeros((T, LANES - QK_NOPE - QK_ROPE), F32)], 1)
    sn = jnp.concatenate([jnp.zeros((T, QK_NOPE), F32), s, s, jnp.zeros((T, LANES - QK_NOPE - QK_ROPE), F32)], 1)
    return cs, sn


def _pad_rows(w, rows):
    return jnp.concatenate([w, jnp.zeros((rows - w.shape[0],) + w.shape[1:], w.dtype)], 0)


def _mla_weights(w_dq_dkv, w_uq, w_ukv):
    K = w_dq_dkv.shape[0]
    z = lambda n: jnp.zeros((K, n), w_dq_dkv.dtype)
    wc = jnp.concatenate([w_dq_dkv[:, :Q_LORA + KV_LORA], z(QK_NOPE), w_dq_dkv[:, Q_LORA + KV_LORA:],
                          z(LANES - QK_NOPE - QK_ROPE)], 1)
    wq = w_uq.reshape(Q_LORA, MLA_HEADS, QK_NOPE + QK_ROPE).transpose(1, 0, 2)
    wq = jnp.concatenate([wq, jnp.zeros((MLA_HEADS, Q_LORA, LANES - QK_NOPE - QK_ROPE), wq.dtype)], 2)
    wkv = w_ukv.reshape(KV_LORA, MLA_HEADS, QK_NOPE + V_HEAD).transpose(1, 0, 2)
    zero = jnp.zeros_like(wkv[:, :, :QK_NOPE])
    wk = jnp.concatenate([wkv[:, :, :QK_NOPE], zero], 2)
    wv_lo = jnp.concatenate([wkv[:, :, QK_NOPE:], zero], 2)
    wv_hi = jnp.concatenate([zero, wkv[:, :, QK_NOPE:]], 2)
    odd = (jnp.arange(MLA_HEADS) % 2 == 1)[:, None, None]
    wv = jnp.where(odd, wv_hi, wv_lo)
    return wc, wq, wk, wv


def _mla_weight_grads(dwc, dwq, dwk, dwv):
    d_dq = jnp.concatenate([dwc[:, :Q_LORA + KV_LORA],
                            dwc[:, Q_LORA + KV_LORA + QK_NOPE:Q_LORA + KV_LORA + QK_NOPE + QK_ROPE]], 1)
    d_uq = dwq[:, :, :QK_NOPE + QK_ROPE].transpose(1, 0, 2).reshape(Q_LORA, MLA_HEADS * (QK_NOPE + QK_ROPE))
    odd = (jnp.arange(MLA_HEADS) % 2 == 1)[:, None, None]
    dv = jnp.where(odd, dwv[:, :, V_HEAD:], dwv[:, :, :V_HEAD])
    d_ukv = jnp.concatenate([dwk[:, :, :QK_NOPE], dv], 2).transpose(1, 0, 2).reshape(
        KV_LORA, MLA_HEADS * (QK_NOPE + V_HEAD))
    return d_dq, d_uq, d_ukv


def _local_step(x, mem, positions, target, W):
    G = {}
    row = lambda v: v.reshape(1, -1)
    cs, sn = _rope_tables(positions)
    saved = []
    for l in range(DEPTH):
        L = f"l{l}"
        s = {"x0": x}
        if l % 2 == 0:
            e = l // 2
            s["z"], s["h"] = _nmm(x, row(W["norm_mix_g"][l]), (W["pc_w_in"], e), name=f"{L}_mix_in", out_dtype=F32)
            s["dw_w"] = _pad_rows(W["conv_dw_w"][e], CONV_K + 1)
            s["mix_p"] = (W["pool_w"][e], row(W["pool_scale"][e]), s["dw_w"], row(W["conv_dw_b"][e]),
                          row(W["conv_ln_g"][e]), row(W["conv_ln_b"][e]))
            s["ycat"] = _mixer_fwd(s["z"], *s["mix_p"], name=f"{L}_mix_mid")
            x = _mm_res(s["ycat"], (W["pc_w_out"], e), x, name=f"{L}_mix_out")
        else:
            o = l // 2
            wc, wq, wk, wv = _mla_weights(W["mla_w_dq_dkv"][o], W["mla_w_uq"][o], W["mla_w_ukv"][o])
            s["mla_w"] = (wc, wq, wk, wv)
            s["c"], s["h"] = _nmm(x, row(W["norm_mix_g"][l]), wc, name=f"{L}_mla_down", out_dtype=F32)
            s["qg"], s["kvg"] = row(W["mla_q_norm_g"][o]), row(W["mla_kv_norm_g"][o])
            s["qn"], s["kvn"], kpe = _mla_mid_fwd(s["c"], s["qg"], s["kvg"], cs, sn, name=f"{L}_mla_mid")
            s["q"], s["k"], s["v"] = _mla_qkv_fwd(s["qn"], s["kvn"], kpe, cs, sn, wq, wk, wv, name=f"{L}_mla_qkv")
            s["o"], s["lse"] = _flash_fwd(s["q"], s["k"], s["v"], name=f"{L}_mla_attn")
            x = _mm_res(s["o"], (W["mla_w_o"], o), x, name=f"{L}_mla_out")
        s["x1"] = x
        s["xq"], s["hx"] = _nmm(x, row(W["norm_xa_g"][l]), (W["xa_wq"], l), name=f"{L}_xa_q", out_dtype=CD)
        s["xkv"], s["hm"] = _nmm(mem, row(W["norm_mem_g"][l]), (W["xa_wkv"], l), name=f"{L}_xa_kv", out_dtype=CD)
        s["xo"] = _xattn_fwd(s["xq"], s["xkv"], name=f"{L}_xa_attn")
        x = _mm_res(s["xo"], (W["xa_wo"], l), x, name=f"{L}_xa_out")
        s["x2"] = x
        s["up"], s["hf"] = _nmm(x, row(W["norm_ffn_g"][l]), (W["ffn_w_up"], l), name=f"{L}_ffn_up", out_dtype=CD,
                                tn_target=2816)
        s["cw"], s["cb"] = _pad_rows(W["ffn_conv_w"][l], 8), row(W["ffn_conv_b"][l])
        s["act"] = _ffn_mid_fwd(s["up"], s["cw"], s["cb"], name=f"{L}_ffn_mid")
        x = _mm_res(s["act"], (W["ffn_w_down"], l), x, name=f"{L}_ffn_down")
        saved.append(s)
    dx, G["final_norm_g"], loss = _loss_head(x, row(W["final_norm_g"]), target, name="loss_head")
    G["final_norm_g"] = G["final_norm_g"].reshape(-1)

    per_layer = {}

    def put(name, l, val):
        per_layer.setdefault(name, {})[l] = val

    def put_dw(weight, l, a, g, **kw):
        stack = G[weight] if weight in G else lax.empty(W[weight].shape, F32)
        G[weight] = _mm_tn(a, g, into=(stack, l), **kw)

    for l in reversed(range(DEPTH)):
        L = f"l{l}"
        s = saved[l]
        put_dw("ffn_w_down", l, s["act"], dx, name=f"{L}_ffn_down_dw", tk_target=1408)
        dact = _mm_nt(dx, (W["ffn_w_down"], l), name=f"{L}_ffn_down_dx", out_dtype=CD, tn_target=2816)
        dup, dcw, dcb = _ffn_mid_bwd(s["up"], dact, s["cw"], s["cb"], name=f"{L}_ffn_mid_bwd")
        put("ffn_conv_w", l, dcw[:FFN_K])
        put("ffn_conv_b", l, dcb[0])
        put_dw("ffn_w_up", l, s["hf"], dup, name=f"{L}_ffn_up_dw", tn_target=1408)
        dx, dg = _mm_nt_normbwd(dup, (W["ffn_w_up"], l), s["x2"], row(W["norm_ffn_g"][l]), dx, name=f"{L}_ffn_up_dx")
        put("norm_ffn_g", l, dg[0])
        put_dw("xa_wo", l, s["xo"], dx, name=f"{L}_xa_out_dw")
        do = _mm_nt(dx, (W["xa_wo"], l), name=f"{L}_xa_out_dx", out_dtype=CD)
        dq, dkv = _xattn_bwd(s["xq"], s["xkv"], do, name=f"{L}_xa_attn_bwd")
        put_dw("xa_wq", l, s["hx"], dq, name=f"{L}_xa_q_dw")
        dx, dg = _mm_nt_normbwd(dq, (W["xa_wq"], l), s["x1"], row(W["norm_xa_g"][l]), dx, name=f"{L}_xa_q_dx")
        put("norm_xa_g", l, dg[0])
        put_dw("xa_wkv", l, s["hm"], dkv, name=f"{L}_xa_kv_dw", tt=MEM_LEN)
        _, dg = _mm_nt_normbwd(dkv, (W["xa_wkv"], l), mem, row(W["norm_mem_g"][l]), jnp.zeros_like(mem),
                               name=f"{L}_xa_kv_dx", tm=MEM_LEN)
        put("norm_mem_g", l, dg[0])
        if l % 2 == 0:
            e = l // 2
            put_dw("pc_w_out", e, s["ycat"], dx, name=f"{L}_mix_out_dw")
            dy = _mm_nt(dx, (W["pc_w_out"], e), name=f"{L}_mix_out_dx", out_dtype=F32)
            dz, dpw, dps, ddw, ddb, dlg, dlb = _mixer_bwd(s["z"], dy, *s["mix_p"], name=f"{L}_mix_mid_bwd")
            put("pool_w", e, dpw)
            put("pool_scale", e, dps[0])
            put("conv_dw_w", e, ddw[:CONV_K])
            put("conv_dw_b", e, ddb[0])
            put("conv_ln_g", e, dlg[0])
            put("conv_ln_b", e, dlb[0])
            put_dw("pc_w_in", e, s["h"], dz, name=f"{L}_mix_in_dw")
            dx, dg = _mm_nt_normbwd(dz, (W["pc_w_in"], e), s["x0"], row(W["norm_mix_g"][l]), dx, name=f"{L}_mix_in_dx")
        else:
            o = l // 2
            wc, wq, wk, wv = s["mla_w"]
            put_dw("mla_w_o", o, s["o"], dx, name=f"{L}_mla_out_dw")
            do = _mm_nt(dx, (W["mla_w_o"], o), name=f"{L}_mla_out_dx", out_dtype=CD)
            delta = _flash_delta(s["o"], do, name=f"{L}_mla_attn_delta")
            dq, dk, dv = _flash_bwd(s["q"], s["k"], s["v"], do, s["lse"], delta, name=f"{L}_mla_attn_bwd")
            dqn, dkvn, dks, dwq, dwk, dwv = _mla_qkv_bwd(dq, dk, dv, s["qn"], s["kvn"], cs, sn, wq, wk, wv,
                                                         name=f"{L}_mla_qkv_bwd")
            dc, dqg, dkg = _mla_mid_bwd(s["c"], dqn, dkvn, dks, s["qg"], s["kvg"], cs, sn, name=f"{L}_mla_mid_bwd")
            put("mla_q_norm_g", o, dqg[0])
            put("mla_kv_norm_g", o, dkg[0])
            dwc = _mm_tn(s["h"], dc, name=f"{L}_mla_down_dw")
            d_dq, d_uq, d_ukv = _mla_weight_grads(dwc, dwq, dwk, dwv)
            put("mla_w_dq_dkv", o, d_dq)
            put("mla_w_uq", o, d_uq)
            put("mla_w_ukv", o, d_ukv)
            dx, dg = _mm_nt_normbwd(dc, wc, s["x0"], row(W["norm_mix_g"][l]), dx, name=f"{L}_mla_down_dx",
                                    tk_target=768)
        put("norm_mix_g", l, dg[0])
    for name, d in per_layer.items():
        G[name] = jnp.stack([d[i] for i in sorted(d)], 0)
    return loss, dx, G


_ANY = pl.BlockSpec(memory_space=pl.ANY)


def _all_gather(xs, *, name):
    n = len(xs)

    def body(*refs):
        x_refs, out_refs = refs[:n], refs[n:2 * n]
        send_sems, recv_sems, local_sems = refs[2 * n:]
        mx, my, mc = lax.axis_index("x"), lax.axis_index("y"), lax.axis_index("c")
        me, sibling = (mx, my, mc), (mx, my, 1 - mc)
        xn, yn, dg = (1 - mx, my), (mx, 1 - my), (1 - mx, 1 - my)
        src = (mx + (1 - mc) * (1 - 2 * mx), my + mc * (1 - 2 * my))
        dst = (mx + mc * (1 - 2 * mx), my + (1 - mc) * (1 - 2 * my))
        SIB, XN, YN, DG, PASS = 0, 1, 2, 3, 4

        def copy(a, k, block, to, own=False):
            px, py, pc = block
            slot = out_refs[a].at[4 * px + 2 * py + pc]
            return pltpu.make_async_remote_copy(
                src_ref=x_refs[a] if own else slot, dst_ref=slot,
                send_sem=send_sems.at[7 * a + k], recv_sem=recv_sems.at[7 * a + k],
                device_id=to, device_id_type=MESH)

        mine = [pltpu.make_async_copy(x_refs[a], out_refs[a].at[4 * mx + 2 * my + mc], local_sems.at[a])
                for a in range(n)]
        for cp in mine:
            cp.start()
        sent = [copy(a, XN, me, (*xn, mc), own=True) for a in range(n)]
        sent += [copy(a, YN, me, (*yn, mc), own=True) for a in range(n)]
        sent += [copy(a, SIB, me, sibling, own=True) for a in range(n)]
        for cp in sent:
            cp.start()
        for a in range(n):
            for k, chip in ((XN, xn), (YN, yn)):
                copy(a, k, (*chip, mc), me).wait_recv()
                sent.append(copy(a, PASS + k - 1, (*chip, mc), sibling))
                sent[-1].start()
            sent.append(copy(a, DG, (*src, mc), (*dst, mc)))
            sent[-1].start()
        for a in range(n):
            copy(a, DG, (*dg, mc), me).wait_recv()
            sent.append(copy(a, PASS + DG - 1, (*dg, mc), sibling))
            sent[-1].start()
        for a in range(n):
            copy(a, SIB, sibling, me).wait_recv()
            for k, chip in ((XN, xn), (YN, yn), (DG, dg)):
                copy(a, PASS + k - 1, (*chip, 1 - mc), me).wait_recv()
        for cp in sent:
            cp.wait_send()
        for cp in mine:
            cp.wait()

    return pl.pallas_call(
        body, name=name, in_specs=[_ANY] * n, out_specs=[_ANY] * n,
        out_shape=[jax.ShapeDtypeStruct((N_DEV,) + x.shape, x.dtype) for x in xs],
        scratch_shapes=[pltpu.SemaphoreType.DMA((7 * n,)), pltpu.SemaphoreType.DMA((7 * n,)),
                        pltpu.SemaphoreType.DMA((n,))],
    )(*xs)


N_CHIP = 4


def _pair_exchange(ps, *, name):
    n = len(ps)

    def body(*refs):
        p_refs, out_refs = refs[:n], refs[n:2 * n]
        send_sems, recv_sems = refs[2 * n:]
        mx, my, mc = lax.axis_index("x"), lax.axis_index("y"), lax.axis_index("c")
        copies = []
        for a in range(n):
            for chip in range(N_CHIP):
                copies.append(pltpu.make_async_remote_copy(
                    src_ref=p_refs[a].at[2 * chip + (1 - mc)], dst_ref=out_refs[a].at[chip],
                    send_sem=send_sems.at[N_CHIP * a + chip], recv_sem=recv_sems.at[N_CHIP * a + chip],
                    device_id=(mx, my, 1 - mc), device_id_type=MESH))
        for cp in copies:
            cp.start()
        for cp in copies:
            cp.wait()

    return pl.pallas_call(
        body, name=name, in_specs=[_ANY] * n, out_specs=[_ANY] * n,
        out_shape=[jax.ShapeDtypeStruct((N_CHIP,) + p.shape[1:], p.dtype) for p in ps],
        scratch_shapes=[pltpu.SemaphoreType.DMA((N_CHIP * n,)), pltpu.SemaphoreType.DMA((N_CHIP * n,))],
    )(*ps)


def _pair_sum(p, recv, core, *, name):
    _, R, C = p.shape
    tr = _row_tile(R, C, 4 * ROW_TILE_ELEMS)
    p4 = p.reshape(N_CHIP, 2, R, C)

    def body(core_ref, a_ref, b_ref, o_ref):
        o_ref[...] = (a_ref[...].astype(F32) + b_ref[...].astype(F32)).astype(o_ref.dtype)

    return pl.pallas_call(
        body, name=name,
        grid_spec=pltpu.PrefetchScalarGridSpec(
            num_scalar_prefetch=1, grid=(N_CHIP, R // tr),
            in_specs=[pl.BlockSpec((None, None, tr, C), lambda ch, i, core: (ch, core[0], i, 0)),
                      pl.BlockSpec((None, tr, C), lambda ch, i, core: (ch, i, 0))],
            out_specs=pl.BlockSpec((None, tr, C), lambda ch, i, core: (ch, i, 0))),
        out_shape=jax.ShapeDtypeStruct((N_CHIP, R, C), p.dtype),
        compiler_params=_params("parallel", "parallel"),
    )(core, p4, recv)


def _chip_exchange(ss, *, name):
    n = len(ss)

    def body(*refs):
        s_refs, out_refs, stage_refs = refs[:n], refs[n:2 * n], refs[2 * n:3 * n]
        send_sems, recv_sems, local_sems = refs[3 * n:]
        mx, my, mc = lax.axis_index("x"), lax.axis_index("y"), lax.axis_index("c")
        chip = 2 * mx + my
        xn, yn, dg = (1 - mx, my), (mx, 1 - my), (1 - mx, 1 - my)
        via = (mx + (1 - mc) * (1 - 2 * mx), my + mc * (1 - 2 * my))
        onward = (mx + mc * (1 - 2 * mx), my + (1 - mc) * (1 - 2 * my))
        XN, YN, STAGE, ONWARD = 0, 1, 2, 3

        def copy(a, k, src, dst, to):
            return pltpu.make_async_remote_copy(
                src_ref=src, dst_ref=dst, send_sem=send_sems.at[4 * a + k], recv_sem=recv_sems.at[4 * a + k],
                device_id=(*to, mc), device_id_type=MESH)

        def slot(ref, c):
            return ref.at[2 * c[0] + c[1]]

        mine = [pltpu.make_async_copy(s_refs[a].at[chip], out_refs[a].at[chip], local_sems.at[a]) for a in range(n)]
        for cp in mine:
            cp.start()
        first = []
        for a in range(n):
            first.append(copy(a, STAGE, slot(s_refs[a], dg), stage_refs[a], via))
            first.append(copy(a, XN, slot(s_refs[a], xn), out_refs[a].at[chip], xn))
            first.append(copy(a, YN, slot(s_refs[a], yn), out_refs[a].at[chip], yn))
        for cp in first:
            cp.start()
        onwards = []
        for a in range(n):
            first[3 * a].wait_recv()
            onwards.append(copy(a, ONWARD, stage_refs[a], slot(out_refs[a], via), onward))
            onwards[-1].start()
        for a in range(n):
            first[3 * a + 1].wait_recv()
            first[3 * a + 2].wait_recv()
            onwards[a].wait_recv()
        for cp in first + onwards:
            cp.wait_send()
        for cp in mine:
            cp.wait()

    outs = pl.pallas_call(
        body, name=name, in_specs=[_ANY] * n, out_specs=[_ANY] * (2 * n),
        out_shape=[jax.ShapeDtypeStruct(s.shape, s.dtype) for s in ss]
                  + [jax.ShapeDtypeStruct(s.shape[1:], s.dtype) for s in ss],
        scratch_shapes=[pltpu.SemaphoreType.DMA((4 * n,)), pltpu.SemaphoreType.DMA((4 * n,)),
                        pltpu.SemaphoreType.DMA((n,))],
    )(*ss)
    return outs[:n]


ROW_TILE_ELEMS = 256 * 1024


def _row_tile(R, C, elems=None):
    elems = ROW_TILE_ELEMS if elems is None else elems
    for t in (4096, 2048, 1024, 512, 256, 128, 64, 32, 16):
        if R % t == 0 and t * C <= elems:
            return t
    raise ValueError((R, C))


def _sum_slots(gs, *, name):
    S, R, C = gs.shape
    tr = _row_tile(R, C)

    def body(g_ref, o_ref):
        g = g_ref[0].astype(F32)
        for s in range(1, S):
            g = g + g_ref[s].astype(F32)
        o_ref[...] = g

    return pl.pallas_call(
        body, name=name, grid=(R // tr,),
        in_specs=[pl.BlockSpec((S, tr, C), lambda i: (0, i, 0))],
        out_specs=pl.BlockSpec((tr, C), lambda i: (i, 0)),
        out_shape=jax.ShapeDtypeStruct((R, C), F32),
        compiler_params=_params("parallel"),
    )(gs)


def _adamw(gs, w, m, v, *, name):
    S, R, C = gs.shape
    tr = _row_tile(R, C, 2 * ROW_TILE_ELEMS)

    def body(g_ref, w_ref, m_ref, v_ref, g_out, d_out, m_out, v_out):
        g = g_ref[0].astype(F32)
        for s in range(1, S):
            g = g + g_ref[s].astype(F32)
        m_new = ADAM_B1 * m_ref[...] + (1.0 - ADAM_B1) * g
        v_new = ADAM_B2 * v_ref[...] + (1.0 - ADAM_B2) * (g * g)
        m_hat = m_new / (1.0 - ADAM_B1 ** ADAM_STEP)
        v_hat = v_new / (1.0 - ADAM_B2 ** ADAM_STEP)
        g_out[...] = g
        d_out[...] = -ADAM_LR * (m_hat / (jnp.sqrt(v_hat) + ADAM_EPS) + ADAM_WD * w_ref[...])
        m_out[...] = m_new
        v_out[...] = v_new

    blk = pl.BlockSpec((tr, C), lambda i: (i, 0))
    sh = jax.ShapeDtypeStruct((R, C), F32)
    return pl.pallas_call(
        body, name=name, grid=(R // tr,),
        in_specs=[pl.BlockSpec((S, tr, C), lambda i: (0, i, 0)), blk, blk, blk],
        out_specs=[blk, blk, blk, blk], out_shape=[sh, sh, sh, sh],
        compiler_params=_params("parallel"),
    )(gs, w, m, v)


PACK_ROWS = 8


def _pack(arrs, dtype, lead, row_mult):
    lead_shape = arrs[0].shape[:lead]
    parts, meta, off = [], [], 0
    for a in arrs:
        size = math.prod(a.shape[lead:])
        rows = -(-size // LANES)
        x = a.astype(dtype)
        if size % LANES:
            x = jnp.concatenate([x.reshape(lead_shape + (size,)),
                                 jnp.zeros(lead_shape + (rows * LANES - size,), dtype)], -1)
        x = x.reshape(lead_shape + (rows, LANES))
        padded = -(-rows // PACK_ROWS) * PACK_ROWS
        if padded != rows:
            x = jnp.concatenate([x, jnp.zeros(lead_shape + (padded - rows, LANES), dtype)], lead)
        parts.append(x)
        meta.append((off, size, a.shape[lead:]))
        off += padded
    total = -(-off // row_mult) * row_mult
    if total != off:
        parts.append(jnp.zeros(lead_shape + (total - off, LANES), dtype))
    return jnp.concatenate(parts, lead), meta


def _unpack(packed, meta, lead):
    lead_shape = packed.shape[:lead]
    out = []
    for off, size, shape in meta:
        rows = -(-size // LANES)
        x = lax.slice_in_dim(packed, off, off + rows, axis=lead)
        if size % LANES:
            x = x.reshape(lead_shape + (rows * LANES,))[..., :size]
        out.append(x.reshape(lead_shape + shape))
    return out


ARG_NAMES = ['x', 'mem', 'positions', 'norm_mix_g', 'norm_xa_g', 'norm_mem_g', 'xa_wq', 'xa_wkv', 'xa_wo', 'norm_ffn_g', 'ffn_w_up', 'ffn_conv_w', 'ffn_conv_b', 'ffn_w_down', 'pc_w_in', 'pool_w', 'pool_scale', 'conv_dw_w', 'conv_dw_b', 'conv_ln_g', 'conv_ln_b', 'pc_w_out', 'mla_w_dq_dkv', 'mla_q_norm_g', 'mla_w_uq', 'mla_kv_norm_g', 'mla_w_ukv', 'mla_w_o', 'final_norm_g', 'loss_target']
WEIGHTS = ARG_NAMES[3:29]
BIG = {'xa_wq': 1, 'xa_wkv': 2, 'xa_wo': 1, 'ffn_w_up': 2, 'ffn_w_down': 1, 'pc_w_in': 2, 'pc_w_out': 1,
       'mla_w_dq_dkv': 1, 'mla_w_uq': 2, 'mla_w_ukv': 2, 'mla_w_o': 1}
SMALL_SHARDED = {'ffn_conv_w': 2, 'conv_dw_w': 2, 'mla_q_norm_g': 1, 'mla_kv_norm_g': 1}
REPLICATED = [n for n in WEIGHTS if n not in BIG and n not in SMALL_SHARDED]


def _from_slots(g, axis):
    t = jnp.moveaxis(g, 0, axis)
    return t.reshape(t.shape[:axis] + (t.shape[axis] * t.shape[axis + 1],) + t.shape[axis + 2:])


def _to_slots(full, axis):
    n = full.shape[axis] // N_DEV
    t = full.reshape(full.shape[:axis] + (N_DEV, n) + full.shape[axis + 1:])
    return jnp.moveaxis(t, axis, 0)


def kernel(x, mem, positions, norm_mix_g, norm_xa_g, norm_mem_g, xa_wq, xa_wkv, xa_wo, norm_ffn_g, ffn_w_up, ffn_conv_w, ffn_conv_b, ffn_w_down, pc_w_in, pool_w, pool_scale, conv_dw_w, conv_dw_b, conv_ln_g, conv_ln_b, pc_w_out, mla_w_dq_dkv, mla_q_norm_g, mla_w_uq, mla_kv_norm_g, mla_w_ukv, mla_w_o, final_norm_g, loss_target, m_norm_mix_g, m_norm_xa_g, m_norm_mem_g, m_xa_wq, m_xa_wkv, m_xa_wo, m_norm_ffn_g, m_ffn_w_up, m_ffn_conv_w, m_ffn_conv_b, m_ffn_w_down, m_pc_w_in, m_pool_w, m_pool_scale, m_conv_dw_w, m_conv_dw_b, m_conv_ln_g, m_conv_ln_b, m_pc_w_out, m_mla_w_dq_dkv, m_mla_q_norm_g, m_mla_w_uq, m_mla_kv_norm_g, m_mla_w_ukv, m_mla_w_o, m_final_norm_g, v_norm_mix_g, v_norm_xa_g, v_norm_mem_g, v_xa_wq, v_xa_wkv, v_xa_wo, v_norm_ffn_g, v_ffn_w_up, v_ffn_conv_w, v_ffn_conv_b, v_ffn_w_down, v_pc_w_in, v_pool_w, v_pool_scale, v_conv_dw_w, v_conv_dw_b, v_conv_ln_g, v_conv_ln_b, v_pc_w_out, v_mla_w_dq_dkv, v_mla_q_norm_g, v_mla_w_uq, v_mla_kv_norm_g, v_mla_w_ukv, v_mla_w_o, v_final_norm_g):
    args = (x, mem, positions, norm_mix_g, norm_xa_g, norm_mem_g, xa_wq, xa_wkv, xa_wo, norm_ffn_g, ffn_w_up, ffn_conv_w, ffn_conv_b, ffn_w_down, pc_w_in, pool_w, pool_scale, conv_dw_w, conv_dw_b, conv_ln_g, conv_ln_b, pc_w_out, mla_w_dq_dkv, mla_q_norm_g, mla_w_uq, mla_kv_norm_g, mla_w_ukv, mla_w_o, final_norm_g, loss_target)
    a = dict(zip(ARG_NAMES, args))
    mom = dict(zip(WEIGHTS, (m_norm_mix_g, m_norm_xa_g, m_norm_mem_g, m_xa_wq, m_xa_wkv, m_xa_wo, m_norm_ffn_g, m_ffn_w_up, m_ffn_conv_w, m_ffn_conv_b, m_ffn_w_down, m_pc_w_in, m_pool_w, m_pool_scale, m_conv_dw_w, m_conv_dw_b, m_conv_ln_g, m_conv_ln_b, m_pc_w_out, m_mla_w_dq_dkv, m_mla_q_norm_g, m_mla_w_uq, m_mla_kv_norm_g, m_mla_w_ukv, m_mla_w_o, m_final_norm_g)))
    var = dict(zip(WEIGHTS, (v_norm_mix_g, v_norm_xa_g, v_norm_mem_g, v_xa_wq, v_xa_wkv, v_xa_wo, v_norm_ffn_g, v_ffn_w_up, v_ffn_conv_w, v_ffn_conv_b, v_ffn_w_down, v_pc_w_in, v_pool_w, v_pool_scale, v_conv_dw_w, v_conv_dw_b, v_conv_ln_g, v_conv_ln_b, v_pc_w_out, v_mla_w_dq_dkv, v_mla_q_norm_g, v_mla_w_uq, v_mla_kv_norm_g, v_mla_w_ukv, v_mla_w_o, v_final_norm_g)))
    me = 4 * lax.axis_index("x") + 2 * lax.axis_index("y") + lax.axis_index("c")

    big_all = _all_gather([a[n].astype(CD) for n in BIG], name="gather_weights")
    sm_pack, sm_meta = _pack([a[n] for n in SMALL_SHARDED], F32, 0, 8)
    sm_all = _unpack(_all_gather([sm_pack], name="gather_small")[0], sm_meta, 1)
    W = {n: a[n] for n in REPLICATED}
    for (n, ax), g in zip(BIG.items(), big_all):
        W[n] = _from_slots(g, ax)
    for (n, ax), g in zip(SMALL_SHARDED.items(), sm_all):
        W[n] = _from_slots(g, ax)

    loss, dx, G = _local_step(x[0], mem[0], positions[0], loss_target[0], W)

    parts = [_to_slots(G[n], ax).astype(CD) for n, ax in BIG.items()]
    from_sibling = _pair_exchange(parts, name="grads_to_sibling")
    core = lax.axis_index("c").astype(jnp.int32).reshape(1)
    sums = []
    for n, p, r in zip(BIG, parts, from_sibling):
        cols = p.shape[-1]
        s = _pair_sum(p.reshape(N_DEV, -1, cols), r.reshape(N_CHIP, -1, cols), core, name=f"pair_sum_{n}")
        sums.append(s.reshape((N_CHIP,) + p.shape[1:]))
    recv = _chip_exchange(sums, name="scatter_grads")
    out = {}
    for n, r in zip(BIG, recv):
        shape = a[n].shape
        rows = lambda t: t.reshape(-1, shape[-1])
        res = _adamw(r.reshape(N_CHIP, -1, shape[-1]), rows(a[n]), rows(mom[n]), rows(var[n]), name=f"adamw_{n}")
        out[n] = tuple(t.reshape(shape) for t in res)

    small_names = REPLICATED + list(SMALL_SHARDED)
    spack, smeta = _pack([G[n] for n in small_names] + [loss], F32, 0, 256)
    stot = _unpack(_sum_slots(_all_gather([spack], name="gather_small_grads")[0], name="sum_small_grads"), smeta, 0)
    loss_total = stot[-1][0, 0]
    gsm = dict(zip(small_names, stot[:-1]))
    for n, ax in SMALL_SHARDED.items():
        width = a[n].shape[ax]
        gsm[n] = lax.dynamic_slice_in_dim(gsm[n], me * width, width, ax)
    g1, meta1 = _pack([gsm[n] for n in small_names], F32, 0, 256)
    w1, _ = _pack([a[n] for n in small_names], F32, 0, 256)
    m1, _ = _pack([mom[n] for n in small_names], F32, 0, 256)
    v1, _ = _pack([var[n] for n in small_names], F32, 0, 256)
    res = [_unpack(r, meta1, 0) for r in _adamw(g1[None], w1, m1, v1, name="adamw_small")]
    for i, n in enumerate(small_names):
        out[n] = tuple(r[i] for r in res)

    return (loss_total, dx[None],
            *[out[n][0] for n in WEIGHTS], *[out[n][1] for n in WEIGHTS],
            *[out[n][2] for n in WEIGHTS], *[out[n][3] for n in WEIGHTS])
```
